```python
import jax, jax.numpy as jnp
from jax import lax
import numpy as np

D_MODEL = 1024
BATCH = 32
SEQ = 2048
DEPTH = 1

MIX_WIDTH = D_MODEL
LRU_WIDTH = MIX_WIDTH // 2
LRU_HEADS = 8
LRU_HEAD_DIM = LRU_WIDTH // LRU_HEADS
LRU_CONV_K = 4
LRU_C = 8.0
CONV_WIDTH = MIX_WIDTH - LRU_WIDTH
CONV_HEADS = 8
CONV_HEAD_DIM = CONV_WIDTH // CONV_HEADS
CONV_K = 31
D_FF = ((8 * D_MODEL // 3 + 255) // 256) * 256
FFN_CONV_K = 3
ALPHA = (2 * DEPTH) ** 0.25
BETA = (8 * DEPTH) ** -0.25
LN_EPS = 1e-5

kernel_name = "hybrid_rglru_conformer_convglu_deepnorm_block"


def layer_norm(x, g, b):
    xf = x.astype(jnp.float32)
    mu = jnp.mean(xf, axis=-1, keepdims=True)
    var = jnp.mean(jnp.square(xf - mu), axis=-1, keepdims=True)
    y = (xf - mu) * lax.rsqrt(var + LN_EPS)
    return (y * g.astype(jnp.float32) + b.astype(jnp.float32)).astype(x.dtype)


def head_layer_norm(x, n_heads, g, b):
    bsz, seq, ch = x.shape
    xf = x.astype(jnp.float32).reshape(bsz, seq, n_heads, ch // n_heads)
    mu = jnp.mean(xf, axis=-1, keepdims=True)
    var = jnp.mean(jnp.square(xf - mu), axis=-1, keepdims=True)
    y = ((xf - mu) * lax.rsqrt(var + LN_EPS)).reshape(bsz, seq, ch)
    return (y * g.astype(jnp.float32) + b.astype(jnp.float32)).astype(x.dtype)


def causal_dwconv(x, w, b):
    k, ch = w.shape
    y = lax.conv_general_dilated(
        x, w[:, None, :].astype(x.dtype), window_strides=(1,),
        padding=[(k - 1, 0)], dimension_numbers=("NWC", "WIO", "NWC"),
        feature_group_count=ch)
    return y + b.astype(x.dtype)


def rg_lru(x, w_r, b_r, w_i, b_i, lam):
    bsz, seq, ch = x.shape
    xf = x.astype(jnp.float32)
    xh = xf.reshape(bsz, seq, LRU_HEADS, LRU_HEAD_DIM)
    gr = jnp.einsum("bshi,hij->bshj", xh, w_r.astype(jnp.float32)).reshape(bsz, seq, ch)
    gi = jnp.einsum("bshi,hij->bshj", xh, w_i.astype(jnp.float32)).reshape(bsz, seq, ch)
    r = jax.nn.sigmoid(gr + b_r.astype(jnp.float32))
    i = jax.nn.sigmoid(gi + b_i.astype(jnp.float32))
    log_a = -LRU_C * r * jax.nn.softplus(-lam.astype(jnp.float32))
    a = jnp.exp(log_a)
    mult = jnp.sqrt(-jnp.expm1(2.0 * log_a))
    u = mult * (i * xf)

    def combine(left, right):
        a_l, h_l = left
        a_r, h_r = right
        return a_l * a_r, a_r * h_l + h_r

    _, h = lax.associative_scan(combine, (a, u), axis=1)
    return h.astype(x.dtype)


def hybrid_mixer(u, w_in, lru_conv_w, lru_conv_b, lru_w_r, lru_b_r, lru_w_i, lru_b_i,
                 lru_lambda, conv_w, conv_b, conv_norm_g, conv_norm_b, w_out):
    proj = jnp.einsum("bsd,de->bse", u, w_in)
    xa, ga, vb, gb = jnp.split(
        proj, [LRU_WIDTH, 2 * LRU_WIDTH, 2 * LRU_WIDTH + CONV_WIDTH], axis=-1)
    xa = causal_dwconv(xa, lru_conv_w, lru_conv_b)
    ha = rg_lru(xa, lru_w_r, lru_b_r, lru_w_i, lru_b_i, lru_lambda)
    ya = jax.nn.gelu(ga) * ha
    vb = vb * jax.nn.sigmoid(gb)
    vb = causal_dwconv(vb, conv_w, conv_b)
    yb = jax.nn.silu(head_layer_norm(vb, CONV_HEADS, conv_norm_g, conv_norm_b))
    y = jnp.concatenate([ya, yb], axis=-1)
    return jnp.einsum("bse,ed->bsd", y, w_out)


def conv_glu_ffn(u, w_up, conv_w, conv_b, w_down):
    h = jnp.einsum("bsd,df->bsf", u, w_up)
    v, g = jnp.split(h, 2, axis=-1)
    g = causal_dwconv(g, conv_w, conv_b)
    return jnp.einsum("bsf,fd->bsd", jax.nn.silu(g) * v, w_down)


def _fwd_setup_inputs(seed: int = 0) -> dict:
    key = jax.random.key(seed)
    ks = jax.random.split(key, 26)
    L, D = DEPTH, D_MODEL

    def nrm(k, shape, scale):
        return jax.random.normal(k, shape, jnp.float32) * scale

    a_pow = jax.random.uniform(ks[11], (L, LRU_WIDTH), jnp.float32, 0.9, 0.999)
    a_base = a_pow ** (1.0 / LRU_C)
    lru_lambda = jnp.log(a_base) - jnp.log1p(-a_base)
    return {
        "x": nrm(ks[0], (BATCH, SEQ, D), 1.0),
        "c": nrm(ks[1], (BATCH, D), 1.0),
        "w_ada": nrm(ks[2], (L, D, 6 * D), 0.1 * D ** -0.5),
        "b_ada": nrm(ks[3], (L, 6 * D), 0.01),
        "w_in": nrm(ks[4], (L, D, 2 * LRU_WIDTH + 2 * CONV_WIDTH), D ** -0.5),
        "lru_conv_w": nrm(ks[5], (L, LRU_CONV_K, LRU_WIDTH), LRU_CONV_K ** -0.5),
        "lru_conv_b": nrm(ks[6], (L, LRU_WIDTH), 0.01),
        "lru_w_r": nrm(ks[7], (L, LRU_HEADS, LRU_HEAD_DIM, LRU_HEAD_DIM), LRU_HEAD_DIM ** -0.5),
        "lru_b_r": nrm(ks[8], (L, LRU_WIDTH), 0.01),
        "lru_w_i": nrm(ks[9], (L, LRU_HEADS, LRU_HEAD_DIM, LRU_HEAD_DIM), LRU_HEAD_DIM ** -0.5),
        "lru_b_i": nrm(ks[10], (L, LRU_WIDTH), 0.01),
        "lru_lambda": lru_lambda,
        "conv_w": nrm(ks[12], (L, CONV_K, CONV_WIDTH), CONV_K ** -0.5),
        "conv_b": nrm(ks[13], (L, CONV_WIDTH), 0.01),
        "conv_norm_g": 1.0 + nrm(ks[14], (L, CONV_WIDTH), 0.01),
        "conv_norm_b": nrm(ks[15], (L, CONV_WIDTH), 0.01),
        "w_out": nrm(ks[16], (L, MIX_WIDTH, D), BETA * MIX_WIDTH ** -0.5),
        "ln1_g": 1.0 + nrm(ks[17], (L, D), 0.01),
        "ln1_b": nrm(ks[18], (L, D), 0.01),
        "ffn_w_up": nrm(ks[19], (L, D, 2 * D_FF), D ** -0.5),
        "ffn_conv_w": nrm(ks[20], (L, FFN_CONV_K, D_FF), FFN_CONV_K ** -0.5),
        "ffn_conv_b": nrm(ks[21], (L, D_FF), 0.01),
        "ffn_w_down": nrm(ks[22], (L, D_FF, D), BETA * D_FF ** -0.5),
        "ln2_g": 1.0 + nrm(ks[23], (L, D), 0.01),
        "ln2_b": nrm(ks[24], (L, D), 0.01),
    }


def _fwd_reference(x, c, w_ada, b_ada, w_in, lru_conv_w, lru_conv_b, lru_w_r, lru_b_r,
              lru_w_i, lru_b_i, lru_lambda, conv_w, conv_b, conv_norm_g, conv_norm_b,
              w_out, ln1_g, ln1_b, ffn_w_up, ffn_conv_w, ffn_conv_b, ffn_w_down,
              ln2_g, ln2_b):
    c_act = jax.nn.silu(c)
    for l in range(DEPTH):
        mod = jnp.einsum("bd,de->be", c_act, w_ada[l]) + b_ada[l]
        sh1, sc1, gt1, sh2, sc2, gt2 = [m[:, None, :] for m in jnp.split(mod, 6, axis=-1)]
        u = x * (1.0 + sc1) + sh1
        y = hybrid_mixer(u, w_in[l], lru_conv_w[l], lru_conv_b[l], lru_w_r[l], lru_b_r[l],
                         lru_w_i[l], lru_b_i[l], lru_lambda[l], conv_w[l], conv_b[l],
                         conv_norm_g[l], conv_norm_b[l], w_out[l])
        x = layer_norm(ALPHA * x + (1.0 + gt1) * y, ln1_g[l], ln1_b[l])
        u = x * (1.0 + sc2) + sh2
        y = conv_glu_ffn(u, ffn_w_up[l], ffn_conv_w[l], ffn_conv_b[l], ffn_w_down[l])
        x = layer_norm(ALPHA * x + (1.0 + gt2) * y, ln2_g[l], ln2_b[l])
    return x


import jax as _jax
import jax.numpy as _jnp

TWIN_FORMAT = 'train_step'
FWD_PARAMS = ['x', 'c', 'w_ada', 'b_ada', 'w_in', 'lru_conv_w', 'lru_conv_b', 'lru_w_r', 'lru_b_r', 'lru_w_i', 'lru_b_i', 'lru_lambda', 'conv_w', 'conv_b', 'conv_norm_g', 'conv_norm_b', 'w_out', 'ln1_g', 'ln1_b', 'ffn_w_up', 'ffn_conv_w', 'ffn_conv_b', 'ffn_w_down', 'ln2_g', 'ln2_b']
TWIN_WEIGHTS = ['w_ada', 'b_ada', 'w_in', 'lru_conv_w', 'lru_conv_b', 'lru_w_r', 'lru_b_r', 'lru_w_i', 'lru_b_i', 'lru_lambda', 'conv_w', 'conv_b', 'conv_norm_g', 'conv_norm_b', 'w_out', 'ln1_g', 'ln1_b', 'ffn_w_up', 'ffn_conv_w', 'ffn_conv_b', 'ffn_w_down', 'ln2_g', 'ln2_b']
TWIN_DIFF_INPUT = 'x'
TWIN_INPUTS = ['x', 'c', 'w_ada', 'b_ada', 'w_in', 'lru_conv_w', 'lru_conv_b', 'lru_w_r', 'lru_b_r', 'lru_w_i', 'lru_b_i', 'lru_lambda', 'conv_w', 'conv_b', 'conv_norm_g', 'conv_norm_b', 'w_out', 'ln1_g', 'ln1_b', 'ffn_w_up', 'ffn_conv_w', 'ffn_conv_b', 'ffn_w_down', 'ln2_g', 'ln2_b', 'loss_target', 'm_w_ada', 'm_b_ada', 'm_w_in', 'm_lru_conv_w', 'm_lru_conv_b', 'm_lru_w_r', 'm_lru_b_r', 'm_lru_w_i', 'm_lru_b_i', 'm_lru_lambda', 'm_conv_w', 'm_conv_b', 'm_conv_norm_g', 'm_conv_norm_b', 'm_w_out', 'm_ln1_g', 'm_ln1_b', 'm_ffn_w_up', 'm_ffn_conv_w', 'm_ffn_conv_b', 'm_ffn_w_down', 'm_ln2_g', 'm_ln2_b', 'v_w_ada', 'v_b_ada', 'v_w_in', 'v_lru_conv_w', 'v_lru_conv_b', 'v_lru_w_r', 'v_lru_b_r', 'v_lru_w_i', 'v_lru_b_i', 'v_lru_lambda', 'v_conv_w', 'v_conv_b', 'v_conv_norm_g', 'v_conv_norm_b', 'v_w_out', 'v_ln1_g', 'v_ln1_b', 'v_ffn_w_up', 'v_ffn_conv_w', 'v_ffn_conv_b', 'v_ffn_w_down', 'v_ln2_g', 'v_ln2_b']
TWIN_OUTPUTS = ['loss', 'grad_x', 'grad_w_ada', 'grad_b_ada', 'grad_w_in', 'grad_lru_conv_w', 'grad_lru_conv_b', 'grad_lru_w_r', 'grad_lru_b_r', 'grad_lru_w_i', 'grad_lru_b_i', 'grad_lru_lambda', 'grad_conv_w', 'grad_conv_b', 'grad_conv_norm_g', 'grad_conv_norm_b', 'grad_w_out', 'grad_ln1_g', 'grad_ln1_b', 'grad_ffn_w_up', 'grad_ffn_conv_w', 'grad_ffn_conv_b', 'grad_ffn_w_down', 'grad_ln2_g', 'grad_ln2_b', 'delta_w_ada', 'delta_b_ada', 'delta_w_in', 'delta_lru_conv_w', 'delta_lru_conv_b', 'delta_lru_w_r', 'delta_lru_b_r', 'delta_lru_w_i', 'delta_lru_b_i', 'delta_lru_lambda', 'delta_conv_w', 'delta_conv_b', 'delta_conv_norm_g', 'delta_conv_norm_b', 'delta_w_out', 'delta_ln1_g', 'delta_ln1_b', 'delta_ffn_w_up', 'delta_ffn_conv_w', 'delta_ffn_conv_b', 'delta_ffn_w_down', 'delta_ln2_g', 'delta_ln2_b', 'new_m_w_ada', 'new_m_b_ada', 'new_m_w_in', 'new_m_lru_conv_w', 'new_m_lru_conv_b', 'new_m_lru_w_r', 'new_m_lru_b_r', 'new_m_lru_w_i', 'new_m_lru_b_i', 'new_m_lru_lambda', 'new_m_conv_w', 'new_m_conv_b', 'new_m_conv_norm_g', 'new_m_conv_norm_b', 'new_m_w_out', 'new_m_ln1_g', 'new_m_ln1_b', 'new_m_ffn_w_up', 'new_m_ffn_conv_w', 'new_m_ffn_conv_b', 'new_m_ffn_w_down', 'new_m_ln2_g', 'new_m_ln2_b', 'new_v_w_ada', 'new_v_b_ada', 'new_v_w_in', 'new_v_lru_conv_w', 'new_v_lru_conv_b', 'new_v_lru_w_r', 'new_v_lru_b_r', 'new_v_lru_w_i', 'new_v_lru_b_i', 'new_v_lru_lambda', 'new_v_conv_w', 'new_v_conv_b', 'new_v_conv_norm_g', 'new_v_conv_norm_b', 'new_v_w_out', 'new_v_ln1_g', 'new_v_ln1_b', 'new_v_ffn_w_up', 'new_v_ffn_conv_w', 'new_v_ffn_conv_b', 'new_v_ffn_w_down', 'new_v_ln2_g', 'new_v_ln2_b']
TWIN_LEAF_KINDS = {'loss': 'loss', 'grad_x': 'grad_x', 'grad_w_ada': 'grad_w', 'grad_b_ada': 'grad_w', 'grad_w_in': 'grad_w', 'grad_lru_conv_w': 'grad_w', 'grad_lru_conv_b': 'grad_w', 'grad_lru_w_r': 'grad_w', 'grad_lru_b_r': 'grad_w', 'grad_lru_w_i': 'grad_w', 'grad_lru_b_i': 'grad_w', 'grad_lru_lambda': 'grad_w', 'grad_conv_w': 'grad_w', 'grad_conv_b': 'grad_w', 'grad_conv_norm_g': 'grad_w', 'grad_conv_norm_b': 'grad_w', 'grad_w_out': 'grad_w', 'grad_ln1_g': 'grad_w', 'grad_ln1_b': 'grad_w', 'grad_ffn_w_up': 'grad_w', 'grad_ffn_conv_w': 'grad_w', 'grad_ffn_conv_b': 'grad_w', 'grad_ffn_w_down': 'grad_w', 'grad_ln2_g': 'grad_w', 'grad_ln2_b': 'grad_w', 'delta_w_ada': 'delta_w', 'delta_b_ada': 'delta_w', 'delta_w_in': 'delta_w', 'delta_lru_conv_w': 'delta_w', 'delta_lru_conv_b': 'delta_w', 'delta_lru_w_r': 'delta_w', 'delta_lru_b_r': 'delta_w', 'delta_lru_w_i': 'delta_w', 'delta_lru_b_i': 'delta_w', 'delta_lru_lambda': 'delta_w', 'delta_conv_w': 'delta_w', 'delta_conv_b': 'delta_w', 'delta_conv_norm_g': 'delta_w', 'delta_conv_norm_b': 'delta_w', 'delta_w_out': 'delta_w', 'delta_ln1_g': 'delta_w', 'delta_ln1_b': 'delta_w', 'delta_ffn_w_up': 'delta_w', 'delta_ffn_conv_w': 'delta_w', 'delta_ffn_conv_b': 'delta_w', 'delta_ffn_w_down': 'delta_w', 'delta_ln2_g': 'delta_w', 'delta_ln2_b': 'delta_w', 'new_m_w_ada': 'new_m', 'new_m_b_ada': 'new_m', 'new_m_w_in': 'new_m', 'new_m_lru_conv_w': 'new_m', 'new_m_lru_conv_b': 'new_m', 'new_m_lru_w_r': 'new_m', 'new_m_lru_b_r': 'new_m', 'new_m_lru_w_i': 'new_m', 'new_m_lru_b_i': 'new_m', 'new_m_lru_lambda': 'new_m', 'new_m_conv_w': 'new_m', 'new_m_conv_b': 'new_m', 'new_m_conv_norm_g': 'new_m', 'new_m_conv_norm_b': 'new_m', 'new_m_w_out': 'new_m', 'new_m_ln1_g': 'new_m', 'new_m_ln1_b': 'new_m', 'new_m_ffn_w_up': 'new_m', 'new_m_ffn_conv_w': 'new_m', 'new_m_ffn_conv_b': 'new_m', 'new_m_ffn_w_down': 'new_m', 'new_m_ln2_g': 'new_m', 'new_m_ln2_b': 'new_m', 'new_v_w_ada': 'new_v', 'new_v_b_ada': 'new_v', 'new_v_w_in': 'new_v', 'new_v_lru_conv_w': 'new_v', 'new_v_lru_conv_b': 'new_v', 'new_v_lru_w_r': 'new_v', 'new_v_lru_b_r': 'new_v', 'new_v_lru_w_i': 'new_v', 'new_v_lru_b_i': 'new_v', 'new_v_lru_lambda': 'new_v', 'new_v_conv_w': 'new_v', 'new_v_conv_b': 'new_v', 'new_v_conv_norm_g': 'new_v', 'new_v_conv_norm_b': 'new_v', 'new_v_w_out': 'new_v', 'new_v_ln1_g': 'new_v', 'new_v_ln1_b': 'new_v', 'new_v_ffn_w_up': 'new_v', 'new_v_ffn_conv_w': 'new_v', 'new_v_ffn_conv_b': 'new_v', 'new_v_ffn_w_down': 'new_v', 'new_v_ln2_g': 'new_v', 'new_v_ln2_b': 'new_v'}


def _forward(args):
    return _fwd_reference(*[args[k] for k in FWD_PARAMS])


def _output_shape():
    out = _jax.eval_shape(lambda: _forward(_fwd_setup_inputs(0)))
    return out.shape, out.dtype

N_MICROBATCH = 1
ADAM_LR = 0.001
ADAM_B1 = 0.9
ADAM_B2 = 0.999
ADAM_EPS = 1e-08
ADAM_WD = 0.01
ADAM_STEP = 10
PER_EXAMPLE_BATCH_AXIS = {'x': 0, 'c': 0, 'loss_target': 0}
SHARED_INPUTS = []
_WEIGHT_DTYPES = {'w_ada': _jnp.float32, 'b_ada': _jnp.float32, 'w_in': _jnp.float32, 'lru_conv_w': _jnp.float32, 'lru_conv_b': _jnp.float32, 'lru_w_r': _jnp.float32, 'lru_b_r': _jnp.float32, 'lru_w_i': _jnp.float32, 'lru_b_i': _jnp.float32, 'lru_lambda': _jnp.float32, 'conv_w': _jnp.float32, 'conv_b': _jnp.float32, 'conv_norm_g': _jnp.float32, 'conv_norm_b': _jnp.float32, 'w_out': _jnp.float32, 'ln1_g': _jnp.float32, 'ln1_b': _jnp.float32, 'ffn_w_up': _jnp.float32, 'ffn_conv_w': _jnp.float32, 'ffn_conv_b': _jnp.float32, 'ffn_w_down': _jnp.float32, 'ln2_g': _jnp.float32, 'ln2_b': _jnp.float32}
MOMENT_SCALE = {'w_ada': 1.496365e-01, 'b_ada': 2.308531e-01, 'w_in': 6.924285e-02, 'lru_conv_w': 8.678901e-02, 'lru_conv_b': 7.921437e-01, 'lru_w_r': 2.694608e-02, 'lru_b_r': 3.059444e-02, 'lru_w_i': 4.978445e-02, 'lru_b_i': 2.806240e-02, 'lru_lambda': 5.506505e-02, 'conv_w': 8.010879e-02, 'conv_b': 1.737862e-01, 'conv_norm_g': 9.406516e-02, 'conv_norm_b': 1.001419e-01, 'w_out': 1.388137e-01, 'ln1_g': 6.946662e-01, 'ln1_b': 3.736298e-01, 'ffn_w_up': 4.529591e-02, 'ffn_conv_w': 4.683146e-02, 'ffn_conv_b': 4.370144e-02, 'ffn_w_down': 1.243102e-01, 'ln2_g': 6.396809e+01, 'ln2_b': 5.174804e+00}


def _to_microbatches(a, axis):
    t = _jnp.moveaxis(a, axis, 0)
    t = t.reshape((N_MICROBATCH, t.shape[0] // N_MICROBATCH) + t.shape[1:])
    return _jnp.moveaxis(t, 1, axis + 1)


def setup_inputs(seed: int = 0) -> dict:
    inp = _fwd_setup_inputs(seed)
    key = _jax.random.fold_in(_jax.random.key(seed), 7919)
    shape, _ = _output_shape()
    out = dict(inp)
    out["loss_target"] = _jax.random.normal(_jax.random.fold_in(key, 0), shape, _jnp.float32)
    for i, name in enumerate(TWIN_WEIGHTS):
        w = inp[name].astype(_jnp.float32)
        if MOMENT_SCALE is None:
            s = _jnp.sqrt(_jnp.mean(_jnp.square(w)) + 1e-30)
        else:
            s = MOMENT_SCALE[name]
        km, kv = _jax.random.split(_jax.random.fold_in(key, i + 1))
        out[name] = w
        out["m_" + name] = s * _jax.random.normal(km, w.shape, _jnp.float32)
        out["v_" + name] = (s * s) * _jax.random.uniform(kv, w.shape, _jnp.float32, 0.5, 1.5)
    if N_MICROBATCH > 1:
        for name, axis in PER_EXAMPLE_BATCH_AXIS.items():
            out[name] = _to_microbatches(out[name], axis)
    return {'x': out['x'], 'c': out['c'], 'w_ada': out['w_ada'], 'b_ada': out['b_ada'], 'w_in': out['w_in'], 'lru_conv_w': out['lru_conv_w'], 'lru_conv_b': out['lru_conv_b'], 'lru_w_r': out['lru_w_r'], 'lru_b_r': out['lru_b_r'], 'lru_w_i': out['lru_w_i'], 'lru_b_i': out['lru_b_i'], 'lru_lambda': out['lru_lambda'], 'conv_w': out['conv_w'], 'conv_b': out['conv_b'], 'conv_norm_g': out['conv_norm_g'], 'conv_norm_b': out['conv_norm_b'], 'w_out': out['w_out'], 'ln1_g': out['ln1_g'], 'ln1_b': out['ln1_b'], 'ffn_w_up': out['ffn_w_up'], 'ffn_conv_w': out['ffn_conv_w'], 'ffn_conv_b': out['ffn_conv_b'], 'ffn_w_down': out['ffn_w_down'], 'ln2_g': out['ln2_g'], 'ln2_b': out['ln2_b'], 'loss_target': out['loss_target'], 'm_w_ada': out['m_w_ada'], 'm_b_ada': out['m_b_ada'], 'm_w_in': out['m_w_in'], 'm_lru_conv_w': out['m_lru_conv_w'], 'm_lru_conv_b': out['m_lru_conv_b'], 'm_lru_w_r': out['m_lru_w_r'], 'm_lru_b_r': out['m_lru_b_r'], 'm_lru_w_i': out['m_lru_w_i'], 'm_lru_b_i': out['m_lru_b_i'], 'm_lru_lambda': out['m_lru_lambda'], 'm_conv_w': out['m_conv_w'], 'm_conv_b': out['m_conv_b'], 'm_conv_norm_g': out['m_conv_norm_g'], 'm_conv_norm_b': out['m_conv_norm_b'], 'm_w_out': out['m_w_out'], 'm_ln1_g': out['m_ln1_g'], 'm_ln1_b': out['m_ln1_b'], 'm_ffn_w_up': out['m_ffn_w_up'], 'm_ffn_conv_w': out['m_ffn_conv_w'], 'm_ffn_conv_b': out['m_ffn_conv_b'], 'm_ffn_w_down': out['m_ffn_w_down'], 'm_ln2_g': out['m_ln2_g'], 'm_ln2_b': out['m_ln2_b'], 'v_w_ada': out['v_w_ada'], 'v_b_ada': out['v_b_ada'], 'v_w_in': out['v_w_in'], 'v_lru_conv_w': out['v_lru_conv_w'], 'v_lru_conv_b': out['v_lru_conv_b'], 'v_lru_w_r': out['v_lru_w_r'], 'v_lru_b_r': out['v_lru_b_r'], 'v_lru_w_i': out['v_lru_w_i'], 'v_lru_b_i': out['v_lru_b_i'], 'v_lru_lambda': out['v_lru_lambda'], 'v_conv_w': out['v_conv_w'], 'v_conv_b': out['v_conv_b'], 'v_conv_norm_g': out['v_conv_norm_g'], 'v_conv_norm_b': out['v_conv_norm_b'], 'v_w_out': out['v_w_out'], 'v_ln1_g': out['v_ln1_g'], 'v_ln1_b': out['v_ln1_b'], 'v_ffn_w_up': out['v_ffn_w_up'], 'v_ffn_conv_w': out['v_ffn_conv_w'], 'v_ffn_conv_b': out['v_ffn_conv_b'], 'v_ffn_w_down': out['v_ffn_w_down'], 'v_ln2_g': out['v_ln2_g'], 'v_ln2_b': out['v_ln2_b']}


def _loss(weights, diff, rest, loss_target):
    with _jax.named_scope("forward"):
        args = {**rest, TWIN_DIFF_INPUT: diff, **{k: w.astype(_WEIGHT_DTYPES[k]) for k, w in weights.items()}}
        y = _forward(args)
    with _jax.named_scope("loss_head"):
        err = _jnp.square(y.astype(_jnp.float32) - loss_target)
        return 0.5 * _jnp.sum(_jnp.mean(err, axis=-1)) if err.ndim else 0.5 * err


def _adamw(w, g, m, v):
    m = ADAM_B1 * m + (1.0 - ADAM_B1) * g
    v = ADAM_B2 * v + (1.0 - ADAM_B2) * _jnp.square(g)
    m_hat = m / (1.0 - ADAM_B1 ** ADAM_STEP)
    v_hat = v / (1.0 - ADAM_B2 ** ADAM_STEP)
    delta = -ADAM_LR * (m_hat / (_jnp.sqrt(v_hat) + ADAM_EPS) + ADAM_WD * w)
    return delta, m, v


def reference(x, c, w_ada, b_ada, w_in, lru_conv_w, lru_conv_b, lru_w_r, lru_b_r, lru_w_i, lru_b_i, lru_lambda, conv_w, conv_b, conv_norm_g, conv_norm_b, w_out, ln1_g, ln1_b, ffn_w_up, ffn_conv_w, ffn_conv_b, ffn_w_down, ln2_g, ln2_b, loss_target, m_w_ada, m_b_ada, m_w_in, m_lru_conv_w, m_lru_conv_b, m_lru_w_r, m_lru_b_r, m_lru_w_i, m_lru_b_i, m_lru_lambda, m_conv_w, m_conv_b, m_conv_norm_g, m_conv_norm_b, m_w_out, m_ln1_g, m_ln1_b, m_ffn_w_up, m_ffn_conv_w, m_ffn_conv_b, m_ffn_w_down, m_ln2_g, m_ln2_b, v_w_ada, v_b_ada, v_w_in, v_lru_conv_w, v_lru_conv_b, v_lru_w_r, v_lru_b_r, v_lru_w_i, v_lru_b_i, v_lru_lambda, v_conv_w, v_conv_b, v_conv_norm_g, v_conv_norm_b, v_w_out, v_ln1_g, v_ln1_b, v_ffn_w_up, v_ffn_conv_w, v_ffn_conv_b, v_ffn_w_down, v_ln2_g, v_ln2_b):
    given = dict(x=x, c=c, w_ada=w_ada, b_ada=b_ada, w_in=w_in, lru_conv_w=lru_conv_w, lru_conv_b=lru_conv_b, lru_w_r=lru_w_r, lru_b_r=lru_b_r, lru_w_i=lru_w_i, lru_b_i=lru_b_i, lru_lambda=lru_lambda, conv_w=conv_w, conv_b=conv_b, conv_norm_g=conv_norm_g, conv_norm_b=conv_norm_b, w_out=w_out, ln1_g=ln1_g, ln1_b=ln1_b, ffn_w_up=ffn_w_up, ffn_conv_w=ffn_conv_w, ffn_conv_b=ffn_conv_b, ffn_w_down=ffn_w_down, ln2_g=ln2_g, ln2_b=ln2_b, loss_target=loss_target, m_w_ada=m_w_ada, m_b_ada=m_b_ada, m_w_in=m_w_in, m_lru_conv_w=m_lru_conv_w, m_lru_conv_b=m_lru_conv_b, m_lru_w_r=m_lru_w_r, m_lru_b_r=m_lru_b_r, m_lru_w_i=m_lru_w_i, m_lru_b_i=m_lru_b_i, m_lru_lambda=m_lru_lambda, m_conv_w=m_conv_w, m_conv_b=m_conv_b, m_conv_norm_g=m_conv_norm_g, m_conv_norm_b=m_conv_norm_b, m_w_out=m_w_out, m_ln1_g=m_ln1_g, m_ln1_b=m_ln1_b, m_ffn_w_up=m_ffn_w_up, m_ffn_conv_w=m_ffn_conv_w, m_ffn_conv_b=m_ffn_conv_b, m_ffn_w_down=m_ffn_w_down, m_ln2_g=m_ln2_g, m_ln2_b=m_ln2_b, v_w_ada=v_w_ada, v_b_ada=v_b_ada, v_w_in=v_w_in, v_lru_conv_w=v_lru_conv_w, v_lru_conv_b=v_lru_conv_b, v_lru_w_r=v_lru_w_r, v_lru_b_r=v_lru_b_r, v_lru_w_i=v_lru_w_i, v_lru_b_i=v_lru_b_i, v_lru_lambda=v_lru_lambda, v_conv_w=v_conv_w, v_conv_b=v_conv_b, v_conv_norm_g=v_conv_norm_g, v_conv_norm_b=v_conv_norm_b, v_w_out=v_w_out, v_ln1_g=v_ln1_g, v_ln1_b=v_ln1_b, v_ffn_w_up=v_ffn_w_up, v_ffn_conv_w=v_ffn_conv_w, v_ffn_conv_b=v_ffn_conv_b, v_ffn_w_down=v_ffn_w_down, v_ln2_g=v_ln2_g, v_ln2_b=v_ln2_b)
    weights = {n: given[n] for n in TWIN_WEIGHTS}
    shared = {n: given[n] for n in SHARED_INPUTS}
    per_example = {n: given[n] for n in ['x', 'c']}
    grad_fn = _jax.value_and_grad(_loss, argnums=(0, 1))

    def one_microbatch(ex, loss_target):
        ex = dict(ex)
        diff = ex.pop(TWIN_DIFF_INPUT)
        return grad_fn(weights, diff, {**shared, **ex}, loss_target)

    if N_MICROBATCH == 1:
        loss, (grad_w, grad_x) = one_microbatch(per_example, given["loss_target"])
    else:
        def body(carry, xs):
            loss_sum, grad_sum = carry
            l_k, (gw_k, gx_k) = one_microbatch(xs[0], xs[1])
            with _jax.named_scope("update"):
                return (loss_sum + l_k, _jax.tree.map(_jnp.add, grad_sum, gw_k)), gx_k

        init = (_jnp.zeros((), _jnp.float32), _jax.tree.map(_jnp.zeros_like, weights))
        (loss, grad_w), grad_x = _jax.lax.scan(body, init, (per_example, given["loss_target"]))
    with _jax.named_scope("update"):
        delta_w, new_m, new_v = {}, {}, {}
        for n in TWIN_WEIGHTS:
            delta_w[n], new_m[n], new_v[n] = _adamw(weights[n], grad_w[n], given["m_" + n], given["v_" + n])
    return (loss, grad_x, *[grad_w[n] for n in TWIN_WEIGHTS], *[delta_w[n] for n in TWIN_WEIGHTS],
            *[new_m[n] for n in TWIN_WEIGHTS], *[new_v[n] for n in TWIN_WEIGHTS])
```

```python
import functools
import math

import jax
import jax.numpy as jnp
from jax import lax
from jax.experimental import pallas as pl
from jax.experimental.pallas import tpu as pltpu

NDEV = 8
D = 1024
LW = 512
CW = 512
HEAD = 64
DFF = 2816
FF_CHUNK = 1408
LRU_K = 4
CONV_K = 31
FFN_K = 3
LRU_C = 8.0
ALPHA = (2 * 1) ** 0.25
LN_EPS = 1e-5
ADAM_LR = 0.001
ADAM_B1 = 0.9
ADAM_B2 = 0.999
ADAM_EPS = 1e-08
ADAM_WD = 0.01
ADAM_STEP = 10

MXU_DTYPE = jnp.bfloat16
TOK_TILE = 256
WG_TOK_TILE = 512
VMEM_LIMIT = 60 * 1024 * 1024
HALO31 = 32
HALO = 8

F32 = jnp.float32
MESH = pl.DeviceIdType.MESH


def _sigmoid(x):
    return 1.0 / (1.0 + jnp.exp(-x))


def _dot(a, b):
    return jnp.dot(a, b, preferred_element_type=F32)


def _dot_nt(a, b):
    return lax.dot_general(a, b, (((1,), (1,)), ((), ())), preferred_element_type=F32)


def _dot_tn(a, b):
    return lax.dot_general(a, b, (((0,), (0,)), ((), ())), preferred_element_type=F32)


def _colsum(v):
    return jnp.sum(v, axis=0, keepdims=True)


def _rowmean(v):
    return jnp.mean(v, axis=-1, keepdims=True)


def _head_mean(v, mavg):
    hi = v.astype(MXU_DTYPE)
    lo = (v - hi.astype(F32)).astype(MXU_DTYPE)
    return _dot(hi, mavg) + _dot(lo, mavg)


_GELU_C0 = math.sqrt(2.0 / math.pi)
_GELU_C1 = 0.044715


def _gelu_and_grad(x):
    x2 = x * x
    th = jnp.tanh(_GELU_C0 * (x + _GELU_C1 * x * x2))
    ge = 0.5 * x * (1.0 + th)
    dge = 0.5 * (1.0 + th) + 0.5 * x * (1.0 - th * th) * (_GELU_C0 * (1.0 + 3.0 * _GELU_C1 * x2))
    return ge, dge


def _softplus(x):
    return jnp.maximum(x, 0.0) + jnp.log1p(jnp.exp(-jnp.abs(x)))


def _layer_norm_stats(r):
    mu = _rowmean(r)
    dl = r - mu
    var = _rowmean(dl * dl)
    rstd = lax.rsqrt(var + LN_EPS)
    return dl * rstd, rstd


def _scan_fwd(a, u, rows):
    n = a.shape[0]
    d = 1
    while d < n:
        keep = rows >= d
        a_s = jnp.where(keep, pltpu.roll(a, d, 0), 1.0)
        u_s = jnp.where(keep, pltpu.roll(u, d, 0), 0.0)
        u = a * u_s + u
        a = a * a_s
        d *= 2
    return u, a


def _scan_bwd(bm, g, rows):
    n = bm.shape[0]
    d = 1
    while d < n:
        keep = rows < n - d
        b_s = jnp.where(keep, pltpu.roll(bm, n - d, 0), 1.0)
        g_s = jnp.where(keep, pltpu.roll(g, n - d, 0), 0.0)
        g = g + bm * g_s
        bm = bm * b_s
        d *= 2
    return g, bm


def _lru_gates(xc, wr, wi, va_ref):
    xcb = xc.astype(MXU_DTYPE)
    r = _sigmoid(_dot(xcb, wr) + va_ref[5:6, :])
    ig = _sigmoid(_dot(xcb, wi) + va_ref[6:7, :])
    sp = _softplus(-va_ref[7:8, :])
    la = (-LRU_C) * r * sp
    a = jnp.exp(la)
    a2 = a * a
    mult = jnp.sqrt(-jnp.tanh(la) * (a2 + 1.0))
    return xcb, r, ig, sp, a, a2, mult


def _full(shape):
    nd = len(shape)
    return pl.BlockSpec(shape, lambda *_: (0,) * nd)


_ANY = pl.BlockSpec(memory_space=pl.ANY)


def _params(n_grid):
    return pltpu.CompilerParams(dimension_semantics=("arbitrary",) * n_grid, vmem_limit_bytes=VMEM_LIMIT)


def _my_place():
    return lax.axis_index("x"), lax.axis_index("y"), lax.axis_index("c")


def _all_gather(arrays, name):
    n_arr = len(arrays)

    def body(*refs):
        x_refs = refs[:n_arr]
        out_refs = refs[n_arr:2 * n_arr]
        send_sems, recv_sems, local_sems = refs[2 * n_arr:]
        x, y, c = _my_place()
        me, sibling = (x, y, c), (x, y, 1 - c)
        chips = [(1 - x, y), (x, 1 - y), (1 - x, 1 - y)]

        def rows(k, px, py, pc):
            m = x_refs[k].shape[0]
            return out_refs[k].at[pl.ds((4 * px + 2 * py + pc) * m, m), :]

        def copy(k, s, block, to, src=None):
            return pltpu.make_async_remote_copy(
                src_ref=rows(k, *block) if src is None else src, dst_ref=rows(k, *block),
                send_sem=send_sems.at[k, s], recv_sem=recv_sems.at[k, s], device_id=to, device_id_type=MESH)

        mine = [pltpu.make_async_copy(x_refs[k], rows(k, *me), local_sems.at[k]) for k in range(n_arr)]
        for cp in mine:
            cp.start()
        first = []
        for k in range(n_arr):
            first.append(copy(k, 0, me, sibling, src=x_refs[k]))
            first += [copy(k, 1 + j, me, (*chip, c), src=x_refs[k]) for j, chip in enumerate(chips)]
        for cp in first:
            cp.start()
        passed = []
        for k in range(n_arr):
            for j, chip in enumerate(chips):
                copy(k, 1 + j, (*chip, c), me).wait_recv()
                fwd = copy(k, 4 + j, (*chip, c), sibling)
                fwd.start()
                passed.append(fwd)
        for k in range(n_arr):
            copy(k, 0, sibling, me).wait_recv()
            for j, chip in enumerate(chips):
                copy(k, 4 + j, (*chip, 1 - c), me).wait_recv()
        for cp in first + passed:
            cp.wait_send()
        for cp in mine:
            cp.wait()

    return pl.pallas_call(
        body, name=name,
        out_shape=[jax.ShapeDtypeStruct((NDEV * a.shape[0], a.shape[1]), a.dtype) for a in arrays],
        in_specs=[_ANY] * n_arr, out_specs=[_ANY] * n_arr,
        scratch_shapes=[pltpu.SemaphoreType.DMA((n_arr, 7)), pltpu.SemaphoreType.DMA((n_arr, 7)),
                        pltpu.SemaphoreType.DMA((n_arr,))],
    )(*arrays)


def _exchange_blocks(arrays, name):
    n_arr = len(arrays)

    def body(*refs):
        g_refs = refs[:n_arr]
        out_refs = refs[n_arr:2 * n_arr]
        send_sems, recv_sems, local_sems = refs[2 * n_arr:]
        x, y, c = _my_place()
        me = 4 * x + 2 * y + c

        def copy(k, rel):
            px, py, pc = x ^ ((rel >> 2) & 1), y ^ ((rel >> 1) & 1), c ^ (rel & 1)
            m = out_refs[k].shape[1]
            return pltpu.make_async_remote_copy(
                src_ref=g_refs[k].at[pl.ds((4 * px + 2 * py + pc) * m, m), :], dst_ref=out_refs[k].at[me],
                send_sem=send_sems.at[k, rel - 1], recv_sem=recv_sems.at[k, rel - 1],
                device_id=(px, py, pc), device_id_type=MESH)

        mine = []
        for k in range(n_arr):
            m = out_refs[k].shape[1]
            mine.append(pltpu.make_async_copy(g_refs[k].at[pl.ds(me * m, m), :], out_refs[k].at[me], local_sems.at[k]))
        for cp in mine:
            cp.start()
        copies = [copy(k, rel) for k in range(n_arr) for rel in range(1, NDEV)]
        for cp in copies:
            cp.start()
        for cp in copies:
            cp.wait()
        for cp in mine:
            cp.wait()

    return pl.pallas_call(
        body, name=name,
        out_shape=[jax.ShapeDtypeStruct((NDEV, a.shape[0] // NDEV, a.shape[1]), a.dtype) for a in arrays],
        in_specs=[_ANY] * n_arr, out_specs=[_ANY] * n_arr,
        scratch_shapes=[pltpu.SemaphoreType.DMA((n_arr, 7)), pltpu.SemaphoreType.DMA((n_arr, 7)),
                        pltpu.SemaphoreType.DMA((n_arr,))],
    )(*arrays)


def _ada_fwd(c_all, w_ada_loc, b_ada_loc):
    def body(c_ref, w_ref, b_ref, o_ref):
        cv = c_ref[...]
        ca = (cv * _sigmoid(cv)).astype(MXU_DTYPE)
        o_ref[...] = _dot(ca, w_ref[...].astype(MXU_DTYPE)) + b_ref[...]

    return pl.pallas_call(
        body, name="ada_fwd", out_shape=jax.ShapeDtypeStruct((c_all.shape[0], w_ada_loc.shape[1]), F32),
        in_specs=[_full(c_all.shape), _full(w_ada_loc.shape), _full(b_ada_loc.shape)],
        out_specs=_full((c_all.shape[0], w_ada_loc.shape[1])),
        compiler_params=pltpu.CompilerParams(vmem_limit_bytes=VMEM_LIMIT),
    )(c_all, w_ada_loc, b_ada_loc)


def _ada_bwd(c_all, dmod_all, dmod_cols):
    def body(c_ref, da_ref, d_ref, o_ref, b_ref):
        cv = c_ref[...]
        ca = (cv * _sigmoid(cv)).astype(MXU_DTYPE)
        o_ref[...] = _dot_tn(ca, d_ref[...].astype(MXU_DTYPE))
        b_ref[...] = _colsum(da_ref[...])

    return pl.pallas_call(
        body, name="ada_bwd",
        out_shape=[jax.ShapeDtypeStruct((c_all.shape[1], dmod_cols.shape[1]), F32),
                   jax.ShapeDtypeStruct((1, dmod_all.shape[1]), F32)],
        in_specs=[_full(c_all.shape), _full(dmod_all.shape), _full(dmod_cols.shape)],
        out_specs=[_full((c_all.shape[1], dmod_cols.shape[1])), _full((1, dmod_all.shape[1]))],
        compiler_params=pltpu.CompilerParams(vmem_limit_bytes=VMEM_LIMIT),
    )(c_all, dmod_all, dmod_cols)


def _mixer_fwd(x, mods, win_t, wout, wr, wi, mavg, va, vb, ln1):
    bl, seq, _ = x.shape
    t = min(TOK_TILE, seq)
    nt = seq // t

    def body(x_ref, mod_ref, win_hbm, wout_hbm, wr_ref, wi_ref, mavg_ref, va_ref, vb_ref, ln1_ref,
             proj_ref, h_ref, vc_ref, y_ref, mixed_ref, x1_ref, u1_ref,
             win_v, wout_v, xa_ext, vg_ext, hcar, sems):
        b, i = pl.program_id(0), pl.program_id(1)

        @pl.when((b == 0) & (i == 0))
        def _():
            cps = [pltpu.make_async_copy(win_hbm, win_v, sems.at[0]), pltpu.make_async_copy(wout_hbm, wout_v, sems.at[1])]
            for cp in cps:
                cp.start()
            for cp in cps:
                cp.wait()

        @pl.when(i == 0)
        def _():
            xa_ext[0:HALO, :] = jnp.zeros((HALO, LW), F32)
            vg_ext[0:HALO31, :] = jnp.zeros((HALO31, CW), F32)
            hcar[...] = jnp.zeros_like(hcar)

        rows = lax.broadcasted_iota(jnp.int32, (t, LW), 0)
        xv = x_ref[...]
        sh, sc, gt = mod_ref[0:1, :], mod_ref[1:2, :], mod_ref[2:3, :]
        ub = (xv * (1.0 + sc) + sh).astype(MXU_DTYPE)
        u1_ref[...] = ub
        proj = _dot_nt(ub, win_v[...])
        proj_ref[...] = proj
        xa, ga = proj[:, 0:LW], proj[:, LW:2 * LW]
        vbr, gb = proj[:, 2 * LW:2 * LW + CW], proj[:, 2 * LW + CW:]

        xa_ext[HALO:HALO + t, :] = xa
        xc = va_ref[4:5, :]
        for k in range(LRU_K):
            xc = xc + va_ref[k:k + 1, :] * xa_ext[pl.ds(HALO - (LRU_K - 1) + k, t), :]
        xa_ext[0:HALO, :] = xa_ext[t:t + HALO, :]
        _, _, ig, _, a, _, mult = _lru_gates(xc, wr_ref[...], wi_ref[...], va_ref)
        h_loc, a_cum = _scan_fwd(a, mult * (ig * xc), rows)
        h = h_loc + a_cum * hcar[0:1, :]
        hcar[0:1, :] = h[t - 1:t, :]
        h_ref[...] = h
        ge, _ = _gelu_and_grad(ga)
        ya = ge * h

        vg_ext[HALO31:HALO31 + t, :] = vbr * _sigmoid(gb)
        vc = vb_ref[CONV_K:CONV_K + 1, :]
        for k in range(CONV_K):
            vc = vc + vb_ref[k:k + 1, :] * vg_ext[pl.ds(HALO31 - (CONV_K - 1) + k, t), :]
        vg_ext[0:HALO31, :] = vg_ext[t:t + HALO31, :]
        vc_ref[...] = vc
        mavg_v = mavg_ref[...]
        dl = vc - _head_mean(vc, mavg_v)
        rs = lax.rsqrt(_head_mean(dl * dl, mavg_v) + LN_EPS)
        yl = dl * rs * vb_ref[CONV_K + 1:CONV_K + 2, :] + vb_ref[CONV_K + 2:CONV_K + 3, :]
        yb = yl * _sigmoid(yl)

        yv = jnp.concatenate([ya, yb], axis=1).astype(MXU_DTYPE)
        y_ref[...] = yv
        mixed = _dot(yv, wout_v[...])
        mixed_ref[...] = mixed
        xh, _ = _layer_norm_stats(ALPHA * xv + (1.0 + gt) * mixed)
        x1_ref[...] = xh * ln1_ref[0:1, :] + ln1_ref[1:2, :]

    tok = lambda w: pl.BlockSpec((None, t, w), lambda b, i: (b, i, 0))
    outs = [(2 * LW + 2 * CW, F32), (LW, F32), (CW, F32), (D, MXU_DTYPE), (D, F32), (D, F32), (D, MXU_DTYPE)]
    return pl.pallas_call(
        body, name="mixer_fwd", grid=(bl, nt),
        out_shape=[jax.ShapeDtypeStruct((bl, seq, w), dt) for w, dt in outs],
        in_specs=[tok(D), pl.BlockSpec((None, 8, D), lambda b, i: (b, 0, 0)), _ANY, _ANY,
                  _full(wr.shape), _full(wi.shape), _full(mavg.shape), _full(va.shape), _full(vb.shape), _full(ln1.shape)],
        out_specs=[tok(w) for w, _ in outs],
        scratch_shapes=[pltpu.VMEM(win_t.shape, MXU_DTYPE), pltpu.VMEM(wout.shape, MXU_DTYPE),
                        pltpu.VMEM((t + HALO, LW), F32), pltpu.VMEM((t + HALO31, CW), F32), pltpu.VMEM((8, LW), F32),
                        pltpu.SemaphoreType.DMA((2,))],
        compiler_params=_params(2),
    )(x, mods, win_t, wout, wr, wi, mavg, va, vb, ln1)


def _ffn_fwd(x1, mods, tgt, wup_t, wdown, f3, ln2):
    bl, seq, _ = x1.shape
    t = min(TOK_TILE, seq)
    nt = seq // t
    n_chunk = DFF // FF_CHUNK

    def body(x1_ref, mod_ref, tgt_ref, wup_hbm, wdown_hbm, f3_ref, ln2_ref,
             hh_ref, z_ref, u2_ref, y2_ref, loss_ref,
             wup_v, wdown_v, g_ext, sems):
        b, i = pl.program_id(0), pl.program_id(1)

        @pl.when((b == 0) & (i == 0))
        def _():
            cps = [pltpu.make_async_copy(wup_hbm, wup_v, sems.at[0]), pltpu.make_async_copy(wdown_hbm, wdown_v, sems.at[1])]
            for cp in cps:
                cp.start()
            for cp in cps:
                cp.wait()
            loss_ref[...] = jnp.zeros_like(loss_ref)

        @pl.when(i == 0)
        def _():
            for ch in range(n_chunk):
                g_ext[ch, 0:HALO, :] = jnp.zeros((HALO, FF_CHUNK), F32)

        x1v = x1_ref[...]
        sh, sc, gt = mod_ref[3:4, :], mod_ref[4:5, :], mod_ref[5:6, :]
        ub = (x1v * (1.0 + sc) + sh).astype(MXU_DTYPE)
        u2_ref[...] = ub
        y2 = jnp.zeros((t, D), F32)
        for ch in range(n_chunk):
            lo = ch * FF_CHUNK
            v = _dot_nt(ub, wup_v[lo:lo + FF_CHUNK, :])
            g = _dot_nt(ub, wup_v[DFF + lo:DFF + lo + FF_CHUNK, :])
            hh_ref[:, lo:lo + FF_CHUNK] = v
            hh_ref[:, DFF + lo:DFF + lo + FF_CHUNK] = g
            g_ext[ch, HALO:HALO + t, :] = g
            gc = f3_ref[FFN_K:FFN_K + 1, lo:lo + FF_CHUNK]
            for k in range(FFN_K):
                gc = gc + f3_ref[k:k + 1, lo:lo + FF_CHUNK] * g_ext[ch, pl.ds(HALO - (FFN_K - 1) + k, t), :]
            g_ext[ch, 0:HALO, :] = g_ext[ch, t:t + HALO, :]
            zb = (gc * _sigmoid(gc) * v).astype(MXU_DTYPE)
            z_ref[:, lo:lo + FF_CHUNK] = zb
            y2 = y2 + _dot(zb, wdown_v[lo:lo + FF_CHUNK, :])
        y2_ref[...] = y2
        xh, _ = _layer_norm_stats(ALPHA * x1v + (1.0 + gt) * y2)
        err = xh * ln2_ref[0:1, :] + ln2_ref[1:2, :] - tgt_ref[...]
        e2 = (err * err).reshape(t // 8, 8, D).sum(axis=0)
        part = e2[:, 0:128]
        for j in range(1, D // 128):
            part = part + e2[:, 128 * j:128 * (j + 1)]
        loss_ref[...] += part

    tok = lambda w: pl.BlockSpec((None, t, w), lambda b, i: (b, i, 0))
    outs = [(2 * DFF, F32), (DFF, MXU_DTYPE), (D, MXU_DTYPE), (D, F32)]
    return pl.pallas_call(
        body, name="ffn_fwd", grid=(bl, nt),
        out_shape=[jax.ShapeDtypeStruct((bl, seq, w), dt) for w, dt in outs] + [jax.ShapeDtypeStruct((8, 128), F32)],
        in_specs=[tok(D), pl.BlockSpec((None, 8, D), lambda b, i: (b, 0, 0)), tok(D), _ANY, _ANY,
                  _full(f3.shape), _full(ln2.shape)],
        out_specs=[tok(w) for w, _ in outs] + [_full((8, 128))],
        scratch_shapes=[pltpu.VMEM(wup_t.shape, MXU_DTYPE), pltpu.VMEM(wdown.shape, MXU_DTYPE),
                        pltpu.VMEM((n_chunk, t + HALO, FF_CHUNK), F32), pltpu.SemaphoreType.DMA((2,))],
        compiler_params=_params(2),
    )(x1, mods, tgt, wup_t, wdown, f3, ln2)


def _ffn_bwd(x1, y2, tgt, hh, mods, wup_t, wdown, f3, ln2):
    bl, seq, _ = x1.shape
    t = min(TOK_TILE, seq)
    nt = seq // t
    n_chunk = DFF // FF_CHUNK

    def body(x1_ref, y2_ref, tgt_ref, hh_ref, halo_ref, mod_ref, wup_hbm, wdown_hbm, f3_ref, ln2_ref,
             dx1_ref, dy2_ref, dhh_ref, dln2_ref, df3_ref, dmod_ref,
             wup_v, wdown_v, g_ext, dgc_ext, sems):
        b, i = pl.program_id(0), pl.program_id(1)
        tt = nt - 1 - i

        @pl.when((b == 0) & (i == 0))
        def _():
            cps = [pltpu.make_async_copy(wup_hbm, wup_v, sems.at[0]), pltpu.make_async_copy(wdown_hbm, wdown_v, sems.at[1])]
            for cp in cps:
                cp.start()
            for cp in cps:
                cp.wait()
            dln2_ref[...] = jnp.zeros_like(dln2_ref)
            df3_ref[...] = jnp.zeros_like(df3_ref)

        @pl.when(i == 0)
        def _():
            dmod_ref[...] = jnp.zeros_like(dmod_ref)
            for ch in range(n_chunk):
                dgc_ext[ch, t:t + HALO, :] = jnp.zeros((HALO, FF_CHUNK), F32)

        x1v, y2v = x1_ref[...], y2_ref[...]
        sc, gt = mod_ref[4:5, :], mod_ref[5:6, :]
        xh, rstd = _layer_norm_stats(ALPHA * x1v + (1.0 + gt) * y2v)
        g2 = ln2_ref[0:1, :]
        dx2 = (xh * g2 + ln2_ref[1:2, :] - tgt_ref[...]) * (1.0 / D)
        dln2_ref[0:1, :] += _colsum(dx2 * xh)
        dln2_ref[1:2, :] += _colsum(dx2)
        dxh = dx2 * g2
        dr2 = rstd * (dxh - _rowmean(dxh) - xh * _rowmean(dxh * xh))
        dmod_ref[2:3, :] += _colsum(dr2 * y2v)
        dyb = ((1.0 + gt) * dr2).astype(MXU_DTYPE)
        dy2_ref[...] = dyb

        halo_keep = (tt > 0).astype(F32)
        du2 = jnp.zeros((t, D), F32)
        for ch in range(n_chunk):
            lo = ch * FF_CHUNK
            dz = _dot_nt(dyb, wdown_v[lo:lo + FF_CHUNK, :])
            v = hh_ref[:, lo:lo + FF_CHUNK]
            g_ext[ch, 0:HALO, :] = halo_ref[:, lo:lo + FF_CHUNK] * halo_keep
            g_ext[ch, HALO:HALO + t, :] = hh_ref[:, DFF + lo:DFF + lo + FF_CHUNK]
            gc = f3_ref[FFN_K:FFN_K + 1, lo:lo + FF_CHUNK]
            for k in range(FFN_K):
                gc = gc + f3_ref[k:k + 1, lo:lo + FF_CHUNK] * g_ext[ch, pl.ds(HALO - (FFN_K - 1) + k, t), :]
            s = _sigmoid(gc)
            dv = dz * (gc * s)
            dgc = dz * v * (s * (1.0 + gc * (1.0 - s)))
            df3_ref[FFN_K:FFN_K + 1, lo:lo + FF_CHUNK] += _colsum(dgc)
            for k in range(FFN_K):
                df3_ref[k:k + 1, lo:lo + FF_CHUNK] += _colsum(dgc * g_ext[ch, pl.ds(HALO - (FFN_K - 1) + k, t), :])
            dgc_ext[ch, 0:t, :] = dgc
            dg = jnp.zeros((t, FF_CHUNK), F32)
            for k in range(FFN_K):
                dg = dg + f3_ref[k:k + 1, lo:lo + FF_CHUNK] * dgc_ext[ch, pl.ds(FFN_K - 1 - k, t), :]
            dgc_ext[ch, t:t + HALO, :] = dgc[0:HALO, :]
            dvb, dgb = dv.astype(MXU_DTYPE), dg.astype(MXU_DTYPE)
            dhh_ref[:, lo:lo + FF_CHUNK] = dvb
            dhh_ref[:, DFF + lo:DFF + lo + FF_CHUNK] = dgb
            du2 = du2 + _dot(dvb, wup_v[lo:lo + FF_CHUNK, :]) + _dot(dgb, wup_v[DFF + lo:DFF + lo + FF_CHUNK, :])
        dx1_ref[...] = ALPHA * dr2 + du2 * (1.0 + sc)
        dmod_ref[1:2, :] += _colsum(du2 * x1v)
        dmod_ref[0:1, :] += _colsum(du2)

    rev = lambda w: pl.BlockSpec((None, t, w), lambda b, i: (b, nt - 1 - i, 0))
    halo = pl.BlockSpec((None, HALO, DFF), lambda b, i: (b, jnp.maximum((nt - 1 - i) * (t // HALO) - 1, 0), 1))
    return pl.pallas_call(
        body, name="ffn_bwd", grid=(bl, nt),
        out_shape=[jax.ShapeDtypeStruct((bl, seq, D), F32), jax.ShapeDtypeStruct((bl, seq, D), MXU_DTYPE),
                   jax.ShapeDtypeStruct((bl, seq, 2 * DFF), MXU_DTYPE), jax.ShapeDtypeStruct((8, D), F32),
                   jax.ShapeDtypeStruct((8, DFF), F32), jax.ShapeDtypeStruct((bl, 8, D), F32)],
        in_specs=[rev(D), rev(D), rev(D), rev(2 * DFF), halo, pl.BlockSpec((None, 8, D), lambda b, i: (b, 0, 0)),
                  _ANY, _ANY, _full(f3.shape), _full(ln2.shape)],
        out_specs=[rev(D), rev(D), rev(2 * DFF), _full((8, D)), _full((8, DFF)),
                   pl.BlockSpec((None, 8, D), lambda b, i: (b, 0, 0))],
        scratch_shapes=[pltpu.VMEM(wup_t.shape, MXU_DTYPE), pltpu.VMEM(wdown.shape, MXU_DTYPE),
                        pltpu.VMEM((n_chunk, t + HALO, FF_CHUNK), F32), pltpu.VMEM((n_chunk, t + HALO, FF_CHUNK), F32),
                        pltpu.SemaphoreType.DMA((2,))],
        compiler_params=_params(2),
    )(x1, y2, tgt, hh, hh, mods, wup_t, wdown, f3, ln2)


def _mixer_bwd(x, dx1, proj, h, vc, mixed, mods, win_t, wout, wr, wi, mavg, va, vb, ln1):
    bl, seq, _ = x.shape
    t = min(TOK_TILE, seq)
    nt = seq // t
    pw = 2 * LW + 2 * CW

    def body(x_ref, dx1_ref, proj_ref, phalo_ref, h_ref, hhalo_ref, vc_ref, mixed_ref, mod_ref,
             win_hbm, wout_hbm, wr_ref, wi_ref, mavg_ref, va_ref, vb_ref, ln1_ref,
             gx_ref, dproj_ref, dm_ref, ga_ref, gw31_ref, dwr_ref, dwi_ref, dln1_ref, dmod_ref,
             win_v, wout_v, xa_ext, vg_ext, dvc_ext, dxc_ext, car, sems):
        b, i = pl.program_id(0), pl.program_id(1)
        tt = nt - 1 - i

        @pl.when((b == 0) & (i == 0))
        def _():
            cps = [pltpu.make_async_copy(win_hbm, win_v, sems.at[0]), pltpu.make_async_copy(wout_hbm, wout_v, sems.at[1])]
            for cp in cps:
                cp.start()
            for cp in cps:
                cp.wait()
            for ref in (ga_ref, gw31_ref, dwr_ref, dwi_ref, dln1_ref):
                ref[...] = jnp.zeros_like(ref)

        @pl.when(i == 0)
        def _():
            dmod_ref[...] = jnp.zeros_like(dmod_ref)
            dvc_ext[t:t + HALO31, :] = jnp.zeros((HALO31, CW), F32)
            dxc_ext[t:t + HALO, :] = jnp.zeros((HALO, LW), F32)
            car[...] = jnp.zeros_like(car)

        rows = lax.broadcasted_iota(jnp.int32, (t, LW), 0)
        halo_keep = (tt > 0).astype(F32)
        xv, mixed = x_ref[...], mixed_ref[...]
        sc, gt = mod_ref[1:2, :], mod_ref[2:3, :]

        xh, rstd = _layer_norm_stats(ALPHA * xv + (1.0 + gt) * mixed)
        dx1 = dx1_ref[...]
        dln1_ref[0:1, :] += _colsum(dx1 * xh)
        dln1_ref[1:2, :] += _colsum(dx1)
        dxh = dx1 * ln1_ref[0:1, :]
        dr1 = rstd * (dxh - _rowmean(dxh) - xh * _rowmean(dxh * xh))
        dmod_ref[2:3, :] += _colsum(dr1 * mixed)
        dmb = ((1.0 + gt) * dr1).astype(MXU_DTYPE)
        dm_ref[...] = dmb
        dy = _dot_nt(dmb, wout_v[...])
        dya, dyb = dy[:, 0:LW], dy[:, LW:]

        proj = proj_ref[...]
        xa, ga = proj[:, 0:LW], proj[:, LW:2 * LW]
        vbr, gb = proj[:, 2 * LW:2 * LW + CW], proj[:, 2 * LW + CW:]
        phalo = phalo_ref[...] * halo_keep

        sgb = _sigmoid(gb)
        vg_ext[0:HALO31, :] = phalo[:, 2 * LW:2 * LW + CW] * _sigmoid(phalo[:, 2 * LW + CW:])
        vg_ext[HALO31:HALO31 + t, :] = vbr * sgb
        mavg_v = mavg_ref[...]
        vcv = vc_ref[...]
        dl = vcv - _head_mean(vcv, mavg_v)
        rs = lax.rsqrt(_head_mean(dl * dl, mavg_v) + LN_EPS)
        yn = dl * rs
        ng = vb_ref[CONV_K + 1:CONV_K + 2, :]
        yl = yn * ng + vb_ref[CONV_K + 2:CONV_K + 3, :]
        s = _sigmoid(yl)
        dyl = dyb * (s * (1.0 + yl * (1.0 - s)))
        ga_ref[8:9, :] += _colsum(dyl * yn)
        ga_ref[9:10, :] += _colsum(dyl)
        dyn = dyl * ng
        dvc = rs * (dyn - _head_mean(dyn, mavg_v) - yn * _head_mean(dyn * yn, mavg_v))
        ga_ref[10:11, :] += _colsum(dvc)
        dvc_ext[0:t, :] = dvc
        dvg = jnp.zeros((t, CW), F32)
        for k in range(CONV_K):
            gw31_ref[k:k + 1, :] += _colsum(dvc * vg_ext[pl.ds(HALO31 - (CONV_K - 1) + k, t), :])
            dvg = dvg + vb_ref[k:k + 1, :] * dvc_ext[pl.ds(CONV_K - 1 - k, t), :]
        dvc_ext[t:t + HALO31, :] = dvc[0:HALO31, :]
        dvb = dvg * sgb
        dgb = dvg * vbr * sgb * (1.0 - sgb)

        xa_ext[0:HALO, :] = phalo[HALO31 - HALO:HALO31, 0:LW]
        xa_ext[HALO:HALO + t, :] = xa
        xc = va_ref[4:5, :]
        for k in range(LRU_K):
            xc = xc + va_ref[k:k + 1, :] * xa_ext[pl.ds(HALO - (LRU_K - 1) + k, t), :]
        wr_v, wi_v = wr_ref[...], wi_ref[...]
        xcb, r, ig, sp, a, a2, mult = _lru_gates(xc, wr_v, wi_v, va_ref)
        hv = h_ref[...]
        hprev = jnp.where(rows >= 1, pltpu.roll(hv, 1, 0), hhalo_ref[HALO - 1:HALO, :] * halo_keep)
        ge, dge = _gelu_and_grad(ga)
        dga = dya * hv * dge
        bm = jnp.where(rows < t - 1, pltpu.roll(a, t - 1, 0), car[0:1, :])
        g_loc, b_cum = _scan_bwd(bm, dya * ge, rows)
        gv = g_loc + b_cum * car[1:2, :]
        car[0:1, :] = a[0:1, :]
        car[1:2, :] = gv[0:1, :]
        da = gv * hprev
        dmult = gv * (ig * xc)
        di = gv * (mult * xc)
        dxc = gv * (mult * ig)
        dla = da * a - dmult * (a2 / mult)
        dr = dla * ((-LRU_C) * sp)
        lam = va_ref[7:8, :]
        ga_ref[7:8, :] += _colsum(dla * ((-LRU_C) * r)) * (-_sigmoid(-lam))
        dgr = dr * r * (1.0 - r)
        dgi = di * ig * (1.0 - ig)
        ga_ref[5:6, :] += _colsum(dgr)
        ga_ref[6:7, :] += _colsum(dgi)
        dgrb, dgib = dgr.astype(MXU_DTYPE), dgi.astype(MXU_DTYPE)
        dwr_ref[...] += _dot_tn(xcb, dgrb)
        dwi_ref[...] += _dot_tn(xcb, dgib)
        dxc = dxc + _dot_nt(dgrb, wr_v) + _dot_nt(dgib, wi_v)
        ga_ref[4:5, :] += _colsum(dxc)
        dxc_ext[0:t, :] = dxc
        dxa = jnp.zeros((t, LW), F32)
        for k in range(LRU_K):
            ga_ref[k:k + 1, :] += _colsum(dxc * xa_ext[pl.ds(HALO - (LRU_K - 1) + k, t), :])
            dxa = dxa + va_ref[k:k + 1, :] * dxc_ext[pl.ds(LRU_K - 1 - k, t), :]
        dxc_ext[t:t + HALO, :] = dxc[0:HALO, :]

        dpb = jnp.concatenate([dxa, dga, dvb, dgb], axis=1).astype(MXU_DTYPE)
        dproj_ref[...] = dpb
        du1 = _dot(dpb, win_v[...])
        gx_ref[...] = ALPHA * dr1 + du1 * (1.0 + sc)
        dmod_ref[1:2, :] += _colsum(du1 * xv)
        dmod_ref[0:1, :] += _colsum(du1)

    rev = lambda w: pl.BlockSpec((None, t, w), lambda b, i: (b, nt - 1 - i, 0))

    def halo(rows_, w):
        return pl.BlockSpec((None, rows_, w), lambda b, i: (b, jnp.maximum((nt - 1 - i) * (t // rows_) - 1, 0), 0))

    return pl.pallas_call(
        body, name="mixer_bwd", grid=(bl, nt),
        out_shape=[jax.ShapeDtypeStruct((bl, seq, D), F32), jax.ShapeDtypeStruct((bl, seq, pw), MXU_DTYPE),
                   jax.ShapeDtypeStruct((bl, seq, D), MXU_DTYPE), jax.ShapeDtypeStruct((16, LW), F32),
                   jax.ShapeDtypeStruct((32, CW), F32), jax.ShapeDtypeStruct((LW, LW), F32),
                   jax.ShapeDtypeStruct((LW, LW), F32), jax.ShapeDtypeStruct((8, D), F32),
                   jax.ShapeDtypeStruct((bl, 8, D), F32)],
        in_specs=[rev(D), rev(D), rev(pw), halo(HALO31, pw), rev(LW), halo(HALO, LW), rev(CW), rev(D),
                  pl.BlockSpec((None, 8, D), lambda b, i: (b, 0, 0)), _ANY, _ANY,
                  _full(wr.shape), _full(wi.shape), _full(mavg.shape), _full(va.shape), _full(vb.shape), _full(ln1.shape)],
        out_specs=[rev(D), rev(pw), rev(D), _full((16, LW)), _full((32, CW)), _full((LW, LW)), _full((LW, LW)),
                   _full((8, D)), pl.BlockSpec((None, 8, D), lambda b, i: (b, 0, 0))],
        scratch_shapes=[pltpu.VMEM(win_t.shape, MXU_DTYPE), pltpu.VMEM(wout.shape, MXU_DTYPE),
                        pltpu.VMEM((t + HALO, LW), F32), pltpu.VMEM((t + HALO31, CW), F32),
                        pltpu.VMEM((t + HALO31, CW), F32), pltpu.VMEM((t + HALO, LW), F32), pltpu.VMEM((8, LW), F32),
                        pltpu.SemaphoreType.DMA((2,))],
        compiler_params=_params(2),
    )(x, dx1, proj, proj, h, h, vc, mixed, mods, win_t, wout, wr, wi, mavg, va, vb, ln1)


def _weight_grad(a, bmat, tm, name):
    ntok, m = a.shape
    n = bmat.shape[1]
    tk = min(WG_TOK_TILE, ntok)
    nk = ntok // tk

    def body(a_ref, b_ref, o_ref):
        @pl.when(pl.program_id(1) == 0)
        def _():
            o_ref[...] = jnp.zeros_like(o_ref)

        o_ref[...] += _dot_tn(a_ref[...], b_ref[...])

    return pl.pallas_call(
        body, name=name, grid=(m // tm, nk), out_shape=jax.ShapeDtypeStruct((m, n), F32),
        in_specs=[pl.BlockSpec((tk, tm), lambda i, k: (k, i)), pl.BlockSpec((tk, n), lambda i, k: (k, 0))],
        out_specs=pl.BlockSpec((tm, n), lambda i, k: (i, 0)),
        compiler_params=_params(2),
    )(a, bmat)


def _sum_slots(slots, name):
    _, r, cdim = slots.shape
    tr = 32 if r % 32 == 0 else r

    def body(s_ref, o_ref):
        acc = s_ref[0]
        for j in range(1, NDEV):
            acc = acc + s_ref[j]
        o_ref[...] = acc

    return pl.pallas_call(
        body, name=name, grid=(r // tr,), out_shape=jax.ShapeDtypeStruct((r, cdim), F32),
        in_specs=[pl.BlockSpec((NDEV, tr, cdim), lambda i: (0, i, 0))],
        out_specs=pl.BlockSpec((tr, cdim), lambda i: (i, 0)),
        compiler_params=_params(1),
    )(slots)


def _adamw(w, g, m, v, name):
    r, cdim = w.shape
    tr = 128 if r % 128 == 0 else r

    def body(w_ref, g_ref, m_ref, v_ref, d_ref, nm_ref, nv_ref):
        gv = g_ref[...]
        nm = ADAM_B1 * m_ref[...] + (1.0 - ADAM_B1) * gv
        nv = ADAM_B2 * v_ref[...] + (1.0 - ADAM_B2) * (gv * gv)
        m_hat = nm / (1.0 - ADAM_B1 ** ADAM_STEP)
        v_hat = nv / (1.0 - ADAM_B2 ** ADAM_STEP)
        d_ref[...] = -ADAM_LR * (m_hat / (jnp.sqrt(v_hat) + ADAM_EPS) + ADAM_WD * w_ref[...])
        nm_ref[...] = nm
        nv_ref[...] = nv

    spec = pl.BlockSpec((tr, cdim), lambda i: (i, 0))
    return pl.pallas_call(
        body, name=name, grid=(r // tr,), out_shape=[jax.ShapeDtypeStruct((r, cdim), F32)] * 3,
        in_specs=[spec] * 4, out_specs=[spec] * 3, compiler_params=_params(1),
    )(w, g, m, v)


def _pack(arrs, width=D, row_mult=8):
    parts = []
    for a in arrs:
        flat = a.reshape(-1)
        pad = (-flat.shape[0]) % width
        parts.append(jnp.pad(flat, (0, pad)))
    flat = jnp.concatenate(parts)
    pad = (-flat.shape[0]) % (width * row_mult)
    return jnp.pad(flat, (0, pad)).reshape(-1, width)


def _unpack(buf, shapes, width=D):
    out, row = [], 0
    for shp in shapes:
        size = math.prod(shp)
        nrow = -(-size // width)
        out.append(buf[row:row + nrow].reshape(-1)[:size].reshape(shp))
        row += nrow
    return out


def _block_diag(w):
    hn, dh, _ = w.shape
    eye = jnp.eye(hn, dtype=w.dtype)
    return (w[:, :, None, :] * eye[:, None, :, None]).reshape(hn * dh, hn * dh)


def _diag_blocks(wfull, hn):
    dh = wfull.shape[0] // hn
    return jnp.stack([wfull[k * dh:(k + 1) * dh, k * dh:(k + 1) * dh] for k in range(hn)])


def _ungather_cols(g, k):
    n = g.shape[1]
    return g.reshape(NDEV, k, n).transpose(1, 0, 2).reshape(k, NDEV * n)


def _pad_rows(a, rows):
    return jnp.pad(a, ((0, rows - a.shape[0]), (0, 0)))


def kernel(x, c, w_ada, b_ada, w_in, lru_conv_w, lru_conv_b, lru_w_r, lru_b_r, lru_w_i, lru_b_i, lru_lambda, conv_w, conv_b, conv_norm_g, conv_norm_b, w_out, ln1_g, ln1_b, ffn_w_up, ffn_conv_w, ffn_conv_b, ffn_w_down, ln2_g, ln2_b, loss_target, m_w_ada, m_b_ada, m_w_in, m_lru_conv_w, m_lru_conv_b, m_lru_w_r, m_lru_b_r, m_lru_w_i, m_lru_b_i, m_lru_lambda, m_conv_w, m_conv_b, m_conv_norm_g, m_conv_norm_b, m_w_out, m_ln1_g, m_ln1_b, m_ffn_w_up, m_ffn_conv_w, m_ffn_conv_b, m_ffn_w_down, m_ln2_g, m_ln2_b, v_w_ada, v_b_ada, v_w_in, v_lru_conv_w, v_lru_conv_b, v_lru_w_r, v_lru_b_r, v_lru_w_i, v_lru_b_i, v_lru_lambda, v_conv_w, v_conv_b, v_conv_norm_g, v_conv_norm_b, v_w_out, v_ln1_g, v_ln1_b, v_ffn_w_up, v_ffn_conv_w, v_ffn_conv_b, v_ffn_w_down, v_ln2_g, v_ln2_b):
    weights = dict(w_ada=w_ada, b_ada=b_ada, w_in=w_in, lru_conv_w=lru_conv_w, lru_conv_b=lru_conv_b, lru_w_r=lru_w_r,
                   lru_b_r=lru_b_r, lru_w_i=lru_w_i, lru_b_i=lru_b_i, lru_lambda=lru_lambda, conv_w=conv_w, conv_b=conv_b,
                   conv_norm_g=conv_norm_g, conv_norm_b=conv_norm_b, w_out=w_out, ln1_g=ln1_g, ln1_b=ln1_b,
                   ffn_w_up=ffn_w_up, ffn_conv_w=ffn_conv_w, ffn_conv_b=ffn_conv_b, ffn_w_down=ffn_w_down, ln2_g=ln2_g,
                   ln2_b=ln2_b)
    mom_m = dict(w_ada=m_w_ada, b_ada=m_b_ada, w_in=m_w_in, lru_conv_w=m_lru_conv_w, lru_conv_b=m_lru_conv_b,
                 lru_w_r=m_lru_w_r, lru_b_r=m_lru_b_r, lru_w_i=m_lru_w_i, lru_b_i=m_lru_b_i, lru_lambda=m_lru_lambda,
                 conv_w=m_conv_w, conv_b=m_conv_b, conv_norm_g=m_conv_norm_g, conv_norm_b=m_conv_norm_b, w_out=m_w_out,
                 ln1_g=m_ln1_g, ln1_b=m_ln1_b, ffn_w_up=m_ffn_w_up, ffn_conv_w=m_ffn_conv_w, ffn_conv_b=m_ffn_conv_b,
                 ffn_w_down=m_ffn_w_down, ln2_g=m_ln2_g, ln2_b=m_ln2_b)
    mom_v = dict(w_ada=v_w_ada, b_ada=v_b_ada, w_in=v_w_in, lru_conv_w=v_lru_conv_w, lru_conv_b=v_lru_conv_b,
                 lru_w_r=v_lru_w_r, lru_b_r=v_lru_b_r, lru_w_i=v_lru_w_i, lru_b_i=v_lru_b_i, lru_lambda=v_lru_lambda,
                 conv_w=v_conv_w, conv_b=v_conv_b, conv_norm_g=v_conv_norm_g, conv_norm_b=v_conv_norm_b, w_out=v_w_out,
                 ln1_g=v_ln1_g, ln1_b=v_ln1_b, ffn_w_up=v_ffn_w_up, ffn_conv_w=v_ffn_conv_w, ffn_conv_b=v_ffn_conv_b,
                 ffn_w_down=v_ffn_w_down, ln2_g=v_ln2_g, ln2_b=v_ln2_b)
    names = list(weights)
    bl, seq, _ = x.shape
    ntok = bl * seq
    me = 4 * lax.axis_index("x") + 2 * lax.axis_index("y") + lax.axis_index("c")

    small_shapes = [(bl, D), (LRU_K, LW // NDEV), (CONV_K, CW // NDEV), (FFN_K, DFF // NDEV)]
    small = _pack([c, lru_conv_w[0], conv_w[0], ffn_conv_w[0]], width=128)
    n_small = small.shape[0]
    (small_all,) = _all_gather([small], "gather_small")
    small_all = small_all.reshape(NDEV, n_small, 128)
    per_dev = [_unpack(small_all[j], small_shapes, width=128) for j in range(NDEV)]
    c_all = jnp.concatenate([p[0] for p in per_dev], axis=0)
    lru_conv_w_f = jnp.concatenate([p[1] for p in per_dev], axis=1)
    conv_w_f = jnp.concatenate([p[2] for p in per_dev], axis=1)
    ffn_conv_w_f = jnp.concatenate([p[3] for p in per_dev], axis=1)

    ncol = w_ada.shape[2]
    b_ada_loc = lax.dynamic_slice(b_ada, (0, me * ncol), (1, ncol))
    mod_cols = _ada_fwd(c_all, w_ada[0], b_ada_loc)
    (mod_all,) = _all_gather([mod_cols], "gather_mod")
    mod_all = mod_all.reshape(NDEV, NDEV * bl, ncol)
    mod_mine = lax.dynamic_slice(mod_all, (0, me * bl, 0), (NDEV, bl, ncol))
    mods = mod_mine.transpose(1, 0, 2).reshape(bl, 6, D)
    mods = jnp.pad(mods, ((0, 0), (0, 2), (0, 0)))

    win_t, wout_b, wup_t, wdown_b = _all_gather(
        [w_in[0].T.astype(MXU_DTYPE), w_out[0].astype(MXU_DTYPE), ffn_w_up[0].T.astype(MXU_DTYPE),
         ffn_w_down[0].astype(MXU_DTYPE)], "gather_weights")

    wr_bd = _block_diag(lru_w_r[0]).astype(MXU_DTYPE)
    wi_bd = _block_diag(lru_w_i[0]).astype(MXU_DTYPE)
    mavg = _block_diag(jnp.full((CW // HEAD, HEAD, HEAD), 1.0 / HEAD, F32)).astype(MXU_DTYPE)
    va = jnp.concatenate([lru_conv_w_f, lru_conv_b, lru_b_r, lru_b_i, lru_lambda], axis=0)
    vb = _pad_rows(jnp.concatenate([conv_w_f, conv_b, conv_norm_g, conv_norm_b], axis=0), 40)
    ln1 = _pad_rows(jnp.concatenate([ln1_g, ln1_b], axis=0), 8)
    ln2 = _pad_rows(jnp.concatenate([ln2_g, ln2_b], axis=0), 8)
    f3 = _pad_rows(jnp.concatenate([ffn_conv_w_f, ffn_conv_b], axis=0), 8)

    proj, h, vc, y_b, mixed, x1, u1_b = _mixer_fwd(x, mods, win_t, wout_b, wr_bd, wi_bd, mavg, va, vb, ln1)
    hh, z_b, u2_b, y2, loss_part = _ffn_fwd(x1, mods, loss_target, wup_t, wdown_b, f3, ln2)
    loss = lax.psum(jnp.sum(loss_part) * (0.5 / D), ("x", "y", "c"))

    dx1, dy2_b, dhh_b, dln2, df3, dmod_b = _ffn_bwd(x1, y2, loss_target, hh, mods, wup_t, wdown_b, f3, ln2)
    g_down = _weight_grad(z_b.reshape(ntok, DFF), dy2_b.reshape(ntok, D), FF_CHUNK, "wgrad_down")
    g_up_t = _weight_grad(dhh_b.reshape(ntok, 2 * DFF), u2_b.reshape(ntok, D), FF_CHUNK, "wgrad_up")
    grad_x, dproj_b, dm_b, g_a, g_w31, g_wr, g_wi, dln1, dmod_a = _mixer_bwd(
        x, dx1, proj, h, vc, mixed, mods, win_t, wout_b, wr_bd, wi_bd, mavg, va, vb, ln1)
    g_out = _weight_grad(y_b.reshape(ntok, D), dm_b.reshape(ntok, D), 512, "wgrad_out")
    g_in_t = _weight_grad(dproj_b.reshape(ntok, 2 * LW + 2 * CW), u1_b.reshape(ntok, D), 512, "wgrad_in")

    s_in, s_out, s_up, s_down = _exchange_blocks([g_in_t, g_out, g_up_t, g_down], "exchange_wgrads")
    gs_in_t, gs_out = _sum_slots(s_in, "sum_w_in"), _sum_slots(s_out, "sum_w_out")
    gs_up_t, gs_down = _sum_slots(s_up, "sum_w_up"), _sum_slots(s_down, "sum_w_down")

    dmod = jnp.concatenate([dmod_a[:, 0:3, :], dmod_b[:, 0:3, :]], axis=1).reshape(bl, 6 * D)
    sharded_small = {"lru_conv_w": (1, LRU_K, LW), "conv_w": (1, CONV_K, CW), "ffn_conv_w": (1, FFN_K, DFF)}
    parts = {
        "lru_conv_w": g_a[0:4], "lru_conv_b": g_a[4:5], "lru_w_r": _diag_blocks(g_wr, LW // HEAD), "lru_b_r": g_a[5:6],
        "lru_w_i": _diag_blocks(g_wi, LW // HEAD), "lru_b_i": g_a[6:7], "lru_lambda": g_a[7:8], "conv_w": g_w31[0:CONV_K],
        "conv_b": g_a[10:11], "conv_norm_g": g_a[8:9], "conv_norm_b": g_a[9:10], "ln1_g": dln1[0:1], "ln1_b": dln1[1:2],
        "ffn_conv_w": df3[0:FFN_K], "ffn_conv_b": df3[FFN_K:FFN_K + 1], "ln2_g": dln2[0:1], "ln2_b": dln2[1:2]}
    part_names = list(parts)
    part_shapes = [sharded_small.get(k, weights[k].shape) for k in part_names]
    packed = _pack([dmod] + [parts[k] for k in part_names])
    n_rows = packed.shape[0]
    (packed_all,) = _all_gather([packed], "gather_small_grads")
    packed_all = packed_all.reshape(NDEV, n_rows, D)
    summed = _sum_slots(packed_all, "sum_small_grads")
    full = dict(zip(part_names, _unpack(summed[6 * bl:], part_shapes)))
    dmod_all = packed_all[:, 0:6 * bl, :].reshape(NDEV * bl, 6 * D)

    grads = {}
    g_w_ada, g_b_ada = _ada_bwd(c_all, dmod_all, lax.dynamic_slice(dmod_all, (0, me * ncol), (NDEV * bl, ncol)))
    grads["w_ada"], grads["b_ada"] = g_w_ada[None], g_b_ada
    grads["w_in"] = gs_in_t.T[None]
    grads["w_out"] = gs_out[None]
    grads["ffn_w_up"] = gs_up_t.T[None]
    grads["ffn_w_down"] = gs_down[None]
    for k in part_names:
        gk = full[k]
        if k in sharded_small:
            nloc = gk.shape[2] // NDEV
            gk = lax.dynamic_slice(gk, (0, 0, me * nloc), (1, gk.shape[1], nloc))
        grads[k] = gk

    delta, new_m, new_v = {}, {}, {}
    big = ("w_ada", "w_in", "w_out", "ffn_w_up", "ffn_w_down")
    for k in big:
        d_, m_, v_ = _adamw(weights[k][0], grads[k][0], mom_m[k][0], mom_v[k][0], "adamw_" + k)
        delta[k], new_m[k], new_v[k] = d_[None], m_[None], v_[None]
    small_names = [k for k in names if k not in big]
    shapes = [weights[k].shape for k in small_names]
    d_, m_, v_ = _adamw(_pack([weights[k] for k in small_names]), _pack([grads[k] for k in small_names]),
                        _pack([mom_m[k] for k in small_names]), _pack([mom_v[k] for k in small_names]), "adamw_small")
    for k, dk, mk, vk in zip(small_names, _unpack(d_, shapes), _unpack(m_, shapes), _unpack(v_, shapes)):
        delta[k], new_m[k], new_v[k] = dk, mk, vk

    return (loss, grad_x, *[grads[k] for k in names], *[delta[k] for k in names], *[new_m[k] for k in names],
            *[new_v[k] for k in names])
```

```python
import functools
import math

import jax
import jax.numpy as jnp
from jax import lax
from jax.experimental import pallas as pl
from jax.experimental.pallas import tpu as pltpu

NDEV = 8
D = 1024
LW = 512
CW = 512
HEAD = 64
DFF = 2816
FF_CHUNK = 1408
LRU_K = 4
CONV_K = 31
FFN_K = 3
LRU_C = 8.0
ALPHA = (2 * 1) ** 0.25
LN_EPS = 1e-5
ADAM_LR = 0.001
ADAM_B1 = 0.9
ADAM_B2 = 0.999
ADAM_EPS = 1e-08
ADAM_WD = 0.01
ADAM_STEP = 10

MXU_DTYPE = jnp.bfloat16
TOK_TILE = 256
WG_TOK_TILE = 512
VMEM_LIMIT = 60 * 1024 * 1024
HALO31 = 32
HALO = 8

F32 = jnp.float32
MESH = pl.DeviceIdType.MESH


def _sigmoid(x):
    return 1.0 / (1.0 + jnp.exp(-x))


def _dot(a, b):
    return jnp.dot(a, b, preferred_element_type=F32)


def _dot_nt(a, b):
    return lax.dot_general(a, b, (((1,), (1,)), ((), ())), preferred_element_type=F32)


def _dot_tn(a, b):
    return lax.dot_general(a, b, (((0,), (0,)), ((), ())), preferred_element_type=F32)


def _colsum(v):
    return jnp.sum(v, axis=0, keepdims=True)


def _rowmean(v):
    return jnp.mean(v, axis=-1, keepdims=True)


def _head_mean(v, mavg):
    hi = v.astype(MXU_DTYPE)
    lo = (v - hi.astype(F32)).astype(MXU_DTYPE)
    return _dot(hi, mavg) + _dot(lo, mavg)


_GELU_C0 = math.sqrt(2.0 / math.pi)
_GELU_C1 = 0.044715


def _gelu_and_grad(x):
    x2 = x * x
    th = jnp.tanh(_GELU_C0 * (x + _GELU_C1 * x * x2))
    ge = 0.5 * x * (1.0 + th)
    dge = 0.5 * (1.0 + th) + 0.5 * x * (1.0 - th * th) * (_GELU_C0 * (1.0 + 3.0 * _GELU_C1 * x2))
    return ge, dge


def _softplus(x):
    return jnp.maximum(x, 0.0) + jnp.log1p(jnp.exp(-jnp.abs(x)))


def _layer_norm_stats(r):
    mu = _rowmean(r)
    dl = r - mu
    var = _rowmean(dl * dl)
    rstd = lax.rsqrt(var + LN_EPS)
    return dl * rstd, rstd


def _scan_fwd(a, u, rows):
    n = a.shape[0]
    d = 1
    while d < n:
        keep = rows >= d
        a_s = jnp.where(keep, pltpu.roll(a, d, 0), 1.0)
        u_s = jnp.where(keep, pltpu.roll(u, d, 0), 0.0)
        u = a * u_s + u
        a = a * a_s
        d *= 2
    return u, a


def _scan_bwd(bm, g, rows):
    n = bm.shape[0]
    d = 1
    while d < n:
        keep = rows < n - d
        b_s = jnp.where(keep, pltpu.roll(bm, n - d, 0), 1.0)
        g_s = jnp.where(keep, pltpu.roll(g, n - d, 0), 0.0)
        g = g + bm * g_s
        bm = bm * b_s
        d *= 2
    return g, bm


def _lru_gates(xc, wr, wi, va_ref):
    xcb = xc.astype(MXU_DTYPE)
    r = _sigmoid(_dot(xcb, wr) + va_ref[5:6, :])
    ig = _sigmoid(_dot(xcb, wi) + va_ref[6:7, :])
    sp = _softplus(-va_ref[7:8, :])
    la = (-LRU_C) * r * sp
    a = jnp.exp(la)
    a2 = a * a
    mult = jnp.sqrt(-jnp.tanh(la) * (a2 + 1.0))
    return xcb, r, ig, sp, a, a2, mult


def _full(shape):
    nd = len(shape)
    return pl.BlockSpec(shape, lambda *_: (0,) * nd)


_ANY = pl.BlockSpec(memory_space=pl.ANY)


def _params(n_grid):
    return pltpu.CompilerParams(dimension_semantics=("arbitrary",) * n_grid, vmem_limit_bytes=VMEM_LIMIT)


def _my_place():
    return lax.axis_index("x"), lax.axis_index("y"), lax.axis_index("c")


def _all_gather(arrays, name):
    n_arr = len(arrays)

    def body(*refs):
        start, forward, finish = _gather_steps(refs[:n_arr], refs[n_arr:2 * n_arr], *refs[2 * n_arr:])
        start()
        forward()
        finish()

    return pl.pallas_call(
        body, name=name, out_shape=_gather_out_shapes(arrays),
        in_specs=[_ANY] * n_arr, out_specs=[_ANY] * n_arr, scratch_shapes=_comm_sems(n_arr),
    )(*arrays)


def _gather_out_shapes(arrays):
    return [jax.ShapeDtypeStruct((NDEV * a.shape[0], a.shape[1]), a.dtype) for a in arrays]


def _comm_sems(n_arr):
    return [pltpu.SemaphoreType.DMA((n_arr, 7)), pltpu.SemaphoreType.DMA((n_arr, 7)), pltpu.SemaphoreType.DMA((n_arr,))]


def _gather_steps(x_refs, out_refs, send_sems, recv_sems, local_sems):
    n_arr = len(x_refs)
    x, y, c = _my_place()
    me, sibling = (x, y, c), (x, y, 1 - c)
    chips = [(1 - x, y), (x, 1 - y), (1 - x, 1 - y)]

    def rows(k, px, py, pc):
        m = x_refs[k].shape[0]
        return out_refs[k].at[pl.ds((4 * px + 2 * py + pc) * m, m), :]

    def copy(k, s, block, to, src=None):
        return pltpu.make_async_remote_copy(
            src_ref=rows(k, *block) if src is None else src, dst_ref=rows(k, *block),
            send_sem=send_sems.at[k, s], recv_sem=recv_sems.at[k, s], device_id=to, device_id_type=MESH)

    def mine():
        return [pltpu.make_async_copy(x_refs[k], rows(k, *me), local_sems.at[k]) for k in range(n_arr)]

    def first():
        cps = []
        for k in range(n_arr):
            cps.append(copy(k, 0, me, sibling, src=x_refs[k]))
            cps += [copy(k, 1 + j, me, (*chip, c), src=x_refs[k]) for j, chip in enumerate(chips)]
        return cps

    def passed():
        return [copy(k, 4 + j, (*chip, c), sibling) for k in range(n_arr) for j, chip in enumerate(chips)]

    def start():
        for cp in mine() + first():
            cp.start()

    def forward():
        fwd = passed()
        for k in range(n_arr):
            for j, chip in enumerate(chips):
                copy(k, 1 + j, (*chip, c), me).wait_recv()
                fwd[3 * k + j].start()

    def finish():
        for k in range(n_arr):
            copy(k, 0, sibling, me).wait_recv()
            for j, chip in enumerate(chips):
                copy(k, 4 + j, (*chip, 1 - c), me).wait_recv()
        for cp in first() + passed():
            cp.wait_send()
        for cp in mine():
            cp.wait()

    return start, forward, finish


def _exchange_blocks(arrays, name):
    n_arr = len(arrays)

    def body(*refs):
        start, finish = _exchange_steps(refs[:n_arr], refs[n_arr:2 * n_arr], *refs[2 * n_arr:])
        start()
        finish()

    return pl.pallas_call(
        body, name=name, out_shape=_exchange_out_shapes(arrays),
        in_specs=[_ANY] * n_arr, out_specs=[_ANY] * n_arr, scratch_shapes=_comm_sems(n_arr),
    )(*arrays)


def _exchange_out_shapes(arrays):
    return [jax.ShapeDtypeStruct((NDEV, a.shape[0] // NDEV, a.shape[1]), a.dtype) for a in arrays]


def _exchange_steps(g_refs, out_refs, send_sems, recv_sems, local_sems):
    n_arr = len(g_refs)
    x, y, c = _my_place()
    me = 4 * x + 2 * y + c

    def copy(k, rel):
        px, py, pc = x ^ ((rel >> 2) & 1), y ^ ((rel >> 1) & 1), c ^ (rel & 1)
        m = out_refs[k].shape[1]
        return pltpu.make_async_remote_copy(
            src_ref=g_refs[k].at[pl.ds((4 * px + 2 * py + pc) * m, m), :], dst_ref=out_refs[k].at[me],
            send_sem=send_sems.at[k, rel - 1], recv_sem=recv_sems.at[k, rel - 1],
            device_id=(px, py, pc), device_id_type=MESH)

    def copies():
        cps = []
        for k in range(n_arr):
            m = out_refs[k].shape[1]
            cps.append(pltpu.make_async_copy(g_refs[k].at[pl.ds(me * m, m), :], out_refs[k].at[me], local_sems.at[k]))
            cps += [copy(k, rel) for rel in range(1, NDEV)]
        return cps

    def start():
        for cp in copies():
            cp.start()

    def finish():
        for cp in copies():
            cp.wait()

    return start, finish


def _ada_fwd(c_all, w_ada_loc, b_ada_loc):
    def body(c_ref, w_ref, b_ref, o_ref):
        cv = c_ref[...]
        ca = (cv * _sigmoid(cv)).astype(MXU_DTYPE)
        o_ref[...] = _dot(ca, w_ref[...].astype(MXU_DTYPE)) + b_ref[...]

    return pl.pallas_call(
        body, name="ada_fwd", out_shape=jax.ShapeDtypeStruct((c_all.shape[0], w_ada_loc.shape[1]), F32),
        in_specs=[_full(c_all.shape), _full(w_ada_loc.shape), _full(b_ada_loc.shape)],
        out_specs=_full((c_all.shape[0], w_ada_loc.shape[1])),
        compiler_params=pltpu.CompilerParams(vmem_limit_bytes=VMEM_LIMIT),
    )(c_all, w_ada_loc, b_ada_loc)


def _ada_bwd(c_all, dmod_all, dmod_cols):
    def body(c_ref, da_ref, d_ref, o_ref, b_ref):
        cv = c_ref[...]
        ca = (cv * _sigmoid(cv)).astype(MXU_DTYPE)
        o_ref[...] = _dot_tn(ca, d_ref[...].astype(MXU_DTYPE))
        b_ref[...] = _colsum(da_ref[...])

    return pl.pallas_call(
        body, name="ada_bwd",
        out_shape=[jax.ShapeDtypeStruct((c_all.shape[1], dmod_cols.shape[1]), F32),
                   jax.ShapeDtypeStruct((1, dmod_all.shape[1]), F32)],
        in_specs=[_full(c_all.shape), _full(dmod_all.shape), _full(dmod_cols.shape)],
        out_specs=[_full((c_all.shape[1], dmod_cols.shape[1])), _full((1, dmod_all.shape[1]))],
        compiler_params=pltpu.CompilerParams(vmem_limit_bytes=VMEM_LIMIT),
    )(c_all, dmod_all, dmod_cols)


def _mixer_fwd(x, mods, win_t, wout, wr, wi, mavg, va, vb, ln1, ffn_shards):
    bl, seq, _ = x.shape
    t = min(TOK_TILE, seq)
    nt = seq // t
    n_g = len(ffn_shards)

    def body(x_ref, mod_ref, win_hbm, wout_hbm, wr_ref, wi_ref, mavg_ref, va_ref, vb_ref, ln1_ref, *rest):
        shard_refs, rest = rest[:n_g], rest[n_g:]
        proj_ref, h_ref, vc_ref, y_ref, mixed_ref, x1_ref, u1_ref = rest[:7]
        gathered_refs, rest = rest[7:7 + n_g], rest[7 + n_g:]
        win_v, wout_v, xa_ext, vg_ext, hcar, sems, g_send, g_recv, g_local = rest
        b, i = pl.program_id(0), pl.program_id(1)
        step = b * nt + i
        g_start, g_forward, g_finish = _gather_steps(shard_refs, gathered_refs, g_send, g_recv, g_local)

        @pl.when(step == 0)
        def _():
            g_start()
            cps = [pltpu.make_async_copy(win_hbm, win_v, sems.at[0]), pltpu.make_async_copy(wout_hbm, wout_v, sems.at[1])]
            for cp in cps:
                cp.start()
            for cp in cps:
                cp.wait()

        pl.when(step == (bl * nt) // 2)(g_forward)

        @pl.when(i == 0)
        def _():
            xa_ext[0:HALO, :] = jnp.zeros((HALO, LW), F32)
            vg_ext[0:HALO31, :] = jnp.zeros((HALO31, CW), F32)
            hcar[...] = jnp.zeros_like(hcar)

        rows = lax.broadcasted_iota(jnp.int32, (t, LW), 0)
        xv = x_ref[...]
        sh, sc, gt = mod_ref[0:1, :], mod_ref[1:2, :], mod_ref[2:3, :]
        ub = (xv * (1.0 + sc) + sh).astype(MXU_DTYPE)
        u1_ref[...] = ub
        proj = _dot_nt(ub, win_v[...])
        proj_ref[...] = proj
        xa, ga = proj[:, 0:LW], proj[:, LW:2 * LW]
        vbr, gb = proj[:, 2 * LW:2 * LW + CW], proj[:, 2 * LW + CW:]

        xa_ext[HALO:HALO + t, :] = xa
        xc = va_ref[4:5, :]
        for k in range(LRU_K):
            xc = xc + va_ref[k:k + 1, :] * xa_ext[pl.ds(HALO - (LRU_K - 1) + k, t), :]
        xa_ext[0:HALO, :] = xa_ext[t:t + HALO, :]
        _, _, ig, _, a, _, mult = _lru_gates(xc, wr_ref[...], wi_ref[...], va_ref)
        h_loc, a_cum = _scan_fwd(a, mult * (ig * xc), rows)
        h = h_loc + a_cum * hcar[0:1, :]
        hcar[0:1, :] = h[t - 1:t, :]
        h_ref[...] = h
        ge, _ = _gelu_and_grad(ga)
        ya = ge * h

        vg_ext[HALO31:HALO31 + t, :] = vbr * _sigmoid(gb)
        vc = vb_ref[CONV_K:CONV_K + 1, :]
        for k in range(CONV_K):
            vc = vc + vb_ref[k:k + 1, :] * vg_ext[pl.ds(HALO31 - (CONV_K - 1) + k, t), :]
        vg_ext[0:HALO31, :] = vg_ext[t:t + HALO31, :]
        vc_ref[...] = vc
        mavg_v = mavg_ref[...]
        dl = vc - _head_mean(vc, mavg_v)
        rs = lax.rsqrt(_head_mean(dl * dl, mavg_v) + LN_EPS)
        yl = dl * rs * vb_ref[CONV_K + 1:CONV_K + 2, :] + vb_ref[CONV_K + 2:CONV_K + 3, :]
        yb = yl * _sigmoid(yl)

        yv = jnp.concatenate([ya, yb], axis=1).astype(MXU_DTYPE)
        y_ref[...] = yv
        mixed = _dot(yv, wout_v[...])
        mixed_ref[...] = mixed
        xh, _ = _layer_norm_stats(ALPHA * xv + (1.0 + gt) * mixed)
        x1_ref[...] = xh * ln1_ref[0:1, :] + ln1_ref[1:2, :]

        pl.when(step == bl * nt - 1)(g_finish)

    tok = lambda w: pl.BlockSpec((None, t, w), lambda b, i: (b, i, 0))
    outs = [(2 * LW + 2 * CW, F32), (LW, F32), (CW, F32), (D, MXU_DTYPE), (D, F32), (D, F32), (D, MXU_DTYPE)]
    return pl.pallas_call(
        body, name="mixer_fwd", grid=(bl, nt),
        out_shape=[jax.ShapeDtypeStruct((bl, seq, w), dt) for w, dt in outs] + _gather_out_shapes(ffn_shards),
        in_specs=[tok(D), pl.BlockSpec((None, 8, D), lambda b, i: (b, 0, 0)), _ANY, _ANY,
                  _full(wr.shape), _full(wi.shape), _full(mavg.shape), _full(va.shape), _full(vb.shape), _full(ln1.shape)]
        + [_ANY] * n_g,
        out_specs=[tok(w) for w, _ in outs] + [_ANY] * n_g,
        scratch_shapes=[pltpu.VMEM(win_t.shape, MXU_DTYPE), pltpu.VMEM(wout.shape, MXU_DTYPE),
                        pltpu.VMEM((t + HALO, LW), F32), pltpu.VMEM((t + HALO31, CW), F32), pltpu.VMEM((8, LW), F32),
                        pltpu.SemaphoreType.DMA((2,))] + _comm_sems(n_g),
        compiler_params=_params(2),
    )(x, mods, win_t, wout, wr, wi, mavg, va, vb, ln1, *ffn_shards)


def _ffn_fwd(x1, mods, tgt, wup_t, wdown, f3, ln2):
    bl, seq, _ = x1.shape
    t = min(TOK_TILE, seq)
    nt = seq // t
    n_chunk = DFF // FF_CHUNK

    def body(x1_ref, mod_ref, tgt_ref, wup_hbm, wdown_hbm, f3_ref, ln2_ref,
             hh_ref, z_ref, u2_ref, y2_ref, loss_ref,
             wup_v, wdown_v, g_ext, sems):
        b, i = pl.program_id(0), pl.program_id(1)

        @pl.when((b == 0) & (i == 0))
        def _():
            cps = [pltpu.make_async_copy(wup_hbm, wup_v, sems.at[0]), pltpu.make_async_copy(wdown_hbm, wdown_v, sems.at[1])]
            for cp in cps:
                cp.start()
            for cp in cps:
                cp.wait()
            loss_ref[...] = jnp.zeros_like(loss_ref)

        @pl.when(i == 0)
        def _():
            for ch in range(n_chunk):
                g_ext[ch, 0:HALO, :] = jnp.zeros((HALO, FF_CHUNK), F32)

        x1v = x1_ref[...]
        sh, sc, gt = mod_ref[3:4, :], mod_ref[4:5, :], mod_ref[5:6, :]
        ub = (x1v * (1.0 + sc) + sh).astype(MXU_DTYPE)
        u2_ref[...] = ub
        y2 = jnp.zeros((t, D), F32)
        for ch in range(n_chunk):
            lo = ch * FF_CHUNK
            v = _dot_nt(ub, wup_v[lo:lo + FF_CHUNK, :])
            g = _dot_nt(ub, wup_v[DFF + lo:DFF + lo + FF_CHUNK, :])
            hh_ref[:, lo:lo + FF_CHUNK] = v
            hh_ref[:, DFF + lo:DFF + lo + FF_CHUNK] = g
            g_ext[ch, HALO:HALO + t, :] = g
            gc = f3_ref[FFN_K:FFN_K + 1, lo:lo + FF_CHUNK]
            for k in range(FFN_K):
                gc = gc + f3_ref[k:k + 1, lo:lo + FF_CHUNK] * g_ext[ch, pl.ds(HALO - (FFN_K - 1) + k, t), :]
            g_ext[ch, 0:HALO, :] = g_ext[ch, t:t + HALO, :]
            zb = (gc * _sigmoid(gc) * v).astype(MXU_DTYPE)
            z_ref[:, lo:lo + FF_CHUNK] = zb
            y2 = y2 + _dot(zb, wdown_v[lo:lo + FF_CHUNK, :])
        y2_ref[...] = y2
        xh, _ = _layer_norm_stats(ALPHA * x1v + (1.0 + gt) * y2)
        err = xh * ln2_ref[0:1, :] + ln2_ref[1:2, :] - tgt_ref[...]
        e2 = (err * err).reshape(t // 8, 8, D).sum(axis=0)
        part = e2[:, 0:128]
        for j in range(1, D // 128):
            part = part + e2[:, 128 * j:128 * (j + 1)]
        loss_ref[...] += part

    tok = lambda w: pl.BlockSpec((None, t, w), lambda b, i: (b, i, 0))
    outs = [(2 * DFF, F32), (DFF, MXU_DTYPE), (D, MXU_DTYPE), (D, F32)]
    return pl.pallas_call(
        body, name="ffn_fwd", grid=(bl, nt),
        out_shape=[jax.ShapeDtypeStruct((bl, seq, w), dt) for w, dt in outs] + [jax.ShapeDtypeStruct((8, 128), F32)],
        in_specs=[tok(D), pl.BlockSpec((None, 8, D), lambda b, i: (b, 0, 0)), tok(D), _ANY, _ANY,
                  _full(f3.shape), _full(ln2.shape)],
        out_specs=[tok(w) for w, _ in outs] + [_full((8, 128))],
        scratch_shapes=[pltpu.VMEM(wup_t.shape, MXU_DTYPE), pltpu.VMEM(wdown.shape, MXU_DTYPE),
                        pltpu.VMEM((n_chunk, t + HALO, FF_CHUNK), F32), pltpu.SemaphoreType.DMA((2,))],
        compiler_params=_params(2),
    )(x1, mods, tgt, wup_t, wdown, f3, ln2)


def _ffn_bwd(x1, y2, tgt, hh, mods, wup_t, wdown, f3, ln2):
    bl, seq, _ = x1.shape
    t = min(TOK_TILE, seq)
    nt = seq // t
    n_chunk = DFF // FF_CHUNK

    def body(x1_ref, y2_ref, tgt_ref, hh_ref, halo_ref, mod_ref, wup_hbm, wdown_hbm, f3_ref, ln2_ref,
             dx1_ref, dy2_ref, dhh_ref, dln2_ref, df3_ref, dmod_ref,
             wup_v, wdown_v, g_ext, dgc_ext, sems):
        b, i = pl.program_id(0), pl.program_id(1)
        tt = nt - 1 - i

        @pl.when((b == 0) & (i == 0))
        def _():
            cps = [pltpu.make_async_copy(wup_hbm, wup_v, sems.at[0]), pltpu.make_async_copy(wdown_hbm, wdown_v, sems.at[1])]
            for cp in cps:
                cp.start()
            for cp in cps:
                cp.wait()
            dln2_ref[...] = jnp.zeros_like(dln2_ref)
            df3_ref[...] = jnp.zeros_like(df3_ref)

        @pl.when(i == 0)
        def _():
            dmod_ref[...] = jnp.zeros_like(dmod_ref)
            for ch in range(n_chunk):
                dgc_ext[ch, t:t + HALO, :] = jnp.zeros((HALO, FF_CHUNK), F32)

        x1v, y2v = x1_ref[...], y2_ref[...]
        sc, gt = mod_ref[4:5, :], mod_ref[5:6, :]
        xh, rstd = _layer_norm_stats(ALPHA * x1v + (1.0 + gt) * y2v)
        g2 = ln2_ref[0:1, :]
        dx2 = (xh * g2 + ln2_ref[1:2, :] - tgt_ref[...]) * (1.0 / D)
        dln2_ref[0:1, :] += _colsum(dx2 * xh)
        dln2_ref[1:2, :] += _colsum(dx2)
        dxh = dx2 * g2
        dr2 = rstd * (dxh - _rowmean(dxh) - xh * _rowmean(dxh * xh))
        dmod_ref[2:3, :] += _colsum(dr2 * y2v)
        dyb = ((1.0 + gt) * dr2).astype(MXU_DTYPE)
        dy2_ref[...] = dyb

        halo_keep = (tt > 0).astype(F32)
        du2 = jnp.zeros((t, D), F32)
        for ch in range(n_chunk):
            lo = ch * FF_CHUNK
            dz = _dot_nt(dyb, wdown_v[lo:lo + FF_CHUNK, :])
            v = hh_ref[:, lo:lo + FF_CHUNK]
            g_ext[ch, 0:HALO, :] = halo_ref[:, lo:lo + FF_CHUNK] * halo_keep
            g_ext[ch, HALO:HALO + t, :] = hh_ref[:, DFF + lo:DFF + lo + FF_CHUNK]
            gc = f3_ref[FFN_K:FFN_K + 1, lo:lo + FF_CHUNK]
            for k in range(FFN_K):
                gc = gc + f3_ref[k:k + 1, lo:lo + FF_CHUNK] * g_ext[ch, pl.ds(HALO - (FFN_K - 1) + k, t), :]
            s = _sigmoid(gc)
            dv = dz * (gc * s)
            dgc = dz * v * (s * (1.0 + gc * (1.0 - s)))
            df3_ref[FFN_K:FFN_K + 1, lo:lo + FF_CHUNK] += _colsum(dgc)
            for k in range(FFN_K):
                df3_ref[k:k + 1, lo:lo + FF_CHUNK] += _colsum(dgc * g_ext[ch, pl.ds(HALO - (FFN_K - 1) + k, t), :])
            dgc_ext[ch, 0:t, :] = dgc
            dg = jnp.zeros((t, FF_CHUNK), F32)
            for k in range(FFN_K):
                dg = dg + f3_ref[k:k + 1, lo:lo + FF_CHUNK] * dgc_ext[ch, pl.ds(FFN_K - 1 - k, t), :]
            dgc_ext[ch, t:t + HALO, :] = dgc[0:HALO, :]
            dvb, dgb = dv.astype(MXU_DTYPE), dg.astype(MXU_DTYPE)
            dhh_ref[:, lo:lo + FF_CHUNK] = dvb
            dhh_ref[:, DFF + lo:DFF + lo + FF_CHUNK] = dgb
            du2 = du2 + _dot(dvb, wup_v[lo:lo + FF_CHUNK, :]) + _dot(dgb, wup_v[DFF + lo:DFF + lo + FF_CHUNK, :])
        dx1_ref[...] = ALPHA * dr2 + du2 * (1.0 + sc)
        dmod_ref[1:2, :] += _colsum(du2 * x1v)
        dmod_ref[0:1, :] += _colsum(du2)

    rev = lambda w: pl.BlockSpec((None, t, w), lambda b, i: (b, nt - 1 - i, 0))
    halo = pl.BlockSpec((None, HALO, DFF), lambda b, i: (b, jnp.maximum((nt - 1 - i) * (t // HALO) - 1, 0), 1))
    return pl.pallas_call(
        body, name="ffn_bwd", grid=(bl, nt),
        out_shape=[jax.ShapeDtypeStruct((bl, seq, D), F32), jax.ShapeDtypeStruct((bl, seq, D), MXU_DTYPE),
                   jax.ShapeDtypeStruct((bl, seq, 2 * DFF), MXU_DTYPE), jax.ShapeDtypeStruct((8, D), F32),
                   jax.ShapeDtypeStruct((8, DFF), F32), jax.ShapeDtypeStruct((bl, 8, D), F32)],
        in_specs=[rev(D), rev(D), rev(D), rev(2 * DFF), halo, pl.BlockSpec((None, 8, D), lambda b, i: (b, 0, 0)),
                  _ANY, _ANY, _full(f3.shape), _full(ln2.shape)],
        out_specs=[rev(D), rev(D), rev(2 * DFF), _full((8, D)), _full((8, DFF)),
                   pl.BlockSpec((None, 8, D), lambda b, i: (b, 0, 0))],
        scratch_shapes=[pltpu.VMEM(wup_t.shape, MXU_DTYPE), pltpu.VMEM(wdown.shape, MXU_DTYPE),
                        pltpu.VMEM((n_chunk, t + HALO, FF_CHUNK), F32), pltpu.VMEM((n_chunk, t + HALO, FF_CHUNK), F32),
                        pltpu.SemaphoreType.DMA((2,))],
        compiler_params=_params(2),
    )(x1, y2, tgt, hh, hh, mods, wup_t, wdown, f3, ln2)


def _mixer_bwd(x, dx1, proj, h, vc, mixed, mods, win_t, wout, wr, wi, mavg, va, vb, ln1, ffn_wgrads):
    bl, seq, _ = x.shape
    t = min(TOK_TILE, seq)
    nt = seq // t
    pw = 2 * LW + 2 * CW
    n_g = len(ffn_wgrads)

    def body(x_ref, dx1_ref, proj_ref, phalo_ref, h_ref, hhalo_ref, vc_ref, mixed_ref, mod_ref,
             win_hbm, wout_hbm, wr_ref, wi_ref, mavg_ref, va_ref, vb_ref, ln1_ref, *rest):
        wgrad_refs, rest = rest[:n_g], rest[n_g:]
        gx_ref, dproj_ref, dm_ref, ga_ref, gw31_ref, dwr_ref, dwi_ref, dln1_ref, dmod_ref = rest[:9]
        slot_refs, rest = rest[9:9 + n_g], rest[9 + n_g:]
        win_v, wout_v, xa_ext, vg_ext, dvc_ext, dxc_ext, car, sems, e_send, e_recv, e_local = rest
        b, i = pl.program_id(0), pl.program_id(1)
        tt = nt - 1 - i
        e_start, e_finish = _exchange_steps(wgrad_refs, slot_refs, e_send, e_recv, e_local)

        @pl.when((b == 0) & (i == 0))
        def _():
            e_start()
            cps = [pltpu.make_async_copy(win_hbm, win_v, sems.at[0]), pltpu.make_async_copy(wout_hbm, wout_v, sems.at[1])]
            for cp in cps:
                cp.start()
            for cp in cps:
                cp.wait()
            for ref in (ga_ref, gw31_ref, dwr_ref, dwi_ref, dln1_ref):
                ref[...] = jnp.zeros_like(ref)

        @pl.when(i == 0)
        def _():
            dmod_ref[...] = jnp.zeros_like(dmod_ref)
            dvc_ext[t:t + HALO31, :] = jnp.zeros((HALO31, CW), F32)
            dxc_ext[t:t + HALO, :] = jnp.zeros((HALO, LW), F32)
            car[...] = jnp.zeros_like(car)

        rows = lax.broadcasted_iota(jnp.int32, (t, LW), 0)
        halo_keep = (tt > 0).astype(F32)
        xv, mixed = x_ref[...], mixed_ref[...]
        sc, gt = mod_ref[1:2, :], mod_ref[2:3, :]

        xh, rstd = _layer_norm_stats(ALPHA * xv + (1.0 + gt) * mixed)
        dx1 = dx1_ref[...]
        dln1_ref[0:1, :] += _colsum(dx1 * xh)
        dln1_ref[1:2, :] += _colsum(dx1)
        dxh = dx1 * ln1_ref[0:1, :]
        dr1 = rstd * (dxh - _rowmean(dxh) - xh * _rowmean(dxh * xh))
        dmod_ref[2:3, :] += _colsum(dr1 * mixed)
        dmb = ((1.0 + gt) * dr1).astype(MXU_DTYPE)
        dm_ref[...] = dmb
        dy = _dot_nt(dmb, wout_v[...])
        dya, dyb = dy[:, 0:LW], dy[:, LW:]

        proj = proj_ref[...]
        xa, ga = proj[:, 0:LW], proj[:, LW:2 * LW]
        vbr, gb = proj[:, 2 * LW:2 * LW + CW], proj[:, 2 * LW + CW:]
        phalo = phalo_ref[...] * halo_keep

        sgb = _sigmoid(gb)
        vg_ext[0:HALO31, :] = phalo[:, 2 * LW:2 * LW + CW] * _sigmoid(phalo[:, 2 * LW + CW:])
        vg_ext[HALO31:HALO31 + t, :] = vbr * sgb
        mavg_v = mavg_ref[...]
        vcv = vc_ref[...]
        dl = vcv - _head_mean(vcv, mavg_v)
        rs = lax.rsqrt(_head_mean(dl * dl, mavg_v) + LN_EPS)
        yn = dl * rs
        ng = vb_ref[CONV_K + 1:CONV_K + 2, :]
        yl = yn * ng + vb_ref[CONV_K + 2:CONV_K + 3, :]
        s = _sigmoid(yl)
        dyl = dyb * (s * (1.0 + yl * (1.0 - s)))
        ga_ref[8:9, :] += _colsum(dyl * yn)
        ga_ref[9:10, :] += _colsum(dyl)
        dyn = dyl * ng
        dvc = rs * (dyn - _head_mean(dyn, mavg_v) - yn * _head_mean(dyn * yn, mavg_v))
        ga_ref[10:11, :] += _colsum(dvc)
        dvc_ext[0:t, :] = dvc
        dvg = jnp.zeros((t, CW), F32)
        for k in range(CONV_K):
            gw31_ref[k:k + 1, :] += _colsum(dvc * vg_ext[pl.ds(HALO31 - (CONV_K - 1) + k, t), :])
            dvg = dvg + vb_ref[k:k + 1, :] * dvc_ext[pl.ds(CONV_K - 1 - k, t), :]
        dvc_ext[t:t + HALO31, :] = dvc[0:HALO31, :]
        dvb = dvg * sgb
        dgb = dvg * vbr * sgb * (1.0 - sgb)

        xa_ext[0:HALO, :] = phalo[HALO31 - HALO:HALO31, 0:LW]
        xa_ext[HALO:HALO + t, :] = xa
        xc = va_ref[4:5, :]
        for k in range(LRU_K):
            xc = xc + va_ref[k:k + 1, :] * xa_ext[pl.ds(HALO - (LRU_K - 1) + k, t), :]
        wr_v, wi_v = wr_ref[...], wi_ref[...]
        xcb, r, ig, sp, a, a2, mult = _lru_gates(xc, wr_v, wi_v, va_ref)
        hv = h_ref[...]
        hprev = jnp.where(rows >= 1, pltpu.roll(hv, 1, 0), hhalo_ref[HALO - 1:HALO, :] * halo_keep)
        ge, dge = _gelu_and_grad(ga)
        dga = dya * hv * dge
        bm = jnp.where(rows < t - 1, pltpu.roll(a, t - 1, 0), car[0:1, :])
        g_loc, b_cum = _scan_bwd(bm, dya * ge, rows)
        gv = g_loc + b_cum * car[1:2, :]
        car[0:1, :] = a[0:1, :]
        car[1:2, :] = gv[0:1, :]
        da = gv * hprev
        dmult = gv * (ig * xc)
        di = gv * (mult * xc)
        dxc = gv * (mult * ig)
        dla = da * a - dmult * (a2 / mult)
        dr = dla * ((-LRU_C) * sp)
        lam = va_ref[7:8, :]
        ga_ref[7:8, :] += _colsum(dla * ((-LRU_C) * r)) * (-_sigmoid(-lam))
        dgr = dr * r * (1.0 - r)
        dgi = di * ig * (1.0 - ig)
        ga_ref[5:6, :] += _colsum(dgr)
        ga_ref[6:7, :] += _colsum(dgi)
        dgrb, dgib = dgr.astype(MXU_DTYPE), dgi.astype(MXU_DTYPE)
        dwr_ref[...] += _dot_tn(xcb, dgrb)
        dwi_ref[...] += _dot_tn(xcb, dgib)
        dxc = dxc + _dot_nt(dgrb, wr_v) + _dot_nt(dgib, wi_v)
        ga_ref[4:5, :] += _colsum(dxc)
        dxc_ext[0:t, :] = dxc
        dxa = jnp.zeros((t, LW), F32)
        for k in range(LRU_K):
            ga_ref[k:k + 1, :] += _colsum(dxc * xa_ext[pl.ds(HALO - (LRU_K - 1) + k, t), :])
            dxa = dxa + va_ref[k:k + 1, :] * dxc_ext[pl.ds(LRU_K - 1 - k, t), :]
        dxc_ext[t:t + HALO, :] = dxc[0:HALO, :]

        dpb = jnp.concatenate([dxa, dga, dvb, dgb], axis=1).astype(MXU_DTYPE)
        dproj_ref[...] = dpb
        du1 = _dot(dpb, win_v[...])
        gx_ref[...] = ALPHA * dr1 + du1 * (1.0 + sc)
        dmod_ref[1:2, :] += _colsum(du1 * xv)
        dmod_ref[0:1, :] += _colsum(du1)

        pl.when((b == bl - 1) & (i == nt - 1))(e_finish)

    rev = lambda w: pl.BlockSpec((None, t, w), lambda b, i: (b, nt - 1 - i, 0))

    def halo(rows_, w):
        return pl.BlockSpec((None, rows_, w), lambda b, i: (b, jnp.maximum((nt - 1 - i) * (t // rows_) - 1, 0), 0))

    return pl.pallas_call(
        body, name="mixer_bwd", grid=(bl, nt),
        out_shape=[jax.ShapeDtypeStruct((bl, seq, D), F32), jax.ShapeDtypeStruct((bl, seq, pw), MXU_DTYPE),
                   jax.ShapeDtypeStruct((bl, seq, D), MXU_DTYPE), jax.ShapeDtypeStruct((16, LW), F32),
                   jax.ShapeDtypeStruct((32, CW), F32), jax.ShapeDtypeStruct((LW, LW), F32),
                   jax.ShapeDtypeStruct((LW, LW), F32), jax.ShapeDtypeStruct((8, D), F32),
                   jax.ShapeDtypeStruct((bl, 8, D), F32)] + _exchange_out_shapes(ffn_wgrads),
        in_specs=[rev(D), rev(D), rev(pw), halo(HALO31, pw), rev(LW), halo(HALO, LW), rev(CW), rev(D),
                  pl.BlockSpec((None, 8, D), lambda b, i: (b, 0, 0)), _ANY, _ANY,
                  _full(wr.shape), _full(wi.shape), _full(mavg.shape), _full(va.shape), _full(vb.shape), _full(ln1.shape)]
        + [_ANY] * n_g,
        out_specs=[rev(D), rev(pw), rev(D), _full((16, LW)), _full((32, CW)), _full((LW, LW)), _full((LW, LW)),
                   _full((8, D)), pl.BlockSpec((None, 8, D), lambda b, i: (b, 0, 0))] + [_ANY] * n_g,
        scratch_shapes=[pltpu.VMEM(win_t.shape, MXU_DTYPE), pltpu.VMEM(wout.shape, MXU_DTYPE),
                        pltpu.VMEM((t + HALO, LW), F32), pltpu.VMEM((t + HALO31, CW), F32),
                        pltpu.VMEM((t + HALO31, CW), F32), pltpu.VMEM((t + HALO, LW), F32), pltpu.VMEM((8, LW), F32),
                        pltpu.SemaphoreType.DMA((2,))] + _comm_sems(n_g),
        compiler_params=_params(2),
    )(x, dx1, proj, proj, h, h, vc, mixed, mods, win_t, wout, wr, wi, mavg, va, vb, ln1, *ffn_wgrads)


def _weight_grad(a, bmat, tm, name):
    ntok, m = a.shape
    n = bmat.shape[1]
    tk = min(WG_TOK_TILE, ntok)
    nk = ntok // tk

    def body(a_ref, b_ref, o_ref):
        @pl.when(pl.program_id(1) == 0)
        def _():
            o_ref[...] = jnp.zeros_like(o_ref)

        o_ref[...] += _dot_tn(a_ref[...], b_ref[...])

    return pl.pallas_call(
        body, name=name, grid=(m // tm, nk), out_shape=jax.ShapeDtypeStruct((m, n), F32),
        in_specs=[pl.BlockSpec((tk, tm), lambda i, k: (k, i)), pl.BlockSpec((tk, n), lambda i, k: (k, 0))],
        out_specs=pl.BlockSpec((tm, n), lambda i, k: (i, 0)),
        compiler_params=_params(2),
    )(a, bmat)


def _sum_slots(slots, name):
    _, r, cdim = slots.shape
    tr = 32 if r % 32 == 0 else r

    def body(s_ref, o_ref):
        acc = s_ref[0]
        for j in range(1, NDEV):
            acc = acc + s_ref[j]
        o_ref[...] = acc

    return pl.pallas_call(
        body, name=name, grid=(r // tr,), out_shape=jax.ShapeDtypeStruct((r, cdim), F32),
        in_specs=[pl.BlockSpec((NDEV, tr, cdim), lambda i: (0, i, 0))],
        out_specs=pl.BlockSpec((tr, cdim), lambda i: (i, 0)),
        compiler_params=_params(1),
    )(slots)


def _adamw_update(w_ref, g_ref, m_ref, v_ref, d_ref, nm_ref, nv_ref):
    gv = g_ref[...]
    nm = ADAM_B1 * m_ref[...] + (1.0 - ADAM_B1) * gv
    nv = ADAM_B2 * v_ref[...] + (1.0 - ADAM_B2) * (gv * gv)
    m_hat = nm / (1.0 - ADAM_B1 ** ADAM_STEP)
    v_hat = nv / (1.0 - ADAM_B2 ** ADAM_STEP)
    d_ref[...] = -ADAM_LR * (m_hat / (jnp.sqrt(v_hat) + ADAM_EPS) + ADAM_WD * w_ref[...])
    nm_ref[...] = nm
    nv_ref[...] = nv


def _adamw_many(ws, gs, ms, vs, name):
    n = len(ws)

    def body(*refs):
        ins, outs = refs[:4 * n], refs[4 * n:]
        for k in range(n):
            _adamw_update(ins[k], ins[n + k], ins[2 * n + k], ins[3 * n + k], outs[k], outs[n + k], outs[2 * n + k])

    specs = [_full(w.shape) for w in ws]
    res = pl.pallas_call(
        body, name=name, out_shape=[jax.ShapeDtypeStruct(w.shape, F32) for w in ws] * 3,
        in_specs=specs * 4, out_specs=specs * 3,
        compiler_params=pltpu.CompilerParams(vmem_limit_bytes=VMEM_LIMIT),
    )(*ws, *gs, *ms, *vs)
    return res[:n], res[n:2 * n], res[2 * n:]


def _adamw(w, g, m, v, name):
    r, cdim = w.shape
    tr = 128 if r % 128 == 0 else r
    body = functools.partial(_adamw_update)
    spec = pl.BlockSpec((tr, cdim), lambda i: (i, 0))
    return pl.pallas_call(
        body, name=name, grid=(r // tr,), out_shape=[jax.ShapeDtypeStruct((r, cdim), F32)] * 3,
        in_specs=[spec] * 4, out_specs=[spec] * 3, compiler_params=_params(1),
    )(w, g, m, v)


def _pack(arrs, width=D, row_mult=8):
    parts = []
    for a in arrs:
        flat = a.reshape(-1)
        pad = (-flat.shape[0]) % width
        parts.append(jnp.pad(flat, (0, pad)))
    flat = jnp.concatenate(parts)
    pad = (-flat.shape[0]) % (width * row_mult)
    return jnp.pad(flat, (0, pad)).reshape(-1, width)


def _unpack(buf, shapes, width=D):
    out, row = [], 0
    for shp in shapes:
        size = math.prod(shp)
        nrow = -(-size // width)
        out.append(buf[row:row + nrow].reshape(-1)[:size].reshape(shp))
        row += nrow
    return out


def _unpack_gathered(buf, shapes, width=D):
    out, row = [], 0
    for shp in shapes:
        size = math.prod(shp)
        nrow = -(-size // width)
        out.append(buf[:, row:row + nrow].reshape(NDEV, -1)[:, :size].reshape((NDEV,) + tuple(shp)))
        row += nrow
    return out


def _block_diag(w):
    hn, dh, _ = w.shape
    eye = jnp.eye(hn, dtype=w.dtype)
    return (w[:, :, None, :] * eye[:, None, :, None]).reshape(hn * dh, hn * dh)


def _diag_blocks(wfull, hn):
    dh = wfull.shape[0] // hn
    return jnp.stack([wfull[k * dh:(k + 1) * dh, k * dh:(k + 1) * dh] for k in range(hn)])


def _ungather_cols(g, k):
    n = g.shape[1]
    return g.reshape(NDEV, k, n).transpose(1, 0, 2).reshape(k, NDEV * n)


def _pad_rows(a, rows):
    return jnp.pad(a, ((0, rows - a.shape[0]), (0, 0)))


def kernel(x, c, w_ada, b_ada, w_in, lru_conv_w, lru_conv_b, lru_w_r, lru_b_r, lru_w_i, lru_b_i, lru_lambda, conv_w, conv_b, conv_norm_g, conv_norm_b, w_out, ln1_g, ln1_b, ffn_w_up, ffn_conv_w, ffn_conv_b, ffn_w_down, ln2_g, ln2_b, loss_target, m_w_ada, m_b_ada, m_w_in, m_lru_conv_w, m_lru_conv_b, m_lru_w_r, m_lru_b_r, m_lru_w_i, m_lru_b_i, m_lru_lambda, m_conv_w, m_conv_b, m_conv_norm_g, m_conv_norm_b, m_w_out, m_ln1_g, m_ln1_b, m_ffn_w_up, m_ffn_conv_w, m_ffn_conv_b, m_ffn_w_down, m_ln2_g, m_ln2_b, v_w_ada, v_b_ada, v_w_in, v_lru_conv_w, v_lru_conv_b, v_lru_w_r, v_lru_b_r, v_lru_w_i, v_lru_b_i, v_lru_lambda, v_conv_w, v_conv_b, v_conv_norm_g, v_conv_norm_b, v_w_out, v_ln1_g, v_ln1_b, v_ffn_w_up, v_ffn_conv_w, v_ffn_conv_b, v_ffn_w_down, v_ln2_g, v_ln2_b):
    weights = dict(w_ada=w_ada, b_ada=b_ada, w_in=w_in, lru_conv_w=lru_conv_w, lru_conv_b=lru_conv_b, lru_w_r=lru_w_r,
                   lru_b_r=lru_b_r, lru_w_i=lru_w_i, lru_b_i=lru_b_i, lru_lambda=lru_lambda, conv_w=conv_w, conv_b=conv_b,
                   conv_norm_g=conv_norm_g, conv_norm_b=conv_norm_b, w_out=w_out, ln1_g=ln1_g, ln1_b=ln1_b,
                   ffn_w_up=ffn_w_up, ffn_conv_w=ffn_conv_w, ffn_conv_b=ffn_conv_b, ffn_w_down=ffn_w_down, ln2_g=ln2_g,
                   ln2_b=ln2_b)
    mom_m = dict(w_ada=m_w_ada, b_ada=m_b_ada, w_in=m_w_in, lru_conv_w=m_lru_conv_w, lru_conv_b=m_lru_conv_b,
                 lru_w_r=m_lru_w_r, lru_b_r=m_lru_b_r, lru_w_i=m_lru_w_i, lru_b_i=m_lru_b_i, lru_lambda=m_lru_lambda,
                 conv_w=m_conv_w, conv_b=m_conv_b, conv_norm_g=m_conv_norm_g, conv_norm_b=m_conv_norm_b, w_out=m_w_out,
                 ln1_g=m_ln1_g, ln1_b=m_ln1_b, ffn_w_up=m_ffn_w_up, ffn_conv_w=m_ffn_conv_w, ffn_conv_b=m_ffn_conv_b,
                 ffn_w_down=m_ffn_w_down, ln2_g=m_ln2_g, ln2_b=m_ln2_b)
    mom_v = dict(w_ada=v_w_ada, b_ada=v_b_ada, w_in=v_w_in, lru_conv_w=v_lru_conv_w, lru_conv_b=v_lru_conv_b,
                 lru_w_r=v_lru_w_r, lru_b_r=v_lru_b_r, lru_w_i=v_lru_w_i, lru_b_i=v_lru_b_i, lru_lambda=v_lru_lambda,
                 conv_w=v_conv_w, conv_b=v_conv_b, conv_norm_g=v_conv_norm_g, conv_norm_b=v_conv_norm_b, w_out=v_w_out,
                 ln1_g=v_ln1_g, ln1_b=v_ln1_b, ffn_w_up=v_ffn_w_up, ffn_conv_w=v_ffn_conv_w, ffn_conv_b=v_ffn_conv_b,
                 ffn_w_down=v_ffn_w_down, ln2_g=v_ln2_g, ln2_b=v_ln2_b)
    names = list(weights)
    bl, seq, _ = x.shape
    ntok = bl * seq
    me = 4 * lax.axis_index("x") + 2 * lax.axis_index("y") + lax.axis_index("c")

    small_shapes = [(bl, D), (LRU_K, LW // NDEV), (CONV_K, CW // NDEV), (FFN_K, DFF // NDEV)]
    small = _pack([c, lru_conv_w[0], conv_w[0], ffn_conv_w[0]], width=128)
    n_small = small.shape[0]
    (small_all,) = _all_gather([small], "gather_small")
    small_all = small_all.reshape(NDEV, n_small, 128)
    c_parts, k4_parts, k31_parts, k3_parts = _unpack_gathered(small_all, small_shapes, width=128)
    c_all = c_parts.reshape(NDEV * bl, D)
    lru_conv_w_f = k4_parts.transpose(1, 0, 2).reshape(LRU_K, LW)
    conv_w_f = k31_parts.transpose(1, 0, 2).reshape(CONV_K, CW)
    ffn_conv_w_f = k3_parts.transpose(1, 0, 2).reshape(FFN_K, DFF)

    ncol = w_ada.shape[2]
    b_ada_loc = lax.dynamic_slice(b_ada, (0, me * ncol), (1, ncol))
    mod_cols = _ada_fwd(c_all, w_ada[0], b_ada_loc)
    (mod_all,) = _all_gather([mod_cols], "gather_mod")
    mod_all = mod_all.reshape(NDEV, NDEV * bl, ncol)
    mod_mine = lax.dynamic_slice(mod_all, (0, me * bl, 0), (NDEV, bl, ncol))
    mods = mod_mine.transpose(1, 0, 2).reshape(bl, 6, D)
    mods = jnp.pad(mods, ((0, 0), (0, 2), (0, 0)))

    win_t, wout_b = _all_gather([w_in[0].T.astype(MXU_DTYPE), w_out[0].astype(MXU_DTYPE)], "gather_mixer_weights")
    ffn_shards = [ffn_w_up[0].T.astype(MXU_DTYPE), ffn_w_down[0].astype(MXU_DTYPE)]

    wr_bd = _block_diag(lru_w_r[0]).astype(MXU_DTYPE)
    wi_bd = _block_diag(lru_w_i[0]).astype(MXU_DTYPE)
    mavg = _block_diag(jnp.full((CW // HEAD, HEAD, HEAD), 1.0 / HEAD, F32)).astype(MXU_DTYPE)
    va = jnp.concatenate([lru_conv_w_f, lru_conv_b, lru_b_r, lru_b_i, lru_lambda], axis=0)
    vb = _pad_rows(jnp.concatenate([conv_w_f, conv_b, conv_norm_g, conv_norm_b], axis=0), 40)
    ln1 = _pad_rows(jnp.concatenate([ln1_g, ln1_b], axis=0), 8)
    ln2 = _pad_rows(jnp.concatenate([ln2_g, ln2_b], axis=0), 8)
    f3 = _pad_rows(jnp.concatenate([ffn_conv_w_f, ffn_conv_b], axis=0), 8)

    proj, h, vc, y_b, mixed, x1, u1_b, wup_t, wdown_b = _mixer_fwd(
        x, mods, win_t, wout_b, wr_bd, wi_bd, mavg, va, vb, ln1, ffn_shards)
    hh, z_b, u2_b, y2, loss_part = _ffn_fwd(x1, mods, loss_target, wup_t, wdown_b, f3, ln2)
    loss = lax.psum(jnp.sum(loss_part) * (0.5 / D), ("x", "y", "c"))

    dx1, dy2_b, dhh_b, dln2, df3, dmod_b = _ffn_bwd(x1, y2, loss_target, hh, mods, wup_t, wdown_b, f3, ln2)
    g_down = _weight_grad(z_b.reshape(ntok, DFF), dy2_b.reshape(ntok, D), FF_CHUNK, "wgrad_down")
    g_up_t = _weight_grad(dhh_b.reshape(ntok, 2 * DFF), u2_b.reshape(ntok, D), FF_CHUNK, "wgrad_up")
    grad_x, dproj_b, dm_b, g_a, g_w31, g_wr, g_wi, dln1, dmod_a, s_up, s_down = _mixer_bwd(
        x, dx1, proj, h, vc, mixed, mods, win_t, wout_b, wr_bd, wi_bd, mavg, va, vb, ln1, [g_up_t, g_down])
    g_out = _weight_grad(y_b.reshape(ntok, D), dm_b.reshape(ntok, D), 512, "wgrad_out")
    g_in_t = _weight_grad(dproj_b.reshape(ntok, 2 * LW + 2 * CW), u1_b.reshape(ntok, D), 512, "wgrad_in")

    s_in, s_out = _exchange_blocks([g_in_t, g_out], "exchange_wgrads")
    gs_in_t, gs_out = _sum_slots(s_in, "sum_w_in"), _sum_slots(s_out, "sum_w_out")
    gs_up_t, gs_down = _sum_slots(s_up, "sum_w_up"), _sum_slots(s_down, "sum_w_down")

    dmod = jnp.concatenate([dmod_a[:, 0:3, :], dmod_b[:, 0:3, :]], axis=1).reshape(bl, 6 * D)
    sharded_small = {"lru_conv_w": (1, LRU_K, LW), "conv_w": (1, CONV_K, CW), "ffn_conv_w": (1, FFN_K, DFF)}
    parts = {
        "lru_conv_w": g_a[0:4], "lru_conv_b": g_a[4:5], "lru_w_r": _diag_blocks(g_wr, LW // HEAD), "lru_b_r": g_a[5:6],
        "lru_w_i": _diag_blocks(g_wi, LW // HEAD), "lru_b_i": g_a[6:7], "lru_lambda": g_a[7:8], "conv_w": g_w31[0:CONV_K],
        "conv_b": g_a[10:11], "conv_norm_g": g_a[8:9], "conv_norm_b": g_a[9:10], "ln1_g": dln1[0:1], "ln1_b": dln1[1:2],
        "ffn_conv_w": df3[0:FFN_K], "ffn_conv_b": df3[FFN_K:FFN_K + 1], "ln2_g": dln2[0:1], "ln2_b": dln2[1:2]}
    part_names = list(parts)
    part_shapes = [sharded_small.get(k, weights[k].shape) for k in part_names]
    packed = _pack([dmod] + [parts[k] for k in part_names])
    n_rows = packed.shape[0]
    (packed_all,) = _all_gather([packed], "gather_small_grads")
    packed_all = packed_all.reshape(NDEV, n_rows, D)
    summed = _sum_slots(packed_all, "sum_small_grads")
    full = dict(zip(part_names, _unpack(summed[6 * bl:], part_shapes)))
    dmod_all = packed_all[:, 0:6 * bl, :].reshape(NDEV * bl, 6 * D)

    grads = {}
    g_w_ada, g_b_ada = _ada_bwd(c_all, dmod_all, lax.dynamic_slice(dmod_all, (0, me * ncol), (NDEV * bl, ncol)))
    grads["w_ada"], grads["b_ada"] = g_w_ada[None], g_b_ada
    grads["w_in"] = gs_in_t.T[None]
    grads["w_out"] = gs_out[None]
    grads["ffn_w_up"] = gs_up_t.T[None]
    grads["ffn_w_down"] = gs_down[None]
    for k in part_names:
        gk = full[k]
        if k in sharded_small:
            nloc = gk.shape[2] // NDEV
            gk = lax.dynamic_slice(gk, (0, 0, me * nloc), (1, gk.shape[1], nloc))
        grads[k] = gk

    delta, new_m, new_v = {}, {}, {}
    big = ("w_ada", "w_in", "w_out", "ffn_w_up", "ffn_w_down")
    for k in big:
        d_, m_, v_ = _adamw(weights[k][0], grads[k][0], mom_m[k][0], mom_v[k][0], "adamw_" + k)
        delta[k], new_m[k], new_v[k] = d_[None], m_[None], v_[None]
    small_names = [k for k in names if k not in big]
    d_, m_, v_ = _adamw_many([weights[k] for k in small_names], [grads[k] for k in small_names],
                             [mom_m[k] for k in small_names], [mom_v[k] for k in small_names], "adamw_small")
    for k, dk, mk, vk in zip(small_names, d_, m_, v_):
        delta[k], new_m[k], new_v[k] = dk, mk, vk

    return (loss, grad_x, *[grads[k] for k in names], *[delta[k] for k in names], *[new_m[k] for k in names],
            *[new_v[k] for k in names])
```

```python
import functools
import math

import jax
import jax.numpy as jnp
from jax import lax
from jax.experimental import pallas as pl
from jax.experimental.pallas import tpu as pltpu

NDEV = 8
D = 1024
LW = 512
CW = 512
HEAD = 64
DFF = 2816
FF_CHUNK = 1408
LRU_K = 4
CONV_K = 31
FFN_K = 3
LRU_C = 8.0
ALPHA = (2 * 1) ** 0.25
LN_EPS = 1e-5
ADAM_LR = 0.001
ADAM_B1 = 0.9
ADAM_B2 = 0.999
ADAM_EPS = 1e-08
ADAM_WD = 0.01
ADAM_STEP = 10

MXU_DTYPE = jnp.bfloat16
TOK_TILE = 256
WG_TOK_TILE = 512
VMEM_LIMIT = 60 * 1024 * 1024
HALO31 = 32
HALO = 8

F32 = jnp.float32
MESH = pl.DeviceIdType.MESH


def _sigmoid(x):
    return 0.5 * jnp.tanh(0.5 * x) + 0.5


def _shifted_rows(ext_ref, rot_ref, n_rows):
    span = rot_ref.shape[1]
    for r in range(1, 8):
        rot_ref[r - 1] = ext_ref[pl.ds(r, span), :]

    def window(o):
        q, r = divmod(o, 8)
        if r == 0:
            return ext_ref[8 * q:8 * q + n_rows, :]
        return rot_ref[r - 1, 8 * q:8 * q + n_rows, :]

    return window


def _dot(a, b):
    return jnp.dot(a, b, preferred_element_type=F32)


def _dot_nt(a, b):
    return lax.dot_general(a, b, (((1,), (1,)), ((), ())), preferred_element_type=F32)


def _dot_tn(a, b):
    return lax.dot_general(a, b, (((0,), (0,)), ((), ())), preferred_element_type=F32)


def _colsum(v):
    return jnp.sum(v, axis=0, keepdims=True)


def _rowmean(v):
    return jnp.mean(v, axis=-1, keepdims=True)


def _head_mean(v, mavg):
    hi = v.astype(MXU_DTYPE)
    lo = (v - hi.astype(F32)).astype(MXU_DTYPE)
    return _dot(hi, mavg) + _dot(lo, mavg)


_GELU_C0 = math.sqrt(2.0 / math.pi)
_GELU_C1 = 0.044715


def _gelu_and_grad(x):
    x2 = x * x
    th = jnp.tanh(_GELU_C0 * (x + _GELU_C1 * x * x2))
    ge = 0.5 * x * (1.0 + th)
    dge = 0.5 * (1.0 + th) + 0.5 * x * (1.0 - th * th) * (_GELU_C0 * (1.0 + 3.0 * _GELU_C1 * x2))
    return ge, dge


def _softplus(x):
    return jnp.maximum(x, 0.0) + jnp.log1p(jnp.exp(-jnp.abs(x)))


def _layer_norm_stats(r):
    mu = _rowmean(r)
    dl = r - mu
    var = _rowmean(dl * dl)
    rstd = lax.rsqrt(var + LN_EPS)
    return dl * rstd, rstd


def _scan_fwd(a, u, rows):
    n = a.shape[0]
    d = 1
    while d < n:
        keep = rows >= d
        a_s = jnp.where(keep, pltpu.roll(a, d, 0), 1.0)
        u_s = jnp.where(keep, pltpu.roll(u, d, 0), 0.0)
        u = a * u_s + u
        a = a * a_s
        d *= 2
    return u, a


def _scan_bwd(bm, g, rows):
    n = bm.shape[0]
    d = 1
    while d < n:
        keep = rows < n - d
        b_s = jnp.where(keep, pltpu.roll(bm, n - d, 0), 1.0)
        g_s = jnp.where(keep, pltpu.roll(g, n - d, 0), 0.0)
        g = g + bm * g_s
        bm = bm * b_s
        d *= 2
    return g, bm


def _lru_gates(xc, wr, wi, va_ref):
    xcb = xc.astype(MXU_DTYPE)
    r = _sigmoid(_dot(xcb, wr) + va_ref[5:6, :])
    ig = _sigmoid(_dot(xcb, wi) + va_ref[6:7, :])
    sp = _softplus(-va_ref[7:8, :])
    la = (-LRU_C) * r * sp
    a = jnp.exp(la)
    a2 = a * a
    mult = jnp.sqrt(-jnp.tanh(la) * (a2 + 1.0))
    return xcb, r, ig, sp, a, a2, mult


def _full(shape):
    nd = len(shape)
    return pl.BlockSpec(shape, lambda *_: (0,) * nd)


_ANY = pl.BlockSpec(memory_space=pl.ANY)


def _params(n_grid):
    return pltpu.CompilerParams(dimension_semantics=("arbitrary",) * n_grid, vmem_limit_bytes=VMEM_LIMIT)


def _my_place():
    return lax.axis_index("x"), lax.axis_index("y"), lax.axis_index("c")


def _all_gather(arrays, name):
    n_arr = len(arrays)

    def body(*refs):
        start, forward, finish = _gather_steps(refs[:n_arr], refs[n_arr:2 * n_arr], *refs[2 * n_arr:])
        start()
        forward()
        finish()

    return pl.pallas_call(
        body, name=name, out_shape=_gather_out_shapes(arrays),
        in_specs=[_ANY] * n_arr, out_specs=[_ANY] * n_arr, scratch_shapes=_comm_sems(n_arr),
    )(*arrays)


def _gather_out_shapes(arrays):
    return [jax.ShapeDtypeStruct((NDEV * a.shape[0], a.shape[1]), a.dtype) for a in arrays]


def _comm_sems(n_arr):
    return [pltpu.SemaphoreType.DMA((n_arr, 7)), pltpu.SemaphoreType.DMA((n_arr, 7)), pltpu.SemaphoreType.DMA((n_arr,))]


def _gather_steps(x_refs, out_refs, send_sems, recv_sems, local_sems):
    n_arr = len(x_refs)
    x, y, c = _my_place()
    me, sibling = (x, y, c), (x, y, 1 - c)
    chips = [(1 - x, y), (x, 1 - y), (1 - x, 1 - y)]

    def rows(k, px, py, pc):
        m = x_refs[k].shape[0]
        return out_refs[k].at[pl.ds((4 * px + 2 * py + pc) * m, m), :]

    def copy(k, s, block, to, src=None):
        return pltpu.make_async_remote_copy(
            src_ref=rows(k, *block) if src is None else src, dst_ref=rows(k, *block),
            send_sem=send_sems.at[k, s], recv_sem=recv_sems.at[k, s], device_id=to, device_id_type=MESH)

    def mine():
        return [pltpu.make_async_copy(x_refs[k], rows(k, *me), local_sems.at[k]) for k in range(n_arr)]

    def first():
        cps = []
        for k in range(n_arr):
            cps.append(copy(k, 0, me, sibling, src=x_refs[k]))
            cps += [copy(k, 1 + j, me, (*chip, c), src=x_refs[k]) for j, chip in enumerate(chips)]
        return cps

    def passed():
        return [copy(k, 4 + j, (*chip, c), sibling) for k in range(n_arr) for j, chip in enumerate(chips)]

    def start():
        for cp in mine() + first():
            cp.start()

    def forward():
        fwd = passed()
        for k in range(n_arr):
            for j, chip in enumerate(chips):
                copy(k, 1 + j, (*chip, c), me).wait_recv()
                fwd[3 * k + j].start()

    def finish():
        for k in range(n_arr):
            copy(k, 0, sibling, me).wait_recv()
            for j, chip in enumerate(chips):
                copy(k, 4 + j, (*chip, 1 - c), me).wait_recv()
        for cp in first() + passed():
            cp.wait_send()
        for cp in mine():
            cp.wait()

    return start, forward, finish


def _exchange_blocks(arrays, name):
    n_arr = len(arrays)

    def body(*refs):
        start, finish = _exchange_steps(refs[:n_arr], refs[n_arr:2 * n_arr], *refs[2 * n_arr:])
        start()
        finish()

    return pl.pallas_call(
        body, name=name, out_shape=_exchange_out_shapes(arrays),
        in_specs=[_ANY] * n_arr, out_specs=[_ANY] * n_arr, scratch_shapes=_comm_sems(n_arr),
    )(*arrays)


def _exchange_out_shapes(arrays):
    return [jax.ShapeDtypeStruct((NDEV, a.shape[0] // NDEV, a.shape[1]), a.dtype) for a in arrays]


def _exchange_steps(g_refs, out_refs, send_sems, recv_sems, local_sems):
    n_arr = len(g_refs)
    x, y, c = _my_place()
    me = 4 * x + 2 * y + c

    def copy(k, rel):
        px, py, pc = x ^ ((rel >> 2) & 1), y ^ ((rel >> 1) & 1), c ^ (rel & 1)
        m = out_refs[k].shape[1]
        return pltpu.make_async_remote_copy(
            src_ref=g_refs[k].at[pl.ds((4 * px + 2 * py + pc) * m, m), :], dst_ref=out_refs[k].at[me],
            send_sem=send_sems.at[k, rel - 1], recv_sem=recv_sems.at[k, rel - 1],
            device_id=(px, py, pc), device_id_type=MESH)

    def copies():
        cps = []
        for k in range(n_arr):
            m = out_refs[k].shape[1]
            cps.append(pltpu.make_async_copy(g_refs[k].at[pl.ds(me * m, m), :], out_refs[k].at[me], local_sems.at[k]))
            cps += [copy(k, rel) for rel in range(1, NDEV)]
        return cps

    def start():
        for cp in copies():
            cp.start()

    def finish():
        for cp in copies():
            cp.wait()

    return start, finish


def _ada_fwd(c_all, w_ada_loc, b_ada_loc):
    def body(c_ref, w_ref, b_ref, o_ref):
        cv = c_ref[...]
        ca = (cv * _sigmoid(cv)).astype(MXU_DTYPE)
        o_ref[...] = _dot(ca, w_ref[...].astype(MXU_DTYPE)) + b_ref[...]

    return pl.pallas_call(
        body, name="ada_fwd", out_shape=jax.ShapeDtypeStruct((c_all.shape[0], w_ada_loc.shape[1]), F32),
        in_specs=[_full(c_all.shape), _full(w_ada_loc.shape), _full(b_ada_loc.shape)],
        out_specs=_full((c_all.shape[0], w_ada_loc.shape[1])),
        compiler_params=pltpu.CompilerParams(vmem_limit_bytes=VMEM_LIMIT),
    )(c_all, w_ada_loc, b_ada_loc)


def _ada_bwd(c_all, dmod_all, dmod_cols):
    def body(c_ref, da_ref, d_ref, o_ref, b_ref):
        cv = c_ref[...]
        ca = (cv * _sigmoid(cv)).astype(MXU_DTYPE)
        o_ref[...] = _dot_tn(ca, d_ref[...].astype(MXU_DTYPE))
        b_ref[...] = _colsum(da_ref[...])

    return pl.pallas_call(
        body, name="ada_bwd",
        out_shape=[jax.ShapeDtypeStruct((c_all.shape[1], dmod_cols.shape[1]), F32),
                   jax.ShapeDtypeStruct((1, dmod_all.shape[1]), F32)],
        in_specs=[_full(c_all.shape), _full(dmod_all.shape), _full(dmod_cols.shape)],
        out_specs=[_full((c_all.shape[1], dmod_cols.shape[1])), _full((1, dmod_all.shape[1]))],
        compiler_params=pltpu.CompilerParams(vmem_limit_bytes=VMEM_LIMIT),
    )(c_all, dmod_all, dmod_cols)


def _mixer_fwd(x, mods, win_t, wout, wr, wi, mavg, va, vb, ln1, ffn_shards):
    bl, seq, _ = x.shape
    t = min(TOK_TILE, seq)
    nt = seq // t
    n_g = len(ffn_shards)

    def body(x_ref, mod_ref, win_hbm, wout_hbm, wr_ref, wi_ref, mavg_ref, va_ref, vb_ref, ln1_ref, *rest):
        shard_refs, rest = rest[:n_g], rest[n_g:]
        proj_ref, h_ref, vc_ref, y_ref, mixed_ref, x1_ref, u1_ref = rest[:7]
        gathered_refs, rest = rest[7:7 + n_g], rest[7 + n_g:]
        win_v, wout_v, xa_ext, vg_ext, rot_a, hcar, sems, g_send, g_recv, g_local = rest
        b, i = pl.program_id(0), pl.program_id(1)
        step = b * nt + i
        g_start, g_forward, g_finish = _gather_steps(shard_refs, gathered_refs, g_send, g_recv, g_local)

        @pl.when(step == 0)
        def _():
            g_start()
            cps = [pltpu.make_async_copy(win_hbm, win_v, sems.at[0]), pltpu.make_async_copy(wout_hbm, wout_v, sems.at[1])]
            for cp in cps:
                cp.start()
            for cp in cps:
                cp.wait()

        pl.when(step == (bl * nt) // 2)(g_forward)

        @pl.when(i == 0)
        def _():
            xa_ext[0:HALO, :] = jnp.zeros((HALO, LW), F32)
            vg_ext[0:HALO31, :] = jnp.zeros((HALO31, CW), F32)
            hcar[...] = jnp.zeros_like(hcar)

        rows = lax.broadcasted_iota(jnp.int32, (t, LW), 0)
        xv = x_ref[...]
        sh, sc, gt = mod_ref[0:1, :], mod_ref[1:2, :], mod_ref[2:3, :]
        ub = (xv * (1.0 + sc) + sh).astype(MXU_DTYPE)
        u1_ref[...] = ub
        proj = _dot_nt(ub, win_v[...])
        proj_ref[...] = proj
        xa, ga = proj[:, 0:LW], proj[:, LW:2 * LW]
        vbr, gb = proj[:, 2 * LW:2 * LW + CW], proj[:, 2 * LW + CW:]

        xa_ext[HALO:HALO + t, :] = xa
        xc = va_ref[4:5, :]
        for k in range(LRU_K):
            xc = xc + va_ref[k:k + 1, :] * xa_ext[pl.ds(HALO - (LRU_K - 1) + k, t), :]
        xa_ext[0:HALO, :] = xa_ext[t:t + HALO, :]
        _, _, ig, _, a, _, mult = _lru_gates(xc, wr_ref[...], wi_ref[...], va_ref)
        h_loc, a_cum = _scan_fwd(a, mult * (ig * xc), rows)
        h = h_loc + a_cum * hcar[0:1, :]
        hcar[0:1, :] = h[t - 1:t, :]
        h_ref[...] = h
        ge, _ = _gelu_and_grad(ga)
        ya = ge * h

        vg_ext[HALO31:HALO31 + t, :] = vbr * _sigmoid(gb)
        vg_win = _shifted_rows(vg_ext, rot_a, t)
        vc = vb_ref[CONV_K:CONV_K + 1, :]
        for k in range(CONV_K):
            vc = vc + vb_ref[k:k + 1, :] * vg_win(HALO31 - (CONV_K - 1) + k)
        vg_ext[0:HALO31, :] = vg_ext[t:t + HALO31, :]
        vc_ref[...] = vc
        mavg_v = mavg_ref[...]
        dl = vc - _head_mean(vc, mavg_v)
        rs = lax.rsqrt(_head_mean(dl * dl, mavg_v) + LN_EPS)
        yl = dl * rs * vb_ref[CONV_K + 1:CONV_K + 2, :] + vb_ref[CONV_K + 2:CONV_K + 3, :]
        yb = yl * _sigmoid(yl)

        yv = jnp.concatenate([ya, yb], axis=1).astype(MXU_DTYPE)
        y_ref[...] = yv
        mixed = _dot(yv, wout_v[...])
        mixed_ref[...] = mixed
        xh, _ = _layer_norm_stats(ALPHA * xv + (1.0 + gt) * mixed)
        x1_ref[...] = xh * ln1_ref[0:1, :] + ln1_ref[1:2, :]

        pl.when(step == bl * nt - 1)(g_finish)

    tok = lambda w: pl.BlockSpec((None, t, w), lambda b, i: (b, i, 0))
    outs = [(2 * LW + 2 * CW, F32), (LW, F32), (CW, F32), (D, MXU_DTYPE), (D, F32), (D, F32), (D, MXU_DTYPE)]
    return pl.pallas_call(
        body, name="mixer_fwd", grid=(bl, nt),
        out_shape=[jax.ShapeDtypeStruct((bl, seq, w), dt) for w, dt in outs] + _gather_out_shapes(ffn_shards),
        in_specs=[tok(D), pl.BlockSpec((None, 8, D), lambda b, i: (b, 0, 0)), _ANY, _ANY,
                  _full(wr.shape), _full(wi.shape), _full(mavg.shape), _full(va.shape), _full(vb.shape), _full(ln1.shape)]
        + [_ANY] * n_g,
        out_specs=[tok(w) for w, _ in outs] + [_ANY] * n_g,
        scratch_shapes=[pltpu.VMEM(win_t.shape, MXU_DTYPE), pltpu.VMEM(wout.shape, MXU_DTYPE),
                        pltpu.VMEM((t + HALO, LW), F32), pltpu.VMEM((t + HALO31, CW), F32),
                        pltpu.VMEM((7, t + HALO31 - 8, CW), F32), pltpu.VMEM((8, LW), F32),
                        pltpu.SemaphoreType.DMA((2,))] + _comm_sems(n_g),
        compiler_params=_params(2),
    )(x, mods, win_t, wout, wr, wi, mavg, va, vb, ln1, *ffn_shards)


def _ffn_fwd(x1, mods, tgt, wup_t, wdown, f3, ln2):
    bl, seq, _ = x1.shape
    t = min(TOK_TILE, seq)
    nt = seq // t
    n_chunk = DFF // FF_CHUNK

    def body(x1_ref, mod_ref, tgt_ref, wup_hbm, wdown_hbm, f3_ref, ln2_ref,
             hh_ref, z_ref, u2_ref, y2_ref, loss_ref,
             wup_v, wdown_v, g_ext, sems):
        b, i = pl.program_id(0), pl.program_id(1)

        @pl.when((b == 0) & (i == 0))
        def _():
            cps = [pltpu.make_async_copy(wup_hbm, wup_v, sems.at[0]), pltpu.make_async_copy(wdown_hbm, wdown_v, sems.at[1])]
            for cp in cps:
                cp.start()
            for cp in cps:
                cp.wait()
            loss_ref[...] = jnp.zeros_like(loss_ref)

        @pl.when(i == 0)
        def _():
            for ch in range(n_chunk):
                g_ext[ch, 0:HALO, :] = jnp.zeros((HALO, FF_CHUNK), F32)

        x1v = x1_ref[...]
        sh, sc, gt = mod_ref[3:4, :], mod_ref[4:5, :], mod_ref[5:6, :]
        ub = (x1v * (1.0 + sc) + sh).astype(MXU_DTYPE)
        u2_ref[...] = ub
        y2 = jnp.zeros((t, D), F32)
        for ch in range(n_chunk):
            lo = ch * FF_CHUNK
            v = _dot_nt(ub, wup_v[lo:lo + FF_CHUNK, :])
            g = _dot_nt(ub, wup_v[DFF + lo:DFF + lo + FF_CHUNK, :])
            hh_ref[:, lo:lo + FF_CHUNK] = v
            hh_ref[:, DFF + lo:DFF + lo + FF_CHUNK] = g
            g_ext[ch, HALO:HALO + t, :] = g
            gc = f3_ref[FFN_K:FFN_K + 1, lo:lo + FF_CHUNK]
            for k in range(FFN_K):
                gc = gc + f3_ref[k:k + 1, lo:lo + FF_CHUNK] * g_ext[ch, pl.ds(HALO - (FFN_K - 1) + k, t), :]
            g_ext[ch, 0:HALO, :] = g_ext[ch, t:t + HALO, :]
            zb = (gc * _sigmoid(gc) * v).astype(MXU_DTYPE)
            z_ref[:, lo:lo + FF_CHUNK] = zb
            y2 = y2 + _dot(zb, wdown_v[lo:lo + FF_CHUNK, :])
        y2_ref[...] = y2
        xh, _ = _layer_norm_stats(ALPHA * x1v + (1.0 + gt) * y2)
        err = xh * ln2_ref[0:1, :] + ln2_ref[1:2, :] - tgt_ref[...]
        e2 = (err * err).reshape(t // 8, 8, D).sum(axis=0)
        part = e2[:, 0:128]
        for j in range(1, D // 128):
            part = part + e2[:, 128 * j:128 * (j + 1)]
        loss_ref[...] += part

    tok = lambda w: pl.BlockSpec((None, t, w), lambda b, i: (b, i, 0))
    outs = [(2 * DFF, F32), (DFF, MXU_DTYPE), (D, MXU_DTYPE), (D, F32)]
    return pl.pallas_call(
        body, name="ffn_fwd", grid=(bl, nt),
        out_shape=[jax.ShapeDtypeStruct((bl, seq, w), dt) for w, dt in outs] + [jax.ShapeDtypeStruct((8, 128), F32)],
        in_specs=[tok(D), pl.BlockSpec((None, 8, D), lambda b, i: (b, 0, 0)), tok(D), _ANY, _ANY,
                  _full(f3.shape), _full(ln2.shape)],
        out_specs=[tok(w) for w, _ in outs] + [_full((8, 128))],
        scratch_shapes=[pltpu.VMEM(wup_t.shape, MXU_DTYPE), pltpu.VMEM(wdown.shape, MXU_DTYPE),
                        pltpu.VMEM((n_chunk, t + HALO, FF_CHUNK), F32), pltpu.SemaphoreType.DMA((2,))],
        compiler_params=_params(2),
    )(x1, mods, tgt, wup_t, wdown, f3, ln2)


def _ffn_bwd(x1, y2, tgt, hh, mods, wup_t, wdown, f3, ln2):
    bl, seq, _ = x1.shape
    t = min(TOK_TILE, seq)
    nt = seq // t
    n_chunk = DFF // FF_CHUNK

    def body(x1_ref, y2_ref, tgt_ref, hh_ref, halo_ref, mod_ref, wup_hbm, wdown_hbm, f3_ref, ln2_ref,
             dx1_ref, dy2_ref, dhh_ref, dln2_ref, df3_ref, dmod_ref,
             wup_v, wdown_v, g_ext, dgc_ext, sems):
        b, i = pl.program_id(0), pl.program_id(1)
        tt = nt - 1 - i

        @pl.when((b == 0) & (i == 0))
        def _():
            cps = [pltpu.make_async_copy(wup_hbm, wup_v, sems.at[0]), pltpu.make_async_copy(wdown_hbm, wdown_v, sems.at[1])]
            for cp in cps:
                cp.start()
            for cp in cps:
                cp.wait()
            dln2_ref[...] = jnp.zeros_like(dln2_ref)
            df3_ref[...] = jnp.zeros_like(df3_ref)

        @pl.when(i == 0)
        def _():
            dmod_ref[...] = jnp.zeros_like(dmod_ref)
            for ch in range(n_chunk):
                dgc_ext[ch, t:t + HALO, :] = jnp.zeros((HALO, FF_CHUNK), F32)

        x1v, y2v = x1_ref[...], y2_ref[...]
        sc, gt = mod_ref[4:5, :], mod_ref[5:6, :]
        xh, rstd = _layer_norm_stats(ALPHA * x1v + (1.0 + gt) * y2v)
        g2 = ln2_ref[0:1, :]
        dx2 = (xh * g2 + ln2_ref[1:2, :] - tgt_ref[...]) * (1.0 / D)
        dln2_ref[0:1, :] += _colsum(dx2 * xh)
        dln2_ref[1:2, :] += _colsum(dx2)
        dxh = dx2 * g2
        dr2 = rstd * (dxh - _rowmean(dxh) - xh * _rowmean(dxh * xh))
        dmod_ref[2:3, :] += _colsum(dr2 * y2v)
        dyb = ((1.0 + gt) * dr2).astype(MXU_DTYPE)
        dy2_ref[...] = dyb

        halo_keep = (tt > 0).astype(F32)
        du2 = jnp.zeros((t, D), F32)
        for ch in range(n_chunk):
            lo = ch * FF_CHUNK
            dz = _dot_nt(dyb, wdown_v[lo:lo + FF_CHUNK, :])
            v = hh_ref[:, lo:lo + FF_CHUNK]
            g_ext[ch, 0:HALO, :] = halo_ref[:, lo:lo + FF_CHUNK] * halo_keep
            g_ext[ch, HALO:HALO + t, :] = hh_ref[:, DFF + lo:DFF + lo + FF_CHUNK]
            g_taps = [g_ext[ch, pl.ds(HALO - (FFN_K - 1) + k, t), :] for k in range(FFN_K)]
            gc = f3_ref[FFN_K:FFN_K + 1, lo:lo + FF_CHUNK]
            for k in range(FFN_K):
                gc = gc + f3_ref[k:k + 1, lo:lo + FF_CHUNK] * g_taps[k]
            s = _sigmoid(gc)
            dv = dz * (gc * s)
            dgc = dz * v * (s * (1.0 + gc * (1.0 - s)))
            df3_ref[FFN_K:FFN_K + 1, lo:lo + FF_CHUNK] += _colsum(dgc)
            for k in range(FFN_K):
                df3_ref[k:k + 1, lo:lo + FF_CHUNK] += _colsum(dgc * g_taps[k])
            dgc_ext[ch, 0:t, :] = dgc
            dg = jnp.zeros((t, FF_CHUNK), F32)
            for k in range(FFN_K):
                dg = dg + f3_ref[k:k + 1, lo:lo + FF_CHUNK] * dgc_ext[ch, pl.ds(FFN_K - 1 - k, t), :]
            dgc_ext[ch, t:t + HALO, :] = dgc[0:HALO, :]
            dvb, dgb = dv.astype(MXU_DTYPE), dg.astype(MXU_DTYPE)
            dhh_ref[:, lo:lo + FF_CHUNK] = dvb
            dhh_ref[:, DFF + lo:DFF + lo + FF_CHUNK] = dgb
            du2 = du2 + _dot(dvb, wup_v[lo:lo + FF_CHUNK, :]) + _dot(dgb, wup_v[DFF + lo:DFF + lo + FF_CHUNK, :])
        dx1_ref[...] = ALPHA * dr2 + du2 * (1.0 + sc)
        dmod_ref[1:2, :] += _colsum(du2 * x1v)
        dmod_ref[0:1, :] += _colsum(du2)

    rev = lambda w: pl.BlockSpec((None, t, w), lambda b, i: (b, nt - 1 - i, 0))
    halo = pl.BlockSpec((None, HALO, DFF), lambda b, i: (b, jnp.maximum((nt - 1 - i) * (t // HALO) - 1, 0), 1))
    return pl.pallas_call(
        body, name="ffn_bwd", grid=(bl, nt),
        out_shape=[jax.ShapeDtypeStruct((bl, seq, D), F32), jax.ShapeDtypeStruct((bl, seq, D), MXU_DTYPE),
                   jax.ShapeDtypeStruct((bl, seq, 2 * DFF), MXU_DTYPE), jax.ShapeDtypeStruct((8, D), F32),
                   jax.ShapeDtypeStruct((8, DFF), F32), jax.ShapeDtypeStruct((bl, 8, D), F32)],
        in_specs=[rev(D), rev(D), rev(D), rev(2 * DFF), halo, pl.BlockSpec((None, 8, D), lambda b, i: (b, 0, 0)),
                  _ANY, _ANY, _full(f3.shape), _full(ln2.shape)],
        out_specs=[rev(D), rev(D), rev(2 * DFF), _full((8, D)), _full((8, DFF)),
                   pl.BlockSpec((None, 8, D), lambda b, i: (b, 0, 0))],
        scratch_shapes=[pltpu.VMEM(wup_t.shape, MXU_DTYPE), pltpu.VMEM(wdown.shape, MXU_DTYPE),
                        pltpu.VMEM((n_chunk, t + HALO, FF_CHUNK), F32), pltpu.VMEM((n_chunk, t + HALO, FF_CHUNK), F32),
                        pltpu.SemaphoreType.DMA((2,))],
        compiler_params=_params(2),
    )(x1, y2, tgt, hh, hh, mods, wup_t, wdown, f3, ln2)


def _mixer_bwd(x, dx1, proj, h, vc, mixed, mods, win_t, wout, wr, wi, mavg, va, vb, ln1, ffn_wgrads):
    bl, seq, _ = x.shape
    t = min(TOK_TILE, seq)
    nt = seq // t
    pw = 2 * LW + 2 * CW
    n_g = len(ffn_wgrads)

    def body(x_ref, dx1_ref, proj_ref, phalo_ref, h_ref, hhalo_ref, vc_ref, mixed_ref, mod_ref,
             win_hbm, wout_hbm, wr_ref, wi_ref, mavg_ref, va_ref, vb_ref, ln1_ref, *rest):
        wgrad_refs, rest = rest[:n_g], rest[n_g:]
        gx_ref, dproj_ref, dm_ref, ga_ref, gw31_ref, dwr_ref, dwi_ref, dln1_ref, dmod_ref = rest[:9]
        slot_refs, rest = rest[9:9 + n_g], rest[9 + n_g:]
        win_v, wout_v, xa_ext, vg_ext, dvc_ext, dxc_ext, rot_a, rot_b, car, sems, e_send, e_recv, e_local = rest
        b, i = pl.program_id(0), pl.program_id(1)
        tt = nt - 1 - i
        e_start, e_finish = _exchange_steps(wgrad_refs, slot_refs, e_send, e_recv, e_local)

        @pl.when((b == 0) & (i == 0))
        def _():
            e_start()
            cps = [pltpu.make_async_copy(win_hbm, win_v, sems.at[0]), pltpu.make_async_copy(wout_hbm, wout_v, sems.at[1])]
            for cp in cps:
                cp.start()
            for cp in cps:
                cp.wait()
            for ref in (ga_ref, gw31_ref, dwr_ref, dwi_ref, dln1_ref):
                ref[...] = jnp.zeros_like(ref)

        @pl.when(i == 0)
        def _():
            dmod_ref[...] = jnp.zeros_like(dmod_ref)
            dvc_ext[t:t + HALO31, :] = jnp.zeros((HALO31, CW), F32)
            dxc_ext[t:t + HALO, :] = jnp.zeros((HALO, LW), F32)
            car[...] = jnp.zeros_like(car)

        rows = lax.broadcasted_iota(jnp.int32, (t, LW), 0)
        halo_keep = (tt > 0).astype(F32)
        xv, mixed = x_ref[...], mixed_ref[...]
        sc, gt = mod_ref[1:2, :], mod_ref[2:3, :]

        xh, rstd = _layer_norm_stats(ALPHA * xv + (1.0 + gt) * mixed)
        dx1 = dx1_ref[...]
        dln1_ref[0:1, :] += _colsum(dx1 * xh)
        dln1_ref[1:2, :] += _colsum(dx1)
        dxh = dx1 * ln1_ref[0:1, :]
        dr1 = rstd * (dxh - _rowmean(dxh) - xh * _rowmean(dxh * xh))
        dmod_ref[2:3, :] += _colsum(dr1 * mixed)
        dmb = ((1.0 + gt) * dr1).astype(MXU_DTYPE)
        dm_ref[...] = dmb
        dy = _dot_nt(dmb, wout_v[...])
        dya, dyb = dy[:, 0:LW], dy[:, LW:]

        proj = proj_ref[...]
        xa, ga = proj[:, 0:LW], proj[:, LW:2 * LW]
        vbr, gb = proj[:, 2 * LW:2 * LW + CW], proj[:, 2 * LW + CW:]
        phalo = phalo_ref[...] * halo_keep

        sgb = _sigmoid(gb)
        vg_ext[0:HALO31, :] = phalo[:, 2 * LW:2 * LW + CW] * _sigmoid(phalo[:, 2 * LW + CW:])
        vg_ext[HALO31:HALO31 + t, :] = vbr * sgb
        mavg_v = mavg_ref[...]
        vcv = vc_ref[...]
        dl = vcv - _head_mean(vcv, mavg_v)
        rs = lax.rsqrt(_head_mean(dl * dl, mavg_v) + LN_EPS)
        yn = dl * rs
        ng = vb_ref[CONV_K + 1:CONV_K + 2, :]
        yl = yn * ng + vb_ref[CONV_K + 2:CONV_K + 3, :]
        s = _sigmoid(yl)
        dyl = dyb * (s * (1.0 + yl * (1.0 - s)))
        ga_ref[8:9, :] += _colsum(dyl * yn)
        ga_ref[9:10, :] += _colsum(dyl)
        dyn = dyl * ng
        dvc = rs * (dyn - _head_mean(dyn, mavg_v) - yn * _head_mean(dyn * yn, mavg_v))
        ga_ref[10:11, :] += _colsum(dvc)
        dvc_ext[0:t, :] = dvc
        vg_win = _shifted_rows(vg_ext, rot_a, t)
        dvc_win = _shifted_rows(dvc_ext, rot_b, t)
        dvg = jnp.zeros((t, CW), F32)
        for k in range(CONV_K):
            gw31_ref[k:k + 1, :] += _colsum(dvc * vg_win(HALO31 - (CONV_K - 1) + k))
            dvg = dvg + vb_ref[k:k + 1, :] * dvc_win(CONV_K - 1 - k)
        dvc_ext[t:t + HALO31, :] = dvc[0:HALO31, :]
        dvb = dvg * sgb
        dgb = dvg * vbr * sgb * (1.0 - sgb)

        xa_ext[0:HALO, :] = phalo[HALO31 - HALO:HALO31, 0:LW]
        xa_ext[HALO:HALO + t, :] = xa
        xa_taps = [xa_ext[pl.ds(HALO - (LRU_K - 1) + k, t), :] for k in range(LRU_K)]
        xc = va_ref[4:5, :]
        for k in range(LRU_K):
            xc = xc + va_ref[k:k + 1, :] * xa_taps[k]
        wr_v, wi_v = wr_ref[...], wi_ref[...]
        xcb, r, ig, sp, a, a2, mult = _lru_gates(xc, wr_v, wi_v, va_ref)
        hv = h_ref[...]
        hprev = jnp.where(rows >= 1, pltpu.roll(hv, 1, 0), hhalo_ref[HALO - 1:HALO, :] * halo_keep)
        ge, dge = _gelu_and_grad(ga)
        dga = dya * hv * dge
        bm = jnp.where(rows < t - 1, pltpu.roll(a, t - 1, 0), car[0:1, :])
        g_loc, b_cum = _scan_bwd(bm, dya * ge, rows)
        gv = g_loc + b_cum * car[1:2, :]
        car[0:1, :] = a[0:1, :]
        car[1:2, :] = gv[0:1, :]
        da = gv * hprev
        dmult = gv * (ig * xc)
        di = gv * (mult * xc)
        dxc = gv * (mult * ig)
        dla = da * a - dmult * (a2 / mult)
        dr = dla * ((-LRU_C) * sp)
        lam = va_ref[7:8, :]
        ga_ref[7:8, :] += _colsum(dla * ((-LRU_C) * r)) * (-_sigmoid(-lam))
        dgr = dr * r * (1.0 - r)
        dgi = di * ig * (1.0 - ig)
        ga_ref[5:6, :] += _colsum(dgr)
        ga_ref[6:7, :] += _colsum(dgi)
        dgrb, dgib = dgr.astype(MXU_DTYPE), dgi.astype(MXU_DTYPE)
        dwr_ref[...] += _dot_tn(xcb, dgrb)
        dwi_ref[...] += _dot_tn(xcb, dgib)
        dxc = dxc + _dot_nt(dgrb, wr_v) + _dot_nt(dgib, wi_v)
        ga_ref[4:5, :] += _colsum(dxc)
        dxc_ext[0:t, :] = dxc
        dxa = jnp.zeros((t, LW), F32)
        for k in range(LRU_K):
            ga_ref[k:k + 1, :] += _colsum(dxc * xa_taps[k])
            dxa = dxa + va_ref[k:k + 1, :] * dxc_ext[pl.ds(LRU_K - 1 - k, t), :]
        dxc_ext[t:t + HALO, :] = dxc[0:HALO, :]

        dpb = jnp.concatenate([dxa, dga, dvb, dgb], axis=1).astype(MXU_DTYPE)
        dproj_ref[...] = dpb
        du1 = _dot(dpb, win_v[...])
        gx_ref[...] = ALPHA * dr1 + du1 * (1.0 + sc)
        dmod_ref[1:2, :] += _colsum(du1 * xv)
        dmod_ref[0:1, :] += _colsum(du1)

        pl.when((b == bl - 1) & (i == nt - 1))(e_finish)

    rev = lambda w: pl.BlockSpec((None, t, w), lambda b, i: (b, nt - 1 - i, 0))

    def halo(rows_, w):
        return pl.BlockSpec((None, rows_, w), lambda b, i: (b, jnp.maximum((nt - 1 - i) * (t // rows_) - 1, 0), 0))

    return pl.pallas_call(
        body, name="mixer_bwd", grid=(bl, nt),
        out_shape=[jax.ShapeDtypeStruct((bl, seq, D), F32), jax.ShapeDtypeStruct((bl, seq, pw), MXU_DTYPE),
                   jax.ShapeDtypeStruct((bl, seq, D), MXU_DTYPE), jax.ShapeDtypeStruct((16, LW), F32),
                   jax.ShapeDtypeStruct((32, CW), F32), jax.ShapeDtypeStruct((LW, LW), F32),
                   jax.ShapeDtypeStruct((LW, LW), F32), jax.ShapeDtypeStruct((8, D), F32),
                   jax.ShapeDtypeStruct((bl, 8, D), F32)] + _exchange_out_shapes(ffn_wgrads),
        in_specs=[rev(D), rev(D), rev(pw), halo(HALO31, pw), rev(LW), halo(HALO, LW), rev(CW), rev(D),
                  pl.BlockSpec((None, 8, D), lambda b, i: (b, 0, 0)), _ANY, _ANY,
                  _full(wr.shape), _full(wi.shape), _full(mavg.shape), _full(va.shape), _full(vb.shape), _full(ln1.shape)]
        + [_ANY] * n_g,
        out_specs=[rev(D), rev(pw), rev(D), _full((16, LW)), _full((32, CW)), _full((LW, LW)), _full((LW, LW)),
                   _full((8, D)), pl.BlockSpec((None, 8, D), lambda b, i: (b, 0, 0))] + [_ANY] * n_g,
        scratch_shapes=[pltpu.VMEM(win_t.shape, MXU_DTYPE), pltpu.VMEM(wout.shape, MXU_DTYPE),
                        pltpu.VMEM((t + HALO, LW), F32), pltpu.VMEM((t + HALO31, CW), F32),
                        pltpu.VMEM((t + HALO31, CW), F32), pltpu.VMEM((t + HALO, LW), F32),
                        pltpu.VMEM((7, t + HALO31 - 8, CW), F32), pltpu.VMEM((7, t + HALO31 - 8, CW), F32),
                        pltpu.VMEM((8, LW), F32), pltpu.SemaphoreType.DMA((2,))] + _comm_sems(n_g),
        compiler_params=_params(2),
    )(x, dx1, proj, proj, h, h, vc, mixed, mods, win_t, wout, wr, wi, mavg, va, vb, ln1, *ffn_wgrads)


def _weight_grad(a, bmat, tm, name):
    ntok, m = a.shape
    n = bmat.shape[1]
    tk = min(WG_TOK_TILE, ntok)
    nk = ntok // tk

    def body(a_ref, b_ref, o_ref):
        @pl.when(pl.program_id(1) == 0)
        def _():
            o_ref[...] = jnp.zeros_like(o_ref)

        o_ref[...] += _dot_tn(a_ref[...], b_ref[...])

    return pl.pallas_call(
        body, name=name, grid=(m // tm, nk), out_shape=jax.ShapeDtypeStruct((m, n), F32),
        in_specs=[pl.BlockSpec((tk, tm), lambda i, k: (k, i)), pl.BlockSpec((tk, n), lambda i, k: (k, 0))],
        out_specs=pl.BlockSpec((tm, n), lambda i, k: (i, 0)),
        compiler_params=_params(2),
    )(a, bmat)


def _sum_slots(slots, name):
    _, r, cdim = slots.shape
    tr = 32 if r % 32 == 0 else r

    def body(s_ref, o_ref):
        acc = s_ref[0]
        for j in range(1, NDEV):
            acc = acc + s_ref[j]
        o_ref[...] = acc

    return pl.pallas_call(
        body, name=name, grid=(r // tr,), out_shape=jax.ShapeDtypeStruct((r, cdim), F32),
        in_specs=[pl.BlockSpec((NDEV, tr, cdim), lambda i: (0, i, 0))],
        out_specs=pl.BlockSpec((tr, cdim), lambda i: (i, 0)),
        compiler_params=_params(1),
    )(slots)


def _adamw_update(w_ref, g_ref, m_ref, v_ref, d_ref, nm_ref, nv_ref):
    gv = g_ref[...]
    nm = ADAM_B1 * m_ref[...] + (1.0 - ADAM_B1) * gv
    nv = ADAM_B2 * v_ref[...] + (1.0 - ADAM_B2) * (gv * gv)
    m_hat = nm / (1.0 - ADAM_B1 ** ADAM_STEP)
    v_hat = nv / (1.0 - ADAM_B2 ** ADAM_STEP)
    d_ref[...] = -ADAM_LR * (m_hat / (jnp.sqrt(v_hat) + ADAM_EPS) + ADAM_WD * w_ref[...])
    nm_ref[...] = nm
    nv_ref[...] = nv


def _adamw_many(ws, gs, ms, vs, name):
    n = len(ws)

    def body(*refs):
        ins, outs = refs[:4 * n], refs[4 * n:]
        for k in range(n):
            _adamw_update(ins[k], ins[n + k], ins[2 * n + k], ins[3 * n + k], outs[k], outs[n + k], outs[2 * n + k])

    specs = [_full(w.shape) for w in ws]
    res = pl.pallas_call(
        body, name=name, out_shape=[jax.ShapeDtypeStruct(w.shape, F32) for w in ws] * 3,
        in_specs=specs * 4, out_specs=specs * 3,
        compiler_params=pltpu.CompilerParams(vmem_limit_bytes=VMEM_LIMIT),
    )(*ws, *gs, *ms, *vs)
    return res[:n], res[n:2 * n], res[2 * n:]


def _adamw(w, g, m, v, name):
    r, cdim = w.shape
    tr = 128 if r % 128 == 0 else r
    body = functools.partial(_adamw_update)
    spec = pl.BlockSpec((tr, cdim), lambda i: (i, 0))
    return pl.pallas_call(
        body, name=name, grid=(r // tr,), out_shape=[jax.ShapeDtypeStruct((r, cdim), F32)] * 3,
        in_specs=[spec] * 4, out_specs=[spec] * 3, compiler_params=_params(1),
    )(w, g, m, v)


def _pack(arrs, width=D, row_mult=8):
    parts = []
    for a in arrs:
        flat = a.reshape(-1)
        pad = (-flat.shape[0]) % width
        parts.append(jnp.pad(flat, (0, pad)))
    flat = jnp.concatenate(parts)
    pad = (-flat.shape[0]) % (width * row_mult)
    return jnp.pad(flat, (0, pad)).reshape(-1, width)


def _unpack(buf, shapes, width=D):
    out, row = [], 0
    for shp in shapes:
        size = math.prod(shp)
        nrow = -(-size // width)
        out.append(buf[row:row + nrow].reshape(-1)[:size].reshape(shp))
        row += nrow
    return out


def _unpack_gathered(buf, shapes, width=D):
    out, row = [], 0
    for shp in shapes:
        size = math.prod(shp)
        nrow = -(-size // width)
        out.append(buf[:, row:row + nrow].reshape(NDEV, -1)[:, :size].reshape((NDEV,) + tuple(shp)))
        row += nrow
    return out


def _block_diag(w):
    hn, dh, _ = w.shape
    eye = jnp.eye(hn, dtype=w.dtype)
    return (w[:, :, None, :] * eye[:, None, :, None]).reshape(hn * dh, hn * dh)


def _diag_blocks(wfull, hn):
    dh = wfull.shape[0] // hn
    return jnp.stack([wfull[k * dh:(k + 1) * dh, k * dh:(k + 1) * dh] for k in range(hn)])


def _ungather_cols(g, k):
    n = g.shape[1]
    return g.reshape(NDEV, k, n).transpose(1, 0, 2).reshape(k, NDEV * n)


def _pad_rows(a, rows):
    return jnp.pad(a, ((0, rows - a.shape[0]), (0, 0)))


def kernel(x, c, w_ada, b_ada, w_in, lru_conv_w, lru_conv_b, lru_w_r, lru_b_r, lru_w_i, lru_b_i, lru_lambda, conv_w, conv_b, conv_norm_g, conv_norm_b, w_out, ln1_g, ln1_b, ffn_w_up, ffn_conv_w, ffn_conv_b, ffn_w_down, ln2_g, ln2_b, loss_target, m_w_ada, m_b_ada, m_w_in, m_lru_conv_w, m_lru_conv_b, m_lru_w_r, m_lru_b_r, m_lru_w_i, m_lru_b_i, m_lru_lambda, m_conv_w, m_conv_b, m_conv_norm_g, m_conv_norm_b, m_w_out, m_ln1_g, m_ln1_b, m_ffn_w_up, m_ffn_conv_w, m_ffn_conv_b, m_ffn_w_down, m_ln2_g, m_ln2_b, v_w_ada, v_b_ada, v_w_in, v_lru_conv_w, v_lru_conv_b, v_lru_w_r, v_lru_b_r, v_lru_w_i, v_lru_b_i, v_lru_lambda, v_conv_w, v_conv_b, v_conv_norm_g, v_conv_norm_b, v_w_out, v_ln1_g, v_ln1_b, v_ffn_w_up, v_ffn_conv_w, v_ffn_conv_b, v_ffn_w_down, v_ln2_g, v_ln2_b):
    weights = dict(w_ada=w_ada, b_ada=b_ada, w_in=w_in, lru_conv_w=lru_conv_w, lru_conv_b=lru_conv_b, lru_w_r=lru_w_r,
                   lru_b_r=lru_b_r, lru_w_i=lru_w_i, lru_b_i=lru_b_i, lru_lambda=lru_lambda, conv_w=conv_w, conv_b=conv_b,
                   conv_norm_g=conv_norm_g, conv_norm_b=conv_norm_b, w_out=w_out, ln1_g=ln1_g, ln1_b=ln1_b,
                   ffn_w_up=ffn_w_up, ffn_conv_w=ffn_conv_w, ffn_conv_b=ffn_conv_b, ffn_w_down=ffn_w_down, ln2_g=ln2_g,
                   ln2_b=ln2_b)
    mom_m = dict(w_ada=m_w_ada, b_ada=m_b_ada, w_in=m_w_in, lru_conv_w=m_lru_conv_w, lru_conv_b=m_lru_conv_b,
                 lru_w_r=m_lru_w_r, lru_b_r=m_lru_b_r, lru_w_i=m_lru_w_i, lru_b_i=m_lru_b_i, lru_lambda=m_lru_lambda,
                 conv_w=m_conv_w, conv_b=m_conv_b, conv_norm_g=m_conv_norm_g, conv_norm_b=m_conv_norm_b, w_out=m_w_out,
                 ln1_g=m_ln1_g, ln1_b=m_ln1_b, ffn_w_up=m_ffn_w_up, ffn_conv_w=m_ffn_conv_w, ffn_conv_b=m_ffn_conv_b,
                 ffn_w_down=m_ffn_w_down, ln2_g=m_ln2_g, ln2_b=m_ln2_b)
    mom_v = dict(w_ada=v_w_ada, b_ada=v_b_ada, w_in=v_w_in, lru_conv_w=v_lru_conv_w, lru_conv_b=v_lru_conv_b,
                 lru_w_r=v_lru_w_r, lru_b_r=v_lru_b_r, lru_w_i=v_lru_w_i, lru_b_i=v_lru_b_i, lru_lambda=v_lru_lambda,
                 conv_w=v_conv_w, conv_b=v_conv_b, conv_norm_g=v_conv_norm_g, conv_norm_b=v_conv_norm_b, w_out=v_w_out,
                 ln1_g=v_ln1_g, ln1_b=v_ln1_b, ffn_w_up=v_ffn_w_up, ffn_conv_w=v_ffn_conv_w, ffn_conv_b=v_ffn_conv_b,
                 ffn_w_down=v_ffn_w_down, ln2_g=v_ln2_g, ln2_b=v_ln2_b)
    names = list(weights)
    bl, seq, _ = x.shape
    ntok = bl * seq
    me = 4 * lax.axis_index("x") + 2 * lax.axis_index("y") + lax.axis_index("c")

    small_shapes = [(bl, D), (LRU_K, LW // NDEV), (CONV_K, CW // NDEV), (FFN_K, DFF // NDEV)]
    small = _pack([c, lru_conv_w[0], conv_w[0], ffn_conv_w[0]], width=128)
    n_small = small.shape[0]
    (small_all,) = _all_gather([small], "gather_small")
    small_all = small_all.reshape(NDEV, n_small, 128)
    c_parts, k4_parts, k31_parts, k3_parts = _unpack_gathered(small_all, small_shapes, width=128)
    c_all = c_parts.reshape(NDEV * bl, D)
    lru_conv_w_f = k4_parts.transpose(1, 0, 2).reshape(LRU_K, LW)
    conv_w_f = k31_parts.transpose(1, 0, 2).reshape(CONV_K, CW)
    ffn_conv_w_f = k3_parts.transpose(1, 0, 2).reshape(FFN_K, DFF)

    ncol = w_ada.shape[2]
    b_ada_loc = lax.dynamic_slice(b_ada, (0, me * ncol), (1, ncol))
    mod_cols = _ada_fwd(c_all, w_ada[0], b_ada_loc)
    (mod_all,) = _all_gather([mod_cols], "gather_mod")
    mod_all = mod_all.reshape(NDEV, NDEV * bl, ncol)
    mod_mine = lax.dynamic_slice(mod_all, (0, me * bl, 0), (NDEV, bl, ncol))
    mods = mod_mine.transpose(1, 0, 2).reshape(bl, 6, D)
    mods = jnp.pad(mods, ((0, 0), (0, 2), (0, 0)))

    win_t, wout_b = _all_gather([w_in[0].T.astype(MXU_DTYPE), w_out[0].astype(MXU_DTYPE)], "gather_mixer_weights")
    ffn_shards = [ffn_w_up[0].T.astype(MXU_DTYPE), ffn_w_down[0].astype(MXU_DTYPE)]

    wr_bd = _block_diag(lru_w_r[0]).astype(MXU_DTYPE)
    wi_bd = _block_diag(lru_w_i[0]).astype(MXU_DTYPE)
    mavg = _block_diag(jnp.full((CW // HEAD, HEAD, HEAD), 1.0 / HEAD, F32)).astype(MXU_DTYPE)
    va = jnp.concatenate([lru_conv_w_f, lru_conv_b, lru_b_r, lru_b_i, lru_lambda], axis=0)
    vb = _pad_rows(jnp.concatenate([conv_w_f, conv_b, conv_norm_g, conv_norm_b], axis=0), 40)
    ln1 = _pad_rows(jnp.concatenate([ln1_g, ln1_b], axis=0), 8)
    ln2 = _pad_rows(jnp.concatenate([ln2_g, ln2_b], axis=0), 8)
    f3 = _pad_rows(jnp.concatenate([ffn_conv_w_f, ffn_conv_b], axis=0), 8)

    proj, h, vc, y_b, mixed, x1, u1_b, wup_t, wdown_b = _mixer_fwd(
        x, mods, win_t, wout_b, wr_bd, wi_bd, mavg, va, vb, ln1, ffn_shards)
    hh, z_b, u2_b, y2, loss_part = _ffn_fwd(x1, mods, loss_target, wup_t, wdown_b, f3, ln2)
    loss = lax.psum(jnp.sum(loss_part) * (0.5 / D), ("x", "y", "c"))

    dx1, dy2_b, dhh_b, dln2, df3, dmod_b = _ffn_bwd(x1, y2, loss_target, hh, mods, wup_t, wdown_b, f3, ln2)
    g_down = _weight_grad(z_b.reshape(ntok, DFF), dy2_b.reshape(ntok, D), FF_CHUNK, "wgrad_down")
    g_up_t = _weight_grad(dhh_b.reshape(ntok, 2 * DFF), u2_b.reshape(ntok, D), FF_CHUNK, "wgrad_up")
    grad_x, dproj_b, dm_b, g_a, g_w31, g_wr, g_wi, dln1, dmod_a, s_up, s_down = _mixer_bwd(
        x, dx1, proj, h, vc, mixed, mods, win_t, wout_b, wr_bd, wi_bd, mavg, va, vb, ln1, [g_up_t, g_down])
    g_out = _weight_grad(y_b.reshape(ntok, D), dm_b.reshape(ntok, D), 512, "wgrad_out")
    g_in_t = _weight_grad(dproj_b.reshape(ntok, 2 * LW + 2 * CW), u1_b.reshape(ntok, D), 512, "wgrad_in")

    s_in, s_out = _exchange_blocks([g_in_t, g_out], "exchange_wgrads")
    gs_in_t, gs_out = _sum_slots(s_in, "sum_w_in"), _sum_slots(s_out, "sum_w_out")
    gs_up_t, gs_down = _sum_slots(s_up, "sum_w_up"), _sum_slots(s_down, "sum_w_down")

    dmod = jnp.concatenate([dmod_a[:, 0:3, :], dmod_b[:, 0:3, :]], axis=1).reshape(bl, 6 * D)
    sharded_small = {"lru_conv_w": (1, LRU_K, LW), "conv_w": (1, CONV_K, CW), "ffn_conv_w": (1, FFN_K, DFF)}
    parts = {
        "lru_conv_w": g_a[0:4], "lru_conv_b": g_a[4:5], "lru_w_r": _diag_blocks(g_wr, LW // HEAD), "lru_b_r": g_a[5:6],
        "lru_w_i": _diag_blocks(g_wi, LW // HEAD), "lru_b_i": g_a[6:7], "lru_lambda": g_a[7:8], "conv_w": g_w31[0:CONV_K],
        "conv_b": g_a[10:11], "conv_norm_g": g_a[8:9], "conv_norm_b": g_a[9:10], "ln1_g": dln1[0:1], "ln1_b": dln1[1:2],
        "ffn_conv_w": df3[0:FFN_K], "ffn_conv_b": df3[FFN_K:FFN_K + 1], "ln2_g": dln2[0:1], "ln2_b": dln2[1:2]}
    part_names = list(parts)
    part_shapes = [sharded_small.get(k, weights[k].shape) for k in part_names]
    packed = _pack([dmod] + [parts[k] for k in part_names])
    n_rows = packed.shape[0]
    (packed_all,) = _all_gather([packed], "gather_small_grads")
    packed_all = packed_all.reshape(NDEV, n_rows, D)
    summed = _sum_slots(packed_all, "sum_small_grads")
    full = dict(zip(part_names, _unpack(summed[6 * bl:], part_shapes)))
    dmod_all = packed_all[:, 0:6 * bl, :].reshape(NDEV * bl, 6 * D)

    grads = {}
    g_w_ada, g_b_ada = _ada_bwd(c_all, dmod_all, lax.dynamic_slice(dmod_all, (0, me * ncol), (NDEV * bl, ncol)))
    grads["w_ada"], grads["b_ada"] = g_w_ada[None], g_b_ada
    grads["w_in"] = gs_in_t.T[None]
    grads["w_out"] = gs_out[None]
    grads["ffn_w_up"] = gs_up_t.T[None]
    grads["ffn_w_down"] = gs_down[None]
    for k in part_names:
        gk = full[k]
        if k in sharded_small:
            nloc = gk.shape[2] // NDEV
            gk = lax.dynamic_slice(gk, (0, 0, me * nloc), (1, gk.shape[1], nloc))
        grads[k] = gk

    delta, new_m, new_v = {}, {}, {}
    big = ("w_ada", "w_in", "w_out", "ffn_w_up", "ffn_w_down")
    for k in big:
        d_, m_, v_ = _adamw(weights[k][0], grads[k][0], mom_m[k][0], mom_v[k][0], "adamw_" + k)
        delta[k], new_m[k], new_v[k] = d_[None], m_[None], v_[None]
    small_names = [k for k in names if k not in big]
    d_, m_, v_ = _adamw_many([weights[k] for k in small_names], [grads[k] for k in small_names],
                             [mom_m[k] for k in small_names], [mom_v[k] for k in small_names], "adamw_small")
    for k, dk, mk, vk in zip(small_names, d_, m_, v_):
        delta[k], new_m[k], new_v[k] = dk, mk, vk

    return (loss, grad_x, *[grads[k] for k in names], *[delta[k] for k in names], *[new_m[k] for k in names],
            *[new_v[k] for k in names])
```

```python
import functools
import math

import jax
import jax.numpy as jnp
from jax import lax
from jax.experimental import pallas as pl
from jax.experimental.pallas import tpu as pltpu

NDEV = 8
D = 1024
LW = 512
CW = 512
HEAD = 64
DFF = 2816
FF_CHUNK = 1408
LRU_K = 4
CONV_K = 31
FFN_K = 3
LRU_C = 8.0
ALPHA = (2 * 1) ** 0.25
LN_EPS = 1e-5
ADAM_LR = 0.001
ADAM_B1 = 0.9
ADAM_B2 = 0.999
ADAM_EPS = 1e-08
ADAM_WD = 0.01
ADAM_STEP = 10

MXU_DTYPE = jnp.bfloat16
WIRE_DTYPE = jnp.bfloat16
TOK_TILE = 256
WG_TOK_TILE = 512
VMEM_LIMIT = 60 * 1024 * 1024
HALO31 = 32
HALO = 8

F32 = jnp.float32
MESH = pl.DeviceIdType.MESH


def _sigmoid(x):
    return 0.5 * jnp.tanh(0.5 * x) + 0.5


def _shifted_rows(ext_ref, rot_ref, n_rows):
    span = rot_ref.shape[1]
    for r in range(1, 8):
        rot_ref[r - 1] = ext_ref[pl.ds(r, span), :]

    def window(o, r0=0, cols=slice(None)):
        q, r = divmod(o, 8)
        if r == 0:
            return ext_ref[pl.ds(8 * q + r0, n_rows), cols]
        return rot_ref[r - 1, pl.ds(8 * q + r0, n_rows), cols]

    return window


def _store_hilo(hilo_ref, rows, c0, v):
    half = hilo_ref.shape[1] // 2
    hi = v.astype(hilo_ref.dtype)
    hilo_ref[rows, c0:c0 + v.shape[1]] = hi
    hilo_ref[rows, half + c0:half + c0 + v.shape[1]] = (v - hi.astype(F32)).astype(hilo_ref.dtype)


def _scan_bwd_blocks(a_ref, g_ref, car, n_rows):
    row = lax.broadcasted_iota(jnp.int32, (8, LANE), 0)
    for c0 in range(0, a_ref.shape[1], LANE):
        cols = slice(c0, c0 + LANE)
        a_next = jnp.broadcast_to(car[0:1, cols], (8, LANE))
        g_next = jnp.broadcast_to(car[1:2, cols], (8, LANE))
        for r0 in range(n_rows - 8, -1, -8):
            a_blk, g = a_ref[r0:r0 + 8, cols], g_ref[r0:r0 + 8, cols]
            bm = jnp.where(row < 7, pltpu.roll(a_blk, 7, 0), a_next)
            for d in (1, 2, 4):
                keep = row < 8 - d
                g = g + bm * jnp.where(keep, pltpu.roll(g, 8 - d, 0), 0.0)
                bm = bm * jnp.where(keep, pltpu.roll(bm, 8 - d, 0), 1.0)
            g = g + bm * g_next
            g_ref[r0:r0 + 8, cols] = g
            a_next = jnp.broadcast_to(a_blk[0:1, :], (8, LANE))
            g_next = jnp.broadcast_to(g[0:1, :], (8, LANE))
        car[0:1, cols] = a_next[0:1, :]
        car[1:2, cols] = g_next[0:1, :]


def _scan_fwd_blocks(a_ref, u_ref, hcar, n_rows):
    row = lax.broadcasted_iota(jnp.int32, (8, LANE), 0)
    for c0 in range(0, a_ref.shape[1], LANE):
        cols = slice(c0, c0 + LANE)
        h_prev = jnp.broadcast_to(hcar[0:1, cols], (8, LANE))
        for r0 in range(0, n_rows, 8):
            a, u = a_ref[r0:r0 + 8, cols], u_ref[r0:r0 + 8, cols]
            for d in (1, 2, 4):
                keep = row >= d
                u = a * jnp.where(keep, pltpu.roll(u, d, 0), 0.0) + u
                a = a * jnp.where(keep, pltpu.roll(a, d, 0), 1.0)
            u = u + a * h_prev
            u_ref[r0:r0 + 8, cols] = u
            h_prev = jnp.broadcast_to(u[7:8, :], (8, LANE))
        hcar[0:1, cols] = h_prev[0:1, :]


def _dot(a, b):
    return jnp.dot(a, b, preferred_element_type=F32)


def _dot_nt(a, b):
    return lax.dot_general(a, b, (((1,), (1,)), ((), ())), preferred_element_type=F32)


def _dot_tn(a, b):
    return lax.dot_general(a, b, (((0,), (0,)), ((), ())), preferred_element_type=F32)


def _colsum(v):
    return jnp.sum(v, axis=0, keepdims=True)


def _rowmean(v):
    return jnp.mean(v, axis=-1, keepdims=True)


def _head_mean(v, mavg):
    hi = v.astype(MXU_DTYPE)
    lo = (v - hi.astype(F32)).astype(MXU_DTYPE)
    return _dot(hi, mavg) + _dot(lo, mavg)


_GELU_C0 = math.sqrt(2.0 / math.pi)
_GELU_C1 = 0.044715


def _gelu_and_grad(x):
    x2 = x * x
    th = jnp.tanh(_GELU_C0 * (x + _GELU_C1 * x * x2))
    ge = 0.5 * x * (1.0 + th)
    dge = 0.5 * (1.0 + th) + 0.5 * x * (1.0 - th * th) * (_GELU_C0 * (1.0 + 3.0 * _GELU_C1 * x2))
    return ge, dge


def _softplus(x):
    return jnp.maximum(x, 0.0) + jnp.log1p(jnp.exp(-jnp.abs(x)))


def _layer_norm_stats(r):
    mu = _rowmean(r)
    dl = r - mu
    var = _rowmean(dl * dl)
    rstd = lax.rsqrt(var + LN_EPS)
    return dl * rstd, rstd


def _scan_fwd(a, u, rows):
    n = a.shape[0]
    d = 1
    while d < n:
        keep = rows >= d
        a_s = jnp.where(keep, pltpu.roll(a, d, 0), 1.0)
        u_s = jnp.where(keep, pltpu.roll(u, d, 0), 0.0)
        u = a * u_s + u
        a = a * a_s
        d *= 2
    return u, a


def _scan_bwd(bm, g, rows):
    n = bm.shape[0]
    d = 1
    while d < n:
        keep = rows < n - d
        b_s = jnp.where(keep, pltpu.roll(bm, n - d, 0), 1.0)
        g_s = jnp.where(keep, pltpu.roll(g, n - d, 0), 0.0)
        g = g + bm * g_s
        bm = bm * b_s
        d *= 2
    return g, bm


def _lru_gates(xc, wr, wi, va_ref):
    xcb = xc.astype(MXU_DTYPE)
    r = _sigmoid(_dot(xcb, wr) + va_ref[5:6, :])
    ig = _sigmoid(_dot(xcb, wi) + va_ref[6:7, :])
    sp = _softplus(-va_ref[7:8, :])
    la = (-LRU_C) * r * sp
    a = jnp.exp(la)
    a2 = a * a
    mult = jnp.sqrt(-jnp.tanh(la) * (a2 + 1.0))
    return xcb, r, ig, sp, a, a2, mult


ROW_BLK = 32
LANE = 128


def _row_blocks(n_rows, rb, body, init):
    carry = init
    for j in range(n_rows // rb):
        carry = body(j * rb, carry)
    return carry


def _fold8(v):
    r, c = v.shape
    return v if r == 8 else v.reshape(r // 8, 8, c).sum(axis=0)


def _cast_rows(src_ref, dst_ref, n_rows):
    rb = min(16, n_rows)

    def rows(r0, carry):
        dst_ref[pl.ds(r0, rb), :] = src_ref[pl.ds(r0, rb), :].astype(dst_ref.dtype)
        return carry

    _row_blocks(n_rows, rb, rows, ())


def _full(shape):
    nd = len(shape)
    return pl.BlockSpec(shape, lambda *_: (0,) * nd)


_ANY = pl.BlockSpec(memory_space=pl.ANY)


def _params(n_grid):
    return pltpu.CompilerParams(dimension_semantics=("arbitrary",) * n_grid, vmem_limit_bytes=VMEM_LIMIT)


def _my_place():
    return lax.axis_index("x"), lax.axis_index("y"), lax.axis_index("c")


def _all_gather(arrays, name):
    n_arr = len(arrays)

    def body(*refs):
        start, forward, finish = _gather_steps(refs[:n_arr], refs[n_arr:2 * n_arr], *refs[2 * n_arr:])
        start()
        forward()
        finish()

    return pl.pallas_call(
        body, name=name, out_shape=_gather_out_shapes(arrays),
        in_specs=[_ANY] * n_arr, out_specs=[_ANY] * n_arr, scratch_shapes=_comm_sems(n_arr),
    )(*arrays)


def _gather_out_shapes(arrays):
    return [jax.ShapeDtypeStruct((NDEV * a.shape[0], a.shape[1]), a.dtype) for a in arrays]


def _comm_sems(n_arr):
    return [pltpu.SemaphoreType.DMA((n_arr, 7)), pltpu.SemaphoreType.DMA((n_arr, 7)), pltpu.SemaphoreType.DMA((n_arr,))]


def _gather_steps(x_refs, out_refs, send_sems, recv_sems, local_sems):
    n_arr = len(x_refs)
    x, y, c = _my_place()
    me, sibling = (x, y, c), (x, y, 1 - c)
    chips = [(1 - x, y), (x, 1 - y), (1 - x, 1 - y)]

    def rows(k, px, py, pc):
        m = x_refs[k].shape[0]
        return out_refs[k].at[pl.ds((4 * px + 2 * py + pc) * m, m), :]

    def copy(k, s, block, to, src=None):
        return pltpu.make_async_remote_copy(
            src_ref=rows(k, *block) if src is None else src, dst_ref=rows(k, *block),
            send_sem=send_sems.at[k, s], recv_sem=recv_sems.at[k, s], device_id=to, device_id_type=MESH)

    def mine():
        return [pltpu.make_async_copy(x_refs[k], rows(k, *me), local_sems.at[k]) for k in range(n_arr)]

    def first():
        cps = []
        for k in range(n_arr):
            cps.append(copy(k, 0, me, sibling, src=x_refs[k]))
            cps += [copy(k, 1 + j, me, (*chip, c), src=x_refs[k]) for j, chip in enumerate(chips)]
        return cps

    def passed():
        return [copy(k, 4 + j, (*chip, c), sibling) for k in range(n_arr) for j, chip in enumerate(chips)]

    def start():
        for cp in mine() + first():
            cp.start()

    def forward():
        fwd = passed()
        for k in range(n_arr):
            for j, chip in enumerate(chips):
                copy(k, 1 + j, (*chip, c), me).wait_recv()
                fwd[3 * k + j].start()

    def finish():
        for k in range(n_arr):
            copy(k, 0, sibling, me).wait_recv()
            for j, chip in enumerate(chips):
                copy(k, 4 + j, (*chip, 1 - c), me).wait_recv()
        for cp in first() + passed():
            cp.wait_send()
        for cp in mine():
            cp.wait()

    return start, forward, finish


def _exchange_blocks(arrays, name):
    n_arr = len(arrays)

    def body(*refs):
        start, finish = _exchange_steps(refs[:n_arr], refs[n_arr:2 * n_arr], *refs[2 * n_arr:])
        start()
        finish()

    return pl.pallas_call(
        body, name=name, out_shape=_exchange_out_shapes(arrays),
        in_specs=[_ANY] * n_arr, out_specs=[_ANY] * n_arr, scratch_shapes=_comm_sems(n_arr),
    )(*arrays)


def _exchange_and_gather(ex_arrays, ga_arrays, name):
    n_e, n_g = len(ex_arrays), len(ga_arrays)

    def body(*refs):
        e_in, g_in, refs = refs[:n_e], refs[n_e:n_e + n_g], refs[n_e + n_g:]
        e_out, g_out, sems = refs[:n_e], refs[n_e:n_e + n_g], refs[n_e + n_g:]
        e_start, e_finish = _exchange_steps(e_in, e_out, *sems[:3])
        g_start, g_forward, g_finish = _gather_steps(g_in, g_out, *sems[3:])
        g_start()
        e_start()
        g_forward()
        g_finish()
        e_finish()

    res = pl.pallas_call(
        body, name=name, out_shape=_exchange_out_shapes(ex_arrays) + _gather_out_shapes(ga_arrays),
        in_specs=[_ANY] * (n_e + n_g), out_specs=[_ANY] * (n_e + n_g),
        scratch_shapes=_comm_sems(n_e) + _comm_sems(n_g),
    )(*ex_arrays, *ga_arrays)
    return res[:n_e], res[n_e:]


def _exchange_out_shapes(arrays):
    return [jax.ShapeDtypeStruct((NDEV, a.shape[0] // NDEV, a.shape[1]), a.dtype) for a in arrays]


def _exchange_steps(g_refs, out_refs, send_sems, recv_sems, local_sems):
    n_arr = len(g_refs)
    x, y, c = _my_place()
    me = 4 * x + 2 * y + c

    def copy(k, rel):
        px, py, pc = x ^ ((rel >> 2) & 1), y ^ ((rel >> 1) & 1), c ^ (rel & 1)
        m = out_refs[k].shape[1]
        return pltpu.make_async_remote_copy(
            src_ref=g_refs[k].at[pl.ds((4 * px + 2 * py + pc) * m, m), :], dst_ref=out_refs[k].at[me],
            send_sem=send_sems.at[k, rel - 1], recv_sem=recv_sems.at[k, rel - 1],
            device_id=(px, py, pc), device_id_type=MESH)

    def copies():
        cps = []
        for k in range(n_arr):
            m = out_refs[k].shape[1]
            cps.append(pltpu.make_async_copy(g_refs[k].at[pl.ds(me * m, m), :], out_refs[k].at[me], local_sems.at[k]))
            cps += [copy(k, rel) for rel in range(1, NDEV)]
        return cps

    def start():
        for cp in copies():
            cp.start()

    def finish():
        for cp in copies():
            cp.wait()

    return start, finish


def _ada_fwd(c_all, w_ada_loc, b_ada_loc):
    def body(c_ref, w_ref, b_ref, o_ref):
        cv = c_ref[...]
        ca = (cv * _sigmoid(cv)).astype(MXU_DTYPE)
        o_ref[...] = _dot(ca, w_ref[...].astype(MXU_DTYPE)) + b_ref[...]

    return pl.pallas_call(
        body, name="ada_fwd", out_shape=jax.ShapeDtypeStruct((c_all.shape[0], w_ada_loc.shape[1]), F32),
        in_specs=[_full(c_all.shape), _full(w_ada_loc.shape), _full(b_ada_loc.shape)],
        out_specs=_full((c_all.shape[0], w_ada_loc.shape[1])),
        compiler_params=pltpu.CompilerParams(vmem_limit_bytes=VMEM_LIMIT),
    )(c_all, w_ada_loc, b_ada_loc)


def _ada_bwd(c_all, dmod_all, dmod_cols):
    def body(c_ref, da_ref, d_ref, o_ref, b_ref):
        cv = c_ref[...]
        ca = (cv * _sigmoid(cv)).astype(MXU_DTYPE)
        o_ref[...] = _dot_tn(ca, d_ref[...].astype(MXU_DTYPE))
        b_ref[...] = _colsum(da_ref[...])

    return pl.pallas_call(
        body, name="ada_bwd",
        out_shape=[jax.ShapeDtypeStruct((c_all.shape[1], dmod_cols.shape[1]), F32),
                   jax.ShapeDtypeStruct((1, dmod_all.shape[1]), F32)],
        in_specs=[_full(c_all.shape), _full(dmod_all.shape), _full(dmod_cols.shape)],
        out_specs=[_full((c_all.shape[1], dmod_cols.shape[1])), _full((1, dmod_all.shape[1]))],
        compiler_params=pltpu.CompilerParams(vmem_limit_bytes=VMEM_LIMIT),
    )(c_all, dmod_all, dmod_cols)


def _mixer_fwd(x, mods, win_t, wout, wr, wi, mavg, va, vb, ln1, ffn_shards):
    bl, seq, _ = x.shape
    t = min(TOK_TILE, seq)
    nt = seq // t
    n_g = len(ffn_shards)

    def body(x_ref, mod_ref, win_hbm, wout_hbm, wr_ref, wi_ref, mavg_ref, va_ref, vb_ref, ln1_ref, *rest):
        shard_refs, rest = rest[:n_g], rest[n_g:]
        proj_ref, h_ref, vc_ref, y_ref, mixed_ref, x1_ref, u1_ref = rest[:7]
        gathered_refs, rest = rest[7:7 + n_g], rest[7 + n_g:]
        win_v, wout_v, xa_ext, vg_ext, rot_a, hcar, sems, g_send, g_recv, g_local = rest
        b, i = pl.program_id(0), pl.program_id(1)
        step = b * nt + i
        g_start, g_forward, g_finish = _gather_steps(shard_refs, gathered_refs, g_send, g_recv, g_local)

        @pl.when(step == 0)
        def _():
            g_start()
            cps = [pltpu.make_async_copy(win_hbm, win_v, sems.at[0]), pltpu.make_async_copy(wout_hbm, wout_v, sems.at[1])]
            for cp in cps:
                cp.start()
            for cp in cps:
                cp.wait()

        pl.when(step == (bl * nt) // 2)(g_forward)

        @pl.when(i == 0)
        def _():
            xa_ext[0:HALO, :] = jnp.zeros((HALO, LW), F32)
            vg_ext[0:HALO31, :] = jnp.zeros((HALO31, CW), F32)
            hcar[...] = jnp.zeros_like(hcar)

        rows = lax.broadcasted_iota(jnp.int32, (t, LW), 0)
        xv = x_ref[...]
        sh, sc, gt = mod_ref[0:1, :], mod_ref[1:2, :], mod_ref[2:3, :]
        ub = (xv * (1.0 + sc) + sh).astype(MXU_DTYPE)
        u1_ref[...] = ub
        proj = _dot_nt(ub, win_v[...])
        proj_ref[...] = proj
        xa, ga = proj[:, 0:LW], proj[:, LW:2 * LW]
        vbr, gb = proj[:, 2 * LW:2 * LW + CW], proj[:, 2 * LW + CW:]

        xa_ext[HALO:HALO + t, :] = xa
        xc = va_ref[4:5, :]
        for k in range(LRU_K):
            xc = xc + va_ref[k:k + 1, :] * xa_ext[pl.ds(HALO - (LRU_K - 1) + k, t), :]
        xa_ext[0:HALO, :] = xa_ext[t:t + HALO, :]
        _, _, ig, _, a, _, mult = _lru_gates(xc, wr_ref[...], wi_ref[...], va_ref)
        h_loc, a_cum = _scan_fwd(a, mult * (ig * xc), rows)
        h = h_loc + a_cum * hcar[0:1, :]
        hcar[0:1, :] = h[t - 1:t, :]
        h_ref[...] = h
        ge, _ = _gelu_and_grad(ga)
        ya = ge * h

        vg_ext[HALO31:HALO31 + t, :] = vbr * _sigmoid(gb)
        vg_win = _shifted_rows(vg_ext, rot_a, t)
        vc = vb_ref[CONV_K:CONV_K + 1, :]
        for k in range(CONV_K):
            vc = vc + vb_ref[k:k + 1, :] * vg_win(HALO31 - (CONV_K - 1) + k)
        vg_ext[0:HALO31, :] = vg_ext[t:t + HALO31, :]
        vc_ref[...] = vc
        mavg_v = mavg_ref[...]
        dl = vc - _head_mean(vc, mavg_v)
        rs = lax.rsqrt(_head_mean(dl * dl, mavg_v) + LN_EPS)
        yl = dl * rs * vb_ref[CONV_K + 1:CONV_K + 2, :] + vb_ref[CONV_K + 2:CONV_K + 3, :]
        yb = yl * _sigmoid(yl)

        yv = jnp.concatenate([ya, yb], axis=1).astype(MXU_DTYPE)
        y_ref[...] = yv
        mixed = _dot(yv, wout_v[...])
        mixed_ref[...] = mixed
        xh, _ = _layer_norm_stats(ALPHA * xv + (1.0 + gt) * mixed)
        x1_ref[...] = xh * ln1_ref[0:1, :] + ln1_ref[1:2, :]

        pl.when(step == bl * nt - 1)(g_finish)

    tok = lambda w: pl.BlockSpec((None, t, w), lambda b, i: (b, i, 0))
    outs = [(2 * LW + 2 * CW, F32), (LW, F32), (CW, F32), (D, MXU_DTYPE), (D, F32), (D, F32), (D, MXU_DTYPE)]
    return pl.pallas_call(
        body, name="mixer_fwd", grid=(bl, nt),
        out_shape=[jax.ShapeDtypeStruct((bl, seq, w), dt) for w, dt in outs] + _gather_out_shapes(ffn_shards),
        in_specs=[tok(D), pl.BlockSpec((None, 8, D), lambda b, i: (b, 0, 0)), _ANY, _ANY,
                  _full(wr.shape), _full(wi.shape), _full(mavg.shape), _full(va.shape), _full(vb.shape), _full(ln1.shape)]
        + [_ANY] * n_g,
        out_specs=[tok(w) for w, _ in outs] + [_ANY] * n_g,
        scratch_shapes=[pltpu.VMEM(win_t.shape, MXU_DTYPE), pltpu.VMEM(wout.shape, MXU_DTYPE),
                        pltpu.VMEM((t + HALO, LW), F32), pltpu.VMEM((t + HALO31, CW), F32),
                        pltpu.VMEM((7, t + HALO31 - 8, CW), F32), pltpu.VMEM((8, LW), F32),
                        pltpu.SemaphoreType.DMA((2,))] + _comm_sems(n_g),
        compiler_params=_params(2),
    )(x, mods, win_t, wout, wr, wi, mavg, va, vb, ln1, *ffn_shards)


def _ffn_fwd(x1, mods, tgt, wup_t, wdown, f3, ln2):
    bl, seq, _ = x1.shape
    t = min(TOK_TILE, seq)
    nt = seq // t
    n_chunk = DFF // FF_CHUNK

    def body(x1_ref, mod_ref, tgt_ref, wup_hbm, wdown_hbm, f3_ref, ln2_ref,
             hh_ref, z_ref, u2_ref, y2_ref, loss_ref,
             wup_v, wdown_v, g_ext, sems):
        b, i = pl.program_id(0), pl.program_id(1)

        @pl.when((b == 0) & (i == 0))
        def _():
            cps = [pltpu.make_async_copy(wup_hbm, wup_v, sems.at[0]), pltpu.make_async_copy(wdown_hbm, wdown_v, sems.at[1])]
            for cp in cps:
                cp.start()
            for cp in cps:
                cp.wait()
            loss_ref[...] = jnp.zeros_like(loss_ref)

        @pl.when(i == 0)
        def _():
            for ch in range(n_chunk):
                g_ext[ch, 0:HALO, :] = jnp.zeros((HALO, FF_CHUNK), F32)

        x1v = x1_ref[...]
        sh, sc, gt = mod_ref[3:4, :], mod_ref[4:5, :], mod_ref[5:6, :]
        ub = (x1v * (1.0 + sc) + sh).astype(MXU_DTYPE)
        u2_ref[...] = ub
        y2 = jnp.zeros((t, D), F32)
        for ch in range(n_chunk):
            lo = ch * FF_CHUNK
            v = _dot_nt(ub, wup_v[lo:lo + FF_CHUNK, :])
            g = _dot_nt(ub, wup_v[DFF + lo:DFF + lo + FF_CHUNK, :])
            hh_ref[:, lo:lo + FF_CHUNK] = v
            hh_ref[:, DFF + lo:DFF + lo + FF_CHUNK] = g
            g_ext[ch, HALO:HALO + t, :] = g
            gc = f3_ref[FFN_K:FFN_K + 1, lo:lo + FF_CHUNK]
            for k in range(FFN_K):
                gc = gc + f3_ref[k:k + 1, lo:lo + FF_CHUNK] * g_ext[ch, pl.ds(HALO - (FFN_K - 1) + k, t), :]
            g_ext[ch, 0:HALO, :] = g_ext[ch, t:t + HALO, :]
            zb = (gc * _sigmoid(gc) * v).astype(MXU_DTYPE)
            z_ref[:, lo:lo + FF_CHUNK] = zb
            y2 = y2 + _dot(zb, wdown_v[lo:lo + FF_CHUNK, :])
        y2_ref[...] = y2
        xh, _ = _layer_norm_stats(ALPHA * x1v + (1.0 + gt) * y2)
        err = xh * ln2_ref[0:1, :] + ln2_ref[1:2, :] - tgt_ref[...]
        e2 = (err * err).reshape(t // 8, 8, D).sum(axis=0)
        part = e2[:, 0:128]
        for j in range(1, D // 128):
            part = part + e2[:, 128 * j:128 * (j + 1)]
        loss_ref[...] += part

    tok = lambda w: pl.BlockSpec((None, t, w), lambda b, i: (b, i, 0))
    outs = [(2 * DFF, F32), (DFF, MXU_DTYPE), (D, MXU_DTYPE), (D, F32)]
    return pl.pallas_call(
        body, name="ffn_fwd", grid=(bl, nt),
        out_shape=[jax.ShapeDtypeStruct((bl, seq, w), dt) for w, dt in outs] + [jax.ShapeDtypeStruct((8, 128), F32)],
        in_specs=[tok(D), pl.BlockSpec((None, 8, D), lambda b, i: (b, 0, 0)), tok(D), _ANY, _ANY,
                  _full(f3.shape), _full(ln2.shape)],
        out_specs=[tok(w) for w, _ in outs] + [_full((8, 128))],
        scratch_shapes=[pltpu.VMEM(wup_t.shape, MXU_DTYPE), pltpu.VMEM(wdown.shape, MXU_DTYPE),
                        pltpu.VMEM((n_chunk, t + HALO, FF_CHUNK), F32), pltpu.SemaphoreType.DMA((2,))],
        compiler_params=_params(2),
    )(x1, mods, tgt, wup_t, wdown, f3, ln2)


def _ffn_bwd(x1, y2, tgt, hh, mods, wup_t, wdown, f3, ln2):
    bl, seq, _ = x1.shape
    t = min(TOK_TILE, seq)
    nt = seq // t
    n_chunk = DFF // FF_CHUNK
    rb = min(ROW_BLK, t)

    def body(x1_ref, y2_ref, tgt_ref, hh_ref, halo_ref, mod_ref, wup_hbm, wdown_hbm, f3_ref, ln2_ref,
             dx1_ref, dy2_ref, dhh_ref, dln2_ref, df3_ref, dmod_ref,
             wup_v, wdown_v, g_ext, dgc_ext, rot, dz_scr, wide_scr, sems):
        b, i = pl.program_id(0), pl.program_id(1)
        tt = nt - 1 - i

        @pl.when((b == 0) & (i == 0))
        def _():
            cps = [pltpu.make_async_copy(wup_hbm, wup_v, sems.at[0]), pltpu.make_async_copy(wdown_hbm, wdown_v, sems.at[1])]
            for cp in cps:
                cp.start()
            for cp in cps:
                cp.wait()
            dln2_ref[...] = jnp.zeros_like(dln2_ref)
            df3_ref[...] = jnp.zeros_like(df3_ref)

        @pl.when(i == 0)
        def _():
            dmod_ref[...] = jnp.zeros_like(dmod_ref)
            for ch in range(n_chunk):
                dgc_ext[ch, t:t + HALO, :] = jnp.zeros((HALO, FF_CHUNK), F32)

        def ln_rows(r0, acc):
            rows = pl.ds(r0, 8)
            x1v, y2v = x1_ref[rows, :], y2_ref[rows, :]
            gt = mod_ref[5:6, :]
            xh, rstd = _layer_norm_stats(ALPHA * x1v + (1.0 + gt) * y2v)
            g2 = ln2_ref[0:1, :]
            dx2 = (xh * g2 + ln2_ref[1:2, :] - tgt_ref[rows, :]) * (1.0 / D)
            dxh = dx2 * g2
            dr2 = rstd * (dxh - _rowmean(dxh) - xh * _rowmean(dxh * xh))
            wide_scr[rows, :] = (1.0 + gt) * dr2
            dx1_ref[rows, :] = ALPHA * dr2
            return acc[0] + dx2 * xh, acc[1] + dx2, acc[2] + dr2 * y2v

        zero_d = jnp.zeros((8, D), F32)
        acc = _row_blocks(t, 8, ln_rows, (zero_d, zero_d, zero_d))
        dln2_ref[0:1, :] += _colsum(acc[0])
        dln2_ref[1:2, :] += _colsum(acc[1])
        dmod_ref[2:3, :] += _colsum(acc[2])
        _cast_rows(wide_scr, dy2_ref, t)

        halo_keep = (tt > 0).astype(F32)
        zero_l = jnp.zeros((8, LANE), F32)
        for ch in range(n_chunk):
            lo = ch * FF_CHUNK
            dz_scr[...] = _dot_nt(dy2_ref[...], wdown_v[lo:lo + FF_CHUNK, :])
            g_ext[0:HALO, :] = halo_ref[:, lo:lo + FF_CHUNK] * halo_keep
            g_ext[HALO:HALO + t, :] = hh_ref[:, DFF + lo:DFF + lo + FF_CHUNK]
            for k in range(FFN_K - 1):
                rot[k] = g_ext[pl.ds(HALO - (FFN_K - 1) + k, t), :]
            for cb in range(FF_CHUNK // LANE):
                cols = slice(cb * LANE, (cb + 1) * LANE)
                fcols = slice(lo + cb * LANE, lo + (cb + 1) * LANE)

                def gate_rows(r0, acc, cols=cols, fcols=fcols, ch=ch):
                    rows = pl.ds(r0, rb)
                    taps = [rot[0, rows, cols], rot[1, rows, cols], g_ext[pl.ds(r0 + HALO, rb), cols]]
                    gc = f3_ref[FFN_K:FFN_K + 1, fcols]
                    for k in range(FFN_K):
                        gc = gc + f3_ref[k:k + 1, fcols] * taps[k]
                    s = _sigmoid(gc)
                    dz = dz_scr[rows, cols]
                    dhh_ref[rows, fcols] = (dz * (gc * s)).astype(MXU_DTYPE)
                    dgc = dz * hh_ref[rows, fcols] * (s * (1.0 + gc * (1.0 - s)))
                    dgc_ext[ch, rows, cols] = dgc
                    return tuple(acc[k] + _fold8(dgc * taps[k]) for k in range(FFN_K)) + (acc[FFN_K] + _fold8(dgc),)

                acc = _row_blocks(t, rb, gate_rows, (zero_l,) * (FFN_K + 1))
                for k in range(FFN_K + 1):
                    df3_ref[k:k + 1, fcols] += _colsum(acc[k])
            for k in range(FFN_K - 1):
                rot[k] = dgc_ext[ch, pl.ds(k + 1, t), :]
            for cb in range(FF_CHUNK // LANE):
                cols = slice(cb * LANE, (cb + 1) * LANE)
                fcols = slice(lo + cb * LANE, lo + (cb + 1) * LANE)
                gcols = slice(DFF + lo + cb * LANE, DFF + lo + (cb + 1) * LANE)

                def dg_rows(r0, acc, cols=cols, fcols=fcols, gcols=gcols, ch=ch):
                    rows = pl.ds(r0, rb)
                    dg = (f3_ref[2:3, fcols] * dgc_ext[ch, rows, cols] + f3_ref[1:2, fcols] * rot[0, rows, cols]
                          + f3_ref[0:1, fcols] * rot[1, rows, cols])
                    dhh_ref[rows, gcols] = dg.astype(MXU_DTYPE)
                    return acc

                _row_blocks(t, rb, dg_rows, ())
            dgc_ext[ch, t:t + HALO, :] = dgc_ext[ch, 0:HALO, :]

        wide_scr[...] = _dot(dhh_ref[...], wup_v[...])

        def out_rows(r0, acc):
            rows = pl.ds(r0, 8)
            du2 = wide_scr[rows, :]
            dx1_ref[rows, :] = dx1_ref[rows, :] + du2 * (1.0 + mod_ref[4:5, :])
            return acc[0] + du2 * x1_ref[rows, :], acc[1] + du2

        acc = _row_blocks(t, 8, out_rows, (zero_d, zero_d))
        dmod_ref[1:2, :] += _colsum(acc[0])
        dmod_ref[0:1, :] += _colsum(acc[1])

    rev = lambda w: pl.BlockSpec((None, t, w), lambda b, i: (b, nt - 1 - i, 0))
    halo = pl.BlockSpec((None, HALO, DFF), lambda b, i: (b, jnp.maximum((nt - 1 - i) * (t // HALO) - 1, 0), 1))
    return pl.pallas_call(
        body, name="ffn_bwd", grid=(bl, nt),
        out_shape=[jax.ShapeDtypeStruct((bl, seq, D), F32), jax.ShapeDtypeStruct((bl, seq, D), MXU_DTYPE),
                   jax.ShapeDtypeStruct((bl, seq, 2 * DFF), MXU_DTYPE), jax.ShapeDtypeStruct((8, D), F32),
                   jax.ShapeDtypeStruct((8, DFF), F32), jax.ShapeDtypeStruct((bl, 8, D), F32)],
        in_specs=[rev(D), rev(D), rev(D), rev(2 * DFF), halo, pl.BlockSpec((None, 8, D), lambda b, i: (b, 0, 0)),
                  _ANY, _ANY, _full(f3.shape), _full(ln2.shape)],
        out_specs=[rev(D), rev(D), rev(2 * DFF), _full((8, D)), _full((8, DFF)),
                   pl.BlockSpec((None, 8, D), lambda b, i: (b, 0, 0))],
        scratch_shapes=[pltpu.VMEM(wup_t.shape, MXU_DTYPE), pltpu.VMEM(wdown.shape, MXU_DTYPE),
                        pltpu.VMEM((t + HALO, FF_CHUNK), F32), pltpu.VMEM((n_chunk, t + HALO, FF_CHUNK), F32),
                        pltpu.VMEM((FFN_K - 1, t, FF_CHUNK), F32), pltpu.VMEM((t, FF_CHUNK), F32),
                        pltpu.VMEM((t, D), F32), pltpu.SemaphoreType.DMA((2,))],
        compiler_params=_params(2),
    )(x1, y2, tgt, hh, hh, mods, wup_t, wdown, f3, ln2)


def _mixer_bwd(x, dx1, proj, h, vc, mixed, mods, win_t, wout, wri, mavg2, va, vb, ln1, ffn_wgrads):
    bl, seq, _ = x.shape
    t = min(TOK_TILE, seq)
    nt = seq // t
    pw = 2 * LW + 2 * CW
    n_g = len(ffn_wgrads)
    rb = min(ROW_BLK, t)
    half_f32 = [pltpu.VMEM((t, LW), F32)] * 10

    def body(x_ref, dx1_ref, proj_ref, phalo_ref, h_ref, hhalo_ref, vc_ref, mixed_ref, mod_ref,
             win_hbm, wout_hbm, wri_ref, mavg_ref, va_ref, vb_ref, ln1_ref, *rest):
        wgrad_refs, rest = rest[:n_g], rest[n_g:]
        gx_ref, dproj_ref, dm_ref, ga_ref, gw31_ref, dwr_ref, dwi_ref, dln1_ref, dmod_ref = rest[:9]
        slot_refs, rest = rest[9:9 + n_g], rest[9 + n_g:]
        (win_v, wout_v, xa_ext, vg_scr, dvc_ext, dxc_ext, rot_b, car, sems,
         st_a, st_b, st_c, yn_scr, dyn_scr, hprev_scr, xc_scr, a_scr, mult_scr, g_scr,
         wide, gates, hilo_a, hilo_b, dgates, xcb_scr, rot_c, rot_d, hp_ext, e_send, e_recv, e_local) = rest
        b, i = pl.program_id(0), pl.program_id(1)
        tt = nt - 1 - i
        e_start, e_finish = _exchange_steps(wgrad_refs, slot_refs, e_send, e_recv, e_local)

        @pl.when((b == 0) & (i == 0))
        def _():
            e_start()
            cps = [pltpu.make_async_copy(win_hbm, win_v, sems.at[0]), pltpu.make_async_copy(wout_hbm, wout_v, sems.at[1])]
            for cp in cps:
                cp.start()
            for cp in cps:
                cp.wait()
            for ref in (ga_ref, gw31_ref, dwr_ref, dwi_ref, dln1_ref):
                ref[...] = jnp.zeros_like(ref)

        @pl.when(i == 0)
        def _():
            dmod_ref[...] = jnp.zeros_like(dmod_ref)
            dvc_ext[t:t + HALO31, :] = jnp.zeros((HALO31, CW), F32)
            dxc_ext[t:t + HALO, :] = jnp.zeros((HALO, LW), F32)
            car[...] = jnp.zeros_like(car)

        halo_keep = (tt > 0).astype(F32)
        zero_d = jnp.zeros((8, D), F32)
        zero_l = jnp.zeros((8, LANE), F32)
        blocks = [(r0, c0) for c0 in range(0, LW, LANE) for r0 in range(0, t, rb)]

        def add_row(ref, row, c0, acc):
            ref[row:row + 1, c0:c0 + LANE] += _colsum(acc)

        def ln_rows(r0, acc):
            rows = pl.ds(r0, 8)
            xv, mixed = x_ref[rows, :], mixed_ref[rows, :]
            gt = mod_ref[2:3, :]
            xh, rstd = _layer_norm_stats(ALPHA * xv + (1.0 + gt) * mixed)
            dx1 = dx1_ref[rows, :]
            dxh = dx1 * ln1_ref[0:1, :]
            dr1 = rstd * (dxh - _rowmean(dxh) - xh * _rowmean(dxh * xh))
            wide[rows, :] = (1.0 + gt) * dr1
            gx_ref[rows, :] = ALPHA * dr1
            return acc[0] + dx1 * xh, acc[1] + dx1, acc[2] + dr1 * mixed

        acc = _row_blocks(t, 8, ln_rows, (zero_d, zero_d, zero_d))
        dln1_ref[0:1, :] += _colsum(acc[0])
        dln1_ref[1:2, :] += _colsum(acc[1])
        dmod_ref[2:3, :] += _colsum(acc[2])
        _cast_rows(wide, dm_ref, t)
        wide[...] = _dot_nt(dm_ref[...], wout_v[...])

        mavg2 = mavg_ref[...]
        for r0, c0 in blocks:
            rows, cols = pl.ds(r0, rb), slice(c0, c0 + LANE)
            _store_hilo(hilo_a, rows, c0, vc_ref[rows, cols])
        st_a[...] = _dot(hilo_a[...], mavg2)
        for r0, c0 in blocks:
            rows, cols = pl.ds(r0, rb), slice(c0, c0 + LANE)
            dl = vc_ref[rows, cols] - st_a[rows, cols]
            yn_scr[rows, cols] = dl
            _store_hilo(hilo_a, rows, c0, dl * dl)
        st_b[...] = _dot(hilo_a[...], mavg2)
        for c0 in range(0, CW, LANE):
            cols, pc = slice(c0, c0 + LANE), 2 * LW + c0
            ng, nb = vb_ref[CONV_K + 1:CONV_K + 2, cols], vb_ref[CONV_K + 2:CONV_K + 3, cols]
            acc_g, acc_b = zero_l, zero_l
            for r0 in range(0, t, rb):
                rows = pl.ds(r0, rb)
                rs = lax.rsqrt(st_b[rows, cols] + LN_EPS)
                yn = yn_scr[rows, cols] * rs
                yl = yn * ng + nb
                s = _sigmoid(yl)
                dyl = wide[rows, LW + c0:LW + c0 + LANE] * (s * (1.0 + yl * (1.0 - s)))
                acc_g, acc_b = acc_g + _fold8(dyl * yn), acc_b + _fold8(dyl)
                dyn = dyl * ng
                st_b[rows, cols] = rs
                yn_scr[rows, cols] = yn
                dyn_scr[rows, cols] = dyn
                _store_hilo(hilo_a, rows, c0, dyn)
                _store_hilo(hilo_b, rows, c0, dyn * yn)
                vg_scr[rows, cols] = proj_ref[rows, pc:pc + LANE] * _sigmoid(proj_ref[rows, pc + CW:pc + CW + LANE])
            add_row(ga_ref, 8, c0, acc_g)
            add_row(ga_ref, 9, c0, acc_b)
        st_a[...] = _dot(hilo_a[...], mavg2)
        st_c[...] = _dot(hilo_b[...], mavg2)
        for c0 in range(0, CW, LANE):
            cols = slice(c0, c0 + LANE)
            acc_b = zero_l
            for r0 in range(0, t, rb):
                rows = pl.ds(r0, rb)
                dvc = st_b[rows, cols] * (dyn_scr[rows, cols] - st_a[rows, cols] - yn_scr[rows, cols] * st_c[rows, cols])
                dvc_ext[rows, cols] = dvc
                acc_b = acc_b + _fold8(dvc)
            add_row(ga_ref, 10, c0, acc_b)
        dvc_win = _shifted_rows(dvc_ext, rot_b, rb)
        for c0 in range(0, CW, LANE):
            cols, pc = slice(c0, c0 + LANE), 2 * LW + c0
            for k in range(CONV_K):
                acc_w = zero_l
                for r0 in range(0, t, rb):
                    acc_w = acc_w + _fold8(vg_scr[pl.ds(r0, rb), cols] * dvc_win(CONV_K - 1 - k, r0, cols))
                add_row(gw31_ref, k, c0, acc_w)
            for r0 in range(0, t, rb):
                rows = pl.ds(r0, rb)
                dvg = jnp.zeros((rb, LANE), F32)
                for k in range(CONV_K):
                    dvg = dvg + vb_ref[k:k + 1, cols] * dvc_win(CONV_K - 1 - k, r0, cols)
                vbr = proj_ref[rows, pc:pc + LANE]
                sgb = _sigmoid(proj_ref[rows, pc + CW:pc + CW + LANE])
                dproj_ref[rows, pc:pc + LANE] = (dvg * sgb).astype(MXU_DTYPE)
                dproj_ref[rows, pc + CW:pc + CW + LANE] = (dvg * vbr * sgb * (1.0 - sgb)).astype(MXU_DTYPE)
        dvc_ext[t:t + HALO31, :] = dvc_ext[0:HALO31, :]

        xa_ext[0:HALO, :] = phalo_ref[HALO31 - HALO:HALO31, 0:LW] * halo_keep
        xa_ext[HALO:HALO + t, :] = proj_ref[:, 0:LW]
        for k in range(LRU_K - 1):
            rot_c[k] = xa_ext[pl.ds(HALO - (LRU_K - 1) + k, t), :]
        hp_ext[0:HALO, :] = hhalo_ref[...] * halo_keep
        hp_ext[HALO:HALO + t, :] = h_ref[...]
        hprev_scr[...] = hp_ext[pl.ds(HALO - 1, t), :]

        def xa_tap(k, rows, r0, cols):
            return xa_ext[pl.ds(HALO + r0, rb), cols] if k == LRU_K - 1 else rot_c[k, rows, cols]

        for r0, c0 in blocks:
            rows, cols = pl.ds(r0, rb), slice(c0, c0 + LANE)
            xc = va_ref[4:5, cols]
            for k in range(LRU_K):
                xc = xc + va_ref[k:k + 1, cols] * xa_tap(k, rows, r0, cols)
            xc_scr[rows, cols] = xc
            xcb_scr[rows, cols] = xc.astype(MXU_DTYPE)
        gates[...] = _dot(xcb_scr[...], wri_ref[...])
        for r0, c0 in blocks:
            rows, cols = pl.ds(r0, rb), slice(c0, c0 + LANE)
            r = _sigmoid(gates[rows, cols] + va_ref[5:6, cols])
            ig = _sigmoid(gates[rows, LW + c0:LW + c0 + LANE] + va_ref[6:7, cols])
            la = (-LRU_C) * r * _softplus(-va_ref[7:8, cols])
            a = jnp.exp(la)
            gates[rows, cols] = r
            gates[rows, LW + c0:LW + c0 + LANE] = ig
            a_scr[rows, cols] = a
            mult_scr[rows, cols] = jnp.sqrt(-jnp.tanh(la) * (a * a + 1.0))
            ge, dge = _gelu_and_grad(proj_ref[rows, LW + c0:LW + c0 + LANE])
            dya = wide[rows, cols]
            dproj_ref[rows, LW + c0:LW + c0 + LANE] = (dya * h_ref[rows, cols] * dge).astype(MXU_DTYPE)
            g_scr[rows, cols] = dya * ge
        _scan_bwd_blocks(a_scr, g_scr, car, t)
        for c0 in range(0, LW, LANE):
            cols = slice(c0, c0 + LANE)
            sp = _softplus(-va_ref[7:8, cols])
            acc_l, acc_r, acc_i = zero_l, zero_l, zero_l
            for r0 in range(0, t, rb):
                rows = pl.ds(r0, rb)
                gv, xc, a, mult = g_scr[rows, cols], xc_scr[rows, cols], a_scr[rows, cols], mult_scr[rows, cols]
                r, ig = gates[rows, cols], gates[rows, LW + c0:LW + c0 + LANE]
                dla = gv * hprev_scr[rows, cols] * a - gv * (ig * xc) * (a * a / mult)
                acc_l = acc_l + _fold8(dla * ((-LRU_C) * r))
                dgr = dla * ((-LRU_C) * sp) * r * (1.0 - r)
                dgi = gv * (mult * xc) * ig * (1.0 - ig)
                acc_r, acc_i = acc_r + _fold8(dgr), acc_i + _fold8(dgi)
                dgates[rows, cols] = dgr.astype(MXU_DTYPE)
                dgates[rows, LW + c0:LW + c0 + LANE] = dgi.astype(MXU_DTYPE)
                dxc_ext[rows, cols] = gv * (mult * ig)
            add_row(ga_ref, 5, c0, acc_r)
            add_row(ga_ref, 6, c0, acc_i)
            ga_ref[7:8, cols] += _colsum(acc_l) * (-_sigmoid(-va_ref[7:8, cols]))
        dwr_ref[...] += _dot_tn(xcb_scr[...], dgates[:, 0:LW])
        dwi_ref[...] += _dot_tn(xcb_scr[...], dgates[:, LW:])
        st_a[...] = _dot_nt(dgates[...], wri_ref[...])
        for c0 in range(0, LW, LANE):
            cols = slice(c0, c0 + LANE)
            acc_b = zero_l
            for r0 in range(0, t, rb):
                rows = pl.ds(r0, rb)
                dxc = dxc_ext[rows, cols] + st_a[rows, cols]
                dxc_ext[rows, cols] = dxc
                acc_b = acc_b + _fold8(dxc)
            add_row(ga_ref, 4, c0, acc_b)
        for k in range(LRU_K - 1):
            rot_d[k] = dxc_ext[pl.ds(k + 1, t), :]
        for c0 in range(0, LW, LANE):
            cols = slice(c0, c0 + LANE)
            acc_w = [zero_l] * LRU_K
            for r0 in range(0, t, rb):
                rows = pl.ds(r0, rb)
                dxc = dxc_ext[rows, cols]
                dxa = va_ref[LRU_K - 1:LRU_K, cols] * dxc
                for k in range(LRU_K):
                    acc_w[k] = acc_w[k] + _fold8(dxc * xa_tap(k, rows, r0, cols))
                    if k < LRU_K - 1:
                        dxa = dxa + va_ref[k:k + 1, cols] * rot_d[LRU_K - 2 - k, rows, cols]
                dproj_ref[rows, cols] = dxa.astype(MXU_DTYPE)
            for k in range(LRU_K):
                add_row(ga_ref, k, c0, acc_w[k])
        dxc_ext[t:t + HALO, :] = dxc_ext[0:HALO, :]

        wide[...] = _dot(dproj_ref[...], win_v[...])

        def out_rows(r0, acc):
            rows = pl.ds(r0, 8)
            du1 = wide[rows, :]
            gx_ref[rows, :] = gx_ref[rows, :] + du1 * (1.0 + mod_ref[1:2, :])
            return acc[0] + du1 * x_ref[rows, :], acc[1] + du1

        acc = _row_blocks(t, 8, out_rows, (zero_d, zero_d))
        dmod_ref[1:2, :] += _colsum(acc[0])
        dmod_ref[0:1, :] += _colsum(acc[1])

        pl.when((b == bl - 1) & (i == nt - 1))(e_finish)

    rev = lambda w: pl.BlockSpec((None, t, w), lambda b, i: (b, nt - 1 - i, 0))

    def halo(rows_, w):
        return pl.BlockSpec((None, rows_, w), lambda b, i: (b, jnp.maximum((nt - 1 - i) * (t // rows_) - 1, 0), 0))

    return pl.pallas_call(
        body, name="mixer_bwd", grid=(bl, nt),
        out_shape=[jax.ShapeDtypeStruct((bl, seq, D), F32), jax.ShapeDtypeStruct((bl, seq, pw), MXU_DTYPE),
                   jax.ShapeDtypeStruct((bl, seq, D), MXU_DTYPE), jax.ShapeDtypeStruct((16, LW), F32),
                   jax.ShapeDtypeStruct((32, CW), F32), jax.ShapeDtypeStruct((LW, LW), F32),
                   jax.ShapeDtypeStruct((LW, LW), F32), jax.ShapeDtypeStruct((8, D), F32),
                   jax.ShapeDtypeStruct((bl, 8, D), F32)] + _exchange_out_shapes(ffn_wgrads),
        in_specs=[rev(D), rev(D), rev(pw), halo(HALO31, pw), rev(LW), halo(HALO, LW), rev(CW), rev(D),
                  pl.BlockSpec((None, 8, D), lambda b, i: (b, 0, 0)), _ANY, _ANY,
                  _full(wri.shape), _full(mavg2.shape), _full(va.shape), _full(vb.shape), _full(ln1.shape)]
        + [_ANY] * n_g,
        out_specs=[rev(D), rev(pw), rev(D), _full((16, LW)), _full((32, CW)), _full((LW, LW)), _full((LW, LW)),
                   _full((8, D)), pl.BlockSpec((None, 8, D), lambda b, i: (b, 0, 0))] + [_ANY] * n_g,
        scratch_shapes=[pltpu.VMEM(win_t.shape, MXU_DTYPE), pltpu.VMEM(wout.shape, MXU_DTYPE),
                        pltpu.VMEM((t + HALO, LW), F32), pltpu.VMEM((t, CW), F32),
                        pltpu.VMEM((t + HALO31, CW), F32), pltpu.VMEM((t + HALO, LW), F32),
                        pltpu.VMEM((7, t + HALO31 - 8, CW), F32),
                        pltpu.VMEM((8, LW), F32), pltpu.SemaphoreType.DMA((2,))] + half_f32
        + [pltpu.VMEM((t, D), F32), pltpu.VMEM((t, 2 * LW), F32), pltpu.VMEM((t, 2 * CW), MXU_DTYPE),
           pltpu.VMEM((t, 2 * CW), MXU_DTYPE), pltpu.VMEM((t, 2 * LW), MXU_DTYPE), pltpu.VMEM((t, LW), MXU_DTYPE),
           pltpu.VMEM((LRU_K - 1, t, LW), F32), pltpu.VMEM((LRU_K - 1, t, LW), F32), pltpu.VMEM((t + HALO, LW), F32)]
        + _comm_sems(n_g),
        compiler_params=_params(2),
    )(x, dx1, proj, proj, h, h, vc, mixed, mods, win_t, wout, wri, mavg2, va, vb, ln1, *ffn_wgrads)


def _weight_grad(a, bmat, tm, name):
    ntok, m = a.shape
    n = bmat.shape[1]
    tk = min(WG_TOK_TILE, ntok)
    nk = ntok // tk

    def body(a_ref, b_ref, o_ref):
        @pl.when(pl.program_id(1) == 0)
        def _():
            o_ref[...] = jnp.zeros_like(o_ref)

        o_ref[...] += _dot_tn(a_ref[...], b_ref[...])

    return pl.pallas_call(
        body, name=name, grid=(m // tm, nk), out_shape=jax.ShapeDtypeStruct((m, n), F32),
        in_specs=[pl.BlockSpec((tk, tm), lambda i, k: (k, i)), pl.BlockSpec((tk, n), lambda i, k: (k, 0))],
        out_specs=pl.BlockSpec((tm, n), lambda i, k: (i, 0)),
        compiler_params=_params(2),
    )(a, bmat)


def _sum_slots(slots, name):
    _, r, cdim = slots.shape
    tr = 32 if r % 32 == 0 else r

    def body(s_ref, o_ref):
        acc = s_ref[0].astype(F32)
        for j in range(1, NDEV):
            acc = acc + s_ref[j].astype(F32)
        o_ref[...] = acc

    return pl.pallas_call(
        body, name=name, grid=(r // tr,), out_shape=jax.ShapeDtypeStruct((r, cdim), F32),
        in_specs=[pl.BlockSpec((NDEV, tr, cdim), lambda i: (0, i, 0))],
        out_specs=pl.BlockSpec((tr, cdim), lambda i: (i, 0)),
        compiler_params=_params(1),
    )(slots)


def _adamw_update(w_ref, g_ref, m_ref, v_ref, d_ref, nm_ref, nv_ref):
    gv = g_ref[...]
    nm = ADAM_B1 * m_ref[...] + (1.0 - ADAM_B1) * gv
    nv = ADAM_B2 * v_ref[...] + (1.0 - ADAM_B2) * (gv * gv)
    m_hat = nm / (1.0 - ADAM_B1 ** ADAM_STEP)
    v_hat = nv / (1.0 - ADAM_B2 ** ADAM_STEP)
    d_ref[...] = -ADAM_LR * (m_hat / (jnp.sqrt(v_hat) + ADAM_EPS) + ADAM_WD * w_ref[...])
    nm_ref[...] = nm
    nv_ref[...] = nv


def _adamw_many(ws, gs, ms, vs, name):
    n = len(ws)

    def body(*refs):
        ins, outs = refs[:4 * n], refs[4 * n:]
        for k in range(n):
            _adamw_update(ins[k], ins[n + k], ins[2 * n + k], ins[3 * n + k], outs[k], outs[n + k], outs[2 * n + k])

    specs = [_full(w.shape) for w in ws]
    res = pl.pallas_call(
        body, name=name, out_shape=[jax.ShapeDtypeStruct(w.shape, F32) for w in ws] * 3,
        in_specs=specs * 4, out_specs=specs * 3,
        compiler_params=pltpu.CompilerParams(vmem_limit_bytes=VMEM_LIMIT),
    )(*ws, *gs, *ms, *vs)
    return res[:n], res[n:2 * n], res[2 * n:]


def _adamw(w, g, m, v, name):
    r, cdim = w.shape
    tr = 128 if r % 128 == 0 else r
    body = functools.partial(_adamw_update)
    spec = pl.BlockSpec((tr, cdim), lambda i: (i, 0))
    return pl.pallas_call(
        body, name=name, grid=(r // tr,), out_shape=[jax.ShapeDtypeStruct((r, cdim), F32)] * 3,
        in_specs=[spec] * 4, out_specs=[spec] * 3, compiler_params=_params(1),
    )(w, g, m, v)


def _pack(arrs, width=D, row_mult=8):
    parts = []
    for a in arrs:
        flat = a.reshape(-1)
        pad = (-flat.shape[0]) % width
        parts.append(jnp.pad(flat, (0, pad)))
    flat = jnp.concatenate(parts)
    pad = (-flat.shape[0]) % (width * row_mult)
    return jnp.pad(flat, (0, pad)).reshape(-1, width)


def _unpack(buf, shapes, width=D):
    out, row = [], 0
    for shp in shapes:
        size = math.prod(shp)
        nrow = -(-size // width)
        out.append(buf[row:row + nrow].reshape(-1)[:size].reshape(shp))
        row += nrow
    return out


def _unpack_gathered(buf, shapes, width=D):
    out, row = [], 0
    for shp in shapes:
        size = math.prod(shp)
        nrow = -(-size // width)
        out.append(buf[:, row:row + nrow].reshape(NDEV, -1)[:, :size].reshape((NDEV,) + tuple(shp)))
        row += nrow
    return out


def _block_diag(w):
    hn, dh, _ = w.shape
    eye = jnp.eye(hn, dtype=w.dtype)
    return (w[:, :, None, :] * eye[:, None, :, None]).reshape(hn * dh, hn * dh)


def _diag_blocks(wfull, hn):
    dh = wfull.shape[0] // hn
    return jnp.stack([wfull[k * dh:(k + 1) * dh, k * dh:(k + 1) * dh] for k in range(hn)])


def _ungather_cols(g, k):
    n = g.shape[1]
    return g.reshape(NDEV, k, n).transpose(1, 0, 2).reshape(k, NDEV * n)


def _pad_rows(a, rows):
    return jnp.pad(a, ((0, rows - a.shape[0]), (0, 0)))


def kernel(x, c, w_ada, b_ada, w_in, lru_conv_w, lru_conv_b, lru_w_r, lru_b_r, lru_w_i, lru_b_i, lru_lambda, conv_w, conv_b, conv_norm_g, conv_norm_b, w_out, ln1_g, ln1_b, ffn_w_up, ffn_conv_w, ffn_conv_b, ffn_w_down, ln2_g, ln2_b, loss_target, m_w_ada, m_b_ada, m_w_in, m_lru_conv_w, m_lru_conv_b, m_lru_w_r, m_lru_b_r, m_lru_w_i, m_lru_b_i, m_lru_lambda, m_conv_w, m_conv_b, m_conv_norm_g, m_conv_norm_b, m_w_out, m_ln1_g, m_ln1_b, m_ffn_w_up, m_ffn_conv_w, m_ffn_conv_b, m_ffn_w_down, m_ln2_g, m_ln2_b, v_w_ada, v_b_ada, v_w_in, v_lru_conv_w, v_lru_conv_b, v_lru_w_r, v_lru_b_r, v_lru_w_i, v_lru_b_i, v_lru_lambda, v_conv_w, v_conv_b, v_conv_norm_g, v_conv_norm_b, v_w_out, v_ln1_g, v_ln1_b, v_ffn_w_up, v_ffn_conv_w, v_ffn_conv_b, v_ffn_w_down, v_ln2_g, v_ln2_b):
    weights = dict(w_ada=w_ada, b_ada=b_ada, w_in=w_in, lru_conv_w=lru_conv_w, lru_conv_b=lru_conv_b, lru_w_r=lru_w_r,
                   lru_b_r=lru_b_r, lru_w_i=lru_w_i, lru_b_i=lru_b_i, lru_lambda=lru_lambda, conv_w=conv_w, conv_b=conv_b,
                   conv_norm_g=conv_norm_g, conv_norm_b=conv_norm_b, w_out=w_out, ln1_g=ln1_g, ln1_b=ln1_b,
                   ffn_w_up=ffn_w_up, ffn_conv_w=ffn_conv_w, ffn_conv_b=ffn_conv_b, ffn_w_down=ffn_w_down, ln2_g=ln2_g,
                   ln2_b=ln2_b)
    mom_m = dict(w_ada=m_w_ada, b_ada=m_b_ada, w_in=m_w_in, lru_conv_w=m_lru_conv_w, lru_conv_b=m_lru_conv_b,
                 lru_w_r=m_lru_w_r, lru_b_r=m_lru_b_r, lru_w_i=m_lru_w_i, lru_b_i=m_lru_b_i, lru_lambda=m_lru_lambda,
                 conv_w=m_conv_w, conv_b=m_conv_b, conv_norm_g=m_conv_norm_g, conv_norm_b=m_conv_norm_b, w_out=m_w_out,
                 ln1_g=m_ln1_g, ln1_b=m_ln1_b, ffn_w_up=m_ffn_w_up, ffn_conv_w=m_ffn_conv_w, ffn_conv_b=m_ffn_conv_b,
                 ffn_w_down=m_ffn_w_down, ln2_g=m_ln2_g, ln2_b=m_ln2_b)
    mom_v = dict(w_ada=v_w_ada, b_ada=v_b_ada, w_in=v_w_in, lru_conv_w=v_lru_conv_w, lru_conv_b=v_lru_conv_b,
                 lru_w_r=v_lru_w_r, lru_b_r=v_lru_b_r, lru_w_i=v_lru_w_i, lru_b_i=v_lru_b_i, lru_lambda=v_lru_lambda,
                 conv_w=v_conv_w, conv_b=v_conv_b, conv_norm_g=v_conv_norm_g, conv_norm_b=v_conv_norm_b, w_out=v_w_out,
                 ln1_g=v_ln1_g, ln1_b=v_ln1_b, ffn_w_up=v_ffn_w_up, ffn_conv_w=v_ffn_conv_w, ffn_conv_b=v_ffn_conv_b,
                 ffn_w_down=v_ffn_w_down, ln2_g=v_ln2_g, ln2_b=v_ln2_b)
    names = list(weights)
    bl, seq, _ = x.shape
    ntok = bl * seq
    me = 4 * lax.axis_index("x") + 2 * lax.axis_index("y") + lax.axis_index("c")

    small_shapes = [(bl, D), (LRU_K, LW // NDEV), (CONV_K, CW // NDEV), (FFN_K, DFF // NDEV)]
    small = _pack([c, lru_conv_w[0], conv_w[0], ffn_conv_w[0]], width=128)
    n_small = small.shape[0]
    small_all, win_t, wout_b = _all_gather(
        [small, w_in[0].T.astype(MXU_DTYPE), w_out[0].astype(MXU_DTYPE)], "gather_small_and_mixer_weights")
    small_all = small_all.reshape(NDEV, n_small, 128)
    c_parts, k4_parts, k31_parts, k3_parts = _unpack_gathered(small_all, small_shapes, width=128)
    c_all = c_parts.reshape(NDEV * bl, D)
    lru_conv_w_f = k4_parts.transpose(1, 0, 2).reshape(LRU_K, LW)
    conv_w_f = k31_parts.transpose(1, 0, 2).reshape(CONV_K, CW)
    ffn_conv_w_f = k3_parts.transpose(1, 0, 2).reshape(FFN_K, DFF)

    ncol = w_ada.shape[2]
    b_ada_loc = lax.dynamic_slice(b_ada, (0, me * ncol), (1, ncol))
    mod_cols = _ada_fwd(c_all, w_ada[0], b_ada_loc)
    (mod_all,) = _all_gather([mod_cols], "gather_mod")
    mod_all = mod_all.reshape(NDEV, NDEV * bl, ncol)
    mod_mine = lax.dynamic_slice(mod_all, (0, me * bl, 0), (NDEV, bl, ncol))
    mods = mod_mine.transpose(1, 0, 2).reshape(bl, 6, D)
    mods = jnp.pad(mods, ((0, 0), (0, 2), (0, 0)))

    ffn_shards = [ffn_w_up[0].T.astype(MXU_DTYPE), ffn_w_down[0].astype(MXU_DTYPE)]

    wr_bd = _block_diag(lru_w_r[0]).astype(MXU_DTYPE)
    wi_bd = _block_diag(lru_w_i[0]).astype(MXU_DTYPE)
    mavg = _block_diag(jnp.full((CW // HEAD, HEAD, HEAD), 1.0 / HEAD, F32)).astype(MXU_DTYPE)
    mavg2 = jnp.concatenate([mavg, mavg], axis=0)
    wri = jnp.concatenate([wr_bd, wi_bd], axis=1)
    va = jnp.concatenate([lru_conv_w_f, lru_conv_b, lru_b_r, lru_b_i, lru_lambda], axis=0)
    vb = _pad_rows(jnp.concatenate([conv_w_f, conv_b, conv_norm_g, conv_norm_b], axis=0), 40)
    ln1 = _pad_rows(jnp.concatenate([ln1_g, ln1_b], axis=0), 8)
    ln2 = _pad_rows(jnp.concatenate([ln2_g, ln2_b], axis=0), 8)
    f3 = _pad_rows(jnp.concatenate([ffn_conv_w_f, ffn_conv_b], axis=0), 8)

    proj, h, vc, y_b, mixed, x1, u1_b, wup_t, wdown_b = _mixer_fwd(
        x, mods, win_t, wout_b, wr_bd, wi_bd, mavg, va, vb, ln1, ffn_shards)
    hh, z_b, u2_b, y2, loss_part = _ffn_fwd(x1, mods, loss_target, wup_t, wdown_b, f3, ln2)
    loss = lax.psum(jnp.sum(loss_part) * (0.5 / D), ("x", "y", "c"))

    dx1, dy2_b, dhh_b, dln2, df3, dmod_b = _ffn_bwd(x1, y2, loss_target, hh, mods, wup_t, wdown_b, f3, ln2)
    g_down = _weight_grad(z_b.reshape(ntok, DFF), dy2_b.reshape(ntok, D), FF_CHUNK, "wgrad_down")
    g_up_t = _weight_grad(dhh_b.reshape(ntok, 2 * DFF), u2_b.reshape(ntok, D), FF_CHUNK, "wgrad_up")
    grad_x, dproj_b, dm_b, g_a, g_w31, g_wr, g_wi, dln1, dmod_a, s_up, s_down = _mixer_bwd(
        x, dx1, proj, h, vc, mixed, mods, win_t, wout_b, wri, mavg2, va, vb, ln1, [g_up_t, g_down])
    g_out = _weight_grad(y_b.reshape(ntok, D), dm_b.reshape(ntok, D), 512, "wgrad_out")
    g_in_t = _weight_grad(dproj_b.reshape(ntok, 2 * LW + 2 * CW), u1_b.reshape(ntok, D), 512, "wgrad_in")

    dmod =jnp.concatenate([dmod_a[:, 0:3, :], dmod_b[:, 0:3, :]], axis=1).reshape(bl, 6 * D)
    sharded_small = {"lru_conv_w": (1, LRU_K, LW), "conv_w": (1, CONV_K, CW), "ffn_conv_w": (1, FFN_K, DFF)}
    parts = {
        "lru_conv_w": g_a[0:4], "lru_conv_b": g_a[4:5], "lru_w_r": _diag_blocks(g_wr, LW // HEAD), "lru_b_r": g_a[5:6],
        "lru_w_i": _diag_blocks(g_wi, LW // HEAD), "lru_b_i": g_a[6:7], "lru_lambda": g_a[7:8], "conv_w": g_w31[0:CONV_K],
        "conv_b": g_a[10:11], "conv_norm_g": g_a[8:9], "conv_norm_b": g_a[9:10], "ln1_g": dln1[0:1], "ln1_b": dln1[1:2],
        "ffn_conv_w": df3[0:FFN_K], "ffn_conv_b": df3[FFN_K:FFN_K + 1], "ln2_g": dln2[0:1], "ln2_b": dln2[1:2]}
    part_names = list(parts)
    part_shapes = [sharded_small.get(k, weights[k].shape) for k in part_names]
    packed = _pack([dmod] + [parts[k] for k in part_names])
    n_rows = packed.shape[0]
    (s_in, s_out), (packed_all,) = _exchange_and_gather(
        [g_in_t.astype(WIRE_DTYPE), g_out.astype(WIRE_DTYPE)], [packed], "exchange_wgrads_gather_small_grads")
    gs_in_t, gs_out = _sum_slots(s_in, "sum_w_in"), _sum_slots(s_out, "sum_w_out")
    gs_up_t, gs_down = _sum_slots(s_up, "sum_w_up"), _sum_slots(s_down, "sum_w_down")
    packed_all = packed_all.reshape(NDEV, n_rows, D)
    summed = _sum_slots(packed_all, "sum_small_grads")
    full = dict(zip(part_names, _unpack(summed[6 * bl:], part_shapes)))
    dmod_all = packed_all[:, 0:6 * bl, :].reshape(NDEV * bl, 6 * D)

    grads = {}
    g_w_ada, g_b_ada = _ada_bwd(c_all, dmod_all, lax.dynamic_slice(dmod_all, (0, me * ncol), (NDEV * bl, ncol)))
    grads["w_ada"], grads["b_ada"] = g_w_ada[None], g_b_ada
    grads["w_in"] = gs_in_t.T[None]
    grads["w_out"] = gs_out[None]
    grads["ffn_w_up"] = gs_up_t.T[None]
    grads["ffn_w_down"] = gs_down[None]
    for k in part_names:
        gk = full[k]
        if k in sharded_small:
            nloc = gk.shape[2] // NDEV
            gk = lax.dynamic_slice(gk, (0, 0, me * nloc), (1, gk.shape[1], nloc))
        grads[k] = gk

    delta, new_m, new_v = {}, {}, {}
    big = ("w_ada", "w_in", "w_out", "ffn_w_up", "ffn_w_down")
    for k in big:
        d_, m_, v_ = _adamw(weights[k][0], grads[k][0], mom_m[k][0], mom_v[k][0], "adamw_" + k)
        delta[k], new_m[k], new_v[k] = d_[None], m_[None], v_[None]
    small_names = [k for k in names if k not in big]
    d_, m_, v_ = _adamw_many([weights[k] for k in small_names], [grads[k] for k in small_names],
                             [mom_m[k] for k in small_names], [mom_v[k] for k in small_names], "adamw_small")
    for k, dk, mk, vk in zip(small_names, d_, m_, v_):
        delta[k], new_m[k], new_v[k] = dk, mk, vk

    return (loss, grad_x, *[grads[k] for k in names], *[delta[k] for k in names], *[new_m[k] for k in names],
            *[new_v[k] for k in names])
```

```python
import functools
import math

import jax
import jax.numpy as jnp
import numpy as np
from jax import lax
from jax.experimental import pallas as pl
from jax.experimental.pallas import tpu as pltpu

NDEV = 8
D = 1024
LW = 512
CW = 512
HEAD = 64
DFF = 2816
FF_CHUNK = 1408
LRU_K = 4
CONV_K = 31
FFN_K = 3
LRU_C = 8.0
ALPHA = (2 * 1) ** 0.25
LN_EPS = 1e-5
ADAM_LR = 0.001
ADAM_B1 = 0.9
ADAM_B2 = 0.999
ADAM_EPS = 1e-08
ADAM_WD = 0.01
ADAM_STEP = 10

MXU_DTYPE = jnp.bfloat16
WIRE_DTYPE = jnp.bfloat16
TOK_TILE = 256
WG_TOK_TILE = 1024
VMEM_LIMIT = 60 * 1024 * 1024
HALO31 = 32
HALO = 8

F32 = jnp.float32
MESH = pl.DeviceIdType.MESH


def _sigmoid(x):
    return 0.5 * jnp.tanh(0.5 * x) + 0.5


def _shifted_rows(ext_ref, rot_ref, n_rows):
    span = rot_ref.shape[1]
    for r in range(1, 8):
        rot_ref[r - 1] = ext_ref[pl.ds(r, span), :]

    def window(o, r0=0, cols=slice(None)):
        q, r = divmod(o, 8)
        if r == 0:
            return ext_ref[pl.ds(8 * q + r0, n_rows), cols]
        return rot_ref[r - 1, pl.ds(8 * q + r0, n_rows), cols]

    return window


def _store_hilo(hilo_ref, rows, c0, v):
    half = hilo_ref.shape[1] // 2
    hi = v.astype(hilo_ref.dtype)
    hilo_ref[rows, c0:c0 + v.shape[1]] = hi
    hilo_ref[rows, half + c0:half + c0 + v.shape[1]] = (v - hi.astype(F32)).astype(hilo_ref.dtype)


def _scan_bwd_blocks(a_ref, g_ref, car, n_rows):
    row = lax.broadcasted_iota(jnp.int32, (8, LANE), 0)
    for c0 in range(0, a_ref.shape[1], LANE):
        cols = slice(c0, c0 + LANE)
        a_next = jnp.broadcast_to(car[0:1, cols], (8, LANE))
        g_next = jnp.broadcast_to(car[1:2, cols], (8, LANE))
        for r0 in range(n_rows - 8, -1, -8):
            a_blk, g = a_ref[r0:r0 + 8, cols], g_ref[r0:r0 + 8, cols]
            bm = jnp.where(row < 7, pltpu.roll(a_blk, 7, 0), a_next)
            for d in (1, 2, 4):
                keep = row < 8 - d
                g = g + bm * jnp.where(keep, pltpu.roll(g, 8 - d, 0), 0.0)
                bm = bm * jnp.where(keep, pltpu.roll(bm, 8 - d, 0), 1.0)
            g = g + bm * g_next
            g_ref[r0:r0 + 8, cols] = g
            a_next = jnp.broadcast_to(a_blk[0:1, :], (8, LANE))
            g_next = jnp.broadcast_to(g[0:1, :], (8, LANE))
        car[0:1, cols] = a_next[0:1, :]
        car[1:2, cols] = g_next[0:1, :]


def _scan_fwd_blocks(a_ref, u_ref, hcar, n_rows):
    row = lax.broadcasted_iota(jnp.int32, (8, LANE), 0)
    for c0 in range(0, a_ref.shape[1], LANE):
        cols = slice(c0, c0 + LANE)
        h_prev = jnp.broadcast_to(hcar[0:1, cols], (8, LANE))
        for r0 in range(0, n_rows, 8):
            a, u = a_ref[r0:r0 + 8, cols], u_ref[r0:r0 + 8, cols]
            for d in (1, 2, 4):
                keep = row >= d
                u = a * jnp.where(keep, pltpu.roll(u, d, 0), 0.0) + u
                a = a * jnp.where(keep, pltpu.roll(a, d, 0), 1.0)
            u = u + a * h_prev
            u_ref[r0:r0 + 8, cols] = u
            h_prev = jnp.broadcast_to(u[7:8, :], (8, LANE))
        hcar[0:1, cols] = h_prev[0:1, :]


def _dot(a, b):
    return jnp.dot(a, b, preferred_element_type=F32)


def _dot_nt(a, b):
    return lax.dot_general(a, b, (((1,), (1,)), ((), ())), preferred_element_type=F32)


def _dot_tn(a, b):
    return lax.dot_general(a, b, (((0,), (0,)), ((), ())), preferred_element_type=F32)


def _colsum(v):
    return jnp.sum(v, axis=0, keepdims=True)


def _rowmean(v):
    return jnp.mean(v, axis=-1, keepdims=True)


def _head_mean(v, mavg):
    hi = v.astype(MXU_DTYPE)
    lo = (v - hi.astype(F32)).astype(MXU_DTYPE)
    return _dot(hi, mavg) + _dot(lo, mavg)


_GELU_C0 = math.sqrt(2.0 / math.pi)
_GELU_C1 = 0.044715


def _gelu_and_grad(x):
    x2 = x * x
    th = jnp.tanh(_GELU_C0 * (x + _GELU_C1 * x * x2))
    ge = 0.5 * x * (1.0 + th)
    dge = 0.5 * (1.0 + th) + 0.5 * x * (1.0 - th * th) * (_GELU_C0 * (1.0 + 3.0 * _GELU_C1 * x2))
    return ge, dge


def _softplus(x):
    return jnp.maximum(x, 0.0) + jnp.log1p(jnp.exp(-jnp.abs(x)))


def _layer_norm_stats(r):
    mu = _rowmean(r)
    dl = r - mu
    var = _rowmean(dl * dl)
    rstd = lax.rsqrt(var + LN_EPS)
    return dl * rstd, rstd


def _scan_fwd(a, u, rows):
    n = a.shape[0]
    d = 1
    while d < n:
        keep = rows >= d
        a_s = jnp.where(keep, pltpu.roll(a, d, 0), 1.0)
        u_s = jnp.where(keep, pltpu.roll(u, d, 0), 0.0)
        u = a * u_s + u
        a = a * a_s
        d *= 2
    return u, a


def _scan_bwd(bm, g, rows):
    n = bm.shape[0]
    d = 1
    while d < n:
        keep = rows < n - d
        b_s = jnp.where(keep, pltpu.roll(bm, n - d, 0), 1.0)
        g_s = jnp.where(keep, pltpu.roll(g, n - d, 0), 0.0)
        g = g + bm * g_s
        bm = bm * b_s
        d *= 2
    return g, bm


def _lru_gates(xc, wr, wi, va_ref):
    xcb = xc.astype(MXU_DTYPE)
    r = _sigmoid(_dot(xcb, wr) + va_ref[5:6, :])
    ig = _sigmoid(_dot(xcb, wi) + va_ref[6:7, :])
    sp = _softplus(-va_ref[7:8, :])
    la = (-LRU_C) * r * sp
    a = jnp.exp(la)
    a2 = a * a
    mult = jnp.sqrt(-jnp.tanh(la) * (a2 + 1.0))
    return xcb, r, ig, sp, a, a2, mult


ROW_BLK = 32
LANE = 128


def _row_blocks(n_rows, rb, body, init):
    carry = init
    for j in range(n_rows // rb):
        carry = body(j * rb, carry)
    return carry


def _fold8(v):
    r, c = v.shape
    return v if r == 8 else v.reshape(r // 8, 8, c).sum(axis=0)


def _cast_rows(src_ref, dst_ref, n_rows):
    rb = min(16, n_rows)

    def rows(r0, carry):
        dst_ref[pl.ds(r0, rb), :] = src_ref[pl.ds(r0, rb), :].astype(dst_ref.dtype)
        return carry

    _row_blocks(n_rows, rb, rows, ())


def _full(shape):
    nd = len(shape)
    return pl.BlockSpec(shape, lambda *_: (0,) * nd)


_ANY = pl.BlockSpec(memory_space=pl.ANY)


def _params(n_grid):
    return pltpu.CompilerParams(dimension_semantics=("arbitrary",) * n_grid, vmem_limit_bytes=VMEM_LIMIT)


def _my_place():
    return lax.axis_index("x"), lax.axis_index("y"), lax.axis_index("c")


def _all_gather(arrays, name):
    n_arr = len(arrays)

    def body(*refs):
        start, forward, finish = _gather_steps(refs[:n_arr], refs[n_arr:2 * n_arr], *refs[2 * n_arr:])
        start()
        forward()
        finish()

    return pl.pallas_call(
        body, name=name, out_shape=_gather_out_shapes(arrays),
        in_specs=[_ANY] * n_arr, out_specs=[_ANY] * n_arr, scratch_shapes=_comm_sems(n_arr),
    )(*arrays)


def _gather_out_shapes(arrays):
    return [jax.ShapeDtypeStruct((NDEV * a.shape[0], a.shape[1]), a.dtype) for a in arrays]


def _comm_sems(n_arr):
    return [pltpu.SemaphoreType.DMA((n_arr, 7)), pltpu.SemaphoreType.DMA((n_arr, 7)), pltpu.SemaphoreType.DMA((n_arr,))]


def _gather_steps(x_refs, out_refs, send_sems, recv_sems, local_sems):
    n_arr = len(x_refs)
    x, y, c = _my_place()
    me, sibling = (x, y, c), (x, y, 1 - c)
    chips = [(1 - x, y), (x, 1 - y), (1 - x, 1 - y)]

    def rows(k, px, py, pc):
        m = x_refs[k].shape[0]
        return out_refs[k].at[pl.ds((4 * px + 2 * py + pc) * m, m), :]

    def copy(k, s, block, to, src=None):
        return pltpu.make_async_remote_copy(
            src_ref=rows(k, *block) if src is None else src, dst_ref=rows(k, *block),
            send_sem=send_sems.at[k, s], recv_sem=recv_sems.at[k, s], device_id=to, device_id_type=MESH)

    def mine():
        return [pltpu.make_async_copy(x_refs[k], rows(k, *me), local_sems.at[k]) for k in range(n_arr)]

    def first():
        cps = []
        for k in range(n_arr):
            cps.append(copy(k, 0, me, sibling, src=x_refs[k]))
            cps += [copy(k, 1 + j, me, (*chip, c), src=x_refs[k]) for j, chip in enumerate(chips)]
        return cps

    def passed():
        return [copy(k, 4 + j, (*chip, c), sibling) for k in range(n_arr) for j, chip in enumerate(chips)]

    def start():
        for cp in mine() + first():
            cp.start()

    def forward():
        fwd = passed()
        for k in range(n_arr):
            for j, chip in enumerate(chips):
                copy(k, 1 + j, (*chip, c), me).wait_recv()
                fwd[3 * k + j].start()

    def finish():
        for k in range(n_arr):
            copy(k, 0, sibling, me).wait_recv()
            for j, chip in enumerate(chips):
                copy(k, 4 + j, (*chip, 1 - c), me).wait_recv()
        for cp in first() + passed():
            cp.wait_send()
        for cp in mine():
            cp.wait()

    return start, forward, finish


def _exchange_blocks(arrays, name):
    n_arr = len(arrays)

    def body(*refs):
        start, finish = _exchange_steps(refs[:n_arr], refs[n_arr:2 * n_arr], *refs[2 * n_arr:])
        start()
        finish()

    return pl.pallas_call(
        body, name=name, out_shape=_exchange_out_shapes(arrays),
        in_specs=[_ANY] * n_arr, out_specs=[_ANY] * n_arr, scratch_shapes=_comm_sems(n_arr),
    )(*arrays)


def _exchange_and_gather(ex_arrays, ga_arrays, name):
    n_e, n_g = len(ex_arrays), len(ga_arrays)

    def body(*refs):
        e_in, g_in, refs = refs[:n_e], refs[n_e:n_e + n_g], refs[n_e + n_g:]
        e_out, g_out, sems = refs[:n_e], refs[n_e:n_e + n_g], refs[n_e + n_g:]
        e_start, e_finish = _exchange_steps(e_in, e_out, *sems[:3])
        g_start, g_forward, g_finish = _gather_steps(g_in, g_out, *sems[3:])
        g_start()
        e_start()
        g_forward()
        g_finish()
        e_finish()

    res = pl.pallas_call(
        body, name=name, out_shape=_exchange_out_shapes(ex_arrays) + _gather_out_shapes(ga_arrays),
        in_specs=[_ANY] * (n_e + n_g), out_specs=[_ANY] * (n_e + n_g),
        scratch_shapes=_comm_sems(n_e) + _comm_sems(n_g),
    )(*ex_arrays, *ga_arrays)
    return res[:n_e], res[n_e:]


def _exchange_out_shapes(arrays):
    return [jax.ShapeDtypeStruct((NDEV, a.shape[0] // NDEV, a.shape[1]), a.dtype) for a in arrays]


def _exchange_steps(g_refs, out_refs, send_sems, recv_sems, local_sems):
    n_arr = len(g_refs)
    x, y, c = _my_place()
    me = 4 * x + 2 * y + c

    def copy(k, rel):
        px, py, pc = x ^ ((rel >> 2) & 1), y ^ ((rel >> 1) & 1), c ^ (rel & 1)
        m = out_refs[k].shape[1]
        return pltpu.make_async_remote_copy(
            src_ref=g_refs[k].at[pl.ds((4 * px + 2 * py + pc) * m, m), :], dst_ref=out_refs[k].at[me],
            send_sem=send_sems.at[k, rel - 1], recv_sem=recv_sems.at[k, rel - 1],
            device_id=(px, py, pc), device_id_type=MESH)

    def copies():
        cps = []
        for k in range(n_arr):
            m = out_refs[k].shape[1]
            cps.append(pltpu.make_async_copy(g_refs[k].at[pl.ds(me * m, m), :], out_refs[k].at[me], local_sems.at[k]))
            cps += [copy(k, rel) for rel in range(1, NDEV)]
        return cps

    def start():
        for cp in copies():
            cp.start()

    def finish():
        for cp in copies():
            cp.wait()

    return start, finish


def _ada_fwd(c_all, w_ada_loc, b_ada_loc):
    def body(c_ref, w_ref, b_ref, o_ref):
        cv = c_ref[...]
        ca = (cv * _sigmoid(cv)).astype(MXU_DTYPE)
        o_ref[...] = _dot(ca, w_ref[...].astype(MXU_DTYPE)) + b_ref[...]

    return pl.pallas_call(
        body, name="ada_fwd", out_shape=jax.ShapeDtypeStruct((c_all.shape[0], w_ada_loc.shape[1]), F32),
        in_specs=[_full(c_all.shape), _full(w_ada_loc.shape), _full(b_ada_loc.shape)],
        out_specs=_full((c_all.shape[0], w_ada_loc.shape[1])),
        compiler_params=pltpu.CompilerParams(vmem_limit_bytes=VMEM_LIMIT),
    )(c_all, w_ada_loc, b_ada_loc)


def _ada_bwd(c_all, dmod_all, dmod_cols):
    def body(c_ref, da_ref, d_ref, o_ref, b_ref):
        cv = c_ref[...]
        ca = (cv * _sigmoid(cv)).astype(MXU_DTYPE)
        o_ref[...] = _dot_tn(ca, d_ref[...].astype(MXU_DTYPE))
        b_ref[...] = _colsum(da_ref[...])

    return pl.pallas_call(
        body, name="ada_bwd",
        out_shape=[jax.ShapeDtypeStruct((c_all.shape[1], dmod_cols.shape[1]), F32),
                   jax.ShapeDtypeStruct((1, dmod_all.shape[1]), F32)],
        in_specs=[_full(c_all.shape), _full(dmod_all.shape), _full(dmod_cols.shape)],
        out_specs=[_full((c_all.shape[1], dmod_cols.shape[1])), _full((1, dmod_all.shape[1]))],
        compiler_params=pltpu.CompilerParams(vmem_limit_bytes=VMEM_LIMIT),
    )(c_all, dmod_all, dmod_cols)


def _mixer_fwd(x, mods, win_t, wout, wri, mavg2, va, vb, ln1, ffn_shards):
    bl, seq, _ = x.shape
    t = min(TOK_TILE, seq)
    nt = seq // t
    n_g = len(ffn_shards)
    rb = min(ROW_BLK, t)

    def body(x_ref, mod_ref, win_hbm, wout_hbm, wri_ref, mavg_ref, va_ref, vb_ref, ln1_ref, *rest):
        shard_refs, rest = rest[:n_g], rest[n_g:]
        proj_ref, h_ref, vc_ref, y_ref, mixed_ref, x1_ref, u1_ref = rest[:7]
        gathered_refs, rest = rest[7:7 + n_g], rest[7 + n_g:]
        (win_v, wout_v, xa_ext, vg_ext, rot_a, hcar, sems, rot_c, xc_scr, a_scr, st_a, st_b, xcb_scr, gates, hilo,
         g_send, g_recv, g_local) = rest
        b, i = pl.program_id(0), pl.program_id(1)
        step = b * nt + i
        g_start, g_forward, g_finish = _gather_steps(shard_refs, gathered_refs, g_send, g_recv, g_local)

        @pl.when(step == 0)
        def _():
            g_start()
            cps = [pltpu.make_async_copy(win_hbm, win_v, sems.at[0]), pltpu.make_async_copy(wout_hbm, wout_v, sems.at[1])]
            for cp in cps:
                cp.start()
            for cp in cps:
                cp.wait()

        pl.when(step == (bl * nt) // 2)(g_forward)

        @pl.when(i == 0)
        def _():
            xa_ext[0:HALO, :] = jnp.zeros((HALO, LW), F32)
            vg_ext[0:HALO31, :] = jnp.zeros((HALO31, CW), F32)
            hcar[...] = jnp.zeros_like(hcar)

        blocks = [(r0, c0) for c0 in range(0, LW, LANE) for r0 in range(0, t, rb)]

        def in_rows(r0, acc):
            rows = pl.ds(r0, 16)
            u1_ref[rows, :] = (x_ref[rows, :] * (1.0 + mod_ref[1:2, :]) + mod_ref[0:1, :]).astype(MXU_DTYPE)
            return acc

        _row_blocks(t, min(16, t), in_rows, ())
        proj_ref[...] = _dot_nt(u1_ref[...], win_v[...])

        xa_ext[HALO:HALO + t, :] = proj_ref[:, 0:LW]
        for k in range(LRU_K - 1):
            rot_c[k] = xa_ext[pl.ds(HALO - (LRU_K - 1) + k, t), :]
        for r0, c0 in blocks:
            rows, cols = pl.ds(r0, rb), slice(c0, c0 + LANE)
            xc = va_ref[4:5, cols] + va_ref[LRU_K - 1:LRU_K, cols] * xa_ext[pl.ds(HALO + r0, rb), cols]
            for k in range(LRU_K - 1):
                xc = xc + va_ref[k:k + 1, cols] * rot_c[k, rows, cols]
            xc_scr[rows, cols] = xc
            xcb_scr[rows, cols] = xc.astype(MXU_DTYPE)
        xa_ext[0:HALO, :] = xa_ext[t:t + HALO, :]
        gates[...] = _dot(xcb_scr[...], wri_ref[...])
        for r0, c0 in blocks:
            rows, cols = pl.ds(r0, rb), slice(c0, c0 + LANE)
            r = _sigmoid(gates[rows, cols] + va_ref[5:6, cols])
            ig = _sigmoid(gates[rows, LW + c0:LW + c0 + LANE] + va_ref[6:7, cols])
            la = (-LRU_C) * r * _softplus(-va_ref[7:8, cols])
            a = jnp.exp(la)
            a_scr[rows, cols] = a
            h_ref[rows, cols] = jnp.sqrt(-jnp.tanh(la) * (a * a + 1.0)) * (ig * xc_scr[rows, cols])
        _scan_fwd_blocks(a_scr, h_ref, hcar, t)

        for r0, c0 in blocks:
            rows, cols, pc = pl.ds(r0, rb), slice(c0, c0 + LANE), 2 * LW + c0
            vg_ext[pl.ds(HALO31 + r0, rb), cols] = (
                proj_ref[rows, pc:pc + LANE] * _sigmoid(proj_ref[rows, pc + CW:pc + CW + LANE]))
        vg_win = _shifted_rows(vg_ext, rot_a, rb)
        for r0, c0 in blocks:
            rows, cols = pl.ds(r0, rb), slice(c0, c0 + LANE)
            vc = vb_ref[CONV_K:CONV_K + 1, cols]
            for k in range(CONV_K):
                vc = vc + vb_ref[k:k + 1, cols] * vg_win(HALO31 - (CONV_K - 1) + k, r0, cols)
            vc_ref[rows, cols] = vc
            _store_hilo(hilo, rows, c0, vc)
        vg_ext[0:HALO31, :] = vg_ext[t:t + HALO31, :]
        mavg2 = mavg_ref[...]
        st_a[...] = _dot(hilo[...], mavg2)
        for r0, c0 in blocks:
            rows, cols = pl.ds(r0, rb), slice(c0, c0 + LANE)
            dl = vc_ref[rows, cols] - st_a[rows, cols]
            st_a[rows, cols] = dl
            _store_hilo(hilo, rows, c0, dl * dl)
        st_b[...] = _dot(hilo[...], mavg2)
        for r0, c0 in blocks:
            rows, cols = pl.ds(r0, rb), slice(c0, c0 + LANE)
            ge, _ = _gelu_and_grad(proj_ref[rows, LW + c0:LW + c0 + LANE])
            y_ref[rows, cols] = (ge * h_ref[rows, cols]).astype(MXU_DTYPE)
            yl = (st_a[rows, cols] * lax.rsqrt(st_b[rows, cols] + LN_EPS) * vb_ref[CONV_K + 1:CONV_K + 2, cols]
                  + vb_ref[CONV_K + 2:CONV_K + 3, cols])
            y_ref[rows, LW + c0:LW + c0 + LANE] = (yl * _sigmoid(yl)).astype(MXU_DTYPE)
        mixed_ref[...] = _dot(y_ref[...], wout_v[...])

        def out_rows(r0, acc):
            rows = pl.ds(r0, 8)
            xh, _ = _layer_norm_stats(ALPHA * x_ref[rows, :] + (1.0 + mod_ref[2:3, :]) * mixed_ref[rows, :])
            x1_ref[rows, :] = xh * ln1_ref[0:1, :] + ln1_ref[1:2, :]
            return acc

        _row_blocks(t, 8, out_rows, ())

        pl.when(step == bl * nt - 1)(g_finish)

    tok = lambda w: pl.BlockSpec((None, t, w), lambda b, i: (b, i, 0))
    outs = [(2 * LW + 2 * CW, F32), (LW, F32), (CW, F32), (D, MXU_DTYPE), (D, F32), (D, F32), (D, MXU_DTYPE)]
    return pl.pallas_call(
        body, name="mixer_fwd", grid=(bl, nt),
        out_shape=[jax.ShapeDtypeStruct((bl, seq, w), dt) for w, dt in outs] + _gather_out_shapes(ffn_shards),
        in_specs=[tok(D), pl.BlockSpec((None, 8, D), lambda b, i: (b, 0, 0)), _ANY, _ANY,
                  _full(wri.shape), _full(mavg2.shape), _full(va.shape), _full(vb.shape), _full(ln1.shape)]
        + [_ANY] * n_g,
        out_specs=[tok(w) for w, _ in outs] + [_ANY] * n_g,
        scratch_shapes=[pltpu.VMEM(win_t.shape, MXU_DTYPE), pltpu.VMEM(wout.shape, MXU_DTYPE),
                        pltpu.VMEM((t + HALO, LW), F32), pltpu.VMEM((t + HALO31, CW), F32),
                        pltpu.VMEM((7, t + HALO31 - 8, CW), F32), pltpu.VMEM((8, LW), F32),
                        pltpu.SemaphoreType.DMA((2,)), pltpu.VMEM((LRU_K - 1, t, LW), F32)]
        + [pltpu.VMEM((t, LW), F32)] * 4
        + [pltpu.VMEM((t, LW), MXU_DTYPE), pltpu.VMEM((t, 2 * LW), F32), pltpu.VMEM((t, 2 * CW), MXU_DTYPE)]
        + _comm_sems(n_g),
        compiler_params=_params(2),
    )(x, mods, win_t, wout, wri, mavg2, va, vb, ln1, *ffn_shards)


def _ffn_fwd(x1, mods, tgt, wup_t, wdown, f3, ln2):
    bl, seq, _ = x1.shape
    t = min(TOK_TILE, seq)
    nt = seq // t
    n_chunk = DFF // FF_CHUNK

    rb = min(ROW_BLK, t)

    def body(x1_ref, mod_ref, tgt_ref, wup_hbm, wdown_hbm, f3_ref, ln2_ref,
             hh_ref, z_ref, u2_ref, y2_ref, loss_ref,
             wup_v, wdown_v, g_ext, rot, sems):
        b, i = pl.program_id(0), pl.program_id(1)

        @pl.when((b == 0) & (i == 0))
        def _():
            cps = [pltpu.make_async_copy(wup_hbm, wup_v, sems.at[0]), pltpu.make_async_copy(wdown_hbm, wdown_v, sems.at[1])]
            for cp in cps:
                cp.start()
            for cp in cps:
                cp.wait()
            loss_ref[...] = jnp.zeros_like(loss_ref)

        @pl.when(i == 0)
        def _():
            for ch in range(n_chunk):
                g_ext[ch, 0:HALO, :] = jnp.zeros((HALO, FF_CHUNK), F32)

        def in_rows(r0, acc):
            rows = pl.ds(r0, 16)
            u2_ref[rows, :] = (x1_ref[rows, :] * (1.0 + mod_ref[4:5, :]) + mod_ref[3:4, :]).astype(MXU_DTYPE)
            return acc

        _row_blocks(t, min(16, t), in_rows, ())
        for ch in range(n_chunk):
            lo = ch * FF_CHUNK
            hh_ref[:, lo:lo + FF_CHUNK] = _dot_nt(u2_ref[...], wup_v[lo:lo + FF_CHUNK, :])
            hh_ref[:, DFF + lo:DFF + lo + FF_CHUNK] = _dot_nt(u2_ref[...], wup_v[DFF + lo:DFF + lo + FF_CHUNK, :])
            g_ext[ch, HALO:HALO + t, :] = hh_ref[:, DFF + lo:DFF + lo + FF_CHUNK]
            for k in range(FFN_K - 1):
                rot[k] = g_ext[ch, pl.ds(HALO - (FFN_K - 1) + k, t), :]
            for cb in range(FF_CHUNK // LANE):
                cols = slice(cb * LANE, (cb + 1) * LANE)
                fcols = slice(lo + cb * LANE, lo + (cb + 1) * LANE)
                for r0 in range(0, t, rb):
                    rows = pl.ds(r0, rb)
                    gc = (f3_ref[FFN_K:FFN_K + 1, fcols] + f3_ref[0:1, fcols] * rot[0, rows, cols]
                          + f3_ref[1:2, fcols] * rot[1, rows, cols]
                          + f3_ref[2:3, fcols] * g_ext[ch, pl.ds(HALO + r0, rb), cols])
                    z_ref[rows, fcols] = (gc * _sigmoid(gc) * hh_ref[rows, fcols]).astype(MXU_DTYPE)
            g_ext[ch, 0:HALO, :] = g_ext[ch, t:t + HALO, :]
        y2_ref[...] = _dot(z_ref[...], wdown_v[...])

        def out_rows(r0, acc):
            rows = pl.ds(r0, 8)
            xh, _ = _layer_norm_stats(ALPHA * x1_ref[rows, :] + (1.0 + mod_ref[5:6, :]) * y2_ref[rows, :])
            err = xh * ln2_ref[0:1, :] + ln2_ref[1:2, :] - tgt_ref[rows, :]
            return acc + err * err

        e2 = _row_blocks(t, 8, out_rows, jnp.zeros((8, D), F32))
        part = e2[:, 0:128]
        for j in range(1, D // 128):
            part = part + e2[:, 128 * j:128 * (j + 1)]
        loss_ref[...] += part

    tok = lambda w: pl.BlockSpec((None, t, w), lambda b, i: (b, i, 0))
    outs = [(2 * DFF, F32), (DFF, MXU_DTYPE), (D, MXU_DTYPE), (D, F32)]
    return pl.pallas_call(
        body, name="ffn_fwd", grid=(bl, nt),
        out_shape=[jax.ShapeDtypeStruct((bl, seq, w), dt) for w, dt in outs] + [jax.ShapeDtypeStruct((8, 128), F32)],
        in_specs=[tok(D), pl.BlockSpec((None, 8, D), lambda b, i: (b, 0, 0)), tok(D), _ANY, _ANY,
                  _full(f3.shape), _full(ln2.shape)],
        out_specs=[tok(w) for w, _ in outs] + [_full((8, 128))],
        scratch_shapes=[pltpu.VMEM(wup_t.shape, MXU_DTYPE), pltpu.VMEM(wdown.shape, MXU_DTYPE),
                        pltpu.VMEM((n_chunk, t + HALO, FF_CHUNK), F32), pltpu.VMEM((FFN_K - 1, t, FF_CHUNK), F32),
                        pltpu.SemaphoreType.DMA((2,))],
        compiler_params=_params(2),
    )(x1, mods, tgt, wup_t, wdown, f3, ln2)


def _ffn_bwd(x1, y2, tgt, hh, mods, wup_t, wdown, f3, ln2):
    bl, seq, _ = x1.shape
    t = min(TOK_TILE, seq)
    nt = seq // t
    n_chunk = DFF // FF_CHUNK
    rb = min(ROW_BLK, t)

    def body(x1_ref, y2_ref, tgt_ref, hh_ref, halo_ref, mod_ref, wup_hbm, wdown_hbm, f3_ref, ln2_ref,
             dx1_ref, dy2_ref, dhh_ref, dln2_ref, df3_ref, dmod_ref,
             wup_v, wdown_v, g_ext, dgc_ext, rot, dz_scr, wide_scr, sems):
        b, i = pl.program_id(0), pl.program_id(1)
        tt = nt - 1 - i

        @pl.when((b == 0) & (i == 0))
        def _():
            cps = [pltpu.make_async_copy(wup_hbm, wup_v, sems.at[0]), pltpu.make_async_copy(wdown_hbm, wdown_v, sems.at[1])]
            for cp in cps:
                cp.start()
            for cp in cps:
                cp.wait()
            dln2_ref[...] = jnp.zeros_like(dln2_ref)
            df3_ref[...] = jnp.zeros_like(df3_ref)

        @pl.when(i == 0)
        def _():
            dmod_ref[...] = jnp.zeros_like(dmod_ref)
            for ch in range(n_chunk):
                dgc_ext[ch, t:t + HALO, :] = jnp.zeros((HALO, FF_CHUNK), F32)

        def ln_rows(r0, acc):
            rows = pl.ds(r0, 8)
            x1v, y2v = x1_ref[rows, :], y2_ref[rows, :]
            gt = mod_ref[5:6, :]
            xh, rstd = _layer_norm_stats(ALPHA * x1v + (1.0 + gt) * y2v)
            g2 = ln2_ref[0:1, :]
            dx2 = (xh * g2 + ln2_ref[1:2, :] - tgt_ref[rows, :]) * (1.0 / D)
            dxh = dx2 * g2
            dr2 = rstd * (dxh - _rowmean(dxh) - xh * _rowmean(dxh * xh))
            wide_scr[rows, :] = (1.0 + gt) * dr2
            dx1_ref[rows, :] = ALPHA * dr2
            return acc[0] + dx2 * xh, acc[1] + dx2, acc[2] + dr2 * y2v

        zero_d = jnp.zeros((8, D), F32)
        acc = _row_blocks(t, 8, ln_rows, (zero_d, zero_d, zero_d))
        dln2_ref[0:1, :] += _colsum(acc[0])
        dln2_ref[1:2, :] += _colsum(acc[1])
        dmod_ref[2:3, :] += _colsum(acc[2])
        _cast_rows(wide_scr, dy2_ref, t)

        halo_keep = (tt > 0).astype(F32)
        zero_l = jnp.zeros((8, LANE), F32)
        for ch in range(n_chunk):
            lo = ch * FF_CHUNK
            dz_scr[...] = _dot_nt(dy2_ref[...], wdown_v[lo:lo + FF_CHUNK, :])
            g_ext[0:HALO, :] = halo_ref[:, lo:lo + FF_CHUNK] * halo_keep
            g_ext[HALO:HALO + t, :] = hh_ref[:, DFF + lo:DFF + lo + FF_CHUNK]
            for k in range(FFN_K - 1):
                rot[k] = g_ext[pl.ds(HALO - (FFN_K - 1) + k, t), :]
            for cb in range(FF_CHUNK // LANE):
                cols = slice(cb * LANE, (cb + 1) * LANE)
                fcols = slice(lo + cb * LANE, lo + (cb + 1) * LANE)

                def gate_rows(r0, acc, cols=cols, fcols=fcols, ch=ch):
                    rows = pl.ds(r0, rb)
                    taps = [rot[0, rows, cols], rot[1, rows, cols], g_ext[pl.ds(r0 + HALO, rb), cols]]
                    gc = f3_ref[FFN_K:FFN_K + 1, fcols]
                    for k in range(FFN_K):
                        gc = gc + f3_ref[k:k + 1, fcols] * taps[k]
                    s = _sigmoid(gc)
                    dz = dz_scr[rows, cols]
                    dhh_ref[rows, fcols] = (dz * (gc * s)).astype(MXU_DTYPE)
                    dgc = dz * hh_ref[rows, fcols] * (s * (1.0 + gc * (1.0 - s)))
                    dgc_ext[ch, rows, cols] = dgc
                    return tuple(acc[k] + _fold8(dgc * taps[k]) for k in range(FFN_K)) + (acc[FFN_K] + _fold8(dgc),)

                acc = _row_blocks(t, rb, gate_rows, (zero_l,) * (FFN_K + 1))
                for k in range(FFN_K + 1):
                    df3_ref[k:k + 1, fcols] += _colsum(acc[k])
            for k in range(FFN_K - 1):
                rot[k] = dgc_ext[ch, pl.ds(k + 1, t), :]
            for cb in range(FF_CHUNK // LANE):
                cols = slice(cb * LANE, (cb + 1) * LANE)
                fcols = slice(lo + cb * LANE, lo + (cb + 1) * LANE)
                gcols = slice(DFF + lo + cb * LANE, DFF + lo + (cb + 1) * LANE)

                def dg_rows(r0, acc, cols=cols, fcols=fcols, gcols=gcols, ch=ch):
                    rows = pl.ds(r0, rb)
                    dg = (f3_ref[2:3, fcols] * dgc_ext[ch, rows, cols] + f3_ref[1:2, fcols] * rot[0, rows, cols]
                          + f3_ref[0:1, fcols] * rot[1, rows, cols])
                    dhh_ref[rows, gcols] = dg.astype(MXU_DTYPE)
                    return acc

                _row_blocks(t, rb, dg_rows, ())
            dgc_ext[ch, t:t + HALO, :] = dgc_ext[ch, 0:HALO, :]

        wide_scr[...] = _dot(dhh_ref[...], wup_v[...])

        def out_rows(r0, acc):
            rows = pl.ds(r0, 8)
            du2 = wide_scr[rows, :]
            dx1_ref[rows, :] = dx1_ref[rows, :] + du2 * (1.0 + mod_ref[4:5, :])
            return acc[0] + du2 * x1_ref[rows, :], acc[1] + du2

        acc = _row_blocks(t, 8, out_rows, (zero_d, zero_d))
        dmod_ref[1:2, :] += _colsum(acc[0])
        dmod_ref[0:1, :] += _colsum(acc[1])

    rev = lambda w: pl.BlockSpec((None, t, w), lambda b, i: (b, nt - 1 - i, 0))
    halo = pl.BlockSpec((None, HALO, DFF), lambda b, i: (b, jnp.maximum((nt - 1 - i) * (t // HALO) - 1, 0), 1))
    return pl.pallas_call(
        body, name="ffn_bwd", grid=(bl, nt),
        out_shape=[jax.ShapeDtypeStruct((bl, seq, D), F32), jax.ShapeDtypeStruct((bl, seq, D), MXU_DTYPE),
                   jax.ShapeDtypeStruct((bl, seq, 2 * DFF), MXU_DTYPE), jax.ShapeDtypeStruct((8, D), F32),
                   jax.ShapeDtypeStruct((8, DFF), F32), jax.ShapeDtypeStruct((bl, 8, D), F32)],
        in_specs=[rev(D), rev(D), rev(D), rev(2 * DFF), halo, pl.BlockSpec((None, 8, D), lambda b, i: (b, 0, 0)),
                  _ANY, _ANY, _full(f3.shape), _full(ln2.shape)],
        out_specs=[rev(D), rev(D), rev(2 * DFF), _full((8, D)), _full((8, DFF)),
                   pl.BlockSpec((None, 8, D), lambda b, i: (b, 0, 0))],
        scratch_shapes=[pltpu.VMEM(wup_t.shape, MXU_DTYPE), pltpu.VMEM(wdown.shape, MXU_DTYPE),
                        pltpu.VMEM((t + HALO, FF_CHUNK), F32), pltpu.VMEM((n_chunk, t + HALO, FF_CHUNK), F32),
                        pltpu.VMEM((FFN_K - 1, t, FF_CHUNK), F32), pltpu.VMEM((t, FF_CHUNK), F32),
                        pltpu.VMEM((t, D), F32), pltpu.SemaphoreType.DMA((2,))],
        compiler_params=_params(2),
    )(x1, y2, tgt, hh, hh, mods, wup_t, wdown, f3, ln2)


def _mixer_bwd(x, dx1, proj, h, vc, mixed, mods, win_t, wout, wri, mavg2, va, vb, ln1, ffn_wgrads):
    bl, seq, _ = x.shape
    t = min(TOK_TILE, seq)
    nt = seq // t
    pw = 2 * LW + 2 * CW
    n_g = len(ffn_wgrads)
    rb = min(ROW_BLK, t)
    half_f32 = [pltpu.VMEM((t, LW), F32)] * 10

    def body(x_ref, dx1_ref, proj_ref, phalo_ref, h_ref, hhalo_ref, vc_ref, mixed_ref, mod_ref,
             win_hbm, wout_hbm, wri_ref, mavg_ref, va_ref, vb_ref, ln1_ref, *rest):
        wgrad_refs, rest = rest[:n_g], rest[n_g:]
        gx_ref, dproj_ref, dm_ref, ga_ref, gw31_ref, dwr_ref, dwi_ref, dln1_ref, dmod_ref = rest[:9]
        slot_refs, rest = rest[9:9 + n_g], rest[9 + n_g:]
        (win_v, wout_v, xa_ext, vg_scr, dvc_ext, dxc_ext, rot_b, car, sems,
         st_a, st_b, st_c, yn_scr, dyn_scr, hprev_scr, xc_scr, a_scr, mult_scr, g_scr,
         wide, gates, hilo_a, hilo_b, dgates, xcb_scr, rot_c, rot_d, hp_ext, e_send, e_recv, e_local) = rest
        b, i = pl.program_id(0), pl.program_id(1)
        tt = nt - 1 - i
        e_start, e_finish = _exchange_steps(wgrad_refs, slot_refs, e_send, e_recv, e_local)

        @pl.when((b == 0) & (i == 0))
        def _():
            e_start()
            cps = [pltpu.make_async_copy(win_hbm, win_v, sems.at[0]), pltpu.make_async_copy(wout_hbm, wout_v, sems.at[1])]
            for cp in cps:
                cp.start()
            for cp in cps:
                cp.wait()
            for ref in (ga_ref, gw31_ref, dwr_ref, dwi_ref, dln1_ref):
                ref[...] = jnp.zeros_like(ref)

        @pl.when(i == 0)
        def _():
            dmod_ref[...] = jnp.zeros_like(dmod_ref)
            dvc_ext[t:t + HALO31, :] = jnp.zeros((HALO31, CW), F32)
            dxc_ext[t:t + HALO, :] = jnp.zeros((HALO, LW), F32)
            car[...] = jnp.zeros_like(car)

        halo_keep = (tt > 0).astype(F32)
        zero_d = jnp.zeros((8, D), F32)
        zero_l = jnp.zeros((8, LANE), F32)
        blocks = [(r0, c0) for c0 in range(0, LW, LANE) for r0 in range(0, t, rb)]

        def add_row(ref, row, c0, acc):
            ref[row:row + 1, c0:c0 + LANE] += _colsum(acc)

        def ln_rows(r0, acc):
            rows = pl.ds(r0, 8)
            xv, mixed = x_ref[rows, :], mixed_ref[rows, :]
            gt = mod_ref[2:3, :]
            xh, rstd = _layer_norm_stats(ALPHA * xv + (1.0 + gt) * mixed)
            dx1 = dx1_ref[rows, :]
            dxh = dx1 * ln1_ref[0:1, :]
            dr1 = rstd * (dxh - _rowmean(dxh) - xh * _rowmean(dxh * xh))
            wide[rows, :] = (1.0 + gt) * dr1
            gx_ref[rows, :] = ALPHA * dr1
            return acc[0] + dx1 * xh, acc[1] + dx1, acc[2] + dr1 * mixed

        acc = _row_blocks(t, 8, ln_rows, (zero_d, zero_d, zero_d))
        dln1_ref[0:1, :] += _colsum(acc[0])
        dln1_ref[1:2, :] += _colsum(acc[1])
        dmod_ref[2:3, :] += _colsum(acc[2])
        _cast_rows(wide, dm_ref, t)
        wide[...] = _dot_nt(dm_ref[...], wout_v[...])

        mavg2 = mavg_ref[...]
        for r0, c0 in blocks:
            rows, cols = pl.ds(r0, rb), slice(c0, c0 + LANE)
            _store_hilo(hilo_a, rows, c0, vc_ref[rows, cols])
        st_a[...] = _dot(hilo_a[...], mavg2)
        for r0, c0 in blocks:
            rows, cols = pl.ds(r0, rb), slice(c0, c0 + LANE)
            dl = vc_ref[rows, cols] - st_a[rows, cols]
            yn_scr[rows, cols] = dl
            _store_hilo(hilo_a, rows, c0, dl * dl)
        st_b[...] = _dot(hilo_a[...], mavg2)
        for c0 in range(0, CW, LANE):
            cols, pc = slice(c0, c0 + LANE), 2 * LW + c0
            ng, nb = vb_ref[CONV_K + 1:CONV_K + 2, cols], vb_ref[CONV_K + 2:CONV_K + 3, cols]
            acc_g, acc_b = zero_l, zero_l
            for r0 in range(0, t, rb):
                rows = pl.ds(r0, rb)
                rs = lax.rsqrt(st_b[rows, cols] + LN_EPS)
                yn = yn_scr[rows, cols] * rs
                yl = yn * ng + nb
                s = _sigmoid(yl)
                dyl = wide[rows, LW + c0:LW + c0 + LANE] * (s * (1.0 + yl * (1.0 - s)))
                acc_g, acc_b = acc_g + _fold8(dyl * yn), acc_b + _fold8(dyl)
                dyn = dyl * ng
                st_b[rows, cols] = rs
                yn_scr[rows, cols] = yn
                dyn_scr[rows, cols] = dyn
                _store_hilo(hilo_a, rows, c0, dyn)
                _store_hilo(hilo_b, rows, c0, dyn * yn)
                vg_scr[rows, cols] = proj_ref[rows, pc:pc + LANE] * _sigmoid(proj_ref[rows, pc + CW:pc + CW + LANE])
            add_row(ga_ref, 8, c0, acc_g)
            add_row(ga_ref, 9, c0, acc_b)
        st_a[...] = _dot(hilo_a[...], mavg2)
        st_c[...] = _dot(hilo_b[...], mavg2)
        for c0 in range(0, CW, LANE):
            cols = slice(c0, c0 + LANE)
            acc_b = zero_l
            for r0 in range(0, t, rb):
                rows = pl.ds(r0, rb)
                dvc = st_b[rows, cols] * (dyn_scr[rows, cols] - st_a[rows, cols] - yn_scr[rows, cols] * st_c[rows, cols])
                dvc_ext[rows, cols] = dvc
                acc_b = acc_b + _fold8(dvc)
            add_row(ga_ref, 10, c0, acc_b)
        dvc_win = _shifted_rows(dvc_ext, rot_b, rb)
        for c0 in range(0, CW, LANE):
            cols, pc = slice(c0, c0 + LANE), 2 * LW + c0
            for k in range(CONV_K):
                acc_w = zero_l
                for r0 in range(0, t, rb):
                    acc_w = acc_w + _fold8(vg_scr[pl.ds(r0, rb), cols] * dvc_win(CONV_K - 1 - k, r0, cols))
                add_row(gw31_ref, k, c0, acc_w)
            for r0 in range(0, t, rb):
                rows = pl.ds(r0, rb)
                dvg = jnp.zeros((rb, LANE), F32)
                for k in range(CONV_K):
                    dvg = dvg + vb_ref[k:k + 1, cols] * dvc_win(CONV_K - 1 - k, r0, cols)
                vbr = proj_ref[rows, pc:pc + LANE]
                sgb = _sigmoid(proj_ref[rows, pc + CW:pc + CW + LANE])
                dproj_ref[rows, pc:pc + LANE] = (dvg * sgb).astype(MXU_DTYPE)
                dproj_ref[rows, pc + CW:pc + CW + LANE] = (dvg * vbr * sgb * (1.0 - sgb)).astype(MXU_DTYPE)
        dvc_ext[t:t + HALO31, :] = dvc_ext[0:HALO31, :]

        xa_ext[0:HALO, :] = phalo_ref[HALO31 - HALO:HALO31, 0:LW] * halo_keep
        xa_ext[HALO:HALO + t, :] = proj_ref[:, 0:LW]
        for k in range(LRU_K - 1):
            rot_c[k] = xa_ext[pl.ds(HALO - (LRU_K - 1) + k, t), :]
        hp_ext[0:HALO, :] = hhalo_ref[...] * halo_keep
        hp_ext[HALO:HALO + t, :] = h_ref[...]
        hprev_scr[...] = hp_ext[pl.ds(HALO - 1, t), :]

        def xa_tap(k, rows, r0, cols):
            return xa_ext[pl.ds(HALO + r0, rb), cols] if k == LRU_K - 1 else rot_c[k, rows, cols]

        for r0, c0 in blocks:
            rows, cols = pl.ds(r0, rb), slice(c0, c0 + LANE)
            xc = va_ref[4:5, cols]
            for k in range(LRU_K):
                xc = xc + va_ref[k:k + 1, cols] * xa_tap(k, rows, r0, cols)
            xc_scr[rows, cols] = xc
            xcb_scr[rows, cols] = xc.astype(MXU_DTYPE)
        gates[...] = _dot(xcb_scr[...], wri_ref[...])
        for r0, c0 in blocks:
            rows, cols = pl.ds(r0, rb), slice(c0, c0 + LANE)
            r = _sigmoid(gates[rows, cols] + va_ref[5:6, cols])
            ig = _sigmoid(gates[rows, LW + c0:LW + c0 + LANE] + va_ref[6:7, cols])
            la = (-LRU_C) * r * _softplus(-va_ref[7:8, cols])
            a = jnp.exp(la)
            gates[rows, cols] = r
            gates[rows, LW + c0:LW + c0 + LANE] = ig
            a_scr[rows, cols] = a
            mult_scr[rows, cols] = jnp.sqrt(-jnp.tanh(la) * (a * a + 1.0))
            ge, dge = _gelu_and_grad(proj_ref[rows, LW + c0:LW + c0 + LANE])
            dya = wide[rows, cols]
            dproj_ref[rows, LW + c0:LW + c0 + LANE] = (dya * h_ref[rows, cols] * dge).astype(MXU_DTYPE)
            g_scr[rows, cols] = dya * ge
        _scan_bwd_blocks(a_scr, g_scr, car, t)
        for c0 in range(0, LW, LANE):
            cols = slice(c0, c0 + LANE)
            sp = _softplus(-va_ref[7:8, cols])
            acc_l, acc_r, acc_i = zero_l, zero_l, zero_l
            for r0 in range(0, t, rb):
                rows = pl.ds(r0, rb)
                gv, xc, a, mult = g_scr[rows, cols], xc_scr[rows, cols], a_scr[rows, cols], mult_scr[rows, cols]
                r, ig = gates[rows, cols], gates[rows, LW + c0:LW + c0 + LANE]
                dla = gv * hprev_scr[rows, cols] * a - gv * (ig * xc) * (a * a / mult)
                acc_l = acc_l + _fold8(dla * ((-LRU_C) * r))
                dgr = dla * ((-LRU_C) * sp) * r * (1.0 - r)
                dgi = gv * (mult * xc) * ig * (1.0 - ig)
                acc_r, acc_i = acc_r + _fold8(dgr), acc_i + _fold8(dgi)
                dgates[rows, cols] = dgr.astype(MXU_DTYPE)
                dgates[rows, LW + c0:LW + c0 + LANE] = dgi.astype(MXU_DTYPE)
                dxc_ext[rows, cols] = gv * (mult * ig)
            add_row(ga_ref, 5, c0, acc_r)
            add_row(ga_ref, 6, c0, acc_i)
            ga_ref[7:8, cols] += _colsum(acc_l) * (-_sigmoid(-va_ref[7:8, cols]))
        dwr_ref[...] += _dot_tn(xcb_scr[...], dgates[:, 0:LW])
        dwi_ref[...] += _dot_tn(xcb_scr[...], dgates[:, LW:])
        st_a[...] = _dot_nt(dgates[...], wri_ref[...])
        for c0 in range(0, LW, LANE):
            cols = slice(c0, c0 + LANE)
            acc_b = zero_l
            for r0 in range(0, t, rb):
                rows = pl.ds(r0, rb)
                dxc = dxc_ext[rows, cols] + st_a[rows, cols]
                dxc_ext[rows, cols] = dxc
                acc_b = acc_b + _fold8(dxc)
            add_row(ga_ref, 4, c0, acc_b)
        for k in range(LRU_K - 1):
            rot_d[k] = dxc_ext[pl.ds(k + 1, t), :]
        for c0 in range(0, LW, LANE):
            cols = slice(c0, c0 + LANE)
            acc_w = [zero_l] * LRU_K
            for r0 in range(0, t, rb):
                rows = pl.ds(r0, rb)
                dxc = dxc_ext[rows, cols]
                dxa = va_ref[LRU_K - 1:LRU_K, cols] * dxc
                for k in range(LRU_K):
                    acc_w[k] = acc_w[k] + _fold8(dxc * xa_tap(k, rows, r0, cols))
                    if k < LRU_K - 1:
                        dxa = dxa + va_ref[k:k + 1, cols] * rot_d[LRU_K - 2 - k, rows, cols]
                dproj_ref[rows, cols] = dxa.astype(MXU_DTYPE)
            for k in range(LRU_K):
                add_row(ga_ref, k, c0, acc_w[k])
        dxc_ext[t:t + HALO, :] = dxc_ext[0:HALO, :]

        wide[...] = _dot(dproj_ref[...], win_v[...])

        def out_rows(r0, acc):
            rows = pl.ds(r0, 8)
            du1 = wide[rows, :]
            gx_ref[rows, :] = gx_ref[rows, :] + du1 * (1.0 + mod_ref[1:2, :])
            return acc[0] + du1 * x_ref[rows, :], acc[1] + du1

        acc = _row_blocks(t, 8, out_rows, (zero_d, zero_d))
        dmod_ref[1:2, :] += _colsum(acc[0])
        dmod_ref[0:1, :] += _colsum(acc[1])

        pl.when((b == bl - 1) & (i == nt - 1))(e_finish)

    rev = lambda w: pl.BlockSpec((None, t, w), lambda b, i: (b, nt - 1 - i, 0))

    def halo(rows_, w):
        return pl.BlockSpec((None, rows_, w), lambda b, i: (b, jnp.maximum((nt - 1 - i) * (t // rows_) - 1, 0), 0))

    return pl.pallas_call(
        body, name="mixer_bwd", grid=(bl, nt),
        out_shape=[jax.ShapeDtypeStruct((bl, seq, D), F32), jax.ShapeDtypeStruct((bl, seq, pw), MXU_DTYPE),
                   jax.ShapeDtypeStruct((bl, seq, D), MXU_DTYPE), jax.ShapeDtypeStruct((16, LW), F32),
                   jax.ShapeDtypeStruct((32, CW), F32), jax.ShapeDtypeStruct((LW, LW), F32),
                   jax.ShapeDtypeStruct((LW, LW), F32), jax.ShapeDtypeStruct((8, D), F32),
                   jax.ShapeDtypeStruct((bl, 8, D), F32)] + _exchange_out_shapes(ffn_wgrads),
        in_specs=[rev(D), rev(D), rev(pw), halo(HALO31, pw), rev(LW), halo(HALO, LW), rev(CW), rev(D),
                  pl.BlockSpec((None, 8, D), lambda b, i: (b, 0, 0)), _ANY, _ANY,
                  _full(wri.shape), _full(mavg2.shape), _full(va.shape), _full(vb.shape), _full(ln1.shape)]
        + [_ANY] * n_g,
        out_specs=[rev(D), rev(pw), rev(D), _full((16, LW)), _full((32, CW)), _full((LW, LW)), _full((LW, LW)),
                   _full((8, D)), pl.BlockSpec((None, 8, D), lambda b, i: (b, 0, 0))] + [_ANY] * n_g,
        scratch_shapes=[pltpu.VMEM(win_t.shape, MXU_DTYPE), pltpu.VMEM(wout.shape, MXU_DTYPE),
                        pltpu.VMEM((t + HALO, LW), F32), pltpu.VMEM((t, CW), F32),
                        pltpu.VMEM((t + HALO31, CW), F32), pltpu.VMEM((t + HALO, LW), F32),
                        pltpu.VMEM((7, t + HALO31 - 8, CW), F32),
                        pltpu.VMEM((8, LW), F32), pltpu.SemaphoreType.DMA((2,))] + half_f32
        + [pltpu.VMEM((t, D), F32), pltpu.VMEM((t, 2 * LW), F32), pltpu.VMEM((t, 2 * CW), MXU_DTYPE),
           pltpu.VMEM((t, 2 * CW), MXU_DTYPE), pltpu.VMEM((t, 2 * LW), MXU_DTYPE), pltpu.VMEM((t, LW), MXU_DTYPE),
           pltpu.VMEM((LRU_K - 1, t, LW), F32), pltpu.VMEM((LRU_K - 1, t, LW), F32), pltpu.VMEM((t + HALO, LW), F32)]
        + _comm_sems(n_g),
        compiler_params=_params(2),
    )(x, dx1, proj, proj, h, h, vc, mixed, mods, win_t, wout, wri, mavg2, va, vb, ln1, *ffn_wgrads)


def _weight_grad(a, bmat, tm, name):
    ntok, m = a.shape
    n = bmat.shape[1]
    tk = min(WG_TOK_TILE, ntok)
    nk = ntok // tk

    def body(a_ref, b_ref, o_ref):
        @pl.when(pl.program_id(1) == 0)
        def _():
            o_ref[...] = jnp.zeros_like(o_ref)

        o_ref[...] += _dot_tn(a_ref[...], b_ref[...])

    return pl.pallas_call(
        body, name=name, grid=(m // tm, nk), out_shape=jax.ShapeDtypeStruct((m, n), F32),
        in_specs=[pl.BlockSpec((tk, tm), lambda i, k: (k, i)), pl.BlockSpec((tk, n), lambda i, k: (k, 0))],
        out_specs=pl.BlockSpec((tm, n), lambda i, k: (i, 0)),
        compiler_params=_params(2),
    )(a, bmat)


def _sum_slots(slots, name):
    _, r, cdim = slots.shape
    tr = 32 if r % 32 == 0 else r

    def body(s_ref, o_ref):
        acc = s_ref[0].astype(F32)
        for j in range(1, NDEV):
            acc = acc + s_ref[j].astype(F32)
        o_ref[...] = acc

    return pl.pallas_call(
        body, name=name, grid=(r // tr,), out_shape=jax.ShapeDtypeStruct((r, cdim), F32),
        in_specs=[pl.BlockSpec((NDEV, tr, cdim), lambda i: (0, i, 0))],
        out_specs=pl.BlockSpec((tr, cdim), lambda i: (i, 0)),
        compiler_params=_params(1),
    )(slots)


def _small_grad_layout(bl):
    r0 = 16 * bl
    return {"dmod_a": 0, "dmod_b": 8 * bl, "g_a": r0, "g_w31": r0 + 16, "dln1": r0 + 48, "dln2": r0 + 56,
            "df3": r0 + 64, "g_wr": r0 + 88, "g_wi": r0 + 120, "rows": r0 + 152}


def _pack_small_grads(dmod_a, dmod_b, g_a, g_w31, dln1, dln2, df3, g_wr, g_wi):
    bl = dmod_a.shape[0] // 8
    lay = _small_grad_layout(bl)
    ins = [dmod_a, dmod_b, g_a, g_w31, dln1, dln2, df3, g_wr, g_wi]

    def body(ma_ref, mb_ref, ga_ref, gw_ref, l1_ref, l2_ref, f3_ref, wr_ref, wi_ref, o_ref):
        o_ref[...] = jnp.zeros_like(o_ref)
        o_ref[lay["dmod_a"]:lay["dmod_a"] + 8 * bl, :] = ma_ref[...]
        o_ref[lay["dmod_b"]:lay["dmod_b"] + 8 * bl, :] = mb_ref[...]
        o_ref[lay["g_a"]:lay["g_a"] + 16, 0:LW] = ga_ref[...]
        o_ref[lay["g_w31"]:lay["g_w31"] + 32, 0:CW] = gw_ref[...]
        o_ref[lay["dln1"]:lay["dln1"] + 8, :] = l1_ref[...]
        o_ref[lay["dln2"]:lay["dln2"] + 8, :] = l2_ref[...]
        for j, c0 in enumerate(range(0, DFF, D)):
            w = min(D, DFF - c0)
            o_ref[lay["df3"] + 8 * j:lay["df3"] + 8 * j + 8, 0:w] = f3_ref[:, c0:c0 + w]
        o_ref[lay["g_wr"]:lay["g_wr"] + 32, :] = wr_ref[...]
        o_ref[lay["g_wi"]:lay["g_wi"] + 32, :] = wi_ref[...]

    return pl.pallas_call(
        body, name="pack_small_grads", out_shape=jax.ShapeDtypeStruct((lay["rows"], D), F32),
        in_specs=[_full(a.shape) for a in ins], out_specs=_full((lay["rows"], D)),
        compiler_params=pltpu.CompilerParams(vmem_limit_bytes=VMEM_LIMIT),
    )(*ins)


def _sum_small_grads(packed_all, bl):
    lay = _small_grad_layout(bl)
    shapes = {"lru_conv_w": (LRU_K, LW), "lru_conv_b": (1, LW), "lru_b_r": (1, LW), "lru_b_i": (1, LW),
              "lru_lambda": (1, LW), "conv_norm_g": (1, CW), "conv_norm_b": (1, CW), "conv_b": (1, CW),
              "conv_w": (CONV_K, CW), "ln1_g": (1, D), "ln1_b": (1, D), "ln2_g": (1, D), "ln2_b": (1, D),
              "ffn_conv_w": (FFN_K, DFF), "ffn_conv_b": (1, DFF), "lru_w_r": (32, D), "lru_w_i": (32, D)}
    names = list(shapes)

    def body(p_ref, *outs):
        o = dict(zip(names, outs))

        def part(r0, nr, c0=0, nc=D):
            acc = p_ref[0, r0:r0 + nr, c0:c0 + nc]
            for j in range(1, NDEV):
                acc = acc + p_ref[j, r0:r0 + nr, c0:c0 + nc]
            return acc

        ga = part(lay["g_a"], 16, 0, LW)
        o["lru_conv_w"][...] = ga[0:LRU_K]
        for row, k in ((4, "lru_conv_b"), (5, "lru_b_r"), (6, "lru_b_i"), (7, "lru_lambda"), (8, "conv_norm_g"),
                       (9, "conv_norm_b"), (10, "conv_b")):
            o[k][...] = ga[row:row + 1]
        o["conv_w"][...] = part(lay["g_w31"], 32, 0, CW)[0:CONV_K]
        l1, l2 = part(lay["dln1"], 8), part(lay["dln2"], 8)
        o["ln1_g"][...], o["ln1_b"][...] = l1[0:1], l1[1:2]
        o["ln2_g"][...], o["ln2_b"][...] = l2[0:1], l2[1:2]
        for j, c0 in enumerate(range(0, DFF, D)):
            w = min(D, DFF - c0)
            f3 = part(lay["df3"] + 8 * j, 8, 0, w)
            o["ffn_conv_w"][:, c0:c0 + w] = f3[0:FFN_K]
            o["ffn_conv_b"][:, c0:c0 + w] = f3[FFN_K:FFN_K + 1]
        o["lru_w_r"][...] = part(lay["g_wr"], 32)
        o["lru_w_i"][...] = part(lay["g_wi"], 32)

    res = pl.pallas_call(
        body, name="sum_small_grads", out_shape=[jax.ShapeDtypeStruct(shapes[k], F32) for k in names],
        in_specs=[_full(packed_all.shape)], out_specs=[_full(shapes[k]) for k in names],
        compiler_params=pltpu.CompilerParams(vmem_limit_bytes=VMEM_LIMIT),
    )(packed_all)
    return dict(zip(names, res))


def _adamw_update(w_ref, g_ref, m_ref, v_ref, d_ref, nm_ref, nv_ref):
    gv = g_ref[...]
    nm = ADAM_B1 * m_ref[...] + (1.0 - ADAM_B1) * gv
    nv = ADAM_B2 * v_ref[...] + (1.0 - ADAM_B2) * (gv * gv)
    m_hat = nm / (1.0 - ADAM_B1 ** ADAM_STEP)
    v_hat = nv / (1.0 - ADAM_B2 ** ADAM_STEP)
    d_ref[...] = -ADAM_LR * (m_hat / (jnp.sqrt(v_hat) + ADAM_EPS) + ADAM_WD * w_ref[...])
    nm_ref[...] = nm
    nv_ref[...] = nv


def _adamw_many(ws, gs, ms, vs, name):
    n = len(ws)

    def body(*refs):
        ins, outs = refs[:4 * n], refs[4 * n:]
        for k in range(n):
            _adamw_update(ins[k], ins[n + k], ins[2 * n + k], ins[3 * n + k], outs[k], outs[n + k], outs[2 * n + k])

    specs = [_full(w.shape) for w in ws]
    res = pl.pallas_call(
        body, name=name, out_shape=[jax.ShapeDtypeStruct(w.shape, F32) for w in ws] * 3,
        in_specs=specs * 4, out_specs=specs * 3,
        compiler_params=pltpu.CompilerParams(vmem_limit_bytes=VMEM_LIMIT),
    )(*ws, *gs, *ms, *vs)
    return res[:n], res[n:2 * n], res[2 * n:]


def _adamw(w, g, m, v, name):
    r, cdim = w.shape
    tr = 128 if r % 128 == 0 else r
    body = functools.partial(_adamw_update)
    spec = pl.BlockSpec((tr, cdim), lambda i: (i, 0))
    return pl.pallas_call(
        body, name=name, grid=(r // tr,), out_shape=[jax.ShapeDtypeStruct((r, cdim), F32)] * 3,
        in_specs=[spec] * 4, out_specs=[spec] * 3, compiler_params=_params(1),
    )(w, g, m, v)


def _pack(arrs, width=D, row_mult=8):
    parts = []
    for a in arrs:
        flat = a.reshape(-1)
        pad = (-flat.shape[0]) % width
        parts.append(jnp.pad(flat, (0, pad)))
    flat = jnp.concatenate(parts)
    pad = (-flat.shape[0]) % (width * row_mult)
    return jnp.pad(flat, (0, pad)).reshape(-1, width)


def _unpack(buf, shapes, width=D):
    out, row = [], 0
    for shp in shapes:
        size = math.prod(shp)
        nrow = -(-size // width)
        out.append(buf[row:row + nrow].reshape(-1)[:size].reshape(shp))
        row += nrow
    return out


def _unpack_gathered(buf, shapes, width=D):
    out, row = [], 0
    for shp in shapes:
        size = math.prod(shp)
        nrow = -(-size // width)
        out.append(buf[:, row:row + nrow].reshape(NDEV, -1)[:, :size].reshape((NDEV,) + tuple(shp)))
        row += nrow
    return out


def _block_diag(w):
    hn, dh, _ = w.shape
    eye = jnp.eye(hn, dtype=w.dtype)
    return (w[:, :, None, :] * eye[:, None, :, None]).reshape(hn * dh, hn * dh)


def _diag_blocks(wfull, hn):
    dh = wfull.shape[0] // hn
    return jnp.stack([wfull[k * dh:(k + 1) * dh, k * dh:(k + 1) * dh] for k in range(hn)])


def _ungather_cols(g, k):
    n = g.shape[1]
    return g.reshape(NDEV, k, n).transpose(1, 0, 2).reshape(k, NDEV * n)


def _pad_rows(a, rows):
    return jnp.pad(a, ((0, rows - a.shape[0]), (0, 0)))


def kernel(x, c, w_ada, b_ada, w_in, lru_conv_w, lru_conv_b, lru_w_r, lru_b_r, lru_w_i, lru_b_i, lru_lambda, conv_w, conv_b, conv_norm_g, conv_norm_b, w_out, ln1_g, ln1_b, ffn_w_up, ffn_conv_w, ffn_conv_b, ffn_w_down, ln2_g, ln2_b, loss_target, m_w_ada, m_b_ada, m_w_in, m_lru_conv_w, m_lru_conv_b, m_lru_w_r, m_lru_b_r, m_lru_w_i, m_lru_b_i, m_lru_lambda, m_conv_w, m_conv_b, m_conv_norm_g, m_conv_norm_b, m_w_out, m_ln1_g, m_ln1_b, m_ffn_w_up, m_ffn_conv_w, m_ffn_conv_b, m_ffn_w_down, m_ln2_g, m_ln2_b, v_w_ada, v_b_ada, v_w_in, v_lru_conv_w, v_lru_conv_b, v_lru_w_r, v_lru_b_r, v_lru_w_i, v_lru_b_i, v_lru_lambda, v_conv_w, v_conv_b, v_conv_norm_g, v_conv_norm_b, v_w_out, v_ln1_g, v_ln1_b, v_ffn_w_up, v_ffn_conv_w, v_ffn_conv_b, v_ffn_w_down, v_ln2_g, v_ln2_b):
    weights = dict(w_ada=w_ada, b_ada=b_ada, w_in=w_in, lru_conv_w=lru_conv_w, lru_conv_b=lru_conv_b, lru_w_r=lru_w_r,
                   lru_b_r=lru_b_r, lru_w_i=lru_w_i, lru_b_i=lru_b_i, lru_lambda=lru_lambda, conv_w=conv_w, conv_b=conv_b,
                   conv_norm_g=conv_norm_g, conv_norm_b=conv_norm_b, w_out=w_out, ln1_g=ln1_g, ln1_b=ln1_b,
                   ffn_w_up=ffn_w_up, ffn_conv_w=ffn_conv_w, ffn_conv_b=ffn_conv_b, ffn_w_down=ffn_w_down, ln2_g=ln2_g,
                   ln2_b=ln2_b)
    mom_m = dict(w_ada=m_w_ada, b_ada=m_b_ada, w_in=m_w_in, lru_conv_w=m_lru_conv_w, lru_conv_b=m_lru_conv_b,
                 lru_w_r=m_lru_w_r, lru_b_r=m_lru_b_r, lru_w_i=m_lru_w_i, lru_b_i=m_lru_b_i, lru_lambda=m_lru_lambda,
                 conv_w=m_conv_w, conv_b=m_conv_b, conv_norm_g=m_conv_norm_g, conv_norm_b=m_conv_norm_b, w_out=m_w_out,
                 ln1_g=m_ln1_g, ln1_b=m_ln1_b, ffn_w_up=m_ffn_w_up, ffn_conv_w=m_ffn_conv_w, ffn_conv_b=m_ffn_conv_b,
                 ffn_w_down=m_ffn_w_down, ln2_g=m_ln2_g, ln2_b=m_ln2_b)
    mom_v = dict(w_ada=v_w_ada, b_ada=v_b_ada, w_in=v_w_in, lru_conv_w=v_lru_conv_w, lru_conv_b=v_lru_conv_b,
                 lru_w_r=v_lru_w_r, lru_b_r=v_lru_b_r, lru_w_i=v_lru_w_i, lru_b_i=v_lru_b_i, lru_lambda=v_lru_lambda,
                 conv_w=v_conv_w, conv_b=v_conv_b, conv_norm_g=v_conv_norm_g, conv_norm_b=v_conv_norm_b, w_out=v_w_out,
                 ln1_g=v_ln1_g, ln1_b=v_ln1_b, ffn_w_up=v_ffn_w_up, ffn_conv_w=v_ffn_conv_w, ffn_conv_b=v_ffn_conv_b,
                 ffn_w_down=v_ffn_w_down, ln2_g=v_ln2_g, ln2_b=v_ln2_b)
    names = list(weights)
    bl, seq, _ = x.shape
    ntok = bl * seq
    me = 4 * lax.axis_index("x") + 2 * lax.axis_index("y") + lax.axis_index("c")

    small_shapes = [(bl, D), (LRU_K, LW // NDEV), (CONV_K, CW // NDEV), (FFN_K, DFF // NDEV)]
    small = _pack([c, lru_conv_w[0], conv_w[0], ffn_conv_w[0]], width=128)
    n_small = small.shape[0]
    small_all, win_t, wout_b = _all_gather(
        [small, w_in[0].T.astype(MXU_DTYPE), w_out[0].astype(MXU_DTYPE)], "gather_small_and_mixer_weights")
    small_all = small_all.reshape(NDEV, n_small, 128)
    c_parts, k4_parts, k31_parts, k3_parts = _unpack_gathered(small_all, small_shapes, width=128)
    c_all = c_parts.reshape(NDEV * bl, D)
    lru_conv_w_f = k4_parts.transpose(1, 0, 2).reshape(LRU_K, LW)
    conv_w_f = k31_parts.transpose(1, 0, 2).reshape(CONV_K, CW)
    ffn_conv_w_f = k3_parts.transpose(1, 0, 2).reshape(FFN_K, DFF)

    ncol = w_ada.shape[2]
    b_ada_loc = lax.dynamic_slice(b_ada, (0, me * ncol), (1, ncol))
    mod_cols = _ada_fwd(c_all, w_ada[0], b_ada_loc)
    (mod_all,) = _all_gather([mod_cols], "gather_mod")
    mod_all = mod_all.reshape(NDEV, NDEV * bl, ncol)
    mod_mine = lax.dynamic_slice(mod_all, (0, me * bl, 0), (NDEV, bl, ncol))
    mods = mod_mine.transpose(1, 0, 2).reshape(bl, 6, D)
    mods = jnp.pad(mods, ((0, 0), (0, 2), (0, 0)))

    ffn_shards = [ffn_w_up[0].T.astype(MXU_DTYPE), ffn_w_down[0].astype(MXU_DTYPE)]

    wr_bd = _block_diag(lru_w_r[0]).astype(MXU_DTYPE)
    wi_bd = _block_diag(lru_w_i[0]).astype(MXU_DTYPE)
    mavg_np = np.kron(np.eye(CW // HEAD, dtype=np.float32), np.full((HEAD, HEAD), 1.0 / HEAD, np.float32))
    mavg = jnp.asarray(mavg_np, MXU_DTYPE)
    mavg2 = jnp.asarray(np.concatenate([mavg_np, mavg_np], axis=0), MXU_DTYPE)
    wri = jnp.concatenate([wr_bd, wi_bd], axis=1)
    va = jnp.concatenate([lru_conv_w_f, lru_conv_b, lru_b_r, lru_b_i, lru_lambda], axis=0)
    vb = _pad_rows(jnp.concatenate([conv_w_f, conv_b, conv_norm_g, conv_norm_b], axis=0), 40)
    ln1 = _pad_rows(jnp.concatenate([ln1_g, ln1_b], axis=0), 8)
    ln2 = _pad_rows(jnp.concatenate([ln2_g, ln2_b], axis=0), 8)
    f3 = _pad_rows(jnp.concatenate([ffn_conv_w_f, ffn_conv_b], axis=0), 8)

    proj, h, vc, y_b, mixed, x1, u1_b, wup_t, wdown_b = _mixer_fwd(
        x, mods, win_t, wout_b, wri, mavg2, va, vb, ln1, ffn_shards)
    hh, z_b, u2_b, y2, loss_part = _ffn_fwd(x1, mods, loss_target, wup_t, wdown_b, f3, ln2)
    loss = lax.psum(jnp.sum(loss_part) * (0.5 / D), ("x", "y", "c"))

    dx1, dy2_b, dhh_b, dln2, df3, dmod_b = _ffn_bwd(x1, y2, loss_target, hh, mods, wup_t, wdown_b, f3, ln2)
    g_down = _weight_grad(z_b.reshape(ntok, DFF), dy2_b.reshape(ntok, D), FF_CHUNK, "wgrad_down")
    g_up_t = _weight_grad(dhh_b.reshape(ntok, 2 * DFF), u2_b.reshape(ntok, D), FF_CHUNK, "wgrad_up")
    grad_x, dproj_b, dm_b, g_a, g_w31, g_wr, g_wi, dln1, dmod_a, s_up, s_down = _mixer_bwd(
        x, dx1, proj, h, vc, mixed, mods, win_t, wout_b, wri, mavg2, va, vb, ln1, [g_up_t, g_down])
    g_out = _weight_grad(y_b.reshape(ntok, D), dm_b.reshape(ntok, D), D, "wgrad_out")
    g_in_t = _weight_grad(dproj_b.reshape(ntok, 2 * LW + 2 * CW), u1_b.reshape(ntok, D), D, "wgrad_in")

    heads = LW // HEAD
    packed = _pack_small_grads(
        dmod_a.reshape(bl * 8, D), dmod_b.reshape(bl * 8, D), g_a, g_w31, dln1, dln2, df3,
        _diag_blocks(g_wr, heads).reshape(32, D), _diag_blocks(g_wi, heads).reshape(32, D))
    n_rows = packed.shape[0]
    (s_in, s_out), (packed_all,) = _exchange_and_gather(
        [g_in_t.astype(WIRE_DTYPE), g_out.astype(WIRE_DTYPE)], [packed], "exchange_wgrads_gather_small_grads")
    gs_in_t, gs_out = _sum_slots(s_in, "sum_w_in"), _sum_slots(s_out, "sum_w_out")
    gs_up_t, gs_down = _sum_slots(s_up, "sum_w_up"), _sum_slots(s_down, "sum_w_down")
    packed_all = packed_all.reshape(NDEV, n_rows, D)
    full = _sum_small_grads(packed_all, bl)
    dmod_all = jnp.concatenate(
        [packed_all[:, 0:8 * bl].reshape(NDEV * bl, 8, D)[:, 0:3], packed_all[:, 8 * bl:16 * bl].reshape(NDEV * bl, 8, D)[:, 0:3]],
        axis=1).reshape(NDEV * bl, 6 * D)

    grads = {}
    g_w_ada, g_b_ada = _ada_bwd(c_all, dmod_all, lax.dynamic_slice(dmod_all, (0, me * ncol), (NDEV * bl, ncol)))
    grads["w_ada"], grads["b_ada"] = g_w_ada[None], g_b_ada
    grads["w_in"] = gs_in_t.T[None]
    grads["w_out"] = gs_out[None]
    grads["ffn_w_up"] = gs_up_t.T[None]
    grads["ffn_w_down"] = gs_down[None]
    for k, gk in full.items():
        if k in ("lru_conv_w", "conv_w", "ffn_conv_w"):
            nloc = gk.shape[1] // NDEV
            gk = lax.dynamic_slice(gk, (0, me * nloc), (gk.shape[0], nloc))
        grads[k] = gk.reshape(weights[k].shape)

    delta, new_m, new_v = {}, {}, {}
    big = ("w_ada", "w_in", "w_out", "ffn_w_up", "ffn_w_down")
    for k in big:
        d_, m_, v_ = _adamw(weights[k][0], grads[k][0], mom_m[k][0], mom_v[k][0], "adamw_" + k)
        delta[k], new_m[k], new_v[k] = d_[None], m_[None], v_[None]
    small_names = [k for k in names if k not in big]
    d_, m_, v_ = _adamw_many([weights[k] for k in small_names], [grads[k] for k in small_names],
                             [mom_m[k] for k in small_names], [mom_v[k] for k in small_names], "adamw_small")
    for k, dk, mk, vk in zip(small_names, d_, m_, v_):
        delta[k], new_m[k], new_v[k] = dk, mk, vk

    return (loss, grad_x, *[grads[k] for k in names], *[delta[k] for k in names], *[new_m[k] for k in names],
            *[new_v[k] for k in names])
```

```python
import functools
import math

import jax
import jax.numpy as jnp
import numpy as np
from jax import lax
from jax.experimental import pallas as pl
from jax.experimental.pallas import tpu as pltpu

NDEV = 8
D = 1024
LW = 512
CW = 512
HEAD = 64
DFF = 2816
FF_CHUNK = 1408
LRU_K = 4
CONV_K = 31
FFN_K = 3
LRU_C = 8.0
ALPHA = (2 * 1) ** 0.25
LN_EPS = 1e-5
ADAM_LR = 0.001
ADAM_B1 = 0.9
ADAM_B2 = 0.999
ADAM_EPS = 1e-08
ADAM_WD = 0.01
ADAM_STEP = 10

MXU_DTYPE = jnp.bfloat16
WIRE_DTYPE = jnp.bfloat16
TOK_TILE = 256
WG_TOK_TILE = 1024
VMEM_LIMIT = 60 * 1024 * 1024
HALO31 = 32
HALO = 8

F32 = jnp.float32
MESH = pl.DeviceIdType.MESH


def _sigmoid(x):
    return 0.5 * jnp.tanh(0.5 * x) + 0.5


def _shifted_rows(ext_ref, rot_ref, n_rows):
    span = rot_ref.shape[1]
    for r in range(1, 8):
        rot_ref[r - 1] = ext_ref[pl.ds(r, span), :]

    def window(o, r0=0, cols=slice(None)):
        q, r = divmod(o, 8)
        if r == 0:
            return ext_ref[pl.ds(8 * q + r0, n_rows), cols]
        return rot_ref[r - 1, pl.ds(8 * q + r0, n_rows), cols]

    return window


def _store_hilo(hilo_ref, rows, c0, v):
    half = hilo_ref.shape[1] // 2
    hi = v.astype(hilo_ref.dtype)
    hilo_ref[rows, c0:c0 + v.shape[1]] = hi
    hilo_ref[rows, half + c0:half + c0 + v.shape[1]] = (v - hi.astype(F32)).astype(hilo_ref.dtype)


def _scan_bwd_blocks(a_ref, g_ref, car, n_rows):
    row = lax.broadcasted_iota(jnp.int32, (8, LANE), 0)
    for c0 in range(0, a_ref.shape[1], LANE):
        cols = slice(c0, c0 + LANE)
        a_next = jnp.broadcast_to(car[0:1, cols], (8, LANE))
        g_next = jnp.broadcast_to(car[1:2, cols], (8, LANE))
        for r0 in range(n_rows - 8, -1, -8):
            a_blk, g = a_ref[r0:r0 + 8, cols], g_ref[r0:r0 + 8, cols]
            bm = jnp.where(row < 7, pltpu.roll(a_blk, 7, 0), a_next)
            for d in (1, 2, 4):
                keep = row < 8 - d
                g = g + bm * jnp.where(keep, pltpu.roll(g, 8 - d, 0), 0.0)
                bm = bm * jnp.where(keep, pltpu.roll(bm, 8 - d, 0), 1.0)
            g = g + bm * g_next
            g_ref[r0:r0 + 8, cols] = g
            a_next = jnp.broadcast_to(a_blk[0:1, :], (8, LANE))
            g_next = jnp.broadcast_to(g[0:1, :], (8, LANE))
        car[0:1, cols] = a_next[0:1, :]
        car[1:2, cols] = g_next[0:1, :]


def _scan_fwd_blocks(a_ref, u_ref, hcar, n_rows):
    row = lax.broadcasted_iota(jnp.int32, (8, LANE), 0)
    for c0 in range(0, a_ref.shape[1], LANE):
        cols = slice(c0, c0 + LANE)
        h_prev = jnp.broadcast_to(hcar[0:1, cols], (8, LANE))
        for r0 in range(0, n_rows, 8):
            a, u = a_ref[r0:r0 + 8, cols], u_ref[r0:r0 + 8, cols]
            for d in (1, 2, 4):
                keep = row >= d
                u = a * jnp.where(keep, pltpu.roll(u, d, 0), 0.0) + u
                a = a * jnp.where(keep, pltpu.roll(a, d, 0), 1.0)
            u = u + a * h_prev
            u_ref[r0:r0 + 8, cols] = u
            h_prev = jnp.broadcast_to(u[7:8, :], (8, LANE))
        hcar[0:1, cols] = h_prev[0:1, :]


def _dot(a, b):
    return jnp.dot(a, b, preferred_element_type=F32)


def _dot_nt(a, b):
    return lax.dot_general(a, b, (((1,), (1,)), ((), ())), preferred_element_type=F32)


def _dot_tn(a, b):
    return lax.dot_general(a, b, (((0,), (0,)), ((), ())), preferred_element_type=F32)


def _colsum(v):
    return jnp.sum(v, axis=0, keepdims=True)


def _rowmean(v):
    return jnp.mean(v, axis=-1, keepdims=True)


def _head_mean(v, mavg):
    hi = v.astype(MXU_DTYPE)
    lo = (v - hi.astype(F32)).astype(MXU_DTYPE)
    return _dot(hi, mavg) + _dot(lo, mavg)


_GELU_C0 = math.sqrt(2.0 / math.pi)
_GELU_C1 = 0.044715


def _gelu_and_grad(x):
    x2 = x * x
    th = jnp.tanh(_GELU_C0 * (x + _GELU_C1 * x * x2))
    ge = 0.5 * x * (1.0 + th)
    dge = 0.5 * (1.0 + th) + 0.5 * x * (1.0 - th * th) * (_GELU_C0 * (1.0 + 3.0 * _GELU_C1 * x2))
    return ge, dge


def _softplus(x):
    return jnp.maximum(x, 0.0) + jnp.log1p(jnp.exp(-jnp.abs(x)))


def _layer_norm_stats(r):
    mu = _rowmean(r)
    dl = r - mu
    var = _rowmean(dl * dl)
    rstd = lax.rsqrt(var + LN_EPS)
    return dl * rstd, rstd


def _scan_fwd(a, u, rows):
    n = a.shape[0]
    d = 1
    while d < n:
        keep = rows >= d
        a_s = jnp.where(keep, pltpu.roll(a, d, 0), 1.0)
        u_s = jnp.where(keep, pltpu.roll(u, d, 0), 0.0)
        u = a * u_s + u
        a = a * a_s
        d *= 2
    return u, a


def _scan_bwd(bm, g, rows):
    n = bm.shape[0]
    d = 1
    while d < n:
        keep = rows < n - d
        b_s = jnp.where(keep, pltpu.roll(bm, n - d, 0), 1.0)
        g_s = jnp.where(keep, pltpu.roll(g, n - d, 0), 0.0)
        g = g + bm * g_s
        bm = bm * b_s
        d *= 2
    return g, bm


def _lru_gates(xc, wr, wi, va_ref):
    xcb = xc.astype(MXU_DTYPE)
    r = _sigmoid(_dot(xcb, wr) + va_ref[5:6, :])
    ig = _sigmoid(_dot(xcb, wi) + va_ref[6:7, :])
    sp = _softplus(-va_ref[7:8, :])
    la = (-LRU_C) * r * sp
    a = jnp.exp(la)
    a2 = a * a
    mult = jnp.sqrt(-jnp.tanh(la) * (a2 + 1.0))
    return xcb, r, ig, sp, a, a2, mult


ROW_BLK = 32
LANE = 128


def _row_blocks(n_rows, rb, body, init):
    carry = init
    for j in range(n_rows // rb):
        carry = body(j * rb, carry)
    return carry


def _fold8(v):
    r, c = v.shape
    return v if r == 8 else v.reshape(r // 8, 8, c).sum(axis=0)


def _cast_rows(src_ref, dst_ref, n_rows):
    rb = min(16, n_rows)

    def rows(r0, carry):
        dst_ref[pl.ds(r0, rb), :] = src_ref[pl.ds(r0, rb), :].astype(dst_ref.dtype)
        return carry

    _row_blocks(n_rows, rb, rows, ())


def _full(shape):
    nd = len(shape)
    return pl.BlockSpec(shape, lambda *_: (0,) * nd)


_ANY = pl.BlockSpec(memory_space=pl.ANY)


def _params(n_grid):
    return pltpu.CompilerParams(dimension_semantics=("arbitrary",) * n_grid, vmem_limit_bytes=VMEM_LIMIT)


def _my_place():
    return lax.axis_index("x"), lax.axis_index("y"), lax.axis_index("c")


def _all_gather(arrays, name):
    n_arr = len(arrays)

    def body(*refs):
        start, forward, finish = _gather_steps(refs[:n_arr], refs[n_arr:2 * n_arr], *refs[2 * n_arr:])
        start()
        forward()
        finish()

    return pl.pallas_call(
        body, name=name, out_shape=_gather_out_shapes(arrays),
        in_specs=[_ANY] * n_arr, out_specs=[_ANY] * n_arr, scratch_shapes=_comm_sems(n_arr),
    )(*arrays)


def _gather_out_shapes(arrays):
    return [jax.ShapeDtypeStruct((NDEV * a.shape[0], a.shape[1]), a.dtype) for a in arrays]


def _comm_sems(n_arr):
    return [pltpu.SemaphoreType.DMA((n_arr, 7)), pltpu.SemaphoreType.DMA((n_arr, 7)), pltpu.SemaphoreType.DMA((n_arr,))]


def _gather_steps(x_refs, out_refs, send_sems, recv_sems, local_sems):
    n_arr = len(x_refs)
    x, y, c = _my_place()
    me, sibling = (x, y, c), (x, y, 1 - c)
    chips = [(1 - x, y), (x, 1 - y), (1 - x, 1 - y)]

    def rows(k, px, py, pc):
        m = x_refs[k].shape[0]
        return out_refs[k].at[pl.ds((4 * px + 2 * py + pc) * m, m), :]

    def copy(k, s, block, to, src=None):
        return pltpu.make_async_remote_copy(
            src_ref=rows(k, *block) if src is None else src, dst_ref=rows(k, *block),
            send_sem=send_sems.at[k, s], recv_sem=recv_sems.at[k, s], device_id=to, device_id_type=MESH)

    def mine():
        return [pltpu.make_async_copy(x_refs[k], rows(k, *me), local_sems.at[k]) for k in range(n_arr)]

    def first():
        cps = []
        for k in range(n_arr):
            cps.append(copy(k, 0, me, sibling, src=x_refs[k]))
            cps += [copy(k, 1 + j, me, (*chip, c), src=x_refs[k]) for j, chip in enumerate(chips)]
        return cps

    def passed():
        return [copy(k, 4 + j, (*chip, c), sibling) for k in range(n_arr) for j, chip in enumerate(chips)]

    def start():
        for cp in mine() + first():
            cp.start()

    def forward():
        fwd = passed()
        for k in range(n_arr):
            for j, chip in enumerate(chips):
                copy(k, 1 + j, (*chip, c), me).wait_recv()
                fwd[3 * k + j].start()

    def finish():
        for k in range(n_arr):
            copy(k, 0, sibling, me).wait_recv()
            for j, chip in enumerate(chips):
                copy(k, 4 + j, (*chip, 1 - c), me).wait_recv()
        for cp in first() + passed():
            cp.wait_send()
        for cp in mine():
            cp.wait()

    return start, forward, finish


def _exchange_blocks(arrays, name):
    n_arr = len(arrays)

    def body(*refs):
        start, finish = _exchange_steps(refs[:n_arr], refs[n_arr:2 * n_arr], *refs[2 * n_arr:])
        start()
        finish()

    return pl.pallas_call(
        body, name=name, out_shape=_exchange_out_shapes(arrays),
        in_specs=[_ANY] * n_arr, out_specs=[_ANY] * n_arr, scratch_shapes=_comm_sems(n_arr),
    )(*arrays)


def _exchange_and_gather(ex_arrays, ga_arrays, name):
    n_e, n_g = len(ex_arrays), len(ga_arrays)

    def body(*refs):
        e_in, g_in, refs = refs[:n_e], refs[n_e:n_e + n_g], refs[n_e + n_g:]
        e_out, g_out, sems = refs[:n_e], refs[n_e:n_e + n_g], refs[n_e + n_g:]
        e_start, e_finish = _exchange_steps(e_in, e_out, *sems[:3])
        g_start, g_forward, g_finish = _gather_steps(g_in, g_out, *sems[3:])
        g_start()
        e_start()
        g_forward()
        g_finish()
        e_finish()

    res = pl.pallas_call(
        body, name=name, out_shape=_exchange_out_shapes(ex_arrays) + _gather_out_shapes(ga_arrays),
        in_specs=[_ANY] * (n_e + n_g), out_specs=[_ANY] * (n_e + n_g),
        scratch_shapes=_comm_sems(n_e) + _comm_sems(n_g),
    )(*ex_arrays, *ga_arrays)
    return res[:n_e], res[n_e:]


def _exchange_out_shapes(arrays):
    return [jax.ShapeDtypeStruct((NDEV, a.shape[0] // NDEV, a.shape[1]), a.dtype) for a in arrays]


def _exchange_steps(g_refs, out_refs, send_sems, recv_sems, local_sems):
    n_arr = len(g_refs)
    x, y, c = _my_place()
    me = 4 * x + 2 * y + c

    def copy(k, rel):
        px, py, pc = x ^ ((rel >> 2) & 1), y ^ ((rel >> 1) & 1), c ^ (rel & 1)
        m = out_refs[k].shape[1]
        return pltpu.make_async_remote_copy(
            src_ref=g_refs[k].at[pl.ds((4 * px + 2 * py + pc) * m, m), :], dst_ref=out_refs[k].at[me],
            send_sem=send_sems.at[k, rel - 1], recv_sem=recv_sems.at[k, rel - 1],
            device_id=(px, py, pc), device_id_type=MESH)

    def copies():
        cps = []
        for k in range(n_arr):
            m = out_refs[k].shape[1]
            cps.append(pltpu.make_async_copy(g_refs[k].at[pl.ds(me * m, m), :], out_refs[k].at[me], local_sems.at[k]))
            cps += [copy(k, rel) for rel in range(1, NDEV)]
        return cps

    def start():
        for cp in copies():
            cp.start()

    def finish():
        for cp in copies():
            cp.wait()

    return start, finish


def _ada_fwd(c_all, w_ada_loc, b_ada_loc):
    def body(c_ref, w_ref, b_ref, o_ref):
        cv = c_ref[...]
        ca = (cv * _sigmoid(cv)).astype(MXU_DTYPE)
        o_ref[...] = _dot(ca, w_ref[...].astype(MXU_DTYPE)) + b_ref[...]

    return pl.pallas_call(
        body, name="ada_fwd", out_shape=jax.ShapeDtypeStruct((c_all.shape[0], w_ada_loc.shape[1]), F32),
        in_specs=[_full(c_all.shape), _full(w_ada_loc.shape), _full(b_ada_loc.shape)],
        out_specs=_full((c_all.shape[0], w_ada_loc.shape[1])),
        compiler_params=pltpu.CompilerParams(vmem_limit_bytes=VMEM_LIMIT),
    )(c_all, w_ada_loc, b_ada_loc)


def _ada_bwd(c_all, dmod_all, dmod_cols):
    def body(c_ref, da_ref, d_ref, o_ref, b_ref):
        cv = c_ref[...]
        ca = (cv * _sigmoid(cv)).astype(MXU_DTYPE)
        o_ref[...] = _dot_tn(ca, d_ref[...].astype(MXU_DTYPE))
        b_ref[...] = _colsum(da_ref[...])

    return pl.pallas_call(
        body, name="ada_bwd",
        out_shape=[jax.ShapeDtypeStruct((c_all.shape[1], dmod_cols.shape[1]), F32),
                   jax.ShapeDtypeStruct((1, dmod_all.shape[1]), F32)],
        in_specs=[_full(c_all.shape), _full(dmod_all.shape), _full(dmod_cols.shape)],
        out_specs=[_full((c_all.shape[1], dmod_cols.shape[1])), _full((1, dmod_all.shape[1]))],
        compiler_params=pltpu.CompilerParams(vmem_limit_bytes=VMEM_LIMIT),
    )(c_all, dmod_all, dmod_cols)


def _mixer_fwd(x, mods, win_t, wout, wri, mavg2, va, vb, ln1, ffn_shards):
    bl, seq, _ = x.shape
    t = min(TOK_TILE, seq)
    nt = seq // t
    n_g = len(ffn_shards)
    rb = min(ROW_BLK, t)

    def body(x_ref, mod_ref, win_hbm, wout_hbm, wri_ref, mavg_ref, va_ref, vb_ref, ln1_ref, *rest):
        shard_refs, rest = rest[:n_g], rest[n_g:]
        proj_ref, h_ref, vc_ref, y_ref, mixed_ref, x1_ref, u1_ref = rest[:7]
        gathered_refs, rest = rest[7:7 + n_g], rest[7 + n_g:]
        (win_v, wout_v, xa_ext, vg_ext, rot_a, hcar, sems, rot_c, xc_scr, a_scr, st_a, st_b, xcb_scr, gates, hilo,
         g_send, g_recv, g_local) = rest
        b, i = pl.program_id(0), pl.program_id(1)
        step = b * nt + i
        g_start, g_forward, g_finish = _gather_steps(shard_refs, gathered_refs, g_send, g_recv, g_local)

        @pl.when(step == 0)
        def _():
            g_start()
            cps = [pltpu.make_async_copy(win_hbm, win_v, sems.at[0]), pltpu.make_async_copy(wout_hbm, wout_v, sems.at[1])]
            for cp in cps:
                cp.start()
            for cp in cps:
                cp.wait()

        pl.when(step == (bl * nt) // 2)(g_forward)

        @pl.when(i == 0)
        def _():
            xa_ext[0:HALO, :] = jnp.zeros((HALO, LW), F32)
            vg_ext[0:HALO31, :] = jnp.zeros((HALO31, CW), F32)
            hcar[...] = jnp.zeros_like(hcar)

        blocks = [(r0, c0) for c0 in range(0, LW, LANE) for r0 in range(0, t, rb)]

        def in_rows(r0, acc):
            rows = pl.ds(r0, 16)
            u1_ref[rows, :] = (x_ref[rows, :] * (1.0 + mod_ref[1:2, :]) + mod_ref[0:1, :]).astype(MXU_DTYPE)
            return acc

        _row_blocks(t, min(16, t), in_rows, ())
        proj_ref[...] = _dot_nt(u1_ref[...], win_v[...])

        xa_ext[HALO:HALO + t, :] = proj_ref[:, 0:LW]
        for k in range(LRU_K - 1):
            rot_c[k] = xa_ext[pl.ds(HALO - (LRU_K - 1) + k, t), :]
        for r0, c0 in blocks:
            rows, cols = pl.ds(r0, rb), slice(c0, c0 + LANE)
            xc = va_ref[4:5, cols] + va_ref[LRU_K - 1:LRU_K, cols] * xa_ext[pl.ds(HALO + r0, rb), cols]
            for k in range(LRU_K - 1):
                xc = xc + va_ref[k:k + 1, cols] * rot_c[k, rows, cols]
            xc_scr[rows, cols] = xc
            xcb_scr[rows, cols] = xc.astype(MXU_DTYPE)
        xa_ext[0:HALO, :] = xa_ext[t:t + HALO, :]
        gates[...] = _dot(xcb_scr[...], wri_ref[...])
        for r0, c0 in blocks:
            rows, cols = pl.ds(r0, rb), slice(c0, c0 + LANE)
            r = _sigmoid(gates[rows, cols] + va_ref[5:6, cols])
            ig = _sigmoid(gates[rows, LW + c0:LW + c0 + LANE] + va_ref[6:7, cols])
            la = (-LRU_C) * r * _softplus(-va_ref[7:8, cols])
            a = jnp.exp(la)
            a_scr[rows, cols] = a
            h_ref[rows, cols] = jnp.sqrt(-jnp.tanh(la) * (a * a + 1.0)) * (ig * xc_scr[rows, cols])
        _scan_fwd_blocks(a_scr, h_ref, hcar, t)

        for r0, c0 in blocks:
            rows, cols, pc = pl.ds(r0, rb), slice(c0, c0 + LANE), 2 * LW + c0
            vg_ext[pl.ds(HALO31 + r0, rb), cols] = (
                proj_ref[rows, pc:pc + LANE] * _sigmoid(proj_ref[rows, pc + CW:pc + CW + LANE]))
        vg_win = _shifted_rows(vg_ext, rot_a, rb)
        for r0, c0 in blocks:
            rows, cols = pl.ds(r0, rb), slice(c0, c0 + LANE)
            vc = vb_ref[CONV_K:CONV_K + 1, cols]
            for k in range(CONV_K):
                vc = vc + vb_ref[k:k + 1, cols] * vg_win(HALO31 - (CONV_K - 1) + k, r0, cols)
            vc_ref[rows, cols] = vc
            _store_hilo(hilo, rows, c0, vc)
        vg_ext[0:HALO31, :] = vg_ext[t:t + HALO31, :]
        mavg2 = mavg_ref[...]
        st_a[...] = _dot(hilo[...], mavg2)
        for r0, c0 in blocks:
            rows, cols = pl.ds(r0, rb), slice(c0, c0 + LANE)
            dl = vc_ref[rows, cols] - st_a[rows, cols]
            st_a[rows, cols] = dl
            _store_hilo(hilo, rows, c0, dl * dl)
        st_b[...] = _dot(hilo[...], mavg2)
        for r0, c0 in blocks:
            rows, cols = pl.ds(r0, rb), slice(c0, c0 + LANE)
            ge, _ = _gelu_and_grad(proj_ref[rows, LW + c0:LW + c0 + LANE])
            y_ref[rows, cols] = (ge * h_ref[rows, cols]).astype(MXU_DTYPE)
            yl = (st_a[rows, cols] * lax.rsqrt(st_b[rows, cols] + LN_EPS) * vb_ref[CONV_K + 1:CONV_K + 2, cols]
                  + vb_ref[CONV_K + 2:CONV_K + 3, cols])
            y_ref[rows, LW + c0:LW + c0 + LANE] = (yl * _sigmoid(yl)).astype(MXU_DTYPE)
        mixed_ref[...] = _dot(y_ref[...], wout_v[...])

        def out_rows(r0, acc):
            rows = pl.ds(r0, 8)
            xh, _ = _layer_norm_stats(ALPHA * x_ref[rows, :] + (1.0 + mod_ref[2:3, :]) * mixed_ref[rows, :])
            x1_ref[rows, :] = xh * ln1_ref[0:1, :] + ln1_ref[1:2, :]
            return acc

        _row_blocks(t, 8, out_rows, ())

        pl.when(step == bl * nt - 1)(g_finish)

    tok = lambda w: pl.BlockSpec((None, t, w), lambda b, i: (b, i, 0))
    outs = [(2 * LW + 2 * CW, F32), (LW, F32), (CW, F32), (D, MXU_DTYPE), (D, F32), (D, F32), (D, MXU_DTYPE)]
    return pl.pallas_call(
        body, name="mixer_fwd", grid=(bl, nt),
        out_shape=[jax.ShapeDtypeStruct((bl, seq, w), dt) for w, dt in outs] + _gather_out_shapes(ffn_shards),
        in_specs=[tok(D), pl.BlockSpec((None, 8, D), lambda b, i: (b, 0, 0)), _ANY, _ANY,
                  _full(wri.shape), _full(mavg2.shape), _full(va.shape), _full(vb.shape), _full(ln1.shape)]
        + [_ANY] * n_g,
        out_specs=[tok(w) for w, _ in outs] + [_ANY] * n_g,
        scratch_shapes=[pltpu.VMEM(win_t.shape, MXU_DTYPE), pltpu.VMEM(wout.shape, MXU_DTYPE),
                        pltpu.VMEM((t + HALO, LW), F32), pltpu.VMEM((t + HALO31, CW), F32),
                        pltpu.VMEM((7, t + HALO31 - 8, CW), F32), pltpu.VMEM((8, LW), F32),
                        pltpu.SemaphoreType.DMA((2,)), pltpu.VMEM((LRU_K - 1, t, LW), F32)]
        + [pltpu.VMEM((t, LW), F32)] * 4
        + [pltpu.VMEM((t, LW), MXU_DTYPE), pltpu.VMEM((t, 2 * LW), F32), pltpu.VMEM((t, 2 * CW), MXU_DTYPE)]
        + _comm_sems(n_g),
        compiler_params=_params(2),
    )(x, mods, win_t, wout, wri, mavg2, va, vb, ln1, *ffn_shards)


def _ffn_fwd(x1, mods, tgt, wup_t, wdown, f3, ln2):
    bl, seq, _ = x1.shape
    t = min(TOK_TILE, seq)
    nt = seq // t
    n_chunk = DFF // FF_CHUNK

    rb = min(ROW_BLK, t)

    def body(x1_ref, mod_ref, tgt_ref, wup_hbm, wdown_hbm, f3_ref, ln2_ref,
             hh_ref, z_ref, u2_ref, y2_ref, loss_ref,
             wup_v, wdown_v, g_ext, rot, sems):
        b, i = pl.program_id(0), pl.program_id(1)

        @pl.when((b == 0) & (i == 0))
        def _():
            cps = [pltpu.make_async_copy(wup_hbm, wup_v, sems.at[0]), pltpu.make_async_copy(wdown_hbm, wdown_v, sems.at[1])]
            for cp in cps:
                cp.start()
            for cp in cps:
                cp.wait()
            loss_ref[...] = jnp.zeros_like(loss_ref)

        @pl.when(i == 0)
        def _():
            for ch in range(n_chunk):
                g_ext[ch, 0:HALO, :] = jnp.zeros((HALO, FF_CHUNK), F32)

        def in_rows(r0, acc):
            rows = pl.ds(r0, 16)
            u2_ref[rows, :] = (x1_ref[rows, :] * (1.0 + mod_ref[4:5, :]) + mod_ref[3:4, :]).astype(MXU_DTYPE)
            return acc

        _row_blocks(t, min(16, t), in_rows, ())
        for ch in range(n_chunk):
            lo = ch * FF_CHUNK
            hh_ref[:, lo:lo + FF_CHUNK] = _dot_nt(u2_ref[...], wup_v[lo:lo + FF_CHUNK, :])
            hh_ref[:, DFF + lo:DFF + lo + FF_CHUNK] = _dot_nt(u2_ref[...], wup_v[DFF + lo:DFF + lo + FF_CHUNK, :])
            g_ext[ch, HALO:HALO + t, :] = hh_ref[:, DFF + lo:DFF + lo + FF_CHUNK]
            for k in range(FFN_K - 1):
                rot[k] = g_ext[ch, pl.ds(HALO - (FFN_K - 1) + k, t), :]
            for cb in range(FF_CHUNK // LANE):
                cols = slice(cb * LANE, (cb + 1) * LANE)
                fcols = slice(lo + cb * LANE, lo + (cb + 1) * LANE)
                for r0 in range(0, t, rb):
                    rows = pl.ds(r0, rb)
                    gc = (f3_ref[FFN_K:FFN_K + 1, fcols] + f3_ref[0:1, fcols] * rot[0, rows, cols]
                          + f3_ref[1:2, fcols] * rot[1, rows, cols]
                          + f3_ref[2:3, fcols] * g_ext[ch, pl.ds(HALO + r0, rb), cols])
                    z_ref[rows, fcols] = (gc * _sigmoid(gc) * hh_ref[rows, fcols]).astype(MXU_DTYPE)
            g_ext[ch, 0:HALO, :] = g_ext[ch, t:t + HALO, :]
        y2_ref[...] = _dot(z_ref[...], wdown_v[...])

        def out_rows(r0, acc):
            rows = pl.ds(r0, 8)
            xh, _ = _layer_norm_stats(ALPHA * x1_ref[rows, :] + (1.0 + mod_ref[5:6, :]) * y2_ref[rows, :])
            err = xh * ln2_ref[0:1, :] + ln2_ref[1:2, :] - tgt_ref[rows, :]
            return acc + err * err

        e2 = _row_blocks(t, 8, out_rows, jnp.zeros((8, D), F32))
        part = e2[:, 0:128]
        for j in range(1, D // 128):
            part = part + e2[:, 128 * j:128 * (j + 1)]
        loss_ref[...] += part

    tok = lambda w: pl.BlockSpec((None, t, w), lambda b, i: (b, i, 0))
    outs = [(2 * DFF, F32), (DFF, MXU_DTYPE), (D, MXU_DTYPE), (D, F32)]
    return pl.pallas_call(
        body, name="ffn_fwd", grid=(bl, nt),
        out_shape=[jax.ShapeDtypeStruct((bl, seq, w), dt) for w, dt in outs] + [jax.ShapeDtypeStruct((8, 128), F32)],
        in_specs=[tok(D), pl.BlockSpec((None, 8, D), lambda b, i: (b, 0, 0)), tok(D), _ANY, _ANY,
                  _full(f3.shape), _full(ln2.shape)],
        out_specs=[tok(w) for w, _ in outs] + [_full((8, 128))],
        scratch_shapes=[pltpu.VMEM(wup_t.shape, MXU_DTYPE), pltpu.VMEM(wdown.shape, MXU_DTYPE),
                        pltpu.VMEM((n_chunk, t + HALO, FF_CHUNK), F32), pltpu.VMEM((FFN_K - 1, t, FF_CHUNK), F32),
                        pltpu.SemaphoreType.DMA((2,))],
        compiler_params=_params(2),
    )(x1, mods, tgt, wup_t, wdown, f3, ln2)


def _ffn_bwd(x1, y2, tgt, hh, mods, wup_t, wdown, f3, ln2):
    bl, seq, _ = x1.shape
    t = min(TOK_TILE, seq)
    nt = seq // t
    n_chunk = DFF // FF_CHUNK
    rb = min(ROW_BLK, t)

    def body(x1_ref, y2_ref, tgt_ref, hh_ref, halo_ref, mod_ref, wup_hbm, wdown_hbm, f3_ref, ln2_ref,
             dx1_ref, dy2_ref, dhh_ref, dln2_ref, df3_ref, dmod_ref,
             wup_v, wdown_v, g_ext, dgc_ext, rot, dz_scr, wide_scr, sems):
        b, i = pl.program_id(0), pl.program_id(1)
        tt = nt - 1 - i

        @pl.when((b == 0) & (i == 0))
        def _():
            cps = [pltpu.make_async_copy(wup_hbm, wup_v, sems.at[0]), pltpu.make_async_copy(wdown_hbm, wdown_v, sems.at[1])]
            for cp in cps:
                cp.start()
            for cp in cps:
                cp.wait()
            dln2_ref[...] = jnp.zeros_like(dln2_ref)
            df3_ref[...] = jnp.zeros_like(df3_ref)

        @pl.when(i == 0)
        def _():
            dmod_ref[...] = jnp.zeros_like(dmod_ref)
            for ch in range(n_chunk):
                dgc_ext[ch, t:t + HALO, :] = jnp.zeros((HALO, FF_CHUNK), F32)

        def ln_rows(r0, acc):
            rows = pl.ds(r0, 8)
            x1v, y2v = x1_ref[rows, :], y2_ref[rows, :]
            gt = mod_ref[5:6, :]
            xh, rstd = _layer_norm_stats(ALPHA * x1v + (1.0 + gt) * y2v)
            g2 = ln2_ref[0:1, :]
            dx2 = (xh * g2 + ln2_ref[1:2, :] - tgt_ref[rows, :]) * (1.0 / D)
            dxh = dx2 * g2
            dr2 = rstd * (dxh - _rowmean(dxh) - xh * _rowmean(dxh * xh))
            wide_scr[rows, :] = (1.0 + gt) * dr2
            dx1_ref[rows, :] = ALPHA * dr2
            return acc[0] + dx2 * xh, acc[1] + dx2, acc[2] + dr2 * y2v

        zero_d = jnp.zeros((8, D), F32)
        acc = _row_blocks(t, 8, ln_rows, (zero_d, zero_d, zero_d))
        dln2_ref[0:1, :] += _colsum(acc[0])
        dln2_ref[1:2, :] += _colsum(acc[1])
        dmod_ref[2:3, :] += _colsum(acc[2])
        _cast_rows(wide_scr, dy2_ref, t)

        halo_keep = (tt > 0).astype(F32)
        zero_l = jnp.zeros((8, LANE), F32)
        for ch in range(n_chunk):
            lo = ch * FF_CHUNK
            dz_scr[...] = _dot_nt(dy2_ref[...], wdown_v[lo:lo + FF_CHUNK, :])
            g_ext[0:HALO, :] = halo_ref[:, lo:lo + FF_CHUNK] * halo_keep
            g_ext[HALO:HALO + t, :] = hh_ref[:, DFF + lo:DFF + lo + FF_CHUNK]
            for k in range(FFN_K - 1):
                rot[k] = g_ext[pl.ds(HALO - (FFN_K - 1) + k, t), :]
            for cb in range(FF_CHUNK // LANE):
                cols = slice(cb * LANE, (cb + 1) * LANE)
                fcols = slice(lo + cb * LANE, lo + (cb + 1) * LANE)

                def gate_rows(r0, acc, cols=cols, fcols=fcols, ch=ch):
                    rows = pl.ds(r0, rb)
                    taps = [rot[0, rows, cols], rot[1, rows, cols], g_ext[pl.ds(r0 + HALO, rb), cols]]
                    gc = f3_ref[FFN_K:FFN_K + 1, fcols]
                    for k in range(FFN_K):
                        gc = gc + f3_ref[k:k + 1, fcols] * taps[k]
                    s = _sigmoid(gc)
                    dz = dz_scr[rows, cols]
                    dhh_ref[rows, fcols] = (dz * (gc * s)).astype(MXU_DTYPE)
                    dgc = dz * hh_ref[rows, fcols] * (s * (1.0 + gc * (1.0 - s)))
                    dgc_ext[ch, rows, cols] = dgc
                    return tuple(acc[k] + _fold8(dgc * taps[k]) for k in range(FFN_K)) + (acc[FFN_K] + _fold8(dgc),)

                acc = _row_blocks(t, rb, gate_rows, (zero_l,) * (FFN_K + 1))
                for k in range(FFN_K + 1):
                    df3_ref[k:k + 1, fcols] += _colsum(acc[k])
            for k in range(FFN_K - 1):
                rot[k] = dgc_ext[ch, pl.ds(k + 1, t), :]
            for cb in range(FF_CHUNK // LANE):
                cols = slice(cb * LANE, (cb + 1) * LANE)
                fcols = slice(lo + cb * LANE, lo + (cb + 1) * LANE)
                gcols = slice(DFF + lo + cb * LANE, DFF + lo + (cb + 1) * LANE)

                def dg_rows(r0, acc, cols=cols, fcols=fcols, gcols=gcols, ch=ch):
                    rows = pl.ds(r0, rb)
                    dg = (f3_ref[2:3, fcols] * dgc_ext[ch, rows, cols] + f3_ref[1:2, fcols] * rot[0, rows, cols]
                          + f3_ref[0:1, fcols] * rot[1, rows, cols])
                    dhh_ref[rows, gcols] = dg.astype(MXU_DTYPE)
                    return acc

                _row_blocks(t, rb, dg_rows, ())
            dgc_ext[ch, t:t + HALO, :] = dgc_ext[ch, 0:HALO, :]

        wide_scr[...] = _dot(dhh_ref[...], wup_v[...])

        def out_rows(r0, acc):
            rows = pl.ds(r0, 8)
            du2 = wide_scr[rows, :]
            dx1_ref[rows, :] = dx1_ref[rows, :] + du2 * (1.0 + mod_ref[4:5, :])
            return acc[0] + du2 * x1_ref[rows, :], acc[1] + du2

        acc = _row_blocks(t, 8, out_rows, (zero_d, zero_d))
        dmod_ref[1:2, :] += _colsum(acc[0])
        dmod_ref[0:1, :] += _colsum(acc[1])

    rev = lambda w: pl.BlockSpec((None, t, w), lambda b, i: (b, nt - 1 - i, 0))
    halo = pl.BlockSpec((None, HALO, DFF), lambda b, i: (b, jnp.maximum((nt - 1 - i) * (t // HALO) - 1, 0), 1))
    return pl.pallas_call(
        body, name="ffn_bwd", grid=(bl, nt),
        out_shape=[jax.ShapeDtypeStruct((bl, seq, D), F32), jax.ShapeDtypeStruct((bl, seq, D), MXU_DTYPE),
                   jax.ShapeDtypeStruct((bl, seq, 2 * DFF), MXU_DTYPE), jax.ShapeDtypeStruct((8, D), F32),
                   jax.ShapeDtypeStruct((8, DFF), F32), jax.ShapeDtypeStruct((bl, 8, D), F32)],
        in_specs=[rev(D), rev(D), rev(D), rev(2 * DFF), halo, pl.BlockSpec((None, 8, D), lambda b, i: (b, 0, 0)),
                  _ANY, _ANY, _full(f3.shape), _full(ln2.shape)],
        out_specs=[rev(D), rev(D), rev(2 * DFF), _full((8, D)), _full((8, DFF)),
                   pl.BlockSpec((None, 8, D), lambda b, i: (b, 0, 0))],
        scratch_shapes=[pltpu.VMEM(wup_t.shape, MXU_DTYPE), pltpu.VMEM(wdown.shape, MXU_DTYPE),
                        pltpu.VMEM((t + HALO, FF_CHUNK), F32), pltpu.VMEM((n_chunk, t + HALO, FF_CHUNK), F32),
                        pltpu.VMEM((FFN_K - 1, t, FF_CHUNK), F32), pltpu.VMEM((t, FF_CHUNK), F32),
                        pltpu.VMEM((t, D), F32), pltpu.SemaphoreType.DMA((2,))],
        compiler_params=_params(2),
    )(x1, y2, tgt, hh, hh, mods, wup_t, wdown, f3, ln2)


def _mixer_bwd(x, dx1, proj, h, vc, mixed, mods, win_t, wout, wri, mavg2, va, vb, ln1, ffn_wgrads):
    bl, seq, _ = x.shape
    t = min(TOK_TILE, seq)
    nt = seq // t
    pw = 2 * LW + 2 * CW
    n_g = len(ffn_wgrads)
    rb = min(ROW_BLK, t)
    half_f32 = [pltpu.VMEM((t, LW), F32)] * 10

    def body(x_ref, dx1_ref, proj_ref, phalo_ref, h_ref, hhalo_ref, vc_ref, mixed_ref, mod_ref,
             win_hbm, wout_hbm, wri_ref, mavg_ref, va_ref, vb_ref, ln1_ref, *rest):
        wgrad_refs, rest = rest[:n_g], rest[n_g:]
        gx_ref, dproj_ref, dm_ref, ga_ref, gw31_ref, dwr_ref, dwi_ref, dln1_ref, dmod_ref = rest[:9]
        slot_refs, rest = rest[9:9 + n_g], rest[9 + n_g:]
        (win_v, wout_v, xa_ext, vg_scr, dvc_ext, dxc_ext, rot_b, car, sems,
         st_a, st_b, st_c, yn_scr, dyn_scr, hprev_scr, xc_scr, a_scr, mult_scr, g_scr,
         wide, gates, hilo_a, hilo_b, dgates, xcb_scr, rot_c, rot_d, hp_ext, e_send, e_recv, e_local) = rest
        b, i = pl.program_id(0), pl.program_id(1)
        tt = nt - 1 - i
        e_start, e_finish = _exchange_steps(wgrad_refs, slot_refs, e_send, e_recv, e_local)

        @pl.when((b == 0) & (i == 0))
        def _():
            e_start()
            cps = [pltpu.make_async_copy(win_hbm, win_v, sems.at[0]), pltpu.make_async_copy(wout_hbm, wout_v, sems.at[1])]
            for cp in cps:
                cp.start()
            for cp in cps:
                cp.wait()
            for ref in (ga_ref, gw31_ref, dwr_ref, dwi_ref, dln1_ref):
                ref[...] = jnp.zeros_like(ref)

        @pl.when(i == 0)
        def _():
            dmod_ref[...] = jnp.zeros_like(dmod_ref)
            dvc_ext[t:t + HALO31, :] = jnp.zeros((HALO31, CW), F32)
            dxc_ext[t:t + HALO, :] = jnp.zeros((HALO, LW), F32)
            car[...] = jnp.zeros_like(car)

        halo_keep = (tt > 0).astype(F32)
        zero_d = jnp.zeros((8, D), F32)
        zero_l = jnp.zeros((8, LANE), F32)
        blocks = [(r0, c0) for c0 in range(0, LW, LANE) for r0 in range(0, t, rb)]

        def add_row(ref, row, c0, acc):
            ref[row:row + 1, c0:c0 + LANE] += _colsum(acc)

        def ln_rows(r0, acc):
            rows = pl.ds(r0, 8)
            xv, mixed = x_ref[rows, :], mixed_ref[rows, :]
            gt = mod_ref[2:3, :]
            xh, rstd = _layer_norm_stats(ALPHA * xv + (1.0 + gt) * mixed)
            dx1 = dx1_ref[rows, :]
            dxh = dx1 * ln1_ref[0:1, :]
            dr1 = rstd * (dxh - _rowmean(dxh) - xh * _rowmean(dxh * xh))
            wide[rows, :] = (1.0 + gt) * dr1
            gx_ref[rows, :] = ALPHA * dr1
            return acc[0] + dx1 * xh, acc[1] + dx1, acc[2] + dr1 * mixed

        acc = _row_blocks(t, 8, ln_rows, (zero_d, zero_d, zero_d))
        dln1_ref[0:1, :] += _colsum(acc[0])
        dln1_ref[1:2, :] += _colsum(acc[1])
        dmod_ref[2:3, :] += _colsum(acc[2])
        _cast_rows(wide, dm_ref, t)
        wide[...] = _dot_nt(dm_ref[...], wout_v[...])

        mavg2 = mavg_ref[...]
        for r0, c0 in blocks:
            rows, cols = pl.ds(r0, rb), slice(c0, c0 + LANE)
            _store_hilo(hilo_a, rows, c0, vc_ref[rows, cols])
        st_a[...] = _dot(hilo_a[...], mavg2)
        for r0, c0 in blocks:
            rows, cols = pl.ds(r0, rb), slice(c0, c0 + LANE)
            dl = vc_ref[rows, cols] - st_a[rows, cols]
            yn_scr[rows, cols] = dl
            _store_hilo(hilo_a, rows, c0, dl * dl)
        st_b[...] = _dot(hilo_a[...], mavg2)
        for c0 in range(0, CW, LANE):
            cols, pc = slice(c0, c0 + LANE), 2 * LW + c0
            ng, nb = vb_ref[CONV_K + 1:CONV_K + 2, cols], vb_ref[CONV_K + 2:CONV_K + 3, cols]
            acc_g, acc_b = zero_l, zero_l
            for r0 in range(0, t, rb):
                rows = pl.ds(r0, rb)
                rs = lax.rsqrt(st_b[rows, cols] + LN_EPS)
                yn = yn_scr[rows, cols] * rs
                yl = yn * ng + nb
                s = _sigmoid(yl)
                dyl = wide[rows, LW + c0:LW + c0 + LANE] * (s * (1.0 + yl * (1.0 - s)))
                acc_g, acc_b = acc_g + _fold8(dyl * yn), acc_b + _fold8(dyl)
                dyn = dyl * ng
                st_b[rows, cols] = rs
                yn_scr[rows, cols] = yn
                dyn_scr[rows, cols] = dyn
                _store_hilo(hilo_a, rows, c0, dyn)
                _store_hilo(hilo_b, rows, c0, dyn * yn)
                vg_scr[rows, cols] = proj_ref[rows, pc:pc + LANE] * _sigmoid(proj_ref[rows, pc + CW:pc + CW + LANE])
            add_row(ga_ref, 8, c0, acc_g)
            add_row(ga_ref, 9, c0, acc_b)
        st_a[...] = _dot(hilo_a[...], mavg2)
        st_c[...] = _dot(hilo_b[...], mavg2)
        for c0 in range(0, CW, LANE):
            cols = slice(c0, c0 + LANE)
            acc_b = zero_l
            for r0 in range(0, t, rb):
                rows = pl.ds(r0, rb)
                dvc = st_b[rows, cols] * (dyn_scr[rows, cols] - st_a[rows, cols] - yn_scr[rows, cols] * st_c[rows, cols])
                dvc_ext[rows, cols] = dvc
                acc_b = acc_b + _fold8(dvc)
            add_row(ga_ref, 10, c0, acc_b)
        dvc_win = _shifted_rows(dvc_ext, rot_b, rb)
        for c0 in range(0, CW, LANE):
            cols, pc = slice(c0, c0 + LANE), 2 * LW + c0
            for k in range(CONV_K):
                acc_w = zero_l
                for r0 in range(0, t, rb):
                    acc_w = acc_w + _fold8(vg_scr[pl.ds(r0, rb), cols] * dvc_win(CONV_K - 1 - k, r0, cols))
                add_row(gw31_ref, k, c0, acc_w)
            for r0 in range(0, t, rb):
                rows = pl.ds(r0, rb)
                dvg = jnp.zeros((rb, LANE), F32)
                for k in range(CONV_K):
                    dvg = dvg + vb_ref[k:k + 1, cols] * dvc_win(CONV_K - 1 - k, r0, cols)
                vbr = proj_ref[rows, pc:pc + LANE]
                sgb = _sigmoid(proj_ref[rows, pc + CW:pc + CW + LANE])
                dproj_ref[rows, pc:pc + LANE] = (dvg * sgb).astype(MXU_DTYPE)
                dproj_ref[rows, pc + CW:pc + CW + LANE] = (dvg * vbr * sgb * (1.0 - sgb)).astype(MXU_DTYPE)
        dvc_ext[t:t + HALO31, :] = dvc_ext[0:HALO31, :]

        xa_ext[0:HALO, :] = phalo_ref[HALO31 - HALO:HALO31, 0:LW] * halo_keep
        xa_ext[HALO:HALO + t, :] = proj_ref[:, 0:LW]
        for k in range(LRU_K - 1):
            rot_c[k] = xa_ext[pl.ds(HALO - (LRU_K - 1) + k, t), :]
        hp_ext[0:HALO, :] = hhalo_ref[...] * halo_keep
        hp_ext[HALO:HALO + t, :] = h_ref[...]
        hprev_scr[...] = hp_ext[pl.ds(HALO - 1, t), :]

        def xa_tap(k, rows, r0, cols):
            return xa_ext[pl.ds(HALO + r0, rb), cols] if k == LRU_K - 1 else rot_c[k, rows, cols]

        for r0, c0 in blocks:
            rows, cols = pl.ds(r0, rb), slice(c0, c0 + LANE)
            xc = va_ref[4:5, cols]
            for k in range(LRU_K):
                xc = xc + va_ref[k:k + 1, cols] * xa_tap(k, rows, r0, cols)
            xc_scr[rows, cols] = xc
            xcb_scr[rows, cols] = xc.astype(MXU_DTYPE)
        gates[...] = _dot(xcb_scr[...], wri_ref[...])
        for r0, c0 in blocks:
            rows, cols = pl.ds(r0, rb), slice(c0, c0 + LANE)
            r = _sigmoid(gates[rows, cols] + va_ref[5:6, cols])
            ig = _sigmoid(gates[rows, LW + c0:LW + c0 + LANE] + va_ref[6:7, cols])
            la = (-LRU_C) * r * _softplus(-va_ref[7:8, cols])
            a = jnp.exp(la)
            gates[rows, cols] = r
            gates[rows, LW + c0:LW + c0 + LANE] = ig
            a_scr[rows, cols] = a
            mult_scr[rows, cols] = jnp.sqrt(-jnp.tanh(la) * (a * a + 1.0))
            ge, dge = _gelu_and_grad(proj_ref[rows, LW + c0:LW + c0 + LANE])
            dya = wide[rows, cols]
            dproj_ref[rows, LW + c0:LW + c0 + LANE] = (dya * h_ref[rows, cols] * dge).astype(MXU_DTYPE)
            g_scr[rows, cols] = dya * ge
        _scan_bwd_blocks(a_scr, g_scr, car, t)
        for c0 in range(0, LW, LANE):
            cols = slice(c0, c0 + LANE)
            sp = _softplus(-va_ref[7:8, cols])
            acc_l, acc_r, acc_i = zero_l, zero_l, zero_l
            for r0 in range(0, t, rb):
                rows = pl.ds(r0, rb)
                gv, xc, a, mult = g_scr[rows, cols], xc_scr[rows, cols], a_scr[rows, cols], mult_scr[rows, cols]
                r, ig = gates[rows, cols], gates[rows, LW + c0:LW + c0 + LANE]
                dla = gv * hprev_scr[rows, cols] * a - gv * (ig * xc) * (a * a / mult)
                acc_l = acc_l + _fold8(dla * ((-LRU_C) * r))
                dgr = dla * ((-LRU_C) * sp) * r * (1.0 - r)
                dgi = gv * (mult * xc) * ig * (1.0 - ig)
                acc_r, acc_i = acc_r + _fold8(dgr), acc_i + _fold8(dgi)
                dgates[rows, cols] = dgr.astype(MXU_DTYPE)
                dgates[rows, LW + c0:LW + c0 + LANE] = dgi.astype(MXU_DTYPE)
                dxc_ext[rows, cols] = gv * (mult * ig)
            add_row(ga_ref, 5, c0, acc_r)
            add_row(ga_ref, 6, c0, acc_i)
            ga_ref[7:8, cols] += _colsum(acc_l) * (-_sigmoid(-va_ref[7:8, cols]))
        dwr_ref[...] += _dot_tn(xcb_scr[...], dgates[:, 0:LW])
        dwi_ref[...] += _dot_tn(xcb_scr[...], dgates[:, LW:])
        st_a[...] = _dot_nt(dgates[...], wri_ref[...])
        for c0 in range(0, LW, LANE):
            cols = slice(c0, c0 + LANE)
            acc_b = zero_l
            for r0 in range(0, t, rb):
                rows = pl.ds(r0, rb)
                dxc = dxc_ext[rows, cols] + st_a[rows, cols]
                dxc_ext[rows, cols] = dxc
                acc_b = acc_b + _fold8(dxc)
            add_row(ga_ref, 4, c0, acc_b)
        for k in range(LRU_K - 1):
            rot_d[k] = dxc_ext[pl.ds(k + 1, t), :]
        for c0 in range(0, LW, LANE):
            cols = slice(c0, c0 + LANE)
            acc_w = [zero_l] * LRU_K
            for r0 in range(0, t, rb):
                rows = pl.ds(r0, rb)
                dxc = dxc_ext[rows, cols]
                dxa = va_ref[LRU_K - 1:LRU_K, cols] * dxc
                for k in range(LRU_K):
                    acc_w[k] = acc_w[k] + _fold8(dxc * xa_tap(k, rows, r0, cols))
                    if k < LRU_K - 1:
                        dxa = dxa + va_ref[k:k + 1, cols] * rot_d[LRU_K - 2 - k, rows, cols]
                dproj_ref[rows, cols] = dxa.astype(MXU_DTYPE)
            for k in range(LRU_K):
                add_row(ga_ref, k, c0, acc_w[k])
        dxc_ext[t:t + HALO, :] = dxc_ext[0:HALO, :]

        wide[...] = _dot(dproj_ref[...], win_v[...])

        def out_rows(r0, acc):
            rows = pl.ds(r0, 8)
            du1 = wide[rows, :]
            gx_ref[rows, :] = gx_ref[rows, :] + du1 * (1.0 + mod_ref[1:2, :])
            return acc[0] + du1 * x_ref[rows, :], acc[1] + du1

        acc = _row_blocks(t, 8, out_rows, (zero_d, zero_d))
        dmod_ref[1:2, :] += _colsum(acc[0])
        dmod_ref[0:1, :] += _colsum(acc[1])

        pl.when((b == bl - 1) & (i == nt - 1))(e_finish)

    rev = lambda w: pl.BlockSpec((None, t, w), lambda b, i: (b, nt - 1 - i, 0))

    def halo(rows_, w):
        return pl.BlockSpec((None, rows_, w), lambda b, i: (b, jnp.maximum((nt - 1 - i) * (t // rows_) - 1, 0), 0))

    return pl.pallas_call(
        body, name="mixer_bwd", grid=(bl, nt),
        out_shape=[jax.ShapeDtypeStruct((bl, seq, D), F32), jax.ShapeDtypeStruct((bl, seq, pw), MXU_DTYPE),
                   jax.ShapeDtypeStruct((bl, seq, D), MXU_DTYPE), jax.ShapeDtypeStruct((16, LW), F32),
                   jax.ShapeDtypeStruct((32, CW), F32), jax.ShapeDtypeStruct((LW, LW), F32),
                   jax.ShapeDtypeStruct((LW, LW), F32), jax.ShapeDtypeStruct((8, D), F32),
                   jax.ShapeDtypeStruct((bl, 8, D), F32)] + _exchange_out_shapes(ffn_wgrads),
        in_specs=[rev(D), rev(D), rev(pw), halo(HALO31, pw), rev(LW), halo(HALO, LW), rev(CW), rev(D),
                  pl.BlockSpec((None, 8, D), lambda b, i: (b, 0, 0)), _ANY, _ANY,
                  _full(wri.shape), _full(mavg2.shape), _full(va.shape), _full(vb.shape), _full(ln1.shape)]
        + [_ANY] * n_g,
        out_specs=[rev(D), rev(pw), rev(D), _full((16, LW)), _full((32, CW)), _full((LW, LW)), _full((LW, LW)),
                   _full((8, D)), pl.BlockSpec((None, 8, D), lambda b, i: (b, 0, 0))] + [_ANY] * n_g,
        scratch_shapes=[pltpu.VMEM(win_t.shape, MXU_DTYPE), pltpu.VMEM(wout.shape, MXU_DTYPE),
                        pltpu.VMEM((t + HALO, LW), F32), pltpu.VMEM((t, CW), F32),
                        pltpu.VMEM((t + HALO31, CW), F32), pltpu.VMEM((t + HALO, LW), F32),
                        pltpu.VMEM((7, t + HALO31 - 8, CW), F32),
                        pltpu.VMEM((8, LW), F32), pltpu.SemaphoreType.DMA((2,))] + half_f32
        + [pltpu.VMEM((t, D), F32), pltpu.VMEM((t, 2 * LW), F32), pltpu.VMEM((t, 2 * CW), MXU_DTYPE),
           pltpu.VMEM((t, 2 * CW), MXU_DTYPE), pltpu.VMEM((t, 2 * LW), MXU_DTYPE), pltpu.VMEM((t, LW), MXU_DTYPE),
           pltpu.VMEM((LRU_K - 1, t, LW), F32), pltpu.VMEM((LRU_K - 1, t, LW), F32), pltpu.VMEM((t + HALO, LW), F32)]
        + _comm_sems(n_g),
        compiler_params=_params(2),
    )(x, dx1, proj, proj, h, h, vc, mixed, mods, win_t, wout, wri, mavg2, va, vb, ln1, *ffn_wgrads)


def _weight_grad(a, bmat, tm, name, out_dtype=F32, exchange=()):
    ntok, m = a.shape
    n = bmat.shape[1]
    tk = min(WG_TOK_TILE, ntok)
    nk = ntok // tk
    ni = m // tm
    n_e = len(exchange)

    def body(a_ref, b_ref, *rest):
        e_in, o_ref, e_out = rest[:n_e], rest[n_e], rest[n_e + 1:2 * n_e + 1]
        acc_ref = rest[2 * n_e + 1]
        i, k = pl.program_id(0), pl.program_id(1)
        if n_e:
            e_start, e_finish = _exchange_steps(e_in, e_out, *rest[2 * n_e + 2:])
            pl.when((i == 0) & (k == 0))(e_start)

        @pl.when(k == 0)
        def _():
            acc_ref[...] = jnp.zeros_like(acc_ref)

        acc_ref[...] += _dot_tn(a_ref[...], b_ref[...])

        @pl.when(k == nk - 1)
        def _():
            o_ref[...] = acc_ref[...].astype(out_dtype)

        if n_e:
            pl.when((i == ni - 1) & (k == nk - 1))(e_finish)

    res = pl.pallas_call(
        body, name=name, grid=(ni, nk),
        out_shape=[jax.ShapeDtypeStruct((m, n), out_dtype)] + _exchange_out_shapes(exchange),
        in_specs=[pl.BlockSpec((tk, tm), lambda i, k: (k, i)), pl.BlockSpec((tk, n), lambda i, k: (k, 0))] + [_ANY] * n_e,
        out_specs=[pl.BlockSpec((tm, n), lambda i, k: (i, 0))] + [_ANY] * n_e,
        scratch_shapes=[pltpu.VMEM((tm, n), F32)] + (_comm_sems(n_e) if n_e else []),
        compiler_params=_params(2),
    )(a, bmat, *exchange)
    return res if n_e else res[0]


def _sum_slots(slots, name):
    _, r, cdim = slots.shape
    tr = 32 if r % 32 == 0 else r

    def body(s_ref, o_ref):
        acc = s_ref[0].astype(F32)
        for j in range(1, NDEV):
            acc = acc + s_ref[j].astype(F32)
        o_ref[...] = acc

    return pl.pallas_call(
        body, name=name, grid=(r // tr,), out_shape=jax.ShapeDtypeStruct((r, cdim), F32),
        in_specs=[pl.BlockSpec((NDEV, tr, cdim), lambda i: (0, i, 0))],
        out_specs=pl.BlockSpec((tr, cdim), lambda i: (i, 0)),
        compiler_params=_params(1),
    )(slots)


def _small_grad_layout(bl):
    r0 = 16 * bl
    return {"dmod_a": 0, "dmod_b": 8 * bl, "g_a": r0, "g_w31": r0 + 16, "dln1": r0 + 48, "dln2": r0 + 56,
            "df3": r0 + 64, "g_wr": r0 + 88, "g_wi": r0 + 120, "loss": r0 + 152, "rows": r0 + 160}


def _pack_small_grads(dmod_a, dmod_b, g_a, g_w31, dln1, dln2, df3, g_wr, g_wi, loss_part):
    bl = dmod_a.shape[0] // 8
    lay = _small_grad_layout(bl)
    ins = [dmod_a, dmod_b, g_a, g_w31, dln1, dln2, df3, g_wr, g_wi, loss_part]

    def body(ma_ref, mb_ref, ga_ref, gw_ref, l1_ref, l2_ref, f3_ref, wr_ref, wi_ref, ls_ref, o_ref):
        o_ref[...] = jnp.zeros_like(o_ref)
        o_ref[lay["loss"]:lay["loss"] + 8, 0:LANE] = ls_ref[...]
        o_ref[lay["dmod_a"]:lay["dmod_a"] + 8 * bl, :] = ma_ref[...]
        o_ref[lay["dmod_b"]:lay["dmod_b"] + 8 * bl, :] = mb_ref[...]
        o_ref[lay["g_a"]:lay["g_a"] + 16, 0:LW] = ga_ref[...]
        o_ref[lay["g_w31"]:lay["g_w31"] + 32, 0:CW] = gw_ref[...]
        o_ref[lay["dln1"]:lay["dln1"] + 8, :] = l1_ref[...]
        o_ref[lay["dln2"]:lay["dln2"] + 8, :] = l2_ref[...]
        for j, c0 in enumerate(range(0, DFF, D)):
            w = min(D, DFF - c0)
            o_ref[lay["df3"] + 8 * j:lay["df3"] + 8 * j + 8, 0:w] = f3_ref[:, c0:c0 + w]
        o_ref[lay["g_wr"]:lay["g_wr"] + 32, :] = wr_ref[...]
        o_ref[lay["g_wi"]:lay["g_wi"] + 32, :] = wi_ref[...]

    return pl.pallas_call(
        body, name="pack_small_grads", out_shape=jax.ShapeDtypeStruct((lay["rows"], D), F32),
        in_specs=[_full(a.shape) for a in ins], out_specs=_full((lay["rows"], D)),
        compiler_params=pltpu.CompilerParams(vmem_limit_bytes=VMEM_LIMIT),
    )(*ins)


def _sum_small_grads(packed_all, bl):
    lay = _small_grad_layout(bl)
    shapes = {"lru_conv_w": (LRU_K, LW), "lru_conv_b": (1, LW), "lru_b_r": (1, LW), "lru_b_i": (1, LW),
              "lru_lambda": (1, LW), "conv_norm_g": (1, CW), "conv_norm_b": (1, CW), "conv_b": (1, CW),
              "conv_w": (CONV_K, CW), "ln1_g": (1, D), "ln1_b": (1, D), "ln2_g": (1, D), "ln2_b": (1, D),
              "ffn_conv_w": (FFN_K, DFF), "ffn_conv_b": (1, DFF), "lru_w_r": (32, D), "lru_w_i": (32, D),
              "loss": (8, LANE)}
    names = list(shapes)

    def body(p_ref, *outs):
        o = dict(zip(names, outs))

        def part(r0, nr, c0=0, nc=D):
            acc = p_ref[0, r0:r0 + nr, c0:c0 + nc]
            for j in range(1, NDEV):
                acc = acc + p_ref[j, r0:r0 + nr, c0:c0 + nc]
            return acc

        ga = part(lay["g_a"], 16, 0, LW)
        o["lru_conv_w"][...] = ga[0:LRU_K]
        for row, k in ((4, "lru_conv_b"), (5, "lru_b_r"), (6, "lru_b_i"), (7, "lru_lambda"), (8, "conv_norm_g"),
                       (9, "conv_norm_b"), (10, "conv_b")):
            o[k][...] = ga[row:row + 1]
        o["conv_w"][...] = part(lay["g_w31"], 32, 0, CW)[0:CONV_K]
        l1, l2 = part(lay["dln1"], 8), part(lay["dln2"], 8)
        o["ln1_g"][...], o["ln1_b"][...] = l1[0:1], l1[1:2]
        o["ln2_g"][...], o["ln2_b"][...] = l2[0:1], l2[1:2]
        for j, c0 in enumerate(range(0, DFF, D)):
            w = min(D, DFF - c0)
            f3 = part(lay["df3"] + 8 * j, 8, 0, w)
            o["ffn_conv_w"][:, c0:c0 + w] = f3[0:FFN_K]
            o["ffn_conv_b"][:, c0:c0 + w] = f3[FFN_K:FFN_K + 1]
        o["lru_w_r"][...] = part(lay["g_wr"], 32)
        o["lru_w_i"][...] = part(lay["g_wi"], 32)
        o["loss"][...] = part(lay["loss"], 8, 0, LANE)

    res = pl.pallas_call(
        body, name="sum_small_grads", out_shape=[jax.ShapeDtypeStruct(shapes[k], F32) for k in names],
        in_specs=[_full(packed_all.shape)], out_specs=[_full(shapes[k]) for k in names],
        compiler_params=pltpu.CompilerParams(vmem_limit_bytes=VMEM_LIMIT),
    )(packed_all)
    return dict(zip(names, res))


def _adamw_update(w_ref, g_ref, m_ref, v_ref, d_ref, nm_ref, nv_ref):
    gv = g_ref[...]
    nm = ADAM_B1 * m_ref[...] + (1.0 - ADAM_B1) * gv
    nv = ADAM_B2 * v_ref[...] + (1.0 - ADAM_B2) * (gv * gv)
    m_hat = nm / (1.0 - ADAM_B1 ** ADAM_STEP)
    v_hat = nv / (1.0 - ADAM_B2 ** ADAM_STEP)
    d_ref[...] = -ADAM_LR * (m_hat / (jnp.sqrt(v_hat) + ADAM_EPS) + ADAM_WD * w_ref[...])
    nm_ref[...] = nm
    nv_ref[...] = nv


def _sum_adamw(slots, w, m, v, name):
    _, r, cdim = slots.shape
    tr = 32 if r % 32 == 0 else r

    def body(s_ref, w_ref, m_ref, v_ref, g_ref, d_ref, nm_ref, nv_ref):
        acc = s_ref[0].astype(F32)
        for j in range(1, NDEV):
            acc = acc + s_ref[j].astype(F32)
        g_ref[...] = acc
        _adamw_update(w_ref, g_ref, m_ref, v_ref, d_ref, nm_ref, nv_ref)

    spec = pl.BlockSpec((tr, cdim), lambda i: (i, 0))
    return pl.pallas_call(
        body, name=name, grid=(r // tr,), out_shape=[jax.ShapeDtypeStruct((r, cdim), F32)] * 4,
        in_specs=[pl.BlockSpec((NDEV, tr, cdim), lambda i: (0, i, 0))] + [spec] * 3, out_specs=[spec] * 4,
        compiler_params=_params(1),
    )(slots, w, m, v)


def _adamw_many(ws, gs, ms, vs, name):
    n = len(ws)

    def body(*refs):
        ins, outs = refs[:4 * n], refs[4 * n:]
        for k in range(n):
            _adamw_update(ins[k], ins[n + k], ins[2 * n + k], ins[3 * n + k], outs[k], outs[n + k], outs[2 * n + k])

    specs = [_full(w.shape) for w in ws]
    res = pl.pallas_call(
        body, name=name, out_shape=[jax.ShapeDtypeStruct(w.shape, F32) for w in ws] * 3,
        in_specs=specs * 4, out_specs=specs * 3,
        compiler_params=pltpu.CompilerParams(vmem_limit_bytes=VMEM_LIMIT),
    )(*ws, *gs, *ms, *vs)
    return res[:n], res[n:2 * n], res[2 * n:]


def _adamw(w, g, m, v, name):
    r, cdim = w.shape
    tr = 128 if r % 128 == 0 else r
    body = functools.partial(_adamw_update)
    spec = pl.BlockSpec((tr, cdim), lambda i: (i, 0))
    return pl.pallas_call(
        body, name=name, grid=(r // tr,), out_shape=[jax.ShapeDtypeStruct((r, cdim), F32)] * 3,
        in_specs=[spec] * 4, out_specs=[spec] * 3, compiler_params=_params(1),
    )(w, g, m, v)


def _pack(arrs, width=D, row_mult=8):
    parts = []
    for a in arrs:
        flat = a.reshape(-1)
        pad = (-flat.shape[0]) % width
        parts.append(jnp.pad(flat, (0, pad)))
    flat = jnp.concatenate(parts)
    pad = (-flat.shape[0]) % (width * row_mult)
    return jnp.pad(flat, (0, pad)).reshape(-1, width)


def _unpack(buf, shapes, width=D):
    out, row = [], 0
    for shp in shapes:
        size = math.prod(shp)
        nrow = -(-size // width)
        out.append(buf[row:row + nrow].reshape(-1)[:size].reshape(shp))
        row += nrow
    return out


def _unpack_gathered(buf, shapes, width=D):
    out, row = [], 0
    for shp in shapes:
        size = math.prod(shp)
        nrow = -(-size // width)
        out.append(buf[:, row:row + nrow].reshape(NDEV, -1)[:, :size].reshape((NDEV,) + tuple(shp)))
        row += nrow
    return out


def _block_diag(w):
    hn, dh, _ = w.shape
    eye = jnp.eye(hn, dtype=w.dtype)
    return (w[:, :, None, :] * eye[:, None, :, None]).reshape(hn * dh, hn * dh)


def _diag_blocks(wfull, hn):
    dh = wfull.shape[0] // hn
    keep = np.eye(hn, dtype=bool)[:, None, :, None]
    return jnp.where(keep, wfull.reshape(hn, dh, hn, dh), 0.0).sum(axis=2)


def _ungather_cols(g, k):
    n = g.shape[1]
    return g.reshape(NDEV, k, n).transpose(1, 0, 2).reshape(k, NDEV * n)


def _pad_rows(a, rows):
    return jnp.pad(a, ((0, rows - a.shape[0]), (0, 0)))


def kernel(x, c, w_ada, b_ada, w_in, lru_conv_w, lru_conv_b, lru_w_r, lru_b_r, lru_w_i, lru_b_i, lru_lambda, conv_w, conv_b, conv_norm_g, conv_norm_b, w_out, ln1_g, ln1_b, ffn_w_up, ffn_conv_w, ffn_conv_b, ffn_w_down, ln2_g, ln2_b, loss_target, m_w_ada, m_b_ada, m_w_in, m_lru_conv_w, m_lru_conv_b, m_lru_w_r, m_lru_b_r, m_lru_w_i, m_lru_b_i, m_lru_lambda, m_conv_w, m_conv_b, m_conv_norm_g, m_conv_norm_b, m_w_out, m_ln1_g, m_ln1_b, m_ffn_w_up, m_ffn_conv_w, m_ffn_conv_b, m_ffn_w_down, m_ln2_g, m_ln2_b, v_w_ada, v_b_ada, v_w_in, v_lru_conv_w, v_lru_conv_b, v_lru_w_r, v_lru_b_r, v_lru_w_i, v_lru_b_i, v_lru_lambda, v_conv_w, v_conv_b, v_conv_norm_g, v_conv_norm_b, v_w_out, v_ln1_g, v_ln1_b, v_ffn_w_up, v_ffn_conv_w, v_ffn_conv_b, v_ffn_w_down, v_ln2_g, v_ln2_b):
    weights = dict(w_ada=w_ada, b_ada=b_ada, w_in=w_in, lru_conv_w=lru_conv_w, lru_conv_b=lru_conv_b, lru_w_r=lru_w_r,
                   lru_b_r=lru_b_r, lru_w_i=lru_w_i, lru_b_i=lru_b_i, lru_lambda=lru_lambda, conv_w=conv_w, conv_b=conv_b,
                   conv_norm_g=conv_norm_g, conv_norm_b=conv_norm_b, w_out=w_out, ln1_g=ln1_g, ln1_b=ln1_b,
                   ffn_w_up=ffn_w_up, ffn_conv_w=ffn_conv_w, ffn_conv_b=ffn_conv_b, ffn_w_down=ffn_w_down, ln2_g=ln2_g,
                   ln2_b=ln2_b)
    mom_m = dict(w_ada=m_w_ada, b_ada=m_b_ada, w_in=m_w_in, lru_conv_w=m_lru_conv_w, lru_conv_b=m_lru_conv_b,
                 lru_w_r=m_lru_w_r, lru_b_r=m_lru_b_r, lru_w_i=m_lru_w_i, lru_b_i=m_lru_b_i, lru_lambda=m_lru_lambda,
                 conv_w=m_conv_w, conv_b=m_conv_b, conv_norm_g=m_conv_norm_g, conv_norm_b=m_conv_norm_b, w_out=m_w_out,
                 ln1_g=m_ln1_g, ln1_b=m_ln1_b, ffn_w_up=m_ffn_w_up, ffn_conv_w=m_ffn_conv_w, ffn_conv_b=m_ffn_conv_b,
                 ffn_w_down=m_ffn_w_down, ln2_g=m_ln2_g, ln2_b=m_ln2_b)
    mom_v = dict(w_ada=v_w_ada, b_ada=v_b_ada, w_in=v_w_in, lru_conv_w=v_lru_conv_w, lru_conv_b=v_lru_conv_b,
                 lru_w_r=v_lru_w_r, lru_b_r=v_lru_b_r, lru_w_i=v_lru_w_i, lru_b_i=v_lru_b_i, lru_lambda=v_lru_lambda,
                 conv_w=v_conv_w, conv_b=v_conv_b, conv_norm_g=v_conv_norm_g, conv_norm_b=v_conv_norm_b, w_out=v_w_out,
                 ln1_g=v_ln1_g, ln1_b=v_ln1_b, ffn_w_up=v_ffn_w_up, ffn_conv_w=v_ffn_conv_w, ffn_conv_b=v_ffn_conv_b,
                 ffn_w_down=v_ffn_w_down, ln2_g=v_ln2_g, ln2_b=v_ln2_b)
    names = list(weights)
    bl, seq, _ = x.shape
    ntok = bl * seq
    me = 4 * lax.axis_index("x") + 2 * lax.axis_index("y") + lax.axis_index("c")

    small_shapes = [(bl, D), (LRU_K, LW // NDEV), (CONV_K, CW // NDEV), (FFN_K, DFF // NDEV)]
    small = _pack([c, lru_conv_w[0], conv_w[0], ffn_conv_w[0]], width=128)
    n_small = small.shape[0]
    small_all, win_t, wout_b = _all_gather(
        [small, w_in[0].T.astype(MXU_DTYPE), w_out[0].astype(MXU_DTYPE)], "gather_small_and_mixer_weights")
    small_all = small_all.reshape(NDEV, n_small, 128)
    c_parts, k4_parts, k31_parts, k3_parts = _unpack_gathered(small_all, small_shapes, width=128)
    c_all = c_parts.reshape(NDEV * bl, D)
    lru_conv_w_f = k4_parts.transpose(1, 0, 2).reshape(LRU_K, LW)
    conv_w_f = k31_parts.transpose(1, 0, 2).reshape(CONV_K, CW)
    ffn_conv_w_f = k3_parts.transpose(1, 0, 2).reshape(FFN_K, DFF)

    ncol = w_ada.shape[2]
    b_ada_loc = lax.dynamic_slice(b_ada, (0, me * ncol), (1, ncol))
    mod_cols = _ada_fwd(c_all, w_ada[0], b_ada_loc)
    (mod_all,) = _all_gather([mod_cols], "gather_mod")
    mod_all = mod_all.reshape(NDEV, NDEV * bl, ncol)
    mod_mine = lax.dynamic_slice(mod_all, (0, me * bl, 0), (NDEV, bl, ncol))
    mods = mod_mine.transpose(1, 0, 2).reshape(bl, 6, D)
    mods = jnp.pad(mods, ((0, 0), (0, 2), (0, 0)))

    ffn_shards = [ffn_w_up[0].T.astype(MXU_DTYPE), ffn_w_down[0].astype(MXU_DTYPE)]

    wr_bd = _block_diag(lru_w_r[0]).astype(MXU_DTYPE)
    wi_bd = _block_diag(lru_w_i[0]).astype(MXU_DTYPE)
    mavg_np = np.kron(np.eye(CW // HEAD, dtype=np.float32), np.full((HEAD, HEAD), 1.0 / HEAD, np.float32))
    mavg = jnp.asarray(mavg_np, MXU_DTYPE)
    mavg2 = jnp.asarray(np.concatenate([mavg_np, mavg_np], axis=0), MXU_DTYPE)
    wri = jnp.concatenate([wr_bd, wi_bd], axis=1)
    va = jnp.concatenate([lru_conv_w_f, lru_conv_b, lru_b_r, lru_b_i, lru_lambda], axis=0)
    vb = _pad_rows(jnp.concatenate([conv_w_f, conv_b, conv_norm_g, conv_norm_b], axis=0), 40)
    ln1 = _pad_rows(jnp.concatenate([ln1_g, ln1_b], axis=0), 8)
    ln2 = _pad_rows(jnp.concatenate([ln2_g, ln2_b], axis=0), 8)
    f3 = _pad_rows(jnp.concatenate([ffn_conv_w_f, ffn_conv_b], axis=0), 8)

    proj, h, vc, y_b, mixed, x1, u1_b, wup_t, wdown_b = _mixer_fwd(
        x, mods, win_t, wout_b, wri, mavg2, va, vb, ln1, ffn_shards)
    hh, z_b, u2_b, y2, loss_part = _ffn_fwd(x1, mods, loss_target, wup_t, wdown_b, f3, ln2)

    dx1, dy2_b, dhh_b, dln2, df3, dmod_b = _ffn_bwd(x1, y2, loss_target, hh, mods, wup_t, wdown_b, f3, ln2)
    g_down = _weight_grad(z_b.reshape(ntok, DFF), dy2_b.reshape(ntok, D), FF_CHUNK, "wgrad_down")
    g_up_t = _weight_grad(dhh_b.reshape(ntok, 2 * DFF), u2_b.reshape(ntok, D), FF_CHUNK, "wgrad_up")
    grad_x, dproj_b, dm_b, g_a, g_w31, g_wr, g_wi, dln1, dmod_a, s_up, s_down = _mixer_bwd(
        x, dx1, proj, h, vc, mixed, mods, win_t, wout_b, wri, mavg2, va, vb, ln1, [g_up_t, g_down])
    g_out_b = _weight_grad(y_b.reshape(ntok, D), dm_b.reshape(ntok, D), D, "wgrad_out", out_dtype=WIRE_DTYPE)
    g_in_b, s_out = _weight_grad(dproj_b.reshape(ntok, 2 * LW + 2 * CW), u1_b.reshape(ntok, D), D, "wgrad_in",
                                 out_dtype=WIRE_DTYPE, exchange=[g_out_b])

    heads = LW // HEAD
    packed = _pack_small_grads(
        dmod_a.reshape(bl * 8, D), dmod_b.reshape(bl * 8, D), g_a, g_w31, dln1, dln2, df3,
        _diag_blocks(g_wr, heads).reshape(32, D), _diag_blocks(g_wi, heads).reshape(32, D), loss_part)
    n_rows = packed.shape[0]
    (s_in,), (packed_all,) = _exchange_and_gather([g_in_b], [packed], "exchange_w_in_grads_gather_small_grads")
    packed_all = packed_all.reshape(NDEV, n_rows, D)
    full = _sum_small_grads(packed_all, bl)
    loss = jnp.sum(full.pop("loss")) * (0.5 / D)
    dmod_all = jnp.concatenate(
        [packed_all[:, 0:8 * bl].reshape(NDEV * bl, 8, D)[:, 0:3], packed_all[:, 8 * bl:16 * bl].reshape(NDEV * bl, 8, D)[:, 0:3]],
        axis=1).reshape(NDEV * bl, 6 * D)

    grads, delta, new_m, new_v = {}, {}, {}, {}
    for k, slots in (("w_out", s_out), ("ffn_w_down", s_down)):
        g_, d_, m_, v_ = _sum_adamw(slots, weights[k][0], mom_m[k][0], mom_v[k][0], "sum_adamw_" + k)
        grads[k], delta[k], new_m[k], new_v[k] = g_[None], d_[None], m_[None], v_[None]
    g_w_ada, g_b_ada = _ada_bwd(c_all, dmod_all, lax.dynamic_slice(dmod_all, (0, me * ncol), (NDEV * bl, ncol)))
    grads["w_ada"], grads["b_ada"] = g_w_ada[None], g_b_ada
    grads["w_in"] = _sum_slots(s_in, "sum_w_in").T[None]
    grads["ffn_w_up"] = _sum_slots(s_up, "sum_w_up").T[None]
    for k, gk in full.items():
        if k in ("lru_conv_w", "conv_w", "ffn_conv_w"):
            nloc = gk.shape[1] // NDEV
            gk = lax.dynamic_slice(gk, (0, me * nloc), (gk.shape[0], nloc))
        grads[k] = gk.reshape(weights[k].shape)

    big = ("w_ada", "w_in", "w_out", "ffn_w_up", "ffn_w_down")
    for k in ("w_ada", "w_in", "ffn_w_up"):
        d_, m_, v_ = _adamw(weights[k][0], grads[k][0], mom_m[k][0], mom_v[k][0], "adamw_" + k)
        delta[k], new_m[k], new_v[k] = d_[None], m_[None], v_[None]
    small_names = [k for k in names if k not in big]
    d_, m_, v_ = _adamw_many([weights[k] for k in small_names], [grads[k] for k in small_names],
                             [mom_m[k] for k in small_names], [mom_v[k] for k in small_names], "adamw_small")
    for k, dk, mk, vk in zip(small_names, d_, m_, v_):
        delta[k], new_m[k], new_v[k] = dk, mk, vk

    return (loss, grad_x, *[grads[k] for k in names], *[delta[k] for k in names], *[new_m[k] for k in names],
            *[new_v[k] for k in names])
```

```python
import functools
import math

import jax
import jax.numpy as jnp
import numpy as np
from jax import lax
from jax.experimental import pallas as pl
from jax.experimental.pallas import tpu as pltpu

NDEV = 8
D = 1024
LW = 512
CW = 512
HEAD = 64
DFF = 2816
FF_CHUNK = 1408
LRU_K = 4
CONV_K = 31
FFN_K = 3
LRU_C = 8.0
ALPHA = (2 * 1) ** 0.25
LN_EPS = 1e-5
ADAM_LR = 0.001
ADAM_B1 = 0.9
ADAM_B2 = 0.999
ADAM_EPS = 1e-08
ADAM_WD = 0.01
ADAM_STEP = 10

MXU_DTYPE = jnp.bfloat16
WIRE_DTYPE = jnp.bfloat16
TOK_TILE = 256
WG_TOK_TILE = 1024
VMEM_LIMIT = 60 * 1024 * 1024
HALO31 = 32
HALO = 8

F32 = jnp.float32
MESH = pl.DeviceIdType.MESH


def _sigmoid(x):
    return 0.5 * jnp.tanh(0.5 * x) + 0.5


def _shifted_rows(ext_ref, rot_ref, n_rows):
    span = rot_ref.shape[1]
    for r in range(1, 8):
        rot_ref[r - 1] = ext_ref[pl.ds(r, span), :]

    def window(o, r0=0, cols=slice(None)):
        q, r = divmod(o, 8)
        if r == 0:
            return ext_ref[pl.ds(8 * q + r0, n_rows), cols]
        return rot_ref[r - 1, pl.ds(8 * q + r0, n_rows), cols]

    return window


def _store_hilo(hilo_ref, rows, c0, v):
    half = hilo_ref.shape[1] // 2
    hi = v.astype(hilo_ref.dtype)
    hilo_ref[rows, c0:c0 + v.shape[1]] = hi
    hilo_ref[rows, half + c0:half + c0 + v.shape[1]] = (v - hi.astype(F32)).astype(hilo_ref.dtype)


def _scan_bwd_blocks(a_ref, g_ref, car, n_rows):
    row = lax.broadcasted_iota(jnp.int32, (8, LANE), 0)
    for c0 in range(0, a_ref.shape[1], LANE):
        cols = slice(c0, c0 + LANE)
        a_next = jnp.broadcast_to(car[0:1, cols], (8, LANE))
        g_next = jnp.broadcast_to(car[1:2, cols], (8, LANE))
        for r0 in range(n_rows - 8, -1, -8):
            a_blk, g = a_ref[r0:r0 + 8, cols], g_ref[r0:r0 + 8, cols]
            bm = jnp.where(row < 7, pltpu.roll(a_blk, 7, 0), a_next)
            for d in (1, 2, 4):
                keep = row < 8 - d
                g = g + bm * jnp.where(keep, pltpu.roll(g, 8 - d, 0), 0.0)
                bm = bm * jnp.where(keep, pltpu.roll(bm, 8 - d, 0), 1.0)
            g = g + bm * g_next
            g_ref[r0:r0 + 8, cols] = g
            a_next = jnp.broadcast_to(a_blk[0:1, :], (8, LANE))
            g_next = jnp.broadcast_to(g[0:1, :], (8, LANE))
        car[0:1, cols] = a_next[0:1, :]
        car[1:2, cols] = g_next[0:1, :]


def _scan_fwd_blocks(a_ref, u_ref, hcar, n_rows):
    row = lax.broadcasted_iota(jnp.int32, (8, LANE), 0)
    for c0 in range(0, a_ref.shape[1], LANE):
        cols = slice(c0, c0 + LANE)
        h_prev = jnp.broadcast_to(hcar[0:1, cols], (8, LANE))
        for r0 in range(0, n_rows, 8):
            a, u = a_ref[r0:r0 + 8, cols], u_ref[r0:r0 + 8, cols]
            for d in (1, 2, 4):
                keep = row >= d
                u = a * jnp.where(keep, pltpu.roll(u, d, 0), 0.0) + u
                a = a * jnp.where(keep, pltpu.roll(a, d, 0), 1.0)
            u = u + a * h_prev
            u_ref[r0:r0 + 8, cols] = u
            h_prev = jnp.broadcast_to(u[7:8, :], (8, LANE))
        hcar[0:1, cols] = h_prev[0:1, :]


def _dot(a, b):
    return jnp.dot(a, b, preferred_element_type=F32)


def _dot_nt(a, b):
    return lax.dot_general(a, b, (((1,), (1,)), ((), ())), preferred_element_type=F32)


def _dot_tn(a, b):
    return lax.dot_general(a, b, (((0,), (0,)), ((), ())), preferred_element_type=F32)


def _colsum(v):
    return jnp.sum(v, axis=0, keepdims=True)


def _rowmean(v):
    return jnp.mean(v, axis=-1, keepdims=True)


def _head_mean(v, mavg):
    hi = v.astype(MXU_DTYPE)
    lo = (v - hi.astype(F32)).astype(MXU_DTYPE)
    return _dot(hi, mavg) + _dot(lo, mavg)


_GELU_C0 = math.sqrt(2.0 / math.pi)
_GELU_C1 = 0.044715


def _gelu_and_grad(x):
    x2 = x * x
    th = jnp.tanh(_GELU_C0 * (x + _GELU_C1 * x * x2))
    ge = 0.5 * x * (1.0 + th)
    dge = 0.5 * (1.0 + th) + 0.5 * x * (1.0 - th * th) * (_GELU_C0 * (1.0 + 3.0 * _GELU_C1 * x2))
    return ge, dge


def _softplus(x):
    return jnp.maximum(x, 0.0) + jnp.log1p(jnp.exp(-jnp.abs(x)))


def _layer_norm_stats(r):
    mu = _rowmean(r)
    dl = r - mu
    var = _rowmean(dl * dl)
    rstd = lax.rsqrt(var + LN_EPS)
    return dl * rstd, rstd


def _scan_fwd(a, u, rows):
    n = a.shape[0]
    d = 1
    while d < n:
        keep = rows >= d
        a_s = jnp.where(keep, pltpu.roll(a, d, 0), 1.0)
        u_s = jnp.where(keep, pltpu.roll(u, d, 0), 0.0)
        u = a * u_s + u
        a = a * a_s
        d *= 2
    return u, a


def _scan_bwd(bm, g, rows):
    n = bm.shape[0]
    d = 1
    while d < n:
        keep = rows < n - d
        b_s = jnp.where(keep, pltpu.roll(bm, n - d, 0), 1.0)
        g_s = jnp.where(keep, pltpu.roll(g, n - d, 0), 0.0)
        g = g + bm * g_s
        bm = bm * b_s
        d *= 2
    return g, bm


def _lru_gates(xc, wr, wi, va_ref):
    xcb = xc.astype(MXU_DTYPE)
    r = _sigmoid(_dot(xcb, wr) + va_ref[5:6, :])
    ig = _sigmoid(_dot(xcb, wi) + va_ref[6:7, :])
    sp = _softplus(-va_ref[7:8, :])
    la = (-LRU_C) * r * sp
    a = jnp.exp(la)
    a2 = a * a
    mult = jnp.sqrt(-jnp.tanh(la) * (a2 + 1.0))
    return xcb, r, ig, sp, a, a2, mult


ROW_BLK = 32
LANE = 128


def _row_blocks(n_rows, rb, body, init):
    carry = init
    for j in range(n_rows // rb):
        carry = body(j * rb, carry)
    return carry


def _fold8(v):
    r, c = v.shape
    return v if r == 8 else v.reshape(r // 8, 8, c).sum(axis=0)


def _cast_rows(src_ref, dst_ref, n_rows):
    rb = min(16, n_rows)

    def rows(r0, carry):
        dst_ref[pl.ds(r0, rb), :] = src_ref[pl.ds(r0, rb), :].astype(dst_ref.dtype)
        return carry

    _row_blocks(n_rows, rb, rows, ())


def _full(shape):
    nd = len(shape)
    return pl.BlockSpec(shape, lambda *_: (0,) * nd)


_ANY = pl.BlockSpec(memory_space=pl.ANY)


def _params(n_grid):
    return pltpu.CompilerParams(dimension_semantics=("arbitrary",) * n_grid, vmem_limit_bytes=VMEM_LIMIT)


def _my_place():
    return lax.axis_index("x"), lax.axis_index("y"), lax.axis_index("c")


def _all_gather(arrays, name):
    n_arr = len(arrays)

    def body(*refs):
        start, forward, finish = _gather_steps(refs[:n_arr], refs[n_arr:2 * n_arr], *refs[2 * n_arr:])
        start()
        forward()
        finish()

    return pl.pallas_call(
        body, name=name, out_shape=_gather_out_shapes(arrays),
        in_specs=[_ANY] * n_arr, out_specs=[_ANY] * n_arr, scratch_shapes=_comm_sems(n_arr),
    )(*arrays)


def _gather_out_shapes(arrays):
    return [jax.ShapeDtypeStruct((NDEV * a.shape[0], a.shape[1]), a.dtype) for a in arrays]


def _comm_sems(n_arr):
    return [pltpu.SemaphoreType.DMA((n_arr, 7)), pltpu.SemaphoreType.DMA((n_arr, 7)), pltpu.SemaphoreType.DMA((n_arr,))]


def _gather_steps(x_refs, out_refs, send_sems, recv_sems, local_sems):
    n_arr = len(x_refs)
    x, y, c = _my_place()
    me, sibling = (x, y, c), (x, y, 1 - c)
    chips = [(1 - x, y), (x, 1 - y), (1 - x, 1 - y)]

    def rows(k, px, py, pc):
        m = x_refs[k].shape[0]
        return out_refs[k].at[pl.ds((4 * px + 2 * py + pc) * m, m), :]

    def copy(k, s, block, to, src=None):
        return pltpu.make_async_remote_copy(
            src_ref=rows(k, *block) if src is None else src, dst_ref=rows(k, *block),
            send_sem=send_sems.at[k, s], recv_sem=recv_sems.at[k, s], device_id=to, device_id_type=MESH)

    def mine():
        return [pltpu.make_async_copy(x_refs[k], rows(k, *me), local_sems.at[k]) for k in range(n_arr)]

    def first():
        cps = []
        for k in range(n_arr):
            cps.append(copy(k, 0, me, sibling, src=x_refs[k]))
            cps += [copy(k, 1 + j, me, (*chip, c), src=x_refs[k]) for j, chip in enumerate(chips)]
        return cps

    def passed():
        return [copy(k, 4 + j, (*chip, c), sibling) for k in range(n_arr) for j, chip in enumerate(chips)]

    def start():
        for cp in mine() + first():
            cp.start()

    def forward():
        fwd = passed()
        for k in range(n_arr):
            for j, chip in enumerate(chips):
                copy(k, 1 + j, (*chip, c), me).wait_recv()
                fwd[3 * k + j].start()

    def finish():
        for k in range(n_arr):
            copy(k, 0, sibling, me).wait_recv()
            for j, chip in enumerate(chips):
                copy(k, 4 + j, (*chip, 1 - c), me).wait_recv()
        for cp in first() + passed():
            cp.wait_send()
        for cp in mine():
            cp.wait()

    return start, forward, finish


def _exchange_blocks(arrays, name):
    n_arr = len(arrays)

    def body(*refs):
        start, finish = _exchange_steps(refs[:n_arr], refs[n_arr:2 * n_arr], *refs[2 * n_arr:])
        start()
        finish()

    return pl.pallas_call(
        body, name=name, out_shape=_exchange_out_shapes(arrays),
        in_specs=[_ANY] * n_arr, out_specs=[_ANY] * n_arr, scratch_shapes=_comm_sems(n_arr),
    )(*arrays)


def _exchange_and_gather(ex_arrays, ga_arrays, name, half=None):
    n_e, n_g = len(ex_arrays), len(ga_arrays)

    def body(*refs):
        e_in, g_in, refs = refs[:n_e], refs[n_e:n_e + n_g], refs[n_e + n_g:]
        e_out, g_out, sems = refs[:n_e], refs[n_e:n_e + n_g], refs[n_e + n_g:]
        e_start, e_finish = _exchange_steps(e_in, e_out, *sems[:3], half=half)
        g_start, g_forward, g_finish = _gather_steps(g_in, g_out, *sems[3:])
        g_start()
        e_start()
        g_forward()
        g_finish()
        e_finish()

    res = pl.pallas_call(
        body, name=name, out_shape=_exchange_out_shapes(ex_arrays, half) + _gather_out_shapes(ga_arrays),
        in_specs=[_ANY] * (n_e + n_g), out_specs=[_ANY] * (n_e + n_g),
        scratch_shapes=_comm_sems(n_e) + _comm_sems(n_g),
    )(*ex_arrays, *ga_arrays)
    return res[:n_e], res[n_e:]


def _exchange_out_shapes(arrays, half=None):
    n_blocks = NDEV if half is None else NDEV // 2
    return [jax.ShapeDtypeStruct((NDEV, a.shape[0] // n_blocks, a.shape[1]), a.dtype) for a in arrays]


def _exchange_steps(g_refs, out_refs, send_sems, recv_sems, local_sems, half=None):
    n_arr = len(g_refs)
    x, y, c = _my_place()
    me = 4 * x + 2 * y + c
    receives = True if half is None else x == half

    def when(cond, fn):
        if cond is True:
            fn()
        else:
            pl.when(cond)(fn)

    def sends(rel):
        return True if half is None else (x ^ ((rel >> 2) & 1)) == half

    def copy(k, rel):
        px, py, pc = x ^ ((rel >> 2) & 1), y ^ ((rel >> 1) & 1), c ^ (rel & 1)
        m = out_refs[k].shape[1]
        blk = 4 * px + 2 * py + pc if half is None else 2 * py + pc
        return pltpu.make_async_remote_copy(
            src_ref=g_refs[k].at[pl.ds(blk * m, m), :], dst_ref=out_refs[k].at[me],
            send_sem=send_sems.at[k, rel - 1], recv_sem=recv_sems.at[k, rel - 1],
            device_id=(px, py, pc), device_id_type=MESH)

    def local(k):
        m = out_refs[k].shape[1]
        blk = me if half is None else 2 * y + c
        return pltpu.make_async_copy(g_refs[k].at[pl.ds(blk * m, m), :], out_refs[k].at[me], local_sems.at[k])

    def start():
        for k in range(n_arr):
            when(receives, local(k).start)
            for rel in range(1, NDEV):
                when(sends(rel), copy(k, rel).start)

    def finish():
        for k in range(n_arr):
            for rel in range(1, NDEV):
                when(sends(rel), copy(k, rel).wait_send)
                when(receives, copy(k, rel).wait_recv)
            when(receives, local(k).wait)

    return start, finish


def _ada_fwd(c_all, w_ada_loc, b_ada_loc):
    def body(c_ref, w_ref, b_ref, o_ref):
        cv = c_ref[...]
        ca = (cv * _sigmoid(cv)).astype(MXU_DTYPE)
        o_ref[...] = _dot(ca, w_ref[...].astype(MXU_DTYPE)) + b_ref[...]

    return pl.pallas_call(
        body, name="ada_fwd", out_shape=jax.ShapeDtypeStruct((c_all.shape[0], w_ada_loc.shape[1]), F32),
        in_specs=[_full(c_all.shape), _full(w_ada_loc.shape), _full(b_ada_loc.shape)],
        out_specs=_full((c_all.shape[0], w_ada_loc.shape[1])),
        compiler_params=pltpu.CompilerParams(vmem_limit_bytes=VMEM_LIMIT),
    )(c_all, w_ada_loc, b_ada_loc)


def _ada_bwd(c_all, dmod_all, dmod_cols):
    def body(c_ref, da_ref, d_ref, o_ref, b_ref):
        cv = c_ref[...]
        ca = (cv * _sigmoid(cv)).astype(MXU_DTYPE)
        o_ref[...] = _dot_tn(ca, d_ref[...].astype(MXU_DTYPE))
        b_ref[...] = _colsum(da_ref[...])

    return pl.pallas_call(
        body, name="ada_bwd",
        out_shape=[jax.ShapeDtypeStruct((c_all.shape[1], dmod_cols.shape[1]), F32),
                   jax.ShapeDtypeStruct((1, dmod_all.shape[1]), F32)],
        in_specs=[_full(c_all.shape), _full(dmod_all.shape), _full(dmod_cols.shape)],
        out_specs=[_full((c_all.shape[1], dmod_cols.shape[1])), _full((1, dmod_all.shape[1]))],
        compiler_params=pltpu.CompilerParams(vmem_limit_bytes=VMEM_LIMIT),
    )(c_all, dmod_all, dmod_cols)


def _mixer_fwd(x, mods, win_t, wout, wri, mavg2, va, vb, ln1, ffn_shards):
    bl, seq, _ = x.shape
    t = min(TOK_TILE, seq)
    nt = seq // t
    n_g = len(ffn_shards)
    rb = min(ROW_BLK, t)

    def body(x_ref, mod_ref, win_hbm, wout_hbm, wri_ref, mavg_ref, va_ref, vb_ref, ln1_ref, *rest):
        shard_refs, rest = rest[:n_g], rest[n_g:]
        proj_ref, h_ref, vc_ref, y_ref, mixed_ref, x1_ref, u1_ref = rest[:7]
        gathered_refs, rest = rest[7:7 + n_g], rest[7 + n_g:]
        (win_v, wout_v, xa_ext, vg_ext, rot_a, hcar, sems, rot_c, xc_scr, a_scr, st_a, st_b, xcb_scr, gates, hilo,
         g_send, g_recv, g_local) = rest
        b, i = pl.program_id(0), pl.program_id(1)
        step = b * nt + i
        g_start, g_forward, g_finish = _gather_steps(shard_refs, gathered_refs, g_send, g_recv, g_local)

        @pl.when(step == 0)
        def _():
            g_start()
            cps = [pltpu.make_async_copy(win_hbm, win_v, sems.at[0]), pltpu.make_async_copy(wout_hbm, wout_v, sems.at[1])]
            for cp in cps:
                cp.start()
            for cp in cps:
                cp.wait()

        pl.when(step == (bl * nt) // 2)(g_forward)

        @pl.when(i == 0)
        def _():
            xa_ext[0:HALO, :] = jnp.zeros((HALO, LW), F32)
            vg_ext[0:HALO31, :] = jnp.zeros((HALO31, CW), F32)
            hcar[...] = jnp.zeros_like(hcar)

        blocks = [(r0, c0) for c0 in range(0, LW, LANE) for r0 in range(0, t, rb)]

        def in_rows(r0, acc):
            rows = pl.ds(r0, 16)
            u1_ref[rows, :] = (x_ref[rows, :] * (1.0 + mod_ref[1:2, :]) + mod_ref[0:1, :]).astype(MXU_DTYPE)
            return acc

        _row_blocks(t, min(16, t), in_rows, ())
        proj_ref[...] = _dot_nt(u1_ref[...], win_v[...])

        xa_ext[HALO:HALO + t, :] = proj_ref[:, 0:LW]
        for k in range(LRU_K - 1):
            rot_c[k] = xa_ext[pl.ds(HALO - (LRU_K - 1) + k, t), :]
        for r0, c0 in blocks:
            rows, cols = pl.ds(r0, rb), slice(c0, c0 + LANE)
            xc = va_ref[4:5, cols] + va_ref[LRU_K - 1:LRU_K, cols] * xa_ext[pl.ds(HALO + r0, rb), cols]
            for k in range(LRU_K - 1):
                xc = xc + va_ref[k:k + 1, cols] * rot_c[k, rows, cols]
            xc_scr[rows, cols] = xc
            xcb_scr[rows, cols] = xc.astype(MXU_DTYPE)
        xa_ext[0:HALO, :] = xa_ext[t:t + HALO, :]
        gates[...] = _dot(xcb_scr[...], wri_ref[...])
        for r0, c0 in blocks:
            rows, cols = pl.ds(r0, rb), slice(c0, c0 + LANE)
            r = _sigmoid(gates[rows, cols] + va_ref[5:6, cols])
            ig = _sigmoid(gates[rows, LW + c0:LW + c0 + LANE] + va_ref[6:7, cols])
            la = (-LRU_C) * r * _softplus(-va_ref[7:8, cols])
            a = jnp.exp(la)
            a_scr[rows, cols] = a
            h_ref[rows, cols] = jnp.sqrt(-jnp.tanh(la) * (a * a + 1.0)) * (ig * xc_scr[rows, cols])
        _scan_fwd_blocks(a_scr, h_ref, hcar, t)

        for r0, c0 in blocks:
            rows, cols, pc = pl.ds(r0, rb), slice(c0, c0 + LANE), 2 * LW + c0
            vg_ext[pl.ds(HALO31 + r0, rb), cols] = (
                proj_ref[rows, pc:pc + LANE] * _sigmoid(proj_ref[rows, pc + CW:pc + CW + LANE]))
        vg_win = _shifted_rows(vg_ext, rot_a, rb)
        for r0, c0 in blocks:
            rows, cols = pl.ds(r0, rb), slice(c0, c0 + LANE)
            vc = vb_ref[CONV_K:CONV_K + 1, cols]
            for k in range(CONV_K):
                vc = vc + vb_ref[k:k + 1, cols] * vg_win(HALO31 - (CONV_K - 1) + k, r0, cols)
            vc_ref[rows, cols] = vc
            _store_hilo(hilo, rows, c0, vc)
        vg_ext[0:HALO31, :] = vg_ext[t:t + HALO31, :]
        mavg2 = mavg_ref[...]
        st_a[...] = _dot(hilo[...], mavg2)
        for r0, c0 in blocks:
            rows, cols = pl.ds(r0, rb), slice(c0, c0 + LANE)
            dl = vc_ref[rows, cols] - st_a[rows, cols]
            st_a[rows, cols] = dl
            _store_hilo(hilo, rows, c0, dl * dl)
        st_b[...] = _dot(hilo[...], mavg2)
        for r0, c0 in blocks:
            rows, cols = pl.ds(r0, rb), slice(c0, c0 + LANE)
            ge, _ = _gelu_and_grad(proj_ref[rows, LW + c0:LW + c0 + LANE])
            y_ref[rows, cols] = (ge * h_ref[rows, cols]).astype(MXU_DTYPE)
            yl = (st_a[rows, cols] * lax.rsqrt(st_b[rows, cols] + LN_EPS) * vb_ref[CONV_K + 1:CONV_K + 2, cols]
                  + vb_ref[CONV_K + 2:CONV_K + 3, cols])
            y_ref[rows, LW + c0:LW + c0 + LANE] = (yl * _sigmoid(yl)).astype(MXU_DTYPE)
        mixed_ref[...] = _dot(y_ref[...], wout_v[...])

        def out_rows(r0, acc):
            rows = pl.ds(r0, 8)
            xh, _ = _layer_norm_stats(ALPHA * x_ref[rows, :] + (1.0 + mod_ref[2:3, :]) * mixed_ref[rows, :])
            x1_ref[rows, :] = xh * ln1_ref[0:1, :] + ln1_ref[1:2, :]
            return acc

        _row_blocks(t, 8, out_rows, ())

        pl.when(step == bl * nt - 1)(g_finish)

    tok = lambda w: pl.BlockSpec((None, t, w), lambda b, i: (b, i, 0))
    outs = [(2 * LW + 2 * CW, F32), (LW, F32), (CW, F32), (D, MXU_DTYPE), (D, F32), (D, F32), (D, MXU_DTYPE)]
    return pl.pallas_call(
        body, name="mixer_fwd", grid=(bl, nt),
        out_shape=[jax.ShapeDtypeStruct((bl, seq, w), dt) for w, dt in outs] + _gather_out_shapes(ffn_shards),
        in_specs=[tok(D), pl.BlockSpec((None, 8, D), lambda b, i: (b, 0, 0)), _ANY, _ANY,
                  _full(wri.shape), _full(mavg2.shape), _full(va.shape), _full(vb.shape), _full(ln1.shape)]
        + [_ANY] * n_g,
        out_specs=[tok(w) for w, _ in outs] + [_ANY] * n_g,
        scratch_shapes=[pltpu.VMEM(win_t.shape, MXU_DTYPE), pltpu.VMEM(wout.shape, MXU_DTYPE),
                        pltpu.VMEM((t + HALO, LW), F32), pltpu.VMEM((t + HALO31, CW), F32),
                        pltpu.VMEM((7, t + HALO31 - 8, CW), F32), pltpu.VMEM((8, LW), F32),
                        pltpu.SemaphoreType.DMA((2,)), pltpu.VMEM((LRU_K - 1, t, LW), F32)]
        + [pltpu.VMEM((t, LW), F32)] * 4
        + [pltpu.VMEM((t, LW), MXU_DTYPE), pltpu.VMEM((t, 2 * LW), F32), pltpu.VMEM((t, 2 * CW), MXU_DTYPE)]
        + _comm_sems(n_g),
        compiler_params=_params(2),
    )(x, mods, win_t, wout, wri, mavg2, va, vb, ln1, *ffn_shards)


def _ffn_fwd(x1, mods, tgt, wup_t, wdown, f3, ln2):
    bl, seq, _ = x1.shape
    t = min(TOK_TILE, seq)
    nt = seq // t
    n_chunk = DFF // FF_CHUNK

    rb = min(ROW_BLK, t)

    def body(x1_ref, mod_ref, tgt_ref, wup_hbm, wdown_hbm, f3_ref, ln2_ref,
             hh_ref, z_ref, u2_ref, y2_ref, loss_ref,
             wup_v, wdown_v, g_ext, rot, sems):
        b, i = pl.program_id(0), pl.program_id(1)

        @pl.when((b == 0) & (i == 0))
        def _():
            cps = [pltpu.make_async_copy(wup_hbm, wup_v, sems.at[0]), pltpu.make_async_copy(wdown_hbm, wdown_v, sems.at[1])]
            for cp in cps:
                cp.start()
            for cp in cps:
                cp.wait()
            loss_ref[...] = jnp.zeros_like(loss_ref)

        @pl.when(i == 0)
        def _():
            for ch in range(n_chunk):
                g_ext[ch, 0:HALO, :] = jnp.zeros((HALO, FF_CHUNK), F32)

        def in_rows(r0, acc):
            rows = pl.ds(r0, 16)
            u2_ref[rows, :] = (x1_ref[rows, :] * (1.0 + mod_ref[4:5, :]) + mod_ref[3:4, :]).astype(MXU_DTYPE)
            return acc

        _row_blocks(t, min(16, t), in_rows, ())
        for ch in range(n_chunk):
            lo = ch * FF_CHUNK
            hh_ref[:, lo:lo + FF_CHUNK] = _dot_nt(u2_ref[...], wup_v[lo:lo + FF_CHUNK, :])
            hh_ref[:, DFF + lo:DFF + lo + FF_CHUNK] = _dot_nt(u2_ref[...], wup_v[DFF + lo:DFF + lo + FF_CHUNK, :])
            g_ext[ch, HALO:HALO + t, :] = hh_ref[:, DFF + lo:DFF + lo + FF_CHUNK]
            for k in range(FFN_K - 1):
                rot[k] = g_ext[ch, pl.ds(HALO - (FFN_K - 1) + k, t), :]
            for cb in range(FF_CHUNK // LANE):
                cols = slice(cb * LANE, (cb + 1) * LANE)
                fcols = slice(lo + cb * LANE, lo + (cb + 1) * LANE)
                for r0 in range(0, t, rb):
                    rows = pl.ds(r0, rb)
                    gc = (f3_ref[FFN_K:FFN_K + 1, fcols] + f3_ref[0:1, fcols] * rot[0, rows, cols]
                          + f3_ref[1:2, fcols] * rot[1, rows, cols]
                          + f3_ref[2:3, fcols] * g_ext[ch, pl.ds(HALO + r0, rb), cols])
                    z_ref[rows, fcols] = (gc * _sigmoid(gc) * hh_ref[rows, fcols]).astype(MXU_DTYPE)
            g_ext[ch, 0:HALO, :] = g_ext[ch, t:t + HALO, :]
        y2_ref[...] = _dot(z_ref[...], wdown_v[...])

        def out_rows(r0, acc):
            rows = pl.ds(r0, 8)
            xh, _ = _layer_norm_stats(ALPHA * x1_ref[rows, :] + (1.0 + mod_ref[5:6, :]) * y2_ref[rows, :])
            err = xh * ln2_ref[0:1, :] + ln2_ref[1:2, :] - tgt_ref[rows, :]
            return acc + err * err

        e2 = _row_blocks(t, 8, out_rows, jnp.zeros((8, D), F32))
        part = e2[:, 0:128]
        for j in range(1, D // 128):
            part = part + e2[:, 128 * j:128 * (j + 1)]
        loss_ref[...] += part

    tok = lambda w: pl.BlockSpec((None, t, w), lambda b, i: (b, i, 0))
    outs = [(2 * DFF, F32), (DFF, MXU_DTYPE), (D, MXU_DTYPE), (D, F32)]
    return pl.pallas_call(
        body, name="ffn_fwd", grid=(bl, nt),
        out_shape=[jax.ShapeDtypeStruct((bl, seq, w), dt) for w, dt in outs] + [jax.ShapeDtypeStruct((8, 128), F32)],
        in_specs=[tok(D), pl.BlockSpec((None, 8, D), lambda b, i: (b, 0, 0)), tok(D), _ANY, _ANY,
                  _full(f3.shape), _full(ln2.shape)],
        out_specs=[tok(w) for w, _ in outs] + [_full((8, 128))],
        scratch_shapes=[pltpu.VMEM(wup_t.shape, MXU_DTYPE), pltpu.VMEM(wdown.shape, MXU_DTYPE),
                        pltpu.VMEM((n_chunk, t + HALO, FF_CHUNK), F32), pltpu.VMEM((FFN_K - 1, t, FF_CHUNK), F32),
                        pltpu.SemaphoreType.DMA((2,))],
        compiler_params=_params(2),
    )(x1, mods, tgt, wup_t, wdown, f3, ln2)


def _ffn_bwd(x1, y2, tgt, hh, mods, wup_t, wdown, f3, ln2):
    bl, seq, _ = x1.shape
    t = min(TOK_TILE, seq)
    nt = seq // t
    n_chunk = DFF // FF_CHUNK
    rb = min(ROW_BLK, t)

    def body(x1_ref, y2_ref, tgt_ref, hh_ref, halo_ref, mod_ref, wup_hbm, wdown_hbm, f3_ref, ln2_ref,
             dx1_ref, dy2_ref, dhh_ref, dln2_ref, df3_ref, dmod_ref,
             wup_v, wdown_v, g_ext, dgc_ext, rot, dz_scr, wide_scr, sems):
        b, i = pl.program_id(0), pl.program_id(1)
        tt = nt - 1 - i

        @pl.when((b == 0) & (i == 0))
        def _():
            cps = [pltpu.make_async_copy(wup_hbm, wup_v, sems.at[0]), pltpu.make_async_copy(wdown_hbm, wdown_v, sems.at[1])]
            for cp in cps:
                cp.start()
            for cp in cps:
                cp.wait()
            dln2_ref[...] = jnp.zeros_like(dln2_ref)
            df3_ref[...] = jnp.zeros_like(df3_ref)

        @pl.when(i == 0)
        def _():
            dmod_ref[...] = jnp.zeros_like(dmod_ref)
            for ch in range(n_chunk):
                dgc_ext[ch, t:t + HALO, :] = jnp.zeros((HALO, FF_CHUNK), F32)

        def ln_rows(r0, acc):
            rows = pl.ds(r0, 8)
            x1v, y2v = x1_ref[rows, :], y2_ref[rows, :]
            gt = mod_ref[5:6, :]
            xh, rstd = _layer_norm_stats(ALPHA * x1v + (1.0 + gt) * y2v)
            g2 = ln2_ref[0:1, :]
            dx2 = (xh * g2 + ln2_ref[1:2, :] - tgt_ref[rows, :]) * (1.0 / D)
            dxh = dx2 * g2
            dr2 = rstd * (dxh - _rowmean(dxh) - xh * _rowmean(dxh * xh))
            wide_scr[rows, :] = (1.0 + gt) * dr2
            dx1_ref[rows, :] = ALPHA * dr2
            return acc[0] + dx2 * xh, acc[1] + dx2, acc[2] + dr2 * y2v

        zero_d = jnp.zeros((8, D), F32)
        acc = _row_blocks(t, 8, ln_rows, (zero_d, zero_d, zero_d))
        dln2_ref[0:1, :] += _colsum(acc[0])
        dln2_ref[1:2, :] += _colsum(acc[1])
        dmod_ref[2:3, :] += _colsum(acc[2])
        _cast_rows(wide_scr, dy2_ref, t)

        halo_keep = (tt > 0).astype(F32)
        zero_l = jnp.zeros((8, LANE), F32)
        for ch in range(n_chunk):
            lo = ch * FF_CHUNK
            dz_scr[...] = _dot_nt(dy2_ref[...], wdown_v[lo:lo + FF_CHUNK, :])
            g_ext[0:HALO, :] = halo_ref[:, lo:lo + FF_CHUNK] * halo_keep
            g_ext[HALO:HALO + t, :] = hh_ref[:, DFF + lo:DFF + lo + FF_CHUNK]
            for k in range(FFN_K - 1):
                rot[k] = g_ext[pl.ds(HALO - (FFN_K - 1) + k, t), :]
            for cb in range(FF_CHUNK // LANE):
                cols = slice(cb * LANE, (cb + 1) * LANE)
                fcols = slice(lo + cb * LANE, lo + (cb + 1) * LANE)

                def gate_rows(r0, acc, cols=cols, fcols=fcols, ch=ch):
                    rows = pl.ds(r0, rb)
                    taps = [rot[0, rows, cols], rot[1, rows, cols], g_ext[pl.ds(r0 + HALO, rb), cols]]
                    gc = f3_ref[FFN_K:FFN_K + 1, fcols]
                    for k in range(FFN_K):
                        gc = gc + f3_ref[k:k + 1, fcols] * taps[k]
                    s = _sigmoid(gc)
                    dz = dz_scr[rows, cols]
                    dhh_ref[rows, fcols] = (dz * (gc * s)).astype(MXU_DTYPE)
                    dgc = dz * hh_ref[rows, fcols] * (s * (1.0 + gc * (1.0 - s)))
                    dgc_ext[ch, rows, cols] = dgc
                    return tuple(acc[k] + _fold8(dgc * taps[k]) for k in range(FFN_K)) + (acc[FFN_K] + _fold8(dgc),)

                acc = _row_blocks(t, rb, gate_rows, (zero_l,) * (FFN_K + 1))
                for k in range(FFN_K + 1):
                    df3_ref[k:k + 1, fcols] += _colsum(acc[k])
            for k in range(FFN_K - 1):
                rot[k] = dgc_ext[ch, pl.ds(k + 1, t), :]
            for cb in range(FF_CHUNK // LANE):
                cols = slice(cb * LANE, (cb + 1) * LANE)
                fcols = slice(lo + cb * LANE, lo + (cb + 1) * LANE)
                gcols = slice(DFF + lo + cb * LANE, DFF + lo + (cb + 1) * LANE)

                def dg_rows(r0, acc, cols=cols, fcols=fcols, gcols=gcols, ch=ch):
                    rows = pl.ds(r0, rb)
                    dg = (f3_ref[2:3, fcols] * dgc_ext[ch, rows, cols] + f3_ref[1:2, fcols] * rot[0, rows, cols]
                          + f3_ref[0:1, fcols] * rot[1, rows, cols])
                    dhh_ref[rows, gcols] = dg.astype(MXU_DTYPE)
                    return acc

                _row_blocks(t, rb, dg_rows, ())
            dgc_ext[ch, t:t + HALO, :] = dgc_ext[ch, 0:HALO, :]

        wide_scr[...] = _dot(dhh_ref[...], wup_v[...])

        def out_rows(r0, acc):
            rows = pl.ds(r0, 8)
            du2 = wide_scr[rows, :]
            dx1_ref[rows, :] = dx1_ref[rows, :] + du2 * (1.0 + mod_ref[4:5, :])
            return acc[0] + du2 * x1_ref[rows, :], acc[1] + du2

        acc = _row_blocks(t, 8, out_rows, (zero_d, zero_d))
        dmod_ref[1:2, :] += _colsum(acc[0])
        dmod_ref[0:1, :] += _colsum(acc[1])

    rev = lambda w: pl.BlockSpec((None, t, w), lambda b, i: (b, nt - 1 - i, 0))
    halo = pl.BlockSpec((None, HALO, DFF), lambda b, i: (b, jnp.maximum((nt - 1 - i) * (t // HALO) - 1, 0), 1))
    return pl.pallas_call(
        body, name="ffn_bwd", grid=(bl, nt),
        out_shape=[jax.ShapeDtypeStruct((bl, seq, D), F32), jax.ShapeDtypeStruct((bl, seq, D), MXU_DTYPE),
                   jax.ShapeDtypeStruct((bl, seq, 2 * DFF), MXU_DTYPE), jax.ShapeDtypeStruct((8, D), F32),
                   jax.ShapeDtypeStruct((8, DFF), F32), jax.ShapeDtypeStruct((bl, 8, D), F32)],
        in_specs=[rev(D), rev(D), rev(D), rev(2 * DFF), halo, pl.BlockSpec((None, 8, D), lambda b, i: (b, 0, 0)),
                  _ANY, _ANY, _full(f3.shape), _full(ln2.shape)],
        out_specs=[rev(D), rev(D), rev(2 * DFF), _full((8, D)), _full((8, DFF)),
                   pl.BlockSpec((None, 8, D), lambda b, i: (b, 0, 0))],
        scratch_shapes=[pltpu.VMEM(wup_t.shape, MXU_DTYPE), pltpu.VMEM(wdown.shape, MXU_DTYPE),
                        pltpu.VMEM((t + HALO, FF_CHUNK), F32), pltpu.VMEM((n_chunk, t + HALO, FF_CHUNK), F32),
                        pltpu.VMEM((FFN_K - 1, t, FF_CHUNK), F32), pltpu.VMEM((t, FF_CHUNK), F32),
                        pltpu.VMEM((t, D), F32), pltpu.SemaphoreType.DMA((2,))],
        compiler_params=_params(2),
    )(x1, y2, tgt, hh, hh, mods, wup_t, wdown, f3, ln2)


def _mixer_bwd(x, dx1, proj, h, vc, mixed, mods, win_t, wout, wri, mavg2, va, vb, ln1, ffn_wgrads):
    bl, seq, _ = x.shape
    t = min(TOK_TILE, seq)
    nt = seq // t
    pw = 2 * LW + 2 * CW
    n_g = len(ffn_wgrads)
    rb = min(ROW_BLK, t)
    half_f32 = [pltpu.VMEM((t, LW), F32)] * 10

    def body(x_ref, dx1_ref, proj_ref, phalo_ref, h_ref, hhalo_ref, vc_ref, mixed_ref, mod_ref,
             win_hbm, wout_hbm, wri_ref, mavg_ref, va_ref, vb_ref, ln1_ref, *rest):
        wgrad_refs, rest = rest[:n_g], rest[n_g:]
        gx_ref, dproj_ref, dm_ref, ga_ref, gw31_ref, dwr_ref, dwi_ref, dln1_ref, dmod_ref = rest[:9]
        slot_refs, rest = rest[9:9 + n_g], rest[9 + n_g:]
        (win_v, wout_v, xa_ext, vg_scr, dvc_ext, dxc_ext, rot_b, car, sems,
         st_a, st_b, st_c, yn_scr, dyn_scr, hprev_scr, xc_scr, a_scr, mult_scr, g_scr,
         wide, gates, hilo_a, hilo_b, dgates, xcb_scr, rot_c, rot_d, hp_ext, e_send, e_recv, e_local) = rest
        b, i = pl.program_id(0), pl.program_id(1)
        tt = nt - 1 - i
        e_start, e_finish = _exchange_steps(wgrad_refs, slot_refs, e_send, e_recv, e_local)

        @pl.when((b == 0) & (i == 0))
        def _():
            e_start()
            cps = [pltpu.make_async_copy(win_hbm, win_v, sems.at[0]), pltpu.make_async_copy(wout_hbm, wout_v, sems.at[1])]
            for cp in cps:
                cp.start()
            for cp in cps:
                cp.wait()
            for ref in (ga_ref, gw31_ref, dwr_ref, dwi_ref, dln1_ref):
                ref[...] = jnp.zeros_like(ref)

        @pl.when(i == 0)
        def _():
            dmod_ref[...] = jnp.zeros_like(dmod_ref)
            dvc_ext[t:t + HALO31, :] = jnp.zeros((HALO31, CW), F32)
            dxc_ext[t:t + HALO, :] = jnp.zeros((HALO, LW), F32)
            car[...] = jnp.zeros_like(car)

        halo_keep = (tt > 0).astype(F32)
        zero_d = jnp.zeros((8, D), F32)
        zero_l = jnp.zeros((8, LANE), F32)
        blocks = [(r0, c0) for c0 in range(0, LW, LANE) for r0 in range(0, t, rb)]

        def add_row(ref, row, c0, acc):
            ref[row:row + 1, c0:c0 + LANE] += _colsum(acc)

        def ln_rows(r0, acc):
            rows = pl.ds(r0, 8)
            xv, mixed = x_ref[rows, :], mixed_ref[rows, :]
            gt = mod_ref[2:3, :]
            xh, rstd = _layer_norm_stats(ALPHA * xv + (1.0 + gt) * mixed)
            dx1 = dx1_ref[rows, :]
            dxh = dx1 * ln1_ref[0:1, :]
            dr1 = rstd * (dxh - _rowmean(dxh) - xh * _rowmean(dxh * xh))
            wide[rows, :] = (1.0 + gt) * dr1
            gx_ref[rows, :] = ALPHA * dr1
            return acc[0] + dx1 * xh, acc[1] + dx1, acc[2] + dr1 * mixed

        acc = _row_blocks(t, 8, ln_rows, (zero_d, zero_d, zero_d))
        dln1_ref[0:1, :] += _colsum(acc[0])
        dln1_ref[1:2, :] += _colsum(acc[1])
        dmod_ref[2:3, :] += _colsum(acc[2])
        _cast_rows(wide, dm_ref, t)
        wide[...] = _dot_nt(dm_ref[...], wout_v[...])

        mavg2 = mavg_ref[...]
        for r0, c0 in blocks:
            rows, cols = pl.ds(r0, rb), slice(c0, c0 + LANE)
            _store_hilo(hilo_a, rows, c0, vc_ref[rows, cols])
        st_a[...] = _dot(hilo_a[...], mavg2)
        for r0, c0 in blocks:
            rows, cols = pl.ds(r0, rb), slice(c0, c0 + LANE)
            dl = vc_ref[rows, cols] - st_a[rows, cols]
            yn_scr[rows, cols] = dl
            _store_hilo(hilo_a, rows, c0, dl * dl)
        st_b[...] = _dot(hilo_a[...], mavg2)
        for c0 in range(0, CW, LANE):
            cols, pc = slice(c0, c0 + LANE), 2 * LW + c0
            ng, nb = vb_ref[CONV_K + 1:CONV_K + 2, cols], vb_ref[CONV_K + 2:CONV_K + 3, cols]
            acc_g, acc_b = zero_l, zero_l
            for r0 in range(0, t, rb):
                rows = pl.ds(r0, rb)
                rs = lax.rsqrt(st_b[rows, cols] + LN_EPS)
                yn = yn_scr[rows, cols] * rs
                yl = yn * ng + nb
                s = _sigmoid(yl)
                dyl = wide[rows, LW + c0:LW + c0 + LANE] * (s * (1.0 + yl * (1.0 - s)))
                acc_g, acc_b = acc_g + _fold8(dyl * yn), acc_b + _fold8(dyl)
                dyn = dyl * ng
                st_b[rows, cols] = rs
                yn_scr[rows, cols] = yn
                dyn_scr[rows, cols] = dyn
                _store_hilo(hilo_a, rows, c0, dyn)
                _store_hilo(hilo_b, rows, c0, dyn * yn)
                vg_scr[rows, cols] = proj_ref[rows, pc:pc + LANE] * _sigmoid(proj_ref[rows, pc + CW:pc + CW + LANE])
            add_row(ga_ref, 8, c0, acc_g)
            add_row(ga_ref, 9, c0, acc_b)
        st_a[...] = _dot(hilo_a[...], mavg2)
        st_c[...] = _dot(hilo_b[...], mavg2)
        for c0 in range(0, CW, LANE):
            cols = slice(c0, c0 + LANE)
            acc_b = zero_l
            for r0 in range(0, t, rb):
                rows = pl.ds(r0, rb)
                dvc = st_b[rows, cols] * (dyn_scr[rows, cols] - st_a[rows, cols] - yn_scr[rows, cols] * st_c[rows, cols])
                dvc_ext[rows, cols] = dvc
                acc_b = acc_b + _fold8(dvc)
            add_row(ga_ref, 10, c0, acc_b)
        dvc_win = _shifted_rows(dvc_ext, rot_b, rb)
        for c0 in range(0, CW, LANE):
            cols, pc = slice(c0, c0 + LANE), 2 * LW + c0
            for k in range(CONV_K):
                acc_w = zero_l
                for r0 in range(0, t, rb):
                    acc_w = acc_w + _fold8(vg_scr[pl.ds(r0, rb), cols] * dvc_win(CONV_K - 1 - k, r0, cols))
                add_row(gw31_ref, k, c0, acc_w)
            for r0 in range(0, t, rb):
                rows = pl.ds(r0, rb)
                dvg = jnp.zeros((rb, LANE), F32)
                for k in range(CONV_K):
                    dvg = dvg + vb_ref[k:k + 1, cols] * dvc_win(CONV_K - 1 - k, r0, cols)
                vbr = proj_ref[rows, pc:pc + LANE]
                sgb = _sigmoid(proj_ref[rows, pc + CW:pc + CW + LANE])
                dproj_ref[rows, pc:pc + LANE] = (dvg * sgb).astype(MXU_DTYPE)
                dproj_ref[rows, pc + CW:pc + CW + LANE] = (dvg * vbr * sgb * (1.0 - sgb)).astype(MXU_DTYPE)
        dvc_ext[t:t + HALO31, :] = dvc_ext[0:HALO31, :]

        xa_ext[0:HALO, :] = phalo_ref[HALO31 - HALO:HALO31, 0:LW] * halo_keep
        xa_ext[HALO:HALO + t, :] = proj_ref[:, 0:LW]
        for k in range(LRU_K - 1):
            rot_c[k] = xa_ext[pl.ds(HALO - (LRU_K - 1) + k, t), :]
        hp_ext[0:HALO, :] = hhalo_ref[...] * halo_keep
        hp_ext[HALO:HALO + t, :] = h_ref[...]
        hprev_scr[...] = hp_ext[pl.ds(HALO - 1, t), :]

        def xa_tap(k, rows, r0, cols):
            return xa_ext[pl.ds(HALO + r0, rb), cols] if k == LRU_K - 1 else rot_c[k, rows, cols]

        for r0, c0 in blocks:
            rows, cols = pl.ds(r0, rb), slice(c0, c0 + LANE)
            xc = va_ref[4:5, cols]
            for k in range(LRU_K):
                xc = xc + va_ref[k:k + 1, cols] * xa_tap(k, rows, r0, cols)
            xc_scr[rows, cols] = xc
            xcb_scr[rows, cols] = xc.astype(MXU_DTYPE)
        gates[...] = _dot(xcb_scr[...], wri_ref[...])
        for r0, c0 in blocks:
            rows, cols = pl.ds(r0, rb), slice(c0, c0 + LANE)
            r = _sigmoid(gates[rows, cols] + va_ref[5:6, cols])
            ig = _sigmoid(gates[rows, LW + c0:LW + c0 + LANE] + va_ref[6:7, cols])
            la = (-LRU_C) * r * _softplus(-va_ref[7:8, cols])
            a = jnp.exp(la)
            gates[rows, cols] = r
            gates[rows, LW + c0:LW + c0 + LANE] = ig
            a_scr[rows, cols] = a
            mult_scr[rows, cols] = jnp.sqrt(-jnp.tanh(la) * (a * a + 1.0))
            ge, dge = _gelu_and_grad(proj_ref[rows, LW + c0:LW + c0 + LANE])
            dya = wide[rows, cols]
            dproj_ref[rows, LW + c0:LW + c0 + LANE] = (dya * h_ref[rows, cols] * dge).astype(MXU_DTYPE)
            g_scr[rows, cols] = dya * ge
        _scan_bwd_blocks(a_scr, g_scr, car, t)
        for c0 in range(0, LW, LANE):
            cols = slice(c0, c0 + LANE)
            sp = _softplus(-va_ref[7:8, cols])
            acc_l, acc_r, acc_i = zero_l, zero_l, zero_l
            for r0 in range(0, t, rb):
                rows = pl.ds(r0, rb)
                gv, xc, a, mult = g_scr[rows, cols], xc_scr[rows, cols], a_scr[rows, cols], mult_scr[rows, cols]
                r, ig = gates[rows, cols], gates[rows, LW + c0:LW + c0 + LANE]
                dla = gv * hprev_scr[rows, cols] * a - gv * (ig * xc) * (a * a / mult)
                acc_l = acc_l + _fold8(dla * ((-LRU_C) * r))
                dgr = dla * ((-LRU_C) * sp) * r * (1.0 - r)
                dgi = gv * (mult * xc) * ig * (1.0 - ig)
                acc_r, acc_i = acc_r + _fold8(dgr), acc_i + _fold8(dgi)
                dgates[rows, cols] = dgr.astype(MXU_DTYPE)
                dgates[rows, LW + c0:LW + c0 + LANE] = dgi.astype(MXU_DTYPE)
                dxc_ext[rows, cols] = gv * (mult * ig)
            add_row(ga_ref, 5, c0, acc_r)
            add_row(ga_ref, 6, c0, acc_i)
            ga_ref[7:8, cols] += _colsum(acc_l) * (-_sigmoid(-va_ref[7:8, cols]))
        dwr_ref[...] += _dot_tn(xcb_scr[...], dgates[:, 0:LW])
        dwi_ref[...] += _dot_tn(xcb_scr[...], dgates[:, LW:])
        st_a[...] = _dot_nt(dgates[...], wri_ref[...])
        for c0 in range(0, LW, LANE):
            cols = slice(c0, c0 + LANE)
            acc_b = zero_l
            for r0 in range(0, t, rb):
                rows = pl.ds(r0, rb)
                dxc = dxc_ext[rows, cols] + st_a[rows, cols]
                dxc_ext[rows, cols] = dxc
                acc_b = acc_b + _fold8(dxc)
            add_row(ga_ref, 4, c0, acc_b)
        for k in range(LRU_K - 1):
            rot_d[k] = dxc_ext[pl.ds(k + 1, t), :]
        for c0 in range(0, LW, LANE):
            cols = slice(c0, c0 + LANE)
            acc_w = [zero_l] * LRU_K
            for r0 in range(0, t, rb):
                rows = pl.ds(r0, rb)
                dxc = dxc_ext[rows, cols]
                dxa = va_ref[LRU_K - 1:LRU_K, cols] * dxc
                for k in range(LRU_K):
                    acc_w[k] = acc_w[k] + _fold8(dxc * xa_tap(k, rows, r0, cols))
                    if k < LRU_K - 1:
                        dxa = dxa + va_ref[k:k + 1, cols] * rot_d[LRU_K - 2 - k, rows, cols]
                dproj_ref[rows, cols] = dxa.astype(MXU_DTYPE)
            for k in range(LRU_K):
                add_row(ga_ref, k, c0, acc_w[k])
        dxc_ext[t:t + HALO, :] = dxc_ext[0:HALO, :]

        wide[...] = _dot(dproj_ref[...], win_v[...])

        def out_rows(r0, acc):
            rows = pl.ds(r0, 8)
            du1 = wide[rows, :]
            gx_ref[rows, :] = gx_ref[rows, :] + du1 * (1.0 + mod_ref[1:2, :])
            return acc[0] + du1 * x_ref[rows, :], acc[1] + du1

        acc = _row_blocks(t, 8, out_rows, (zero_d, zero_d))
        dmod_ref[1:2, :] += _colsum(acc[0])
        dmod_ref[0:1, :] += _colsum(acc[1])

        pl.when((b == bl - 1) & (i == nt - 1))(e_finish)

    rev = lambda w: pl.BlockSpec((None, t, w), lambda b, i: (b, nt - 1 - i, 0))

    def halo(rows_, w):
        return pl.BlockSpec((None, rows_, w), lambda b, i: (b, jnp.maximum((nt - 1 - i) * (t // rows_) - 1, 0), 0))

    return pl.pallas_call(
        body, name="mixer_bwd", grid=(bl, nt),
        out_shape=[jax.ShapeDtypeStruct((bl, seq, D), F32), jax.ShapeDtypeStruct((bl, seq, pw), MXU_DTYPE),
                   jax.ShapeDtypeStruct((bl, seq, D), MXU_DTYPE), jax.ShapeDtypeStruct((16, LW), F32),
                   jax.ShapeDtypeStruct((32, CW), F32), jax.ShapeDtypeStruct((LW, LW), F32),
                   jax.ShapeDtypeStruct((LW, LW), F32), jax.ShapeDtypeStruct((8, D), F32),
                   jax.ShapeDtypeStruct((bl, 8, D), F32)] + _exchange_out_shapes(ffn_wgrads),
        in_specs=[rev(D), rev(D), rev(pw), halo(HALO31, pw), rev(LW), halo(HALO, LW), rev(CW), rev(D),
                  pl.BlockSpec((None, 8, D), lambda b, i: (b, 0, 0)), _ANY, _ANY,
                  _full(wri.shape), _full(mavg2.shape), _full(va.shape), _full(vb.shape), _full(ln1.shape)]
        + [_ANY] * n_g,
        out_specs=[rev(D), rev(pw), rev(D), _full((16, LW)), _full((32, CW)), _full((LW, LW)), _full((LW, LW)),
                   _full((8, D)), pl.BlockSpec((None, 8, D), lambda b, i: (b, 0, 0))] + [_ANY] * n_g,
        scratch_shapes=[pltpu.VMEM(win_t.shape, MXU_DTYPE), pltpu.VMEM(wout.shape, MXU_DTYPE),
                        pltpu.VMEM((t + HALO, LW), F32), pltpu.VMEM((t, CW), F32),
                        pltpu.VMEM((t + HALO31, CW), F32), pltpu.VMEM((t + HALO, LW), F32),
                        pltpu.VMEM((7, t + HALO31 - 8, CW), F32),
                        pltpu.VMEM((8, LW), F32), pltpu.SemaphoreType.DMA((2,))] + half_f32
        + [pltpu.VMEM((t, D), F32), pltpu.VMEM((t, 2 * LW), F32), pltpu.VMEM((t, 2 * CW), MXU_DTYPE),
           pltpu.VMEM((t, 2 * CW), MXU_DTYPE), pltpu.VMEM((t, 2 * LW), MXU_DTYPE), pltpu.VMEM((t, LW), MXU_DTYPE),
           pltpu.VMEM((LRU_K - 1, t, LW), F32), pltpu.VMEM((LRU_K - 1, t, LW), F32), pltpu.VMEM((t + HALO, LW), F32)]
        + _comm_sems(n_g),
        compiler_params=_params(2),
    )(x, dx1, proj, proj, h, h, vc, mixed, mods, win_t, wout, wri, mavg2, va, vb, ln1, *ffn_wgrads)


def _weight_grad(a, bmat, tm, name, out_dtype=F32, exchange=(), exchange_half=None, col_block=None):
    ntok, m = a.shape
    n = bmat.shape[1]
    tk = min(WG_TOK_TILE, ntok)
    nk = ntok // tk
    col0 = 0 if col_block is None else col_block
    if col_block is not None:
        m = tm
    ni = m // tm
    n_e = len(exchange)

    def body(a_ref, b_ref, *rest):
        e_in, o_ref, e_out = rest[:n_e], rest[n_e], rest[n_e + 1:2 * n_e + 1]
        acc_ref = rest[2 * n_e + 1]
        i, k = pl.program_id(0), pl.program_id(1)
        if n_e:
            e_start, e_finish = _exchange_steps(e_in, e_out, *rest[2 * n_e + 2:], half=exchange_half)
            pl.when((i == 0) & (k == 0))(e_start)

        @pl.when(k == 0)
        def _():
            acc_ref[...] = jnp.zeros_like(acc_ref)

        acc_ref[...] += _dot_tn(a_ref[...], b_ref[...])

        @pl.when(k == nk - 1)
        def _():
            o_ref[...] = acc_ref[...].astype(out_dtype)

        if n_e:
            pl.when((i == ni - 1) & (k == nk - 1))(e_finish)

    res = pl.pallas_call(
        body, name=name, grid=(ni, nk),
        out_shape=[jax.ShapeDtypeStruct((m, n), out_dtype)] + _exchange_out_shapes(exchange, exchange_half),
        in_specs=[pl.BlockSpec((tk, tm), lambda i, k: (k, i + col0)), pl.BlockSpec((tk, n), lambda i, k: (k, 0))]
        + [_ANY] * n_e,
        out_specs=[pl.BlockSpec((tm, n), lambda i, k: (i, 0))] + [_ANY] * n_e,
        scratch_shapes=[pltpu.VMEM((tm, n), F32)] + (_comm_sems(n_e) if n_e else []),
        compiler_params=_params(2),
    )(a, bmat, *exchange)
    return res if n_e else res[0]


def _sum_slots(slots, name, slots_x1=None):
    _, r, cdim = slots.shape
    tr = 32 if r % 32 == 0 else r
    ins = [slots] if slots_x1 is None else [slots, slots_x1]

    def body(*refs):
        o_ref = refs[-1]

        def total(s_ref):
            acc = s_ref[0].astype(F32)
            for j in range(1, NDEV):
                acc = acc + s_ref[j].astype(F32)
            o_ref[...] = acc

        if slots_x1 is None:
            total(refs[0])
        else:
            x = lax.axis_index("x")
            pl.when(x == 0)(functools.partial(total, refs[0]))
            pl.when(x == 1)(functools.partial(total, refs[1]))

    return pl.pallas_call(
        body, name=name, grid=(r // tr,), out_shape=jax.ShapeDtypeStruct((r, cdim), F32),
        in_specs=[pl.BlockSpec((NDEV, tr, cdim), lambda i: (0, i, 0))] * len(ins),
        out_specs=pl.BlockSpec((tr, cdim), lambda i: (i, 0)),
        compiler_params=_params(1),
    )(*ins)


def _small_grad_layout(bl):
    r1 = 6 * bl
    return {"dmod": 0, "wide": r1, "ln": r1 + 32, "df3": r1 + 36, "g_wr": r1 + 48, "g_wi": r1 + 80, "rows": r1 + 112}


def _pack_small_grads(dmod_a, dmod_b, g_a, g_w31, dln1, dln2, df3, g_wr, g_wi, loss_part):
    bl = dmod_a.shape[0] // 8
    lay = _small_grad_layout(bl)
    ins = [dmod_a, dmod_b, g_a, g_w31, dln1, dln2, df3, g_wr, g_wi, loss_part]

    def body(ma_ref, mb_ref, ga_ref, gw_ref, l1_ref, l2_ref, f3_ref, wr_ref, wi_ref, ls_ref, o_ref):
        o_ref[...] = jnp.zeros_like(o_ref)
        for b in range(bl):
            o_ref[6 * b:6 * b + 3, :] = ma_ref[8 * b:8 * b + 3, :]
            o_ref[6 * b + 3:6 * b + 6, :] = mb_ref[8 * b:8 * b + 3, :]
        wide = lay["wide"]
        o_ref[wide:wide + 32, 0:CW] = gw_ref[...]
        o_ref[wide:wide + 16, CW:CW + LW] = ga_ref[...]
        o_ref[wide + 16:wide + 24, CW:CW + LANE] = ls_ref[...]
        o_ref[lay["ln"]:lay["ln"] + 2, :] = l1_ref[0:2, :]
        o_ref[lay["ln"] + 2:lay["ln"] + 4, :] = l2_ref[0:2, :]
        for j, c0 in enumerate(range(0, DFF, D)):
            w = min(D, DFF - c0)
            o_ref[lay["df3"] + 4 * j:lay["df3"] + 4 * j + 4, 0:w] = f3_ref[0:4, c0:c0 + w]
        o_ref[lay["g_wr"]:lay["g_wr"] + 32, :] = wr_ref[...]
        o_ref[lay["g_wi"]:lay["g_wi"] + 32, :] = wi_ref[...]

    return pl.pallas_call(
        body, name="pack_small_grads", out_shape=jax.ShapeDtypeStruct((lay["rows"], D), F32),
        in_specs=[_full(a.shape) for a in ins], out_specs=_full((lay["rows"], D)),
        compiler_params=pltpu.CompilerParams(vmem_limit_bytes=VMEM_LIMIT),
    )(*ins)


def _sum_small_grads(packed_all, bl):
    lay = _small_grad_layout(bl)
    shapes = {"lru_conv_w": (LRU_K, LW), "lru_conv_b": (1, LW), "lru_b_r": (1, LW), "lru_b_i": (1, LW),
              "lru_lambda": (1, LW), "conv_norm_g": (1, CW), "conv_norm_b": (1, CW), "conv_b": (1, CW),
              "conv_w": (CONV_K, CW), "ln1_g": (1, D), "ln1_b": (1, D), "ln2_g": (1, D), "ln2_b": (1, D),
              "ffn_conv_w": (FFN_K, DFF), "ffn_conv_b": (1, DFF), "lru_w_r": (32, D), "lru_w_i": (32, D),
              "loss": (8, LANE)}
    names = list(shapes)

    def body(p_ref, *outs):
        o = dict(zip(names, outs))

        def part(r0, nr, c0=0, nc=D):
            acc = p_ref[0, r0:r0 + nr, c0:c0 + nc]
            for j in range(1, NDEV):
                acc = acc + p_ref[j, r0:r0 + nr, c0:c0 + nc]
            return acc

        wide = lay["wide"]
        ga = part(wide, 16, CW, LW)
        o["lru_conv_w"][...] = ga[0:LRU_K]
        for row, k in ((4, "lru_conv_b"), (5, "lru_b_r"), (6, "lru_b_i"), (7, "lru_lambda"), (8, "conv_norm_g"),
                       (9, "conv_norm_b"), (10, "conv_b")):
            o[k][...] = ga[row:row + 1]
        o["conv_w"][...] = part(wide, 32, 0, CW)[0:CONV_K]
        o["loss"][...] = part(wide + 16, 8, CW, LANE)
        ln = part(lay["ln"], 4)
        o["ln1_g"][...], o["ln1_b"][...], o["ln2_g"][...], o["ln2_b"][...] = ln[0:1], ln[1:2], ln[2:3], ln[3:4]
        for j, c0 in enumerate(range(0, DFF, D)):
            w = min(D, DFF - c0)
            f3 = part(lay["df3"] + 4 * j, 4, 0, w)
            o["ffn_conv_w"][:, c0:c0 + w] = f3[0:FFN_K]
            o["ffn_conv_b"][:, c0:c0 + w] = f3[FFN_K:FFN_K + 1]
        o["lru_w_r"][...] = part(lay["g_wr"], 32)
        o["lru_w_i"][...] = part(lay["g_wi"], 32)

    res = pl.pallas_call(
        body, name="sum_small_grads", out_shape=[jax.ShapeDtypeStruct(shapes[k], F32) for k in names],
        in_specs=[_full(packed_all.shape)], out_specs=[_full(shapes[k]) for k in names],
        compiler_params=pltpu.CompilerParams(vmem_limit_bytes=VMEM_LIMIT),
    )(packed_all)
    return dict(zip(names, res))


def _adamw_update(w_ref, g_ref, m_ref, v_ref, d_ref, nm_ref, nv_ref):
    gv = g_ref[...]
    nm = ADAM_B1 * m_ref[...] + (1.0 - ADAM_B1) * gv
    nv = ADAM_B2 * v_ref[...] + (1.0 - ADAM_B2) * (gv * gv)
    m_hat = nm / (1.0 - ADAM_B1 ** ADAM_STEP)
    v_hat = nv / (1.0 - ADAM_B2 ** ADAM_STEP)
    d_ref[...] = -ADAM_LR * (m_hat / (jnp.sqrt(v_hat) + ADAM_EPS) + ADAM_WD * w_ref[...])
    nm_ref[...] = nm
    nv_ref[...] = nv


def _sum_adamw(slots, w, m, v, name):
    _, r, cdim = slots.shape
    tr = 32 if r % 32 == 0 else r

    def body(s_ref, w_ref, m_ref, v_ref, g_ref, d_ref, nm_ref, nv_ref):
        acc = s_ref[0].astype(F32)
        for j in range(1, NDEV):
            acc = acc + s_ref[j].astype(F32)
        g_ref[...] = acc
        _adamw_update(w_ref, g_ref, m_ref, v_ref, d_ref, nm_ref, nv_ref)

    spec = pl.BlockSpec((tr, cdim), lambda i: (i, 0))
    return pl.pallas_call(
        body, name=name, grid=(r // tr,), out_shape=[jax.ShapeDtypeStruct((r, cdim), F32)] * 4,
        in_specs=[pl.BlockSpec((NDEV, tr, cdim), lambda i: (0, i, 0))] + [spec] * 3, out_specs=[spec] * 4,
        compiler_params=_params(1),
    )(slots, w, m, v)


def _adamw_many(ws, gs, ms, vs, name):
    n = len(ws)

    def body(*refs):
        ins, outs = refs[:4 * n], refs[4 * n:]
        for k in range(n):
            _adamw_update(ins[k], ins[n + k], ins[2 * n + k], ins[3 * n + k], outs[k], outs[n + k], outs[2 * n + k])

    specs = [_full(w.shape) for w in ws]
    res = pl.pallas_call(
        body, name=name, out_shape=[jax.ShapeDtypeStruct(w.shape, F32) for w in ws] * 3,
        in_specs=specs * 4, out_specs=specs * 3,
        compiler_params=pltpu.CompilerParams(vmem_limit_bytes=VMEM_LIMIT),
    )(*ws, *gs, *ms, *vs)
    return res[:n], res[n:2 * n], res[2 * n:]


def _adamw(w, g, m, v, name):
    r, cdim = w.shape
    tr = 128 if r % 128 == 0 else r
    body = functools.partial(_adamw_update)
    spec = pl.BlockSpec((tr, cdim), lambda i: (i, 0))
    return pl.pallas_call(
        body, name=name, grid=(r // tr,), out_shape=[jax.ShapeDtypeStruct((r, cdim), F32)] * 3,
        in_specs=[spec] * 4, out_specs=[spec] * 3, compiler_params=_params(1),
    )(w, g, m, v)


def _pack(arrs, width=D, row_mult=8):
    parts = []
    for a in arrs:
        flat = a.reshape(-1)
        pad = (-flat.shape[0]) % width
        parts.append(jnp.pad(flat, (0, pad)))
    flat = jnp.concatenate(parts)
    pad = (-flat.shape[0]) % (width * row_mult)
    return jnp.pad(flat, (0, pad)).reshape(-1, width)


def _unpack(buf, shapes, width=D):
    out, row = [], 0
    for shp in shapes:
        size = math.prod(shp)
        nrow = -(-size // width)
        out.append(buf[row:row + nrow].reshape(-1)[:size].reshape(shp))
        row += nrow
    return out


def _unpack_gathered(buf, shapes, width=D):
    out, row = [], 0
    for shp in shapes:
        size = math.prod(shp)
        nrow = -(-size // width)
        out.append(buf[:, row:row + nrow].reshape(NDEV, -1)[:, :size].reshape((NDEV,) + tuple(shp)))
        row += nrow
    return out


def _block_diag(w):
    hn, dh, _ = w.shape
    eye = jnp.eye(hn, dtype=w.dtype)
    return (w[:, :, None, :] * eye[:, None, :, None]).reshape(hn * dh, hn * dh)


def _diag_blocks(wfull, hn):
    dh = wfull.shape[0] // hn
    keep = np.eye(hn, dtype=bool)[:, None, :, None]
    return jnp.where(keep, wfull.reshape(hn, dh, hn, dh), 0.0).sum(axis=2)


def _ungather_cols(g, k):
    n = g.shape[1]
    return g.reshape(NDEV, k, n).transpose(1, 0, 2).reshape(k, NDEV * n)


def _pad_rows(a, rows):
    return jnp.pad(a, ((0, rows - a.shape[0]), (0, 0)))


def kernel(x, c, w_ada, b_ada, w_in, lru_conv_w, lru_conv_b, lru_w_r, lru_b_r, lru_w_i, lru_b_i, lru_lambda, conv_w, conv_b, conv_norm_g, conv_norm_b, w_out, ln1_g, ln1_b, ffn_w_up, ffn_conv_w, ffn_conv_b, ffn_w_down, ln2_g, ln2_b, loss_target, m_w_ada, m_b_ada, m_w_in, m_lru_conv_w, m_lru_conv_b, m_lru_w_r, m_lru_b_r, m_lru_w_i, m_lru_b_i, m_lru_lambda, m_conv_w, m_conv_b, m_conv_norm_g, m_conv_norm_b, m_w_out, m_ln1_g, m_ln1_b, m_ffn_w_up, m_ffn_conv_w, m_ffn_conv_b, m_ffn_w_down, m_ln2_g, m_ln2_b, v_w_ada, v_b_ada, v_w_in, v_lru_conv_w, v_lru_conv_b, v_lru_w_r, v_lru_b_r, v_lru_w_i, v_lru_b_i, v_lru_lambda, v_conv_w, v_conv_b, v_conv_norm_g, v_conv_norm_b, v_w_out, v_ln1_g, v_ln1_b, v_ffn_w_up, v_ffn_conv_w, v_ffn_conv_b, v_ffn_w_down, v_ln2_g, v_ln2_b):
    weights = dict(w_ada=w_ada, b_ada=b_ada, w_in=w_in, lru_conv_w=lru_conv_w, lru_conv_b=lru_conv_b, lru_w_r=lru_w_r,
                   lru_b_r=lru_b_r, lru_w_i=lru_w_i, lru_b_i=lru_b_i, lru_lambda=lru_lambda, conv_w=conv_w, conv_b=conv_b,
                   conv_norm_g=conv_norm_g, conv_norm_b=conv_norm_b, w_out=w_out, ln1_g=ln1_g, ln1_b=ln1_b,
                   ffn_w_up=ffn_w_up, ffn_conv_w=ffn_conv_w, ffn_conv_b=ffn_conv_b, ffn_w_down=ffn_w_down, ln2_g=ln2_g,
                   ln2_b=ln2_b)
    mom_m = dict(w_ada=m_w_ada, b_ada=m_b_ada, w_in=m_w_in, lru_conv_w=m_lru_conv_w, lru_conv_b=m_lru_conv_b,
                 lru_w_r=m_lru_w_r, lru_b_r=m_lru_b_r, lru_w_i=m_lru_w_i, lru_b_i=m_lru_b_i, lru_lambda=m_lru_lambda,
                 conv_w=m_conv_w, conv_b=m_conv_b, conv_norm_g=m_conv_norm_g, conv_norm_b=m_conv_norm_b, w_out=m_w_out,
                 ln1_g=m_ln1_g, ln1_b=m_ln1_b, ffn_w_up=m_ffn_w_up, ffn_conv_w=m_ffn_conv_w, ffn_conv_b=m_ffn_conv_b,
                 ffn_w_down=m_ffn_w_down, ln2_g=m_ln2_g, ln2_b=m_ln2_b)
    mom_v = dict(w_ada=v_w_ada, b_ada=v_b_ada, w_in=v_w_in, lru_conv_w=v_lru_conv_w, lru_conv_b=v_lru_conv_b,
                 lru_w_r=v_lru_w_r, lru_b_r=v_lru_b_r, lru_w_i=v_lru_w_i, lru_b_i=v_lru_b_i, lru_lambda=v_lru_lambda,
                 conv_w=v_conv_w, conv_b=v_conv_b, conv_norm_g=v_conv_norm_g, conv_norm_b=v_conv_norm_b, w_out=v_w_out,
                 ln1_g=v_ln1_g, ln1_b=v_ln1_b, ffn_w_up=v_ffn_w_up, ffn_conv_w=v_ffn_conv_w, ffn_conv_b=v_ffn_conv_b,
                 ffn_w_down=v_ffn_w_down, ln2_g=v_ln2_g, ln2_b=v_ln2_b)
    names = list(weights)
    bl, seq, _ = x.shape
    ntok = bl * seq
    me = 4 * lax.axis_index("x") + 2 * lax.axis_index("y") + lax.axis_index("c")

    small_shapes = [(bl, D), (LRU_K, LW // NDEV), (CONV_K, CW // NDEV), (FFN_K, DFF // NDEV)]
    small = _pack([c, lru_conv_w[0], conv_w[0], ffn_conv_w[0]], width=128)
    n_small = small.shape[0]
    small_all, win_t, wout_b = _all_gather(
        [small, w_in[0].T.astype(MXU_DTYPE), w_out[0].astype(MXU_DTYPE)], "gather_small_and_mixer_weights")
    small_all = small_all.reshape(NDEV, n_small, 128)
    c_parts, k4_parts, k31_parts, k3_parts = _unpack_gathered(small_all, small_shapes, width=128)
    c_all = c_parts.reshape(NDEV * bl, D)
    lru_conv_w_f = k4_parts.transpose(1, 0, 2).reshape(LRU_K, LW)
    conv_w_f = k31_parts.transpose(1, 0, 2).reshape(CONV_K, CW)
    ffn_conv_w_f = k3_parts.transpose(1, 0, 2).reshape(FFN_K, DFF)

    ncol = w_ada.shape[2]
    b_ada_loc = lax.dynamic_slice(b_ada, (0, me * ncol), (1, ncol))
    mod_cols = _ada_fwd(c_all, w_ada[0], b_ada_loc)
    (mod_all,) = _all_gather([mod_cols], "gather_mod")
    mod_all = mod_all.reshape(NDEV, NDEV * bl, ncol)
    mod_mine = lax.dynamic_slice(mod_all, (0, me * bl, 0), (NDEV, bl, ncol))
    mods = mod_mine.transpose(1, 0, 2).reshape(bl, 6, D)
    mods = jnp.pad(mods, ((0, 0), (0, 2), (0, 0)))

    ffn_shards = [ffn_w_up[0].T.astype(MXU_DTYPE), ffn_w_down[0].astype(MXU_DTYPE)]

    wr_bd = _block_diag(lru_w_r[0]).astype(MXU_DTYPE)
    wi_bd = _block_diag(lru_w_i[0]).astype(MXU_DTYPE)
    mavg_np = np.kron(np.eye(CW // HEAD, dtype=np.float32), np.full((HEAD, HEAD), 1.0 / HEAD, np.float32))
    mavg = jnp.asarray(mavg_np, MXU_DTYPE)
    mavg2 = jnp.asarray(np.concatenate([mavg_np, mavg_np], axis=0), MXU_DTYPE)
    wri = jnp.concatenate([wr_bd, wi_bd], axis=1)
    va = jnp.concatenate([lru_conv_w_f, lru_conv_b, lru_b_r, lru_b_i, lru_lambda], axis=0)
    vb = _pad_rows(jnp.concatenate([conv_w_f, conv_b, conv_norm_g, conv_norm_b], axis=0), 40)
    ln1 = _pad_rows(jnp.concatenate([ln1_g, ln1_b], axis=0), 8)
    ln2 = _pad_rows(jnp.concatenate([ln2_g, ln2_b], axis=0), 8)
    f3 = _pad_rows(jnp.concatenate([ffn_conv_w_f, ffn_conv_b], axis=0), 8)

    proj, h, vc, y_b, mixed, x1, u1_b, wup_t, wdown_b = _mixer_fwd(
        x, mods, win_t, wout_b, wri, mavg2, va, vb, ln1, ffn_shards)
    hh, z_b, u2_b, y2, loss_part = _ffn_fwd(x1, mods, loss_target, wup_t, wdown_b, f3, ln2)

    dx1, dy2_b, dhh_b, dln2, df3, dmod_b = _ffn_bwd(x1, y2, loss_target, hh, mods, wup_t, wdown_b, f3, ln2)
    g_down = _weight_grad(z_b.reshape(ntok, DFF), dy2_b.reshape(ntok, D), FF_CHUNK, "wgrad_down")
    g_up_t = _weight_grad(dhh_b.reshape(ntok, 2 * DFF), u2_b.reshape(ntok, D), FF_CHUNK, "wgrad_up")
    grad_x, dproj_b, dm_b, g_a, g_w31, g_wr, g_wi, dln1, dmod_a, s_up, s_down = _mixer_bwd(
        x, dx1, proj, h, vc, mixed, mods, win_t, wout_b, wri, mavg2, va, vb, ln1, [g_up_t, g_down])
    dproj_2d, u1_2d = dproj_b.reshape(ntok, 2 * LW + 2 * CW), u1_b.reshape(ntok, D)
    g_out_b = _weight_grad(y_b.reshape(ntok, D), dm_b.reshape(ntok, D), D, "wgrad_out", out_dtype=WIRE_DTYPE)
    g_in_x0, s_out = _weight_grad(dproj_2d, u1_2d, D, "wgrad_in_x0", out_dtype=WIRE_DTYPE, exchange=[g_out_b],
                                  col_block=0)
    g_in_x1, s_in_x0 = _weight_grad(dproj_2d, u1_2d, D, "wgrad_in_x1", out_dtype=WIRE_DTYPE, exchange=[g_in_x0],
                                    exchange_half=0, col_block=1)

    heads = LW // HEAD
    packed = _pack_small_grads(
        dmod_a.reshape(bl * 8, D), dmod_b.reshape(bl * 8, D), g_a, g_w31, dln1, dln2, df3,
        _diag_blocks(g_wr, heads).reshape(32, D), _diag_blocks(g_wi, heads).reshape(32, D), loss_part)
    n_rows = packed.shape[0]
    (s_in_x1,), (packed_all,) = _exchange_and_gather(
        [g_in_x1], [packed], "exchange_w_in_grads_gather_small_grads", half=1)
    packed_all = packed_all.reshape(NDEV, n_rows, D)
    full = _sum_small_grads(packed_all, bl)
    loss = jnp.sum(full.pop("loss")) * (0.5 / D)
    dmod_all = packed_all[:, 0:6 * bl].reshape(NDEV * bl, 6 * D)

    grads, delta, new_m, new_v = {}, {}, {}, {}
    for k, slots in (("w_out", s_out), ("ffn_w_down", s_down)):
        g_, d_, m_, v_ = _sum_adamw(slots, weights[k][0], mom_m[k][0], mom_v[k][0], "sum_adamw_" + k)
        grads[k], delta[k], new_m[k], new_v[k] = g_[None], d_[None], m_[None], v_[None]
    g_w_ada, g_b_ada = _ada_bwd(c_all, dmod_all, lax.dynamic_slice(dmod_all, (0, me * ncol), (NDEV * bl, ncol)))
    grads["w_ada"], grads["b_ada"] = g_w_ada[None], g_b_ada
    grads["w_in"] = _sum_slots(s_in_x0, "sum_w_in", slots_x1=s_in_x1).T[None]
    grads["ffn_w_up"] = _sum_slots(s_up, "sum_w_up").T[None]
    for k, gk in full.items():
        if k in ("lru_conv_w", "conv_w", "ffn_conv_w"):
            nloc = gk.shape[1] // NDEV
            gk = lax.dynamic_slice(gk, (0, me * nloc), (gk.shape[0], nloc))
        grads[k] = gk.reshape(weights[k].shape)

    big = ("w_ada", "w_in", "w_out", "ffn_w_up", "ffn_w_down")
    for k in ("w_ada", "w_in", "ffn_w_up"):
        d_, m_, v_ = _adamw(weights[k][0], grads[k][0], mom_m[k][0], mom_v[k][0], "adamw_" + k)
        delta[k], new_m[k], new_v[k] = d_[None], m_[None], v_[None]
    small_names = [k for k in names if k not in big]
    d_, m_, v_ = _adamw_many([weights[k] for k in small_names], [grads[k] for k in small_names],
                             [mom_m[k] for k in small_names], [mom_v[k] for k in small_names], "adamw_small")
    for k, dk, mk, vk in zip(small_names, d_, m_, v_):
        delta[k], new_m[k], new_v[k] = dk, mk, vk

    return (loss, grad_x, *[grads[k] for k in names], *[delta[k] for k in names], *[new_m[k] for k in names],
            *[new_v[k] for k in names])
```

```python
import functools
import math

import jax
import jax.numpy as jnp
import numpy as np
from jax import lax
from jax.experimental import pallas as pl
from jax.experimental.pallas import tpu as pltpu

NDEV = 8
D = 1024
LW = 512
CW = 512
HEAD = 64
DFF = 2816
FF_CHUNK = 1408
LRU_K = 4
CONV_K = 31
FFN_K = 3
LRU_C = 8.0
ALPHA = (2 * 1) ** 0.25
LN_EPS = 1e-5
ADAM_LR = 0.001
ADAM_B1 = 0.9
ADAM_B2 = 0.999
ADAM_EPS = 1e-08
ADAM_WD = 0.01
ADAM_STEP = 10

MXU_DTYPE = jnp.bfloat16
WIRE_DTYPE = jnp.bfloat16
TOK_TILE = 256
WG_TOK_TILE = 1024
VMEM_LIMIT = 60 * 1024 * 1024
HALO31 = 32
HALO = 8

F32 = jnp.float32
MESH = pl.DeviceIdType.MESH


def _sigmoid(x):
    return 0.5 * jnp.tanh(0.5 * x) + 0.5


def _shifted_rows(ext_ref, rot_ref, n_rows):
    span = rot_ref.shape[1]
    for r in range(1, 8):
        rot_ref[r - 1] = ext_ref[pl.ds(r, span), :]

    def window(o, r0=0, cols=slice(None)):
        q, r = divmod(o, 8)
        if r == 0:
            return ext_ref[pl.ds(8 * q + r0, n_rows), cols]
        return rot_ref[r - 1, pl.ds(8 * q + r0, n_rows), cols]

    return window


def _store_hilo(hilo_ref, rows, c0, v):
    half = hilo_ref.shape[1] // 2
    hi = v.astype(hilo_ref.dtype)
    hilo_ref[rows, c0:c0 + v.shape[1]] = hi
    hilo_ref[rows, half + c0:half + c0 + v.shape[1]] = (v - hi.astype(F32)).astype(hilo_ref.dtype)


def _scan_bwd_blocks(a_ref, g_ref, car, n_rows):
    row = lax.broadcasted_iota(jnp.int32, (8, LANE), 0)
    for c0 in range(0, a_ref.shape[1], LANE):
        cols = slice(c0, c0 + LANE)
        a_next = jnp.broadcast_to(car[0:1, cols], (8, LANE))
        g_next = jnp.broadcast_to(car[1:2, cols], (8, LANE))
        for r0 in range(n_rows - 8, -1, -8):
            a_blk, g = a_ref[r0:r0 + 8, cols], g_ref[r0:r0 + 8, cols]
            bm = jnp.where(row < 7, pltpu.roll(a_blk, 7, 0), a_next)
            for d in (1, 2, 4):
                keep = row < 8 - d
                g = g + bm * jnp.where(keep, pltpu.roll(g, 8 - d, 0), 0.0)
                bm = bm * jnp.where(keep, pltpu.roll(bm, 8 - d, 0), 1.0)
            g = g + bm * g_next
            g_ref[r0:r0 + 8, cols] = g
            a_next = jnp.broadcast_to(a_blk[0:1, :], (8, LANE))
            g_next = jnp.broadcast_to(g[0:1, :], (8, LANE))
        car[0:1, cols] = a_next[0:1, :]
        car[1:2, cols] = g_next[0:1, :]


def _scan_fwd_blocks(a_ref, u_ref, hcar, n_rows):
    row = lax.broadcasted_iota(jnp.int32, (8, LANE), 0)
    for c0 in range(0, a_ref.shape[1], LANE):
        cols = slice(c0, c0 + LANE)
        h_prev = jnp.broadcast_to(hcar[0:1, cols], (8, LANE))
        for r0 in range(0, n_rows, 8):
            a, u = a_ref[r0:r0 + 8, cols], u_ref[r0:r0 + 8, cols]
            for d in (1, 2, 4):
                keep = row >= d
                u = a * jnp.where(keep, pltpu.roll(u, d, 0), 0.0) + u
                a = a * jnp.where(keep, pltpu.roll(a, d, 0), 1.0)
            u = u + a * h_prev
            u_ref[r0:r0 + 8, cols] = u
            h_prev = jnp.broadcast_to(u[7:8, :], (8, LANE))
        hcar[0:1, cols] = h_prev[0:1, :]


def _dot(a, b):
    return jnp.dot(a, b, preferred_element_type=F32)


def _dot_nt(a, b):
    return lax.dot_general(a, b, (((1,), (1,)), ((), ())), preferred_element_type=F32)


def _dot_tn(a, b):
    return lax.dot_general(a, b, (((0,), (0,)), ((), ())), preferred_element_type=F32)


def _colsum(v):
    return jnp.sum(v, axis=0, keepdims=True)


def _rowmean(v):
    return jnp.mean(v, axis=-1, keepdims=True)


def _head_mean(v, mavg):
    hi = v.astype(MXU_DTYPE)
    lo = (v - hi.astype(F32)).astype(MXU_DTYPE)
    return _dot(hi, mavg) + _dot(lo, mavg)


_GELU_C0 = math.sqrt(2.0 / math.pi)
_GELU_C1 = 0.044715


def _gelu_and_grad(x):
    x2 = x * x
    th = jnp.tanh(_GELU_C0 * (x + _GELU_C1 * x * x2))
    ge = 0.5 * x * (1.0 + th)
    dge = 0.5 * (1.0 + th) + 0.5 * x * (1.0 - th * th) * (_GELU_C0 * (1.0 + 3.0 * _GELU_C1 * x2))
    return ge, dge


def _softplus(x):
    return jnp.maximum(x, 0.0) + jnp.log1p(jnp.exp(-jnp.abs(x)))


def _layer_norm_stats(r):
    mu = _rowmean(r)
    dl = r - mu
    var = _rowmean(dl * dl)
    rstd = lax.rsqrt(var + LN_EPS)
    return dl * rstd, rstd


def _scan_fwd(a, u, rows):
    n = a.shape[0]
    d = 1
    while d < n:
        keep = rows >= d
        a_s = jnp.where(keep, pltpu.roll(a, d, 0), 1.0)
        u_s = jnp.where(keep, pltpu.roll(u, d, 0), 0.0)
        u = a * u_s + u
        a = a * a_s
        d *= 2
    return u, a


def _scan_bwd(bm, g, rows):
    n = bm.shape[0]
    d = 1
    while d < n:
        keep = rows < n - d
        b_s = jnp.where(keep, pltpu.roll(bm, n - d, 0), 1.0)
        g_s = jnp.where(keep, pltpu.roll(g, n - d, 0), 0.0)
        g = g + bm * g_s
        bm = bm * b_s
        d *= 2
    return g, bm


def _lru_gates(xc, wr, wi, va_ref):
    xcb = xc.astype(MXU_DTYPE)
    r = _sigmoid(_dot(xcb, wr) + va_ref[5:6, :])
    ig = _sigmoid(_dot(xcb, wi) + va_ref[6:7, :])
    sp = _softplus(-va_ref[7:8, :])
    la = (-LRU_C) * r * sp
    a = jnp.exp(la)
    a2 = a * a
    mult = jnp.sqrt(-jnp.tanh(la) * (a2 + 1.0))
    return xcb, r, ig, sp, a, a2, mult


ROW_BLK = 32
LANE = 128


def _row_blocks(n_rows, rb, body, init):
    carry = init
    for j in range(n_rows // rb):
        carry = body(j * rb, carry)
    return carry


def _fold8(v):
    r, c = v.shape
    return v if r == 8 else v.reshape(r // 8, 8, c).sum(axis=0)


def _cast_rows(src_ref, dst_ref, n_rows):
    rb = min(16, n_rows)

    def rows(r0, carry):
        dst_ref[pl.ds(r0, rb), :] = src_ref[pl.ds(r0, rb), :].astype(dst_ref.dtype)
        return carry

    _row_blocks(n_rows, rb, rows, ())


def _full(shape):
    nd = len(shape)
    return pl.BlockSpec(shape, lambda *_: (0,) * nd)


_ANY = pl.BlockSpec(memory_space=pl.ANY)


def _params(n_grid):
    return pltpu.CompilerParams(dimension_semantics=("arbitrary",) * n_grid, vmem_limit_bytes=VMEM_LIMIT)


def _my_place():
    return lax.axis_index("x"), lax.axis_index("y"), lax.axis_index("c")


def _all_gather(arrays, name):
    n_arr = len(arrays)

    def body(*refs):
        start, forward, finish = _gather_steps(refs[:n_arr], refs[n_arr:2 * n_arr], *refs[2 * n_arr:])
        start()
        forward()
        finish()

    return pl.pallas_call(
        body, name=name, out_shape=_gather_out_shapes(arrays),
        in_specs=[_ANY] * n_arr, out_specs=[_ANY] * n_arr, scratch_shapes=_comm_sems(n_arr),
    )(*arrays)


def _gather_out_shapes(arrays):
    return [jax.ShapeDtypeStruct((NDEV * a.shape[0], a.shape[1]), a.dtype) for a in arrays]


def _comm_sems(n_arr):
    return [pltpu.SemaphoreType.DMA((n_arr, 7)), pltpu.SemaphoreType.DMA((n_arr, 7)), pltpu.SemaphoreType.DMA((n_arr,))]


def _gather_steps(x_refs, out_refs, send_sems, recv_sems, local_sems):
    n_arr = len(x_refs)
    x, y, c = _my_place()
    me, sibling = (x, y, c), (x, y, 1 - c)
    chips = [(1 - x, y), (x, 1 - y), (1 - x, 1 - y)]

    def rows(k, px, py, pc):
        m = x_refs[k].shape[0]
        return out_refs[k].at[pl.ds((4 * px + 2 * py + pc) * m, m), :]

    def copy(k, s, block, to, src=None):
        return pltpu.make_async_remote_copy(
            src_ref=rows(k, *block) if src is None else src, dst_ref=rows(k, *block),
            send_sem=send_sems.at[k, s], recv_sem=recv_sems.at[k, s], device_id=to, device_id_type=MESH)

    def mine():
        return [pltpu.make_async_copy(x_refs[k], rows(k, *me), local_sems.at[k]) for k in range(n_arr)]

    def first():
        cps = []
        for k in range(n_arr):
            cps.append(copy(k, 0, me, sibling, src=x_refs[k]))
            cps += [copy(k, 1 + j, me, (*chip, c), src=x_refs[k]) for j, chip in enumerate(chips)]
        return cps

    def passed():
        return [copy(k, 4 + j, (*chip, c), sibling) for k in range(n_arr) for j, chip in enumerate(chips)]

    def start():
        for cp in mine() + first():
            cp.start()

    def forward():
        fwd = passed()
        for k in range(n_arr):
            for j, chip in enumerate(chips):
                copy(k, 1 + j, (*chip, c), me).wait_recv()
                fwd[3 * k + j].start()

    def finish():
        for k in range(n_arr):
            copy(k, 0, sibling, me).wait_recv()
            for j, chip in enumerate(chips):
                copy(k, 4 + j, (*chip, 1 - c), me).wait_recv()
        for cp in first() + passed():
            cp.wait_send()
        for cp in mine():
            cp.wait()

    return start, forward, finish


def _exchange_blocks(arrays, name):
    n_arr = len(arrays)

    def body(*refs):
        start, finish = _exchange_steps(refs[:n_arr], refs[n_arr:2 * n_arr], *refs[2 * n_arr:])
        start()
        finish()

    return pl.pallas_call(
        body, name=name, out_shape=_exchange_out_shapes(arrays),
        in_specs=[_ANY] * n_arr, out_specs=[_ANY] * n_arr, scratch_shapes=_comm_sems(n_arr),
    )(*arrays)


def _exchange_and_gather(ex_arrays, ga_arrays, name, half=None):
    n_e, n_g = len(ex_arrays), len(ga_arrays)

    def body(*refs):
        e_in, g_in, refs = refs[:n_e], refs[n_e:n_e + n_g], refs[n_e + n_g:]
        e_out, g_out, sems = refs[:n_e], refs[n_e:n_e + n_g], refs[n_e + n_g:]
        e_start, e_finish = _exchange_steps(e_in, e_out, *sems[:3], half=half)
        g_start, g_forward, g_finish = _gather_steps(g_in, g_out, *sems[3:])
        g_start()
        e_start()
        g_forward()
        g_finish()
        e_finish()

    res = pl.pallas_call(
        body, name=name, out_shape=_exchange_out_shapes(ex_arrays, half) + _gather_out_shapes(ga_arrays),
        in_specs=[_ANY] * (n_e + n_g), out_specs=[_ANY] * (n_e + n_g),
        scratch_shapes=_comm_sems(n_e) + _comm_sems(n_g),
    )(*ex_arrays, *ga_arrays)
    return res[:n_e], res[n_e:]


def _exchange_out_shapes(arrays, half=None):
    n_blocks = NDEV if half is None else NDEV // 2
    return [jax.ShapeDtypeStruct((NDEV, a.shape[0] // n_blocks, a.shape[1]), a.dtype) for a in arrays]


def _exchange_steps(g_refs, out_refs, send_sems, recv_sems, local_sems, half=None):
    n_arr = len(g_refs)
    x, y, c = _my_place()
    me = 4 * x + 2 * y + c
    receives = True if half is None else c == half

    def when(cond, fn):
        if cond is True:
            fn()
        else:
            pl.when(cond)(fn)

    def sends(rel):
        return True if half is None else (c ^ (rel & 1)) == half

    def copy(k, rel):
        px, py, pc = x ^ ((rel >> 2) & 1), y ^ ((rel >> 1) & 1), c ^ (rel & 1)
        m = out_refs[k].shape[1]
        blk = 4 * px + 2 * py + pc if half is None else 2 * px + py
        return pltpu.make_async_remote_copy(
            src_ref=g_refs[k].at[pl.ds(blk * m, m), :], dst_ref=out_refs[k].at[me],
            send_sem=send_sems.at[k, rel - 1], recv_sem=recv_sems.at[k, rel - 1],
            device_id=(px, py, pc), device_id_type=MESH)

    def local(k):
        m = out_refs[k].shape[1]
        blk = me if half is None else 2 * x + y
        return pltpu.make_async_copy(g_refs[k].at[pl.ds(blk * m, m), :], out_refs[k].at[me], local_sems.at[k])

    def start():
        for k in range(n_arr):
            when(receives, local(k).start)
            for rel in range(1, NDEV):
                when(sends(rel), copy(k, rel).start)

    def finish():
        for k in range(n_arr):
            for rel in range(1, NDEV):
                when(sends(rel), copy(k, rel).wait_send)
                when(receives, copy(k, rel).wait_recv)
            when(receives, local(k).wait)

    return start, finish


def _ada_fwd(c_all, w_ada_loc, b_ada_loc):
    def body(c_ref, w_ref, b_ref, o_ref):
        cv = c_ref[...]
        ca = (cv * _sigmoid(cv)).astype(MXU_DTYPE)
        o_ref[...] = _dot(ca, w_ref[...].astype(MXU_DTYPE)) + b_ref[...]

    return pl.pallas_call(
        body, name="ada_fwd", out_shape=jax.ShapeDtypeStruct((c_all.shape[0], w_ada_loc.shape[1]), F32),
        in_specs=[_full(c_all.shape), _full(w_ada_loc.shape), _full(b_ada_loc.shape)],
        out_specs=_full((c_all.shape[0], w_ada_loc.shape[1])),
        compiler_params=pltpu.CompilerParams(vmem_limit_bytes=VMEM_LIMIT),
    )(c_all, w_ada_loc, b_ada_loc)


def _ada_bwd(c_all, dmod_all, dmod_cols):
    def body(c_ref, da_ref, d_ref, o_ref, b_ref):
        cv = c_ref[...]
        ca = (cv * _sigmoid(cv)).astype(MXU_DTYPE)
        o_ref[...] = _dot_tn(ca, d_ref[...].astype(MXU_DTYPE))
        b_ref[...] = _colsum(da_ref[...])

    return pl.pallas_call(
        body, name="ada_bwd",
        out_shape=[jax.ShapeDtypeStruct((c_all.shape[1], dmod_cols.shape[1]), F32),
                   jax.ShapeDtypeStruct((1, dmod_all.shape[1]), F32)],
        in_specs=[_full(c_all.shape), _full(dmod_all.shape), _full(dmod_cols.shape)],
        out_specs=[_full((c_all.shape[1], dmod_cols.shape[1])), _full((1, dmod_all.shape[1]))],
        compiler_params=pltpu.CompilerParams(vmem_limit_bytes=VMEM_LIMIT),
    )(c_all, dmod_all, dmod_cols)


def _mixer_fwd(x, mods, win_t, wout, wri, mavg2, va, vb, ln1, ffn_shards):
    bl, seq, _ = x.shape
    t = min(TOK_TILE, seq)
    nt = seq // t
    n_g = len(ffn_shards)
    rb = min(ROW_BLK, t)

    def body(x_ref, mod_ref, win_hbm, wout_hbm, wri_ref, mavg_ref, va_ref, vb_ref, ln1_ref, *rest):
        shard_refs, rest = rest[:n_g], rest[n_g:]
        proj_ref, h_ref, vc_ref, y_ref, mixed_ref, x1_ref, u1_ref = rest[:7]
        gathered_refs, rest = rest[7:7 + n_g], rest[7 + n_g:]
        (win_v, wout_v, xa_ext, vg_ext, rot_a, hcar, sems, rot_c, xc_scr, a_scr, st_a, st_b, xcb_scr, gates, hilo,
         g_send, g_recv, g_local) = rest
        b, i = pl.program_id(0), pl.program_id(1)
        step = b * nt + i
        g_start, g_forward, g_finish = _gather_steps(shard_refs, gathered_refs, g_send, g_recv, g_local)

        @pl.when(step == 0)
        def _():
            g_start()
            cps = [pltpu.make_async_copy(win_hbm, win_v, sems.at[0]), pltpu.make_async_copy(wout_hbm, wout_v, sems.at[1])]
            for cp in cps:
                cp.start()
            for cp in cps:
                cp.wait()

        pl.when(step == (bl * nt) // 2)(g_forward)

        @pl.when(i == 0)
        def _():
            xa_ext[0:HALO, :] = jnp.zeros((HALO, LW), F32)
            vg_ext[0:HALO31, :] = jnp.zeros((HALO31, CW), F32)
            hcar[...] = jnp.zeros_like(hcar)

        blocks = [(r0, c0) for c0 in range(0, LW, LANE) for r0 in range(0, t, rb)]

        def in_rows(r0, acc):
            rows = pl.ds(r0, 16)
            u1_ref[rows, :] = (x_ref[rows, :] * (1.0 + mod_ref[1:2, :]) + mod_ref[0:1, :]).astype(MXU_DTYPE)
            return acc

        _row_blocks(t, min(16, t), in_rows, ())
        proj_ref[...] = _dot_nt(u1_ref[...], win_v[...])

        xa_ext[HALO:HALO + t, :] = proj_ref[:, 0:LW]
        for k in range(LRU_K - 1):
            rot_c[k] = xa_ext[pl.ds(HALO - (LRU_K - 1) + k, t), :]
        for r0, c0 in blocks:
            rows, cols = pl.ds(r0, rb), slice(c0, c0 + LANE)
            xc = va_ref[4:5, cols] + va_ref[LRU_K - 1:LRU_K, cols] * xa_ext[pl.ds(HALO + r0, rb), cols]
            for k in range(LRU_K - 1):
                xc = xc + va_ref[k:k + 1, cols] * rot_c[k, rows, cols]
            xc_scr[rows, cols] = xc
            xcb_scr[rows, cols] = xc.astype(MXU_DTYPE)
        xa_ext[0:HALO, :] = xa_ext[t:t + HALO, :]
        gates[...] = _dot(xcb_scr[...], wri_ref[...])
        for r0, c0 in blocks:
            rows, cols = pl.ds(r0, rb), slice(c0, c0 + LANE)
            r = _sigmoid(gates[rows, cols] + va_ref[5:6, cols])
            ig = _sigmoid(gates[rows, LW + c0:LW + c0 + LANE] + va_ref[6:7, cols])
            la = (-LRU_C) * r * _softplus(-va_ref[7:8, cols])
            a = jnp.exp(la)
            a_scr[rows, cols] = a
            h_ref[rows, cols] = jnp.sqrt(-jnp.tanh(la) * (a * a + 1.0)) * (ig * xc_scr[rows, cols])
        _scan_fwd_blocks(a_scr, h_ref, hcar, t)

        for r0, c0 in blocks:
            rows, cols, pc = pl.ds(r0, rb), slice(c0, c0 + LANE), 2 * LW + c0
            vg_ext[pl.ds(HALO31 + r0, rb), cols] = (
                proj_ref[rows, pc:pc + LANE] * _sigmoid(proj_ref[rows, pc + CW:pc + CW + LANE]))
        vg_win = _shifted_rows(vg_ext, rot_a, rb)
        for r0, c0 in blocks:
            rows, cols = pl.ds(r0, rb), slice(c0, c0 + LANE)
            vc = vb_ref[CONV_K:CONV_K + 1, cols]
            for k in range(CONV_K):
                vc = vc + vb_ref[k:k + 1, cols] * vg_win(HALO31 - (CONV_K - 1) + k, r0, cols)
            vc_ref[rows, cols] = vc
            _store_hilo(hilo, rows, c0, vc)
        vg_ext[0:HALO31, :] = vg_ext[t:t + HALO31, :]
        mavg2 = mavg_ref[...]
        st_a[...] = _dot(hilo[...], mavg2)
        for r0, c0 in blocks:
            rows, cols = pl.ds(r0, rb), slice(c0, c0 + LANE)
            dl = vc_ref[rows, cols] - st_a[rows, cols]
            st_a[rows, cols] = dl
            _store_hilo(hilo, rows, c0, dl * dl)
        st_b[...] = _dot(hilo[...], mavg2)
        for r0, c0 in blocks:
            rows, cols = pl.ds(r0, rb), slice(c0, c0 + LANE)
            ge, _ = _gelu_and_grad(proj_ref[rows, LW + c0:LW + c0 + LANE])
            y_ref[rows, cols] = (ge * h_ref[rows, cols]).astype(MXU_DTYPE)
            yl = (st_a[rows, cols] * lax.rsqrt(st_b[rows, cols] + LN_EPS) * vb_ref[CONV_K + 1:CONV_K + 2, cols]
                  + vb_ref[CONV_K + 2:CONV_K + 3, cols])
            y_ref[rows, LW + c0:LW + c0 + LANE] = (yl * _sigmoid(yl)).astype(MXU_DTYPE)
        mixed_ref[...] = _dot(y_ref[...], wout_v[...])

        def out_rows(r0, acc):
            rows = pl.ds(r0, 8)
            xh, _ = _layer_norm_stats(ALPHA * x_ref[rows, :] + (1.0 + mod_ref[2:3, :]) * mixed_ref[rows, :])
            x1_ref[rows, :] = xh * ln1_ref[0:1, :] + ln1_ref[1:2, :]
            return acc

        _row_blocks(t, 8, out_rows, ())

        pl.when(step == bl * nt - 1)(g_finish)

    tok = lambda w: pl.BlockSpec((None, t, w), lambda b, i: (b, i, 0))
    outs = [(2 * LW + 2 * CW, F32), (LW, F32), (CW, F32), (D, MXU_DTYPE), (D, F32), (D, F32), (D, MXU_DTYPE)]
    return pl.pallas_call(
        body, name="mixer_fwd", grid=(bl, nt),
        out_shape=[jax.ShapeDtypeStruct((bl, seq, w), dt) for w, dt in outs] + _gather_out_shapes(ffn_shards),
        in_specs=[tok(D), pl.BlockSpec((None, 8, D), lambda b, i: (b, 0, 0)), _ANY, _ANY,
                  _full(wri.shape), _full(mavg2.shape), _full(va.shape), _full(vb.shape), _full(ln1.shape)]
        + [_ANY] * n_g,
        out_specs=[tok(w) for w, _ in outs] + [_ANY] * n_g,
        scratch_shapes=[pltpu.VMEM(win_t.shape, MXU_DTYPE), pltpu.VMEM(wout.shape, MXU_DTYPE),
                        pltpu.VMEM((t + HALO, LW), F32), pltpu.VMEM((t + HALO31, CW), F32),
                        pltpu.VMEM((7, t + HALO31 - 8, CW), F32), pltpu.VMEM((8, LW), F32),
                        pltpu.SemaphoreType.DMA((2,)), pltpu.VMEM((LRU_K - 1, t, LW), F32)]
        + [pltpu.VMEM((t, LW), F32)] * 4
        + [pltpu.VMEM((t, LW), MXU_DTYPE), pltpu.VMEM((t, 2 * LW), F32), pltpu.VMEM((t, 2 * CW), MXU_DTYPE)]
        + _comm_sems(n_g),
        compiler_params=_params(2),
    )(x, mods, win_t, wout, wri, mavg2, va, vb, ln1, *ffn_shards)


def _ffn_fwd(x1, mods, tgt, wup_t, wdown, f3, ln2):
    bl, seq, _ = x1.shape
    t = min(TOK_TILE, seq)
    nt = seq // t
    n_chunk = DFF // FF_CHUNK

    rb = min(ROW_BLK, t)

    def body(x1_ref, mod_ref, tgt_ref, wup_hbm, wdown_hbm, f3_ref, ln2_ref,
             hh_ref, z_ref, u2_ref, y2_ref, loss_ref,
             wup_v, wdown_v, g_ext, rot, sems):
        b, i = pl.program_id(0), pl.program_id(1)

        @pl.when((b == 0) & (i == 0))
        def _():
            cps = [pltpu.make_async_copy(wup_hbm, wup_v, sems.at[0]), pltpu.make_async_copy(wdown_hbm, wdown_v, sems.at[1])]
            for cp in cps:
                cp.start()
            for cp in cps:
                cp.wait()
            loss_ref[...] = jnp.zeros_like(loss_ref)

        @pl.when(i == 0)
        def _():
            for ch in range(n_chunk):
                g_ext[ch, 0:HALO, :] = jnp.zeros((HALO, FF_CHUNK), F32)

        def in_rows(r0, acc):
            rows = pl.ds(r0, 16)
            u2_ref[rows, :] = (x1_ref[rows, :] * (1.0 + mod_ref[4:5, :]) + mod_ref[3:4, :]).astype(MXU_DTYPE)
            return acc

        _row_blocks(t, min(16, t), in_rows, ())
        for ch in range(n_chunk):
            lo = ch * FF_CHUNK
            hh_ref[:, lo:lo + FF_CHUNK] = _dot_nt(u2_ref[...], wup_v[lo:lo + FF_CHUNK, :])
            hh_ref[:, DFF + lo:DFF + lo + FF_CHUNK] = _dot_nt(u2_ref[...], wup_v[DFF + lo:DFF + lo + FF_CHUNK, :])
            g_ext[ch, HALO:HALO + t, :] = hh_ref[:, DFF + lo:DFF + lo + FF_CHUNK]
            for k in range(FFN_K - 1):
                rot[k] = g_ext[ch, pl.ds(HALO - (FFN_K - 1) + k, t), :]
            for cb in range(FF_CHUNK // LANE):
                cols = slice(cb * LANE, (cb + 1) * LANE)
                fcols = slice(lo + cb * LANE, lo + (cb + 1) * LANE)
                for r0 in range(0, t, rb):
                    rows = pl.ds(r0, rb)
                    gc = (f3_ref[FFN_K:FFN_K + 1, fcols] + f3_ref[0:1, fcols] * rot[0, rows, cols]
                          + f3_ref[1:2, fcols] * rot[1, rows, cols]
                          + f3_ref[2:3, fcols] * g_ext[ch, pl.ds(HALO + r0, rb), cols])
                    z_ref[rows, fcols] = (gc * _sigmoid(gc) * hh_ref[rows, fcols]).astype(MXU_DTYPE)
            g_ext[ch, 0:HALO, :] = g_ext[ch, t:t + HALO, :]
        y2_ref[...] = _dot(z_ref[...], wdown_v[...])

        def out_rows(r0, acc):
            rows = pl.ds(r0, 8)
            xh, _ = _layer_norm_stats(ALPHA * x1_ref[rows, :] + (1.0 + mod_ref[5:6, :]) * y2_ref[rows, :])
            err = xh * ln2_ref[0:1, :] + ln2_ref[1:2, :] - tgt_ref[rows, :]
            return acc + err * err

        e2 = _row_blocks(t, 8, out_rows, jnp.zeros((8, D), F32))
        part = e2[:, 0:128]
        for j in range(1, D // 128):
            part = part + e2[:, 128 * j:128 * (j + 1)]
        loss_ref[...] += part

    tok = lambda w: pl.BlockSpec((None, t, w), lambda b, i: (b, i, 0))
    outs = [(2 * DFF, F32), (DFF, MXU_DTYPE), (D, MXU_DTYPE), (D, F32)]
    return pl.pallas_call(
        body, name="ffn_fwd", grid=(bl, nt),
        out_shape=[jax.ShapeDtypeStruct((bl, seq, w), dt) for w, dt in outs] + [jax.ShapeDtypeStruct((8, 128), F32)],
        in_specs=[tok(D), pl.BlockSpec((None, 8, D), lambda b, i: (b, 0, 0)), tok(D), _ANY, _ANY,
                  _full(f3.shape), _full(ln2.shape)],
        out_specs=[tok(w) for w, _ in outs] + [_full((8, 128))],
        scratch_shapes=[pltpu.VMEM(wup_t.shape, MXU_DTYPE), pltpu.VMEM(wdown.shape, MXU_DTYPE),
                        pltpu.VMEM((n_chunk, t + HALO, FF_CHUNK), F32), pltpu.VMEM((FFN_K - 1, t, FF_CHUNK), F32),
                        pltpu.SemaphoreType.DMA((2,))],
        compiler_params=_params(2),
    )(x1, mods, tgt, wup_t, wdown, f3, ln2)


def _ffn_bwd(x1, y2, tgt, hh, mods, wup_t, wdown, f3, ln2):
    bl, seq, _ = x1.shape
    t = min(TOK_TILE, seq)
    nt = seq // t
    n_chunk = DFF // FF_CHUNK
    rb = min(ROW_BLK, t)

    def body(x1_ref, y2_ref, tgt_ref, hh_ref, halo_ref, mod_ref, wup_hbm, wdown_hbm, f3_ref, ln2_ref,
             dx1_ref, dy2_ref, dhh_ref, dln2_ref, df3_ref, dmod_ref,
             wup_v, wdown_v, g_ext, dgc_ext, rot, dz_scr, wide_scr, sems):
        b, i = pl.program_id(0), pl.program_id(1)
        tt = nt - 1 - i

        @pl.when((b == 0) & (i == 0))
        def _():
            cps = [pltpu.make_async_copy(wup_hbm, wup_v, sems.at[0]), pltpu.make_async_copy(wdown_hbm, wdown_v, sems.at[1])]
            for cp in cps:
                cp.start()
            for cp in cps:
                cp.wait()
            dln2_ref[...] = jnp.zeros_like(dln2_ref)
            df3_ref[...] = jnp.zeros_like(df3_ref)

        @pl.when(i == 0)
        def _():
            dmod_ref[...] = jnp.zeros_like(dmod_ref)
            for ch in range(n_chunk):
                dgc_ext[ch, t:t + HALO, :] = jnp.zeros((HALO, FF_CHUNK), F32)

        def ln_rows(r0, acc):
            rows = pl.ds(r0, 8)
            x1v, y2v = x1_ref[rows, :], y2_ref[rows, :]
            gt = mod_ref[5:6, :]
            xh, rstd = _layer_norm_stats(ALPHA * x1v + (1.0 + gt) * y2v)
            g2 = ln2_ref[0:1, :]
            dx2 = (xh * g2 + ln2_ref[1:2, :] - tgt_ref[rows, :]) * (1.0 / D)
            dxh = dx2 * g2
            dr2 = rstd * (dxh - _rowmean(dxh) - xh * _rowmean(dxh * xh))
            wide_scr[rows, :] = (1.0 + gt) * dr2
            dx1_ref[rows, :] = ALPHA * dr2
            return acc[0] + dx2 * xh, acc[1] + dx2, acc[2] + dr2 * y2v

        zero_d = jnp.zeros((8, D), F32)
        acc = _row_blocks(t, 8, ln_rows, (zero_d, zero_d, zero_d))
        dln2_ref[0:1, :] += _colsum(acc[0])
        dln2_ref[1:2, :] += _colsum(acc[1])
        dmod_ref[2:3, :] += _colsum(acc[2])
        _cast_rows(wide_scr, dy2_ref, t)

        halo_keep = (tt > 0).astype(F32)
        zero_l = jnp.zeros((8, LANE), F32)
        for ch in range(n_chunk):
            lo = ch * FF_CHUNK
            dz_scr[...] = _dot_nt(dy2_ref[...], wdown_v[lo:lo + FF_CHUNK, :])
            g_ext[0:HALO, :] = halo_ref[:, lo:lo + FF_CHUNK] * halo_keep
            g_ext[HALO:HALO + t, :] = hh_ref[:, DFF + lo:DFF + lo + FF_CHUNK]
            for k in range(FFN_K - 1):
                rot[k] = g_ext[pl.ds(HALO - (FFN_K - 1) + k, t), :]
            for cb in range(FF_CHUNK // LANE):
                cols = slice(cb * LANE, (cb + 1) * LANE)
                fcols = slice(lo + cb * LANE, lo + (cb + 1) * LANE)

                def gate_rows(r0, acc, cols=cols, fcols=fcols, ch=ch):
                    rows = pl.ds(r0, rb)
                    taps = [rot[0, rows, cols], rot[1, rows, cols], g_ext[pl.ds(r0 + HALO, rb), cols]]
                    gc = f3_ref[FFN_K:FFN_K + 1, fcols]
                    for k in range(FFN_K):
                        gc = gc + f3_ref[k:k + 1, fcols] * taps[k]
                    s = _sigmoid(gc)
                    dz = dz_scr[rows, cols]
                    dhh_ref[rows, fcols] = (dz * (gc * s)).astype(MXU_DTYPE)
                    dgc = dz * hh_ref[rows, fcols] * (s * (1.0 + gc * (1.0 - s)))
                    dgc_ext[ch, rows, cols] = dgc
                    return tuple(acc[k] + _fold8(dgc * taps[k]) for k in range(FFN_K)) + (acc[FFN_K] + _fold8(dgc),)

                acc = _row_blocks(t, rb, gate_rows, (zero_l,) * (FFN_K + 1))
                for k in range(FFN_K + 1):
                    df3_ref[k:k + 1, fcols] += _colsum(acc[k])
            for k in range(FFN_K - 1):
                rot[k] = dgc_ext[ch, pl.ds(k + 1, t), :]
            for cb in range(FF_CHUNK // LANE):
                cols = slice(cb * LANE, (cb + 1) * LANE)
                fcols = slice(lo + cb * LANE, lo + (cb + 1) * LANE)
                gcols = slice(DFF + lo + cb * LANE, DFF + lo + (cb + 1) * LANE)

                def dg_rows(r0, acc, cols=cols, fcols=fcols, gcols=gcols, ch=ch):
                    rows = pl.ds(r0, rb)
                    dg = (f3_ref[2:3, fcols] * dgc_ext[ch, rows, cols] + f3_ref[1:2, fcols] * rot[0, rows, cols]
                          + f3_ref[0:1, fcols] * rot[1, rows, cols])
                    dhh_ref[rows, gcols] = dg.astype(MXU_DTYPE)
                    return acc

                _row_blocks(t, rb, dg_rows, ())
            dgc_ext[ch, t:t + HALO, :] = dgc_ext[ch, 0:HALO, :]

        wide_scr[...] = _dot(dhh_ref[...], wup_v[...])

        def out_rows(r0, acc):
            rows = pl.ds(r0, 8)
            du2 = wide_scr[rows, :]
            dx1_ref[rows, :] = dx1_ref[rows, :] + du2 * (1.0 + mod_ref[4:5, :])
            return acc[0] + du2 * x1_ref[rows, :], acc[1] + du2

        acc = _row_blocks(t, 8, out_rows, (zero_d, zero_d))
        dmod_ref[1:2, :] += _colsum(acc[0])
        dmod_ref[0:1, :] += _colsum(acc[1])

    rev = lambda w: pl.BlockSpec((None, t, w), lambda b, i: (b, nt - 1 - i, 0))
    halo = pl.BlockSpec((None, HALO, DFF), lambda b, i: (b, jnp.maximum((nt - 1 - i) * (t // HALO) - 1, 0), 1))
    return pl.pallas_call(
        body, name="ffn_bwd", grid=(bl, nt),
        out_shape=[jax.ShapeDtypeStruct((bl, seq, D), F32), jax.ShapeDtypeStruct((bl, seq, D), MXU_DTYPE),
                   jax.ShapeDtypeStruct((bl, seq, 2 * DFF), MXU_DTYPE), jax.ShapeDtypeStruct((8, D), F32),
                   jax.ShapeDtypeStruct((8, DFF), F32), jax.ShapeDtypeStruct((bl, 8, D), F32)],
        in_specs=[rev(D), rev(D), rev(D), rev(2 * DFF), halo, pl.BlockSpec((None, 8, D), lambda b, i: (b, 0, 0)),
                  _ANY, _ANY, _full(f3.shape), _full(ln2.shape)],
        out_specs=[rev(D), rev(D), rev(2 * DFF), _full((8, D)), _full((8, DFF)),
                   pl.BlockSpec((None, 8, D), lambda b, i: (b, 0, 0))],
        scratch_shapes=[pltpu.VMEM(wup_t.shape, MXU_DTYPE), pltpu.VMEM(wdown.shape, MXU_DTYPE),
                        pltpu.VMEM((t + HALO, FF_CHUNK), F32), pltpu.VMEM((n_chunk, t + HALO, FF_CHUNK), F32),
                        pltpu.VMEM((FFN_K - 1, t, FF_CHUNK), F32), pltpu.VMEM((t, FF_CHUNK), F32),
                        pltpu.VMEM((t, D), F32), pltpu.SemaphoreType.DMA((2,))],
        compiler_params=_params(2),
    )(x1, y2, tgt, hh, hh, mods, wup_t, wdown, f3, ln2)


def _mixer_bwd(x, dx1, proj, h, vc, mixed, mods, win_t, wout, wri, mavg2, va, vb, ln1, ffn_wgrads):
    bl, seq, _ = x.shape
    t = min(TOK_TILE, seq)
    nt = seq // t
    pw = 2 * LW + 2 * CW
    n_g = len(ffn_wgrads)
    rb = min(ROW_BLK, t)
    half_f32 = [pltpu.VMEM((t, LW), F32)] * 10

    def body(x_ref, dx1_ref, proj_ref, phalo_ref, h_ref, hhalo_ref, vc_ref, mixed_ref, mod_ref,
             win_hbm, wout_hbm, wri_ref, mavg_ref, va_ref, vb_ref, ln1_ref, *rest):
        wgrad_refs, rest = rest[:n_g], rest[n_g:]
        gx_ref, dproj_ref, dm_ref, ga_ref, gw31_ref, dwr_ref, dwi_ref, dln1_ref, dmod_ref = rest[:9]
        slot_refs, rest = rest[9:9 + n_g], rest[9 + n_g:]
        (win_v, wout_v, xa_ext, vg_scr, dvc_ext, dxc_ext, rot_b, car, sems,
         st_a, st_b, st_c, yn_scr, dyn_scr, hprev_scr, xc_scr, a_scr, mult_scr, g_scr,
         wide, gates, hilo_a, hilo_b, dgates, xcb_scr, rot_c, rot_d, hp_ext, e_send, e_recv, e_local) = rest
        b, i = pl.program_id(0), pl.program_id(1)
        tt = nt - 1 - i
        e_start, e_finish = _exchange_steps(wgrad_refs, slot_refs, e_send, e_recv, e_local)

        @pl.when((b == 0) & (i == 0))
        def _():
            e_start()
            cps = [pltpu.make_async_copy(win_hbm, win_v, sems.at[0]), pltpu.make_async_copy(wout_hbm, wout_v, sems.at[1])]
            for cp in cps:
                cp.start()
            for cp in cps:
                cp.wait()
            for ref in (ga_ref, gw31_ref, dwr_ref, dwi_ref, dln1_ref):
                ref[...] = jnp.zeros_like(ref)

        @pl.when(i == 0)
        def _():
            dmod_ref[...] = jnp.zeros_like(dmod_ref)
            dvc_ext[t:t + HALO31, :] = jnp.zeros((HALO31, CW), F32)
            dxc_ext[t:t + HALO, :] = jnp.zeros((HALO, LW), F32)
            car[...] = jnp.zeros_like(car)

        halo_keep = (tt > 0).astype(F32)
        zero_d = jnp.zeros((8, D), F32)
        zero_l = jnp.zeros((8, LANE), F32)
        blocks = [(r0, c0) for c0 in range(0, LW, LANE) for r0 in range(0, t, rb)]

        def add_row(ref, row, c0, acc):
            ref[row:row + 1, c0:c0 + LANE] += _colsum(acc)

        def ln_rows(r0, acc):
            rows = pl.ds(r0, 8)
            xv, mixed = x_ref[rows, :], mixed_ref[rows, :]
            gt = mod_ref[2:3, :]
            xh, rstd = _layer_norm_stats(ALPHA * xv + (1.0 + gt) * mixed)
            dx1 = dx1_ref[rows, :]
            dxh = dx1 * ln1_ref[0:1, :]
            dr1 = rstd * (dxh - _rowmean(dxh) - xh * _rowmean(dxh * xh))
            wide[rows, :] = (1.0 + gt) * dr1
            gx_ref[rows, :] = ALPHA * dr1
            return acc[0] + dx1 * xh, acc[1] + dx1, acc[2] + dr1 * mixed

        acc = _row_blocks(t, 8, ln_rows, (zero_d, zero_d, zero_d))
        dln1_ref[0:1, :] += _colsum(acc[0])
        dln1_ref[1:2, :] += _colsum(acc[1])
        dmod_ref[2:3, :] += _colsum(acc[2])
        _cast_rows(wide, dm_ref, t)
        wide[...] = _dot_nt(dm_ref[...], wout_v[...])

        mavg2 = mavg_ref[...]
        for r0, c0 in blocks:
            rows, cols = pl.ds(r0, rb), slice(c0, c0 + LANE)
            _store_hilo(hilo_a, rows, c0, vc_ref[rows, cols])
        st_a[...] = _dot(hilo_a[...], mavg2)
        for r0, c0 in blocks:
            rows, cols = pl.ds(r0, rb), slice(c0, c0 + LANE)
            dl = vc_ref[rows, cols] - st_a[rows, cols]
            yn_scr[rows, cols] = dl
            _store_hilo(hilo_a, rows, c0, dl * dl)
        st_b[...] = _dot(hilo_a[...], mavg2)
        for c0 in range(0, CW, LANE):
            cols, pc = slice(c0, c0 + LANE), 2 * LW + c0
            ng, nb = vb_ref[CONV_K + 1:CONV_K + 2, cols], vb_ref[CONV_K + 2:CONV_K + 3, cols]
            acc_g, acc_b = zero_l, zero_l
            for r0 in range(0, t, rb):
                rows = pl.ds(r0, rb)
                rs = lax.rsqrt(st_b[rows, cols] + LN_EPS)
                yn = yn_scr[rows, cols] * rs
                yl = yn * ng + nb
                s = _sigmoid(yl)
                dyl = wide[rows, LW + c0:LW + c0 + LANE] * (s * (1.0 + yl * (1.0 - s)))
                acc_g, acc_b = acc_g + _fold8(dyl * yn), acc_b + _fold8(dyl)
                dyn = dyl * ng
                st_b[rows, cols] = rs
                yn_scr[rows, cols] = yn
                dyn_scr[rows, cols] = dyn
                _store_hilo(hilo_a, rows, c0, dyn)
                _store_hilo(hilo_b, rows, c0, dyn * yn)
                vg_scr[rows, cols] = proj_ref[rows, pc:pc + LANE] * _sigmoid(proj_ref[rows, pc + CW:pc + CW + LANE])
            add_row(ga_ref, 8, c0, acc_g)
            add_row(ga_ref, 9, c0, acc_b)
        st_a[...] = _dot(hilo_a[...], mavg2)
        st_c[...] = _dot(hilo_b[...], mavg2)
        for c0 in range(0, CW, LANE):
            cols = slice(c0, c0 + LANE)
            acc_b = zero_l
            for r0 in range(0, t, rb):
                rows = pl.ds(r0, rb)
                dvc = st_b[rows, cols] * (dyn_scr[rows, cols] - st_a[rows, cols] - yn_scr[rows, cols] * st_c[rows, cols])
                dvc_ext[rows, cols] = dvc
                acc_b = acc_b + _fold8(dvc)
            add_row(ga_ref, 10, c0, acc_b)
        dvc_win = _shifted_rows(dvc_ext, rot_b, rb)
        for c0 in range(0, CW, LANE):
            cols, pc = slice(c0, c0 + LANE), 2 * LW + c0
            for k in range(CONV_K):
                acc_w = zero_l
                for r0 in range(0, t, rb):
                    acc_w = acc_w + _fold8(vg_scr[pl.ds(r0, rb), cols] * dvc_win(CONV_K - 1 - k, r0, cols))
                add_row(gw31_ref, k, c0, acc_w)
            for r0 in range(0, t, rb):
                rows = pl.ds(r0, rb)
                dvg = jnp.zeros((rb, LANE), F32)
                for k in range(CONV_K):
                    dvg = dvg + vb_ref[k:k + 1, cols] * dvc_win(CONV_K - 1 - k, r0, cols)
                vbr = proj_ref[rows, pc:pc + LANE]
                sgb = _sigmoid(proj_ref[rows, pc + CW:pc + CW + LANE])
                dproj_ref[rows, pc:pc + LANE] = (dvg * sgb).astype(MXU_DTYPE)
                dproj_ref[rows, pc + CW:pc + CW + LANE] = (dvg * vbr * sgb * (1.0 - sgb)).astype(MXU_DTYPE)
        dvc_ext[t:t + HALO31, :] = dvc_ext[0:HALO31, :]

        xa_ext[0:HALO, :] = phalo_ref[HALO31 - HALO:HALO31, 0:LW] * halo_keep
        xa_ext[HALO:HALO + t, :] = proj_ref[:, 0:LW]
        for k in range(LRU_K - 1):
            rot_c[k] = xa_ext[pl.ds(HALO - (LRU_K - 1) + k, t), :]
        hp_ext[0:HALO, :] = hhalo_ref[...] * halo_keep
        hp_ext[HALO:HALO + t, :] = h_ref[...]
        hprev_scr[...] = hp_ext[pl.ds(HALO - 1, t), :]

        def xa_tap(k, rows, r0, cols):
            return xa_ext[pl.ds(HALO + r0, rb), cols] if k == LRU_K - 1 else rot_c[k, rows, cols]

        for r0, c0 in blocks:
            rows, cols = pl.ds(r0, rb), slice(c0, c0 + LANE)
            xc = va_ref[4:5, cols]
            for k in range(LRU_K):
                xc = xc + va_ref[k:k + 1, cols] * xa_tap(k, rows, r0, cols)
            xc_scr[rows, cols] = xc
            xcb_scr[rows, cols] = xc.astype(MXU_DTYPE)
        gates[...] = _dot(xcb_scr[...], wri_ref[...])
        for r0, c0 in blocks:
            rows, cols = pl.ds(r0, rb), slice(c0, c0 + LANE)
            r = _sigmoid(gates[rows, cols] + va_ref[5:6, cols])
            ig = _sigmoid(gates[rows, LW + c0:LW + c0 + LANE] + va_ref[6:7, cols])
            la = (-LRU_C) * r * _softplus(-va_ref[7:8, cols])
            a = jnp.exp(la)
            gates[rows, cols] = r
            gates[rows, LW + c0:LW + c0 + LANE] = ig
            a_scr[rows, cols] = a
            mult_scr[rows, cols] = jnp.sqrt(-jnp.tanh(la) * (a * a + 1.0))
            ge, dge = _gelu_and_grad(proj_ref[rows, LW + c0:LW + c0 + LANE])
            dya = wide[rows, cols]
            dproj_ref[rows, LW + c0:LW + c0 + LANE] = (dya * h_ref[rows, cols] * dge).astype(MXU_DTYPE)
            g_scr[rows, cols] = dya * ge
        _scan_bwd_blocks(a_scr, g_scr, car, t)
        for c0 in range(0, LW, LANE):
            cols = slice(c0, c0 + LANE)
            sp = _softplus(-va_ref[7:8, cols])
            acc_l, acc_r, acc_i = zero_l, zero_l, zero_l
            for r0 in range(0, t, rb):
                rows = pl.ds(r0, rb)
                gv, xc, a, mult = g_scr[rows, cols], xc_scr[rows, cols], a_scr[rows, cols], mult_scr[rows, cols]
                r, ig = gates[rows, cols], gates[rows, LW + c0:LW + c0 + LANE]
                dla = gv * hprev_scr[rows, cols] * a - gv * (ig * xc) * (a * a / mult)
                acc_l = acc_l + _fold8(dla * ((-LRU_C) * r))
                dgr = dla * ((-LRU_C) * sp) * r * (1.0 - r)
                dgi = gv * (mult * xc) * ig * (1.0 - ig)
                acc_r, acc_i = acc_r + _fold8(dgr), acc_i + _fold8(dgi)
                dgates[rows, cols] = dgr.astype(MXU_DTYPE)
                dgates[rows, LW + c0:LW + c0 + LANE] = dgi.astype(MXU_DTYPE)
                dxc_ext[rows, cols] = gv * (mult * ig)
            add_row(ga_ref, 5, c0, acc_r)
            add_row(ga_ref, 6, c0, acc_i)
            ga_ref[7:8, cols] += _colsum(acc_l) * (-_sigmoid(-va_ref[7:8, cols]))
        dwr_ref[...] += _dot_tn(xcb_scr[...], dgates[:, 0:LW])
        dwi_ref[...] += _dot_tn(xcb_scr[...], dgates[:, LW:])
        st_a[...] = _dot_nt(dgates[...], wri_ref[...])
        for c0 in range(0, LW, LANE):
            cols = slice(c0, c0 + LANE)
            acc_b = zero_l
            for r0 in range(0, t, rb):
                rows = pl.ds(r0, rb)
                dxc = dxc_ext[rows, cols] + st_a[rows, cols]
                dxc_ext[rows, cols] = dxc
                acc_b = acc_b + _fold8(dxc)
            add_row(ga_ref, 4, c0, acc_b)
        for k in range(LRU_K - 1):
            rot_d[k] = dxc_ext[pl.ds(k + 1, t), :]
        for c0 in range(0, LW, LANE):
            cols = slice(c0, c0 + LANE)
            acc_w = [zero_l] * LRU_K
            for r0 in range(0, t, rb):
                rows = pl.ds(r0, rb)
                dxc = dxc_ext[rows, cols]
                dxa = va_ref[LRU_K - 1:LRU_K, cols] * dxc
                for k in range(LRU_K):
                    acc_w[k] = acc_w[k] + _fold8(dxc * xa_tap(k, rows, r0, cols))
                    if k < LRU_K - 1:
                        dxa = dxa + va_ref[k:k + 1, cols] * rot_d[LRU_K - 2 - k, rows, cols]
                dproj_ref[rows, cols] = dxa.astype(MXU_DTYPE)
            for k in range(LRU_K):
                add_row(ga_ref, k, c0, acc_w[k])
        dxc_ext[t:t + HALO, :] = dxc_ext[0:HALO, :]

        wide[...] = _dot(dproj_ref[...], win_v[...])

        def out_rows(r0, acc):
            rows = pl.ds(r0, 8)
            du1 = wide[rows, :]
            gx_ref[rows, :] = gx_ref[rows, :] + du1 * (1.0 + mod_ref[1:2, :])
            return acc[0] + du1 * x_ref[rows, :], acc[1] + du1

        acc = _row_blocks(t, 8, out_rows, (zero_d, zero_d))
        dmod_ref[1:2, :] += _colsum(acc[0])
        dmod_ref[0:1, :] += _colsum(acc[1])

        pl.when((b == bl - 1) & (i == nt - 1))(e_finish)

    rev = lambda w: pl.BlockSpec((None, t, w), lambda b, i: (b, nt - 1 - i, 0))

    def halo(rows_, w):
        return pl.BlockSpec((None, rows_, w), lambda b, i: (b, jnp.maximum((nt - 1 - i) * (t // rows_) - 1, 0), 0))

    return pl.pallas_call(
        body, name="mixer_bwd", grid=(bl, nt),
        out_shape=[jax.ShapeDtypeStruct((bl, seq, D), F32), jax.ShapeDtypeStruct((bl, seq, pw), MXU_DTYPE),
                   jax.ShapeDtypeStruct((bl, seq, D), MXU_DTYPE), jax.ShapeDtypeStruct((16, LW), F32),
                   jax.ShapeDtypeStruct((32, CW), F32), jax.ShapeDtypeStruct((LW, LW), F32),
                   jax.ShapeDtypeStruct((LW, LW), F32), jax.ShapeDtypeStruct((8, D), F32),
                   jax.ShapeDtypeStruct((bl, 8, D), F32)] + _exchange_out_shapes(ffn_wgrads),
        in_specs=[rev(D), rev(D), rev(pw), halo(HALO31, pw), rev(LW), halo(HALO, LW), rev(CW), rev(D),
                  pl.BlockSpec((None, 8, D), lambda b, i: (b, 0, 0)), _ANY, _ANY,
                  _full(wri.shape), _full(mavg2.shape), _full(va.shape), _full(vb.shape), _full(ln1.shape)]
        + [_ANY] * n_g,
        out_specs=[rev(D), rev(pw), rev(D), _full((16, LW)), _full((32, CW)), _full((LW, LW)), _full((LW, LW)),
                   _full((8, D)), pl.BlockSpec((None, 8, D), lambda b, i: (b, 0, 0))] + [_ANY] * n_g,
        scratch_shapes=[pltpu.VMEM(win_t.shape, MXU_DTYPE), pltpu.VMEM(wout.shape, MXU_DTYPE),
                        pltpu.VMEM((t + HALO, LW), F32), pltpu.VMEM((t, CW), F32),
                        pltpu.VMEM((t + HALO31, CW), F32), pltpu.VMEM((t + HALO, LW), F32),
                        pltpu.VMEM((7, t + HALO31 - 8, CW), F32),
                        pltpu.VMEM((8, LW), F32), pltpu.SemaphoreType.DMA((2,))] + half_f32
        + [pltpu.VMEM((t, D), F32), pltpu.VMEM((t, 2 * LW), F32), pltpu.VMEM((t, 2 * CW), MXU_DTYPE),
           pltpu.VMEM((t, 2 * CW), MXU_DTYPE), pltpu.VMEM((t, 2 * LW), MXU_DTYPE), pltpu.VMEM((t, LW), MXU_DTYPE),
           pltpu.VMEM((LRU_K - 1, t, LW), F32), pltpu.VMEM((LRU_K - 1, t, LW), F32), pltpu.VMEM((t + HALO, LW), F32)]
        + _comm_sems(n_g),
        compiler_params=_params(2),
    )(x, dx1, proj, proj, h, h, vc, mixed, mods, win_t, wout, wri, mavg2, va, vb, ln1, *ffn_wgrads)


def _weight_grad(a, bmat, tm, name, out_dtype=F32, exchange=(), exchange_half=None, col_phase=None, tok_tile=None):
    ntok, m = a.shape
    n = bmat.shape[1]
    tk = min(tok_tile or WG_TOK_TILE, ntok)
    nk = ntok // tk
    stride, col0 = (1, 0) if col_phase is None else (2, col_phase)
    m = m // stride
    ni = m // tm
    n_e = len(exchange)

    def body(a_ref, b_ref, *rest):
        e_in, o_ref, e_out = rest[:n_e], rest[n_e], rest[n_e + 1:2 * n_e + 1]
        acc_ref = rest[2 * n_e + 1]
        i, k = pl.program_id(0), pl.program_id(1)
        if n_e:
            e_start, e_finish = _exchange_steps(e_in, e_out, *rest[2 * n_e + 2:], half=exchange_half)
            pl.when((i == 0) & (k == 0))(e_start)

        @pl.when(k == 0)
        def _():
            acc_ref[...] = jnp.zeros_like(acc_ref)

        acc_ref[...] += _dot_tn(a_ref[...], b_ref[...])

        @pl.when(k == nk - 1)
        def _():
            o_ref[...] = acc_ref[...].astype(out_dtype)

        if n_e:
            pl.when((i == ni - 1) & (k == nk - 1))(e_finish)

    res = pl.pallas_call(
        body, name=name, grid=(ni, nk),
        out_shape=[jax.ShapeDtypeStruct((m, n), out_dtype)] + _exchange_out_shapes(exchange, exchange_half),
        in_specs=[pl.BlockSpec((tk, tm), lambda i, k: (k, i * stride + col0)), pl.BlockSpec((tk, n), lambda i, k: (k, 0))]
        + [_ANY] * n_e,
        out_specs=[pl.BlockSpec((tm, n), lambda i, k: (i, 0))] + [_ANY] * n_e,
        scratch_shapes=[pltpu.VMEM((tm, n), F32)] + (_comm_sems(n_e) if n_e else []),
        compiler_params=_params(2),
    )(a, bmat, *exchange)
    return res if n_e else res[0]


def _sum_slots(slots, name, slots_c1=None):
    _, r, cdim = slots.shape
    tr = 32 if r % 32 == 0 else r
    ins = [slots] if slots_c1 is None else [slots, slots_c1]

    def body(*refs):
        o_ref = refs[-1]

        def total(s_ref):
            acc = s_ref[0].astype(F32)
            for j in range(1, NDEV):
                acc = acc + s_ref[j].astype(F32)
            o_ref[...] = acc

        if slots_c1 is None:
            total(refs[0])
        else:
            c = lax.axis_index("c")
            pl.when(c == 0)(functools.partial(total, refs[0]))
            pl.when(c == 1)(functools.partial(total, refs[1]))

    return pl.pallas_call(
        body, name=name, grid=(r // tr,), out_shape=jax.ShapeDtypeStruct((r, cdim), F32),
        in_specs=[pl.BlockSpec((NDEV, tr, cdim), lambda i: (0, i, 0))] * len(ins),
        out_specs=pl.BlockSpec((tr, cdim), lambda i: (i, 0)),
        compiler_params=_params(1),
    )(*ins)


def _small_grad_layout(bl):
    r1 = 6 * bl
    return {"dmod": 0, "wide": r1, "ln": r1 + 32, "df3": r1 + 36, "g_wr": r1 + 48, "g_wi": r1 + 80, "rows": r1 + 112}


def _pack_small_grads(dmod_a, dmod_b, g_a, g_w31, dln1, dln2, df3, g_wr, g_wi, loss_part):
    bl = dmod_a.shape[0] // 8
    lay = _small_grad_layout(bl)
    ins = [dmod_a, dmod_b, g_a, g_w31, dln1, dln2, df3, g_wr, g_wi, loss_part]

    def body(ma_ref, mb_ref, ga_ref, gw_ref, l1_ref, l2_ref, f3_ref, wr_ref, wi_ref, ls_ref, o_ref):
        o_ref[...] = jnp.zeros_like(o_ref)
        for b in range(bl):
            o_ref[6 * b:6 * b + 3, :] = ma_ref[8 * b:8 * b + 3, :]
            o_ref[6 * b + 3:6 * b + 6, :] = mb_ref[8 * b:8 * b + 3, :]
        wide = lay["wide"]
        o_ref[wide:wide + 32, 0:CW] = gw_ref[...]
        o_ref[wide:wide + 16, CW:CW + LW] = ga_ref[...]
        o_ref[wide + 16:wide + 24, CW:CW + LANE] = ls_ref[...]
        o_ref[lay["ln"]:lay["ln"] + 2, :] = l1_ref[0:2, :]
        o_ref[lay["ln"] + 2:lay["ln"] + 4, :] = l2_ref[0:2, :]
        for j, c0 in enumerate(range(0, DFF, D)):
            w = min(D, DFF - c0)
            o_ref[lay["df3"] + 4 * j:lay["df3"] + 4 * j + 4, 0:w] = f3_ref[0:4, c0:c0 + w]
        o_ref[lay["g_wr"]:lay["g_wr"] + 32, :] = wr_ref[...]
        o_ref[lay["g_wi"]:lay["g_wi"] + 32, :] = wi_ref[...]

    return pl.pallas_call(
        body, name="pack_small_grads", out_shape=jax.ShapeDtypeStruct((lay["rows"], D), F32),
        in_specs=[_full(a.shape) for a in ins], out_specs=_full((lay["rows"], D)),
        compiler_params=pltpu.CompilerParams(vmem_limit_bytes=VMEM_LIMIT),
    )(*ins)


def _sum_small_grads(packed_all, bl):
    lay = _small_grad_layout(bl)
    shapes = {"lru_conv_w": (LRU_K, LW), "lru_conv_b": (1, LW), "lru_b_r": (1, LW), "lru_b_i": (1, LW),
              "lru_lambda": (1, LW), "conv_norm_g": (1, CW), "conv_norm_b": (1, CW), "conv_b": (1, CW),
              "conv_w": (CONV_K, CW), "ln1_g": (1, D), "ln1_b": (1, D), "ln2_g": (1, D), "ln2_b": (1, D),
              "ffn_conv_w": (FFN_K, DFF), "ffn_conv_b": (1, DFF), "lru_w_r": (32, D), "lru_w_i": (32, D),
              "loss": (8, LANE)}
    names = list(shapes)

    def body(p_ref, *outs):
        o = dict(zip(names, outs))

        def part(r0, nr, c0=0, nc=D):
            acc = p_ref[0, r0:r0 + nr, c0:c0 + nc]
            for j in range(1, NDEV):
                acc = acc + p_ref[j, r0:r0 + nr, c0:c0 + nc]
            return acc

        wide = lay["wide"]
        ga = part(wide, 16, CW, LW)
        o["lru_conv_w"][...] = ga[0:LRU_K]
        for row, k in ((4, "lru_conv_b"), (5, "lru_b_r"), (6, "lru_b_i"), (7, "lru_lambda"), (8, "conv_norm_g"),
                       (9, "conv_norm_b"), (10, "conv_b")):
            o[k][...] = ga[row:row + 1]
        o["conv_w"][...] = part(wide, 32, 0, CW)[0:CONV_K]
        o["loss"][...] = part(wide + 16, 8, CW, LANE)
        ln = part(lay["ln"], 4)
        o["ln1_g"][...], o["ln1_b"][...], o["ln2_g"][...], o["ln2_b"][...] = ln[0:1], ln[1:2], ln[2:3], ln[3:4]
        for j, c0 in enumerate(range(0, DFF, D)):
            w = min(D, DFF - c0)
            f3 = part(lay["df3"] + 4 * j, 4, 0, w)
            o["ffn_conv_w"][:, c0:c0 + w] = f3[0:FFN_K]
            o["ffn_conv_b"][:, c0:c0 + w] = f3[FFN_K:FFN_K + 1]
        o["lru_w_r"][...] = part(lay["g_wr"], 32)
        o["lru_w_i"][...] = part(lay["g_wi"], 32)

    res = pl.pallas_call(
        body, name="sum_small_grads", out_shape=[jax.ShapeDtypeStruct(shapes[k], F32) for k in names],
        in_specs=[_full(packed_all.shape)], out_specs=[_full(shapes[k]) for k in names],
        compiler_params=pltpu.CompilerParams(vmem_limit_bytes=VMEM_LIMIT),
    )(packed_all)
    return dict(zip(names, res))


def _adamw_update(w_ref, g_ref, m_ref, v_ref, d_ref, nm_ref, nv_ref):
    gv = g_ref[...]
    nm = ADAM_B1 * m_ref[...] + (1.0 - ADAM_B1) * gv
    nv = ADAM_B2 * v_ref[...] + (1.0 - ADAM_B2) * (gv * gv)
    m_hat = nm / (1.0 - ADAM_B1 ** ADAM_STEP)
    v_hat = nv / (1.0 - ADAM_B2 ** ADAM_STEP)
    d_ref[...] = -ADAM_LR * (m_hat / (jnp.sqrt(v_hat) + ADAM_EPS) + ADAM_WD * w_ref[...])
    nm_ref[...] = nm
    nv_ref[...] = nv


def _sum_adamw(slots, w, m, v, name):
    _, r, cdim = slots.shape
    tr = 32 if r % 32 == 0 else r

    def body(s_ref, w_ref, m_ref, v_ref, g_ref, d_ref, nm_ref, nv_ref):
        acc = s_ref[0].astype(F32)
        for j in range(1, NDEV):
            acc = acc + s_ref[j].astype(F32)
        g_ref[...] = acc
        _adamw_update(w_ref, g_ref, m_ref, v_ref, d_ref, nm_ref, nv_ref)

    spec = pl.BlockSpec((tr, cdim), lambda i: (i, 0))
    return pl.pallas_call(
        body, name=name, grid=(r // tr,), out_shape=[jax.ShapeDtypeStruct((r, cdim), F32)] * 4,
        in_specs=[pl.BlockSpec((NDEV, tr, cdim), lambda i: (0, i, 0))] + [spec] * 3, out_specs=[spec] * 4,
        compiler_params=_params(1),
    )(slots, w, m, v)


def _adamw_many(ws, gs, ms, vs, name):
    n = len(ws)

    def body(*refs):
        ins, outs = refs[:4 * n], refs[4 * n:]
        for k in range(n):
            _adamw_update(ins[k], ins[n + k], ins[2 * n + k], ins[3 * n + k], outs[k], outs[n + k], outs[2 * n + k])

    specs = [_full(w.shape) for w in ws]
    res = pl.pallas_call(
        body, name=name, out_shape=[jax.ShapeDtypeStruct(w.shape, F32) for w in ws] * 3,
        in_specs=specs * 4, out_specs=specs * 3,
        compiler_params=pltpu.CompilerParams(vmem_limit_bytes=VMEM_LIMIT),
    )(*ws, *gs, *ms, *vs)
    return res[:n], res[n:2 * n], res[2 * n:]


def _adamw(w, g, m, v, name):
    r, cdim = w.shape
    tr = 128 if r % 128 == 0 else r
    body = functools.partial(_adamw_update)
    spec = pl.BlockSpec((tr, cdim), lambda i: (i, 0))
    return pl.pallas_call(
        body, name=name, grid=(r // tr,), out_shape=[jax.ShapeDtypeStruct((r, cdim), F32)] * 3,
        in_specs=[spec] * 4, out_specs=[spec] * 3, compiler_params=_params(1),
    )(w, g, m, v)


def _pack(arrs, width=D, row_mult=8):
    parts = []
    for a in arrs:
        flat = a.reshape(-1)
        pad = (-flat.shape[0]) % width
        parts.append(jnp.pad(flat, (0, pad)))
    flat = jnp.concatenate(parts)
    pad = (-flat.shape[0]) % (width * row_mult)
    return jnp.pad(flat, (0, pad)).reshape(-1, width)


def _unpack(buf, shapes, width=D):
    out, row = [], 0
    for shp in shapes:
        size = math.prod(shp)
        nrow = -(-size // width)
        out.append(buf[row:row + nrow].reshape(-1)[:size].reshape(shp))
        row += nrow
    return out


def _unpack_gathered(buf, shapes, width=D):
    out, row = [], 0
    for shp in shapes:
        size = math.prod(shp)
        nrow = -(-size // width)
        out.append(buf[:, row:row + nrow].reshape(NDEV, -1)[:, :size].reshape((NDEV,) + tuple(shp)))
        row += nrow
    return out


def _block_diag(w):
    hn, dh, _ = w.shape
    eye = jnp.eye(hn, dtype=w.dtype)
    return (w[:, :, None, :] * eye[:, None, :, None]).reshape(hn * dh, hn * dh)


def _diag_blocks(wfull, hn):
    dh = wfull.shape[0] // hn
    keep = np.eye(hn, dtype=bool)[:, None, :, None]
    return jnp.where(keep, wfull.reshape(hn, dh, hn, dh), 0.0).sum(axis=2)


def _ungather_cols(g, k):
    n = g.shape[1]
    return g.reshape(NDEV, k, n).transpose(1, 0, 2).reshape(k, NDEV * n)


def _pad_rows(a, rows):
    return jnp.pad(a, ((0, rows - a.shape[0]), (0, 0)))


def kernel(x, c, w_ada, b_ada, w_in, lru_conv_w, lru_conv_b, lru_w_r, lru_b_r, lru_w_i, lru_b_i, lru_lambda, conv_w, conv_b, conv_norm_g, conv_norm_b, w_out, ln1_g, ln1_b, ffn_w_up, ffn_conv_w, ffn_conv_b, ffn_w_down, ln2_g, ln2_b, loss_target, m_w_ada, m_b_ada, m_w_in, m_lru_conv_w, m_lru_conv_b, m_lru_w_r, m_lru_b_r, m_lru_w_i, m_lru_b_i, m_lru_lambda, m_conv_w, m_conv_b, m_conv_norm_g, m_conv_norm_b, m_w_out, m_ln1_g, m_ln1_b, m_ffn_w_up, m_ffn_conv_w, m_ffn_conv_b, m_ffn_w_down, m_ln2_g, m_ln2_b, v_w_ada, v_b_ada, v_w_in, v_lru_conv_w, v_lru_conv_b, v_lru_w_r, v_lru_b_r, v_lru_w_i, v_lru_b_i, v_lru_lambda, v_conv_w, v_conv_b, v_conv_norm_g, v_conv_norm_b, v_w_out, v_ln1_g, v_ln1_b, v_ffn_w_up, v_ffn_conv_w, v_ffn_conv_b, v_ffn_w_down, v_ln2_g, v_ln2_b):
    weights = dict(w_ada=w_ada, b_ada=b_ada, w_in=w_in, lru_conv_w=lru_conv_w, lru_conv_b=lru_conv_b, lru_w_r=lru_w_r,
                   lru_b_r=lru_b_r, lru_w_i=lru_w_i, lru_b_i=lru_b_i, lru_lambda=lru_lambda, conv_w=conv_w, conv_b=conv_b,
                   conv_norm_g=conv_norm_g, conv_norm_b=conv_norm_b, w_out=w_out, ln1_g=ln1_g, ln1_b=ln1_b,
                   ffn_w_up=ffn_w_up, ffn_conv_w=ffn_conv_w, ffn_conv_b=ffn_conv_b, ffn_w_down=ffn_w_down, ln2_g=ln2_g,
                   ln2_b=ln2_b)
    mom_m = dict(w_ada=m_w_ada, b_ada=m_b_ada, w_in=m_w_in, lru_conv_w=m_lru_conv_w, lru_conv_b=m_lru_conv_b,
                 lru_w_r=m_lru_w_r, lru_b_r=m_lru_b_r, lru_w_i=m_lru_w_i, lru_b_i=m_lru_b_i, lru_lambda=m_lru_lambda,
                 conv_w=m_conv_w, conv_b=m_conv_b, conv_norm_g=m_conv_norm_g, conv_norm_b=m_conv_norm_b, w_out=m_w_out,
                 ln1_g=m_ln1_g, ln1_b=m_ln1_b, ffn_w_up=m_ffn_w_up, ffn_conv_w=m_ffn_conv_w, ffn_conv_b=m_ffn_conv_b,
                 ffn_w_down=m_ffn_w_down, ln2_g=m_ln2_g, ln2_b=m_ln2_b)
    mom_v = dict(w_ada=v_w_ada, b_ada=v_b_ada, w_in=v_w_in, lru_conv_w=v_lru_conv_w, lru_conv_b=v_lru_conv_b,
                 lru_w_r=v_lru_w_r, lru_b_r=v_lru_b_r, lru_w_i=v_lru_w_i, lru_b_i=v_lru_b_i, lru_lambda=v_lru_lambda,
                 conv_w=v_conv_w, conv_b=v_conv_b, conv_norm_g=v_conv_norm_g, conv_norm_b=v_conv_norm_b, w_out=v_w_out,
                 ln1_g=v_ln1_g, ln1_b=v_ln1_b, ffn_w_up=v_ffn_w_up, ffn_conv_w=v_ffn_conv_w, ffn_conv_b=v_ffn_conv_b,
                 ffn_w_down=v_ffn_w_down, ln2_g=v_ln2_g, ln2_b=v_ln2_b)
    names = list(weights)
    bl, seq, _ = x.shape
    ntok = bl * seq
    me = 4 * lax.axis_index("x") + 2 * lax.axis_index("y") + lax.axis_index("c")

    small_shapes = [(bl, D), (LRU_K, LW // NDEV), (CONV_K, CW // NDEV), (FFN_K, DFF // NDEV)]
    small = _pack([c, lru_conv_w[0], conv_w[0], ffn_conv_w[0]], width=128)
    n_small = small.shape[0]
    small_all, win_t, wout_b = _all_gather(
        [small, w_in[0].T.astype(MXU_DTYPE), w_out[0].astype(MXU_DTYPE)], "gather_small_and_mixer_weights")
    small_all = small_all.reshape(NDEV, n_small, 128)
    c_parts, k4_parts, k31_parts, k3_parts = _unpack_gathered(small_all, small_shapes, width=128)
    c_all = c_parts.reshape(NDEV * bl, D)
    lru_conv_w_f = k4_parts.transpose(1, 0, 2).reshape(LRU_K, LW)
    conv_w_f = k31_parts.transpose(1, 0, 2).reshape(CONV_K, CW)
    ffn_conv_w_f = k3_parts.transpose(1, 0, 2).reshape(FFN_K, DFF)

    ncol = w_ada.shape[2]
    b_ada_loc = lax.dynamic_slice(b_ada, (0, me * ncol), (1, ncol))
    mod_cols = _ada_fwd(c_all, w_ada[0], b_ada_loc)
    (mod_all,) = _all_gather([mod_cols], "gather_mod")
    mod_all = mod_all.reshape(NDEV, NDEV * bl, ncol)
    mod_mine = lax.dynamic_slice(mod_all, (0, me * bl, 0), (NDEV, bl, ncol))
    mods = mod_mine.transpose(1, 0, 2).reshape(bl, 6, D)
    mods = jnp.pad(mods, ((0, 0), (0, 2), (0, 0)))

    ffn_shards = [ffn_w_up[0].T.astype(MXU_DTYPE), ffn_w_down[0].astype(MXU_DTYPE)]

    wr_bd = _block_diag(lru_w_r[0]).astype(MXU_DTYPE)
    wi_bd = _block_diag(lru_w_i[0]).astype(MXU_DTYPE)
    mavg_np = np.kron(np.eye(CW // HEAD, dtype=np.float32), np.full((HEAD, HEAD), 1.0 / HEAD, np.float32))
    mavg = jnp.asarray(mavg_np, MXU_DTYPE)
    mavg2 = jnp.asarray(np.concatenate([mavg_np, mavg_np], axis=0), MXU_DTYPE)
    wri = jnp.concatenate([wr_bd, wi_bd], axis=1)
    va = jnp.concatenate([lru_conv_w_f, lru_conv_b, lru_b_r, lru_b_i, lru_lambda], axis=0)
    vb = _pad_rows(jnp.concatenate([conv_w_f, conv_b, conv_norm_g, conv_norm_b], axis=0), 40)
    ln1 = _pad_rows(jnp.concatenate([ln1_g, ln1_b], axis=0), 8)
    ln2 = _pad_rows(jnp.concatenate([ln2_g, ln2_b], axis=0), 8)
    f3 = _pad_rows(jnp.concatenate([ffn_conv_w_f, ffn_conv_b], axis=0), 8)

    proj, h, vc, y_b, mixed, x1, u1_b, wup_t, wdown_b = _mixer_fwd(
        x, mods, win_t, wout_b, wri, mavg2, va, vb, ln1, ffn_shards)
    hh, z_b, u2_b, y2, loss_part = _ffn_fwd(x1, mods, loss_target, wup_t, wdown_b, f3, ln2)

    dx1, dy2_b, dhh_b, dln2, df3, dmod_b = _ffn_bwd(x1, y2, loss_target, hh, mods, wup_t, wdown_b, f3, ln2)
    g_down = _weight_grad(z_b.reshape(ntok, DFF), dy2_b.reshape(ntok, D), FF_CHUNK, "wgrad_down")
    g_up_t = _weight_grad(dhh_b.reshape(ntok, 2 * DFF), u2_b.reshape(ntok, D), FF_CHUNK, "wgrad_up")
    grad_x, dproj_b, dm_b, g_a, g_w31, g_wr, g_wi, dln1, dmod_a, s_up, s_down = _mixer_bwd(
        x, dx1, proj, h, vc, mixed, mods, win_t, wout_b, wri, mavg2, va, vb, ln1, [g_up_t, g_down])
    dproj_2d, u1_2d = dproj_b.reshape(ntok, 2 * LW + 2 * CW), u1_b.reshape(ntok, D)
    blk_in = (2 * LW + 2 * CW) // NDEV
    g_out_b = _weight_grad(y_b.reshape(ntok, D), dm_b.reshape(ntok, D), D, "wgrad_out", out_dtype=WIRE_DTYPE)
    g_in_c0, s_out = _weight_grad(dproj_2d, u1_2d, blk_in, "wgrad_in_c0", out_dtype=WIRE_DTYPE, exchange=[g_out_b],
                                  col_phase=0, tok_tile=2 * WG_TOK_TILE)
    g_in_c1, s_in_c0 = _weight_grad(dproj_2d, u1_2d, blk_in, "wgrad_in_c1", out_dtype=WIRE_DTYPE, exchange=[g_in_c0],
                                    exchange_half=0, col_phase=1, tok_tile=2 * WG_TOK_TILE)

    heads = LW // HEAD
    packed = _pack_small_grads(
        dmod_a.reshape(bl * 8, D), dmod_b.reshape(bl * 8, D), g_a, g_w31, dln1, dln2, df3,
        _diag_blocks(g_wr, heads).reshape(32, D), _diag_blocks(g_wi, heads).reshape(32, D), loss_part)
    n_rows = packed.shape[0]
    (s_in_c1,), (packed_all,) = _exchange_and_gather(
        [g_in_c1], [packed], "exchange_w_in_grads_gather_small_grads", half=1)
    packed_all = packed_all.reshape(NDEV, n_rows, D)
    full = _sum_small_grads(packed_all, bl)
    loss = jnp.sum(full.pop("loss")) * (0.5 / D)
    dmod_all = packed_all[:, 0:6 * bl].reshape(NDEV * bl, 6 * D)

    grads, delta, new_m, new_v = {}, {}, {}, {}
    for k, slots in (("w_out", s_out), ("ffn_w_down", s_down)):
        g_, d_, m_, v_ = _sum_adamw(slots, weights[k][0], mom_m[k][0], mom_v[k][0], "sum_adamw_" + k)
        grads[k], delta[k], new_m[k], new_v[k] = g_[None], d_[None], m_[None], v_[None]
    g_w_ada, g_b_ada = _ada_bwd(c_all, dmod_all, lax.dynamic_slice(dmod_all, (0, me * ncol), (NDEV * bl, ncol)))
    grads["w_ada"], grads["b_ada"] = g_w_ada[None], g_b_ada
    grads["w_in"] = _sum_slots(s_in_c0, "sum_w_in", slots_c1=s_in_c1).T[None]
    grads["ffn_w_up"] = _sum_slots(s_up, "sum_w_up").T[None]
    for k, gk in full.items():
        if k in ("lru_conv_w", "conv_w", "ffn_conv_w"):
            nloc = gk.shape[1] // NDEV
            gk = lax.dynamic_slice(gk, (0, me * nloc), (gk.shape[0], nloc))
        grads[k] = gk.reshape(weights[k].shape)

    big = ("w_ada", "w_in", "w_out", "ffn_w_up", "ffn_w_down")
    for k in ("w_ada", "w_in", "ffn_w_up"):
        d_, m_, v_ = _adamw(weights[k][0], grads[k][0], mom_m[k][0], mom_v[k][0], "adamw_" + k)
        delta[k], new_m[k], new_v[k] = d_[None], m_[None], v_[None]
    small_names = [k for k in names if k not in big]
    d_, m_, v_ = _adamw_many([weights[k] for k in small_names], [grads[k] for k in small_names],
                             [mom_m[k] for k in small_names], [mom_v[k] for k in small_names], "adamw_small")
    for k, dk, mk, vk in zip(small_names, d_, m_, v_):
        delta[k], new_m[k], new_v[k] = dk, mk, vk

    return (loss, grad_x, *[grads[k] for k in names], *[delta[k] for k in names], *[new_m[k] for k in names],
            *[new_v[k] for k in names])
```

```python
import functools
import math

import jax
import jax.numpy as jnp
import numpy as np
from jax import lax
from jax.experimental import pallas as pl
from jax.experimental.pallas import tpu as pltpu

NDEV = 8
D = 1024
LW = 512
CW = 512
HEAD = 64
DFF = 2816
FF_CHUNK = 1408
LRU_K = 4
CONV_K = 31
FFN_K = 3
LRU_C = 8.0
ALPHA = (2 * 1) ** 0.25
LN_EPS = 1e-5
ADAM_LR = 0.001
ADAM_B1 = 0.9
ADAM_B2 = 0.999
ADAM_EPS = 1e-08
ADAM_WD = 0.01
ADAM_STEP = 10

MXU_DTYPE = jnp.bfloat16
WIRE_DTYPE = jnp.bfloat16
TOK_TILE = 256
WG_TOK_TILE = 1024
VMEM_LIMIT = 60 * 1024 * 1024
HALO31 = 32
HALO = 8

F32 = jnp.float32
MESH = pl.DeviceIdType.MESH


def _sigmoid(x):
    return 0.5 * jnp.tanh(0.5 * x) + 0.5


def _shifted_rows(ext_ref, rot_ref, n_rows):
    span = rot_ref.shape[1]
    for r in range(1, 8):
        rot_ref[r - 1] = ext_ref[pl.ds(r, span), :]

    def window(o, r0=0, cols=slice(None)):
        q, r = divmod(o, 8)
        if r == 0:
            return ext_ref[pl.ds(8 * q + r0, n_rows), cols]
        return rot_ref[r - 1, pl.ds(8 * q + r0, n_rows), cols]

    return window


def _store_hilo(hilo_ref, rows, c0, v):
    half = hilo_ref.shape[1] // 2
    hi = v.astype(hilo_ref.dtype)
    hilo_ref[rows, c0:c0 + v.shape[1]] = hi
    hilo_ref[rows, half + c0:half + c0 + v.shape[1]] = (v - hi.astype(F32)).astype(hilo_ref.dtype)


def _scan_bwd_blocks(a_ref, g_ref, car, n_rows):
    row = lax.broadcasted_iota(jnp.int32, (8, LANE), 0)
    for c0 in range(0, a_ref.shape[1], LANE):
        cols = slice(c0, c0 + LANE)
        a_next = jnp.broadcast_to(car[0:1, cols], (8, LANE))
        g_next = jnp.broadcast_to(car[1:2, cols], (8, LANE))
        for r0 in range(n_rows - 8, -1, -8):
            a_blk, g = a_ref[r0:r0 + 8, cols], g_ref[r0:r0 + 8, cols]
            bm = jnp.where(row < 7, pltpu.roll(a_blk, 7, 0), a_next)
            for d in (1, 2, 4):
                keep = row < 8 - d
                g = g + bm * jnp.where(keep, pltpu.roll(g, 8 - d, 0), 0.0)
                bm = bm * jnp.where(keep, pltpu.roll(bm, 8 - d, 0), 1.0)
            g = g + bm * g_next
            g_ref[r0:r0 + 8, cols] = g
            a_next = jnp.broadcast_to(a_blk[0:1, :], (8, LANE))
            g_next = jnp.broadcast_to(g[0:1, :], (8, LANE))
        car[0:1, cols] = a_next[0:1, :]
        car[1:2, cols] = g_next[0:1, :]


def _scan_fwd_blocks(a_ref, u_ref, hcar, n_rows):
    row = lax.broadcasted_iota(jnp.int32, (8, LANE), 0)
    for c0 in range(0, a_ref.shape[1], LANE):
        cols = slice(c0, c0 + LANE)
        h_prev = jnp.broadcast_to(hcar[0:1, cols], (8, LANE))
        for r0 in range(0, n_rows, 8):
            a, u = a_ref[r0:r0 + 8, cols], u_ref[r0:r0 + 8, cols]
            for d in (1, 2, 4):
                keep = row >= d
                u = a * jnp.where(keep, pltpu.roll(u, d, 0), 0.0) + u
                a = a * jnp.where(keep, pltpu.roll(a, d, 0), 1.0)
            u = u + a * h_prev
            u_ref[r0:r0 + 8, cols] = u
            h_prev = jnp.broadcast_to(u[7:8, :], (8, LANE))
        hcar[0:1, cols] = h_prev[0:1, :]


def _dot(a, b):
    return jnp.dot(a, b, preferred_element_type=F32)


def _dot_nt(a, b):
    return lax.dot_general(a, b, (((1,), (1,)), ((), ())), preferred_element_type=F32)


def _dot_tn(a, b):
    return lax.dot_general(a, b, (((0,), (0,)), ((), ())), preferred_element_type=F32)


def _colsum(v):
    return jnp.sum(v, axis=0, keepdims=True)


def _rowmean(v):
    return jnp.mean(v, axis=-1, keepdims=True)


_GELU_C0 = math.sqrt(2.0 / math.pi)
_GELU_C1 = 0.044715


def _gelu_and_grad(x):
    x2 = x * x
    th = jnp.tanh(_GELU_C0 * (x + _GELU_C1 * x * x2))
    ge = 0.5 * x * (1.0 + th)
    dge = 0.5 * (1.0 + th) + 0.5 * x * (1.0 - th * th) * (_GELU_C0 * (1.0 + 3.0 * _GELU_C1 * x2))
    return ge, dge


def _softplus(x):
    return jnp.maximum(x, 0.0) + jnp.log1p(jnp.exp(-jnp.abs(x)))


def _layer_norm_stats(r):
    mu = _rowmean(r)
    dl = r - mu
    var = _rowmean(dl * dl)
    rstd = lax.rsqrt(var + LN_EPS)
    return dl * rstd, rstd


ROW_BLK = 32
LANE = 128


def _row_blocks(n_rows, rb, body, init):
    carry = init
    for j in range(n_rows // rb):
        carry = body(j * rb, carry)
    return carry


def _fold8(v):
    r, c = v.shape
    return v if r == 8 else v.reshape(r // 8, 8, c).sum(axis=0)


def _cast_rows(src_ref, dst_ref, n_rows):
    rb = min(16, n_rows)

    def rows(r0, carry):
        dst_ref[pl.ds(r0, rb), :] = src_ref[pl.ds(r0, rb), :].astype(dst_ref.dtype)
        return carry

    _row_blocks(n_rows, rb, rows, ())


def _full(shape):
    nd = len(shape)
    return pl.BlockSpec(shape, lambda *_: (0,) * nd)


_ANY = pl.BlockSpec(memory_space=pl.ANY)


def _params(n_grid):
    return pltpu.CompilerParams(dimension_semantics=("arbitrary",) * n_grid, vmem_limit_bytes=VMEM_LIMIT)


def _my_place():
    return lax.axis_index("x"), lax.axis_index("y"), lax.axis_index("c")


def _all_gather(arrays, name):
    n_arr = len(arrays)

    def body(*refs):
        start, forward, finish = _gather_steps(refs[:n_arr], refs[n_arr:2 * n_arr], *refs[2 * n_arr:])
        start()
        forward()
        finish()

    return pl.pallas_call(
        body, name=name, out_shape=_gather_out_shapes(arrays),
        in_specs=[_ANY] * n_arr, out_specs=[_ANY] * n_arr, scratch_shapes=_comm_sems(n_arr),
    )(*arrays)


def _gather_out_shapes(arrays):
    return [jax.ShapeDtypeStruct((NDEV * a.shape[0], a.shape[1]), a.dtype) for a in arrays]


def _comm_sems(n_arr):
    return [pltpu.SemaphoreType.DMA((n_arr, 7)), pltpu.SemaphoreType.DMA((n_arr, 7)), pltpu.SemaphoreType.DMA((n_arr,))]


def _gather_steps(x_refs, out_refs, send_sems, recv_sems, local_sems):
    n_arr = len(x_refs)
    x, y, c = _my_place()
    me, sibling = (x, y, c), (x, y, 1 - c)
    chips = [(1 - x, y), (x, 1 - y), (1 - x, 1 - y)]

    def rows(k, px, py, pc):
        m = x_refs[k].shape[0]
        return out_refs[k].at[pl.ds((4 * px + 2 * py + pc) * m, m), :]

    def copy(k, s, block, to, src=None):
        return pltpu.make_async_remote_copy(
            src_ref=rows(k, *block) if src is None else src, dst_ref=rows(k, *block),
            send_sem=send_sems.at[k, s], recv_sem=recv_sems.at[k, s], device_id=to, device_id_type=MESH)

    def mine():
        return [pltpu.make_async_copy(x_refs[k], rows(k, *me), local_sems.at[k]) for k in range(n_arr)]

    def first():
        cps = []
        for k in range(n_arr):
            cps.append(copy(k, 0, me, sibling, src=x_refs[k]))
            cps += [copy(k, 1 + j, me, (*chip, c), src=x_refs[k]) for j, chip in enumerate(chips)]
        return cps

    def passed():
        return [copy(k, 4 + j, (*chip, c), sibling) for k in range(n_arr) for j, chip in enumerate(chips)]

    def start():
        for cp in mine() + first():
            cp.start()

    def forward():
        fwd = passed()
        for k in range(n_arr):
            for j, chip in enumerate(chips):
                copy(k, 1 + j, (*chip, c), me).wait_recv()
                fwd[3 * k + j].start()

    def finish():
        for k in range(n_arr):
            copy(k, 0, sibling, me).wait_recv()
            for j, chip in enumerate(chips):
                copy(k, 4 + j, (*chip, 1 - c), me).wait_recv()
        for cp in first() + passed():
            cp.wait_send()
        for cp in mine():
            cp.wait()

    return start, forward, finish


def _exchange_and_gather(ex_arrays, ga_arrays, name, half=None):
    n_e, n_g = len(ex_arrays), len(ga_arrays)

    def body(*refs):
        e_in, g_in, refs = refs[:n_e], refs[n_e:n_e + n_g], refs[n_e + n_g:]
        e_out, g_out, sems = refs[:n_e], refs[n_e:n_e + n_g], refs[n_e + n_g:]
        e_start, e_finish = _exchange_steps(e_in, e_out, *sems[:3], half=half)
        g_start, g_forward, g_finish = _gather_steps(g_in, g_out, *sems[3:])
        g_start()
        e_start()
        g_forward()
        g_finish()
        e_finish()

    res = pl.pallas_call(
        body, name=name, out_shape=_exchange_out_shapes(ex_arrays, half) + _gather_out_shapes(ga_arrays),
        in_specs=[_ANY] * (n_e + n_g), out_specs=[_ANY] * (n_e + n_g),
        scratch_shapes=_comm_sems(n_e) + _comm_sems(n_g),
    )(*ex_arrays, *ga_arrays)
    return res[:n_e], res[n_e:]


def _exchange_out_shapes(arrays, half=None):
    n_blocks = NDEV if half is None else NDEV // 2
    return [jax.ShapeDtypeStruct((NDEV, a.shape[0] // n_blocks, a.shape[1]), a.dtype) for a in arrays]


def _exchange_steps(g_refs, out_refs, send_sems, recv_sems, local_sems, half=None):
    n_arr = len(g_refs)
    x, y, c = _my_place()
    me = 4 * x + 2 * y + c
    receives = True if half is None else c == half

    def when(cond, fn):
        if cond is True:
            fn()
        else:
            pl.when(cond)(fn)

    def sends(rel):
        return True if half is None else (c ^ (rel & 1)) == half

    def copy(k, rel):
        px, py, pc = x ^ ((rel >> 2) & 1), y ^ ((rel >> 1) & 1), c ^ (rel & 1)
        m = out_refs[k].shape[1]
        blk = 4 * px + 2 * py + pc if half is None else 2 * px + py
        return pltpu.make_async_remote_copy(
            src_ref=g_refs[k].at[pl.ds(blk * m, m), :], dst_ref=out_refs[k].at[me],
            send_sem=send_sems.at[k, rel - 1], recv_sem=recv_sems.at[k, rel - 1],
            device_id=(px, py, pc), device_id_type=MESH)

    def local(k):
        m = out_refs[k].shape[1]
        blk = me if half is None else 2 * x + y
        return pltpu.make_async_copy(g_refs[k].at[pl.ds(blk * m, m), :], out_refs[k].at[me], local_sems.at[k])

    def start():
        for k in range(n_arr):
            when(receives, local(k).start)
            for rel in range(1, NDEV):
                when(sends(rel), copy(k, rel).start)

    def finish():
        for k in range(n_arr):
            for rel in range(1, NDEV):
                when(sends(rel), copy(k, rel).wait_send)
                when(receives, copy(k, rel).wait_recv)
            when(receives, local(k).wait)

    return start, finish


def _ada_fwd(c_all, w_ada_loc, b_ada_loc):
    def body(c_ref, w_ref, b_ref, o_ref):
        cv = c_ref[...]
        ca = (cv * _sigmoid(cv)).astype(MXU_DTYPE)
        o_ref[...] = _dot(ca, w_ref[...].astype(MXU_DTYPE)) + b_ref[...]

    return pl.pallas_call(
        body, name="ada_fwd", out_shape=jax.ShapeDtypeStruct((c_all.shape[0], w_ada_loc.shape[1]), F32),
        in_specs=[_full(c_all.shape), _full(w_ada_loc.shape), _full(b_ada_loc.shape)],
        out_specs=_full((c_all.shape[0], w_ada_loc.shape[1])),
        compiler_params=pltpu.CompilerParams(vmem_limit_bytes=VMEM_LIMIT),
    )(c_all, w_ada_loc, b_ada_loc)


def _ada_bwd(c_all, dmod_all, dmod_cols):
    def body(c_ref, da_ref, d_ref, o_ref, b_ref):
        cv = c_ref[...]
        ca = (cv * _sigmoid(cv)).astype(MXU_DTYPE)
        o_ref[...] = _dot_tn(ca, d_ref[...].astype(MXU_DTYPE))
        b_ref[...] = _colsum(da_ref[...])

    return pl.pallas_call(
        body, name="ada_bwd",
        out_shape=[jax.ShapeDtypeStruct((c_all.shape[1], dmod_cols.shape[1]), F32),
                   jax.ShapeDtypeStruct((1, dmod_all.shape[1]), F32)],
        in_specs=[_full(c_all.shape), _full(dmod_all.shape), _full(dmod_cols.shape)],
        out_specs=[_full((c_all.shape[1], dmod_cols.shape[1])), _full((1, dmod_all.shape[1]))],
        compiler_params=pltpu.CompilerParams(vmem_limit_bytes=VMEM_LIMIT),
    )(c_all, dmod_all, dmod_cols)


def _mixer_fwd(x, mods, win_t, wout, wri, mavg2, va, vb, ln1, ffn_shards):
    bl, seq, _ = x.shape
    t = min(TOK_TILE, seq)
    nt = seq // t
    n_g = len(ffn_shards)
    rb = min(ROW_BLK, t)

    def body(x_ref, mod_ref, win_hbm, wout_hbm, wri_ref, mavg_ref, va_ref, vb_ref, ln1_ref, *rest):
        shard_refs, rest = rest[:n_g], rest[n_g:]
        proj_ref, h_ref, vc_ref, y_ref, mixed_ref, x1_ref, u1_ref = rest[:7]
        gathered_refs, rest = rest[7:7 + n_g], rest[7 + n_g:]
        (win_v, wout_v, xa_ext, vg_ext, rot_a, hcar, sems, rot_c, xc_scr, a_scr, st_a, st_b, xcb_scr, gates, hilo,
         g_send, g_recv, g_local) = rest
        b, i = pl.program_id(0), pl.program_id(1)
        step = b * nt + i
        g_start, g_forward, g_finish = _gather_steps(shard_refs, gathered_refs, g_send, g_recv, g_local)

        @pl.when(step == 0)
        def _():
            g_start()
            cps = [pltpu.make_async_copy(win_hbm, win_v, sems.at[0]), pltpu.make_async_copy(wout_hbm, wout_v, sems.at[1])]
            for cp in cps:
                cp.start()
            for cp in cps:
                cp.wait()

        pl.when(step == (bl * nt) // 2)(g_forward)

        @pl.when(i == 0)
        def _():
            xa_ext[0:HALO, :] = jnp.zeros((HALO, LW), F32)
            vg_ext[0:HALO31, :] = jnp.zeros((HALO31, CW), F32)
            hcar[...] = jnp.zeros_like(hcar)

        blocks = [(r0, c0) for c0 in range(0, LW, LANE) for r0 in range(0, t, rb)]

        def in_rows(r0, acc):
            rows = pl.ds(r0, 16)
            u1_ref[rows, :] = (x_ref[rows, :] * (1.0 + mod_ref[1:2, :]) + mod_ref[0:1, :]).astype(MXU_DTYPE)
            return acc

        _row_blocks(t, min(16, t), in_rows, ())
        proj_ref[...] = _dot_nt(u1_ref[...], win_v[...])

        xa_ext[HALO:HALO + t, :] = proj_ref[:, 0:LW]
        for k in range(LRU_K - 1):
            rot_c[k] = xa_ext[pl.ds(HALO - (LRU_K - 1) + k, t), :]
        for r0, c0 in blocks:
            rows, cols = pl.ds(r0, rb), slice(c0, c0 + LANE)
            xc = va_ref[4:5, cols] + va_ref[LRU_K - 1:LRU_K, cols] * xa_ext[pl.ds(HALO + r0, rb), cols]
            for k in range(LRU_K - 1):
                xc = xc + va_ref[k:k + 1, cols] * rot_c[k, rows, cols]
            xc_scr[rows, cols] = xc
            xcb_scr[rows, cols] = xc.astype(MXU_DTYPE)
        xa_ext[0:HALO, :] = xa_ext[t:t + HALO, :]
        gates[...] = _dot(xcb_scr[...], wri_ref[...])
        for r0, c0 in blocks:
            rows, cols = pl.ds(r0, rb), slice(c0, c0 + LANE)
            r = _sigmoid(gates[rows, cols] + va_ref[5:6, cols])
            ig = _sigmoid(gates[rows, LW + c0:LW + c0 + LANE] + va_ref[6:7, cols])
            la = (-LRU_C) * r * _softplus(-va_ref[7:8, cols])
            a = jnp.exp(la)
            a_scr[rows, cols] = a
            h_ref[rows, cols] = jnp.sqrt(-jnp.tanh(la) * (a * a + 1.0)) * (ig * xc_scr[rows, cols])
        _scan_fwd_blocks(a_scr, h_ref, hcar, t)

        for r0, c0 in blocks:
            rows, cols, pc = pl.ds(r0, rb), slice(c0, c0 + LANE), 2 * LW + c0
            vg_ext[pl.ds(HALO31 + r0, rb), cols] = (
                proj_ref[rows, pc:pc + LANE] * _sigmoid(proj_ref[rows, pc + CW:pc + CW + LANE]))
        vg_win = _shifted_rows(vg_ext, rot_a, rb)
        for r0, c0 in blocks:
            rows, cols = pl.ds(r0, rb), slice(c0, c0 + LANE)
            vc = vb_ref[CONV_K:CONV_K + 1, cols]
            for k in range(CONV_K):
                vc = vc + vb_ref[k:k + 1, cols] * vg_win(HALO31 - (CONV_K - 1) + k, r0, cols)
            vc_ref[rows, cols] = vc
            _store_hilo(hilo, rows, c0, vc)
        vg_ext[0:HALO31, :] = vg_ext[t:t + HALO31, :]
        mavg2 = mavg_ref[...]
        st_a[...] = _dot(hilo[...], mavg2)
        for r0, c0 in blocks:
            rows, cols = pl.ds(r0, rb), slice(c0, c0 + LANE)
            dl = vc_ref[rows, cols] - st_a[rows, cols]
            st_a[rows, cols] = dl
            _store_hilo(hilo, rows, c0, dl * dl)
        st_b[...] = _dot(hilo[...], mavg2)
        for r0, c0 in blocks:
            rows, cols = pl.ds(r0, rb), slice(c0, c0 + LANE)
            ge, _ = _gelu_and_grad(proj_ref[rows, LW + c0:LW + c0 + LANE])
            y_ref[rows, cols] = (ge * h_ref[rows, cols]).astype(MXU_DTYPE)
            yl = (st_a[rows, cols] * lax.rsqrt(st_b[rows, cols] + LN_EPS) * vb_ref[CONV_K + 1:CONV_K + 2, cols]
                  + vb_ref[CONV_K + 2:CONV_K + 3, cols])
            y_ref[rows, LW + c0:LW + c0 + LANE] = (yl * _sigmoid(yl)).astype(MXU_DTYPE)
        mixed_ref[...] = _dot(y_ref[...], wout_v[...])

        def out_rows(r0, acc):
            rows = pl.ds(r0, 8)
            xh, _ = _layer_norm_stats(ALPHA * x_ref[rows, :] + (1.0 + mod_ref[2:3, :]) * mixed_ref[rows, :])
            x1_ref[rows, :] = xh * ln1_ref[0:1, :] + ln1_ref[1:2, :]
            return acc

        _row_blocks(t, 8, out_rows, ())

        pl.when(step == bl * nt - 1)(g_finish)

    tok = lambda w: pl.BlockSpec((None, t, w), lambda b, i: (b, i, 0))
    outs = [(2 * LW + 2 * CW, F32), (LW, F32), (CW, F32), (D, MXU_DTYPE), (D, F32), (D, F32), (D, MXU_DTYPE)]
    return pl.pallas_call(
        body, name="mixer_fwd", grid=(bl, nt),
        out_shape=[jax.ShapeDtypeStruct((bl, seq, w), dt) for w, dt in outs] + _gather_out_shapes(ffn_shards),
        in_specs=[tok(D), pl.BlockSpec((None, 8, D), lambda b, i: (b, 0, 0)), _ANY, _ANY,
                  _full(wri.shape), _full(mavg2.shape), _full(va.shape), _full(vb.shape), _full(ln1.shape)]
        + [_ANY] * n_g,
        out_specs=[tok(w) for w, _ in outs] + [_ANY] * n_g,
        scratch_shapes=[pltpu.VMEM(win_t.shape, MXU_DTYPE), pltpu.VMEM(wout.shape, MXU_DTYPE),
                        pltpu.VMEM((t + HALO, LW), F32), pltpu.VMEM((t + HALO31, CW), F32),
                        pltpu.VMEM((7, t + HALO31 - 8, CW), F32), pltpu.VMEM((8, LW), F32),
                        pltpu.SemaphoreType.DMA((2,)), pltpu.VMEM((LRU_K - 1, t, LW), F32)]
        + [pltpu.VMEM((t, LW), F32)] * 4
        + [pltpu.VMEM((t, LW), MXU_DTYPE), pltpu.VMEM((t, 2 * LW), F32), pltpu.VMEM((t, 2 * CW), MXU_DTYPE)]
        + _comm_sems(n_g),
        compiler_params=_params(2),
    )(x, mods, win_t, wout, wri, mavg2, va, vb, ln1, *ffn_shards)


def _ffn_fwd(x1, mods, tgt, wup_t, wdown, f3, ln2):
    bl, seq, _ = x1.shape
    t = min(TOK_TILE, seq)
    nt = seq // t
    n_chunk = DFF // FF_CHUNK

    rb = min(ROW_BLK, t)

    def body(x1_ref, mod_ref, tgt_ref, wup_hbm, wdown_hbm, f3_ref, ln2_ref,
             hh_ref, z_ref, u2_ref, y2_ref, loss_ref,
             wup_v, wdown_v, g_ext, rot, sems):
        b, i = pl.program_id(0), pl.program_id(1)

        @pl.when((b == 0) & (i == 0))
        def _():
            cps = [pltpu.make_async_copy(wup_hbm, wup_v, sems.at[0]), pltpu.make_async_copy(wdown_hbm, wdown_v, sems.at[1])]
            for cp in cps:
                cp.start()
            for cp in cps:
                cp.wait()
            loss_ref[...] = jnp.zeros_like(loss_ref)

        @pl.when(i == 0)
        def _():
            for ch in range(n_chunk):
                g_ext[ch, 0:HALO, :] = jnp.zeros((HALO, FF_CHUNK), F32)

        def in_rows(r0, acc):
            rows = pl.ds(r0, 16)
            u2_ref[rows, :] = (x1_ref[rows, :] * (1.0 + mod_ref[4:5, :]) + mod_ref[3:4, :]).astype(MXU_DTYPE)
            return acc

        _row_blocks(t, min(16, t), in_rows, ())
        for ch in range(n_chunk):
            lo = ch * FF_CHUNK
            hh_ref[:, lo:lo + FF_CHUNK] = _dot_nt(u2_ref[...], wup_v[lo:lo + FF_CHUNK, :])
            hh_ref[:, DFF + lo:DFF + lo + FF_CHUNK] = _dot_nt(u2_ref[...], wup_v[DFF + lo:DFF + lo + FF_CHUNK, :])
            g_ext[ch, HALO:HALO + t, :] = hh_ref[:, DFF + lo:DFF + lo + FF_CHUNK]
            for k in range(FFN_K - 1):
                rot[k] = g_ext[ch, pl.ds(HALO - (FFN_K - 1) + k, t), :]
            for cb in range(FF_CHUNK // LANE):
                cols = slice(cb * LANE, (cb + 1) * LANE)
                fcols = slice(lo + cb * LANE, lo + (cb + 1) * LANE)
                for r0 in range(0, t, rb):
                    rows = pl.ds(r0, rb)
                    gc = (f3_ref[FFN_K:FFN_K + 1, fcols] + f3_ref[0:1, fcols] * rot[0, rows, cols]
                          + f3_ref[1:2, fcols] * rot[1, rows, cols]
                          + f3_ref[2:3, fcols] * g_ext[ch, pl.ds(HALO + r0, rb), cols])
                    z_ref[rows, fcols] = (gc * _sigmoid(gc) * hh_ref[rows, fcols]).astype(MXU_DTYPE)
            g_ext[ch, 0:HALO, :] = g_ext[ch, t:t + HALO, :]
        y2_ref[...] = _dot(z_ref[...], wdown_v[...])

        def out_rows(r0, acc):
            rows = pl.ds(r0, 8)
            xh, _ = _layer_norm_stats(ALPHA * x1_ref[rows, :] + (1.0 + mod_ref[5:6, :]) * y2_ref[rows, :])
            err = xh * ln2_ref[0:1, :] + ln2_ref[1:2, :] - tgt_ref[rows, :]
            return acc + err * err

        e2 = _row_blocks(t, 8, out_rows, jnp.zeros((8, D), F32))
        part = e2[:, 0:128]
        for j in range(1, D // 128):
            part = part + e2[:, 128 * j:128 * (j + 1)]
        loss_ref[...] += part

    tok = lambda w: pl.BlockSpec((None, t, w), lambda b, i: (b, i, 0))
    outs = [(2 * DFF, F32), (DFF, MXU_DTYPE), (D, MXU_DTYPE), (D, F32)]
    return pl.pallas_call(
        body, name="ffn_fwd", grid=(bl, nt),
        out_shape=[jax.ShapeDtypeStruct((bl, seq, w), dt) for w, dt in outs] + [jax.ShapeDtypeStruct((8, 128), F32)],
        in_specs=[tok(D), pl.BlockSpec((None, 8, D), lambda b, i: (b, 0, 0)), tok(D), _ANY, _ANY,
                  _full(f3.shape), _full(ln2.shape)],
        out_specs=[tok(w) for w, _ in outs] + [_full((8, 128))],
        scratch_shapes=[pltpu.VMEM(wup_t.shape, MXU_DTYPE), pltpu.VMEM(wdown.shape, MXU_DTYPE),
                        pltpu.VMEM((n_chunk, t + HALO, FF_CHUNK), F32), pltpu.VMEM((FFN_K - 1, t, FF_CHUNK), F32),
                        pltpu.SemaphoreType.DMA((2,))],
        compiler_params=_params(2),
    )(x1, mods, tgt, wup_t, wdown, f3, ln2)


def _ffn_bwd(x1, y2, tgt, hh, mods, wup_t, wdown, f3, ln2):
    bl, seq, _ = x1.shape
    t = min(TOK_TILE, seq)
    nt = seq // t
    n_chunk = DFF // FF_CHUNK
    rb = min(ROW_BLK, t)

    def body(x1_ref, y2_ref, tgt_ref, hh_ref, halo_ref, mod_ref, wup_hbm, wdown_hbm, f3_ref, ln2_ref,
             dx1_ref, dy2_ref, dhh_ref, dln2_ref, df3_ref, dmod_ref,
             wup_v, wdown_v, g_ext, dgc_ext, rot, dz_scr, wide_scr, sems):
        b, i = pl.program_id(0), pl.program_id(1)
        tt = nt - 1 - i

        @pl.when((b == 0) & (i == 0))
        def _():
            cps = [pltpu.make_async_copy(wup_hbm, wup_v, sems.at[0]), pltpu.make_async_copy(wdown_hbm, wdown_v, sems.at[1])]
            for cp in cps:
                cp.start()
            for cp in cps:
                cp.wait()
            dln2_ref[...] = jnp.zeros_like(dln2_ref)
            df3_ref[...] = jnp.zeros_like(df3_ref)

        @pl.when(i == 0)
        def _():
            dmod_ref[...] = jnp.zeros_like(dmod_ref)
            for ch in range(n_chunk):
                dgc_ext[ch, t:t + HALO, :] = jnp.zeros((HALO, FF_CHUNK), F32)

        def ln_rows(r0, acc):
            rows = pl.ds(r0, 8)
            x1v, y2v = x1_ref[rows, :], y2_ref[rows, :]
            gt = mod_ref[5:6, :]
            xh, rstd = _layer_norm_stats(ALPHA * x1v + (1.0 + gt) * y2v)
            g2 = ln2_ref[0:1, :]
            dx2 = (xh * g2 + ln2_ref[1:2, :] - tgt_ref[rows, :]) * (1.0 / D)
            dxh = dx2 * g2
            dr2 = rstd * (dxh - _rowmean(dxh) - xh * _rowmean(dxh * xh))
            wide_scr[rows, :] = (1.0 + gt) * dr2
            dx1_ref[rows, :] = ALPHA * dr2
            return acc[0] + dx2 * xh, acc[1] + dx2, acc[2] + dr2 * y2v

        zero_d = jnp.zeros((8, D), F32)
        acc = _row_blocks(t, 8, ln_rows, (zero_d, zero_d, zero_d))
        dln2_ref[0:1, :] += _colsum(acc[0])
        dln2_ref[1:2, :] += _colsum(acc[1])
        dmod_ref[2:3, :] += _colsum(acc[2])
        _cast_rows(wide_scr, dy2_ref, t)

        halo_keep = (tt > 0).astype(F32)
        zero_l = jnp.zeros((8, LANE), F32)
        for ch in range(n_chunk):
            lo = ch * FF_CHUNK
            dz_scr[...] = _dot_nt(dy2_ref[...], wdown_v[lo:lo + FF_CHUNK, :])
            g_ext[0:HALO, :] = halo_ref[:, lo:lo + FF_CHUNK] * halo_keep
            g_ext[HALO:HALO + t, :] = hh_ref[:, DFF + lo:DFF + lo + FF_CHUNK]
            for k in range(FFN_K - 1):
                rot[k] = g_ext[pl.ds(HALO - (FFN_K - 1) + k, t), :]
            for cb in range(FF_CHUNK // LANE):
                cols = slice(cb * LANE, (cb + 1) * LANE)
                fcols = slice(lo + cb * LANE, lo + (cb + 1) * LANE)

                def gate_rows(r0, acc, cols=cols, fcols=fcols, ch=ch):
                    rows = pl.ds(r0, rb)
                    taps = [rot[0, rows, cols], rot[1, rows, cols], g_ext[pl.ds(r0 + HALO, rb), cols]]
                    gc = f3_ref[FFN_K:FFN_K + 1, fcols]
                    for k in range(FFN_K):
                        gc = gc + f3_ref[k:k + 1, fcols] * taps[k]
                    s = _sigmoid(gc)
                    dz = dz_scr[rows, cols]
                    dhh_ref[rows, fcols] = (dz * (gc * s)).astype(MXU_DTYPE)
                    dgc = dz * hh_ref[rows, fcols] * (s * (1.0 + gc * (1.0 - s)))
                    dgc_ext[ch, rows, cols] = dgc
                    return tuple(acc[k] + _fold8(dgc * taps[k]) for k in range(FFN_K)) + (acc[FFN_K] + _fold8(dgc),)

                acc = _row_blocks(t, rb, gate_rows, (zero_l,) * (FFN_K + 1))
                for k in range(FFN_K + 1):
                    df3_ref[k:k + 1, fcols] += _colsum(acc[k])
            for k in range(FFN_K - 1):
                rot[k] = dgc_ext[ch, pl.ds(k + 1, t), :]
            for cb in range(FF_CHUNK // LANE):
                cols = slice(cb * LANE, (cb + 1) * LANE)
                fcols = slice(lo + cb * LANE, lo + (cb + 1) * LANE)
                gcols = slice(DFF + lo + cb * LANE, DFF + lo + (cb + 1) * LANE)

                def dg_rows(r0, acc, cols=cols, fcols=fcols, gcols=gcols, ch=ch):
                    rows = pl.ds(r0, rb)
                    dg = (f3_ref[2:3, fcols] * dgc_ext[ch, rows, cols] + f3_ref[1:2, fcols] * rot[0, rows, cols]
                          + f3_ref[0:1, fcols] * rot[1, rows, cols])
                    dhh_ref[rows, gcols] = dg.astype(MXU_DTYPE)
                    return acc

                _row_blocks(t, rb, dg_rows, ())
            dgc_ext[ch, t:t + HALO, :] = dgc_ext[ch, 0:HALO, :]

        wide_scr[...] = _dot(dhh_ref[...], wup_v[...])

        def out_rows(r0, acc):
            rows = pl.ds(r0, 8)
            du2 = wide_scr[rows, :]
            dx1_ref[rows, :] = dx1_ref[rows, :] + du2 * (1.0 + mod_ref[4:5, :])
            return acc[0] + du2 * x1_ref[rows, :], acc[1] + du2

        acc = _row_blocks(t, 8, out_rows, (zero_d, zero_d))
        dmod_ref[1:2, :] += _colsum(acc[0])
        dmod_ref[0:1, :] += _colsum(acc[1])

    rev = lambda w: pl.BlockSpec((None, t, w), lambda b, i: (b, nt - 1 - i, 0))
    halo = pl.BlockSpec((None, HALO, DFF), lambda b, i: (b, jnp.maximum((nt - 1 - i) * (t // HALO) - 1, 0), 1))
    return pl.pallas_call(
        body, name="ffn_bwd", grid=(bl, nt),
        out_shape=[jax.ShapeDtypeStruct((bl, seq, D), F32), jax.ShapeDtypeStruct((bl, seq, D), MXU_DTYPE),
                   jax.ShapeDtypeStruct((bl, seq, 2 * DFF), MXU_DTYPE), jax.ShapeDtypeStruct((8, D), F32),
                   jax.ShapeDtypeStruct((8, DFF), F32), jax.ShapeDtypeStruct((bl, 8, D), F32)],
        in_specs=[rev(D), rev(D), rev(D), rev(2 * DFF), halo, pl.BlockSpec((None, 8, D), lambda b, i: (b, 0, 0)),
                  _ANY, _ANY, _full(f3.shape), _full(ln2.shape)],
        out_specs=[rev(D), rev(D), rev(2 * DFF), _full((8, D)), _full((8, DFF)),
                   pl.BlockSpec((None, 8, D), lambda b, i: (b, 0, 0))],
        scratch_shapes=[pltpu.VMEM(wup_t.shape, MXU_DTYPE), pltpu.VMEM(wdown.shape, MXU_DTYPE),
                        pltpu.VMEM((t + HALO, FF_CHUNK), F32), pltpu.VMEM((n_chunk, t + HALO, FF_CHUNK), F32),
                        pltpu.VMEM((FFN_K - 1, t, FF_CHUNK), F32), pltpu.VMEM((t, FF_CHUNK), F32),
                        pltpu.VMEM((t, D), F32), pltpu.SemaphoreType.DMA((2,))],
        compiler_params=_params(2),
    )(x1, y2, tgt, hh, hh, mods, wup_t, wdown, f3, ln2)


def _mixer_bwd(x, dx1, proj, h, vc, mixed, mods, win_t, wout, wri, mavg2, va, vb, ln1, ffn_wgrads):
    bl, seq, _ = x.shape
    t = min(TOK_TILE, seq)
    nt = seq // t
    pw = 2 * LW + 2 * CW
    n_g = len(ffn_wgrads)
    rb = min(ROW_BLK, t)
    half_f32 = [pltpu.VMEM((t, LW), F32)] * 10

    def body(x_ref, dx1_ref, proj_ref, phalo_ref, h_ref, hhalo_ref, vc_ref, mixed_ref, mod_ref,
             win_hbm, wout_hbm, wri_ref, mavg_ref, va_ref, vb_ref, ln1_ref, *rest):
        wgrad_refs, rest = rest[:n_g], rest[n_g:]
        gx_ref, dproj_ref, dm_ref, ga_ref, gw31_ref, dwr_ref, dwi_ref, dln1_ref, dmod_ref = rest[:9]
        slot_refs, rest = rest[9:9 + n_g], rest[9 + n_g:]
        (win_v, wout_v, xa_ext, vg_scr, dvc_ext, dxc_ext, rot_b, car, sems,
         st_a, st_b, st_c, yn_scr, dyn_scr, hprev_scr, xc_scr, a_scr, mult_scr, g_scr,
         wide, gates, hilo_a, hilo_b, dgates, xcb_scr, rot_c, rot_d, hp_ext, e_send, e_recv, e_local) = rest
        b, i = pl.program_id(0), pl.program_id(1)
        tt = nt - 1 - i
        e_start, e_finish = _exchange_steps(wgrad_refs, slot_refs, e_send, e_recv, e_local)

        @pl.when((b == 0) & (i == 0))
        def _():
            e_start()
            cps = [pltpu.make_async_copy(win_hbm, win_v, sems.at[0]), pltpu.make_async_copy(wout_hbm, wout_v, sems.at[1])]
            for cp in cps:
                cp.start()
            for cp in cps:
                cp.wait()
            for ref in (ga_ref, gw31_ref, dwr_ref, dwi_ref, dln1_ref):
                ref[...] = jnp.zeros_like(ref)

        @pl.when(i == 0)
        def _():
            dmod_ref[...] = jnp.zeros_like(dmod_ref)
            dvc_ext[t:t + HALO31, :] = jnp.zeros((HALO31, CW), F32)
            dxc_ext[t:t + HALO, :] = jnp.zeros((HALO, LW), F32)
            car[...] = jnp.zeros_like(car)

        halo_keep = (tt > 0).astype(F32)
        zero_d = jnp.zeros((8, D), F32)
        zero_l = jnp.zeros((8, LANE), F32)
        blocks = [(r0, c0) for c0 in range(0, LW, LANE) for r0 in range(0, t, rb)]

        def add_row(ref, row, c0, acc):
            ref[row:row + 1, c0:c0 + LANE] += _colsum(acc)

        def ln_rows(r0, acc):
            rows = pl.ds(r0, 8)
            xv, mixed = x_ref[rows, :], mixed_ref[rows, :]
            gt = mod_ref[2:3, :]
            xh, rstd = _layer_norm_stats(ALPHA * xv + (1.0 + gt) * mixed)
            dx1 = dx1_ref[rows, :]
            dxh = dx1 * ln1_ref[0:1, :]
            dr1 = rstd * (dxh - _rowmean(dxh) - xh * _rowmean(dxh * xh))
            wide[rows, :] = (1.0 + gt) * dr1
            gx_ref[rows, :] = ALPHA * dr1
            return acc[0] + dx1 * xh, acc[1] + dx1, acc[2] + dr1 * mixed

        acc = _row_blocks(t, 8, ln_rows, (zero_d, zero_d, zero_d))
        dln1_ref[0:1, :] += _colsum(acc[0])
        dln1_ref[1:2, :] += _colsum(acc[1])
        dmod_ref[2:3, :] += _colsum(acc[2])
        _cast_rows(wide, dm_ref, t)
        wide[...] = _dot_nt(dm_ref[...], wout_v[...])

        mavg2 = mavg_ref[...]
        for r0, c0 in blocks:
            rows, cols = pl.ds(r0, rb), slice(c0, c0 + LANE)
            _store_hilo(hilo_a, rows, c0, vc_ref[rows, cols])
        st_a[...] = _dot(hilo_a[...], mavg2)
        for r0, c0 in blocks:
            rows, cols = pl.ds(r0, rb), slice(c0, c0 + LANE)
            dl = vc_ref[rows, cols] - st_a[rows, cols]
            yn_scr[rows, cols] = dl
            _store_hilo(hilo_a, rows, c0, dl * dl)
        st_b[...] = _dot(hilo_a[...], mavg2)
        for c0 in range(0, CW, LANE):
            cols, pc = slice(c0, c0 + LANE), 2 * LW + c0
            ng, nb = vb_ref[CONV_K + 1:CONV_K + 2, cols], vb_ref[CONV_K + 2:CONV_K + 3, cols]
            acc_g, acc_b = zero_l, zero_l
            for r0 in range(0, t, rb):
                rows = pl.ds(r0, rb)
                rs = lax.rsqrt(st_b[rows, cols] + LN_EPS)
                yn = yn_scr[rows, cols] * rs
                yl = yn * ng + nb
                s = _sigmoid(yl)
                dyl = wide[rows, LW + c0:LW + c0 + LANE] * (s * (1.0 + yl * (1.0 - s)))
                acc_g, acc_b = acc_g + _fold8(dyl * yn), acc_b + _fold8(dyl)
                dyn = dyl * ng
                st_b[rows, cols] = rs
                yn_scr[rows, cols] = yn
                dyn_scr[rows, cols] = dyn
                _store_hilo(hilo_a, rows, c0, dyn)
                _store_hilo(hilo_b, rows, c0, dyn * yn)
                vg_scr[rows, cols] = proj_ref[rows, pc:pc + LANE] * _sigmoid(proj_ref[rows, pc + CW:pc + CW + LANE])
            add_row(ga_ref, 8, c0, acc_g)
            add_row(ga_ref, 9, c0, acc_b)
        st_a[...] = _dot(hilo_a[...], mavg2)
        st_c[...] = _dot(hilo_b[...], mavg2)
        for c0 in range(0, CW, LANE):
            cols = slice(c0, c0 + LANE)
            acc_b = zero_l
            for r0 in range(0, t, rb):
                rows = pl.ds(r0, rb)
                dvc = st_b[rows, cols] * (dyn_scr[rows, cols] - st_a[rows, cols] - yn_scr[rows, cols] * st_c[rows, cols])
                dvc_ext[rows, cols] = dvc
                acc_b = acc_b + _fold8(dvc)
            add_row(ga_ref, 10, c0, acc_b)
        dvc_win = _shifted_rows(dvc_ext, rot_b, rb)
        for c0 in range(0, CW, LANE):
            cols, pc = slice(c0, c0 + LANE), 2 * LW + c0
            for k in range(CONV_K):
                acc_w = zero_l
                for r0 in range(0, t, rb):
                    acc_w = acc_w + _fold8(vg_scr[pl.ds(r0, rb), cols] * dvc_win(CONV_K - 1 - k, r0, cols))
                add_row(gw31_ref, k, c0, acc_w)
            for r0 in range(0, t, rb):
                rows = pl.ds(r0, rb)
                dvg = jnp.zeros((rb, LANE), F32)
                for k in range(CONV_K):
                    dvg = dvg + vb_ref[k:k + 1, cols] * dvc_win(CONV_K - 1 - k, r0, cols)
                vbr = proj_ref[rows, pc:pc + LANE]
                sgb = _sigmoid(proj_ref[rows, pc + CW:pc + CW + LANE])
                dproj_ref[rows, pc:pc + LANE] = (dvg * sgb).astype(MXU_DTYPE)
                dproj_ref[rows, pc + CW:pc + CW + LANE] = (dvg * vbr * sgb * (1.0 - sgb)).astype(MXU_DTYPE)
        dvc_ext[t:t + HALO31, :] = dvc_ext[0:HALO31, :]

        xa_ext[0:HALO, :] = phalo_ref[HALO31 - HALO:HALO31, 0:LW] * halo_keep
        xa_ext[HALO:HALO + t, :] = proj_ref[:, 0:LW]
        for k in range(LRU_K - 1):
            rot_c[k] = xa_ext[pl.ds(HALO - (LRU_K - 1) + k, t), :]
        hp_ext[0:HALO, :] = hhalo_ref[...] * halo_keep
        hp_ext[HALO:HALO + t, :] = h_ref[...]
        hprev_scr[...] = hp_ext[pl.ds(HALO - 1, t), :]

        def xa_tap(k, rows, r0, cols):
            return xa_ext[pl.ds(HALO + r0, rb), cols] if k == LRU_K - 1 else rot_c[k, rows, cols]

        for r0, c0 in blocks:
            rows, cols = pl.ds(r0, rb), slice(c0, c0 + LANE)
            xc = va_ref[4:5, cols]
            for k in range(LRU_K):
                xc = xc + va_ref[k:k + 1, cols] * xa_tap(k, rows, r0, cols)
            xc_scr[rows, cols] = xc
            xcb_scr[rows, cols] = xc.astype(MXU_DTYPE)
        gates[...] = _dot(xcb_scr[...], wri_ref[...])
        for r0, c0 in blocks:
            rows, cols = pl.ds(r0, rb), slice(c0, c0 + LANE)
            r = _sigmoid(gates[rows, cols] + va_ref[5:6, cols])
            ig = _sigmoid(gates[rows, LW + c0:LW + c0 + LANE] + va_ref[6:7, cols])
            la = (-LRU_C) * r * _softplus(-va_ref[7:8, cols])
            a = jnp.exp(la)
            gates[rows, cols] = r
            gates[rows, LW + c0:LW + c0 + LANE] = ig
            a_scr[rows, cols] = a
            mult_scr[rows, cols] = jnp.sqrt(-jnp.tanh(la) * (a * a + 1.0))
            ge, dge = _gelu_and_grad(proj_ref[rows, LW + c0:LW + c0 + LANE])
            dya = wide[rows, cols]
            dproj_ref[rows, LW + c0:LW + c0 + LANE] = (dya * h_ref[rows, cols] * dge).astype(MXU_DTYPE)
            g_scr[rows, cols] = dya * ge
        _scan_bwd_blocks(a_scr, g_scr, car, t)
        for c0 in range(0, LW, LANE):
            cols = slice(c0, c0 + LANE)
            sp = _softplus(-va_ref[7:8, cols])
            acc_l, acc_r, acc_i = zero_l, zero_l, zero_l
            for r0 in range(0, t, rb):
                rows = pl.ds(r0, rb)
                gv, xc, a, mult = g_scr[rows, cols], xc_scr[rows, cols], a_scr[rows, cols], mult_scr[rows, cols]
                r, ig = gates[rows, cols], gates[rows, LW + c0:LW + c0 + LANE]
                dla = gv * hprev_scr[rows, cols] * a - gv * (ig * xc) * (a * a / mult)
                acc_l = acc_l + _fold8(dla * ((-LRU_C) * r))
                dgr = dla * ((-LRU_C) * sp) * r * (1.0 - r)
                dgi = gv * (mult * xc) * ig * (1.0 - ig)
                acc_r, acc_i = acc_r + _fold8(dgr), acc_i + _fold8(dgi)
                dgates[rows, cols] = dgr.astype(MXU_DTYPE)
                dgates[rows, LW + c0:LW + c0 + LANE] = dgi.astype(MXU_DTYPE)
                dxc_ext[rows, cols] = gv * (mult * ig)
            add_row(ga_ref, 5, c0, acc_r)
            add_row(ga_ref, 6, c0, acc_i)
            ga_ref[7:8, cols] += _colsum(acc_l) * (-_sigmoid(-va_ref[7:8, cols]))
        dwr_ref[...] += _dot_tn(xcb_scr[...], dgates[:, 0:LW])
        dwi_ref[...] += _dot_tn(xcb_scr[...], dgates[:, LW:])
        st_a[...] = _dot_nt(dgates[...], wri_ref[...])
        for c0 in range(0, LW, LANE):
            cols = slice(c0, c0 + LANE)
            acc_b = zero_l
            for r0 in range(0, t, rb):
                rows = pl.ds(r0, rb)
                dxc = dxc_ext[rows, cols] + st_a[rows, cols]
                dxc_ext[rows, cols] = dxc
                acc_b = acc_b + _fold8(dxc)
            add_row(ga_ref, 4, c0, acc_b)
        for k in range(LRU_K - 1):
            rot_d[k] = dxc_ext[pl.ds(k + 1, t), :]
        for c0 in range(0, LW, LANE):
            cols = slice(c0, c0 + LANE)
            acc_w = [zero_l] * LRU_K
            for r0 in range(0, t, rb):
                rows = pl.ds(r0, rb)
                dxc = dxc_ext[rows, cols]
                dxa = va_ref[LRU_K - 1:LRU_K, cols] * dxc
                for k in range(LRU_K):
                    acc_w[k] = acc_w[k] + _fold8(dxc * xa_tap(k, rows, r0, cols))
                    if k < LRU_K - 1:
                        dxa = dxa + va_ref[k:k + 1, cols] * rot_d[LRU_K - 2 - k, rows, cols]
                dproj_ref[rows, cols] = dxa.astype(MXU_DTYPE)
            for k in range(LRU_K):
                add_row(ga_ref, k, c0, acc_w[k])
        dxc_ext[t:t + HALO, :] = dxc_ext[0:HALO, :]

        wide[...] = _dot(dproj_ref[...], win_v[...])

        def out_rows(r0, acc):
            rows = pl.ds(r0, 8)
            du1 = wide[rows, :]
            gx_ref[rows, :] = gx_ref[rows, :] + du1 * (1.0 + mod_ref[1:2, :])
            return acc[0] + du1 * x_ref[rows, :], acc[1] + du1

        acc = _row_blocks(t, 8, out_rows, (zero_d, zero_d))
        dmod_ref[1:2, :] += _colsum(acc[0])
        dmod_ref[0:1, :] += _colsum(acc[1])

        pl.when((b == bl - 1) & (i == nt - 1))(e_finish)

    rev = lambda w: pl.BlockSpec((None, t, w), lambda b, i: (b, nt - 1 - i, 0))

    def halo(rows_, w):
        return pl.BlockSpec((None, rows_, w), lambda b, i: (b, jnp.maximum((nt - 1 - i) * (t // rows_) - 1, 0), 0))

    return pl.pallas_call(
        body, name="mixer_bwd", grid=(bl, nt),
        out_shape=[jax.ShapeDtypeStruct((bl, seq, D), F32), jax.ShapeDtypeStruct((bl, seq, pw), MXU_DTYPE),
                   jax.ShapeDtypeStruct((bl, seq, D), MXU_DTYPE), jax.ShapeDtypeStruct((16, LW), F32),
                   jax.ShapeDtypeStruct((32, CW), F32), jax.ShapeDtypeStruct((LW, LW), F32),
                   jax.ShapeDtypeStruct((LW, LW), F32), jax.ShapeDtypeStruct((8, D), F32),
                   jax.ShapeDtypeStruct((bl, 8, D), F32)] + _exchange_out_shapes(ffn_wgrads),
        in_specs=[rev(D), rev(D), rev(pw), halo(HALO31, pw), rev(LW), halo(HALO, LW), rev(CW), rev(D),
                  pl.BlockSpec((None, 8, D), lambda b, i: (b, 0, 0)), _ANY, _ANY,
                  _full(wri.shape), _full(mavg2.shape), _full(va.shape), _full(vb.shape), _full(ln1.shape)]
        + [_ANY] * n_g,
        out_specs=[rev(D), rev(pw), rev(D), _full((16, LW)), _full((32, CW)), _full((LW, LW)), _full((LW, LW)),
                   _full((8, D)), pl.BlockSpec((None, 8, D), lambda b, i: (b, 0, 0))] + [_ANY] * n_g,
        scratch_shapes=[pltpu.VMEM(win_t.shape, MXU_DTYPE), pltpu.VMEM(wout.shape, MXU_DTYPE),
                        pltpu.VMEM((t + HALO, LW), F32), pltpu.VMEM((t, CW), F32),
                        pltpu.VMEM((t + HALO31, CW), F32), pltpu.VMEM((t + HALO, LW), F32),
                        pltpu.VMEM((7, t + HALO31 - 8, CW), F32),
                        pltpu.VMEM((8, LW), F32), pltpu.SemaphoreType.DMA((2,))] + half_f32
        + [pltpu.VMEM((t, D), F32), pltpu.VMEM((t, 2 * LW), F32), pltpu.VMEM((t, 2 * CW), MXU_DTYPE),
           pltpu.VMEM((t, 2 * CW), MXU_DTYPE), pltpu.VMEM((t, 2 * LW), MXU_DTYPE), pltpu.VMEM((t, LW), MXU_DTYPE),
           pltpu.VMEM((LRU_K - 1, t, LW), F32), pltpu.VMEM((LRU_K - 1, t, LW), F32), pltpu.VMEM((t + HALO, LW), F32)]
        + _comm_sems(n_g),
        compiler_params=_params(2),
    )(x, dx1, proj, proj, h, h, vc, mixed, mods, win_t, wout, wri, mavg2, va, vb, ln1, *ffn_wgrads)


def _weight_grad(a, bmat, tm, name, out_dtype=F32, exchange=(), exchange_half=None, col_phase=None, tok_tile=None):
    ntok, m = a.shape
    n = bmat.shape[1]
    tk = min(tok_tile or WG_TOK_TILE, ntok)
    nk = ntok // tk
    stride, col0 = (1, 0) if col_phase is None else (2, col_phase)
    m = m // stride
    ni = m // tm
    n_e = len(exchange)

    def body(a_ref, b_ref, *rest):
        e_in, o_ref, e_out = rest[:n_e], rest[n_e], rest[n_e + 1:2 * n_e + 1]
        acc_ref = rest[2 * n_e + 1]
        i, k = pl.program_id(0), pl.program_id(1)
        if n_e:
            e_start, e_finish = _exchange_steps(e_in, e_out, *rest[2 * n_e + 2:], half=exchange_half)
            pl.when((i == 0) & (k == 0))(e_start)

        @pl.when(k == 0)
        def _():
            acc_ref[...] = jnp.zeros_like(acc_ref)

        acc_ref[...] += _dot_tn(a_ref[...], b_ref[...])

        @pl.when(k == nk - 1)
        def _():
            o_ref[...] = acc_ref[...].astype(out_dtype)

        if n_e:
            pl.when((i == ni - 1) & (k == nk - 1))(e_finish)

    res = pl.pallas_call(
        body, name=name, grid=(ni, nk),
        out_shape=[jax.ShapeDtypeStruct((m, n), out_dtype)] + _exchange_out_shapes(exchange, exchange_half),
        in_specs=[pl.BlockSpec((tk, tm), lambda i, k: (k, i * stride + col0)), pl.BlockSpec((tk, n), lambda i, k: (k, 0))]
        + [_ANY] * n_e,
        out_specs=[pl.BlockSpec((tm, n), lambda i, k: (i, 0))] + [_ANY] * n_e,
        scratch_shapes=[pltpu.VMEM((tm, n), F32)] + (_comm_sems(n_e) if n_e else []),
        compiler_params=_params(2),
    )(a, bmat, *exchange)
    return res if n_e else res[0]


def _sum_slots(slots, name, slots_c1=None):
    _, r, cdim = slots.shape
    tr = next((cand for cand in (64, 32) if r % cand == 0), r)
    ins = [slots] if slots_c1 is None else [slots, slots_c1]

    def body(*refs):
        o_ref = refs[-1]

        def total(s_ref):
            acc = s_ref[0].astype(F32)
            for j in range(1, NDEV):
                acc = acc + s_ref[j].astype(F32)
            o_ref[...] = acc

        if slots_c1 is None:
            total(refs[0])
        else:
            c = lax.axis_index("c")
            pl.when(c == 0)(functools.partial(total, refs[0]))
            pl.when(c == 1)(functools.partial(total, refs[1]))

    return pl.pallas_call(
        body, name=name, grid=(r // tr,), out_shape=jax.ShapeDtypeStruct((r, cdim), F32),
        in_specs=[pl.BlockSpec((NDEV, tr, cdim), lambda i: (0, i, 0))] * len(ins),
        out_specs=pl.BlockSpec((tr, cdim), lambda i: (i, 0)),
        compiler_params=_params(1),
    )(*ins)


def _small_grad_layout(bl):
    r1 = 6 * bl
    return {"dmod": 0, "wide": r1, "ln": r1 + 32, "df3": r1 + 36, "g_wr": r1 + 48, "g_wi": r1 + 80, "rows": r1 + 112}


def _pack_small_grads(dmod_a, dmod_b, g_a, g_w31, dln1, dln2, df3, g_wr, g_wi, loss_part):
    bl = dmod_a.shape[0] // 8
    lay = _small_grad_layout(bl)
    ins = [dmod_a, dmod_b, g_a, g_w31, dln1, dln2, df3, g_wr, g_wi, loss_part]

    def body(ma_ref, mb_ref, ga_ref, gw_ref, l1_ref, l2_ref, f3_ref, wr_ref, wi_ref, ls_ref, o_ref):
        o_ref[...] = jnp.zeros_like(o_ref)
        for b in range(bl):
            o_ref[6 * b:6 * b + 3, :] = ma_ref[8 * b:8 * b + 3, :]
            o_ref[6 * b + 3:6 * b + 6, :] = mb_ref[8 * b:8 * b + 3, :]
        wide = lay["wide"]
        o_ref[wide:wide + 32, 0:CW] = gw_ref[...]
        o_ref[wide:wide + 16, CW:CW + LW] = ga_ref[...]
        o_ref[wide + 16:wide + 24, CW:CW + LANE] = ls_ref[...]
        o_ref[lay["ln"]:lay["ln"] + 2, :] = l1_ref[0:2, :]
        o_ref[lay["ln"] + 2:lay["ln"] + 4, :] = l2_ref[0:2, :]
        for j, c0 in enumerate(range(0, DFF, D)):
            w = min(D, DFF - c0)
            o_ref[lay["df3"] + 4 * j:lay["df3"] + 4 * j + 4, 0:w] = f3_ref[0:4, c0:c0 + w]
        o_ref[lay["g_wr"]:lay["g_wr"] + 32, :] = wr_ref[...]
        o_ref[lay["g_wi"]:lay["g_wi"] + 32, :] = wi_ref[...]

    return pl.pallas_call(
        body, name="pack_small_grads", out_shape=jax.ShapeDtypeStruct((lay["rows"], D), F32),
        in_specs=[_full(a.shape) for a in ins], out_specs=_full((lay["rows"], D)),
        compiler_params=pltpu.CompilerParams(vmem_limit_bytes=VMEM_LIMIT),
    )(*ins)


def _sum_small_grads(packed_all, bl):
    lay = _small_grad_layout(bl)
    shapes = {"lru_conv_w": (LRU_K, LW), "lru_conv_b": (1, LW), "lru_b_r": (1, LW), "lru_b_i": (1, LW),
              "lru_lambda": (1, LW), "conv_norm_g": (1, CW), "conv_norm_b": (1, CW), "conv_b": (1, CW),
              "conv_w": (CONV_K, CW), "ln1_g": (1, D), "ln1_b": (1, D), "ln2_g": (1, D), "ln2_b": (1, D),
              "ffn_conv_w": (FFN_K, DFF), "ffn_conv_b": (1, DFF), "lru_w_r": (32, D), "lru_w_i": (32, D),
              "loss": (8, LANE)}
    names = list(shapes)

    def body(p_ref, *outs):
        o = dict(zip(names, outs))

        def part(r0, nr, c0=0, nc=D):
            acc = p_ref[0, r0:r0 + nr, c0:c0 + nc]
            for j in range(1, NDEV):
                acc = acc + p_ref[j, r0:r0 + nr, c0:c0 + nc]
            return acc

        wide = lay["wide"]
        ga = part(wide, 16, CW, LW)
        o["lru_conv_w"][...] = ga[0:LRU_K]
        for row, k in ((4, "lru_conv_b"), (5, "lru_b_r"), (6, "lru_b_i"), (7, "lru_lambda"), (8, "conv_norm_g"),
                       (9, "conv_norm_b"), (10, "conv_b")):
            o[k][...] = ga[row:row + 1]
        o["conv_w"][...] = part(wide, 32, 0, CW)[0:CONV_K]
        o["loss"][...] = part(wide + 16, 8, CW, LANE)
        ln = part(lay["ln"], 4)
        o["ln1_g"][...], o["ln1_b"][...], o["ln2_g"][...], o["ln2_b"][...] = ln[0:1], ln[1:2], ln[2:3], ln[3:4]
        for j, c0 in enumerate(range(0, DFF, D)):
            w = min(D, DFF - c0)
            f3 = part(lay["df3"] + 4 * j, 4, 0, w)
            o["ffn_conv_w"][:, c0:c0 + w] = f3[0:FFN_K]
            o["ffn_conv_b"][:, c0:c0 + w] = f3[FFN_K:FFN_K + 1]
        o["lru_w_r"][...] = part(lay["g_wr"], 32)
        o["lru_w_i"][...] = part(lay["g_wi"], 32)

    res = pl.pallas_call(
        body, name="sum_small_grads", out_shape=[jax.ShapeDtypeStruct(shapes[k], F32) for k in names],
        in_specs=[_full(packed_all.shape)], out_specs=[_full(shapes[k]) for k in names],
        compiler_params=pltpu.CompilerParams(vmem_limit_bytes=VMEM_LIMIT),
    )(packed_all)
    return dict(zip(names, res))


def _adamw_update(w_ref, g_ref, m_ref, v_ref, d_ref, nm_ref, nv_ref):
    gv = g_ref[...]
    nm = ADAM_B1 * m_ref[...] + (1.0 - ADAM_B1) * gv
    nv = ADAM_B2 * v_ref[...] + (1.0 - ADAM_B2) * (gv * gv)
    m_hat = nm / (1.0 - ADAM_B1 ** ADAM_STEP)
    v_hat = nv / (1.0 - ADAM_B2 ** ADAM_STEP)
    d_ref[...] = -ADAM_LR * (m_hat / (jnp.sqrt(v_hat) + ADAM_EPS) + ADAM_WD * w_ref[...])
    nm_ref[...] = nm
    nv_ref[...] = nv


def _sum_adamw(slots, w, m, v, name):
    _, r, cdim = slots.shape
    tr = next((cand for cand in (64, 32) if r % cand == 0), r)

    def body(s_ref, w_ref, m_ref, v_ref, g_ref, d_ref, nm_ref, nv_ref):
        acc = s_ref[0].astype(F32)
        for j in range(1, NDEV):
            acc = acc + s_ref[j].astype(F32)
        g_ref[...] = acc
        _adamw_update(w_ref, g_ref, m_ref, v_ref, d_ref, nm_ref, nv_ref)

    spec = pl.BlockSpec((tr, cdim), lambda i: (i, 0))
    return pl.pallas_call(
        body, name=name, grid=(r // tr,), out_shape=[jax.ShapeDtypeStruct((r, cdim), F32)] * 4,
        in_specs=[pl.BlockSpec((NDEV, tr, cdim), lambda i: (0, i, 0))] + [spec] * 3, out_specs=[spec] * 4,
        compiler_params=_params(1),
    )(slots, w, m, v)


def _adamw_many(ws, gs, ms, vs, name):
    n = len(ws)

    def body(*refs):
        ins, outs = refs[:4 * n], refs[4 * n:]
        for k in range(n):
            _adamw_update(ins[k], ins[n + k], ins[2 * n + k], ins[3 * n + k], outs[k], outs[n + k], outs[2 * n + k])

    specs = [_full(w.shape) for w in ws]
    res = pl.pallas_call(
        body, name=name, out_shape=[jax.ShapeDtypeStruct(w.shape, F32) for w in ws] * 3,
        in_specs=specs * 4, out_specs=specs * 3,
        compiler_params=pltpu.CompilerParams(vmem_limit_bytes=VMEM_LIMIT),
    )(*ws, *gs, *ms, *vs)
    return res[:n], res[n:2 * n], res[2 * n:]


def _adamw(w, g, m, v, name):
    r, cdim = w.shape
    tr = next((cand for cand in (256, 128) if r % cand == 0), r)
    body = functools.partial(_adamw_update)
    spec = pl.BlockSpec((tr, cdim), lambda i: (i, 0))
    return pl.pallas_call(
        body, name=name, grid=(r // tr,), out_shape=[jax.ShapeDtypeStruct((r, cdim), F32)] * 3,
        in_specs=[spec] * 4, out_specs=[spec] * 3, compiler_params=_params(1),
    )(w, g, m, v)


def _pack(arrs, width=D, row_mult=8):
    parts = []
    for a in arrs:
        flat = a.reshape(-1)
        pad = (-flat.shape[0]) % width
        parts.append(jnp.pad(flat, (0, pad)))
    flat = jnp.concatenate(parts)
    pad = (-flat.shape[0]) % (width * row_mult)
    return jnp.pad(flat, (0, pad)).reshape(-1, width)


def _unpack_gathered(buf, shapes, width=D):
    out, row = [], 0
    for shp in shapes:
        size = math.prod(shp)
        nrow = -(-size // width)
        out.append(buf[:, row:row + nrow].reshape(NDEV, -1)[:, :size].reshape((NDEV,) + tuple(shp)))
        row += nrow
    return out


def _block_diag(w):
    hn, dh, _ = w.shape
    eye = jnp.eye(hn, dtype=w.dtype)
    return (w[:, :, None, :] * eye[:, None, :, None]).reshape(hn * dh, hn * dh)


def _diag_blocks(wfull, hn):
    dh = wfull.shape[0] // hn
    keep = np.eye(hn, dtype=bool)[:, None, :, None]
    return jnp.where(keep, wfull.reshape(hn, dh, hn, dh), 0.0).sum(axis=2)


def _pad_rows(a, rows):
    return jnp.pad(a, ((0, rows - a.shape[0]), (0, 0)))


def kernel(x, c, w_ada, b_ada, w_in, lru_conv_w, lru_conv_b, lru_w_r, lru_b_r, lru_w_i, lru_b_i, lru_lambda, conv_w, conv_b, conv_norm_g, conv_norm_b, w_out, ln1_g, ln1_b, ffn_w_up, ffn_conv_w, ffn_conv_b, ffn_w_down, ln2_g, ln2_b, loss_target, m_w_ada, m_b_ada, m_w_in, m_lru_conv_w, m_lru_conv_b, m_lru_w_r, m_lru_b_r, m_lru_w_i, m_lru_b_i, m_lru_lambda, m_conv_w, m_conv_b, m_conv_norm_g, m_conv_norm_b, m_w_out, m_ln1_g, m_ln1_b, m_ffn_w_up, m_ffn_conv_w, m_ffn_conv_b, m_ffn_w_down, m_ln2_g, m_ln2_b, v_w_ada, v_b_ada, v_w_in, v_lru_conv_w, v_lru_conv_b, v_lru_w_r, v_lru_b_r, v_lru_w_i, v_lru_b_i, v_lru_lambda, v_conv_w, v_conv_b, v_conv_norm_g, v_conv_norm_b, v_w_out, v_ln1_g, v_ln1_b, v_ffn_w_up, v_ffn_conv_w, v_ffn_conv_b, v_ffn_w_down, v_ln2_g, v_ln2_b):
    weights = dict(w_ada=w_ada, b_ada=b_ada, w_in=w_in, lru_conv_w=lru_conv_w, lru_conv_b=lru_conv_b, lru_w_r=lru_w_r,
                   lru_b_r=lru_b_r, lru_w_i=lru_w_i, lru_b_i=lru_b_i, lru_lambda=lru_lambda, conv_w=conv_w, conv_b=conv_b,
                   conv_norm_g=conv_norm_g, conv_norm_b=conv_norm_b, w_out=w_out, ln1_g=ln1_g, ln1_b=ln1_b,
                   ffn_w_up=ffn_w_up, ffn_conv_w=ffn_conv_w, ffn_conv_b=ffn_conv_b, ffn_w_down=ffn_w_down, ln2_g=ln2_g,
                   ln2_b=ln2_b)
    mom_m = dict(w_ada=m_w_ada, b_ada=m_b_ada, w_in=m_w_in, lru_conv_w=m_lru_conv_w, lru_conv_b=m_lru_conv_b,
                 lru_w_r=m_lru_w_r, lru_b_r=m_lru_b_r, lru_w_i=m_lru_w_i, lru_b_i=m_lru_b_i, lru_lambda=m_lru_lambda,
                 conv_w=m_conv_w, conv_b=m_conv_b, conv_norm_g=m_conv_norm_g, conv_norm_b=m_conv_norm_b, w_out=m_w_out,
                 ln1_g=m_ln1_g, ln1_b=m_ln1_b, ffn_w_up=m_ffn_w_up, ffn_conv_w=m_ffn_conv_w, ffn_conv_b=m_ffn_conv_b,
                 ffn_w_down=m_ffn_w_down, ln2_g=m_ln2_g, ln2_b=m_ln2_b)
    mom_v = dict(w_ada=v_w_ada, b_ada=v_b_ada, w_in=v_w_in, lru_conv_w=v_lru_conv_w, lru_conv_b=v_lru_conv_b,
                 lru_w_r=v_lru_w_r, lru_b_r=v_lru_b_r, lru_w_i=v_lru_w_i, lru_b_i=v_lru_b_i, lru_lambda=v_lru_lambda,
                 conv_w=v_conv_w, conv_b=v_conv_b, conv_norm_g=v_conv_norm_g, conv_norm_b=v_conv_norm_b, w_out=v_w_out,
                 ln1_g=v_ln1_g, ln1_b=v_ln1_b, ffn_w_up=v_ffn_w_up, ffn_conv_w=v_ffn_conv_w, ffn_conv_b=v_ffn_conv_b,
                 ffn_w_down=v_ffn_w_down, ln2_g=v_ln2_g, ln2_b=v_ln2_b)
    names = list(weights)
    bl, seq, _ = x.shape
    ntok = bl * seq
    me = 4 * lax.axis_index("x") + 2 * lax.axis_index("y") + lax.axis_index("c")

    small_shapes = [(bl, D), (LRU_K, LW // NDEV), (CONV_K, CW // NDEV), (FFN_K, DFF // NDEV)]
    small = _pack([c, lru_conv_w[0], conv_w[0], ffn_conv_w[0]], width=128)
    n_small = small.shape[0]
    small_all, win_t, wout_b = _all_gather(
        [small, w_in[0].T.astype(MXU_DTYPE), w_out[0].astype(MXU_DTYPE)], "gather_small_and_mixer_weights")
    small_all = small_all.reshape(NDEV, n_small, 128)
    c_parts, k4_parts, k31_parts, k3_parts = _unpack_gathered(small_all, small_shapes, width=128)
    c_all = c_parts.reshape(NDEV * bl, D)
    lru_conv_w_f = k4_parts.transpose(1, 0, 2).reshape(LRU_K, LW)
    conv_w_f = k31_parts.transpose(1, 0, 2).reshape(CONV_K, CW)
    ffn_conv_w_f = k3_parts.transpose(1, 0, 2).reshape(FFN_K, DFF)

    ncol = w_ada.shape[2]
    b_ada_loc = lax.dynamic_slice(b_ada, (0, me * ncol), (1, ncol))
    mod_cols = _ada_fwd(c_all, w_ada[0], b_ada_loc)
    (mod_all,) = _all_gather([mod_cols], "gather_mod")
    mod_all = mod_all.reshape(NDEV, NDEV * bl, ncol)
    mod_mine = lax.dynamic_slice(mod_all, (0, me * bl, 0), (NDEV, bl, ncol))
    mods = mod_mine.transpose(1, 0, 2).reshape(bl, 6, D)
    mods = jnp.pad(mods, ((0, 0), (0, 2), (0, 0)))

    ffn_shards = [ffn_w_up[0].T.astype(MXU_DTYPE), ffn_w_down[0].astype(MXU_DTYPE)]

    wr_bd = _block_diag(lru_w_r[0]).astype(MXU_DTYPE)
    wi_bd = _block_diag(lru_w_i[0]).astype(MXU_DTYPE)
    mavg_np = np.kron(np.eye(CW // HEAD, dtype=np.float32), np.full((HEAD, HEAD), 1.0 / HEAD, np.float32))
    mavg = jnp.asarray(mavg_np, MXU_DTYPE)
    mavg2 = jnp.asarray(np.concatenate([mavg_np, mavg_np], axis=0), MXU_DTYPE)
    wri = jnp.concatenate([wr_bd, wi_bd], axis=1)
    va = jnp.concatenate([lru_conv_w_f, lru_conv_b, lru_b_r, lru_b_i, lru_lambda], axis=0)
    vb = _pad_rows(jnp.concatenate([conv_w_f, conv_b, conv_norm_g, conv_norm_b], axis=0), 40)
    ln1 = _pad_rows(jnp.concatenate([ln1_g, ln1_b], axis=0), 8)
    ln2 = _pad_rows(jnp.concatenate([ln2_g, ln2_b], axis=0), 8)
    f3 = _pad_rows(jnp.concatenate([ffn_conv_w_f, ffn_conv_b], axis=0), 8)

    proj, h, vc, y_b, mixed, x1, u1_b, wup_t, wdown_b = _mixer_fwd(
        x, mods, win_t, wout_b, wri, mavg2, va, vb, ln1, ffn_shards)
    hh, z_b, u2_b, y2, loss_part = _ffn_fwd(x1, mods, loss_target, wup_t, wdown_b, f3, ln2)

    dx1, dy2_b, dhh_b, dln2, df3, dmod_b = _ffn_bwd(x1, y2, loss_target, hh, mods, wup_t, wdown_b, f3, ln2)
    g_down = _weight_grad(z_b.reshape(ntok, DFF), dy2_b.reshape(ntok, D), FF_CHUNK, "wgrad_down", out_dtype=WIRE_DTYPE)
    g_up_t = _weight_grad(dhh_b.reshape(ntok, 2 * DFF), u2_b.reshape(ntok, D), FF_CHUNK, "wgrad_up",
                          out_dtype=WIRE_DTYPE)
    grad_x, dproj_b, dm_b, g_a, g_w31, g_wr, g_wi, dln1, dmod_a, s_up, s_down = _mixer_bwd(
        x, dx1, proj, h, vc, mixed, mods, win_t, wout_b, wri, mavg2, va, vb, ln1, [g_up_t, g_down])
    dproj_2d, u1_2d = dproj_b.reshape(ntok, 2 * LW + 2 * CW), u1_b.reshape(ntok, D)
    blk_in = (2 * LW + 2 * CW) // NDEV
    g_out_b = _weight_grad(y_b.reshape(ntok, D), dm_b.reshape(ntok, D), D, "wgrad_out", out_dtype=WIRE_DTYPE)
    g_in_c0, s_out = _weight_grad(dproj_2d, u1_2d, blk_in, "wgrad_in_c0", out_dtype=WIRE_DTYPE, exchange=[g_out_b],
                                  col_phase=0, tok_tile=2 * WG_TOK_TILE)
    g_in_c1, s_in_c0 = _weight_grad(dproj_2d, u1_2d, blk_in, "wgrad_in_c1", out_dtype=WIRE_DTYPE, exchange=[g_in_c0],
                                    exchange_half=0, col_phase=1, tok_tile=2 * WG_TOK_TILE)

    heads = LW // HEAD
    packed = _pack_small_grads(
        dmod_a.reshape(bl * 8, D), dmod_b.reshape(bl * 8, D), g_a, g_w31, dln1, dln2, df3,
        _diag_blocks(g_wr, heads).reshape(32, D), _diag_blocks(g_wi, heads).reshape(32, D), loss_part)
    n_rows = packed.shape[0]
    (s_in_c1,), (packed_all,) = _exchange_and_gather(
        [g_in_c1], [packed], "exchange_w_in_grads_gather_small_grads", half=1)
    packed_all = packed_all.reshape(NDEV, n_rows, D)
    full = _sum_small_grads(packed_all, bl)
    loss = jnp.sum(full.pop("loss")) * (0.5 / D)
    dmod_all = packed_all[:, 0:6 * bl].reshape(NDEV * bl, 6 * D)

    grads, delta, new_m, new_v = {}, {}, {}, {}
    for k, slots in (("w_out", s_out), ("ffn_w_down", s_down)):
        g_, d_, m_, v_ = _sum_adamw(slots, weights[k][0], mom_m[k][0], mom_v[k][0], "sum_adamw_" + k)
        grads[k], delta[k], new_m[k], new_v[k] = g_[None], d_[None], m_[None], v_[None]
    g_w_ada, g_b_ada = _ada_bwd(c_all, dmod_all, lax.dynamic_slice(dmod_all, (0, me * ncol), (NDEV * bl, ncol)))
    grads["w_ada"], grads["b_ada"] = g_w_ada[None], g_b_ada
    grads["w_in"] = _sum_slots(s_in_c0, "sum_w_in", slots_c1=s_in_c1).T[None]
    grads["ffn_w_up"] = _sum_slots(s_up, "sum_w_up").T[None]
    for k, gk in full.items():
        if k in ("lru_conv_w", "conv_w", "ffn_conv_w"):
            nloc = gk.shape[1] // NDEV
            gk = lax.dynamic_slice(gk, (0, me * nloc), (gk.shape[0], nloc))
        grads[k] = gk.reshape(weights[k].shape)

    big = ("w_ada", "w_in", "w_out", "ffn_w_up", "ffn_w_down")
    for k in ("w_ada", "w_in", "ffn_w_up"):
        d_, m_, v_ = _adamw(weights[k][0], grads[k][0], mom_m[k][0], mom_v[k][0], "adamw_" + k)
        delta[k], new_m[k], new_v[k] = d_[None], m_[None], v_[None]
    small_names = [k for k in names if k not in big]
    d_, m_, v_ = _adamw_many([weights[k] for k in small_names], [grads[k] for k in small_names],
                             [mom_m[k] for k in small_names], [mom_v[k] for k in small_names], "adamw_small")
    for k, dk, mk, vk in zip(small_names, d_, m_, v_):
        delta[k], new_m[k], new_v[k] = dk, mk, vk

    return (loss, grad_x, *[grads[k] for k in names], *[delta[k] for k in names], *[new_m[k] for k in names],
            *[new_v[k] for k in names])
```

```python
import functools
import math

import jax
import jax.numpy as jnp
import numpy as np
from jax import lax
from jax.experimental import pallas as pl
from jax.experimental.pallas import tpu as pltpu

NDEV = 8
D = 1024
LW = 512
CW = 512
HEAD = 64
DFF = 2816
FF_CHUNK = 1408
LRU_K = 4
CONV_K = 31
FFN_K = 3
LRU_C = 8.0
ALPHA = (2 * 1) ** 0.25
LN_EPS = 1e-5
ADAM_LR = 0.001
ADAM_B1 = 0.9
ADAM_B2 = 0.999
ADAM_EPS = 1e-08
ADAM_WD = 0.01
ADAM_STEP = 10

MXU_DTYPE = jnp.bfloat16
WIRE_DTYPE = jnp.bfloat16
TOK_TILE = 256
WG_TOK_TILE = 1024
VMEM_LIMIT = 60 * 1024 * 1024
HALO31 = 32
HALO = 8

F32 = jnp.float32
MESH = pl.DeviceIdType.MESH


def _sigmoid(x):
    return 0.5 * jnp.tanh(0.5 * x) + 0.5


def _shifted_rows(ext_ref, rot_ref, n_rows):
    span = rot_ref.shape[1]
    for r in range(1, 8):
        rot_ref[r - 1] = ext_ref[pl.ds(r, span), :]

    def window(o, r0=0, cols=slice(None)):
        q, r = divmod(o, 8)
        if r == 0:
            return ext_ref[pl.ds(8 * q + r0, n_rows), cols]
        return rot_ref[r - 1, pl.ds(8 * q + r0, n_rows), cols]

    return window


def _store_hilo(hilo_ref, rows, c0, v):
    half = hilo_ref.shape[1] // 2
    hi = v.astype(hilo_ref.dtype)
    hilo_ref[rows, c0:c0 + v.shape[1]] = hi
    hilo_ref[rows, half + c0:half + c0 + v.shape[1]] = (v - hi.astype(F32)).astype(hilo_ref.dtype)


def _scan_bwd_blocks(a_ref, g_ref, car, n_rows):
    row = lax.broadcasted_iota(jnp.int32, (8, LANE), 0)
    for c0 in range(0, a_ref.shape[1], LANE):
        cols = slice(c0, c0 + LANE)
        a_next = jnp.broadcast_to(car[0:1, cols], (8, LANE))
        g_next = jnp.broadcast_to(car[1:2, cols], (8, LANE))
        for r0 in range(n_rows - 8, -1, -8):
            a_blk, g = a_ref[r0:r0 + 8, cols], g_ref[r0:r0 + 8, cols]
            bm = jnp.where(row < 7, pltpu.roll(a_blk, 7, 0), a_next)
            for d in (1, 2, 4):
                keep = row < 8 - d
                g = g + bm * jnp.where(keep, pltpu.roll(g, 8 - d, 0), 0.0)
                bm = bm * jnp.where(keep, pltpu.roll(bm, 8 - d, 0), 1.0)
            g = g + bm * g_next
            g_ref[r0:r0 + 8, cols] = g
            a_next = jnp.broadcast_to(a_blk[0:1, :], (8, LANE))
            g_next = jnp.broadcast_to(g[0:1, :], (8, LANE))
        car[0:1, cols] = a_next[0:1, :]
        car[1:2, cols] = g_next[0:1, :]


def _scan_fwd_blocks(a_ref, u_ref, hcar, n_rows):
    row = lax.broadcasted_iota(jnp.int32, (8, LANE), 0)
    for c0 in range(0, a_ref.shape[1], LANE):
        cols = slice(c0, c0 + LANE)
        h_prev = jnp.broadcast_to(hcar[0:1, cols], (8, LANE))
        for r0 in range(0, n_rows, 8):
            a, u = a_ref[r0:r0 + 8, cols], u_ref[r0:r0 + 8, cols]
            for d in (1, 2, 4):
                keep = row >= d
                u = a * jnp.where(keep, pltpu.roll(u, d, 0), 0.0) + u
                a = a * jnp.where(keep, pltpu.roll(a, d, 0), 1.0)
            u = u + a * h_prev
            u_ref[r0:r0 + 8, cols] = u
            h_prev = jnp.broadcast_to(u[7:8, :], (8, LANE))
        hcar[0:1, cols] = h_prev[0:1, :]


def _dot(a, b):
    return jnp.dot(a, b, preferred_element_type=F32)


def _dot_nt(a, b):
    return lax.dot_general(a, b, (((1,), (1,)), ((), ())), preferred_element_type=F32)


def _dot_tn(a, b):
    return lax.dot_general(a, b, (((0,), (0,)), ((), ())), preferred_element_type=F32)


def _colsum(v):
    return jnp.sum(v, axis=0, keepdims=True)


def _rowmean(v):
    return jnp.mean(v, axis=-1, keepdims=True)


_GELU_C0 = math.sqrt(2.0 / math.pi)
_GELU_C1 = 0.044715


def _gelu_and_grad(x):
    x2 = x * x
    th = jnp.tanh(_GELU_C0 * (x + _GELU_C1 * x * x2))
    ge = 0.5 * x * (1.0 + th)
    dge = 0.5 * (1.0 + th) + 0.5 * x * (1.0 - th * th) * (_GELU_C0 * (1.0 + 3.0 * _GELU_C1 * x2))
    return ge, dge


def _softplus(x):
    return jnp.maximum(x, 0.0) + jnp.log1p(jnp.exp(-jnp.abs(x)))


def _layer_norm_stats(r):
    mu = _rowmean(r)
    dl = r - mu
    var = _rowmean(dl * dl)
    rstd = lax.rsqrt(var + LN_EPS)
    return dl * rstd, rstd


ROW_BLK = 32
LANE = 128


def _row_blocks(n_rows, rb, body, init):
    carry = init
    for j in range(n_rows // rb):
        carry = body(j * rb, carry)
    return carry


def _fold8(v):
    r, c = v.shape
    return v if r == 8 else v.reshape(r // 8, 8, c).sum(axis=0)


def _cast_rows(src_ref, dst_ref, n_rows):
    rb = min(16, n_rows)

    def rows(r0, carry):
        dst_ref[pl.ds(r0, rb), :] = src_ref[pl.ds(r0, rb), :].astype(dst_ref.dtype)
        return carry

    _row_blocks(n_rows, rb, rows, ())


def _full(shape):
    nd = len(shape)
    return pl.BlockSpec(shape, lambda *_: (0,) * nd)


_ANY = pl.BlockSpec(memory_space=pl.ANY)


def _params(n_grid):
    return pltpu.CompilerParams(dimension_semantics=("arbitrary",) * n_grid, vmem_limit_bytes=VMEM_LIMIT)


def _my_place():
    return lax.axis_index("x"), lax.axis_index("y"), lax.axis_index("c")


def _all_gather(arrays, name):
    n_arr = len(arrays)

    def body(*refs):
        start, forward, finish = _gather_steps(refs[:n_arr], refs[n_arr:2 * n_arr], *refs[2 * n_arr:])
        start()
        forward()
        finish()

    return pl.pallas_call(
        body, name=name, out_shape=_gather_out_shapes(arrays),
        in_specs=[_ANY] * n_arr, out_specs=[_ANY] * n_arr, scratch_shapes=_comm_sems(n_arr),
    )(*arrays)


def _gather_out_shapes(arrays):
    return [jax.ShapeDtypeStruct((NDEV * a.shape[0], a.shape[1]), a.dtype) for a in arrays]


def _comm_sems(n_arr):
    return [pltpu.SemaphoreType.DMA((n_arr, 7)), pltpu.SemaphoreType.DMA((n_arr, 7)), pltpu.SemaphoreType.DMA((n_arr,))]


def _gather_steps(x_refs, out_refs, send_sems, recv_sems, local_sems):
    n_arr = len(x_refs)
    x, y, c = _my_place()
    me, sibling = (x, y, c), (x, y, 1 - c)
    chips = [(1 - x, y), (x, 1 - y), (1 - x, 1 - y)]

    def rows(k, px, py, pc):
        m = x_refs[k].shape[0]
        return out_refs[k].at[pl.ds((4 * px + 2 * py + pc) * m, m), :]

    def copy(k, s, block, to, src=None):
        return pltpu.make_async_remote_copy(
            src_ref=rows(k, *block) if src is None else src, dst_ref=rows(k, *block),
            send_sem=send_sems.at[k, s], recv_sem=recv_sems.at[k, s], device_id=to, device_id_type=MESH)

    def mine():
        return [pltpu.make_async_copy(x_refs[k], rows(k, *me), local_sems.at[k]) for k in range(n_arr)]

    def first():
        cps = []
        for k in range(n_arr):
            cps.append(copy(k, 0, me, sibling, src=x_refs[k]))
            cps += [copy(k, 1 + j, me, (*chip, c), src=x_refs[k]) for j, chip in enumerate(chips)]
        return cps

    def passed():
        return [copy(k, 4 + j, (*chip, c), sibling) for k in range(n_arr) for j, chip in enumerate(chips)]

    def start():
        for cp in mine() + first():
            cp.start()

    def forward():
        fwd = passed()
        for k in range(n_arr):
            for j, chip in enumerate(chips):
                copy(k, 1 + j, (*chip, c), me).wait_recv()
                fwd[3 * k + j].start()

    def finish():
        for k in range(n_arr):
            copy(k, 0, sibling, me).wait_recv()
            for j, chip in enumerate(chips):
                copy(k, 4 + j, (*chip, 1 - c), me).wait_recv()
        for cp in first() + passed():
            cp.wait_send()
        for cp in mine():
            cp.wait()

    return start, forward, finish


def _exchange_and_gather(ex_arrays, ga_arrays, name, half=None):
    n_e, n_g = len(ex_arrays), len(ga_arrays)

    def body(*refs):
        e_in, g_in, refs = refs[:n_e], refs[n_e:n_e + n_g], refs[n_e + n_g:]
        e_out, g_out, sems = refs[:n_e], refs[n_e:n_e + n_g], refs[n_e + n_g:]
        e_start, e_finish = _exchange_steps(e_in, e_out, *sems[:3], half=half)
        g_start, g_forward, g_finish = _gather_steps(g_in, g_out, *sems[3:])
        g_start()
        e_start()
        g_forward()
        g_finish()
        e_finish()

    res = pl.pallas_call(
        body, name=name, out_shape=_exchange_out_shapes(ex_arrays, half) + _gather_out_shapes(ga_arrays),
        in_specs=[_ANY] * (n_e + n_g), out_specs=[_ANY] * (n_e + n_g),
        scratch_shapes=_comm_sems(n_e) + _comm_sems(n_g),
    )(*ex_arrays, *ga_arrays)
    return res[:n_e], res[n_e:]


def _exchange_out_shapes(arrays, half=None):
    n_blocks = NDEV if half is None else NDEV // 2
    return [jax.ShapeDtypeStruct((NDEV, a.shape[0] // n_blocks, a.shape[1]), a.dtype) for a in arrays]


def _exchange_steps(g_refs, out_refs, send_sems, recv_sems, local_sems, half=None):
    n_arr = len(g_refs)
    x, y, c = _my_place()
    me = 4 * x + 2 * y + c
    receives = True if half is None else c == half

    def when(cond, fn):
        if cond is True:
            fn()
        else:
            pl.when(cond)(fn)

    def sends(rel):
        return True if half is None else (c ^ (rel & 1)) == half

    def copy(k, rel):
        px, py, pc = x ^ ((rel >> 2) & 1), y ^ ((rel >> 1) & 1), c ^ (rel & 1)
        m = out_refs[k].shape[1]
        blk = 4 * px + 2 * py + pc if half is None else 2 * px + py
        return pltpu.make_async_remote_copy(
            src_ref=g_refs[k].at[pl.ds(blk * m, m), :], dst_ref=out_refs[k].at[me],
            send_sem=send_sems.at[k, rel - 1], recv_sem=recv_sems.at[k, rel - 1],
            device_id=(px, py, pc), device_id_type=MESH)

    def local(k):
        m = out_refs[k].shape[1]
        blk = me if half is None else 2 * x + y
        return pltpu.make_async_copy(g_refs[k].at[pl.ds(blk * m, m), :], out_refs[k].at[me], local_sems.at[k])

    def start():
        for k in range(n_arr):
            when(receives, local(k).start)
            for rel in range(1, NDEV):
                when(sends(rel), copy(k, rel).start)

    def finish():
        for k in range(n_arr):
            for rel in range(1, NDEV):
                when(sends(rel), copy(k, rel).wait_send)
                when(receives, copy(k, rel).wait_recv)
            when(receives, local(k).wait)

    return start, finish


def _ada_fwd(c_all, w_ada_loc, b_ada_loc):
    def body(c_ref, w_ref, b_ref, o_ref):
        cv = c_ref[...]
        ca = (cv * _sigmoid(cv)).astype(MXU_DTYPE)
        o_ref[...] = _dot(ca, w_ref[...].astype(MXU_DTYPE)) + b_ref[...]

    return pl.pallas_call(
        body, name="ada_fwd", out_shape=jax.ShapeDtypeStruct((c_all.shape[0], w_ada_loc.shape[1]), F32),
        in_specs=[_full(c_all.shape), _full(w_ada_loc.shape), _full(b_ada_loc.shape)],
        out_specs=_full((c_all.shape[0], w_ada_loc.shape[1])),
        compiler_params=pltpu.CompilerParams(vmem_limit_bytes=VMEM_LIMIT),
    )(c_all, w_ada_loc, b_ada_loc)


def _ada_bwd(c_all, dmod_all, dmod_cols):
    def body(c_ref, da_ref, d_ref, o_ref, b_ref):
        cv = c_ref[...]
        ca = (cv * _sigmoid(cv)).astype(MXU_DTYPE)
        o_ref[...] = _dot_tn(ca, d_ref[...].astype(MXU_DTYPE))
        b_ref[...] = _colsum(da_ref[...])

    return pl.pallas_call(
        body, name="ada_bwd",
        out_shape=[jax.ShapeDtypeStruct((c_all.shape[1], dmod_cols.shape[1]), F32),
                   jax.ShapeDtypeStruct((1, dmod_all.shape[1]), F32)],
        in_specs=[_full(c_all.shape), _full(dmod_all.shape), _full(dmod_cols.shape)],
        out_specs=[_full((c_all.shape[1], dmod_cols.shape[1])), _full((1, dmod_all.shape[1]))],
        compiler_params=pltpu.CompilerParams(vmem_limit_bytes=VMEM_LIMIT),
    )(c_all, dmod_all, dmod_cols)


def _mixer_fwd(x, mods, win_t, wout, wri, mavg2, va, vb, ln1, ffn_shards):
    bl, seq, _ = x.shape
    t = min(TOK_TILE, seq)
    nt = seq // t
    n_g = len(ffn_shards)
    rb = min(ROW_BLK, t)

    def body(x_ref, mod_ref, win_hbm, wout_hbm, wri_ref, mavg_ref, va_ref, vb_ref, ln1_ref, *rest):
        shard_refs, rest = rest[:n_g], rest[n_g:]
        proj_ref, h_ref, vc_ref, y_ref, mixed_ref, x1_ref, u1_ref = rest[:7]
        gathered_refs, rest = rest[7:7 + n_g], rest[7 + n_g:]
        (win_v, wout_v, xa_ext, vg_ext, rot_a, hcar, sems, rot_c, xc_scr, a_scr, st_a, st_b, xcb_scr, gates, hilo,
         g_send, g_recv, g_local) = rest
        b, i = pl.program_id(0), pl.program_id(1)
        step = b * nt + i
        g_start, g_forward, g_finish = _gather_steps(shard_refs, gathered_refs, g_send, g_recv, g_local)

        @pl.when(step == 0)
        def _():
            g_start()
            cps = [pltpu.make_async_copy(win_hbm, win_v, sems.at[0]), pltpu.make_async_copy(wout_hbm, wout_v, sems.at[1])]
            for cp in cps:
                cp.start()
            for cp in cps:
                cp.wait()

        pl.when(step == (bl * nt) // 2)(g_forward)

        @pl.when(i == 0)
        def _():
            xa_ext[0:HALO, :] = jnp.zeros((HALO, LW), F32)
            vg_ext[0:HALO31, :] = jnp.zeros((HALO31, CW), F32)
            hcar[...] = jnp.zeros_like(hcar)

        blocks = [(r0, c0) for c0 in range(0, LW, LANE) for r0 in range(0, t, rb)]

        def in_rows(r0, acc):
            rows = pl.ds(r0, 16)
            u1_ref[rows, :] = (x_ref[rows, :] * (1.0 + mod_ref[1:2, :]) + mod_ref[0:1, :]).astype(MXU_DTYPE)
            return acc

        _row_blocks(t, min(16, t), in_rows, ())
        proj_ref[...] = _dot_nt(u1_ref[...], win_v[...])

        xa_ext[HALO:HALO + t, :] = proj_ref[:, 0:LW]
        for k in range(LRU_K - 1):
            rot_c[k] = xa_ext[pl.ds(HALO - (LRU_K - 1) + k, t), :]
        for r0, c0 in blocks:
            rows, cols = pl.ds(r0, rb), slice(c0, c0 + LANE)
            xc = va_ref[4:5, cols] + va_ref[LRU_K - 1:LRU_K, cols] * xa_ext[pl.ds(HALO + r0, rb), cols]
            for k in range(LRU_K - 1):
                xc = xc + va_ref[k:k + 1, cols] * rot_c[k, rows, cols]
            xc_scr[rows, cols] = xc
            xcb_scr[rows, cols] = xc.astype(MXU_DTYPE)
        xa_ext[0:HALO, :] = xa_ext[t:t + HALO, :]
        gates[...] = _dot(xcb_scr[...], wri_ref[...])
        for r0, c0 in blocks:
            rows, cols = pl.ds(r0, rb), slice(c0, c0 + LANE)
            r = _sigmoid(gates[rows, cols] + va_ref[5:6, cols])
            ig = _sigmoid(gates[rows, LW + c0:LW + c0 + LANE] + va_ref[6:7, cols])
            la = (-LRU_C) * r * _softplus(-va_ref[7:8, cols])
            a = jnp.exp(la)
            a_scr[rows, cols] = a
            h_ref[rows, cols] = jnp.sqrt(-jnp.tanh(la) * (a * a + 1.0)) * (ig * xc_scr[rows, cols])
        _scan_fwd_blocks(a_scr, h_ref, hcar, t)

        for r0, c0 in blocks:
            rows, cols, pc = pl.ds(r0, rb), slice(c0, c0 + LANE), 2 * LW + c0
            vg_ext[pl.ds(HALO31 + r0, rb), cols] = (
                proj_ref[rows, pc:pc + LANE] * _sigmoid(proj_ref[rows, pc + CW:pc + CW + LANE]))
        vg_win = _shifted_rows(vg_ext, rot_a, rb)
        for r0, c0 in blocks:
            rows, cols = pl.ds(r0, rb), slice(c0, c0 + LANE)
            vc = vb_ref[CONV_K:CONV_K + 1, cols]
            for k in range(CONV_K):
                vc = vc + vb_ref[k:k + 1, cols] * vg_win(HALO31 - (CONV_K - 1) + k, r0, cols)
            vc_ref[rows, cols] = vc
            _store_hilo(hilo, rows, c0, vc)
        vg_ext[0:HALO31, :] = vg_ext[t:t + HALO31, :]
        mavg2 = mavg_ref[...]
        st_a[...] = _dot(hilo[...], mavg2)
        for r0, c0 in blocks:
            rows, cols = pl.ds(r0, rb), slice(c0, c0 + LANE)
            dl = vc_ref[rows, cols] - st_a[rows, cols]
            st_a[rows, cols] = dl
            _store_hilo(hilo, rows, c0, dl * dl)
        st_b[...] = _dot(hilo[...], mavg2)
        for r0, c0 in blocks:
            rows, cols = pl.ds(r0, rb), slice(c0, c0 + LANE)
            ge, _ = _gelu_and_grad(proj_ref[rows, LW + c0:LW + c0 + LANE])
            y_ref[rows, cols] = (ge * h_ref[rows, cols]).astype(MXU_DTYPE)
            yl = (st_a[rows, cols] * lax.rsqrt(st_b[rows, cols] + LN_EPS) * vb_ref[CONV_K + 1:CONV_K + 2, cols]
                  + vb_ref[CONV_K + 2:CONV_K + 3, cols])
            y_ref[rows, LW + c0:LW + c0 + LANE] = (yl * _sigmoid(yl)).astype(MXU_DTYPE)
        mixed_ref[...] = _dot(y_ref[...], wout_v[...])

        def out_rows(r0, acc):
            rows = pl.ds(r0, 8)
            xh, _ = _layer_norm_stats(ALPHA * x_ref[rows, :] + (1.0 + mod_ref[2:3, :]) * mixed_ref[rows, :])
            x1_ref[rows, :] = xh * ln1_ref[0:1, :] + ln1_ref[1:2, :]
            return acc

        _row_blocks(t, 8, out_rows, ())

        pl.when(step == bl * nt - 1)(g_finish)

    tok = lambda w: pl.BlockSpec((None, t, w), lambda b, i: (b, i, 0))
    outs = [(2 * LW + 2 * CW, F32), (LW, F32), (CW, F32), (D, MXU_DTYPE), (D, F32), (D, F32), (D, MXU_DTYPE)]
    return pl.pallas_call(
        body, name="mixer_fwd", grid=(bl, nt),
        out_shape=[jax.ShapeDtypeStruct((bl, seq, w), dt) for w, dt in outs] + _gather_out_shapes(ffn_shards),
        in_specs=[tok(D), pl.BlockSpec((None, 8, D), lambda b, i: (b, 0, 0)), _ANY, _ANY,
                  _full(wri.shape), _full(mavg2.shape), _full(va.shape), _full(vb.shape), _full(ln1.shape)]
        + [_ANY] * n_g,
        out_specs=[tok(w) for w, _ in outs] + [_ANY] * n_g,
        scratch_shapes=[pltpu.VMEM(win_t.shape, MXU_DTYPE), pltpu.VMEM(wout.shape, MXU_DTYPE),
                        pltpu.VMEM((t + HALO, LW), F32), pltpu.VMEM((t + HALO31, CW), F32),
                        pltpu.VMEM((7, t + HALO31 - 8, CW), F32), pltpu.VMEM((8, LW), F32),
                        pltpu.SemaphoreType.DMA((2,)), pltpu.VMEM((LRU_K - 1, t, LW), F32)]
        + [pltpu.VMEM((t, LW), F32)] * 4
        + [pltpu.VMEM((t, LW), MXU_DTYPE), pltpu.VMEM((t, 2 * LW), F32), pltpu.VMEM((t, 2 * CW), MXU_DTYPE)]
        + _comm_sems(n_g),
        compiler_params=_params(2),
    )(x, mods, win_t, wout, wri, mavg2, va, vb, ln1, *ffn_shards)


def _ffn_fwd(x1, mods, tgt, wup_t, wdown, f3, ln2):
    bl, seq, _ = x1.shape
    t = min(TOK_TILE, seq)
    nt = seq // t
    n_chunk = DFF // FF_CHUNK

    rb = min(ROW_BLK, t)

    def body(x1_ref, mod_ref, tgt_ref, wup_hbm, wdown_hbm, f3_ref, ln2_ref,
             hh_ref, z_ref, u2_ref, y2_ref, loss_ref,
             wup_v, wdown_v, g_ext, rot, sems):
        b, i = pl.program_id(0), pl.program_id(1)

        @pl.when((b == 0) & (i == 0))
        def _():
            cps = [pltpu.make_async_copy(wup_hbm, wup_v, sems.at[0]), pltpu.make_async_copy(wdown_hbm, wdown_v, sems.at[1])]
            for cp in cps:
                cp.start()
            for cp in cps:
                cp.wait()
            loss_ref[...] = jnp.zeros_like(loss_ref)

        @pl.when(i == 0)
        def _():
            for ch in range(n_chunk):
                g_ext[ch, 0:HALO, :] = jnp.zeros((HALO, FF_CHUNK), F32)

        def in_rows(r0, acc):
            rows = pl.ds(r0, 16)
            u2_ref[rows, :] = (x1_ref[rows, :] * (1.0 + mod_ref[4:5, :]) + mod_ref[3:4, :]).astype(MXU_DTYPE)
            return acc

        _row_blocks(t, min(16, t), in_rows, ())
        for ch in range(n_chunk):
            lo = ch * FF_CHUNK
            hh_ref[:, lo:lo + FF_CHUNK] = _dot_nt(u2_ref[...], wup_v[lo:lo + FF_CHUNK, :])
            hh_ref[:, DFF + lo:DFF + lo + FF_CHUNK] = _dot_nt(u2_ref[...], wup_v[DFF + lo:DFF + lo + FF_CHUNK, :])
            g_ext[ch, HALO:HALO + t, :] = hh_ref[:, DFF + lo:DFF + lo + FF_CHUNK]
            for k in range(FFN_K - 1):
                rot[k] = g_ext[ch, pl.ds(HALO - (FFN_K - 1) + k, t), :]
            for cb in range(FF_CHUNK // LANE):
                cols = slice(cb * LANE, (cb + 1) * LANE)
                fcols = slice(lo + cb * LANE, lo + (cb + 1) * LANE)
                for r0 in range(0, t, rb):
                    rows = pl.ds(r0, rb)
                    gc = (f3_ref[FFN_K:FFN_K + 1, fcols] + f3_ref[0:1, fcols] * rot[0, rows, cols]
                          + f3_ref[1:2, fcols] * rot[1, rows, cols]
                          + f3_ref[2:3, fcols] * g_ext[ch, pl.ds(HALO + r0, rb), cols])
                    z_ref[rows, fcols] = (gc * _sigmoid(gc) * hh_ref[rows, fcols]).astype(MXU_DTYPE)
            g_ext[ch, 0:HALO, :] = g_ext[ch, t:t + HALO, :]
        y2_ref[...] = _dot(z_ref[...], wdown_v[...])

        def out_rows(r0, acc):
            rows = pl.ds(r0, 8)
            xh, _ = _layer_norm_stats(ALPHA * x1_ref[rows, :] + (1.0 + mod_ref[5:6, :]) * y2_ref[rows, :])
            err = xh * ln2_ref[0:1, :] + ln2_ref[1:2, :] - tgt_ref[rows, :]
            return acc + err * err

        e2 = _row_blocks(t, 8, out_rows, jnp.zeros((8, D), F32))
        part = e2[:, 0:128]
        for j in range(1, D // 128):
            part = part + e2[:, 128 * j:128 * (j + 1)]
        loss_ref[...] += part

    tok = lambda w: pl.BlockSpec((None, t, w), lambda b, i: (b, i, 0))
    outs = [(2 * DFF, F32), (DFF, MXU_DTYPE), (D, MXU_DTYPE), (D, F32)]
    return pl.pallas_call(
        body, name="ffn_fwd", grid=(bl, nt),
        out_shape=[jax.ShapeDtypeStruct((bl, seq, w), dt) for w, dt in outs] + [jax.ShapeDtypeStruct((8, 128), F32)],
        in_specs=[tok(D), pl.BlockSpec((None, 8, D), lambda b, i: (b, 0, 0)), tok(D), _ANY, _ANY,
                  _full(f3.shape), _full(ln2.shape)],
        out_specs=[tok(w) for w, _ in outs] + [_full((8, 128))],
        scratch_shapes=[pltpu.VMEM(wup_t.shape, MXU_DTYPE), pltpu.VMEM(wdown.shape, MXU_DTYPE),
                        pltpu.VMEM((n_chunk, t + HALO, FF_CHUNK), F32), pltpu.VMEM((FFN_K - 1, t, FF_CHUNK), F32),
                        pltpu.SemaphoreType.DMA((2,))],
        compiler_params=_params(2),
    )(x1, mods, tgt, wup_t, wdown, f3, ln2)


def _ffn_bwd(x1, y2, tgt, hh, mods, wup_t, wdown, f3, ln2):
    bl, seq, _ = x1.shape
    t = min(TOK_TILE, seq)
    nt = seq // t
    n_chunk = DFF // FF_CHUNK
    rb = min(ROW_BLK, t)

    def body(x1_ref, y2_ref, tgt_ref, hh_ref, halo_ref, mod_ref, wup_hbm, wdown_hbm, f3_ref, ln2_ref,
             dx1_ref, dy2_ref, dhh_ref, dln2_ref, df3_ref, dmod_ref,
             wup_v, wdown_v, g_ext, dgc_ext, rot, dz_scr, wide_scr, sems):
        b, i = pl.program_id(0), pl.program_id(1)
        tt = nt - 1 - i

        @pl.when((b == 0) & (i == 0))
        def _():
            cps = [pltpu.make_async_copy(wup_hbm, wup_v, sems.at[0]), pltpu.make_async_copy(wdown_hbm, wdown_v, sems.at[1])]
            for cp in cps:
                cp.start()
            for cp in cps:
                cp.wait()
            dln2_ref[...] = jnp.zeros_like(dln2_ref)
            df3_ref[...] = jnp.zeros_like(df3_ref)

        @pl.when(i == 0)
        def _():
            dmod_ref[...] = jnp.zeros_like(dmod_ref)
            for ch in range(n_chunk):
                dgc_ext[ch, t:t + HALO, :] = jnp.zeros((HALO, FF_CHUNK), F32)

        def ln_rows(r0, acc):
            rows = pl.ds(r0, 8)
            x1v, y2v = x1_ref[rows, :], y2_ref[rows, :]
            gt = mod_ref[5:6, :]
            xh, rstd = _layer_norm_stats(ALPHA * x1v + (1.0 + gt) * y2v)
            g2 = ln2_ref[0:1, :]
            dx2 = (xh * g2 + ln2_ref[1:2, :] - tgt_ref[rows, :]) * (1.0 / D)
            dxh = dx2 * g2
            dr2 = rstd * (dxh - _rowmean(dxh) - xh * _rowmean(dxh * xh))
            wide_scr[rows, :] = (1.0 + gt) * dr2
            dx1_ref[rows, :] = ALPHA * dr2
            return acc[0] + dx2 * xh, acc[1] + dx2, acc[2] + dr2 * y2v

        zero_d = jnp.zeros((8, D), F32)
        acc = _row_blocks(t, 8, ln_rows, (zero_d, zero_d, zero_d))
        dln2_ref[0:1, :] += _colsum(acc[0])
        dln2_ref[1:2, :] += _colsum(acc[1])
        dmod_ref[2:3, :] += _colsum(acc[2])
        _cast_rows(wide_scr, dy2_ref, t)

        halo_keep = (tt > 0).astype(F32)
        zero_l = jnp.zeros((8, LANE), F32)
        for ch in range(n_chunk):
            lo = ch * FF_CHUNK
            dz_scr[...] = _dot_nt(dy2_ref[...], wdown_v[lo:lo + FF_CHUNK, :])
            g_ext[0:HALO, :] = halo_ref[:, lo:lo + FF_CHUNK] * halo_keep
            g_ext[HALO:HALO + t, :] = hh_ref[:, DFF + lo:DFF + lo + FF_CHUNK]
            for k in range(FFN_K - 1):
                rot[k] = g_ext[pl.ds(HALO - (FFN_K - 1) + k, t), :]
            for cb in range(FF_CHUNK // LANE):
                cols = slice(cb * LANE, (cb + 1) * LANE)
                fcols = slice(lo + cb * LANE, lo + (cb + 1) * LANE)

                def gate_rows(r0, acc, cols=cols, fcols=fcols, ch=ch):
                    rows = pl.ds(r0, rb)
                    taps = [rot[0, rows, cols], rot[1, rows, cols], g_ext[pl.ds(r0 + HALO, rb), cols]]
                    gc = f3_ref[FFN_K:FFN_K + 1, fcols]
                    for k in range(FFN_K):
                        gc = gc + f3_ref[k:k + 1, fcols] * taps[k]
                    s = _sigmoid(gc)
                    dz = dz_scr[rows, cols]
                    dhh_ref[rows, fcols] = (dz * (gc * s)).astype(MXU_DTYPE)
                    dgc = dz * hh_ref[rows, fcols] * (s * (1.0 + gc * (1.0 - s)))
                    dgc_ext[ch, rows, cols] = dgc
                    return tuple(acc[k] + _fold8(dgc * taps[k]) for k in range(FFN_K)) + (acc[FFN_K] + _fold8(dgc),)

                acc = _row_blocks(t, rb, gate_rows, (zero_l,) * (FFN_K + 1))
                for k in range(FFN_K + 1):
                    df3_ref[k:k + 1, fcols] += _colsum(acc[k])
            for k in range(FFN_K - 1):
                rot[k] = dgc_ext[ch, pl.ds(k + 1, t), :]
            for cb in range(FF_CHUNK // LANE):
                cols = slice(cb * LANE, (cb + 1) * LANE)
                fcols = slice(lo + cb * LANE, lo + (cb + 1) * LANE)
                gcols = slice(DFF + lo + cb * LANE, DFF + lo + (cb + 1) * LANE)

                def dg_rows(r0, acc, cols=cols, fcols=fcols, gcols=gcols, ch=ch):
                    rows = pl.ds(r0, rb)
                    dg = (f3_ref[2:3, fcols] * dgc_ext[ch, rows, cols] + f3_ref[1:2, fcols] * rot[0, rows, cols]
                          + f3_ref[0:1, fcols] * rot[1, rows, cols])
                    dhh_ref[rows, gcols] = dg.astype(MXU_DTYPE)
                    return acc

                _row_blocks(t, rb, dg_rows, ())
            dgc_ext[ch, t:t + HALO, :] = dgc_ext[ch, 0:HALO, :]

        wide_scr[...] = _dot(dhh_ref[...], wup_v[...])

        def out_rows(r0, acc):
            rows = pl.ds(r0, 8)
            du2 = wide_scr[rows, :]
            dx1_ref[rows, :] = dx1_ref[rows, :] + du2 * (1.0 + mod_ref[4:5, :])
            return acc[0] + du2 * x1_ref[rows, :], acc[1] + du2

        acc = _row_blocks(t, 8, out_rows, (zero_d, zero_d))
        dmod_ref[1:2, :] += _colsum(acc[0])
        dmod_ref[0:1, :] += _colsum(acc[1])

    rev = lambda w: pl.BlockSpec((None, t, w), lambda b, i: (b, nt - 1 - i, 0))
    halo = pl.BlockSpec((None, HALO, DFF), lambda b, i: (b, jnp.maximum((nt - 1 - i) * (t // HALO) - 1, 0), 1))
    return pl.pallas_call(
        body, name="ffn_bwd", grid=(bl, nt),
        out_shape=[jax.ShapeDtypeStruct((bl, seq, D), F32), jax.ShapeDtypeStruct((bl, seq, D), MXU_DTYPE),
                   jax.ShapeDtypeStruct((bl, seq, 2 * DFF), MXU_DTYPE), jax.ShapeDtypeStruct((8, D), F32),
                   jax.ShapeDtypeStruct((8, DFF), F32), jax.ShapeDtypeStruct((bl, 8, D), F32)],
        in_specs=[rev(D), rev(D), rev(D), rev(2 * DFF), halo, pl.BlockSpec((None, 8, D), lambda b, i: (b, 0, 0)),
                  _ANY, _ANY, _full(f3.shape), _full(ln2.shape)],
        out_specs=[rev(D), rev(D), rev(2 * DFF), _full((8, D)), _full((8, DFF)),
                   pl.BlockSpec((None, 8, D), lambda b, i: (b, 0, 0))],
        scratch_shapes=[pltpu.VMEM(wup_t.shape, MXU_DTYPE), pltpu.VMEM(wdown.shape, MXU_DTYPE),
                        pltpu.VMEM((t + HALO, FF_CHUNK), F32), pltpu.VMEM((n_chunk, t + HALO, FF_CHUNK), F32),
                        pltpu.VMEM((FFN_K - 1, t, FF_CHUNK), F32), pltpu.VMEM((t, FF_CHUNK), F32),
                        pltpu.VMEM((t, D), F32), pltpu.SemaphoreType.DMA((2,))],
        compiler_params=_params(2),
    )(x1, y2, tgt, hh, hh, mods, wup_t, wdown, f3, ln2)


def _mixer_bwd(x, dx1, proj, h, vc, mixed, mods, win_t, wout, wri, mavg2, va, vb, ln1, ffn_wgrads):
    bl, seq, _ = x.shape
    t = min(TOK_TILE, seq)
    nt = seq // t
    pw = 2 * LW + 2 * CW
    n_g = len(ffn_wgrads)
    rb = min(ROW_BLK, t)
    half_f32 = [pltpu.VMEM((t, LW), F32)] * 10

    def body(x_ref, dx1_ref, proj_ref, phalo_ref, h_ref, hhalo_ref, vc_ref, mixed_ref, mod_ref,
             win_hbm, wout_hbm, wri_ref, mavg_ref, va_ref, vb_ref, ln1_ref, *rest):
        wgrad_refs, rest = rest[:n_g], rest[n_g:]
        gx_ref, dproj_ref, dm_ref, ga_ref, gw31_ref, dwr_ref, dwi_ref, dln1_ref, dmod_ref = rest[:9]
        slot_refs, rest = rest[9:9 + n_g], rest[9 + n_g:]
        (win_v, wout_v, xa_ext, vg_scr, dvc_ext, dxc_ext, rot_b, car, sems,
         st_a, st_b, st_c, yn_scr, dyn_scr, hprev_scr, xc_scr, a_scr, mult_scr, g_scr,
         wide, gates, hilo_a, hilo_b, dgates, xcb_scr, rot_c, rot_d, hp_ext, e_send, e_recv, e_local) = rest
        b, i = pl.program_id(0), pl.program_id(1)
        tt = nt - 1 - i
        e_start, e_finish = _exchange_steps(wgrad_refs, slot_refs, e_send, e_recv, e_local)

        @pl.when((b == 0) & (i == 0))
        def _():
            e_start()
            cps = [pltpu.make_async_copy(win_hbm, win_v, sems.at[0]), pltpu.make_async_copy(wout_hbm, wout_v, sems.at[1])]
            for cp in cps:
                cp.start()
            for cp in cps:
                cp.wait()
            for ref in (ga_ref, gw31_ref, dwr_ref, dwi_ref, dln1_ref):
                ref[...] = jnp.zeros_like(ref)

        @pl.when(i == 0)
        def _():
            dmod_ref[...] = jnp.zeros_like(dmod_ref)
            dvc_ext[t:t + HALO31, :] = jnp.zeros((HALO31, CW), F32)
            dxc_ext[t:t + HALO, :] = jnp.zeros((HALO, LW), F32)
            car[...] = jnp.zeros_like(car)

        halo_keep = (tt > 0).astype(F32)
        zero_d = jnp.zeros((8, D), F32)
        zero_l = jnp.zeros((8, LANE), F32)
        blocks = [(r0, c0) for c0 in range(0, LW, LANE) for r0 in range(0, t, rb)]

        def add_row(ref, row, c0, acc):
            ref[row:row + 1, c0:c0 + LANE] += _colsum(acc)

        def ln_rows(r0, acc):
            rows = pl.ds(r0, 8)
            xv, mixed = x_ref[rows, :], mixed_ref[rows, :]
            gt = mod_ref[2:3, :]
            xh, rstd = _layer_norm_stats(ALPHA * xv + (1.0 + gt) * mixed)
            dx1 = dx1_ref[rows, :]
            dxh = dx1 * ln1_ref[0:1, :]
            dr1 = rstd * (dxh - _rowmean(dxh) - xh * _rowmean(dxh * xh))
            wide[rows, :] = (1.0 + gt) * dr1
            gx_ref[rows, :] = ALPHA * dr1
            return acc[0] + dx1 * xh, acc[1] + dx1, acc[2] + dr1 * mixed

        acc = _row_blocks(t, 8, ln_rows, (zero_d, zero_d, zero_d))
        dln1_ref[0:1, :] += _colsum(acc[0])
        dln1_ref[1:2, :] += _colsum(acc[1])
        dmod_ref[2:3, :] += _colsum(acc[2])
        _cast_rows(wide, dm_ref, t)
        wide[...] = _dot_nt(dm_ref[...], wout_v[...])

        mavg2 = mavg_ref[...]
        for r0, c0 in blocks:
            rows, cols = pl.ds(r0, rb), slice(c0, c0 + LANE)
            _store_hilo(hilo_a, rows, c0, vc_ref[rows, cols])
        st_a[...] = _dot(hilo_a[...], mavg2)
        for r0, c0 in blocks:
            rows, cols = pl.ds(r0, rb), slice(c0, c0 + LANE)
            dl = vc_ref[rows, cols] - st_a[rows, cols]
            yn_scr[rows, cols] = dl
            _store_hilo(hilo_a, rows, c0, dl * dl)
        st_b[...] = _dot(hilo_a[...], mavg2)
        for c0 in range(0, CW, LANE):
            cols, pc = slice(c0, c0 + LANE), 2 * LW + c0
            ng, nb = vb_ref[CONV_K + 1:CONV_K + 2, cols], vb_ref[CONV_K + 2:CONV_K + 3, cols]
            acc_g, acc_b = zero_l, zero_l
            for r0 in range(0, t, rb):
                rows = pl.ds(r0, rb)
                rs = lax.rsqrt(st_b[rows, cols] + LN_EPS)
                yn = yn_scr[rows, cols] * rs
                yl = yn * ng + nb
                s = _sigmoid(yl)
                dyl = wide[rows, LW + c0:LW + c0 + LANE] * (s * (1.0 + yl * (1.0 - s)))
                acc_g, acc_b = acc_g + _fold8(dyl * yn), acc_b + _fold8(dyl)
                dyn = dyl * ng
                st_b[rows, cols] = rs
                yn_scr[rows, cols] = yn
                dyn_scr[rows, cols] = dyn
                _store_hilo(hilo_a, rows, c0, dyn)
                _store_hilo(hilo_b, rows, c0, dyn * yn)
                vg_scr[rows, cols] = proj_ref[rows, pc:pc + LANE] * _sigmoid(proj_ref[rows, pc + CW:pc + CW + LANE])
            add_row(ga_ref, 8, c0, acc_g)
            add_row(ga_ref, 9, c0, acc_b)
        st_a[...] = _dot(hilo_a[...], mavg2)
        st_c[...] = _dot(hilo_b[...], mavg2)
        for c0 in range(0, CW, LANE):
            cols = slice(c0, c0 + LANE)
            acc_b = zero_l
            for r0 in range(0, t, rb):
                rows = pl.ds(r0, rb)
                dvc = st_b[rows, cols] * (dyn_scr[rows, cols] - st_a[rows, cols] - yn_scr[rows, cols] * st_c[rows, cols])
                dvc_ext[rows, cols] = dvc
                acc_b = acc_b + _fold8(dvc)
            add_row(ga_ref, 10, c0, acc_b)
        dvc_win = _shifted_rows(dvc_ext, rot_b, rb)
        for c0 in range(0, CW, LANE):
            cols, pc = slice(c0, c0 + LANE), 2 * LW + c0
            for k in range(CONV_K):
                acc_w = zero_l
                for r0 in range(0, t, rb):
                    acc_w = acc_w + _fold8(vg_scr[pl.ds(r0, rb), cols] * dvc_win(CONV_K - 1 - k, r0, cols))
                add_row(gw31_ref, k, c0, acc_w)
            for r0 in range(0, t, rb):
                rows = pl.ds(r0, rb)
                dvg = jnp.zeros((rb, LANE), F32)
                for k in range(CONV_K):
                    dvg = dvg + vb_ref[k:k + 1, cols] * dvc_win(CONV_K - 1 - k, r0, cols)
                vbr = proj_ref[rows, pc:pc + LANE]
                sgb = _sigmoid(proj_ref[rows, pc + CW:pc + CW + LANE])
                dproj_ref[rows, pc:pc + LANE] = (dvg * sgb).astype(MXU_DTYPE)
                dproj_ref[rows, pc + CW:pc + CW + LANE] = (dvg * vbr * sgb * (1.0 - sgb)).astype(MXU_DTYPE)
        dvc_ext[t:t + HALO31, :] = dvc_ext[0:HALO31, :]

        xa_ext[0:HALO, :] = phalo_ref[HALO31 - HALO:HALO31, 0:LW] * halo_keep
        xa_ext[HALO:HALO + t, :] = proj_ref[:, 0:LW]
        for k in range(LRU_K - 1):
            rot_c[k] = xa_ext[pl.ds(HALO - (LRU_K - 1) + k, t), :]
        hp_ext[0:HALO, :] = hhalo_ref[...] * halo_keep
        hp_ext[HALO:HALO + t, :] = h_ref[...]
        hprev_scr[...] = hp_ext[pl.ds(HALO - 1, t), :]

        def xa_tap(k, rows, r0, cols):
            return xa_ext[pl.ds(HALO + r0, rb), cols] if k == LRU_K - 1 else rot_c[k, rows, cols]

        for r0, c0 in blocks:
            rows, cols = pl.ds(r0, rb), slice(c0, c0 + LANE)
            xc = va_ref[4:5, cols]
            for k in range(LRU_K):
                xc = xc + va_ref[k:k + 1, cols] * xa_tap(k, rows, r0, cols)
            xc_scr[rows, cols] = xc
            xcb_scr[rows, cols] = xc.astype(MXU_DTYPE)
        gates[...] = _dot(xcb_scr[...], wri_ref[...])
        for r0, c0 in blocks:
            rows, cols = pl.ds(r0, rb), slice(c0, c0 + LANE)
            r = _sigmoid(gates[rows, cols] + va_ref[5:6, cols])
            ig = _sigmoid(gates[rows, LW + c0:LW + c0 + LANE] + va_ref[6:7, cols])
            la = (-LRU_C) * r * _softplus(-va_ref[7:8, cols])
            a = jnp.exp(la)
            gates[rows, cols] = r
            gates[rows, LW + c0:LW + c0 + LANE] = ig
            a_scr[rows, cols] = a
            mult_scr[rows, cols] = jnp.sqrt(-jnp.tanh(la) * (a * a + 1.0))
            ge, dge = _gelu_and_grad(proj_ref[rows, LW + c0:LW + c0 + LANE])
            dya = wide[rows, cols]
            dproj_ref[rows, LW + c0:LW + c0 + LANE] = (dya * h_ref[rows, cols] * dge).astype(MXU_DTYPE)
            g_scr[rows, cols] = dya * ge
        _scan_bwd_blocks(a_scr, g_scr, car, t)
        for c0 in range(0, LW, LANE):
            cols = slice(c0, c0 + LANE)
            sp = _softplus(-va_ref[7:8, cols])
            acc_l, acc_r, acc_i = zero_l, zero_l, zero_l
            for r0 in range(0, t, rb):
                rows = pl.ds(r0, rb)
                gv, xc, a, mult = g_scr[rows, cols], xc_scr[rows, cols], a_scr[rows, cols], mult_scr[rows, cols]
                r, ig = gates[rows, cols], gates[rows, LW + c0:LW + c0 + LANE]
                dla = gv * hprev_scr[rows, cols] * a - gv * (ig * xc) * (a * a / mult)
                acc_l = acc_l + _fold8(dla * ((-LRU_C) * r))
                dgr = dla * ((-LRU_C) * sp) * r * (1.0 - r)
                dgi = gv * (mult * xc) * ig * (1.0 - ig)
                acc_r, acc_i = acc_r + _fold8(dgr), acc_i + _fold8(dgi)
                dgates[rows, cols] = dgr.astype(MXU_DTYPE)
                dgates[rows, LW + c0:LW + c0 + LANE] = dgi.astype(MXU_DTYPE)
                dxc_ext[rows, cols] = gv * (mult * ig)
            add_row(ga_ref, 5, c0, acc_r)
            add_row(ga_ref, 6, c0, acc_i)
            ga_ref[7:8, cols] += _colsum(acc_l) * (-_sigmoid(-va_ref[7:8, cols]))
        dwr_ref[...] += _dot_tn(xcb_scr[...], dgates[:, 0:LW])
        dwi_ref[...] += _dot_tn(xcb_scr[...], dgates[:, LW:])
        st_a[...] = _dot_nt(dgates[...], wri_ref[...])
        for c0 in range(0, LW, LANE):
            cols = slice(c0, c0 + LANE)
            acc_b = zero_l
            for r0 in range(0, t, rb):
                rows = pl.ds(r0, rb)
                dxc = dxc_ext[rows, cols] + st_a[rows, cols]
                dxc_ext[rows, cols] = dxc
                acc_b = acc_b + _fold8(dxc)
            add_row(ga_ref, 4, c0, acc_b)
        for k in range(LRU_K - 1):
            rot_d[k] = dxc_ext[pl.ds(k + 1, t), :]
        for c0 in range(0, LW, LANE):
            cols = slice(c0, c0 + LANE)
            acc_w = [zero_l] * LRU_K
            for r0 in range(0, t, rb):
                rows = pl.ds(r0, rb)
                dxc = dxc_ext[rows, cols]
                dxa = va_ref[LRU_K - 1:LRU_K, cols] * dxc
                for k in range(LRU_K):
                    acc_w[k] = acc_w[k] + _fold8(dxc * xa_tap(k, rows, r0, cols))
                    if k < LRU_K - 1:
                        dxa = dxa + va_ref[k:k + 1, cols] * rot_d[LRU_K - 2 - k, rows, cols]
                dproj_ref[rows, cols] = dxa.astype(MXU_DTYPE)
            for k in range(LRU_K):
                add_row(ga_ref, k, c0, acc_w[k])
        dxc_ext[t:t + HALO, :] = dxc_ext[0:HALO, :]

        wide[...] = _dot(dproj_ref[...], win_v[...])

        def out_rows(r0, acc):
            rows = pl.ds(r0, 8)
            du1 = wide[rows, :]
            gx_ref[rows, :] = gx_ref[rows, :] + du1 * (1.0 + mod_ref[1:2, :])
            return acc[0] + du1 * x_ref[rows, :], acc[1] + du1

        acc = _row_blocks(t, 8, out_rows, (zero_d, zero_d))
        dmod_ref[1:2, :] += _colsum(acc[0])
        dmod_ref[0:1, :] += _colsum(acc[1])

        pl.when((b == bl - 1) & (i == nt - 1))(e_finish)

    rev = lambda w: pl.BlockSpec((None, t, w), lambda b, i: (b, nt - 1 - i, 0))

    def halo(rows_, w):
        return pl.BlockSpec((None, rows_, w), lambda b, i: (b, jnp.maximum((nt - 1 - i) * (t // rows_) - 1, 0), 0))

    return pl.pallas_call(
        body, name="mixer_bwd", grid=(bl, nt),
        out_shape=[jax.ShapeDtypeStruct((bl, seq, D), F32), jax.ShapeDtypeStruct((bl, seq, pw), MXU_DTYPE),
                   jax.ShapeDtypeStruct((bl, seq, D), MXU_DTYPE), jax.ShapeDtypeStruct((16, LW), F32),
                   jax.ShapeDtypeStruct((32, CW), F32), jax.ShapeDtypeStruct((LW, LW), F32),
                   jax.ShapeDtypeStruct((LW, LW), F32), jax.ShapeDtypeStruct((8, D), F32),
                   jax.ShapeDtypeStruct((bl, 8, D), F32)] + _exchange_out_shapes(ffn_wgrads),
        in_specs=[rev(D), rev(D), rev(pw), halo(HALO31, pw), rev(LW), halo(HALO, LW), rev(CW), rev(D),
                  pl.BlockSpec((None, 8, D), lambda b, i: (b, 0, 0)), _ANY, _ANY,
                  _full(wri.shape), _full(mavg2.shape), _full(va.shape), _full(vb.shape), _full(ln1.shape)]
        + [_ANY] * n_g,
        out_specs=[rev(D), rev(pw), rev(D), _full((16, LW)), _full((32, CW)), _full((LW, LW)), _full((LW, LW)),
                   _full((8, D)), pl.BlockSpec((None, 8, D), lambda b, i: (b, 0, 0))] + [_ANY] * n_g,
        scratch_shapes=[pltpu.VMEM(win_t.shape, MXU_DTYPE), pltpu.VMEM(wout.shape, MXU_DTYPE),
                        pltpu.VMEM((t + HALO, LW), F32), pltpu.VMEM((t, CW), F32),
                        pltpu.VMEM((t + HALO31, CW), F32), pltpu.VMEM((t + HALO, LW), F32),
                        pltpu.VMEM((7, t + HALO31 - 8, CW), F32),
                        pltpu.VMEM((8, LW), F32), pltpu.SemaphoreType.DMA((2,))] + half_f32
        + [pltpu.VMEM((t, D), F32), pltpu.VMEM((t, 2 * LW), F32), pltpu.VMEM((t, 2 * CW), MXU_DTYPE),
           pltpu.VMEM((t, 2 * CW), MXU_DTYPE), pltpu.VMEM((t, 2 * LW), MXU_DTYPE), pltpu.VMEM((t, LW), MXU_DTYPE),
           pltpu.VMEM((LRU_K - 1, t, LW), F32), pltpu.VMEM((LRU_K - 1, t, LW), F32), pltpu.VMEM((t + HALO, LW), F32)]
        + _comm_sems(n_g),
        compiler_params=_params(2),
    )(x, dx1, proj, proj, h, h, vc, mixed, mods, win_t, wout, wri, mavg2, va, vb, ln1, *ffn_wgrads)


def _weight_grad(a, bmat, tm, name, out_dtype=F32, exchange=(), exchange_half=None, col_phase=None, tok_tile=None):
    ntok, m = a.shape
    n = bmat.shape[1]
    tk = min(tok_tile or WG_TOK_TILE, ntok)
    nk = ntok // tk
    stride, col0 = (1, 0) if col_phase is None else (2, col_phase)
    m = m // stride
    ni = m // tm
    n_e = len(exchange)

    def body(a_ref, b_ref, *rest):
        e_in, o_ref, e_out = rest[:n_e], rest[n_e], rest[n_e + 1:2 * n_e + 1]
        acc_ref = rest[2 * n_e + 1]
        i, k = pl.program_id(0), pl.program_id(1)
        if n_e:
            e_start, e_finish = _exchange_steps(e_in, e_out, *rest[2 * n_e + 2:], half=exchange_half)
            pl.when((i == 0) & (k == 0))(e_start)

        @pl.when(k == 0)
        def _():
            acc_ref[...] = jnp.zeros_like(acc_ref)

        acc_ref[...] += _dot_tn(a_ref[...], b_ref[...])

        @pl.when(k == nk - 1)
        def _():
            o_ref[...] = acc_ref[...].astype(out_dtype)

        if n_e:
            pl.when((i == ni - 1) & (k == nk - 1))(e_finish)

    res = pl.pallas_call(
        body, name=name, grid=(ni, nk),
        out_shape=[jax.ShapeDtypeStruct((m, n), out_dtype)] + _exchange_out_shapes(exchange, exchange_half),
        in_specs=[pl.BlockSpec((tk, tm), lambda i, k: (k, i * stride + col0)), pl.BlockSpec((tk, n), lambda i, k: (k, 0))]
        + [_ANY] * n_e,
        out_specs=[pl.BlockSpec((tm, n), lambda i, k: (i, 0))] + [_ANY] * n_e,
        scratch_shapes=[pltpu.VMEM((tm, n), F32)] + (_comm_sems(n_e) if n_e else []),
        compiler_params=_params(2),
    )(a, bmat, *exchange)
    return res if n_e else res[0]


def _sum_slots(slots, name, slots_c1=None):
    _, r, cdim = slots.shape
    tr = next((cand for cand in (64, 32) if r % cand == 0), r)
    ins = [slots] if slots_c1 is None else [slots, slots_c1]

    def body(*refs):
        o_ref = refs[-1]

        def total(s_ref):
            acc = s_ref[0].astype(F32)
            for j in range(1, NDEV):
                acc = acc + s_ref[j].astype(F32)
            o_ref[...] = acc

        if slots_c1 is None:
            total(refs[0])
        else:
            c = lax.axis_index("c")
            pl.when(c == 0)(functools.partial(total, refs[0]))
            pl.when(c == 1)(functools.partial(total, refs[1]))

    return pl.pallas_call(
        body, name=name, grid=(r // tr,), out_shape=jax.ShapeDtypeStruct((r, cdim), F32),
        in_specs=[pl.BlockSpec((NDEV, tr, cdim), lambda i: (0, i, 0))] * len(ins),
        out_specs=pl.BlockSpec((tr, cdim), lambda i: (i, 0)),
        compiler_params=_params(1),
    )(*ins)


def _small_grad_layout(bl):
    r1 = 6 * bl
    return {"dmod": 0, "wide": r1, "ln": r1 + 32, "df3": r1 + 36, "g_wr": r1 + 48, "g_wi": r1 + 80, "rows": r1 + 112}


def _pack_small_grads(dmod_a, dmod_b, g_a, g_w31, dln1, dln2, df3, g_wr, g_wi, loss_part):
    bl = dmod_a.shape[0] // 8
    lay = _small_grad_layout(bl)
    ins = [dmod_a, dmod_b, g_a, g_w31, dln1, dln2, df3, g_wr, g_wi, loss_part]

    def body(ma_ref, mb_ref, ga_ref, gw_ref, l1_ref, l2_ref, f3_ref, wr_ref, wi_ref, ls_ref, o_ref):
        o_ref[...] = jnp.zeros_like(o_ref)
        for b in range(bl):
            o_ref[6 * b:6 * b + 3, :] = ma_ref[8 * b:8 * b + 3, :]
            o_ref[6 * b + 3:6 * b + 6, :] = mb_ref[8 * b:8 * b + 3, :]
        wide = lay["wide"]
        o_ref[wide:wide + 32, 0:CW] = gw_ref[...]
        o_ref[wide:wide + 16, CW:CW + LW] = ga_ref[...]
        o_ref[wide + 16:wide + 24, CW:CW + LANE] = ls_ref[...]
        o_ref[lay["ln"]:lay["ln"] + 2, :] = l1_ref[0:2, :]
        o_ref[lay["ln"] + 2:lay["ln"] + 4, :] = l2_ref[0:2, :]
        for j, c0 in enumerate(range(0, DFF, D)):
            w = min(D, DFF - c0)
            o_ref[lay["df3"] + 4 * j:lay["df3"] + 4 * j + 4, 0:w] = f3_ref[0:4, c0:c0 + w]
        o_ref[lay["g_wr"]:lay["g_wr"] + 32, :] = wr_ref[...]
        o_ref[lay["g_wi"]:lay["g_wi"] + 32, :] = wi_ref[...]

    return pl.pallas_call(
        body, name="pack_small_grads", out_shape=jax.ShapeDtypeStruct((lay["rows"], D), F32),
        in_specs=[_full(a.shape) for a in ins], out_specs=_full((lay["rows"], D)),
        compiler_params=pltpu.CompilerParams(vmem_limit_bytes=VMEM_LIMIT),
    )(*ins)


def _sum_small_grads(packed_all, bl):
    lay = _small_grad_layout(bl)
    shapes = {"lru_conv_w": (LRU_K, LW), "lru_conv_b": (1, LW), "lru_b_r": (1, LW), "lru_b_i": (1, LW),
              "lru_lambda": (1, LW), "conv_norm_g": (1, CW), "conv_norm_b": (1, CW), "conv_b": (1, CW),
              "conv_w": (CONV_K, CW), "ln1_g": (1, D), "ln1_b": (1, D), "ln2_g": (1, D), "ln2_b": (1, D),
              "ffn_conv_w": (FFN_K, DFF), "ffn_conv_b": (1, DFF), "lru_w_r": (32, D), "lru_w_i": (32, D),
              "loss": (8, LANE)}
    names = list(shapes)

    def body(p_ref, *outs):
        o = dict(zip(names, outs))

        def part(r0, nr, c0=0, nc=D):
            acc = p_ref[0, r0:r0 + nr, c0:c0 + nc]
            for j in range(1, NDEV):
                acc = acc + p_ref[j, r0:r0 + nr, c0:c0 + nc]
            return acc

        wide = lay["wide"]
        ga = part(wide, 16, CW, LW)
        o["lru_conv_w"][...] = ga[0:LRU_K]
        for row, k in ((4, "lru_conv_b"), (5, "lru_b_r"), (6, "lru_b_i"), (7, "lru_lambda"), (8, "conv_norm_g"),
                       (9, "conv_norm_b"), (10, "conv_b")):
            o[k][...] = ga[row:row + 1]
        o["conv_w"][...] = part(wide, 32, 0, CW)[0:CONV_K]
        o["loss"][...] = part(wide + 16, 8, CW, LANE)
        ln = part(lay["ln"], 4)
        o["ln1_g"][...], o["ln1_b"][...], o["ln2_g"][...], o["ln2_b"][...] = ln[0:1], ln[1:2], ln[2:3], ln[3:4]
        for j, c0 in enumerate(range(0, DFF, D)):
            w = min(D, DFF - c0)
            f3 = part(lay["df3"] + 4 * j, 4, 0, w)
            o["ffn_conv_w"][:, c0:c0 + w] = f3[0:FFN_K]
            o["ffn_conv_b"][:, c0:c0 + w] = f3[FFN_K:FFN_K + 1]
        o["lru_w_r"][...] = part(lay["g_wr"], 32)
        o["lru_w_i"][...] = part(lay["g_wi"], 32)

    res = pl.pallas_call(
        body, name="sum_small_grads", out_shape=[jax.ShapeDtypeStruct(shapes[k], F32) for k in names],
        in_specs=[_full(packed_all.shape)], out_specs=[_full(shapes[k]) for k in names],
        compiler_params=pltpu.CompilerParams(vmem_limit_bytes=VMEM_LIMIT),
    )(packed_all)
    return dict(zip(names, res))


def _adamw_update(w_ref, g_ref, m_ref, v_ref, d_ref, nm_ref, nv_ref):
    gv = g_ref[...]
    nm = ADAM_B1 * m_ref[...] + (1.0 - ADAM_B1) * gv
    nv = ADAM_B2 * v_ref[...] + (1.0 - ADAM_B2) * (gv * gv)
    m_hat = nm / (1.0 - ADAM_B1 ** ADAM_STEP)
    v_hat = nv / (1.0 - ADAM_B2 ** ADAM_STEP)
    d_ref[...] = -ADAM_LR * (m_hat / (jnp.sqrt(v_hat) + ADAM_EPS) + ADAM_WD * w_ref[...])
    nm_ref[...] = nm
    nv_ref[...] = nv


def _sum_adamw(slots, w, m, v, name):
    _, r, cdim = slots.shape
    tr = next((cand for cand in (64, 32) if r % cand == 0), r)

    def body(s_ref, w_ref, m_ref, v_ref, g_ref, d_ref, nm_ref, nv_ref):
        acc = s_ref[0].astype(F32)
        for j in range(1, NDEV):
            acc = acc + s_ref[j].astype(F32)
        g_ref[...] = acc
        _adamw_update(w_ref, g_ref, m_ref, v_ref, d_ref, nm_ref, nv_ref)

    spec = pl.BlockSpec((tr, cdim), lambda i: (i, 0))
    return pl.pallas_call(
        body, name=name, grid=(r // tr,), out_shape=[jax.ShapeDtypeStruct((r, cdim), F32)] * 4,
        in_specs=[pl.BlockSpec((NDEV, tr, cdim), lambda i: (0, i, 0))] + [spec] * 3, out_specs=[spec] * 4,
        compiler_params=_params(1),
    )(slots, w, m, v)


def _adamw_many(ws, gs, ms, vs, name):
    n = len(ws)

    def body(*refs):
        ins, outs = refs[:4 * n], refs[4 * n:]
        for k in range(n):
            _adamw_update(ins[k], ins[n + k], ins[2 * n + k], ins[3 * n + k], outs[k], outs[n + k], outs[2 * n + k])

    specs = [_full(w.shape) for w in ws]
    res = pl.pallas_call(
        body, name=name, out_shape=[jax.ShapeDtypeStruct(w.shape, F32) for w in ws] * 3,
        in_specs=specs * 4, out_specs=specs * 3,
        compiler_params=pltpu.CompilerParams(vmem_limit_bytes=VMEM_LIMIT),
    )(*ws, *gs, *ms, *vs)
    return res[:n], res[n:2 * n], res[2 * n:]


def _adamw(w, g, m, v, name):
    r, cdim = w.shape
    tr = next((cand for cand in (256, 128) if r % cand == 0), r)
    body = functools.partial(_adamw_update)
    spec = pl.BlockSpec((tr, cdim), lambda i: (i, 0))
    return pl.pallas_call(
        body, name=name, grid=(r // tr,), out_shape=[jax.ShapeDtypeStruct((r, cdim), F32)] * 3,
        in_specs=[spec] * 4, out_specs=[spec] * 3, compiler_params=_params(1),
    )(w, g, m, v)


def _pack(arrs, width=D, row_mult=8):
    parts = []
    for a in arrs:
        flat = a.reshape(-1)
        pad = (-flat.shape[0]) % width
        parts.append(jnp.pad(flat, (0, pad)))
    flat = jnp.concatenate(parts)
    pad = (-flat.shape[0]) % (width * row_mult)
    return jnp.pad(flat, (0, pad)).reshape(-1, width)


def _unpack_gathered(buf, shapes, width=D):
    out, row = [], 0
    for shp in shapes:
        size = math.prod(shp)
        nrow = -(-size // width)
        out.append(buf[:, row:row + nrow].reshape(NDEV, -1)[:, :size].reshape((NDEV,) + tuple(shp)))
        row += nrow
    return out


def _block_diag(w):
    hn, dh, _ = w.shape
    eye = jnp.eye(hn, dtype=w.dtype)
    return (w[:, :, None, :] * eye[:, None, :, None]).reshape(hn * dh, hn * dh)


def _diag_blocks(wfull, hn):
    dh = wfull.shape[0] // hn
    keep = np.eye(hn, dtype=bool)[:, None, :, None]
    return jnp.where(keep, wfull.reshape(hn, dh, hn, dh), 0.0).sum(axis=2)


def _pad_rows(a, rows):
    return jnp.pad(a, ((0, rows - a.shape[0]), (0, 0)))


def kernel(x, c, w_ada, b_ada, w_in, lru_conv_w, lru_conv_b, lru_w_r, lru_b_r, lru_w_i, lru_b_i, lru_lambda, conv_w, conv_b, conv_norm_g, conv_norm_b, w_out, ln1_g, ln1_b, ffn_w_up, ffn_conv_w, ffn_conv_b, ffn_w_down, ln2_g, ln2_b, loss_target, m_w_ada, m_b_ada, m_w_in, m_lru_conv_w, m_lru_conv_b, m_lru_w_r, m_lru_b_r, m_lru_w_i, m_lru_b_i, m_lru_lambda, m_conv_w, m_conv_b, m_conv_norm_g, m_conv_norm_b, m_w_out, m_ln1_g, m_ln1_b, m_ffn_w_up, m_ffn_conv_w, m_ffn_conv_b, m_ffn_w_down, m_ln2_g, m_ln2_b, v_w_ada, v_b_ada, v_w_in, v_lru_conv_w, v_lru_conv_b, v_lru_w_r, v_lru_b_r, v_lru_w_i, v_lru_b_i, v_lru_lambda, v_conv_w, v_conv_b, v_conv_norm_g, v_conv_norm_b, v_w_out, v_ln1_g, v_ln1_b, v_ffn_w_up, v_ffn_conv_w, v_ffn_conv_b, v_ffn_w_down, v_ln2_g, v_ln2_b):
    weights = dict(w_ada=w_ada, b_ada=b_ada, w_in=w_in, lru_conv_w=lru_conv_w, lru_conv_b=lru_conv_b, lru_w_r=lru_w_r,
                   lru_b_r=lru_b_r, lru_w_i=lru_w_i, lru_b_i=lru_b_i, lru_lambda=lru_lambda, conv_w=conv_w, conv_b=conv_b,
                   conv_norm_g=conv_norm_g, conv_norm_b=conv_norm_b, w_out=w_out, ln1_g=ln1_g, ln1_b=ln1_b,
                   ffn_w_up=ffn_w_up, ffn_conv_w=ffn_conv_w, ffn_conv_b=ffn_conv_b, ffn_w_down=ffn_w_down, ln2_g=ln2_g,
                   ln2_b=ln2_b)
    mom_m = dict(w_ada=m_w_ada, b_ada=m_b_ada, w_in=m_w_in, lru_conv_w=m_lru_conv_w, lru_conv_b=m_lru_conv_b,
                 lru_w_r=m_lru_w_r, lru_b_r=m_lru_b_r, lru_w_i=m_lru_w_i, lru_b_i=m_lru_b_i, lru_lambda=m_lru_lambda,
                 conv_w=m_conv_w, conv_b=m_conv_b, conv_norm_g=m_conv_norm_g, conv_norm_b=m_conv_norm_b, w_out=m_w_out,
                 ln1_g=m_ln1_g, ln1_b=m_ln1_b, ffn_w_up=m_ffn_w_up, ffn_conv_w=m_ffn_conv_w, ffn_conv_b=m_ffn_conv_b,
                 ffn_w_down=m_ffn_w_down, ln2_g=m_ln2_g, ln2_b=m_ln2_b)
    mom_v = dict(w_ada=v_w_ada, b_ada=v_b_ada, w_in=v_w_in, lru_conv_w=v_lru_conv_w, lru_conv_b=v_lru_conv_b,
                 lru_w_r=v_lru_w_r, lru_b_r=v_lru_b_r, lru_w_i=v_lru_w_i, lru_b_i=v_lru_b_i, lru_lambda=v_lru_lambda,
                 conv_w=v_conv_w, conv_b=v_conv_b, conv_norm_g=v_conv_norm_g, conv_norm_b=v_conv_norm_b, w_out=v_w_out,
                 ln1_g=v_ln1_g, ln1_b=v_ln1_b, ffn_w_up=v_ffn_w_up, ffn_conv_w=v_ffn_conv_w, ffn_conv_b=v_ffn_conv_b,
                 ffn_w_down=v_ffn_w_down, ln2_g=v_ln2_g, ln2_b=v_ln2_b)
    names = list(weights)
    bl, seq, _ = x.shape
    ntok = bl * seq
    me = 4 * lax.axis_index("x") + 2 * lax.axis_index("y") + lax.axis_index("c")

    small_shapes = [(bl, D), (LRU_K, LW // NDEV), (CONV_K, CW // NDEV), (FFN_K, DFF // NDEV)]
    small = _pack([c, lru_conv_w[0], conv_w[0], ffn_conv_w[0]], width=128)
    n_small = small.shape[0]
    small_all, win_t, wout_b = _all_gather(
        [small, w_in[0].T.astype(MXU_DTYPE), w_out[0].astype(MXU_DTYPE)], "gather_small_and_mixer_weights")
    small_all = small_all.reshape(NDEV, n_small, 128)
    c_parts, k4_parts, k31_parts, k3_parts = _unpack_gathered(small_all, small_shapes, width=128)
    c_all = c_parts.reshape(NDEV * bl, D)
    lru_conv_w_f = k4_parts.transpose(1, 0, 2).reshape(LRU_K, LW)
    conv_w_f = k31_parts.transpose(1, 0, 2).reshape(CONV_K, CW)
    ffn_conv_w_f = k3_parts.transpose(1, 0, 2).reshape(FFN_K, DFF)

    ncol = w_ada.shape[2]
    b_ada_loc = lax.dynamic_slice(b_ada, (0, me * ncol), (1, ncol))
    mod_cols = _ada_fwd(c_all, w_ada[0], b_ada_loc)
    (mod_all,) = _all_gather([mod_cols], "gather_mod")
    mod_all = mod_all.reshape(NDEV, NDEV * bl, ncol)
    mod_mine = lax.dynamic_slice(mod_all, (0, me * bl, 0), (NDEV, bl, ncol))
    mods = mod_mine.transpose(1, 0, 2).reshape(bl, 6, D)
    mods = jnp.pad(mods, ((0, 0), (0, 2), (0, 0)))

    ffn_shards = [ffn_w_up[0].T.astype(MXU_DTYPE), ffn_w_down[0].astype(MXU_DTYPE)]

    wr_bd = _block_diag(lru_w_r[0]).astype(MXU_DTYPE)
    wi_bd = _block_diag(lru_w_i[0]).astype(MXU_DTYPE)
    mavg_np = np.kron(np.eye(CW // HEAD, dtype=np.float32), np.full((HEAD, HEAD), 1.0 / HEAD, np.float32))
    mavg = jnp.asarray(mavg_np, MXU_DTYPE)
    mavg2 = jnp.asarray(np.concatenate([mavg_np, mavg_np], axis=0), MXU_DTYPE)
    wri = jnp.concatenate([wr_bd, wi_bd], axis=1)
    va = jnp.concatenate([lru_conv_w_f, lru_conv_b, lru_b_r, lru_b_i, lru_lambda], axis=0)
    vb = _pad_rows(jnp.concatenate([conv_w_f, conv_b, conv_norm_g, conv_norm_b], axis=0), 40)
    ln1 = _pad_rows(jnp.concatenate([ln1_g, ln1_b], axis=0), 8)
    ln2 = _pad_rows(jnp.concatenate([ln2_g, ln2_b], axis=0), 8)
    f3 = _pad_rows(jnp.concatenate([ffn_conv_w_f, ffn_conv_b], axis=0), 8)

    proj, h, vc, y_b, mixed, x1, u1_b, wup_t, wdown_b = _mixer_fwd(
        x, mods, win_t, wout_b, wri, mavg2, va, vb, ln1, ffn_shards)
    hh, z_b, u2_b, y2, loss_part = _ffn_fwd(x1, mods, loss_target, wup_t, wdown_b, f3, ln2)

    dx1, dy2_b, dhh_b, dln2, df3, dmod_b = _ffn_bwd(x1, y2, loss_target, hh, mods, wup_t, wdown_b, f3, ln2)
    g_down = _weight_grad(z_b.reshape(ntok, DFF), dy2_b.reshape(ntok, D), FF_CHUNK, "wgrad_down", out_dtype=WIRE_DTYPE)
    g_up_t = _weight_grad(dhh_b.reshape(ntok, 2 * DFF), u2_b.reshape(ntok, D), FF_CHUNK, "wgrad_up",
                          out_dtype=WIRE_DTYPE)
    grad_x, dproj_b, dm_b, g_a, g_w31, g_wr, g_wi, dln1, dmod_a, s_up, s_down = _mixer_bwd(
        x, dx1, proj, h, vc, mixed, mods, win_t, wout_b, wri, mavg2, va, vb, ln1, [g_up_t, g_down])
    dproj_2d, u1_2d = dproj_b.reshape(ntok, 2 * LW + 2 * CW), u1_b.reshape(ntok, D)
    blk_in = (2 * LW + 2 * CW) // NDEV
    g_out_b = _weight_grad(y_b.reshape(ntok, D), dm_b.reshape(ntok, D), D, "wgrad_out", out_dtype=WIRE_DTYPE)
    g_in_c0, s_out = _weight_grad(dproj_2d, u1_2d, blk_in, "wgrad_in_c0", out_dtype=WIRE_DTYPE, exchange=[g_out_b],
                                  col_phase=0, tok_tile=2 * WG_TOK_TILE)
    g_in_c1, s_in_c0 = _weight_grad(dproj_2d, u1_2d, blk_in, "wgrad_in_c1", out_dtype=WIRE_DTYPE, exchange=[g_in_c0],
                                    exchange_half=0, col_phase=1, tok_tile=2 * WG_TOK_TILE)

    heads = LW // HEAD
    packed = _pack_small_grads(
        dmod_a.reshape(bl * 8, D), dmod_b.reshape(bl * 8, D), g_a, g_w31, dln1, dln2, df3,
        _diag_blocks(g_wr, heads).reshape(32, D), _diag_blocks(g_wi, heads).reshape(32, D), loss_part)
    n_rows = packed.shape[0]
    (s_in_c1,), (packed_all,) = _exchange_and_gather(
        [g_in_c1], [packed], "exchange_w_in_grads_gather_small_grads", half=1)
    packed_all = packed_all.reshape(NDEV, n_rows, D)
    full = _sum_small_grads(packed_all, bl)
    loss = jnp.sum(full.pop("loss")) * (0.5 / D)
    dmod_all = packed_all[:, 0:6 * bl].reshape(NDEV * bl, 6 * D)

    grads, delta, new_m, new_v = {}, {}, {}, {}
    for k, slots in (("w_out", s_out), ("ffn_w_down", s_down)):
        g_, d_, m_, v_ = _sum_adamw(slots, weights[k][0], mom_m[k][0], mom_v[k][0], "sum_adamw_" + k)
        grads[k], delta[k], new_m[k], new_v[k] = g_[None], d_[None], m_[None], v_[None]
    g_w_ada, g_b_ada = _ada_bwd(c_all, dmod_all, lax.dynamic_slice(dmod_all, (0, me * ncol), (NDEV * bl, ncol)))
    grads["w_ada"], grads["b_ada"] = g_w_ada[None], g_b_ada
    grads["w_in"] = _sum_slots(s_in_c0, "sum_w_in", slots_c1=s_in_c1).T[None]
    k = "ffn_w_up"
    g_, d_, m_, v_ = _sum_adamw(s_up, weights[k][0].T, mom_m[k][0].T, mom_v[k][0].T, "sum_adamw_" + k)
    grads[k], delta[k], new_m[k], new_v[k] = g_.T[None], d_.T[None], m_.T[None], v_.T[None]
    for k, gk in full.items():
        if k in ("lru_conv_w", "conv_w", "ffn_conv_w"):
            nloc = gk.shape[1] // NDEV
            gk = lax.dynamic_slice(gk, (0, me * nloc), (gk.shape[0], nloc))
        grads[k] = gk.reshape(weights[k].shape)

    big = ("w_ada", "w_in", "w_out", "ffn_w_up", "ffn_w_down")
    for k in ("w_ada", "w_in"):
        d_, m_, v_ = _adamw(weights[k][0], grads[k][0], mom_m[k][0], mom_v[k][0], "adamw_" + k)
        delta[k], new_m[k], new_v[k] = d_[None], m_[None], v_[None]
    small_names = [k for k in names if k not in big]
    d_, m_, v_ = _adamw_many([weights[k] for k in small_names], [grads[k] for k in small_names],
                             [mom_m[k] for k in small_names], [mom_v[k] for k in small_names], "adamw_small")
    for k, dk, mk, vk in zip(small_names, d_, m_, v_):
        delta[k], new_m[k], new_v[k] = dk, mk, vk

    return (loss, grad_x, *[grads[k] for k in names], *[delta[k] for k in names], *[new_m[k] for k in names],
            *[new_v[k] for k in names])
```

```python
import functools
import math

import jax
import jax.numpy as jnp
import numpy as np
from jax import lax
from jax.experimental import pallas as pl
from jax.experimental.pallas import tpu as pltpu

NDEV = 8
D = 1024
LW = 512
CW = 512
HEAD = 64
DFF = 2816
FF_CHUNK = 1408
LRU_K = 4
CONV_K = 31
FFN_K = 3
LRU_C = 8.0
ALPHA = (2 * 1) ** 0.25
LN_EPS = 1e-5
ADAM_LR = 0.001
ADAM_B1 = 0.9
ADAM_B2 = 0.999
ADAM_EPS = 1e-08
ADAM_WD = 0.01
ADAM_STEP = 10

MXU_DTYPE = jnp.bfloat16
WIRE_DTYPE = jnp.bfloat16
TOK_TILE = 256
WG_TOK_TILE = 1024
VMEM_LIMIT = 60 * 1024 * 1024
HALO31 = 32
HALO = 8

F32 = jnp.float32
MESH = pl.DeviceIdType.MESH


def _sigmoid(x):
    return 0.5 * jnp.tanh(0.5 * x) + 0.5


def _shifted_rows(ext_ref, rot_ref, n_rows):
    span = rot_ref.shape[1]
    for r in range(1, 8):
        rot_ref[r - 1] = ext_ref[pl.ds(r, span), :]

    def window(o, r0=0, cols=slice(None)):
        q, r = divmod(o, 8)
        if r == 0:
            return ext_ref[pl.ds(8 * q + r0, n_rows), cols]
        return rot_ref[r - 1, pl.ds(8 * q + r0, n_rows), cols]

    return window


def _store_hilo(hilo_ref, rows, c0, v):
    half = hilo_ref.shape[1] // 2
    hi = v.astype(hilo_ref.dtype)
    hilo_ref[rows, c0:c0 + v.shape[1]] = hi
    hilo_ref[rows, half + c0:half + c0 + v.shape[1]] = (v - hi.astype(F32)).astype(hilo_ref.dtype)


def _scan_bwd_blocks(a_ref, g_ref, car, n_rows):
    row = lax.broadcasted_iota(jnp.int32, (8, LANE), 0)
    for c0 in range(0, a_ref.shape[1], LANE):
        cols = slice(c0, c0 + LANE)
        a_next = jnp.broadcast_to(car[0:1, cols], (8, LANE))
        g_next = jnp.broadcast_to(car[1:2, cols], (8, LANE))
        for r0 in range(n_rows - 8, -1, -8):
            a_blk, g = a_ref[r0:r0 + 8, cols], g_ref[r0:r0 + 8, cols]
            bm = jnp.where(row < 7, pltpu.roll(a_blk, 7, 0), a_next)
            for d in (1, 2, 4):
                keep = row < 8 - d
                g = g + bm * jnp.where(keep, pltpu.roll(g, 8 - d, 0), 0.0)
                bm = bm * jnp.where(keep, pltpu.roll(bm, 8 - d, 0), 1.0)
            g = g + bm * g_next
            g_ref[r0:r0 + 8, cols] = g
            a_next = jnp.broadcast_to(a_blk[0:1, :], (8, LANE))
            g_next = jnp.broadcast_to(g[0:1, :], (8, LANE))
        car[0:1, cols] = a_next[0:1, :]
        car[1:2, cols] = g_next[0:1, :]


def _scan_fwd_blocks(a_ref, u_ref, hcar, n_rows):
    row = lax.broadcasted_iota(jnp.int32, (8, LANE), 0)
    for c0 in range(0, a_ref.shape[1], LANE):
        cols = slice(c0, c0 + LANE)
        h_prev = jnp.broadcast_to(hcar[0:1, cols], (8, LANE))
        for r0 in range(0, n_rows, 8):
            a, u = a_ref[r0:r0 + 8, cols], u_ref[r0:r0 + 8, cols]
            for d in (1, 2, 4):
                keep = row >= d
                u = a * jnp.where(keep, pltpu.roll(u, d, 0), 0.0) + u
                a = a * jnp.where(keep, pltpu.roll(a, d, 0), 1.0)
            u = u + a * h_prev
            u_ref[r0:r0 + 8, cols] = u
            h_prev = jnp.broadcast_to(u[7:8, :], (8, LANE))
        hcar[0:1, cols] = h_prev[0:1, :]


def _dot(a, b):
    return jnp.dot(a, b, preferred_element_type=F32)


def _dot_nt(a, b):
    return lax.dot_general(a, b, (((1,), (1,)), ((), ())), preferred_element_type=F32)


def _dot_tn(a, b):
    return lax.dot_general(a, b, (((0,), (0,)), ((), ())), preferred_element_type=F32)


def _colsum(v):
    return jnp.sum(v, axis=0, keepdims=True)


def _rowmean(v):
    return jnp.mean(v, axis=-1, keepdims=True)


_GELU_C0 = math.sqrt(2.0 / math.pi)
_GELU_C1 = 0.044715


def _gelu_and_grad(x):
    x2 = x * x
    th = jnp.tanh(_GELU_C0 * (x + _GELU_C1 * x * x2))
    ge = 0.5 * x * (1.0 + th)
    dge = 0.5 * (1.0 + th) + 0.5 * x * (1.0 - th * th) * (_GELU_C0 * (1.0 + 3.0 * _GELU_C1 * x2))
    return ge, dge


def _softplus(x):
    return jnp.maximum(x, 0.0) + jnp.log1p(jnp.exp(-jnp.abs(x)))


def _layer_norm_stats(r):
    mu = _rowmean(r)
    dl = r - mu
    var = _rowmean(dl * dl)
    rstd = lax.rsqrt(var + LN_EPS)
    return dl * rstd, rstd


ROW_BLK = 32
LANE = 128


def _row_blocks(n_rows, rb, body, init):
    carry = init
    for j in range(n_rows // rb):
        carry = body(j * rb, carry)
    return carry


def _fold8(v):
    r, c = v.shape
    return v if r == 8 else v.reshape(r // 8, 8, c).sum(axis=0)


def _cast_rows(src_ref, dst_ref, n_rows):
    rb = min(16, n_rows)

    def rows(r0, carry):
        dst_ref[pl.ds(r0, rb), :] = src_ref[pl.ds(r0, rb), :].astype(dst_ref.dtype)
        return carry

    _row_blocks(n_rows, rb, rows, ())


def _full(shape):
    nd = len(shape)
    return pl.BlockSpec(shape, lambda *_: (0,) * nd)


_ANY = pl.BlockSpec(memory_space=pl.ANY)


def _params(n_grid):
    return pltpu.CompilerParams(dimension_semantics=("arbitrary",) * n_grid, vmem_limit_bytes=VMEM_LIMIT)


def _my_place():
    return lax.axis_index("x"), lax.axis_index("y"), lax.axis_index("c")


def _all_gather(arrays, name):
    n_arr = len(arrays)

    def body(*refs):
        start, forward, finish = _gather_steps(refs[:n_arr], refs[n_arr:2 * n_arr], *refs[2 * n_arr:])
        start()
        forward()
        finish()

    return pl.pallas_call(
        body, name=name, out_shape=_gather_out_shapes(arrays),
        in_specs=[_ANY] * n_arr, out_specs=[_ANY] * n_arr, scratch_shapes=_comm_sems(n_arr),
    )(*arrays)


def _gather_out_shapes(arrays):
    return [jax.ShapeDtypeStruct((NDEV * a.shape[0], a.shape[1]), a.dtype) for a in arrays]


def _comm_sems(n_arr):
    return [pltpu.SemaphoreType.DMA((n_arr, 7)), pltpu.SemaphoreType.DMA((n_arr, 7)), pltpu.SemaphoreType.DMA((n_arr,))]


def _gather_steps(x_refs, out_refs, send_sems, recv_sems, local_sems):
    n_arr = len(x_refs)
    x, y, c = _my_place()
    me, sibling = (x, y, c), (x, y, 1 - c)
    chips = [(1 - x, y), (x, 1 - y), (1 - x, 1 - y)]

    def rows(k, px, py, pc):
        m = x_refs[k].shape[0]
        return out_refs[k].at[pl.ds((4 * px + 2 * py + pc) * m, m), :]

    def copy(k, s, block, to, src=None):
        return pltpu.make_async_remote_copy(
            src_ref=rows(k, *block) if src is None else src, dst_ref=rows(k, *block),
            send_sem=send_sems.at[k, s], recv_sem=recv_sems.at[k, s], device_id=to, device_id_type=MESH)

    def mine():
        return [pltpu.make_async_copy(x_refs[k], rows(k, *me), local_sems.at[k]) for k in range(n_arr)]

    def first():
        cps = []
        for k in range(n_arr):
            cps.append(copy(k, 0, me, sibling, src=x_refs[k]))
            cps += [copy(k, 1 + j, me, (*chip, c), src=x_refs[k]) for j, chip in enumerate(chips)]
        return cps

    def passed():
        return [copy(k, 4 + j, (*chip, c), sibling) for k in range(n_arr) for j, chip in enumerate(chips)]

    def start():
        for cp in mine() + first():
            cp.start()

    def forward():
        fwd = passed()
        for k in range(n_arr):
            for j, chip in enumerate(chips):
                copy(k, 1 + j, (*chip, c), me).wait_recv()
                fwd[3 * k + j].start()

    def finish():
        for k in range(n_arr):
            copy(k, 0, sibling, me).wait_recv()
            for j, chip in enumerate(chips):
                copy(k, 4 + j, (*chip, 1 - c), me).wait_recv()
        for cp in first() + passed():
            cp.wait_send()
        for cp in mine():
            cp.wait()

    return start, forward, finish


def _exchange_and_gather(ex_arrays, ga_arrays, name, half=None):
    n_e, n_g = len(ex_arrays), len(ga_arrays)

    def body(*refs):
        e_in, g_in, refs = refs[:n_e], refs[n_e:n_e + n_g], refs[n_e + n_g:]
        e_out, g_out, sems = refs[:n_e], refs[n_e:n_e + n_g], refs[n_e + n_g:]
        e_start, e_finish = _exchange_steps(e_in, e_out, *sems[:3], half=half)
        g_start, g_forward, g_finish = _gather_steps(g_in, g_out, *sems[3:])
        g_start()
        e_start()
        g_forward()
        g_finish()
        e_finish()

    res = pl.pallas_call(
        body, name=name, out_shape=_exchange_out_shapes(ex_arrays, half) + _gather_out_shapes(ga_arrays),
        in_specs=[_ANY] * (n_e + n_g), out_specs=[_ANY] * (n_e + n_g),
        scratch_shapes=_comm_sems(n_e) + _comm_sems(n_g),
    )(*ex_arrays, *ga_arrays)
    return res[:n_e], res[n_e:]


def _exchange_out_shapes(arrays, half=None):
    n_blocks = NDEV if half is None else NDEV // 2
    return [jax.ShapeDtypeStruct((NDEV, a.shape[0] // n_blocks, a.shape[1]), a.dtype) for a in arrays]


def _exchange_steps(g_refs, out_refs, send_sems, recv_sems, local_sems, half=None):
    n_arr = len(g_refs)
    x, y, c = _my_place()
    me = 4 * x + 2 * y + c
    receives = True if half is None else c == half

    def when(cond, fn):
        if cond is True:
            fn()
        else:
            pl.when(cond)(fn)

    def sends(rel):
        return True if half is None else (c ^ (rel & 1)) == half

    def copy(k, rel):
        px, py, pc = x ^ ((rel >> 2) & 1), y ^ ((rel >> 1) & 1), c ^ (rel & 1)
        m = out_refs[k].shape[1]
        blk = 4 * px + 2 * py + pc if half is None else 2 * px + py
        return pltpu.make_async_remote_copy(
            src_ref=g_refs[k].at[pl.ds(blk * m, m), :], dst_ref=out_refs[k].at[me],
            send_sem=send_sems.at[k, rel - 1], recv_sem=recv_sems.at[k, rel - 1],
            device_id=(px, py, pc), device_id_type=MESH)

    def local(k):
        m = out_refs[k].shape[1]
        blk = me if half is None else 2 * x + y
        return pltpu.make_async_copy(g_refs[k].at[pl.ds(blk * m, m), :], out_refs[k].at[me], local_sems.at[k])

    def start():
        for k in range(n_arr):
            when(receives, local(k).start)
            for rel in range(1, NDEV):
                when(sends(rel), copy(k, rel).start)

    def finish():
        for k in range(n_arr):
            for rel in range(1, NDEV):
                when(sends(rel), copy(k, rel).wait_send)
                when(receives, copy(k, rel).wait_recv)
            when(receives, local(k).wait)

    return start, finish


def _ada_fwd(c_all, w_ada_loc, b_ada_loc):
    def body(c_ref, w_ref, b_ref, o_ref):
        cv = c_ref[...]
        ca = (cv * _sigmoid(cv)).astype(MXU_DTYPE)
        o_ref[...] = _dot(ca, w_ref[...].astype(MXU_DTYPE)) + b_ref[...]

    return pl.pallas_call(
        body, name="ada_fwd", out_shape=jax.ShapeDtypeStruct((c_all.shape[0], w_ada_loc.shape[1]), F32),
        in_specs=[_full(c_all.shape), _full(w_ada_loc.shape), _full(b_ada_loc.shape)],
        out_specs=_full((c_all.shape[0], w_ada_loc.shape[1])),
        compiler_params=pltpu.CompilerParams(vmem_limit_bytes=VMEM_LIMIT),
    )(c_all, w_ada_loc, b_ada_loc)


def _ada_bwd(c_all, dmod_all, dmod_cols):
    def body(c_ref, da_ref, d_ref, o_ref, b_ref):
        cv = c_ref[...]
        ca = (cv * _sigmoid(cv)).astype(MXU_DTYPE)
        o_ref[...] = _dot_tn(ca, d_ref[...].astype(MXU_DTYPE))
        b_ref[...] = _colsum(da_ref[...])

    return pl.pallas_call(
        body, name="ada_bwd",
        out_shape=[jax.ShapeDtypeStruct((c_all.shape[1], dmod_cols.shape[1]), F32),
                   jax.ShapeDtypeStruct((1, dmod_all.shape[1]), F32)],
        in_specs=[_full(c_all.shape), _full(dmod_all.shape), _full(dmod_cols.shape)],
        out_specs=[_full((c_all.shape[1], dmod_cols.shape[1])), _full((1, dmod_all.shape[1]))],
        compiler_params=pltpu.CompilerParams(vmem_limit_bytes=VMEM_LIMIT),
    )(c_all, dmod_all, dmod_cols)


def _mixer_fwd(x, mods, win_t, wout, wri, mavg2, va, vb, ln1, ffn_shards):
    bl, seq, _ = x.shape
    t = min(TOK_TILE, seq)
    nt = seq // t
    n_g = len(ffn_shards)
    rb = min(ROW_BLK, t)

    def body(x_ref, mod_ref, win_hbm, wout_hbm, wri_ref, mavg_ref, va_ref, vb_ref, ln1_ref, *rest):
        shard_refs, rest = rest[:n_g], rest[n_g:]
        proj_ref, h_ref, vc_ref, y_ref, mixed_ref, x1_ref, u1_ref = rest[:7]
        gathered_refs, rest = rest[7:7 + n_g], rest[7 + n_g:]
        (win_v, wout_v, xa_ext, vg_ext, rot_a, hcar, sems, rot_c, xc_scr, a_scr, st_a, st_b, xcb_scr, gates, hilo,
         g_send, g_recv, g_local) = rest
        b, i = pl.program_id(0), pl.program_id(1)
        step = b * nt + i
        g_start, g_forward, g_finish = _gather_steps(shard_refs, gathered_refs, g_send, g_recv, g_local)

        @pl.when(step == 0)
        def _():
            g_start()
            cps = [pltpu.make_async_copy(win_hbm, win_v, sems.at[0]), pltpu.make_async_copy(wout_hbm, wout_v, sems.at[1])]
            for cp in cps:
                cp.start()
            for cp in cps:
                cp.wait()

        pl.when(step == (bl * nt) // 2)(g_forward)

        @pl.when(i == 0)
        def _():
            xa_ext[0:HALO, :] = jnp.zeros((HALO, LW), F32)
            vg_ext[0:HALO31, :] = jnp.zeros((HALO31, CW), F32)
            hcar[...] = jnp.zeros_like(hcar)

        blocks = [(r0, c0) for c0 in range(0, LW, LANE) for r0 in range(0, t, rb)]

        def in_rows(r0, acc):
            rows = pl.ds(r0, 16)
            u1_ref[rows, :] = (x_ref[rows, :] * (1.0 + mod_ref[1:2, :]) + mod_ref[0:1, :]).astype(MXU_DTYPE)
            return acc

        _row_blocks(t, min(16, t), in_rows, ())
        proj_ref[...] = _dot_nt(u1_ref[...], win_v[...])

        xa_ext[HALO:HALO + t, :] = proj_ref[:, 0:LW]
        for k in range(LRU_K - 1):
            rot_c[k] = xa_ext[pl.ds(HALO - (LRU_K - 1) + k, t), :]
        for r0, c0 in blocks:
            rows, cols = pl.ds(r0, rb), slice(c0, c0 + LANE)
            xc = va_ref[4:5, cols] + va_ref[LRU_K - 1:LRU_K, cols] * xa_ext[pl.ds(HALO + r0, rb), cols]
            for k in range(LRU_K - 1):
                xc = xc + va_ref[k:k + 1, cols] * rot_c[k, rows, cols]
            xc_scr[rows, cols] = xc
            xcb_scr[rows, cols] = xc.astype(MXU_DTYPE)
        xa_ext[0:HALO, :] = xa_ext[t:t + HALO, :]
        gates[...] = _dot(xcb_scr[...], wri_ref[...])
        for r0, c0 in blocks:
            rows, cols = pl.ds(r0, rb), slice(c0, c0 + LANE)
            r = _sigmoid(gates[rows, cols] + va_ref[5:6, cols])
            ig = _sigmoid(gates[rows, LW + c0:LW + c0 + LANE] + va_ref[6:7, cols])
            la = (-LRU_C) * r * _softplus(-va_ref[7:8, cols])
            a = jnp.exp(la)
            a_scr[rows, cols] = a
            h_ref[rows, cols] = jnp.sqrt(-jnp.tanh(la) * (a * a + 1.0)) * (ig * xc_scr[rows, cols])
        _scan_fwd_blocks(a_scr, h_ref, hcar, t)

        for r0, c0 in blocks:
            rows, cols, pc = pl.ds(r0, rb), slice(c0, c0 + LANE), 2 * LW + c0
            vg_ext[pl.ds(HALO31 + r0, rb), cols] = (
                proj_ref[rows, pc:pc + LANE] * _sigmoid(proj_ref[rows, pc + CW:pc + CW + LANE]))
        vg_win = _shifted_rows(vg_ext, rot_a, rb)
        for r0, c0 in blocks:
            rows, cols = pl.ds(r0, rb), slice(c0, c0 + LANE)
            vc = vb_ref[CONV_K:CONV_K + 1, cols]
            for k in range(CONV_K):
                vc = vc + vb_ref[k:k + 1, cols] * vg_win(HALO31 - (CONV_K - 1) + k, r0, cols)
            vc_ref[rows, cols] = vc
            _store_hilo(hilo, rows, c0, vc)
        vg_ext[0:HALO31, :] = vg_ext[t:t + HALO31, :]
        mavg2 = mavg_ref[...]
        st_a[...] = _dot(hilo[...], mavg2)
        for r0, c0 in blocks:
            rows, cols = pl.ds(r0, rb), slice(c0, c0 + LANE)
            dl = vc_ref[rows, cols] - st_a[rows, cols]
            st_a[rows, cols] = dl
            _store_hilo(hilo, rows, c0, dl * dl)
        st_b[...] = _dot(hilo[...], mavg2)
        for r0, c0 in blocks:
            rows, cols = pl.ds(r0, rb), slice(c0, c0 + LANE)
            ge, _ = _gelu_and_grad(proj_ref[rows, LW + c0:LW + c0 + LANE])
            y_ref[rows, cols] = (ge * h_ref[rows, cols]).astype(MXU_DTYPE)
            yl = (st_a[rows, cols] * lax.rsqrt(st_b[rows, cols] + LN_EPS) * vb_ref[CONV_K + 1:CONV_K + 2, cols]
                  + vb_ref[CONV_K + 2:CONV_K + 3, cols])
            y_ref[rows, LW + c0:LW + c0 + LANE] = (yl * _sigmoid(yl)).astype(MXU_DTYPE)
        mixed_ref[...] = _dot(y_ref[...], wout_v[...])

        def out_rows(r0, acc):
            rows = pl.ds(r0, 8)
            xh, _ = _layer_norm_stats(ALPHA * x_ref[rows, :] + (1.0 + mod_ref[2:3, :]) * mixed_ref[rows, :])
            x1_ref[rows, :] = xh * ln1_ref[0:1, :] + ln1_ref[1:2, :]
            return acc

        _row_blocks(t, 8, out_rows, ())

        pl.when(step == bl * nt - 1)(g_finish)

    tok = lambda w: pl.BlockSpec((None, t, w), lambda b, i: (b, i, 0))
    outs = [(2 * LW + 2 * CW, F32), (LW, F32), (CW, F32), (D, MXU_DTYPE), (D, F32), (D, F32), (D, MXU_DTYPE)]
    return pl.pallas_call(
        body, name="mixer_fwd", grid=(bl, nt),
        out_shape=[jax.ShapeDtypeStruct((bl, seq, w), dt) for w, dt in outs] + _gather_out_shapes(ffn_shards),
        in_specs=[tok(D), pl.BlockSpec((None, 8, D), lambda b, i: (b, 0, 0)), _ANY, _ANY,
                  _full(wri.shape), _full(mavg2.shape), _full(va.shape), _full(vb.shape), _full(ln1.shape)]
        + [_ANY] * n_g,
        out_specs=[tok(w) for w, _ in outs] + [_ANY] * n_g,
        scratch_shapes=[pltpu.VMEM(win_t.shape, MXU_DTYPE), pltpu.VMEM(wout.shape, MXU_DTYPE),
                        pltpu.VMEM((t + HALO, LW), F32), pltpu.VMEM((t + HALO31, CW), F32),
                        pltpu.VMEM((7, t + HALO31 - 8, CW), F32), pltpu.VMEM((8, LW), F32),
                        pltpu.SemaphoreType.DMA((2,)), pltpu.VMEM((LRU_K - 1, t, LW), F32)]
        + [pltpu.VMEM((t, LW), F32)] * 4
        + [pltpu.VMEM((t, LW), MXU_DTYPE), pltpu.VMEM((t, 2 * LW), F32), pltpu.VMEM((t, 2 * CW), MXU_DTYPE)]
        + _comm_sems(n_g),
        compiler_params=_params(2),
    )(x, mods, win_t, wout, wri, mavg2, va, vb, ln1, *ffn_shards)


def _ffn_fwd(x1, mods, tgt, wup_t, wdown, f3, ln2):
    bl, seq, _ = x1.shape
    t = min(TOK_TILE, seq)
    nt = seq // t
    n_chunk = DFF // FF_CHUNK

    rb = min(ROW_BLK, t)

    def body(x1_ref, mod_ref, tgt_ref, wup_hbm, wdown_hbm, f3_ref, ln2_ref,
             hh_ref, z_ref, u2_ref, y2_ref, loss_ref,
             wup_v, wdown_v, g_ext, rot, sems):
        b, i = pl.program_id(0), pl.program_id(1)

        @pl.when((b == 0) & (i == 0))
        def _():
            cps = [pltpu.make_async_copy(wup_hbm, wup_v, sems.at[0]), pltpu.make_async_copy(wdown_hbm, wdown_v, sems.at[1])]
            for cp in cps:
                cp.start()
            for cp in cps:
                cp.wait()
            loss_ref[...] = jnp.zeros_like(loss_ref)

        @pl.when(i == 0)
        def _():
            for ch in range(n_chunk):
                g_ext[ch, 0:HALO, :] = jnp.zeros((HALO, FF_CHUNK), F32)

        def in_rows(r0, acc):
            rows = pl.ds(r0, 16)
            u2_ref[rows, :] = (x1_ref[rows, :] * (1.0 + mod_ref[4:5, :]) + mod_ref[3:4, :]).astype(MXU_DTYPE)
            return acc

        _row_blocks(t, min(16, t), in_rows, ())
        for ch in range(n_chunk):
            lo = ch * FF_CHUNK
            hh_ref[:, lo:lo + FF_CHUNK] = _dot_nt(u2_ref[...], wup_v[lo:lo + FF_CHUNK, :])
            hh_ref[:, DFF + lo:DFF + lo + FF_CHUNK] = _dot_nt(u2_ref[...], wup_v[DFF + lo:DFF + lo + FF_CHUNK, :])
            g_ext[ch, HALO:HALO + t, :] = hh_ref[:, DFF + lo:DFF + lo + FF_CHUNK]
            for k in range(FFN_K - 1):
                rot[k] = g_ext[ch, pl.ds(HALO - (FFN_K - 1) + k, t), :]
            for cb in range(FF_CHUNK // LANE):
                cols = slice(cb * LANE, (cb + 1) * LANE)
                fcols = slice(lo + cb * LANE, lo + (cb + 1) * LANE)
                for r0 in range(0, t, rb):
                    rows = pl.ds(r0, rb)
                    gc = (f3_ref[FFN_K:FFN_K + 1, fcols] + f3_ref[0:1, fcols] * rot[0, rows, cols]
                          + f3_ref[1:2, fcols] * rot[1, rows, cols]
                          + f3_ref[2:3, fcols] * g_ext[ch, pl.ds(HALO + r0, rb), cols])
                    z_ref[rows, fcols] = (gc * _sigmoid(gc) * hh_ref[rows, fcols]).astype(MXU_DTYPE)
            g_ext[ch, 0:HALO, :] = g_ext[ch, t:t + HALO, :]
        y2_ref[...] = _dot(z_ref[...], wdown_v[...])

        def out_rows(r0, acc):
            rows = pl.ds(r0, 8)
            xh, _ = _layer_norm_stats(ALPHA * x1_ref[rows, :] + (1.0 + mod_ref[5:6, :]) * y2_ref[rows, :])
            err = xh * ln2_ref[0:1, :] + ln2_ref[1:2, :] - tgt_ref[rows, :]
            return acc + err * err

        e2 = _row_blocks(t, 8, out_rows, jnp.zeros((8, D), F32))
        part = e2[:, 0:128]
        for j in range(1, D // 128):
            part = part + e2[:, 128 * j:128 * (j + 1)]
        loss_ref[...] += part

    tok = lambda w: pl.BlockSpec((None, t, w), lambda b, i: (b, i, 0))
    outs = [(2 * DFF, F32), (DFF, MXU_DTYPE), (D, MXU_DTYPE), (D, F32)]
    return pl.pallas_call(
        body, name="ffn_fwd", grid=(bl, nt),
        out_shape=[jax.ShapeDtypeStruct((bl, seq, w), dt) for w, dt in outs] + [jax.ShapeDtypeStruct((8, 128), F32)],
        in_specs=[tok(D), pl.BlockSpec((None, 8, D), lambda b, i: (b, 0, 0)), tok(D), _ANY, _ANY,
                  _full(f3.shape), _full(ln2.shape)],
        out_specs=[tok(w) for w, _ in outs] + [_full((8, 128))],
        scratch_shapes=[pltpu.VMEM(wup_t.shape, MXU_DTYPE), pltpu.VMEM(wdown.shape, MXU_DTYPE),
                        pltpu.VMEM((n_chunk, t + HALO, FF_CHUNK), F32), pltpu.VMEM((FFN_K - 1, t, FF_CHUNK), F32),
                        pltpu.SemaphoreType.DMA((2,))],
        compiler_params=_params(2),
    )(x1, mods, tgt, wup_t, wdown, f3, ln2)


def _ffn_bwd(x1, y2, tgt, hh, mods, wup_t, wdown, f3, ln2):
    bl, seq, _ = x1.shape
    t = min(TOK_TILE, seq)
    nt = seq // t
    n_chunk = DFF // FF_CHUNK
    rb = min(ROW_BLK, t)

    def body(x1_ref, y2_ref, tgt_ref, hh_ref, halo_ref, mod_ref, wup_hbm, wdown_hbm, f3_ref, ln2_ref,
             dx1_ref, dy2_ref, dhh_ref, dln2_ref, df3_ref, dmod_ref,
             wup_v, wdown_v, g_ext, dgc_ext, rot, dz_scr, wide_scr, sems):
        b, i = pl.program_id(0), pl.program_id(1)
        tt = nt - 1 - i

        @pl.when((b == 0) & (i == 0))
        def _():
            cps = [pltpu.make_async_copy(wup_hbm, wup_v, sems.at[0]), pltpu.make_async_copy(wdown_hbm, wdown_v, sems.at[1])]
            for cp in cps:
                cp.start()
            for cp in cps:
                cp.wait()
            dln2_ref[...] = jnp.zeros_like(dln2_ref)
            df3_ref[...] = jnp.zeros_like(df3_ref)

        @pl.when(i == 0)
        def _():
            dmod_ref[...] = jnp.zeros_like(dmod_ref)
            for ch in range(n_chunk):
                dgc_ext[ch, t:t + HALO, :] = jnp.zeros((HALO, FF_CHUNK), F32)

        def ln_rows(r0, acc):
            rows = pl.ds(r0, 8)
            x1v, y2v = x1_ref[rows, :], y2_ref[rows, :]
            gt = mod_ref[5:6, :]
            xh, rstd = _layer_norm_stats(ALPHA * x1v + (1.0 + gt) * y2v)
            g2 = ln2_ref[0:1, :]
            dx2 = (xh * g2 + ln2_ref[1:2, :] - tgt_ref[rows, :]) * (1.0 / D)
            dxh = dx2 * g2
            dr2 = rstd * (dxh - _rowmean(dxh) - xh * _rowmean(dxh * xh))
            wide_scr[rows, :] = (1.0 + gt) * dr2
            dx1_ref[rows, :] = ALPHA * dr2
            return acc[0] + dx2 * xh, acc[1] + dx2, acc[2] + dr2 * y2v

        zero_d = jnp.zeros((8, D), F32)
        acc = _row_blocks(t, 8, ln_rows, (zero_d, zero_d, zero_d))
        dln2_ref[0:1, :] += _colsum(acc[0])
        dln2_ref[1:2, :] += _colsum(acc[1])
        dmod_ref[2:3, :] += _colsum(acc[2])
        _cast_rows(wide_scr, dy2_ref, t)

        halo_keep = (tt > 0).astype(F32)
        zero_l = jnp.zeros((8, LANE), F32)
        for ch in range(n_chunk):
            lo = ch * FF_CHUNK
            dz_scr[...] = _dot_nt(dy2_ref[...], wdown_v[lo:lo + FF_CHUNK, :])
            g_ext[0:HALO, :] = halo_ref[:, lo:lo + FF_CHUNK] * halo_keep
            g_ext[HALO:HALO + t, :] = hh_ref[:, DFF + lo:DFF + lo + FF_CHUNK]
            for k in range(FFN_K - 1):
                rot[k] = g_ext[pl.ds(HALO - (FFN_K - 1) + k, t), :]
            for cb in range(FF_CHUNK // LANE):
                cols = slice(cb * LANE, (cb + 1) * LANE)
                fcols = slice(lo + cb * LANE, lo + (cb + 1) * LANE)

                def gate_rows(r0, acc, cols=cols, fcols=fcols, ch=ch):
                    rows = pl.ds(r0, rb)
                    taps = [rot[0, rows, cols], rot[1, rows, cols], g_ext[pl.ds(r0 + HALO, rb), cols]]
                    gc = f3_ref[FFN_K:FFN_K + 1, fcols]
                    for k in range(FFN_K):
                        gc = gc + f3_ref[k:k + 1, fcols] * taps[k]
                    s = _sigmoid(gc)
                    dz = dz_scr[rows, cols]
                    dhh_ref[rows, fcols] = (dz * (gc * s)).astype(MXU_DTYPE)
                    dgc = dz * hh_ref[rows, fcols] * (s * (1.0 + gc * (1.0 - s)))
                    dgc_ext[ch, rows, cols] = dgc
                    return tuple(acc[k] + _fold8(dgc * taps[k]) for k in range(FFN_K)) + (acc[FFN_K] + _fold8(dgc),)

                acc = _row_blocks(t, rb, gate_rows, (zero_l,) * (FFN_K + 1))
                for k in range(FFN_K + 1):
                    df3_ref[k:k + 1, fcols] += _colsum(acc[k])
            for k in range(FFN_K - 1):
                rot[k] = dgc_ext[ch, pl.ds(k + 1, t), :]
            for cb in range(FF_CHUNK // LANE):
                cols = slice(cb * LANE, (cb + 1) * LANE)
                fcols = slice(lo + cb * LANE, lo + (cb + 1) * LANE)
                gcols = slice(DFF + lo + cb * LANE, DFF + lo + (cb + 1) * LANE)

                def dg_rows(r0, acc, cols=cols, fcols=fcols, gcols=gcols, ch=ch):
                    rows = pl.ds(r0, rb)
                    dg = (f3_ref[2:3, fcols] * dgc_ext[ch, rows, cols] + f3_ref[1:2, fcols] * rot[0, rows, cols]
                          + f3_ref[0:1, fcols] * rot[1, rows, cols])
                    dhh_ref[rows, gcols] = dg.astype(MXU_DTYPE)
                    return acc

                _row_blocks(t, rb, dg_rows, ())
            dgc_ext[ch, t:t + HALO, :] = dgc_ext[ch, 0:HALO, :]

        wide_scr[...] = _dot(dhh_ref[...], wup_v[...])

        def out_rows(r0, acc):
            rows = pl.ds(r0, 8)
            du2 = wide_scr[rows, :]
            dx1_ref[rows, :] = dx1_ref[rows, :] + du2 * (1.0 + mod_ref[4:5, :])
            return acc[0] + du2 * x1_ref[rows, :], acc[1] + du2

        acc = _row_blocks(t, 8, out_rows, (zero_d, zero_d))
        dmod_ref[1:2, :] += _colsum(acc[0])
        dmod_ref[0:1, :] += _colsum(acc[1])

    rev = lambda w: pl.BlockSpec((None, t, w), lambda b, i: (b, nt - 1 - i, 0))
    halo = pl.BlockSpec((None, HALO, DFF), lambda b, i: (b, jnp.maximum((nt - 1 - i) * (t // HALO) - 1, 0), 1))
    return pl.pallas_call(
        body, name="ffn_bwd", grid=(bl, nt),
        out_shape=[jax.ShapeDtypeStruct((bl, seq, D), F32), jax.ShapeDtypeStruct((bl, seq, D), MXU_DTYPE),
                   jax.ShapeDtypeStruct((bl, seq, 2 * DFF), MXU_DTYPE), jax.ShapeDtypeStruct((8, D), F32),
                   jax.ShapeDtypeStruct((8, DFF), F32), jax.ShapeDtypeStruct((bl, 8, D), F32)],
        in_specs=[rev(D), rev(D), rev(D), rev(2 * DFF), halo, pl.BlockSpec((None, 8, D), lambda b, i: (b, 0, 0)),
                  _ANY, _ANY, _full(f3.shape), _full(ln2.shape)],
        out_specs=[rev(D), rev(D), rev(2 * DFF), _full((8, D)), _full((8, DFF)),
                   pl.BlockSpec((None, 8, D), lambda b, i: (b, 0, 0))],
        scratch_shapes=[pltpu.VMEM(wup_t.shape, MXU_DTYPE), pltpu.VMEM(wdown.shape, MXU_DTYPE),
                        pltpu.VMEM((t + HALO, FF_CHUNK), F32), pltpu.VMEM((n_chunk, t + HALO, FF_CHUNK), F32),
                        pltpu.VMEM((FFN_K - 1, t, FF_CHUNK), F32), pltpu.VMEM((t, FF_CHUNK), F32),
                        pltpu.VMEM((t, D), F32), pltpu.SemaphoreType.DMA((2,))],
        compiler_params=_params(2),
    )(x1, y2, tgt, hh, hh, mods, wup_t, wdown, f3, ln2)


def _mixer_bwd(x, dx1, proj, h, vc, mixed, mods, win_t, wout, wri, mavg2, va, vb, ln1, ffn_wgrads):
    bl, seq, _ = x.shape
    t = min(TOK_TILE, seq)
    nt = seq // t
    pw = 2 * LW + 2 * CW
    n_g = len(ffn_wgrads)
    rb = min(ROW_BLK, t)
    half_f32 = [pltpu.VMEM((t, LW), F32)] * 10

    def body(x_ref, dx1_ref, proj_ref, phalo_ref, h_ref, hhalo_ref, vc_ref, mixed_ref, mod_ref,
             win_hbm, wout_hbm, wri_ref, mavg_ref, va_ref, vb_ref, ln1_ref, *rest):
        wgrad_refs, rest = rest[:n_g], rest[n_g:]
        gx_ref, dproj_ref, dm_ref, ga_ref, gw31_ref, dwr_ref, dwi_ref, dln1_ref, dmod_ref = rest[:9]
        slot_refs, rest = rest[9:9 + n_g], rest[9 + n_g:]
        (win_v, wout_v, xa_ext, vg_scr, dvc_ext, dxc_ext, rot_b, car, sems,
         st_a, st_b, st_c, yn_scr, dyn_scr, hprev_scr, xc_scr, a_scr, mult_scr, g_scr,
         wide, gates, hilo_a, hilo_b, dgates, xcb_scr, rot_c, rot_d, hp_ext, e_send, e_recv, e_local) = rest
        b, i = pl.program_id(0), pl.program_id(1)
        tt = nt - 1 - i
        e_start, e_finish = _exchange_steps(wgrad_refs, slot_refs, e_send, e_recv, e_local)

        @pl.when((b == 0) & (i == 0))
        def _():
            e_start()
            cps = [pltpu.make_async_copy(win_hbm, win_v, sems.at[0]), pltpu.make_async_copy(wout_hbm, wout_v, sems.at[1])]
            for cp in cps:
                cp.start()
            for cp in cps:
                cp.wait()
            for ref in (ga_ref, gw31_ref, dwr_ref, dwi_ref, dln1_ref):
                ref[...] = jnp.zeros_like(ref)

        @pl.when(i == 0)
        def _():
            dmod_ref[...] = jnp.zeros_like(dmod_ref)
            dvc_ext[t:t + HALO31, :] = jnp.zeros((HALO31, CW), F32)
            dxc_ext[t:t + HALO, :] = jnp.zeros((HALO, LW), F32)
            car[...] = jnp.zeros_like(car)

        halo_keep = (tt > 0).astype(F32)
        zero_d = jnp.zeros((8, D), F32)
        zero_l = jnp.zeros((8, LANE), F32)
        blocks = [(r0, c0) for c0 in range(0, LW, LANE) for r0 in range(0, t, rb)]

        def add_row(ref, row, c0, acc):
            ref[row:row + 1, c0:c0 + LANE] += _colsum(acc)

        def ln_rows(r0, acc):
            rows = pl.ds(r0, 8)
            xv, mixed = x_ref[rows, :], mixed_ref[rows, :]
            gt = mod_ref[2:3, :]
            xh, rstd = _layer_norm_stats(ALPHA * xv + (1.0 + gt) * mixed)
            dx1 = dx1_ref[rows, :]
            dxh = dx1 * ln1_ref[0:1, :]
            dr1 = rstd * (dxh - _rowmean(dxh) - xh * _rowmean(dxh * xh))
            wide[rows, :] = (1.0 + gt) * dr1
            gx_ref[rows, :] = ALPHA * dr1
            return acc[0] + dx1 * xh, acc[1] + dx1, acc[2] + dr1 * mixed

        acc = _row_blocks(t, 8, ln_rows, (zero_d, zero_d, zero_d))
        dln1_ref[0:1, :] += _colsum(acc[0])
        dln1_ref[1:2, :] += _colsum(acc[1])
        dmod_ref[2:3, :] += _colsum(acc[2])
        _cast_rows(wide, dm_ref, t)
        wide[...] = _dot_nt(dm_ref[...], wout_v[...])

        mavg2 = mavg_ref[...]
        for r0, c0 in blocks:
            rows, cols = pl.ds(r0, rb), slice(c0, c0 + LANE)
            _store_hilo(hilo_a, rows, c0, vc_ref[rows, cols])
        st_a[...] = _dot(hilo_a[...], mavg2)
        for r0, c0 in blocks:
            rows, cols = pl.ds(r0, rb), slice(c0, c0 + LANE)
            dl = vc_ref[rows, cols] - st_a[rows, cols]
            yn_scr[rows, cols] = dl
            _store_hilo(hilo_a, rows, c0, dl * dl)
        st_b[...] = _dot(hilo_a[...], mavg2)
        for c0 in range(0, CW, LANE):
            cols, pc = slice(c0, c0 + LANE), 2 * LW + c0
            ng, nb = vb_ref[CONV_K + 1:CONV_K + 2, cols], vb_ref[CONV_K + 2:CONV_K + 3, cols]
            acc_g, acc_b = zero_l, zero_l
            for r0 in range(0, t, rb):
                rows = pl.ds(r0, rb)
                rs = lax.rsqrt(st_b[rows, cols] + LN_EPS)
                yn = yn_scr[rows, cols] * rs
                yl = yn * ng + nb
                s = _sigmoid(yl)
                dyl = wide[rows, LW + c0:LW + c0 + LANE] * (s * (1.0 + yl * (1.0 - s)))
                acc_g, acc_b = acc_g + _fold8(dyl * yn), acc_b + _fold8(dyl)
                dyn = dyl * ng
                st_b[rows, cols] = rs
                yn_scr[rows, cols] = yn
                dyn_scr[rows, cols] = dyn
                _store_hilo(hilo_a, rows, c0, dyn)
                _store_hilo(hilo_b, rows, c0, dyn * yn)
                vg_scr[rows, cols] = proj_ref[rows, pc:pc + LANE] * _sigmoid(proj_ref[rows, pc + CW:pc + CW + LANE])
            add_row(ga_ref, 8, c0, acc_g)
            add_row(ga_ref, 9, c0, acc_b)
        st_a[...] = _dot(hilo_a[...], mavg2)
        st_c[...] = _dot(hilo_b[...], mavg2)
        for c0 in range(0, CW, LANE):
            cols = slice(c0, c0 + LANE)
            acc_b = zero_l
            for r0 in range(0, t, rb):
                rows = pl.ds(r0, rb)
                dvc = st_b[rows, cols] * (dyn_scr[rows, cols] - st_a[rows, cols] - yn_scr[rows, cols] * st_c[rows, cols])
                dvc_ext[rows, cols] = dvc
                acc_b = acc_b + _fold8(dvc)
            add_row(ga_ref, 10, c0, acc_b)
        dvc_win = _shifted_rows(dvc_ext, rot_b, rb)
        for c0 in range(0, CW, LANE):
            cols, pc = slice(c0, c0 + LANE), 2 * LW + c0
            for k in range(CONV_K):
                acc_w = zero_l
                for r0 in range(0, t, rb):
                    acc_w = acc_w + _fold8(vg_scr[pl.ds(r0, rb), cols] * dvc_win(CONV_K - 1 - k, r0, cols))
                add_row(gw31_ref, k, c0, acc_w)
            for r0 in range(0, t, rb):
                rows = pl.ds(r0, rb)
                dvg = jnp.zeros((rb, LANE), F32)
                for k in range(CONV_K):
                    dvg = dvg + vb_ref[k:k + 1, cols] * dvc_win(CONV_K - 1 - k, r0, cols)
                vbr = proj_ref[rows, pc:pc + LANE]
                sgb = _sigmoid(proj_ref[rows, pc + CW:pc + CW + LANE])
                dproj_ref[rows, pc:pc + LANE] = (dvg * sgb).astype(MXU_DTYPE)
                dproj_ref[rows, pc + CW:pc + CW + LANE] = (dvg * vbr * sgb * (1.0 - sgb)).astype(MXU_DTYPE)
        dvc_ext[t:t + HALO31, :] = dvc_ext[0:HALO31, :]

        xa_ext[0:HALO, :] = phalo_ref[HALO31 - HALO:HALO31, 0:LW] * halo_keep
        xa_ext[HALO:HALO + t, :] = proj_ref[:, 0:LW]
        for k in range(LRU_K - 1):
            rot_c[k] = xa_ext[pl.ds(HALO - (LRU_K - 1) + k, t), :]
        hp_ext[0:HALO, :] = hhalo_ref[...] * halo_keep
        hp_ext[HALO:HALO + t, :] = h_ref[...]
        hprev_scr[...] = hp_ext[pl.ds(HALO - 1, t), :]

        def xa_tap(k, rows, r0, cols):
            return xa_ext[pl.ds(HALO + r0, rb), cols] if k == LRU_K - 1 else rot_c[k, rows, cols]

        for r0, c0 in blocks:
            rows, cols = pl.ds(r0, rb), slice(c0, c0 + LANE)
            xc = va_ref[4:5, cols]
            for k in range(LRU_K):
                xc = xc + va_ref[k:k + 1, cols] * xa_tap(k, rows, r0, cols)
            xc_scr[rows, cols] = xc
            xcb_scr[rows, cols] = xc.astype(MXU_DTYPE)
        gates[...] = _dot(xcb_scr[...], wri_ref[...])
        for r0, c0 in blocks:
            rows, cols = pl.ds(r0, rb), slice(c0, c0 + LANE)
            r = _sigmoid(gates[rows, cols] + va_ref[5:6, cols])
            ig = _sigmoid(gates[rows, LW + c0:LW + c0 + LANE] + va_ref[6:7, cols])
            la = (-LRU_C) * r * _softplus(-va_ref[7:8, cols])
            a = jnp.exp(la)
            gates[rows, cols] = r
            gates[rows, LW + c0:LW + c0 + LANE] = ig
            a_scr[rows, cols] = a
            mult_scr[rows, cols] = jnp.sqrt(-jnp.tanh(la) * (a * a + 1.0))
            ge, dge = _gelu_and_grad(proj_ref[rows, LW + c0:LW + c0 + LANE])
            dya = wide[rows, cols]
            dproj_ref[rows, LW + c0:LW + c0 + LANE] = (dya * h_ref[rows, cols] * dge).astype(MXU_DTYPE)
            g_scr[rows, cols] = dya * ge
        _scan_bwd_blocks(a_scr, g_scr, car, t)
        for c0 in range(0, LW, LANE):
            cols = slice(c0, c0 + LANE)
            sp = _softplus(-va_ref[7:8, cols])
            acc_l, acc_r, acc_i = zero_l, zero_l, zero_l
            for r0 in range(0, t, rb):
                rows = pl.ds(r0, rb)
                gv, xc, a, mult = g_scr[rows, cols], xc_scr[rows, cols], a_scr[rows, cols], mult_scr[rows, cols]
                r, ig = gates[rows, cols], gates[rows, LW + c0:LW + c0 + LANE]
                dla = gv * hprev_scr[rows, cols] * a - gv * (ig * xc) * (a * a / mult)
                acc_l = acc_l + _fold8(dla * ((-LRU_C) * r))
                dgr = dla * ((-LRU_C) * sp) * r * (1.0 - r)
                dgi = gv * (mult * xc) * ig * (1.0 - ig)
                acc_r, acc_i = acc_r + _fold8(dgr), acc_i + _fold8(dgi)
                dgates[rows, cols] = dgr.astype(MXU_DTYPE)
                dgates[rows, LW + c0:LW + c0 + LANE] = dgi.astype(MXU_DTYPE)
                dxc_ext[rows, cols] = gv * (mult * ig)
            add_row(ga_ref, 5, c0, acc_r)
            add_row(ga_ref, 6, c0, acc_i)
            ga_ref[7:8, cols] += _colsum(acc_l) * (-_sigmoid(-va_ref[7:8, cols]))
        dwr_ref[...] += _dot_tn(xcb_scr[...], dgates[:, 0:LW])
        dwi_ref[...] += _dot_tn(xcb_scr[...], dgates[:, LW:])
        st_a[...] = _dot_nt(dgates[...], wri_ref[...])
        for c0 in range(0, LW, LANE):
            cols = slice(c0, c0 + LANE)
            acc_b = zero_l
            for r0 in range(0, t, rb):
                rows = pl.ds(r0, rb)
                dxc = dxc_ext[rows, cols] + st_a[rows, cols]
                dxc_ext[rows, cols] = dxc
                acc_b = acc_b + _fold8(dxc)
            add_row(ga_ref, 4, c0, acc_b)
        for k in range(LRU_K - 1):
            rot_d[k] = dxc_ext[pl.ds(k + 1, t), :]
        for c0 in range(0, LW, LANE):
            cols = slice(c0, c0 + LANE)
            acc_w = [zero_l] * LRU_K
            for r0 in range(0, t, rb):
                rows = pl.ds(r0, rb)
                dxc = dxc_ext[rows, cols]
                dxa = va_ref[LRU_K - 1:LRU_K, cols] * dxc
                for k in range(LRU_K):
                    acc_w[k] = acc_w[k] + _fold8(dxc * xa_tap(k, rows, r0, cols))
                    if k < LRU_K - 1:
                        dxa = dxa + va_ref[k:k + 1, cols] * rot_d[LRU_K - 2 - k, rows, cols]
                dproj_ref[rows, cols] = dxa.astype(MXU_DTYPE)
            for k in range(LRU_K):
                add_row(ga_ref, k, c0, acc_w[k])
        dxc_ext[t:t + HALO, :] = dxc_ext[0:HALO, :]

        wide[...] = _dot(dproj_ref[...], win_v[...])

        def out_rows(r0, acc):
            rows = pl.ds(r0, 8)
            du1 = wide[rows, :]
            gx_ref[rows, :] = gx_ref[rows, :] + du1 * (1.0 + mod_ref[1:2, :])
            return acc[0] + du1 * x_ref[rows, :], acc[1] + du1

        acc = _row_blocks(t, 8, out_rows, (zero_d, zero_d))
        dmod_ref[1:2, :] += _colsum(acc[0])
        dmod_ref[0:1, :] += _colsum(acc[1])

        pl.when((b == bl - 1) & (i == nt - 1))(e_finish)

    rev = lambda w: pl.BlockSpec((None, t, w), lambda b, i: (b, nt - 1 - i, 0))

    def halo(rows_, w):
        return pl.BlockSpec((None, rows_, w), lambda b, i: (b, jnp.maximum((nt - 1 - i) * (t // rows_) - 1, 0), 0))

    return pl.pallas_call(
        body, name="mixer_bwd", grid=(bl, nt),
        out_shape=[jax.ShapeDtypeStruct((bl, seq, D), F32), jax.ShapeDtypeStruct((bl, seq, pw), MXU_DTYPE),
                   jax.ShapeDtypeStruct((bl, seq, D), MXU_DTYPE), jax.ShapeDtypeStruct((16, LW), F32),
                   jax.ShapeDtypeStruct((32, CW), F32), jax.ShapeDtypeStruct((LW, LW), F32),
                   jax.ShapeDtypeStruct((LW, LW), F32), jax.ShapeDtypeStruct((8, D), F32),
                   jax.ShapeDtypeStruct((bl, 8, D), F32)] + _exchange_out_shapes(ffn_wgrads),
        in_specs=[rev(D), rev(D), rev(pw), halo(HALO31, pw), rev(LW), halo(HALO, LW), rev(CW), rev(D),
                  pl.BlockSpec((None, 8, D), lambda b, i: (b, 0, 0)), _ANY, _ANY,
                  _full(wri.shape), _full(mavg2.shape), _full(va.shape), _full(vb.shape), _full(ln1.shape)]
        + [_ANY] * n_g,
        out_specs=[rev(D), rev(pw), rev(D), _full((16, LW)), _full((32, CW)), _full((LW, LW)), _full((LW, LW)),
                   _full((8, D)), pl.BlockSpec((None, 8, D), lambda b, i: (b, 0, 0))] + [_ANY] * n_g,
        scratch_shapes=[pltpu.VMEM(win_t.shape, MXU_DTYPE), pltpu.VMEM(wout.shape, MXU_DTYPE),
                        pltpu.VMEM((t + HALO, LW), F32), pltpu.VMEM((t, CW), F32),
                        pltpu.VMEM((t + HALO31, CW), F32), pltpu.VMEM((t + HALO, LW), F32),
                        pltpu.VMEM((7, t + HALO31 - 8, CW), F32),
                        pltpu.VMEM((8, LW), F32), pltpu.SemaphoreType.DMA((2,))] + half_f32
        + [pltpu.VMEM((t, D), F32), pltpu.VMEM((t, 2 * LW), F32), pltpu.VMEM((t, 2 * CW), MXU_DTYPE),
           pltpu.VMEM((t, 2 * CW), MXU_DTYPE), pltpu.VMEM((t, 2 * LW), MXU_DTYPE), pltpu.VMEM((t, LW), MXU_DTYPE),
           pltpu.VMEM((LRU_K - 1, t, LW), F32), pltpu.VMEM((LRU_K - 1, t, LW), F32), pltpu.VMEM((t + HALO, LW), F32)]
        + _comm_sems(n_g),
        compiler_params=_params(2),
    )(x, dx1, proj, proj, h, h, vc, mixed, mods, win_t, wout, wri, mavg2, va, vb, ln1, *ffn_wgrads)


def _weight_grad(a, bmat, tm, name, out_dtype=F32, exchange=(), exchange_half=None, col_phase=None, tok_tile=None):
    ntok, m = a.shape
    n = bmat.shape[1]
    tk = min(tok_tile or WG_TOK_TILE, ntok)
    nk = ntok // tk
    stride, col0 = (1, 0) if col_phase is None else (2, col_phase)
    m = m // stride
    ni = m // tm
    n_e = len(exchange)

    def body(a_ref, b_ref, *rest):
        e_in, o_ref, e_out = rest[:n_e], rest[n_e], rest[n_e + 1:2 * n_e + 1]
        acc_ref = rest[2 * n_e + 1]
        i, k = pl.program_id(0), pl.program_id(1)
        if n_e:
            e_start, e_finish = _exchange_steps(e_in, e_out, *rest[2 * n_e + 2:], half=exchange_half)
            pl.when((i == 0) & (k == 0))(e_start)

        @pl.when(k == 0)
        def _():
            acc_ref[...] = jnp.zeros_like(acc_ref)

        acc_ref[...] += _dot_tn(a_ref[...], b_ref[...])

        @pl.when(k == nk - 1)
        def _():
            o_ref[...] = acc_ref[...].astype(out_dtype)

        if n_e:
            pl.when((i == ni - 1) & (k == nk - 1))(e_finish)

    res = pl.pallas_call(
        body, name=name, grid=(ni, nk),
        out_shape=[jax.ShapeDtypeStruct((m, n), out_dtype)] + _exchange_out_shapes(exchange, exchange_half),
        in_specs=[pl.BlockSpec((tk, tm), lambda i, k: (k, i * stride + col0)), pl.BlockSpec((tk, n), lambda i, k: (k, 0))]
        + [_ANY] * n_e,
        out_specs=[pl.BlockSpec((tm, n), lambda i, k: (i, 0))] + [_ANY] * n_e,
        scratch_shapes=[pltpu.VMEM((tm, n), F32)] + (_comm_sems(n_e) if n_e else []),
        compiler_params=_params(2),
    )(a, bmat, *exchange)
    return res if n_e else res[0]


def _small_grad_layout(bl):
    r1 = 6 * bl
    return {"dmod": 0, "wide": r1, "ln": r1 + 32, "df3": r1 + 36, "g_wr": r1 + 48, "g_wi": r1 + 80, "rows": r1 + 112}


def _pack_small_grads(dmod_a, dmod_b, g_a, g_w31, dln1, dln2, df3, g_wr, g_wi, loss_part):
    bl = dmod_a.shape[0] // 8
    lay = _small_grad_layout(bl)
    ins = [dmod_a, dmod_b, g_a, g_w31, dln1, dln2, df3, g_wr, g_wi, loss_part]

    def body(ma_ref, mb_ref, ga_ref, gw_ref, l1_ref, l2_ref, f3_ref, wr_ref, wi_ref, ls_ref, o_ref):
        o_ref[...] = jnp.zeros_like(o_ref)
        for b in range(bl):
            o_ref[6 * b:6 * b + 3, :] = ma_ref[8 * b:8 * b + 3, :]
            o_ref[6 * b + 3:6 * b + 6, :] = mb_ref[8 * b:8 * b + 3, :]
        wide = lay["wide"]
        o_ref[wide:wide + 32, 0:CW] = gw_ref[...]
        o_ref[wide:wide + 16, CW:CW + LW] = ga_ref[...]
        o_ref[wide + 16:wide + 24, CW:CW + LANE] = ls_ref[...]
        o_ref[lay["ln"]:lay["ln"] + 2, :] = l1_ref[0:2, :]
        o_ref[lay["ln"] + 2:lay["ln"] + 4, :] = l2_ref[0:2, :]
        for j, c0 in enumerate(range(0, DFF, D)):
            w = min(D, DFF - c0)
            o_ref[lay["df3"] + 4 * j:lay["df3"] + 4 * j + 4, 0:w] = f3_ref[0:4, c0:c0 + w]
        o_ref[lay["g_wr"]:lay["g_wr"] + 32, :] = wr_ref[...]
        o_ref[lay["g_wi"]:lay["g_wi"] + 32, :] = wi_ref[...]

    return pl.pallas_call(
        body, name="pack_small_grads", out_shape=jax.ShapeDtypeStruct((lay["rows"], D), F32),
        in_specs=[_full(a.shape) for a in ins], out_specs=_full((lay["rows"], D)),
        compiler_params=pltpu.CompilerParams(vmem_limit_bytes=VMEM_LIMIT),
    )(*ins)


def _sum_small_grads(packed_all, bl):
    lay = _small_grad_layout(bl)
    shapes = {"lru_conv_w": (LRU_K, LW), "lru_conv_b": (1, LW), "lru_b_r": (1, LW), "lru_b_i": (1, LW),
              "lru_lambda": (1, LW), "conv_norm_g": (1, CW), "conv_norm_b": (1, CW), "conv_b": (1, CW),
              "conv_w": (CONV_K, CW), "ln1_g": (1, D), "ln1_b": (1, D), "ln2_g": (1, D), "ln2_b": (1, D),
              "ffn_conv_w": (FFN_K, DFF), "ffn_conv_b": (1, DFF), "lru_w_r": (32, D), "lru_w_i": (32, D),
              "loss": (8, LANE)}
    names = list(shapes)

    def body(p_ref, *outs):
        o = dict(zip(names, outs))

        def part(r0, nr, c0=0, nc=D):
            acc = p_ref[0, r0:r0 + nr, c0:c0 + nc]
            for j in range(1, NDEV):
                acc = acc + p_ref[j, r0:r0 + nr, c0:c0 + nc]
            return acc

        wide = lay["wide"]
        ga = part(wide, 16, CW, LW)
        o["lru_conv_w"][...] = ga[0:LRU_K]
        for row, k in ((4, "lru_conv_b"), (5, "lru_b_r"), (6, "lru_b_i"), (7, "lru_lambda"), (8, "conv_norm_g"),
                       (9, "conv_norm_b"), (10, "conv_b")):
            o[k][...] = ga[row:row + 1]
        o["conv_w"][...] = part(wide, 32, 0, CW)[0:CONV_K]
        o["loss"][...] = part(wide + 16, 8, CW, LANE)
        ln = part(lay["ln"], 4)
        o["ln1_g"][...], o["ln1_b"][...], o["ln2_g"][...], o["ln2_b"][...] = ln[0:1], ln[1:2], ln[2:3], ln[3:4]
        for j, c0 in enumerate(range(0, DFF, D)):
            w = min(D, DFF - c0)
            f3 = part(lay["df3"] + 4 * j, 4, 0, w)
            o["ffn_conv_w"][:, c0:c0 + w] = f3[0:FFN_K]
            o["ffn_conv_b"][:, c0:c0 + w] = f3[FFN_K:FFN_K + 1]
        o["lru_w_r"][...] = part(lay["g_wr"], 32)
        o["lru_w_i"][...] = part(lay["g_wi"], 32)

    res = pl.pallas_call(
        body, name="sum_small_grads", out_shape=[jax.ShapeDtypeStruct(shapes[k], F32) for k in names],
        in_specs=[_full(packed_all.shape)], out_specs=[_full(shapes[k]) for k in names],
        compiler_params=pltpu.CompilerParams(vmem_limit_bytes=VMEM_LIMIT),
    )(packed_all)
    return dict(zip(names, res))


def _adamw_update(w_ref, g_ref, m_ref, v_ref, d_ref, nm_ref, nv_ref):
    gv = g_ref[...]
    nm = ADAM_B1 * m_ref[...] + (1.0 - ADAM_B1) * gv
    nv = ADAM_B2 * v_ref[...] + (1.0 - ADAM_B2) * (gv * gv)
    m_hat = nm / (1.0 - ADAM_B1 ** ADAM_STEP)
    v_hat = nv / (1.0 - ADAM_B2 ** ADAM_STEP)
    d_ref[...] = -ADAM_LR * (m_hat / (jnp.sqrt(v_hat) + ADAM_EPS) + ADAM_WD * w_ref[...])
    nm_ref[...] = nm
    nv_ref[...] = nv


def _sum_adamw(slots, w, m, v, name):
    _, r, cdim = slots.shape
    tr = next((cand for cand in (64, 32) if r % cand == 0), r)

    def body(s_ref, w_ref, m_ref, v_ref, g_ref, d_ref, nm_ref, nv_ref):
        acc = s_ref[0].astype(F32)
        for j in range(1, NDEV):
            acc = acc + s_ref[j].astype(F32)
        g_ref[...] = acc
        _adamw_update(w_ref, g_ref, m_ref, v_ref, d_ref, nm_ref, nv_ref)

    spec = pl.BlockSpec((tr, cdim), lambda i: (i, 0))
    return pl.pallas_call(
        body, name=name, grid=(r // tr,), out_shape=[jax.ShapeDtypeStruct((r, cdim), F32)] * 4,
        in_specs=[pl.BlockSpec((NDEV, tr, cdim), lambda i: (0, i, 0))] + [spec] * 3, out_specs=[spec] * 4,
        compiler_params=_params(1),
    )(slots, w, m, v)


def _sum_adamw_transposed(slots_c0, slots_c1, w, m, v, name):
    _, r, cdim = slots_c0.shape
    tc = 256

    def body(s0_ref, s1_ref, w_ref, m_ref, v_ref, g_ref, d_ref, nm_ref, nv_ref):
        def total(s_ref):
            acc = s_ref[0].astype(F32)
            for j in range(1, NDEV):
                acc = acc + s_ref[j].astype(F32)
            g_ref[...] = acc.T

        c = lax.axis_index("c")
        pl.when(c == 0)(functools.partial(total, s0_ref))
        pl.when(c == 1)(functools.partial(total, s1_ref))
        _adamw_update(w_ref, g_ref, m_ref, v_ref, d_ref, nm_ref, nv_ref)

    slot_spec = pl.BlockSpec((NDEV, r, tc), lambda j: (0, 0, j))
    spec = pl.BlockSpec((tc, r), lambda j: (j, 0))
    return pl.pallas_call(
        body, name=name, grid=(cdim // tc,), out_shape=[jax.ShapeDtypeStruct((cdim, r), F32)] * 4,
        in_specs=[slot_spec, slot_spec] + [spec] * 3, out_specs=[spec] * 4,
        compiler_params=_params(1),
    )(slots_c0, slots_c1, w, m, v)


def _adamw_many(ws, gs, ms, vs, name):
    n = len(ws)

    def body(*refs):
        ins, outs = refs[:4 * n], refs[4 * n:]
        for k in range(n):
            _adamw_update(ins[k], ins[n + k], ins[2 * n + k], ins[3 * n + k], outs[k], outs[n + k], outs[2 * n + k])

    specs = [_full(w.shape) for w in ws]
    res = pl.pallas_call(
        body, name=name, out_shape=[jax.ShapeDtypeStruct(w.shape, F32) for w in ws] * 3,
        in_specs=specs * 4, out_specs=specs * 3,
        compiler_params=pltpu.CompilerParams(vmem_limit_bytes=VMEM_LIMIT),
    )(*ws, *gs, *ms, *vs)
    return res[:n], res[n:2 * n], res[2 * n:]


def _adamw(w, g, m, v, name):
    r, cdim = w.shape
    tr = next((cand for cand in (256, 128) if r % cand == 0), r)
    body = functools.partial(_adamw_update)
    spec = pl.BlockSpec((tr, cdim), lambda i: (i, 0))
    return pl.pallas_call(
        body, name=name, grid=(r // tr,), out_shape=[jax.ShapeDtypeStruct((r, cdim), F32)] * 3,
        in_specs=[spec] * 4, out_specs=[spec] * 3, compiler_params=_params(1),
    )(w, g, m, v)


def _pack(arrs, width=D, row_mult=8):
    parts = []
    for a in arrs:
        flat = a.reshape(-1)
        pad = (-flat.shape[0]) % width
        parts.append(jnp.pad(flat, (0, pad)))
    flat = jnp.concatenate(parts)
    pad = (-flat.shape[0]) % (width * row_mult)
    return jnp.pad(flat, (0, pad)).reshape(-1, width)


def _unpack_gathered(buf, shapes, width=D):
    out, row = [], 0
    for shp in shapes:
        size = math.prod(shp)
        nrow = -(-size // width)
        out.append(buf[:, row:row + nrow].reshape(NDEV, -1)[:, :size].reshape((NDEV,) + tuple(shp)))
        row += nrow
    return out


def _block_diag(w):
    hn, dh, _ = w.shape
    eye = jnp.eye(hn, dtype=w.dtype)
    return (w[:, :, None, :] * eye[:, None, :, None]).reshape(hn * dh, hn * dh)


def _diag_blocks(wfull, hn):
    dh = wfull.shape[0] // hn
    keep = np.eye(hn, dtype=bool)[:, None, :, None]
    return jnp.where(keep, wfull.reshape(hn, dh, hn, dh), 0.0).sum(axis=2)


def _pad_rows(a, rows):
    return jnp.pad(a, ((0, rows - a.shape[0]), (0, 0)))


def kernel(x, c, w_ada, b_ada, w_in, lru_conv_w, lru_conv_b, lru_w_r, lru_b_r, lru_w_i, lru_b_i, lru_lambda, conv_w, conv_b, conv_norm_g, conv_norm_b, w_out, ln1_g, ln1_b, ffn_w_up, ffn_conv_w, ffn_conv_b, ffn_w_down, ln2_g, ln2_b, loss_target, m_w_ada, m_b_ada, m_w_in, m_lru_conv_w, m_lru_conv_b, m_lru_w_r, m_lru_b_r, m_lru_w_i, m_lru_b_i, m_lru_lambda, m_conv_w, m_conv_b, m_conv_norm_g, m_conv_norm_b, m_w_out, m_ln1_g, m_ln1_b, m_ffn_w_up, m_ffn_conv_w, m_ffn_conv_b, m_ffn_w_down, m_ln2_g, m_ln2_b, v_w_ada, v_b_ada, v_w_in, v_lru_conv_w, v_lru_conv_b, v_lru_w_r, v_lru_b_r, v_lru_w_i, v_lru_b_i, v_lru_lambda, v_conv_w, v_conv_b, v_conv_norm_g, v_conv_norm_b, v_w_out, v_ln1_g, v_ln1_b, v_ffn_w_up, v_ffn_conv_w, v_ffn_conv_b, v_ffn_w_down, v_ln2_g, v_ln2_b):
    weights = dict(w_ada=w_ada, b_ada=b_ada, w_in=w_in, lru_conv_w=lru_conv_w, lru_conv_b=lru_conv_b, lru_w_r=lru_w_r,
                   lru_b_r=lru_b_r, lru_w_i=lru_w_i, lru_b_i=lru_b_i, lru_lambda=lru_lambda, conv_w=conv_w, conv_b=conv_b,
                   conv_norm_g=conv_norm_g, conv_norm_b=conv_norm_b, w_out=w_out, ln1_g=ln1_g, ln1_b=ln1_b,
                   ffn_w_up=ffn_w_up, ffn_conv_w=ffn_conv_w, ffn_conv_b=ffn_conv_b, ffn_w_down=ffn_w_down, ln2_g=ln2_g,
                   ln2_b=ln2_b)
    mom_m = dict(w_ada=m_w_ada, b_ada=m_b_ada, w_in=m_w_in, lru_conv_w=m_lru_conv_w, lru_conv_b=m_lru_conv_b,
                 lru_w_r=m_lru_w_r, lru_b_r=m_lru_b_r, lru_w_i=m_lru_w_i, lru_b_i=m_lru_b_i, lru_lambda=m_lru_lambda,
                 conv_w=m_conv_w, conv_b=m_conv_b, conv_norm_g=m_conv_norm_g, conv_norm_b=m_conv_norm_b, w_out=m_w_out,
                 ln1_g=m_ln1_g, ln1_b=m_ln1_b, ffn_w_up=m_ffn_w_up, ffn_conv_w=m_ffn_conv_w, ffn_conv_b=m_ffn_conv_b,
                 ffn_w_down=m_ffn_w_down, ln2_g=m_ln2_g, ln2_b=m_ln2_b)
    mom_v = dict(w_ada=v_w_ada, b_ada=v_b_ada, w_in=v_w_in, lru_conv_w=v_lru_conv_w, lru_conv_b=v_lru_conv_b,
                 lru_w_r=v_lru_w_r, lru_b_r=v_lru_b_r, lru_w_i=v_lru_w_i, lru_b_i=v_lru_b_i, lru_lambda=v_lru_lambda,
                 conv_w=v_conv_w, conv_b=v_conv_b, conv_norm_g=v_conv_norm_g, conv_norm_b=v_conv_norm_b, w_out=v_w_out,
                 ln1_g=v_ln1_g, ln1_b=v_ln1_b, ffn_w_up=v_ffn_w_up, ffn_conv_w=v_ffn_conv_w, ffn_conv_b=v_ffn_conv_b,
                 ffn_w_down=v_ffn_w_down, ln2_g=v_ln2_g, ln2_b=v_ln2_b)
    names = list(weights)
    bl, seq, _ = x.shape
    ntok = bl * seq
    me = 4 * lax.axis_index("x") + 2 * lax.axis_index("y") + lax.axis_index("c")

    small_shapes = [(bl, D), (LRU_K, LW // NDEV), (CONV_K, CW // NDEV), (FFN_K, DFF // NDEV)]
    small = _pack([c, lru_conv_w[0], conv_w[0], ffn_conv_w[0]], width=128)
    n_small = small.shape[0]
    small_all, win_t, wout_b = _all_gather(
        [small, w_in[0].T.astype(MXU_DTYPE), w_out[0].astype(MXU_DTYPE)], "gather_small_and_mixer_weights")
    small_all = small_all.reshape(NDEV, n_small, 128)
    c_parts, k4_parts, k31_parts, k3_parts = _unpack_gathered(small_all, small_shapes, width=128)
    c_all = c_parts.reshape(NDEV * bl, D)
    lru_conv_w_f = k4_parts.transpose(1, 0, 2).reshape(LRU_K, LW)
    conv_w_f = k31_parts.transpose(1, 0, 2).reshape(CONV_K, CW)
    ffn_conv_w_f = k3_parts.transpose(1, 0, 2).reshape(FFN_K, DFF)

    ncol = w_ada.shape[2]
    b_ada_loc = lax.dynamic_slice(b_ada, (0, me * ncol), (1, ncol))
    mod_cols = _ada_fwd(c_all, w_ada[0], b_ada_loc)
    (mod_all,) = _all_gather([mod_cols], "gather_mod")
    mod_all = mod_all.reshape(NDEV, NDEV * bl, ncol)
    mod_mine = lax.dynamic_slice(mod_all, (0, me * bl, 0), (NDEV, bl, ncol))
    mods = mod_mine.transpose(1, 0, 2).reshape(bl, 6, D)
    mods = jnp.pad(mods, ((0, 0), (0, 2), (0, 0)))

    ffn_shards = [ffn_w_up[0].T.astype(MXU_DTYPE), ffn_w_down[0].astype(MXU_DTYPE)]

    wr_bd = _block_diag(lru_w_r[0]).astype(MXU_DTYPE)
    wi_bd = _block_diag(lru_w_i[0]).astype(MXU_DTYPE)
    mavg_np = np.kron(np.eye(CW // HEAD, dtype=np.float32), np.full((HEAD, HEAD), 1.0 / HEAD, np.float32))
    mavg = jnp.asarray(mavg_np, MXU_DTYPE)
    mavg2 = jnp.asarray(np.concatenate([mavg_np, mavg_np], axis=0), MXU_DTYPE)
    wri = jnp.concatenate([wr_bd, wi_bd], axis=1)
    va = jnp.concatenate([lru_conv_w_f, lru_conv_b, lru_b_r, lru_b_i, lru_lambda], axis=0)
    vb = _pad_rows(jnp.concatenate([conv_w_f, conv_b, conv_norm_g, conv_norm_b], axis=0), 40)
    ln1 = _pad_rows(jnp.concatenate([ln1_g, ln1_b], axis=0), 8)
    ln2 = _pad_rows(jnp.concatenate([ln2_g, ln2_b], axis=0), 8)
    f3 = _pad_rows(jnp.concatenate([ffn_conv_w_f, ffn_conv_b], axis=0), 8)

    proj, h, vc, y_b, mixed, x1, u1_b, wup_t, wdown_b = _mixer_fwd(
        x, mods, win_t, wout_b, wri, mavg2, va, vb, ln1, ffn_shards)
    hh, z_b, u2_b, y2, loss_part = _ffn_fwd(x1, mods, loss_target, wup_t, wdown_b, f3, ln2)

    dx1, dy2_b, dhh_b, dln2, df3, dmod_b = _ffn_bwd(x1, y2, loss_target, hh, mods, wup_t, wdown_b, f3, ln2)
    g_down = _weight_grad(z_b.reshape(ntok, DFF), dy2_b.reshape(ntok, D), FF_CHUNK, "wgrad_down", out_dtype=WIRE_DTYPE)
    g_up_t = _weight_grad(dhh_b.reshape(ntok, 2 * DFF), u2_b.reshape(ntok, D), FF_CHUNK, "wgrad_up",
                          out_dtype=WIRE_DTYPE)
    grad_x, dproj_b, dm_b, g_a, g_w31, g_wr, g_wi, dln1, dmod_a, s_up, s_down = _mixer_bwd(
        x, dx1, proj, h, vc, mixed, mods, win_t, wout_b, wri, mavg2, va, vb, ln1, [g_up_t, g_down])
    dproj_2d, u1_2d = dproj_b.reshape(ntok, 2 * LW + 2 * CW), u1_b.reshape(ntok, D)
    blk_in = (2 * LW + 2 * CW) // NDEV
    g_out_b = _weight_grad(y_b.reshape(ntok, D), dm_b.reshape(ntok, D), D, "wgrad_out", out_dtype=WIRE_DTYPE)
    g_in_c0, s_out = _weight_grad(dproj_2d, u1_2d, blk_in, "wgrad_in_c0", out_dtype=WIRE_DTYPE, exchange=[g_out_b],
                                  col_phase=0, tok_tile=2 * WG_TOK_TILE)
    g_in_c1, s_in_c0 = _weight_grad(dproj_2d, u1_2d, blk_in, "wgrad_in_c1", out_dtype=WIRE_DTYPE, exchange=[g_in_c0],
                                    exchange_half=0, col_phase=1, tok_tile=2 * WG_TOK_TILE)

    heads = LW // HEAD
    packed = _pack_small_grads(
        dmod_a.reshape(bl * 8, D), dmod_b.reshape(bl * 8, D), g_a, g_w31, dln1, dln2, df3,
        _diag_blocks(g_wr, heads).reshape(32, D), _diag_blocks(g_wi, heads).reshape(32, D), loss_part)
    n_rows = packed.shape[0]
    (s_in_c1,), (packed_all,) = _exchange_and_gather(
        [g_in_c1], [packed], "exchange_w_in_grads_gather_small_grads", half=1)
    packed_all = packed_all.reshape(NDEV, n_rows, D)
    full = _sum_small_grads(packed_all, bl)
    loss = jnp.sum(full.pop("loss")) * (0.5 / D)
    dmod_all = packed_all[:, 0:6 * bl].reshape(NDEV * bl, 6 * D)

    grads, delta, new_m, new_v = {}, {}, {}, {}
    for k, slots in (("w_out", s_out), ("ffn_w_down", s_down)):
        g_, d_, m_, v_ = _sum_adamw(slots, weights[k][0], mom_m[k][0], mom_v[k][0], "sum_adamw_" + k)
        grads[k], delta[k], new_m[k], new_v[k] = g_[None], d_[None], m_[None], v_[None]
    g_w_ada, g_b_ada = _ada_bwd(c_all, dmod_all, lax.dynamic_slice(dmod_all, (0, me * ncol), (NDEV * bl, ncol)))
    grads["w_ada"], grads["b_ada"] = g_w_ada[None], g_b_ada
    k = "w_in"
    g_, d_, m_, v_ = _sum_adamw_transposed(s_in_c0, s_in_c1, weights[k][0], mom_m[k][0], mom_v[k][0], "sum_adamw_" + k)
    grads[k], delta[k], new_m[k], new_v[k] = g_[None], d_[None], m_[None], v_[None]
    k = "ffn_w_up"
    g_, d_, m_, v_ = _sum_adamw(s_up, weights[k][0].T, mom_m[k][0].T, mom_v[k][0].T, "sum_adamw_" + k)
    grads[k], delta[k], new_m[k], new_v[k] = g_.T[None], d_.T[None], m_.T[None], v_.T[None]
    for k, gk in full.items():
        if k in ("lru_conv_w", "conv_w", "ffn_conv_w"):
            nloc = gk.shape[1] // NDEV
            gk = lax.dynamic_slice(gk, (0, me * nloc), (gk.shape[0], nloc))
        grads[k] = gk.reshape(weights[k].shape)

    big = ("w_ada", "w_in", "w_out", "ffn_w_up", "ffn_w_down")
    for k in ("w_ada",):
        d_, m_, v_ = _adamw(weights[k][0], grads[k][0], mom_m[k][0], mom_v[k][0], "adamw_" + k)
        delta[k], new_m[k], new_v[k] = d_[None], m_[None], v_[None]
    small_names = [k for k in names if k not in big]
    d_, m_, v_ = _adamw_many([weights[k] for k in small_names], [grads[k] for k in small_names],
                             [mom_m[k] for k in small_names], [mom_v[k] for k in small_names], "adamw_small")
    for k, dk, mk, vk in zip(small_names, d_, m_, v_):
        delta[k], new_m[k], new_v[k] = dk, mk, vk

    return (loss, grad_x, *[grads[k] for k in names], *[delta[k] for k in names], *[new_m[k] for k in names],
            *[new_v[k] for k in names])
```

```python
import functools
import math

import jax
import jax.numpy as jnp
import numpy as np
from jax import lax
from jax.experimental import pallas as pl
from jax.experimental.pallas import tpu as pltpu

NDEV = 8
D = 1024
LW = 512
CW = 512
HEAD = 64
DFF = 2816
FF_CHUNK = 1408
LRU_K = 4
CONV_K = 31
FFN_K = 3
LRU_C = 8.0
ALPHA = (2 * 1) ** 0.25
LN_EPS = 1e-5
ADAM_LR = 0.001
ADAM_B1 = 0.9
ADAM_B2 = 0.999
ADAM_EPS = 1e-08
ADAM_WD = 0.01
ADAM_STEP = 10

MXU_DTYPE = jnp.bfloat16
WIRE_DTYPE = jnp.bfloat16
TOK_TILE = 256
WG_TOK_TILE = 1024
VMEM_LIMIT = 60 * 1024 * 1024
HALO31 = 32
HALO = 8

F32 = jnp.float32
MESH = pl.DeviceIdType.MESH


def _sigmoid(x):
    return 0.5 * jnp.tanh(0.5 * x) + 0.5


def _shifted_rows(ext_ref, rot_ref, n_rows):
    span = rot_ref.shape[1]
    for r in range(1, 8):
        rot_ref[r - 1] = ext_ref[pl.ds(r, span), :]

    def window(o, r0=0, cols=slice(None)):
        q, r = divmod(o, 8)
        if r == 0:
            return ext_ref[pl.ds(8 * q + r0, n_rows), cols]
        return rot_ref[r - 1, pl.ds(8 * q + r0, n_rows), cols]

    return window


def _store_hilo(hilo_ref, rows, c0, v):
    half = hilo_ref.shape[1] // 2
    hi = v.astype(hilo_ref.dtype)
    hilo_ref[rows, c0:c0 + v.shape[1]] = hi
    hilo_ref[rows, half + c0:half + c0 + v.shape[1]] = (v - hi.astype(F32)).astype(hilo_ref.dtype)


def _scan_bwd_blocks(a_ref, g_ref, car, n_rows):
    row = lax.broadcasted_iota(jnp.int32, (8, LANE), 0)
    for c0 in range(0, a_ref.shape[1], LANE):
        cols = slice(c0, c0 + LANE)
        a_next = jnp.broadcast_to(car[0:1, cols], (8, LANE))
        g_next = jnp.broadcast_to(car[1:2, cols], (8, LANE))
        for r0 in range(n_rows - 8, -1, -8):
            a_blk, g = a_ref[r0:r0 + 8, cols], g_ref[r0:r0 + 8, cols]
            bm = jnp.where(row < 7, pltpu.roll(a_blk, 7, 0), a_next)
            for d in (1, 2, 4):
                keep = row < 8 - d
                g = g + bm * jnp.where(keep, pltpu.roll(g, 8 - d, 0), 0.0)
                bm = bm * jnp.where(keep, pltpu.roll(bm, 8 - d, 0), 1.0)
            g = g + bm * g_next
            g_ref[r0:r0 + 8, cols] = g
            a_next = jnp.broadcast_to(a_blk[0:1, :], (8, LANE))
            g_next = jnp.broadcast_to(g[0:1, :], (8, LANE))
        car[0:1, cols] = a_next[0:1, :]
        car[1:2, cols] = g_next[0:1, :]


def _scan_fwd_blocks(a_ref, u_ref, hcar, n_rows):
    row = lax.broadcasted_iota(jnp.int32, (8, LANE), 0)
    for c0 in range(0, a_ref.shape[1], LANE):
        cols = slice(c0, c0 + LANE)
        h_prev = jnp.broadcast_to(hcar[0:1, cols], (8, LANE))
        for r0 in range(0, n_rows, 8):
            a, u = a_ref[r0:r0 + 8, cols], u_ref[r0:r0 + 8, cols]
            for d in (1, 2, 4):
                keep = row >= d
                u = a * jnp.where(keep, pltpu.roll(u, d, 0), 0.0) + u
                a = a * jnp.where(keep, pltpu.roll(a, d, 0), 1.0)
            u = u + a * h_prev
            u_ref[r0:r0 + 8, cols] = u
            h_prev = jnp.broadcast_to(u[7:8, :], (8, LANE))
        hcar[0:1, cols] = h_prev[0:1, :]


def _dot(a, b):
    return jnp.dot(a, b, preferred_element_type=F32)


def _dot_nt(a, b):
    return lax.dot_general(a, b, (((1,), (1,)), ((), ())), preferred_element_type=F32)


def _dot_tn(a, b):
    return lax.dot_general(a, b, (((0,), (0,)), ((), ())), preferred_element_type=F32)


def _colsum(v):
    return jnp.sum(v, axis=0, keepdims=True)


def _rowmean(v):
    return jnp.mean(v, axis=-1, keepdims=True)


_GELU_C0 = math.sqrt(2.0 / math.pi)
_GELU_C1 = 0.044715


def _gelu_and_grad(x):
    x2 = x * x
    th = jnp.tanh(_GELU_C0 * (x + _GELU_C1 * x * x2))
    ge = 0.5 * x * (1.0 + th)
    dge = 0.5 * (1.0 + th) + 0.5 * x * (1.0 - th * th) * (_GELU_C0 * (1.0 + 3.0 * _GELU_C1 * x2))
    return ge, dge


def _softplus(x):
    return jnp.maximum(x, 0.0) + jnp.log1p(jnp.exp(-jnp.abs(x)))


def _layer_norm_stats(r):
    mu = _rowmean(r)
    dl = r - mu
    var = _rowmean(dl * dl)
    rstd = lax.rsqrt(var + LN_EPS)
    return dl * rstd, rstd


ROW_BLK = 32
LANE = 128


def _row_blocks(n_rows, rb, body, init):
    carry = init
    for j in range(n_rows // rb):
        carry = body(j * rb, carry)
    return carry


def _fold8(v):
    r, c = v.shape
    return v if r == 8 else v.reshape(r // 8, 8, c).sum(axis=0)


def _cast_rows(src_ref, dst_ref, n_rows):
    rb = min(16, n_rows)

    def rows(r0, carry):
        dst_ref[pl.ds(r0, rb), :] = src_ref[pl.ds(r0, rb), :].astype(dst_ref.dtype)
        return carry

    _row_blocks(n_rows, rb, rows, ())


def _full(shape):
    nd = len(shape)
    return pl.BlockSpec(shape, lambda *_: (0,) * nd)


_ANY = pl.BlockSpec(memory_space=pl.ANY)


def _params(n_grid):
    return pltpu.CompilerParams(dimension_semantics=("arbitrary",) * n_grid, vmem_limit_bytes=VMEM_LIMIT)


def _my_place():
    return lax.axis_index("x"), lax.axis_index("y"), lax.axis_index("c")


def _all_gather(arrays, name):
    n_arr = len(arrays)

    def body(*refs):
        start, forward, finish = _gather_steps(refs[:n_arr], refs[n_arr:2 * n_arr], *refs[2 * n_arr:])
        start()
        forward()
        finish()

    return pl.pallas_call(
        body, name=name, out_shape=_gather_out_shapes(arrays),
        in_specs=[_ANY] * n_arr, out_specs=[_ANY] * n_arr, scratch_shapes=_comm_sems(n_arr),
    )(*arrays)


def _gather_out_shapes(arrays):
    return [jax.ShapeDtypeStruct((NDEV * a.shape[0], a.shape[1]), a.dtype) for a in arrays]


def _comm_sems(n_arr):
    return [pltpu.SemaphoreType.DMA((n_arr, 7)), pltpu.SemaphoreType.DMA((n_arr, 7)), pltpu.SemaphoreType.DMA((n_arr,))]


def _gather_steps(x_refs, out_refs, send_sems, recv_sems, local_sems):
    n_arr = len(x_refs)
    x, y, c = _my_place()
    me, sibling = (x, y, c), (x, y, 1 - c)
    chips = [(1 - x, y), (x, 1 - y), (1 - x, 1 - y)]

    def rows(k, px, py, pc):
        m = x_refs[k].shape[0]
        return out_refs[k].at[pl.ds((4 * px + 2 * py + pc) * m, m), :]

    def copy(k, s, block, to, src=None):
        return pltpu.make_async_remote_copy(
            src_ref=rows(k, *block) if src is None else src, dst_ref=rows(k, *block),
            send_sem=send_sems.at[k, s], recv_sem=recv_sems.at[k, s], device_id=to, device_id_type=MESH)

    def mine():
        return [pltpu.make_async_copy(x_refs[k], rows(k, *me), local_sems.at[k]) for k in range(n_arr)]

    def first():
        cps = []
        for k in range(n_arr):
            cps.append(copy(k, 0, me, sibling, src=x_refs[k]))
            cps += [copy(k, 1 + j, me, (*chip, c), src=x_refs[k]) for j, chip in enumerate(chips)]
        return cps

    def passed():
        return [copy(k, 4 + j, (*chip, c), sibling) for k in range(n_arr) for j, chip in enumerate(chips)]

    def start():
        for cp in mine() + first():
            cp.start()

    def forward():
        fwd = passed()
        for k in range(n_arr):
            for j, chip in enumerate(chips):
                copy(k, 1 + j, (*chip, c), me).wait_recv()
                fwd[3 * k + j].start()

    def finish():
        for k in range(n_arr):
            copy(k, 0, sibling, me).wait_recv()
            for j, chip in enumerate(chips):
                copy(k, 4 + j, (*chip, 1 - c), me).wait_recv()
        for cp in first() + passed():
            cp.wait_send()
        for cp in mine():
            cp.wait()

    return start, forward, finish


def _exchange(ex_arrays, name, half=None):
    n_e = len(ex_arrays)

    def body(*refs):
        e_start, e_finish = _exchange_steps(refs[:n_e], refs[n_e:2 * n_e], *refs[2 * n_e:], half=half)
        e_start()
        e_finish()

    return pl.pallas_call(
        body, name=name, out_shape=_exchange_out_shapes(ex_arrays, half),
        in_specs=[_ANY] * n_e, out_specs=[_ANY] * n_e, scratch_shapes=_comm_sems(n_e),
    )(*ex_arrays)


def _exchange_out_shapes(arrays, half=None):
    n_blocks = NDEV if half is None else NDEV // 2
    return [jax.ShapeDtypeStruct((NDEV, a.shape[0] // n_blocks, a.shape[1]), a.dtype) for a in arrays]


def _exchange_steps(g_refs, out_refs, send_sems, recv_sems, local_sems, half=None):
    n_arr = len(g_refs)
    x, y, c = _my_place()
    me = 4 * x + 2 * y + c
    receives = True if half is None else c == half

    def when(cond, fn):
        if cond is True:
            fn()
        else:
            pl.when(cond)(fn)

    def sends(rel):
        return True if half is None else (c ^ (rel & 1)) == half

    def copy(k, rel):
        px, py, pc = x ^ ((rel >> 2) & 1), y ^ ((rel >> 1) & 1), c ^ (rel & 1)
        m = out_refs[k].shape[1]
        blk = 4 * px + 2 * py + pc if half is None else 2 * px + py
        return pltpu.make_async_remote_copy(
            src_ref=g_refs[k].at[pl.ds(blk * m, m), :], dst_ref=out_refs[k].at[me],
            send_sem=send_sems.at[k, rel - 1], recv_sem=recv_sems.at[k, rel - 1],
            device_id=(px, py, pc), device_id_type=MESH)

    def local(k):
        m = out_refs[k].shape[1]
        blk = me if half is None else 2 * x + y
        return pltpu.make_async_copy(g_refs[k].at[pl.ds(blk * m, m), :], out_refs[k].at[me], local_sems.at[k])

    def start():
        for k in range(n_arr):
            when(receives, local(k).start)
            for rel in range(1, NDEV):
                when(sends(rel), copy(k, rel).start)

    def finish():
        for k in range(n_arr):
            for rel in range(1, NDEV):
                when(sends(rel), copy(k, rel).wait_send)
                when(receives, copy(k, rel).wait_recv)
            when(receives, local(k).wait)

    return start, finish


def _ada_fwd(c_all, w_ada_loc, b_ada_loc):
    def body(c_ref, w_ref, b_ref, o_ref):
        cv = c_ref[...]
        ca = (cv * _sigmoid(cv)).astype(MXU_DTYPE)
        o_ref[...] = _dot(ca, w_ref[...].astype(MXU_DTYPE)) + b_ref[...]

    return pl.pallas_call(
        body, name="ada_fwd", out_shape=jax.ShapeDtypeStruct((c_all.shape[0], w_ada_loc.shape[1]), F32),
        in_specs=[_full(c_all.shape), _full(w_ada_loc.shape), _full(b_ada_loc.shape)],
        out_specs=_full((c_all.shape[0], w_ada_loc.shape[1])),
        compiler_params=pltpu.CompilerParams(vmem_limit_bytes=VMEM_LIMIT),
    )(c_all, w_ada_loc, b_ada_loc)


def _ada_bwd(c_all, dmod_all, dmod_cols):
    def body(c_ref, da_ref, d_ref, o_ref, b_ref):
        cv = c_ref[...]
        ca = (cv * _sigmoid(cv)).astype(MXU_DTYPE)
        o_ref[...] = _dot_tn(ca, d_ref[...].astype(MXU_DTYPE))
        b_ref[...] = _colsum(da_ref[...])

    return pl.pallas_call(
        body, name="ada_bwd",
        out_shape=[jax.ShapeDtypeStruct((c_all.shape[1], dmod_cols.shape[1]), F32),
                   jax.ShapeDtypeStruct((1, dmod_all.shape[1]), F32)],
        in_specs=[_full(c_all.shape), _full(dmod_all.shape), _full(dmod_cols.shape)],
        out_specs=[_full((c_all.shape[1], dmod_cols.shape[1])), _full((1, dmod_all.shape[1]))],
        compiler_params=pltpu.CompilerParams(vmem_limit_bytes=VMEM_LIMIT),
    )(c_all, dmod_all, dmod_cols)


def _mixer_fwd(x, mods, win_t, wout, wri, mavg2, va, vb, ln1, ffn_shards):
    bl, seq, _ = x.shape
    t = min(TOK_TILE, seq)
    nt = seq // t
    n_g = len(ffn_shards)
    rb = min(ROW_BLK, t)

    def body(x_ref, mod_ref, win_hbm, wout_hbm, wri_ref, mavg_ref, va_ref, vb_ref, ln1_ref, *rest):
        shard_refs, rest = rest[:n_g], rest[n_g:]
        proj_ref, h_ref, vc_ref, y_ref, mixed_ref, x1_ref, u1_ref = rest[:7]
        gathered_refs, rest = rest[7:7 + n_g], rest[7 + n_g:]
        (win_v, wout_v, xa_ext, vg_ext, rot_a, hcar, sems, rot_c, xc_scr, a_scr, st_a, st_b, xcb_scr, gates, hilo,
         g_send, g_recv, g_local) = rest
        b, i = pl.program_id(0), pl.program_id(1)
        step = b * nt + i
        g_start, g_forward, g_finish = _gather_steps(shard_refs, gathered_refs, g_send, g_recv, g_local)

        @pl.when(step == 0)
        def _():
            g_start()
            cps = [pltpu.make_async_copy(win_hbm, win_v, sems.at[0]), pltpu.make_async_copy(wout_hbm, wout_v, sems.at[1])]
            for cp in cps:
                cp.start()
            for cp in cps:
                cp.wait()

        pl.when(step == (bl * nt) // 2)(g_forward)

        @pl.when(i == 0)
        def _():
            xa_ext[0:HALO, :] = jnp.zeros((HALO, LW), F32)
            vg_ext[0:HALO31, :] = jnp.zeros((HALO31, CW), F32)
            hcar[...] = jnp.zeros_like(hcar)

        blocks = [(r0, c0) for c0 in range(0, LW, LANE) for r0 in range(0, t, rb)]

        def in_rows(r0, acc):
            rows = pl.ds(r0, 16)
            u1_ref[rows, :] = (x_ref[rows, :] * (1.0 + mod_ref[1:2, :]) + mod_ref[0:1, :]).astype(MXU_DTYPE)
            return acc

        _row_blocks(t, min(16, t), in_rows, ())
        proj_ref[...] = _dot_nt(u1_ref[...], win_v[...])

        xa_ext[HALO:HALO + t, :] = proj_ref[:, 0:LW]
        for k in range(LRU_K - 1):
            rot_c[k] = xa_ext[pl.ds(HALO - (LRU_K - 1) + k, t), :]
        for r0, c0 in blocks:
            rows, cols = pl.ds(r0, rb), slice(c0, c0 + LANE)
            xc = va_ref[4:5, cols] + va_ref[LRU_K - 1:LRU_K, cols] * xa_ext[pl.ds(HALO + r0, rb), cols]
            for k in range(LRU_K - 1):
                xc = xc + va_ref[k:k + 1, cols] * rot_c[k, rows, cols]
            xc_scr[rows, cols] = xc
            xcb_scr[rows, cols] = xc.astype(MXU_DTYPE)
        xa_ext[0:HALO, :] = xa_ext[t:t + HALO, :]
        gates[...] = _dot(xcb_scr[...], wri_ref[...])
        for r0, c0 in blocks:
            rows, cols = pl.ds(r0, rb), slice(c0, c0 + LANE)
            r = _sigmoid(gates[rows, cols] + va_ref[5:6, cols])
            ig = _sigmoid(gates[rows, LW + c0:LW + c0 + LANE] + va_ref[6:7, cols])
            la = (-LRU_C) * r * _softplus(-va_ref[7:8, cols])
            a = jnp.exp(la)
            a_scr[rows, cols] = a
            h_ref[rows, cols] = jnp.sqrt(-jnp.tanh(la) * (a * a + 1.0)) * (ig * xc_scr[rows, cols])
        _scan_fwd_blocks(a_scr, h_ref, hcar, t)

        for r0, c0 in blocks:
            rows, cols, pc = pl.ds(r0, rb), slice(c0, c0 + LANE), 2 * LW + c0
            vg_ext[pl.ds(HALO31 + r0, rb), cols] = (
                proj_ref[rows, pc:pc + LANE] * _sigmoid(proj_ref[rows, pc + CW:pc + CW + LANE]))
        vg_win = _shifted_rows(vg_ext, rot_a, rb)
        for r0, c0 in blocks:
            rows, cols = pl.ds(r0, rb), slice(c0, c0 + LANE)
            vc = vb_ref[CONV_K:CONV_K + 1, cols]
            for k in range(CONV_K):
                vc = vc + vb_ref[k:k + 1, cols] * vg_win(HALO31 - (CONV_K - 1) + k, r0, cols)
            vc_ref[rows, cols] = vc
            _store_hilo(hilo, rows, c0, vc)
        vg_ext[0:HALO31, :] = vg_ext[t:t + HALO31, :]
        mavg2 = mavg_ref[...]
        st_a[...] = _dot(hilo[...], mavg2)
        for r0, c0 in blocks:
            rows, cols = pl.ds(r0, rb), slice(c0, c0 + LANE)
            dl = vc_ref[rows, cols] - st_a[rows, cols]
            st_a[rows, cols] = dl
            _store_hilo(hilo, rows, c0, dl * dl)
        st_b[...] = _dot(hilo[...], mavg2)
        for r0, c0 in blocks:
            rows, cols = pl.ds(r0, rb), slice(c0, c0 + LANE)
            ge, _ = _gelu_and_grad(proj_ref[rows, LW + c0:LW + c0 + LANE])
            y_ref[rows, cols] = (ge * h_ref[rows, cols]).astype(MXU_DTYPE)
            yl = (st_a[rows, cols] * lax.rsqrt(st_b[rows, cols] + LN_EPS) * vb_ref[CONV_K + 1:CONV_K + 2, cols]
                  + vb_ref[CONV_K + 2:CONV_K + 3, cols])
            y_ref[rows, LW + c0:LW + c0 + LANE] = (yl * _sigmoid(yl)).astype(MXU_DTYPE)
        mixed_ref[...] = _dot(y_ref[...], wout_v[...])

        def out_rows(r0, acc):
            rows = pl.ds(r0, 8)
            xh, _ = _layer_norm_stats(ALPHA * x_ref[rows, :] + (1.0 + mod_ref[2:3, :]) * mixed_ref[rows, :])
            x1_ref[rows, :] = xh * ln1_ref[0:1, :] + ln1_ref[1:2, :]
            return acc

        _row_blocks(t, 8, out_rows, ())

        pl.when(step == bl * nt - 1)(g_finish)

    tok = lambda w: pl.BlockSpec((None, t, w), lambda b, i: (b, i, 0))
    outs = [(2 * LW + 2 * CW, F32), (LW, F32), (CW, F32), (D, MXU_DTYPE), (D, F32), (D, F32), (D, MXU_DTYPE)]
    return pl.pallas_call(
        body, name="mixer_fwd", grid=(bl, nt),
        out_shape=[jax.ShapeDtypeStruct((bl, seq, w), dt) for w, dt in outs] + _gather_out_shapes(ffn_shards),
        in_specs=[tok(D), pl.BlockSpec((None, 8, D), lambda b, i: (b, 0, 0)), _ANY, _ANY,
                  _full(wri.shape), _full(mavg2.shape), _full(va.shape), _full(vb.shape), _full(ln1.shape)]
        + [_ANY] * n_g,
        out_specs=[tok(w) for w, _ in outs] + [_ANY] * n_g,
        scratch_shapes=[pltpu.VMEM(win_t.shape, MXU_DTYPE), pltpu.VMEM(wout.shape, MXU_DTYPE),
                        pltpu.VMEM((t + HALO, LW), F32), pltpu.VMEM((t + HALO31, CW), F32),
                        pltpu.VMEM((7, t + HALO31 - 8, CW), F32), pltpu.VMEM((8, LW), F32),
                        pltpu.SemaphoreType.DMA((2,)), pltpu.VMEM((LRU_K - 1, t, LW), F32)]
        + [pltpu.VMEM((t, LW), F32)] * 4
        + [pltpu.VMEM((t, LW), MXU_DTYPE), pltpu.VMEM((t, 2 * LW), F32), pltpu.VMEM((t, 2 * CW), MXU_DTYPE)]
        + _comm_sems(n_g),
        compiler_params=_params(2),
    )(x, mods, win_t, wout, wri, mavg2, va, vb, ln1, *ffn_shards)


def _ffn_fwd(x1, mods, tgt, wup_t, wdown, f3, ln2):
    bl, seq, _ = x1.shape
    t = min(TOK_TILE, seq)
    nt = seq // t
    n_chunk = DFF // FF_CHUNK

    rb = min(ROW_BLK, t)

    def body(x1_ref, mod_ref, tgt_ref, wup_hbm, wdown_hbm, f3_ref, ln2_ref,
             hh_ref, z_ref, u2_ref, y2_ref, loss_ref,
             wup_v, wdown_v, g_ext, rot, sems):
        b, i = pl.program_id(0), pl.program_id(1)

        @pl.when((b == 0) & (i == 0))
        def _():
            cps = [pltpu.make_async_copy(wup_hbm, wup_v, sems.at[0]), pltpu.make_async_copy(wdown_hbm, wdown_v, sems.at[1])]
            for cp in cps:
                cp.start()
            for cp in cps:
                cp.wait()
            loss_ref[...] = jnp.zeros_like(loss_ref)

        @pl.when(i == 0)
        def _():
            for ch in range(n_chunk):
                g_ext[ch, 0:HALO, :] = jnp.zeros((HALO, FF_CHUNK), F32)

        def in_rows(r0, acc):
            rows = pl.ds(r0, 16)
            u2_ref[rows, :] = (x1_ref[rows, :] * (1.0 + mod_ref[4:5, :]) + mod_ref[3:4, :]).astype(MXU_DTYPE)
            return acc

        _row_blocks(t, min(16, t), in_rows, ())
        for ch in range(n_chunk):
            lo = ch * FF_CHUNK
            hh_ref[:, lo:lo + FF_CHUNK] = _dot_nt(u2_ref[...], wup_v[lo:lo + FF_CHUNK, :])
            hh_ref[:, DFF + lo:DFF + lo + FF_CHUNK] = _dot_nt(u2_ref[...], wup_v[DFF + lo:DFF + lo + FF_CHUNK, :])
            g_ext[ch, HALO:HALO + t, :] = hh_ref[:, DFF + lo:DFF + lo + FF_CHUNK]
            for k in range(FFN_K - 1):
                rot[k] = g_ext[ch, pl.ds(HALO - (FFN_K - 1) + k, t), :]
            for cb in range(FF_CHUNK // LANE):
                cols = slice(cb * LANE, (cb + 1) * LANE)
                fcols = slice(lo + cb * LANE, lo + (cb + 1) * LANE)
                for r0 in range(0, t, rb):
                    rows = pl.ds(r0, rb)
                    gc = (f3_ref[FFN_K:FFN_K + 1, fcols] + f3_ref[0:1, fcols] * rot[0, rows, cols]
                          + f3_ref[1:2, fcols] * rot[1, rows, cols]
                          + f3_ref[2:3, fcols] * g_ext[ch, pl.ds(HALO + r0, rb), cols])
                    z_ref[rows, fcols] = (gc * _sigmoid(gc) * hh_ref[rows, fcols]).astype(MXU_DTYPE)
            g_ext[ch, 0:HALO, :] = g_ext[ch, t:t + HALO, :]
        y2_ref[...] = _dot(z_ref[...], wdown_v[...])

        def out_rows(r0, acc):
            rows = pl.ds(r0, 8)
            xh, _ = _layer_norm_stats(ALPHA * x1_ref[rows, :] + (1.0 + mod_ref[5:6, :]) * y2_ref[rows, :])
            err = xh * ln2_ref[0:1, :] + ln2_ref[1:2, :] - tgt_ref[rows, :]
            return acc + err * err

        e2 = _row_blocks(t, 8, out_rows, jnp.zeros((8, D), F32))
        part = e2[:, 0:128]
        for j in range(1, D // 128):
            part = part + e2[:, 128 * j:128 * (j + 1)]
        loss_ref[...] += part

    tok = lambda w: pl.BlockSpec((None, t, w), lambda b, i: (b, i, 0))
    outs = [(2 * DFF, F32), (DFF, MXU_DTYPE), (D, MXU_DTYPE), (D, F32)]
    return pl.pallas_call(
        body, name="ffn_fwd", grid=(bl, nt),
        out_shape=[jax.ShapeDtypeStruct((bl, seq, w), dt) for w, dt in outs] + [jax.ShapeDtypeStruct((8, 128), F32)],
        in_specs=[tok(D), pl.BlockSpec((None, 8, D), lambda b, i: (b, 0, 0)), tok(D), _ANY, _ANY,
                  _full(f3.shape), _full(ln2.shape)],
        out_specs=[tok(w) for w, _ in outs] + [_full((8, 128))],
        scratch_shapes=[pltpu.VMEM(wup_t.shape, MXU_DTYPE), pltpu.VMEM(wdown.shape, MXU_DTYPE),
                        pltpu.VMEM((n_chunk, t + HALO, FF_CHUNK), F32), pltpu.VMEM((FFN_K - 1, t, FF_CHUNK), F32),
                        pltpu.SemaphoreType.DMA((2,))],
        compiler_params=_params(2),
    )(x1, mods, tgt, wup_t, wdown, f3, ln2)


def _ffn_bwd(x1, y2, tgt, hh, mods, wup_t, wdown, f3, ln2):
    bl, seq, _ = x1.shape
    t = min(TOK_TILE, seq)
    nt = seq // t
    n_chunk = DFF // FF_CHUNK
    rb = min(ROW_BLK, t)

    def body(x1_ref, y2_ref, tgt_ref, hh_ref, halo_ref, mod_ref, wup_hbm, wdown_hbm, f3_ref, ln2_ref,
             dx1_ref, dy2_ref, dhh_ref, dln2_ref, df3_ref, dmod_ref,
             wup_v, wdown_v, g_ext, dgc_ext, rot, dz_scr, wide_scr, sems):
        b, i = pl.program_id(0), pl.program_id(1)
        tt = nt - 1 - i

        @pl.when((b == 0) & (i == 0))
        def _():
            cps = [pltpu.make_async_copy(wup_hbm, wup_v, sems.at[0]), pltpu.make_async_copy(wdown_hbm, wdown_v, sems.at[1])]
            for cp in cps:
                cp.start()
            for cp in cps:
                cp.wait()
            dln2_ref[...] = jnp.zeros_like(dln2_ref)
            df3_ref[...] = jnp.zeros_like(df3_ref)

        @pl.when(i == 0)
        def _():
            dmod_ref[...] = jnp.zeros_like(dmod_ref)
            for ch in range(n_chunk):
                dgc_ext[ch, t:t + HALO, :] = jnp.zeros((HALO, FF_CHUNK), F32)

        def ln_rows(r0, acc):
            rows = pl.ds(r0, 8)
            x1v, y2v = x1_ref[rows, :], y2_ref[rows, :]
            gt = mod_ref[5:6, :]
            xh, rstd = _layer_norm_stats(ALPHA * x1v + (1.0 + gt) * y2v)
            g2 = ln2_ref[0:1, :]
            dx2 = (xh * g2 + ln2_ref[1:2, :] - tgt_ref[rows, :]) * (1.0 / D)
            dxh = dx2 * g2
            dr2 = rstd * (dxh - _rowmean(dxh) - xh * _rowmean(dxh * xh))
            wide_scr[rows, :] = (1.0 + gt) * dr2
            dx1_ref[rows, :] = ALPHA * dr2
            return acc[0] + dx2 * xh, acc[1] + dx2, acc[2] + dr2 * y2v

        zero_d = jnp.zeros((8, D), F32)
        acc = _row_blocks(t, 8, ln_rows, (zero_d, zero_d, zero_d))
        dln2_ref[0:1, :] += _colsum(acc[0])
        dln2_ref[1:2, :] += _colsum(acc[1])
        dmod_ref[2:3, :] += _colsum(acc[2])
        _cast_rows(wide_scr, dy2_ref, t)

        halo_keep = (tt > 0).astype(F32)
        zero_l = jnp.zeros((8, LANE), F32)
        for ch in range(n_chunk):
            lo = ch * FF_CHUNK
            dz_scr[...] = _dot_nt(dy2_ref[...], wdown_v[lo:lo + FF_CHUNK, :])
            g_ext[0:HALO, :] = halo_ref[:, lo:lo + FF_CHUNK] * halo_keep
            g_ext[HALO:HALO + t, :] = hh_ref[:, DFF + lo:DFF + lo + FF_CHUNK]
            for k in range(FFN_K - 1):
                rot[k] = g_ext[pl.ds(HALO - (FFN_K - 1) + k, t), :]
            for cb in range(FF_CHUNK // LANE):
                cols = slice(cb * LANE, (cb + 1) * LANE)
                fcols = slice(lo + cb * LANE, lo + (cb + 1) * LANE)

                def gate_rows(r0, acc, cols=cols, fcols=fcols, ch=ch):
                    rows = pl.ds(r0, rb)
                    taps = [rot[0, rows, cols], rot[1, rows, cols], g_ext[pl.ds(r0 + HALO, rb), cols]]
                    gc = f3_ref[FFN_K:FFN_K + 1, fcols]
                    for k in range(FFN_K):
                        gc = gc + f3_ref[k:k + 1, fcols] * taps[k]
                    s = _sigmoid(gc)
                    dz = dz_scr[rows, cols]
                    dhh_ref[rows, fcols] = (dz * (gc * s)).astype(MXU_DTYPE)
                    dgc = dz * hh_ref[rows, fcols] * (s * (1.0 + gc * (1.0 - s)))
                    dgc_ext[ch, rows, cols] = dgc
                    return tuple(acc[k] + _fold8(dgc * taps[k]) for k in range(FFN_K)) + (acc[FFN_K] + _fold8(dgc),)

                acc = _row_blocks(t, rb, gate_rows, (zero_l,) * (FFN_K + 1))
                for k in range(FFN_K + 1):
                    df3_ref[k:k + 1, fcols] += _colsum(acc[k])
            for k in range(FFN_K - 1):
                rot[k] = dgc_ext[ch, pl.ds(k + 1, t), :]
            for cb in range(FF_CHUNK // LANE):
                cols = slice(cb * LANE, (cb + 1) * LANE)
                fcols = slice(lo + cb * LANE, lo + (cb + 1) * LANE)
                gcols = slice(DFF + lo + cb * LANE, DFF + lo + (cb + 1) * LANE)

                def dg_rows(r0, acc, cols=cols, fcols=fcols, gcols=gcols, ch=ch):
                    rows = pl.ds(r0, rb)
                    dg = (f3_ref[2:3, fcols] * dgc_ext[ch, rows, cols] + f3_ref[1:2, fcols] * rot[0, rows, cols]
                          + f3_ref[0:1, fcols] * rot[1, rows, cols])
                    dhh_ref[rows, gcols] = dg.astype(MXU_DTYPE)
                    return acc

                _row_blocks(t, rb, dg_rows, ())
            dgc_ext[ch, t:t + HALO, :] = dgc_ext[ch, 0:HALO, :]

        wide_scr[...] = _dot(dhh_ref[...], wup_v[...])

        def out_rows(r0, acc):
            rows = pl.ds(r0, 8)
            du2 = wide_scr[rows, :]
            dx1_ref[rows, :] = dx1_ref[rows, :] + du2 * (1.0 + mod_ref[4:5, :])
            return acc[0] + du2 * x1_ref[rows, :], acc[1] + du2

        acc = _row_blocks(t, 8, out_rows, (zero_d, zero_d))
        dmod_ref[1:2, :] += _colsum(acc[0])
        dmod_ref[0:1, :] += _colsum(acc[1])

    rev = lambda w: pl.BlockSpec((None, t, w), lambda b, i: (b, nt - 1 - i, 0))
    halo = pl.BlockSpec((None, HALO, DFF), lambda b, i: (b, jnp.maximum((nt - 1 - i) * (t // HALO) - 1, 0), 1))
    return pl.pallas_call(
        body, name="ffn_bwd", grid=(bl, nt),
        out_shape=[jax.ShapeDtypeStruct((bl, seq, D), F32), jax.ShapeDtypeStruct((bl, seq, D), MXU_DTYPE),
                   jax.ShapeDtypeStruct((bl, seq, 2 * DFF), MXU_DTYPE), jax.ShapeDtypeStruct((8, D), F32),
                   jax.ShapeDtypeStruct((8, DFF), F32), jax.ShapeDtypeStruct((bl, 8, D), F32)],
        in_specs=[rev(D), rev(D), rev(D), rev(2 * DFF), halo, pl.BlockSpec((None, 8, D), lambda b, i: (b, 0, 0)),
                  _ANY, _ANY, _full(f3.shape), _full(ln2.shape)],
        out_specs=[rev(D), rev(D), rev(2 * DFF), _full((8, D)), _full((8, DFF)),
                   pl.BlockSpec((None, 8, D), lambda b, i: (b, 0, 0))],
        scratch_shapes=[pltpu.VMEM(wup_t.shape, MXU_DTYPE), pltpu.VMEM(wdown.shape, MXU_DTYPE),
                        pltpu.VMEM((t + HALO, FF_CHUNK), F32), pltpu.VMEM((n_chunk, t + HALO, FF_CHUNK), F32),
                        pltpu.VMEM((FFN_K - 1, t, FF_CHUNK), F32), pltpu.VMEM((t, FF_CHUNK), F32),
                        pltpu.VMEM((t, D), F32), pltpu.SemaphoreType.DMA((2,))],
        compiler_params=_params(2),
    )(x1, y2, tgt, hh, hh, mods, wup_t, wdown, f3, ln2)


def _mixer_bwd(x, dx1, proj, h, vc, mixed, mods, win_t, wout, wri, mavg2, va, vb, ln1, ffn_wgrads):
    bl, seq, _ = x.shape
    t = min(TOK_TILE, seq)
    nt = seq // t
    pw = 2 * LW + 2 * CW
    n_g = len(ffn_wgrads)
    rb = min(ROW_BLK, t)
    half_f32 = [pltpu.VMEM((t, LW), F32)] * 10

    def body(x_ref, dx1_ref, proj_ref, phalo_ref, h_ref, hhalo_ref, vc_ref, mixed_ref, mod_ref,
             win_hbm, wout_hbm, wri_ref, mavg_ref, va_ref, vb_ref, ln1_ref, *rest):
        wgrad_refs, rest = rest[:n_g], rest[n_g:]
        gx_ref, dproj_ref, dm_ref, ga_ref, gw31_ref, dwr_ref, dwi_ref, dln1_ref, dmod_ref = rest[:9]
        slot_refs, rest = rest[9:9 + n_g], rest[9 + n_g:]
        (win_v, wout_v, xa_ext, vg_scr, dvc_ext, dxc_ext, rot_b, car, sems,
         st_a, st_b, st_c, yn_scr, dyn_scr, hprev_scr, xc_scr, a_scr, mult_scr, g_scr,
         wide, gates, hilo_a, hilo_b, dgates, xcb_scr, rot_c, rot_d, hp_ext, e_send, e_recv, e_local) = rest
        b, i = pl.program_id(0), pl.program_id(1)
        tt = nt - 1 - i
        e_start, e_finish = _exchange_steps(wgrad_refs, slot_refs, e_send, e_recv, e_local)

        @pl.when((b == 0) & (i == 0))
        def _():
            e_start()
            cps = [pltpu.make_async_copy(win_hbm, win_v, sems.at[0]), pltpu.make_async_copy(wout_hbm, wout_v, sems.at[1])]
            for cp in cps:
                cp.start()
            for cp in cps:
                cp.wait()
            for ref in (ga_ref, gw31_ref, dwr_ref, dwi_ref, dln1_ref):
                ref[...] = jnp.zeros_like(ref)

        @pl.when(i == 0)
        def _():
            dmod_ref[...] = jnp.zeros_like(dmod_ref)
            dvc_ext[t:t + HALO31, :] = jnp.zeros((HALO31, CW), F32)
            dxc_ext[t:t + HALO, :] = jnp.zeros((HALO, LW), F32)
            car[...] = jnp.zeros_like(car)

        halo_keep = (tt > 0).astype(F32)
        zero_d = jnp.zeros((8, D), F32)
        zero_l = jnp.zeros((8, LANE), F32)
        blocks = [(r0, c0) for c0 in range(0, LW, LANE) for r0 in range(0, t, rb)]

        def add_row(ref, row, c0, acc):
            ref[row:row + 1, c0:c0 + LANE] += _colsum(acc)

        def ln_rows(r0, acc):
            rows = pl.ds(r0, 8)
            xv, mixed = x_ref[rows, :], mixed_ref[rows, :]
            gt = mod_ref[2:3, :]
            xh, rstd = _layer_norm_stats(ALPHA * xv + (1.0 + gt) * mixed)
            dx1 = dx1_ref[rows, :]
            dxh = dx1 * ln1_ref[0:1, :]
            dr1 = rstd * (dxh - _rowmean(dxh) - xh * _rowmean(dxh * xh))
            wide[rows, :] = (1.0 + gt) * dr1
            gx_ref[rows, :] = ALPHA * dr1
            return acc[0] + dx1 * xh, acc[1] + dx1, acc[2] + dr1 * mixed

        acc = _row_blocks(t, 8, ln_rows, (zero_d, zero_d, zero_d))
        dln1_ref[0:1, :] += _colsum(acc[0])
        dln1_ref[1:2, :] += _colsum(acc[1])
        dmod_ref[2:3, :] += _colsum(acc[2])
        _cast_rows(wide, dm_ref, t)
        wide[...] = _dot_nt(dm_ref[...], wout_v[...])

        mavg2 = mavg_ref[...]
        for r0, c0 in blocks:
            rows, cols = pl.ds(r0, rb), slice(c0, c0 + LANE)
            _store_hilo(hilo_a, rows, c0, vc_ref[rows, cols])
        st_a[...] = _dot(hilo_a[...], mavg2)
        for r0, c0 in blocks:
            rows, cols = pl.ds(r0, rb), slice(c0, c0 + LANE)
            dl = vc_ref[rows, cols] - st_a[rows, cols]
            yn_scr[rows, cols] = dl
            _store_hilo(hilo_a, rows, c0, dl * dl)
        st_b[...] = _dot(hilo_a[...], mavg2)
        for c0 in range(0, CW, LANE):
            cols, pc = slice(c0, c0 + LANE), 2 * LW + c0
            ng, nb = vb_ref[CONV_K + 1:CONV_K + 2, cols], vb_ref[CONV_K + 2:CONV_K + 3, cols]
            acc_g, acc_b = zero_l, zero_l
            for r0 in range(0, t, rb):
                rows = pl.ds(r0, rb)
                rs = lax.rsqrt(st_b[rows, cols] + LN_EPS)
                yn = yn_scr[rows, cols] * rs
                yl = yn * ng + nb
                s = _sigmoid(yl)
                dyl = wide[rows, LW + c0:LW + c0 + LANE] * (s * (1.0 + yl * (1.0 - s)))
                acc_g, acc_b = acc_g + _fold8(dyl * yn), acc_b + _fold8(dyl)
                dyn = dyl * ng
                st_b[rows, cols] = rs
                yn_scr[rows, cols] = yn
                dyn_scr[rows, cols] = dyn
                _store_hilo(hilo_a, rows, c0, dyn)
                _store_hilo(hilo_b, rows, c0, dyn * yn)
                vg_scr[rows, cols] = proj_ref[rows, pc:pc + LANE] * _sigmoid(proj_ref[rows, pc + CW:pc + CW + LANE])
            add_row(ga_ref, 8, c0, acc_g)
            add_row(ga_ref, 9, c0, acc_b)
        st_a[...] = _dot(hilo_a[...], mavg2)
        st_c[...] = _dot(hilo_b[...], mavg2)
        for c0 in range(0, CW, LANE):
            cols = slice(c0, c0 + LANE)
            acc_b = zero_l
            for r0 in range(0, t, rb):
                rows = pl.ds(r0, rb)
                dvc = st_b[rows, cols] * (dyn_scr[rows, cols] - st_a[rows, cols] - yn_scr[rows, cols] * st_c[rows, cols])
                dvc_ext[rows, cols] = dvc
                acc_b = acc_b + _fold8(dvc)
            add_row(ga_ref, 10, c0, acc_b)
        dvc_win = _shifted_rows(dvc_ext, rot_b, rb)
        for c0 in range(0, CW, LANE):
            cols, pc = slice(c0, c0 + LANE), 2 * LW + c0
            for k in range(CONV_K):
                acc_w = zero_l
                for r0 in range(0, t, rb):
                    acc_w = acc_w + _fold8(vg_scr[pl.ds(r0, rb), cols] * dvc_win(CONV_K - 1 - k, r0, cols))
                add_row(gw31_ref, k, c0, acc_w)
            for r0 in range(0, t, rb):
                rows = pl.ds(r0, rb)
                dvg = jnp.zeros((rb, LANE), F32)
                for k in range(CONV_K):
                    dvg = dvg + vb_ref[k:k + 1, cols] * dvc_win(CONV_K - 1 - k, r0, cols)
                vbr = proj_ref[rows, pc:pc + LANE]
                sgb = _sigmoid(proj_ref[rows, pc + CW:pc + CW + LANE])
                dproj_ref[rows, pc:pc + LANE] = (dvg * sgb).astype(MXU_DTYPE)
                dproj_ref[rows, pc + CW:pc + CW + LANE] = (dvg * vbr * sgb * (1.0 - sgb)).astype(MXU_DTYPE)
        dvc_ext[t:t + HALO31, :] = dvc_ext[0:HALO31, :]

        xa_ext[0:HALO, :] = phalo_ref[HALO31 - HALO:HALO31, 0:LW] * halo_keep
        xa_ext[HALO:HALO + t, :] = proj_ref[:, 0:LW]
        for k in range(LRU_K - 1):
            rot_c[k] = xa_ext[pl.ds(HALO - (LRU_K - 1) + k, t), :]
        hp_ext[0:HALO, :] = hhalo_ref[...] * halo_keep
        hp_ext[HALO:HALO + t, :] = h_ref[...]
        hprev_scr[...] = hp_ext[pl.ds(HALO - 1, t), :]

        def xa_tap(k, rows, r0, cols):
            return xa_ext[pl.ds(HALO + r0, rb), cols] if k == LRU_K - 1 else rot_c[k, rows, cols]

        for r0, c0 in blocks:
            rows, cols = pl.ds(r0, rb), slice(c0, c0 + LANE)
            xc = va_ref[4:5, cols]
            for k in range(LRU_K):
                xc = xc + va_ref[k:k + 1, cols] * xa_tap(k, rows, r0, cols)
            xc_scr[rows, cols] = xc
            xcb_scr[rows, cols] = xc.astype(MXU_DTYPE)
        gates[...] = _dot(xcb_scr[...], wri_ref[...])
        for r0, c0 in blocks:
            rows, cols = pl.ds(r0, rb), slice(c0, c0 + LANE)
            r = _sigmoid(gates[rows, cols] + va_ref[5:6, cols])
            ig = _sigmoid(gates[rows, LW + c0:LW + c0 + LANE] + va_ref[6:7, cols])
            la = (-LRU_C) * r * _softplus(-va_ref[7:8, cols])
            a = jnp.exp(la)
            gates[rows, cols] = r
            gates[rows, LW + c0:LW + c0 + LANE] = ig
            a_scr[rows, cols] = a
            mult_scr[rows, cols] = jnp.sqrt(-jnp.tanh(la) * (a * a + 1.0))
            ge, dge = _gelu_and_grad(proj_ref[rows, LW + c0:LW + c0 + LANE])
            dya = wide[rows, cols]
            dproj_ref[rows, LW + c0:LW + c0 + LANE] = (dya * h_ref[rows, cols] * dge).astype(MXU_DTYPE)
            g_scr[rows, cols] = dya * ge
        _scan_bwd_blocks(a_scr, g_scr, car, t)
        for c0 in range(0, LW, LANE):
            cols = slice(c0, c0 + LANE)
            sp = _softplus(-va_ref[7:8, cols])
            acc_l, acc_r, acc_i = zero_l, zero_l, zero_l
            for r0 in range(0, t, rb):
                rows = pl.ds(r0, rb)
                gv, xc, a, mult = g_scr[rows, cols], xc_scr[rows, cols], a_scr[rows, cols], mult_scr[rows, cols]
                r, ig = gates[rows, cols], gates[rows, LW + c0:LW + c0 + LANE]
                dla = gv * hprev_scr[rows, cols] * a - gv * (ig * xc) * (a * a / mult)
                acc_l = acc_l + _fold8(dla * ((-LRU_C) * r))
                dgr = dla * ((-LRU_C) * sp) * r * (1.0 - r)
                dgi = gv * (mult * xc) * ig * (1.0 - ig)
                acc_r, acc_i = acc_r + _fold8(dgr), acc_i + _fold8(dgi)
                dgates[rows, cols] = dgr.astype(MXU_DTYPE)
                dgates[rows, LW + c0:LW + c0 + LANE] = dgi.astype(MXU_DTYPE)
                dxc_ext[rows, cols] = gv * (mult * ig)
            add_row(ga_ref, 5, c0, acc_r)
            add_row(ga_ref, 6, c0, acc_i)
            ga_ref[7:8, cols] += _colsum(acc_l) * (-_sigmoid(-va_ref[7:8, cols]))
        dwr_ref[...] += _dot_tn(xcb_scr[...], dgates[:, 0:LW])
        dwi_ref[...] += _dot_tn(xcb_scr[...], dgates[:, LW:])
        st_a[...] = _dot_nt(dgates[...], wri_ref[...])
        for c0 in range(0, LW, LANE):
            cols = slice(c0, c0 + LANE)
            acc_b = zero_l
            for r0 in range(0, t, rb):
                rows = pl.ds(r0, rb)
                dxc = dxc_ext[rows, cols] + st_a[rows, cols]
                dxc_ext[rows, cols] = dxc
                acc_b = acc_b + _fold8(dxc)
            add_row(ga_ref, 4, c0, acc_b)
        for k in range(LRU_K - 1):
            rot_d[k] = dxc_ext[pl.ds(k + 1, t), :]
        for c0 in range(0, LW, LANE):
            cols = slice(c0, c0 + LANE)
            acc_w = [zero_l] * LRU_K
            for r0 in range(0, t, rb):
                rows = pl.ds(r0, rb)
                dxc = dxc_ext[rows, cols]
                dxa = va_ref[LRU_K - 1:LRU_K, cols] * dxc
                for k in range(LRU_K):
                    acc_w[k] = acc_w[k] + _fold8(dxc * xa_tap(k, rows, r0, cols))
                    if k < LRU_K - 1:
                        dxa = dxa + va_ref[k:k + 1, cols] * rot_d[LRU_K - 2 - k, rows, cols]
                dproj_ref[rows, cols] = dxa.astype(MXU_DTYPE)
            for k in range(LRU_K):
                add_row(ga_ref, k, c0, acc_w[k])
        dxc_ext[t:t + HALO, :] = dxc_ext[0:HALO, :]

        wide[...] = _dot(dproj_ref[...], win_v[...])

        def out_rows(r0, acc):
            rows = pl.ds(r0, 8)
            du1 = wide[rows, :]
            gx_ref[rows, :] = gx_ref[rows, :] + du1 * (1.0 + mod_ref[1:2, :])
            return acc[0] + du1 * x_ref[rows, :], acc[1] + du1

        acc = _row_blocks(t, 8, out_rows, (zero_d, zero_d))
        dmod_ref[1:2, :] += _colsum(acc[0])
        dmod_ref[0:1, :] += _colsum(acc[1])

        pl.when((b == bl - 1) & (i == nt - 1))(e_finish)

    rev = lambda w: pl.BlockSpec((None, t, w), lambda b, i: (b, nt - 1 - i, 0))

    def halo(rows_, w):
        return pl.BlockSpec((None, rows_, w), lambda b, i: (b, jnp.maximum((nt - 1 - i) * (t // rows_) - 1, 0), 0))

    return pl.pallas_call(
        body, name="mixer_bwd", grid=(bl, nt),
        out_shape=[jax.ShapeDtypeStruct((bl, seq, D), F32), jax.ShapeDtypeStruct((bl, seq, pw), MXU_DTYPE),
                   jax.ShapeDtypeStruct((bl, seq, D), MXU_DTYPE), jax.ShapeDtypeStruct((16, LW), F32),
                   jax.ShapeDtypeStruct((32, CW), F32), jax.ShapeDtypeStruct((LW, LW), F32),
                   jax.ShapeDtypeStruct((LW, LW), F32), jax.ShapeDtypeStruct((8, D), F32),
                   jax.ShapeDtypeStruct((bl, 8, D), F32)] + _exchange_out_shapes(ffn_wgrads),
        in_specs=[rev(D), rev(D), rev(pw), halo(HALO31, pw), rev(LW), halo(HALO, LW), rev(CW), rev(D),
                  pl.BlockSpec((None, 8, D), lambda b, i: (b, 0, 0)), _ANY, _ANY,
                  _full(wri.shape), _full(mavg2.shape), _full(va.shape), _full(vb.shape), _full(ln1.shape)]
        + [_ANY] * n_g,
        out_specs=[rev(D), rev(pw), rev(D), _full((16, LW)), _full((32, CW)), _full((LW, LW)), _full((LW, LW)),
                   _full((8, D)), pl.BlockSpec((None, 8, D), lambda b, i: (b, 0, 0))] + [_ANY] * n_g,
        scratch_shapes=[pltpu.VMEM(win_t.shape, MXU_DTYPE), pltpu.VMEM(wout.shape, MXU_DTYPE),
                        pltpu.VMEM((t + HALO, LW), F32), pltpu.VMEM((t, CW), F32),
                        pltpu.VMEM((t + HALO31, CW), F32), pltpu.VMEM((t + HALO, LW), F32),
                        pltpu.VMEM((7, t + HALO31 - 8, CW), F32),
                        pltpu.VMEM((8, LW), F32), pltpu.SemaphoreType.DMA((2,))] + half_f32
        + [pltpu.VMEM((t, D), F32), pltpu.VMEM((t, 2 * LW), F32), pltpu.VMEM((t, 2 * CW), MXU_DTYPE),
           pltpu.VMEM((t, 2 * CW), MXU_DTYPE), pltpu.VMEM((t, 2 * LW), MXU_DTYPE), pltpu.VMEM((t, LW), MXU_DTYPE),
           pltpu.VMEM((LRU_K - 1, t, LW), F32), pltpu.VMEM((LRU_K - 1, t, LW), F32), pltpu.VMEM((t + HALO, LW), F32)]
        + _comm_sems(n_g),
        compiler_params=_params(2),
    )(x, dx1, proj, proj, h, h, vc, mixed, mods, win_t, wout, wri, mavg2, va, vb, ln1, *ffn_wgrads)


def _weight_grad(a, bmat, tm, name, out_dtype=F32, exchange=(), exchange_half=None, gather=(), col_phase=None,
                 tok_tile=None):
    ntok, m = a.shape
    n = bmat.shape[1]
    tk = min(tok_tile or WG_TOK_TILE, ntok)
    nk = ntok // tk
    stride, col0 = (1, 0) if col_phase is None else (2, col_phase)
    m = m // stride
    ni = m // tm
    n_e, n_g = len(exchange), len(gather)
    n_c = n_e + n_g

    def body(a_ref, b_ref, *rest):
        e_in, g_in, o_ref = rest[:n_e], rest[n_e:n_c], rest[n_c]
        e_out, g_out = rest[n_c + 1:n_c + 1 + n_e], rest[n_c + 1 + n_e:2 * n_c + 1]
        acc_ref, sems = rest[2 * n_c + 1], rest[2 * n_c + 2:]
        i, k = pl.program_id(0), pl.program_id(1)
        step = i * nk + k
        if n_e:
            e_start, e_finish = _exchange_steps(e_in, e_out, *sems[:3], half=exchange_half)
            pl.when(step == 0)(e_start)
        if n_g:
            g_start, g_forward, g_finish = _gather_steps(g_in, g_out, *sems[3 if n_e else 0:])
            pl.when(step == 0)(g_start)
            pl.when(step == (ni * nk) // 2)(g_forward)

        @pl.when(k == 0)
        def _():
            acc_ref[...] = jnp.zeros_like(acc_ref)

        acc_ref[...] += _dot_tn(a_ref[...], b_ref[...])

        @pl.when(k == nk - 1)
        def _():
            o_ref[...] = acc_ref[...].astype(out_dtype)

        if n_g:
            pl.when(step == ni * nk - 1)(g_finish)
        if n_e:
            pl.when(step == ni * nk - 1)(e_finish)

    res = pl.pallas_call(
        body, name=name, grid=(ni, nk),
        out_shape=[jax.ShapeDtypeStruct((m, n), out_dtype)] + _exchange_out_shapes(exchange, exchange_half)
        + _gather_out_shapes(gather),
        in_specs=[pl.BlockSpec((tk, tm), lambda i, k: (k, i * stride + col0)), pl.BlockSpec((tk, n), lambda i, k: (k, 0))]
        + [_ANY] * n_c,
        out_specs=[pl.BlockSpec((tm, n), lambda i, k: (i, 0))] + [_ANY] * n_c,
        scratch_shapes=[pltpu.VMEM((tm, n), F32)] + (_comm_sems(n_e) if n_e else []) + (_comm_sems(n_g) if n_g else []),
        compiler_params=_params(2),
    )(a, bmat, *exchange, *gather)
    return res if n_c else res[0]


def _small_grad_layout(bl):
    r1 = 6 * bl
    return {"dmod": 0, "wide": r1, "ln": r1 + 32, "df3": r1 + 36, "g_wr": r1 + 48, "g_wi": r1 + 80, "rows": r1 + 112}


def _pack_small_grads(dmod_a, dmod_b, g_a, g_w31, dln1, dln2, df3, g_wr, g_wi, loss_part):
    bl = dmod_a.shape[0] // 8
    lay = _small_grad_layout(bl)
    ins = [dmod_a, dmod_b, g_a, g_w31, dln1, dln2, df3, g_wr, g_wi, loss_part]

    def body(ma_ref, mb_ref, ga_ref, gw_ref, l1_ref, l2_ref, f3_ref, wr_ref, wi_ref, ls_ref, o_ref):
        o_ref[...] = jnp.zeros_like(o_ref)
        for b in range(bl):
            o_ref[6 * b:6 * b + 3, :] = ma_ref[8 * b:8 * b + 3, :]
            o_ref[6 * b + 3:6 * b + 6, :] = mb_ref[8 * b:8 * b + 3, :]
        wide = lay["wide"]
        o_ref[wide:wide + 32, 0:CW] = gw_ref[...]
        o_ref[wide:wide + 16, CW:CW + LW] = ga_ref[...]
        o_ref[wide + 16:wide + 24, CW:CW + LANE] = ls_ref[...]
        o_ref[lay["ln"]:lay["ln"] + 2, :] = l1_ref[0:2, :]
        o_ref[lay["ln"] + 2:lay["ln"] + 4, :] = l2_ref[0:2, :]
        for j, c0 in enumerate(range(0, DFF, D)):
            w = min(D, DFF - c0)
            o_ref[lay["df3"] + 4 * j:lay["df3"] + 4 * j + 4, 0:w] = f3_ref[0:4, c0:c0 + w]
        o_ref[lay["g_wr"]:lay["g_wr"] + 32, :] = wr_ref[...]
        o_ref[lay["g_wi"]:lay["g_wi"] + 32, :] = wi_ref[...]

    return pl.pallas_call(
        body, name="pack_small_grads", out_shape=jax.ShapeDtypeStruct((lay["rows"], D), F32),
        in_specs=[_full(a.shape) for a in ins], out_specs=_full((lay["rows"], D)),
        compiler_params=pltpu.CompilerParams(vmem_limit_bytes=VMEM_LIMIT),
    )(*ins)


def _sum_small_grads(packed_all, bl):
    lay = _small_grad_layout(bl)
    shapes = {"lru_conv_w": (LRU_K, LW), "lru_conv_b": (1, LW), "lru_b_r": (1, LW), "lru_b_i": (1, LW),
              "lru_lambda": (1, LW), "conv_norm_g": (1, CW), "conv_norm_b": (1, CW), "conv_b": (1, CW),
              "conv_w": (CONV_K, CW), "ln1_g": (1, D), "ln1_b": (1, D), "ln2_g": (1, D), "ln2_b": (1, D),
              "ffn_conv_w": (FFN_K, DFF), "ffn_conv_b": (1, DFF), "lru_w_r": (32, D), "lru_w_i": (32, D),
              "loss": (8, LANE)}
    names = list(shapes)

    def body(p_ref, *outs):
        o = dict(zip(names, outs))

        def part(r0, nr, c0=0, nc=D):
            acc = p_ref[0, r0:r0 + nr, c0:c0 + nc]
            for j in range(1, NDEV):
                acc = acc + p_ref[j, r0:r0 + nr, c0:c0 + nc]
            return acc

        wide = lay["wide"]
        ga = part(wide, 16, CW, LW)
        o["lru_conv_w"][...] = ga[0:LRU_K]
        for row, k in ((4, "lru_conv_b"), (5, "lru_b_r"), (6, "lru_b_i"), (7, "lru_lambda"), (8, "conv_norm_g"),
                       (9, "conv_norm_b"), (10, "conv_b")):
            o[k][...] = ga[row:row + 1]
        o["conv_w"][...] = part(wide, 32, 0, CW)[0:CONV_K]
        o["loss"][...] = part(wide + 16, 8, CW, LANE)
        ln = part(lay["ln"], 4)
        o["ln1_g"][...], o["ln1_b"][...], o["ln2_g"][...], o["ln2_b"][...] = ln[0:1], ln[1:2], ln[2:3], ln[3:4]
        for j, c0 in enumerate(range(0, DFF, D)):
            w = min(D, DFF - c0)
            f3 = part(lay["df3"] + 4 * j, 4, 0, w)
            o["ffn_conv_w"][:, c0:c0 + w] = f3[0:FFN_K]
            o["ffn_conv_b"][:, c0:c0 + w] = f3[FFN_K:FFN_K + 1]
        o["lru_w_r"][...] = part(lay["g_wr"], 32)
        o["lru_w_i"][...] = part(lay["g_wi"], 32)

    res = pl.pallas_call(
        body, name="sum_small_grads", out_shape=[jax.ShapeDtypeStruct(shapes[k], F32) for k in names],
        in_specs=[_full(packed_all.shape)], out_specs=[_full(shapes[k]) for k in names],
        compiler_params=pltpu.CompilerParams(vmem_limit_bytes=VMEM_LIMIT),
    )(packed_all)
    return dict(zip(names, res))


def _adamw_update(w_ref, g_ref, m_ref, v_ref, d_ref, nm_ref, nv_ref):
    gv = g_ref[...]
    nm = ADAM_B1 * m_ref[...] + (1.0 - ADAM_B1) * gv
    nv = ADAM_B2 * v_ref[...] + (1.0 - ADAM_B2) * (gv * gv)
    m_hat = nm / (1.0 - ADAM_B1 ** ADAM_STEP)
    v_hat = nv / (1.0 - ADAM_B2 ** ADAM_STEP)
    d_ref[...] = -ADAM_LR * (m_hat / (jnp.sqrt(v_hat) + ADAM_EPS) + ADAM_WD * w_ref[...])
    nm_ref[...] = nm
    nv_ref[...] = nv


def _sum_adamw(slots, w, m, v, name):
    _, r, cdim = slots.shape
    tr = next((cand for cand in (64, 32) if r % cand == 0), r)

    def body(s_ref, w_ref, m_ref, v_ref, g_ref, d_ref, nm_ref, nv_ref):
        acc = s_ref[0].astype(F32)
        for j in range(1, NDEV):
            acc = acc + s_ref[j].astype(F32)
        g_ref[...] = acc
        _adamw_update(w_ref, g_ref, m_ref, v_ref, d_ref, nm_ref, nv_ref)

    spec = pl.BlockSpec((tr, cdim), lambda i: (i, 0))
    return pl.pallas_call(
        body, name=name, grid=(r // tr,), out_shape=[jax.ShapeDtypeStruct((r, cdim), F32)] * 4,
        in_specs=[pl.BlockSpec((NDEV, tr, cdim), lambda i: (0, i, 0))] + [spec] * 3, out_specs=[spec] * 4,
        compiler_params=_params(1),
    )(slots, w, m, v)


def _sum_adamw_transposed(slots_c0, slots_c1, w, m, v, name):
    _, r, cdim = slots_c0.shape
    tc = 256

    def body(s0_ref, s1_ref, w_ref, m_ref, v_ref, g_ref, d_ref, nm_ref, nv_ref):
        def total(s_ref):
            acc = s_ref[0].astype(F32)
            for j in range(1, NDEV):
                acc = acc + s_ref[j].astype(F32)
            g_ref[...] = acc.T

        c = lax.axis_index("c")
        pl.when(c == 0)(functools.partial(total, s0_ref))
        pl.when(c == 1)(functools.partial(total, s1_ref))
        _adamw_update(w_ref, g_ref, m_ref, v_ref, d_ref, nm_ref, nv_ref)

    slot_spec = pl.BlockSpec((NDEV, r, tc), lambda j: (0, 0, j))
    spec = pl.BlockSpec((tc, r), lambda j: (j, 0))
    return pl.pallas_call(
        body, name=name, grid=(cdim // tc,), out_shape=[jax.ShapeDtypeStruct((cdim, r), F32)] * 4,
        in_specs=[slot_spec, slot_spec] + [spec] * 3, out_specs=[spec] * 4,
        compiler_params=_params(1),
    )(slots_c0, slots_c1, w, m, v)


def _adamw_many(ws, gs, ms, vs, name):
    n = len(ws)

    def body(*refs):
        ins, outs = refs[:4 * n], refs[4 * n:]
        for k in range(n):
            _adamw_update(ins[k], ins[n + k], ins[2 * n + k], ins[3 * n + k], outs[k], outs[n + k], outs[2 * n + k])

    specs = [_full(w.shape) for w in ws]
    res = pl.pallas_call(
        body, name=name, out_shape=[jax.ShapeDtypeStruct(w.shape, F32) for w in ws] * 3,
        in_specs=specs * 4, out_specs=specs * 3,
        compiler_params=pltpu.CompilerParams(vmem_limit_bytes=VMEM_LIMIT),
    )(*ws, *gs, *ms, *vs)
    return res[:n], res[n:2 * n], res[2 * n:]


def _adamw(w, g, m, v, name):
    r, cdim = w.shape
    tr = next((cand for cand in (256, 128) if r % cand == 0), r)
    body = functools.partial(_adamw_update)
    spec = pl.BlockSpec((tr, cdim), lambda i: (i, 0))
    return pl.pallas_call(
        body, name=name, grid=(r // tr,), out_shape=[jax.ShapeDtypeStruct((r, cdim), F32)] * 3,
        in_specs=[spec] * 4, out_specs=[spec] * 3, compiler_params=_params(1),
    )(w, g, m, v)


def _pack(arrs, width=D, row_mult=8):
    parts = []
    for a in arrs:
        flat = a.reshape(-1)
        pad = (-flat.shape[0]) % width
        parts.append(jnp.pad(flat, (0, pad)))
    flat = jnp.concatenate(parts)
    pad = (-flat.shape[0]) % (width * row_mult)
    return jnp.pad(flat, (0, pad)).reshape(-1, width)


def _unpack_gathered(buf, shapes, width=D):
    out, row = [], 0
    for shp in shapes:
        size = math.prod(shp)
        nrow = -(-size // width)
        out.append(buf[:, row:row + nrow].reshape(NDEV, -1)[:, :size].reshape((NDEV,) + tuple(shp)))
        row += nrow
    return out


def _block_diag(w):
    hn, dh, _ = w.shape
    eye = jnp.eye(hn, dtype=w.dtype)
    return (w[:, :, None, :] * eye[:, None, :, None]).reshape(hn * dh, hn * dh)


def _diag_blocks(wfull, hn):
    dh = wfull.shape[0] // hn
    keep = np.eye(hn, dtype=bool)[:, None, :, None]
    return jnp.where(keep, wfull.reshape(hn, dh, hn, dh), 0.0).sum(axis=2)


def _pad_rows(a, rows):
    return jnp.pad(a, ((0, rows - a.shape[0]), (0, 0)))


def kernel(x, c, w_ada, b_ada, w_in, lru_conv_w, lru_conv_b, lru_w_r, lru_b_r, lru_w_i, lru_b_i, lru_lambda, conv_w, conv_b, conv_norm_g, conv_norm_b, w_out, ln1_g, ln1_b, ffn_w_up, ffn_conv_w, ffn_conv_b, ffn_w_down, ln2_g, ln2_b, loss_target, m_w_ada, m_b_ada, m_w_in, m_lru_conv_w, m_lru_conv_b, m_lru_w_r, m_lru_b_r, m_lru_w_i, m_lru_b_i, m_lru_lambda, m_conv_w, m_conv_b, m_conv_norm_g, m_conv_norm_b, m_w_out, m_ln1_g, m_ln1_b, m_ffn_w_up, m_ffn_conv_w, m_ffn_conv_b, m_ffn_w_down, m_ln2_g, m_ln2_b, v_w_ada, v_b_ada, v_w_in, v_lru_conv_w, v_lru_conv_b, v_lru_w_r, v_lru_b_r, v_lru_w_i, v_lru_b_i, v_lru_lambda, v_conv_w, v_conv_b, v_conv_norm_g, v_conv_norm_b, v_w_out, v_ln1_g, v_ln1_b, v_ffn_w_up, v_ffn_conv_w, v_ffn_conv_b, v_ffn_w_down, v_ln2_g, v_ln2_b):
    weights = dict(w_ada=w_ada, b_ada=b_ada, w_in=w_in, lru_conv_w=lru_conv_w, lru_conv_b=lru_conv_b, lru_w_r=lru_w_r,
                   lru_b_r=lru_b_r, lru_w_i=lru_w_i, lru_b_i=lru_b_i, lru_lambda=lru_lambda, conv_w=conv_w, conv_b=conv_b,
                   conv_norm_g=conv_norm_g, conv_norm_b=conv_norm_b, w_out=w_out, ln1_g=ln1_g, ln1_b=ln1_b,
                   ffn_w_up=ffn_w_up, ffn_conv_w=ffn_conv_w, ffn_conv_b=ffn_conv_b, ffn_w_down=ffn_w_down, ln2_g=ln2_g,
                   ln2_b=ln2_b)
    mom_m = dict(w_ada=m_w_ada, b_ada=m_b_ada, w_in=m_w_in, lru_conv_w=m_lru_conv_w, lru_conv_b=m_lru_conv_b,
                 lru_w_r=m_lru_w_r, lru_b_r=m_lru_b_r, lru_w_i=m_lru_w_i, lru_b_i=m_lru_b_i, lru_lambda=m_lru_lambda,
                 conv_w=m_conv_w, conv_b=m_conv_b, conv_norm_g=m_conv_norm_g, conv_norm_b=m_conv_norm_b, w_out=m_w_out,
                 ln1_g=m_ln1_g, ln1_b=m_ln1_b, ffn_w_up=m_ffn_w_up, ffn_conv_w=m_ffn_conv_w, ffn_conv_b=m_ffn_conv_b,
                 ffn_w_down=m_ffn_w_down, ln2_g=m_ln2_g, ln2_b=m_ln2_b)
    mom_v = dict(w_ada=v_w_ada, b_ada=v_b_ada, w_in=v_w_in, lru_conv_w=v_lru_conv_w, lru_conv_b=v_lru_conv_b,
                 lru_w_r=v_lru_w_r, lru_b_r=v_lru_b_r, lru_w_i=v_lru_w_i, lru_b_i=v_lru_b_i, lru_lambda=v_lru_lambda,
                 conv_w=v_conv_w, conv_b=v_conv_b, conv_norm_g=v_conv_norm_g, conv_norm_b=v_conv_norm_b, w_out=v_w_out,
                 ln1_g=v_ln1_g, ln1_b=v_ln1_b, ffn_w_up=v_ffn_w_up, ffn_conv_w=v_ffn_conv_w, ffn_conv_b=v_ffn_conv_b,
                 ffn_w_down=v_ffn_w_down, ln2_g=v_ln2_g, ln2_b=v_ln2_b)
    names = list(weights)
    bl, seq, _ = x.shape
    ntok = bl * seq
    me = 4 * lax.axis_index("x") + 2 * lax.axis_index("y") + lax.axis_index("c")

    small_shapes = [(bl, D), (LRU_K, LW // NDEV), (CONV_K, CW // NDEV), (FFN_K, DFF // NDEV)]
    small = _pack([c, lru_conv_w[0], conv_w[0], ffn_conv_w[0]], width=128)
    n_small = small.shape[0]
    small_all, win_t, wout_b = _all_gather(
        [small, w_in[0].T.astype(MXU_DTYPE), w_out[0].astype(MXU_DTYPE)], "gather_small_and_mixer_weights")
    small_all = small_all.reshape(NDEV, n_small, 128)
    c_parts, k4_parts, k31_parts, k3_parts = _unpack_gathered(small_all, small_shapes, width=128)
    c_all = c_parts.reshape(NDEV * bl, D)
    lru_conv_w_f = k4_parts.transpose(1, 0, 2).reshape(LRU_K, LW)
    conv_w_f = k31_parts.transpose(1, 0, 2).reshape(CONV_K, CW)
    ffn_conv_w_f = k3_parts.transpose(1, 0, 2).reshape(FFN_K, DFF)

    ncol = w_ada.shape[2]
    b_ada_loc = lax.dynamic_slice(b_ada, (0, me * ncol), (1, ncol))
    mod_cols = _ada_fwd(c_all, w_ada[0], b_ada_loc)
    (mod_all,) = _all_gather([mod_cols], "gather_mod")
    mod_all = mod_all.reshape(NDEV, NDEV * bl, ncol)
    mod_mine = lax.dynamic_slice(mod_all, (0, me * bl, 0), (NDEV, bl, ncol))
    mods = mod_mine.transpose(1, 0, 2).reshape(bl, 6, D)
    mods = jnp.pad(mods, ((0, 0), (0, 2), (0, 0)))

    ffn_shards = [ffn_w_up[0].T.astype(MXU_DTYPE), ffn_w_down[0].astype(MXU_DTYPE)]

    wr_bd = _block_diag(lru_w_r[0]).astype(MXU_DTYPE)
    wi_bd = _block_diag(lru_w_i[0]).astype(MXU_DTYPE)
    mavg_np = np.kron(np.eye(CW // HEAD, dtype=np.float32), np.full((HEAD, HEAD), 1.0 / HEAD, np.float32))
    mavg = jnp.asarray(mavg_np, MXU_DTYPE)
    mavg2 = jnp.asarray(np.concatenate([mavg_np, mavg_np], axis=0), MXU_DTYPE)
    wri = jnp.concatenate([wr_bd, wi_bd], axis=1)
    va = jnp.concatenate([lru_conv_w_f, lru_conv_b, lru_b_r, lru_b_i, lru_lambda], axis=0)
    vb = _pad_rows(jnp.concatenate([conv_w_f, conv_b, conv_norm_g, conv_norm_b], axis=0), 40)
    ln1 = _pad_rows(jnp.concatenate([ln1_g, ln1_b], axis=0), 8)
    ln2 = _pad_rows(jnp.concatenate([ln2_g, ln2_b], axis=0), 8)
    f3 = _pad_rows(jnp.concatenate([ffn_conv_w_f, ffn_conv_b], axis=0), 8)

    proj, h, vc, y_b, mixed, x1, u1_b, wup_t, wdown_b = _mixer_fwd(
        x, mods, win_t, wout_b, wri, mavg2, va, vb, ln1, ffn_shards)
    hh, z_b, u2_b, y2, loss_part = _ffn_fwd(x1, mods, loss_target, wup_t, wdown_b, f3, ln2)

    dx1, dy2_b, dhh_b, dln2, df3, dmod_b = _ffn_bwd(x1, y2, loss_target, hh, mods, wup_t, wdown_b, f3, ln2)
    g_down = _weight_grad(z_b.reshape(ntok, DFF), dy2_b.reshape(ntok, D), FF_CHUNK, "wgrad_down", out_dtype=WIRE_DTYPE)
    g_up_t = _weight_grad(dhh_b.reshape(ntok, 2 * DFF), u2_b.reshape(ntok, D), FF_CHUNK, "wgrad_up",
                          out_dtype=WIRE_DTYPE)
    grad_x, dproj_b, dm_b, g_a, g_w31, g_wr, g_wi, dln1, dmod_a, s_up, s_down = _mixer_bwd(
        x, dx1, proj, h, vc, mixed, mods, win_t, wout_b, wri, mavg2, va, vb, ln1, [g_up_t, g_down])
    dproj_2d, u1_2d = dproj_b.reshape(ntok, 2 * LW + 2 * CW), u1_b.reshape(ntok, D)
    blk_in = (2 * LW + 2 * CW) // NDEV
    heads = LW // HEAD
    packed = _pack_small_grads(
        dmod_a.reshape(bl * 8, D), dmod_b.reshape(bl * 8, D), g_a, g_w31, dln1, dln2, df3,
        _diag_blocks(g_wr, heads).reshape(32, D), _diag_blocks(g_wi, heads).reshape(32, D), loss_part)
    n_rows = packed.shape[0]
    g_out_b, packed_all = _weight_grad(y_b.reshape(ntok, D), dm_b.reshape(ntok, D), D, "wgrad_out",
                                       out_dtype=WIRE_DTYPE, gather=[packed])
    g_in_c0, s_out = _weight_grad(dproj_2d, u1_2d, blk_in, "wgrad_in_c0", out_dtype=WIRE_DTYPE, exchange=[g_out_b],
                                  col_phase=0, tok_tile=2 * WG_TOK_TILE)
    g_in_c1, s_in_c0 = _weight_grad(dproj_2d, u1_2d, blk_in, "wgrad_in_c1", out_dtype=WIRE_DTYPE, exchange=[g_in_c0],
                                    exchange_half=0, col_phase=1, tok_tile=2 * WG_TOK_TILE)
    (s_in_c1,) = _exchange([g_in_c1], "exchange_w_in_grads", half=1)
    packed_all = packed_all.reshape(NDEV, n_rows, D)
    full = _sum_small_grads(packed_all, bl)
    loss = jnp.sum(full.pop("loss")) * (0.5 / D)
    dmod_all = packed_all[:, 0:6 * bl].reshape(NDEV * bl, 6 * D)

    grads, delta, new_m, new_v = {}, {}, {}, {}
    for k, slots in (("w_out", s_out), ("ffn_w_down", s_down)):
        g_, d_, m_, v_ = _sum_adamw(slots, weights[k][0], mom_m[k][0], mom_v[k][0], "sum_adamw_" + k)
        grads[k], delta[k], new_m[k], new_v[k] = g_[None], d_[None], m_[None], v_[None]
    g_w_ada, g_b_ada = _ada_bwd(c_all, dmod_all, lax.dynamic_slice(dmod_all, (0, me * ncol), (NDEV * bl, ncol)))
    grads["w_ada"], grads["b_ada"] = g_w_ada[None], g_b_ada
    k = "w_in"
    g_, d_, m_, v_ = _sum_adamw_transposed(s_in_c0, s_in_c1, weights[k][0], mom_m[k][0], mom_v[k][0], "sum_adamw_" + k)
    grads[k], delta[k], new_m[k], new_v[k] = g_[None], d_[None], m_[None], v_[None]
    k = "ffn_w_up"
    g_, d_, m_, v_ = _sum_adamw(s_up, weights[k][0].T, mom_m[k][0].T, mom_v[k][0].T, "sum_adamw_" + k)
    grads[k], delta[k], new_m[k], new_v[k] = g_.T[None], d_.T[None], m_.T[None], v_.T[None]
    for k, gk in full.items():
        if k in ("lru_conv_w", "conv_w", "ffn_conv_w"):
            nloc = gk.shape[1] // NDEV
            gk = lax.dynamic_slice(gk, (0, me * nloc), (gk.shape[0], nloc))
        grads[k] = gk.reshape(weights[k].shape)

    big = ("w_ada", "w_in", "w_out", "ffn_w_up", "ffn_w_down")
    for k in ("w_ada",):
        d_, m_, v_ = _adamw(weights[k][0], grads[k][0], mom_m[k][0], mom_v[k][0], "adamw_" + k)
        delta[k], new_m[k], new_v[k] = d_[None], m_[None], v_[None]
    small_names = [k for k in names if k not in big]
    d_, m_, v_ = _adamw_many([weights[k] for k in small_names], [grads[k] for k in small_names],
                             [mom_m[k] for k in small_names], [mom_v[k] for k in small_names], "adamw_small")
    for k, dk, mk, vk in zip(small_names, d_, m_, v_):
        delta[k], new_m[k], new_v[k] = dk, mk, vk

    return (loss, grad_x, *[grads[k] for k in names], *[delta[k] for k in names], *[new_m[k] for k in names],
            *[new_v[k] for k in names])
```

```python
import functools
import math

import jax
import jax.numpy as jnp
import numpy as np
from jax import lax
from jax.experimental import pallas as pl
from jax.experimental.pallas import tpu as pltpu

NDEV = 8
D = 1024
LW = 512
CW = 512
HEAD = 64
DFF = 2816
FF_CHUNK = 1408
LRU_K = 4
CONV_K = 31
FFN_K = 3
LRU_C = 8.0
ALPHA = (2 * 1) ** 0.25
LN_EPS = 1e-5
ADAM_LR = 0.001
ADAM_B1 = 0.9
ADAM_B2 = 0.999
ADAM_EPS = 1e-08
ADAM_WD = 0.01
ADAM_STEP = 10

MXU_DTYPE = jnp.bfloat16
WIRE_DTYPE = jnp.bfloat16
TOK_TILE = 256
WG_TOK_TILE = 2048
VMEM_LIMIT = 60 * 1024 * 1024
HALO31 = 32
HALO = 8

F32 = jnp.float32
MESH = pl.DeviceIdType.MESH


def _sigmoid(x):
    return 0.5 * jnp.tanh(0.5 * x) + 0.5


def _shifted_rows(ext_ref, rot_ref, n_rows):
    span = rot_ref.shape[1]
    for r in range(1, 8):
        rot_ref[r - 1] = ext_ref[pl.ds(r, span), :]

    def window(o, r0=0, cols=slice(None)):
        q, r = divmod(o, 8)
        if r == 0:
            return ext_ref[pl.ds(8 * q + r0, n_rows), cols]
        return rot_ref[r - 1, pl.ds(8 * q + r0, n_rows), cols]

    return window


def _store_hilo(hilo_ref, rows, c0, v):
    half = hilo_ref.shape[1] // 2
    hi = v.astype(hilo_ref.dtype)
    hilo_ref[rows, c0:c0 + v.shape[1]] = hi
    hilo_ref[rows, half + c0:half + c0 + v.shape[1]] = (v - hi.astype(F32)).astype(hilo_ref.dtype)


def _scan_bwd_blocks(a_ref, g_ref, car, n_rows):
    row = lax.broadcasted_iota(jnp.int32, (8, LANE), 0)
    for c0 in range(0, a_ref.shape[1], LANE):
        cols = slice(c0, c0 + LANE)
        a_next = jnp.broadcast_to(car[0:1, cols], (8, LANE))
        g_next = jnp.broadcast_to(car[1:2, cols], (8, LANE))
        for r0 in range(n_rows - 8, -1, -8):
            a_blk, g = a_ref[r0:r0 + 8, cols], g_ref[r0:r0 + 8, cols]
            bm = jnp.where(row < 7, pltpu.roll(a_blk, 7, 0), a_next)
            for d in (1, 2, 4):
                keep = row < 8 - d
                g = g + bm * jnp.where(keep, pltpu.roll(g, 8 - d, 0), 0.0)
                bm = bm * jnp.where(keep, pltpu.roll(bm, 8 - d, 0), 1.0)
            g = g + bm * g_next
            g_ref[r0:r0 + 8, cols] = g
            a_next = jnp.broadcast_to(a_blk[0:1, :], (8, LANE))
            g_next = jnp.broadcast_to(g[0:1, :], (8, LANE))
        car[0:1, cols] = a_next[0:1, :]
        car[1:2, cols] = g_next[0:1, :]


def _scan_fwd_blocks(a_ref, u_ref, hcar, n_rows):
    row = lax.broadcasted_iota(jnp.int32, (8, LANE), 0)
    for c0 in range(0, a_ref.shape[1], LANE):
        cols = slice(c0, c0 + LANE)
        h_prev = jnp.broadcast_to(hcar[0:1, cols], (8, LANE))
        for r0 in range(0, n_rows, 8):
            a, u = a_ref[r0:r0 + 8, cols], u_ref[r0:r0 + 8, cols]
            for d in (1, 2, 4):
                keep = row >= d
                u = a * jnp.where(keep, pltpu.roll(u, d, 0), 0.0) + u
                a = a * jnp.where(keep, pltpu.roll(a, d, 0), 1.0)
            u = u + a * h_prev
            u_ref[r0:r0 + 8, cols] = u
            h_prev = jnp.broadcast_to(u[7:8, :], (8, LANE))
        hcar[0:1, cols] = h_prev[0:1, :]


def _dot(a, b):
    return jnp.dot(a, b, preferred_element_type=F32)


def _dot_nt(a, b):
    return lax.dot_general(a, b, (((1,), (1,)), ((), ())), preferred_element_type=F32)


def _dot_tn(a, b):
    return lax.dot_general(a, b, (((0,), (0,)), ((), ())), preferred_element_type=F32)


def _colsum(v):
    return jnp.sum(v, axis=0, keepdims=True)


def _rowmean(v):
    return jnp.mean(v, axis=-1, keepdims=True)


_GELU_C0 = math.sqrt(2.0 / math.pi)
_GELU_C1 = 0.044715


def _gelu_and_grad(x):
    x2 = x * x
    th = jnp.tanh(_GELU_C0 * (x + _GELU_C1 * x * x2))
    ge = 0.5 * x * (1.0 + th)
    dge = 0.5 * (1.0 + th) + 0.5 * x * (1.0 - th * th) * (_GELU_C0 * (1.0 + 3.0 * _GELU_C1 * x2))
    return ge, dge


def _softplus(x):
    return jnp.maximum(x, 0.0) + jnp.log1p(jnp.exp(-jnp.abs(x)))


def _layer_norm_stats(r):
    mu = _rowmean(r)
    dl = r - mu
    var = _rowmean(dl * dl)
    rstd = lax.rsqrt(var + LN_EPS)
    return dl * rstd, rstd


ROW_BLK = 32
LANE = 128


def _row_blocks(n_rows, rb, body, init):
    carry = init
    for j in range(n_rows // rb):
        carry = body(j * rb, carry)
    return carry


def _fold8(v):
    r, c = v.shape
    return v if r == 8 else v.reshape(r // 8, 8, c).sum(axis=0)


def _cast_rows(src_ref, dst_ref, n_rows):
    rb = min(16, n_rows)

    def rows(r0, carry):
        dst_ref[pl.ds(r0, rb), :] = src_ref[pl.ds(r0, rb), :].astype(dst_ref.dtype)
        return carry

    _row_blocks(n_rows, rb, rows, ())


def _full(shape):
    nd = len(shape)
    return pl.BlockSpec(shape, lambda *_: (0,) * nd)


_ANY = pl.BlockSpec(memory_space=pl.ANY)


def _params(n_grid):
    return pltpu.CompilerParams(dimension_semantics=("arbitrary",) * n_grid, vmem_limit_bytes=VMEM_LIMIT)


def _my_place():
    return lax.axis_index("x"), lax.axis_index("y"), lax.axis_index("c")


def _all_gather(arrays, name):
    n_arr = len(arrays)

    def body(*refs):
        start, forward, finish = _gather_steps(refs[:n_arr], refs[n_arr:2 * n_arr], *refs[2 * n_arr:])
        start()
        forward()
        finish()

    return pl.pallas_call(
        body, name=name, out_shape=_gather_out_shapes(arrays),
        in_specs=[_ANY] * n_arr, out_specs=[_ANY] * n_arr, scratch_shapes=_comm_sems(n_arr),
    )(*arrays)


def _gather_out_shapes(arrays):
    return [jax.ShapeDtypeStruct((NDEV * a.shape[0], a.shape[1]), a.dtype) for a in arrays]


def _comm_sems(n_arr):
    return [pltpu.SemaphoreType.DMA((n_arr, 7)), pltpu.SemaphoreType.DMA((n_arr, 7)), pltpu.SemaphoreType.DMA((n_arr,))]


def _gather_steps(x_refs, out_refs, send_sems, recv_sems, local_sems):
    n_arr = len(x_refs)
    x, y, c = _my_place()
    me, sibling = (x, y, c), (x, y, 1 - c)
    chips = [(1 - x, y), (x, 1 - y), (1 - x, 1 - y)]

    def rows(k, px, py, pc):
        m = x_refs[k].shape[0]
        return out_refs[k].at[pl.ds((4 * px + 2 * py + pc) * m, m), :]

    def copy(k, s, block, to, src=None):
        return pltpu.make_async_remote_copy(
            src_ref=rows(k, *block) if src is None else src, dst_ref=rows(k, *block),
            send_sem=send_sems.at[k, s], recv_sem=recv_sems.at[k, s], device_id=to, device_id_type=MESH)

    def mine():
        return [pltpu.make_async_copy(x_refs[k], rows(k, *me), local_sems.at[k]) for k in range(n_arr)]

    def first():
        cps = []
        for k in range(n_arr):
            cps.append(copy(k, 0, me, sibling, src=x_refs[k]))
            cps += [copy(k, 1 + j, me, (*chip, c), src=x_refs[k]) for j, chip in enumerate(chips)]
        return cps

    def passed():
        return [copy(k, 4 + j, (*chip, c), sibling) for k in range(n_arr) for j, chip in enumerate(chips)]

    def start():
        for cp in mine() + first():
            cp.start()

    def forward():
        fwd = passed()
        for k in range(n_arr):
            for j, chip in enumerate(chips):
                copy(k, 1 + j, (*chip, c), me).wait_recv()
                fwd[3 * k + j].start()

    def finish():
        for k in range(n_arr):
            copy(k, 0, sibling, me).wait_recv()
            for j, chip in enumerate(chips):
                copy(k, 4 + j, (*chip, 1 - c), me).wait_recv()
        for cp in first() + passed():
            cp.wait_send()
        for cp in mine():
            cp.wait()

    return start, forward, finish


def _exchange_and_gather(ex_arrays, ga_arrays, name, half=None):
    n_e, n_g = len(ex_arrays), len(ga_arrays)

    def body(*refs):
        e_in, g_in, refs = refs[:n_e], refs[n_e:n_e + n_g], refs[n_e + n_g:]
        e_out, g_out, sems = refs[:n_e], refs[n_e:n_e + n_g], refs[n_e + n_g:]
        e_start, e_finish = _exchange_steps(e_in, e_out, *sems[:3], half=half)
        g_start, g_forward, g_finish = _gather_steps(g_in, g_out, *sems[3:])
        g_start()
        e_start()
        g_forward()
        g_finish()
        e_finish()

    res = pl.pallas_call(
        body, name=name, out_shape=_exchange_out_shapes(ex_arrays, half) + _gather_out_shapes(ga_arrays),
        in_specs=[_ANY] * (n_e + n_g), out_specs=[_ANY] * (n_e + n_g),
        scratch_shapes=_comm_sems(n_e) + _comm_sems(n_g),
    )(*ex_arrays, *ga_arrays)
    return res[:n_e], res[n_e:]


def _exchange_out_shapes(arrays, half=None):
    n_blocks = NDEV if half is None else NDEV // 2
    return [jax.ShapeDtypeStruct((NDEV, a.shape[0] // n_blocks, a.shape[1]), a.dtype) for a in arrays]


def _exchange_steps(g_refs, out_refs, send_sems, recv_sems, local_sems, half=None):
    n_arr = len(g_refs)
    x, y, c = _my_place()
    me = 4 * x + 2 * y + c
    receives = True if half is None else c == half

    def when(cond, fn):
        if cond is True:
            fn()
        else:
            pl.when(cond)(fn)

    def sends(rel):
        return True if half is None else (c ^ (rel & 1)) == half

    def copy(k, rel):
        px, py, pc = x ^ ((rel >> 2) & 1), y ^ ((rel >> 1) & 1), c ^ (rel & 1)
        m = out_refs[k].shape[1]
        blk = 4 * px + 2 * py + pc if half is None else 2 * px + py
        return pltpu.make_async_remote_copy(
            src_ref=g_refs[k].at[pl.ds(blk * m, m), :], dst_ref=out_refs[k].at[me],
            send_sem=send_sems.at[k, rel - 1], recv_sem=recv_sems.at[k, rel - 1],
            device_id=(px, py, pc), device_id_type=MESH)

    def local(k):
        m = out_refs[k].shape[1]
        blk = me if half is None else 2 * x + y
        return pltpu.make_async_copy(g_refs[k].at[pl.ds(blk * m, m), :], out_refs[k].at[me], local_sems.at[k])

    def start():
        for k in range(n_arr):
            when(receives, local(k).start)
            for rel in range(1, NDEV):
                when(sends(rel), copy(k, rel).start)

    def finish():
        for k in range(n_arr):
            for rel in range(1, NDEV):
                when(sends(rel), copy(k, rel).wait_send)
                when(receives, copy(k, rel).wait_recv)
            when(receives, local(k).wait)

    return start, finish


def _ada_fwd(c_all, w_ada_loc, b_ada_loc):
    def body(c_ref, w_ref, b_ref, o_ref):
        cv = c_ref[...]
        ca = (cv * _sigmoid(cv)).astype(MXU_DTYPE)
        o_ref[...] = _dot(ca, w_ref[...].astype(MXU_DTYPE)) + b_ref[...]

    return pl.pallas_call(
        body, name="ada_fwd", out_shape=jax.ShapeDtypeStruct((c_all.shape[0], w_ada_loc.shape[1]), F32),
        in_specs=[_full(c_all.shape), _full(w_ada_loc.shape), _full(b_ada_loc.shape)],
        out_specs=_full((c_all.shape[0], w_ada_loc.shape[1])),
        compiler_params=pltpu.CompilerParams(vmem_limit_bytes=VMEM_LIMIT),
    )(c_all, w_ada_loc, b_ada_loc)


def _ada_bwd(c_all, dmod_all, dmod_cols):
    def body(c_ref, da_ref, d_ref, o_ref, b_ref):
        cv = c_ref[...]
        ca = (cv * _sigmoid(cv)).astype(MXU_DTYPE)
        o_ref[...] = _dot_tn(ca, d_ref[...].astype(MXU_DTYPE))
        b_ref[...] = _colsum(da_ref[...])

    return pl.pallas_call(
        body, name="ada_bwd",
        out_shape=[jax.ShapeDtypeStruct((c_all.shape[1], dmod_cols.shape[1]), F32),
                   jax.ShapeDtypeStruct((1, dmod_all.shape[1]), F32)],
        in_specs=[_full(c_all.shape), _full(dmod_all.shape), _full(dmod_cols.shape)],
        out_specs=[_full((c_all.shape[1], dmod_cols.shape[1])), _full((1, dmod_all.shape[1]))],
        compiler_params=pltpu.CompilerParams(vmem_limit_bytes=VMEM_LIMIT),
    )(c_all, dmod_all, dmod_cols)


def _mixer_fwd(x, mods, win_t, wout, wri, mavg2, va, vb, ln1, ffn_shards):
    bl, seq, _ = x.shape
    t = min(TOK_TILE, seq)
    nt = seq // t
    n_g = len(ffn_shards)
    rb = min(ROW_BLK, t)

    def body(x_ref, mod_ref, win_hbm, wout_hbm, wri_ref, mavg_ref, va_ref, vb_ref, ln1_ref, *rest):
        shard_refs, rest = rest[:n_g], rest[n_g:]
        proj_ref, h_ref, vc_ref, y_ref, mixed_ref, x1_ref, u1_ref = rest[:7]
        gathered_refs, rest = rest[7:7 + n_g], rest[7 + n_g:]
        (win_v, wout_v, xa_ext, vg_ext, rot_a, hcar, sems, rot_c, xc_scr, a_scr, st_a, st_b, xcb_scr, gates, hilo,
         g_send, g_recv, g_local) = rest
        b, i = pl.program_id(0), pl.program_id(1)
        step = b * nt + i
        g_start, g_forward, g_finish = _gather_steps(shard_refs, gathered_refs, g_send, g_recv, g_local)

        @pl.when(step == 0)
        def _():
            g_start()
            cps = [pltpu.make_async_copy(win_hbm, win_v, sems.at[0]), pltpu.make_async_copy(wout_hbm, wout_v, sems.at[1])]
            for cp in cps:
                cp.start()
            for cp in cps:
                cp.wait()

        pl.when(step == (bl * nt) // 2)(g_forward)

        @pl.when(i == 0)
        def _():
            xa_ext[0:HALO, :] = jnp.zeros((HALO, LW), F32)
            vg_ext[0:HALO31, :] = jnp.zeros((HALO31, CW), F32)
            hcar[...] = jnp.zeros_like(hcar)

        blocks = [(r0, c0) for c0 in range(0, LW, LANE) for r0 in range(0, t, rb)]

        def in_rows(r0, acc):
            rows = pl.ds(r0, 16)
            u1_ref[rows, :] = (x_ref[rows, :] * (1.0 + mod_ref[1:2, :]) + mod_ref[0:1, :]).astype(MXU_DTYPE)
            return acc

        _row_blocks(t, min(16, t), in_rows, ())
        proj_ref[...] = _dot_nt(u1_ref[...], win_v[...])

        xa_ext[HALO:HALO + t, :] = proj_ref[:, 0:LW]
        for k in range(LRU_K - 1):
            rot_c[k] = xa_ext[pl.ds(HALO - (LRU_K - 1) + k, t), :]
        for r0, c0 in blocks:
            rows, cols = pl.ds(r0, rb), slice(c0, c0 + LANE)
            xc = va_ref[4:5, cols] + va_ref[LRU_K - 1:LRU_K, cols] * xa_ext[pl.ds(HALO + r0, rb), cols]
            for k in range(LRU_K - 1):
                xc = xc + va_ref[k:k + 1, cols] * rot_c[k, rows, cols]
            xc_scr[rows, cols] = xc
            xcb_scr[rows, cols] = xc.astype(MXU_DTYPE)
        xa_ext[0:HALO, :] = xa_ext[t:t + HALO, :]
        gates[...] = _dot(xcb_scr[...], wri_ref[...])
        for r0, c0 in blocks:
            rows, cols = pl.ds(r0, rb), slice(c0, c0 + LANE)
            r = _sigmoid(gates[rows, cols] + va_ref[5:6, cols])
            ig = _sigmoid(gates[rows, LW + c0:LW + c0 + LANE] + va_ref[6:7, cols])
            la = (-LRU_C) * r * _softplus(-va_ref[7:8, cols])
            a = jnp.exp(la)
            a_scr[rows, cols] = a
            h_ref[rows, cols] = jnp.sqrt(-jnp.tanh(la) * (a * a + 1.0)) * (ig * xc_scr[rows, cols])
        _scan_fwd_blocks(a_scr, h_ref, hcar, t)

        for r0, c0 in blocks:
            rows, cols, pc = pl.ds(r0, rb), slice(c0, c0 + LANE), 2 * LW + c0
            vg_ext[pl.ds(HALO31 + r0, rb), cols] = (
                proj_ref[rows, pc:pc + LANE] * _sigmoid(proj_ref[rows, pc + CW:pc + CW + LANE]))
        vg_win = _shifted_rows(vg_ext, rot_a, rb)
        for r0, c0 in blocks:
            rows, cols = pl.ds(r0, rb), slice(c0, c0 + LANE)
            vc = vb_ref[CONV_K:CONV_K + 1, cols]
            for k in range(CONV_K):
                vc = vc + vb_ref[k:k + 1, cols] * vg_win(HALO31 - (CONV_K - 1) + k, r0, cols)
            vc_ref[rows, cols] = vc
            _store_hilo(hilo, rows, c0, vc)
        vg_ext[0:HALO31, :] = vg_ext[t:t + HALO31, :]
        mavg2 = mavg_ref[...]
        st_a[...] = _dot(hilo[...], mavg2)
        for r0, c0 in blocks:
            rows, cols = pl.ds(r0, rb), slice(c0, c0 + LANE)
            dl = vc_ref[rows, cols] - st_a[rows, cols]
            st_a[rows, cols] = dl
            _store_hilo(hilo, rows, c0, dl * dl)
        st_b[...] = _dot(hilo[...], mavg2)
        for r0, c0 in blocks:
            rows, cols = pl.ds(r0, rb), slice(c0, c0 + LANE)
            ge, _ = _gelu_and_grad(proj_ref[rows, LW + c0:LW + c0 + LANE])
            y_ref[rows, cols] = (ge * h_ref[rows, cols]).astype(MXU_DTYPE)
            yl = (st_a[rows, cols] * lax.rsqrt(st_b[rows, cols] + LN_EPS) * vb_ref[CONV_K + 1:CONV_K + 2, cols]
                  + vb_ref[CONV_K + 2:CONV_K + 3, cols])
            y_ref[rows, LW + c0:LW + c0 + LANE] = (yl * _sigmoid(yl)).astype(MXU_DTYPE)
        mixed_ref[...] = _dot(y_ref[...], wout_v[...])

        def out_rows(r0, acc):
            rows = pl.ds(r0, 8)
            xh, _ = _layer_norm_stats(ALPHA * x_ref[rows, :] + (1.0 + mod_ref[2:3, :]) * mixed_ref[rows, :])
            x1_ref[rows, :] = xh * ln1_ref[0:1, :] + ln1_ref[1:2, :]
            return acc

        _row_blocks(t, 8, out_rows, ())

        pl.when(step == bl * nt - 1)(g_finish)

    tok = lambda w: pl.BlockSpec((None, t, w), lambda b, i: (b, i, 0))
    outs = [(2 * LW + 2 * CW, F32), (LW, F32), (CW, F32), (D, MXU_DTYPE), (D, F32), (D, F32), (D, MXU_DTYPE)]
    return pl.pallas_call(
        body, name="mixer_fwd", grid=(bl, nt),
        out_shape=[jax.ShapeDtypeStruct((bl, seq, w), dt) for w, dt in outs] + _gather_out_shapes(ffn_shards),
        in_specs=[tok(D), pl.BlockSpec((None, 8, D), lambda b, i: (b, 0, 0)), _ANY, _ANY,
                  _full(wri.shape), _full(mavg2.shape), _full(va.shape), _full(vb.shape), _full(ln1.shape)]
        + [_ANY] * n_g,
        out_specs=[tok(w) for w, _ in outs] + [_ANY] * n_g,
        scratch_shapes=[pltpu.VMEM(win_t.shape, MXU_DTYPE), pltpu.VMEM(wout.shape, MXU_DTYPE),
                        pltpu.VMEM((t + HALO, LW), F32), pltpu.VMEM((t + HALO31, CW), F32),
                        pltpu.VMEM((7, t + HALO31 - 8, CW), F32), pltpu.VMEM((8, LW), F32),
                        pltpu.SemaphoreType.DMA((2,)), pltpu.VMEM((LRU_K - 1, t, LW), F32)]
        + [pltpu.VMEM((t, LW), F32)] * 4
        + [pltpu.VMEM((t, LW), MXU_DTYPE), pltpu.VMEM((t, 2 * LW), F32), pltpu.VMEM((t, 2 * CW), MXU_DTYPE)]
        + _comm_sems(n_g),
        compiler_params=_params(2),
    )(x, mods, win_t, wout, wri, mavg2, va, vb, ln1, *ffn_shards)


def _ffn_fwd(x1, mods, tgt, wup_t, wdown, f3, ln2):
    bl, seq, _ = x1.shape
    t = min(TOK_TILE, seq)
    nt = seq // t
    n_chunk = DFF // FF_CHUNK

    rb = min(ROW_BLK, t)

    def body(x1_ref, mod_ref, tgt_ref, wup_hbm, wdown_hbm, f3_ref, ln2_ref,
             hh_ref, z_ref, u2_ref, y2_ref, loss_ref,
             wup_v, wdown_v, g_ext, rot, sems):
        b, i = pl.program_id(0), pl.program_id(1)

        @pl.when((b == 0) & (i == 0))
        def _():
            cps = [pltpu.make_async_copy(wup_hbm, wup_v, sems.at[0]), pltpu.make_async_copy(wdown_hbm, wdown_v, sems.at[1])]
            for cp in cps:
                cp.start()
            for cp in cps:
                cp.wait()
            loss_ref[...] = jnp.zeros_like(loss_ref)

        @pl.when(i == 0)
        def _():
            for ch in range(n_chunk):
                g_ext[ch, 0:HALO, :] = jnp.zeros((HALO, FF_CHUNK), F32)

        def in_rows(r0, acc):
            rows = pl.ds(r0, 16)
            u2_ref[rows, :] = (x1_ref[rows, :] * (1.0 + mod_ref[4:5, :]) + mod_ref[3:4, :]).astype(MXU_DTYPE)
            return acc

        _row_blocks(t, min(16, t), in_rows, ())
        for ch in range(n_chunk):
            lo = ch * FF_CHUNK
            hh_ref[:, lo:lo + FF_CHUNK] = _dot_nt(u2_ref[...], wup_v[lo:lo + FF_CHUNK, :])
            hh_ref[:, DFF + lo:DFF + lo + FF_CHUNK] = _dot_nt(u2_ref[...], wup_v[DFF + lo:DFF + lo + FF_CHUNK, :])
            g_ext[ch, HALO:HALO + t, :] = hh_ref[:, DFF + lo:DFF + lo + FF_CHUNK]
            for k in range(FFN_K - 1):
                rot[k] = g_ext[ch, pl.ds(HALO - (FFN_K - 1) + k, t), :]
            for cb in range(FF_CHUNK // LANE):
                cols = slice(cb * LANE, (cb + 1) * LANE)
                fcols = slice(lo + cb * LANE, lo + (cb + 1) * LANE)
                for r0 in range(0, t, rb):
                    rows = pl.ds(r0, rb)
                    gc = (f3_ref[FFN_K:FFN_K + 1, fcols] + f3_ref[0:1, fcols] * rot[0, rows, cols]
                          + f3_ref[1:2, fcols] * rot[1, rows, cols]
                          + f3_ref[2:3, fcols] * g_ext[ch, pl.ds(HALO + r0, rb), cols])
                    z_ref[rows, fcols] = (gc * _sigmoid(gc) * hh_ref[rows, fcols]).astype(MXU_DTYPE)
            g_ext[ch, 0:HALO, :] = g_ext[ch, t:t + HALO, :]
        y2_ref[...] = _dot(z_ref[...], wdown_v[...])

        def out_rows(r0, acc):
            rows = pl.ds(r0, 8)
            xh, _ = _layer_norm_stats(ALPHA * x1_ref[rows, :] + (1.0 + mod_ref[5:6, :]) * y2_ref[rows, :])
            err = xh * ln2_ref[0:1, :] + ln2_ref[1:2, :] - tgt_ref[rows, :]
            return acc + err * err

        e2 = _row_blocks(t, 8, out_rows, jnp.zeros((8, D), F32))
        part = e2[:, 0:128]
        for j in range(1, D // 128):
            part = part + e2[:, 128 * j:128 * (j + 1)]
        loss_ref[...] += part

    tok = lambda w: pl.BlockSpec((None, t, w), lambda b, i: (b, i, 0))
    outs = [(2 * DFF, F32), (DFF, MXU_DTYPE), (D, MXU_DTYPE), (D, F32)]
    return pl.pallas_call(
        body, name="ffn_fwd", grid=(bl, nt),
        out_shape=[jax.ShapeDtypeStruct((bl, seq, w), dt) for w, dt in outs] + [jax.ShapeDtypeStruct((8, 128), F32)],
        in_specs=[tok(D), pl.BlockSpec((None, 8, D), lambda b, i: (b, 0, 0)), tok(D), _ANY, _ANY,
                  _full(f3.shape), _full(ln2.shape)],
        out_specs=[tok(w) for w, _ in outs] + [_full((8, 128))],
        scratch_shapes=[pltpu.VMEM(wup_t.shape, MXU_DTYPE), pltpu.VMEM(wdown.shape, MXU_DTYPE),
                        pltpu.VMEM((n_chunk, t + HALO, FF_CHUNK), F32), pltpu.VMEM((FFN_K - 1, t, FF_CHUNK), F32),
                        pltpu.SemaphoreType.DMA((2,))],
        compiler_params=_params(2),
    )(x1, mods, tgt, wup_t, wdown, f3, ln2)


def _ffn_bwd(x1, y2, tgt, hh, mods, wup_t, wdown, f3, ln2):
    bl, seq, _ = x1.shape
    t = min(TOK_TILE, seq)
    nt = seq // t
    n_chunk = DFF // FF_CHUNK
    rb = min(ROW_BLK, t)

    def body(x1_ref, y2_ref, tgt_ref, hh_ref, halo_ref, mod_ref, wup_hbm, wdown_hbm, f3_ref, ln2_ref,
             dx1_ref, dy2_ref, dhh_ref, dln2_ref, df3_ref, dmod_ref,
             wup_v, wdown_v, g_ext, dgc_ext, rot, dz_scr, wide_scr, sems):
        b, i = pl.program_id(0), pl.program_id(1)
        tt = nt - 1 - i

        @pl.when((b == 0) & (i == 0))
        def _():
            cps = [pltpu.make_async_copy(wup_hbm, wup_v, sems.at[0]), pltpu.make_async_copy(wdown_hbm, wdown_v, sems.at[1])]
            for cp in cps:
                cp.start()
            for cp in cps:
                cp.wait()
            dln2_ref[...] = jnp.zeros_like(dln2_ref)
            df3_ref[...] = jnp.zeros_like(df3_ref)

        @pl.when(i == 0)
        def _():
            dmod_ref[...] = jnp.zeros_like(dmod_ref)
            for ch in range(n_chunk):
                dgc_ext[ch, t:t + HALO, :] = jnp.zeros((HALO, FF_CHUNK), F32)

        def ln_rows(r0, acc):
            rows = pl.ds(r0, 8)
            x1v, y2v = x1_ref[rows, :], y2_ref[rows, :]
            gt = mod_ref[5:6, :]
            xh, rstd = _layer_norm_stats(ALPHA * x1v + (1.0 + gt) * y2v)
            g2 = ln2_ref[0:1, :]
            dx2 = (xh * g2 + ln2_ref[1:2, :] - tgt_ref[rows, :]) * (1.0 / D)
            dxh = dx2 * g2
            dr2 = rstd * (dxh - _rowmean(dxh) - xh * _rowmean(dxh * xh))
            wide_scr[rows, :] = (1.0 + gt) * dr2
            dx1_ref[rows, :] = ALPHA * dr2
            return acc[0] + dx2 * xh, acc[1] + dx2, acc[2] + dr2 * y2v

        zero_d = jnp.zeros((8, D), F32)
        acc = _row_blocks(t, 8, ln_rows, (zero_d, zero_d, zero_d))
        dln2_ref[0:1, :] += _colsum(acc[0])
        dln2_ref[1:2, :] += _colsum(acc[1])
        dmod_ref[2:3, :] += _colsum(acc[2])
        _cast_rows(wide_scr, dy2_ref, t)

        halo_keep = (tt > 0).astype(F32)
        zero_l = jnp.zeros((8, LANE), F32)
        for ch in range(n_chunk):
            lo = ch * FF_CHUNK
            dz_scr[...] = _dot_nt(dy2_ref[...], wdown_v[lo:lo + FF_CHUNK, :])
            g_ext[0:HALO, :] = halo_ref[:, lo:lo + FF_CHUNK] * halo_keep
            g_ext[HALO:HALO + t, :] = hh_ref[:, DFF + lo:DFF + lo + FF_CHUNK]
            for k in range(FFN_K - 1):
                rot[k] = g_ext[pl.ds(HALO - (FFN_K - 1) + k, t), :]
            for cb in range(FF_CHUNK // LANE):
                cols = slice(cb * LANE, (cb + 1) * LANE)
                fcols = slice(lo + cb * LANE, lo + (cb + 1) * LANE)

                def gate_rows(r0, acc, cols=cols, fcols=fcols, ch=ch):
                    rows = pl.ds(r0, rb)
                    taps = [rot[0, rows, cols], rot[1, rows, cols], g_ext[pl.ds(r0 + HALO, rb), cols]]
                    gc = f3_ref[FFN_K:FFN_K + 1, fcols]
                    for k in range(FFN_K):
                        gc = gc + f3_ref[k:k + 1, fcols] * taps[k]
                    s = _sigmoid(gc)
                    dz = dz_scr[rows, cols]
                    dhh_ref[rows, fcols] = (dz * (gc * s)).astype(MXU_DTYPE)
                    dgc = dz * hh_ref[rows, fcols] * (s * (1.0 + gc * (1.0 - s)))
                    dgc_ext[ch, rows, cols] = dgc
                    return tuple(acc[k] + _fold8(dgc * taps[k]) for k in range(FFN_K)) + (acc[FFN_K] + _fold8(dgc),)

                acc = _row_blocks(t, rb, gate_rows, (zero_l,) * (FFN_K + 1))
                for k in range(FFN_K + 1):
                    df3_ref[k:k + 1, fcols] += _colsum(acc[k])
            for k in range(FFN_K - 1):
                rot[k] = dgc_ext[ch, pl.ds(k + 1, t), :]
            for cb in range(FF_CHUNK // LANE):
                cols = slice(cb * LANE, (cb + 1) * LANE)
                fcols = slice(lo + cb * LANE, lo + (cb + 1) * LANE)
                gcols = slice(DFF + lo + cb * LANE, DFF + lo + (cb + 1) * LANE)

                def dg_rows(r0, acc, cols=cols, fcols=fcols, gcols=gcols, ch=ch):
                    rows = pl.ds(r0, rb)
                    dg = (f3_ref[2:3, fcols] * dgc_ext[ch, rows, cols] + f3_ref[1:2, fcols] * rot[0, rows, cols]
                          + f3_ref[0:1, fcols] * rot[1, rows, cols])
                    dhh_ref[rows, gcols] = dg.astype(MXU_DTYPE)
                    return acc

                _row_blocks(t, rb, dg_rows, ())
            dgc_ext[ch, t:t + HALO, :] = dgc_ext[ch, 0:HALO, :]

        wide_scr[...] = _dot(dhh_ref[...], wup_v[...])

        def out_rows(r0, acc):
            rows = pl.ds(r0, 8)
            du2 = wide_scr[rows, :]
            dx1_ref[rows, :] = dx1_ref[rows, :] + du2 * (1.0 + mod_ref[4:5, :])
            return acc[0] + du2 * x1_ref[rows, :], acc[1] + du2

        acc = _row_blocks(t, 8, out_rows, (zero_d, zero_d))
        dmod_ref[1:2, :] += _colsum(acc[0])
        dmod_ref[0:1, :] += _colsum(acc[1])

    rev = lambda w: pl.BlockSpec((None, t, w), lambda b, i: (b, nt - 1 - i, 0))
    halo = pl.BlockSpec((None, HALO, DFF), lambda b, i: (b, jnp.maximum((nt - 1 - i) * (t // HALO) - 1, 0), 1))
    return pl.pallas_call(
        body, name="ffn_bwd", grid=(bl, nt),
        out_shape=[jax.ShapeDtypeStruct((bl, seq, D), F32), jax.ShapeDtypeStruct((bl, seq, D), MXU_DTYPE),
                   jax.ShapeDtypeStruct((bl, seq, 2 * DFF), MXU_DTYPE), jax.ShapeDtypeStruct((8, D), F32),
                   jax.ShapeDtypeStruct((8, DFF), F32), jax.ShapeDtypeStruct((bl, 8, D), F32)],
        in_specs=[rev(D), rev(D), rev(D), rev(2 * DFF), halo, pl.BlockSpec((None, 8, D), lambda b, i: (b, 0, 0)),
                  _ANY, _ANY, _full(f3.shape), _full(ln2.shape)],
        out_specs=[rev(D), rev(D), rev(2 * DFF), _full((8, D)), _full((8, DFF)),
                   pl.BlockSpec((None, 8, D), lambda b, i: (b, 0, 0))],
        scratch_shapes=[pltpu.VMEM(wup_t.shape, MXU_DTYPE), pltpu.VMEM(wdown.shape, MXU_DTYPE),
                        pltpu.VMEM((t + HALO, FF_CHUNK), F32), pltpu.VMEM((n_chunk, t + HALO, FF_CHUNK), F32),
                        pltpu.VMEM((FFN_K - 1, t, FF_CHUNK), F32), pltpu.VMEM((t, FF_CHUNK), F32),
                        pltpu.VMEM((t, D), F32), pltpu.SemaphoreType.DMA((2,))],
        compiler_params=_params(2),
    )(x1, y2, tgt, hh, hh, mods, wup_t, wdown, f3, ln2)


def _mixer_bwd(x, dx1, proj, h, vc, mixed, mods, win_t, wout, wri, mavg2, va, vb, ln1, ffn_wgrads):
    bl, seq, _ = x.shape
    t = min(TOK_TILE, seq)
    nt = seq // t
    pw = 2 * LW + 2 * CW
    n_g = len(ffn_wgrads)
    rb = min(ROW_BLK, t)
    half_f32 = [pltpu.VMEM((t, LW), F32)] * 10

    def body(x_ref, dx1_ref, proj_ref, phalo_ref, h_ref, hhalo_ref, vc_ref, mixed_ref, mod_ref,
             win_hbm, wout_hbm, wri_ref, mavg_ref, va_ref, vb_ref, ln1_ref, *rest):
        wgrad_refs, rest = rest[:n_g], rest[n_g:]
        gx_ref, dproj_ref, dm_ref, ga_ref, gw31_ref, dwr_ref, dwi_ref, dln1_ref, dmod_ref = rest[:9]
        slot_refs, rest = rest[9:9 + n_g], rest[9 + n_g:]
        (win_v, wout_v, xa_ext, vg_scr, dvc_ext, dxc_ext, rot_b, car, sems,
         st_a, st_b, st_c, yn_scr, dyn_scr, hprev_scr, xc_scr, a_scr, mult_scr, g_scr,
         wide, gates, hilo_a, hilo_b, dgates, xcb_scr, rot_c, rot_d, hp_ext, e_send, e_recv, e_local) = rest
        b, i = pl.program_id(0), pl.program_id(1)
        tt = nt - 1 - i
        e_start, e_finish = _exchange_steps(wgrad_refs, slot_refs, e_send, e_recv, e_local)

        @pl.when((b == 0) & (i == 0))
        def _():
            e_start()
            cps = [pltpu.make_async_copy(win_hbm, win_v, sems.at[0]), pltpu.make_async_copy(wout_hbm, wout_v, sems.at[1])]
            for cp in cps:
                cp.start()
            for cp in cps:
                cp.wait()
            for ref in (ga_ref, gw31_ref, dwr_ref, dwi_ref, dln1_ref):
                ref[...] = jnp.zeros_like(ref)

        @pl.when(i == 0)
        def _():
            dmod_ref[...] = jnp.zeros_like(dmod_ref)
            dvc_ext[t:t + HALO31, :] = jnp.zeros((HALO31, CW), F32)
            dxc_ext[t:t + HALO, :] = jnp.zeros((HALO, LW), F32)
            car[...] = jnp.zeros_like(car)

        halo_keep = (tt > 0).astype(F32)
        zero_d = jnp.zeros((8, D), F32)
        zero_l = jnp.zeros((8, LANE), F32)
        blocks = [(r0, c0) for c0 in range(0, LW, LANE) for r0 in range(0, t, rb)]

        def add_row(ref, row, c0, acc):
            ref[row:row + 1, c0:c0 + LANE] += _colsum(acc)

        def ln_rows(r0, acc):
            rows = pl.ds(r0, 8)
            xv, mixed = x_ref[rows, :], mixed_ref[rows, :]
            gt = mod_ref[2:3, :]
            xh, rstd = _layer_norm_stats(ALPHA * xv + (1.0 + gt) * mixed)
            dx1 = dx1_ref[rows, :]
            dxh = dx1 * ln1_ref[0:1, :]
            dr1 = rstd * (dxh - _rowmean(dxh) - xh * _rowmean(dxh * xh))
            wide[rows, :] = (1.0 + gt) * dr1
            gx_ref[rows, :] = ALPHA * dr1
            return acc[0] + dx1 * xh, acc[1] + dx1, acc[2] + dr1 * mixed

        acc = _row_blocks(t, 8, ln_rows, (zero_d, zero_d, zero_d))
        dln1_ref[0:1, :] += _colsum(acc[0])
        dln1_ref[1:2, :] += _colsum(acc[1])
        dmod_ref[2:3, :] += _colsum(acc[2])
        _cast_rows(wide, dm_ref, t)
        wide[...] = _dot_nt(dm_ref[...], wout_v[...])

        mavg2 = mavg_ref[...]
        for r0, c0 in blocks:
            rows, cols = pl.ds(r0, rb), slice(c0, c0 + LANE)
            _store_hilo(hilo_a, rows, c0, vc_ref[rows, cols])
        st_a[...] = _dot(hilo_a[...], mavg2)
        for r0, c0 in blocks:
            rows, cols = pl.ds(r0, rb), slice(c0, c0 + LANE)
            dl = vc_ref[rows, cols] - st_a[rows, cols]
            yn_scr[rows, cols] = dl
            _store_hilo(hilo_a, rows, c0, dl * dl)
        st_b[...] = _dot(hilo_a[...], mavg2)
        for c0 in range(0, CW, LANE):
            cols, pc = slice(c0, c0 + LANE), 2 * LW + c0
            ng, nb = vb_ref[CONV_K + 1:CONV_K + 2, cols], vb_ref[CONV_K + 2:CONV_K + 3, cols]
            acc_g, acc_b = zero_l, zero_l
            for r0 in range(0, t, rb):
                rows = pl.ds(r0, rb)
                rs = lax.rsqrt(st_b[rows, cols] + LN_EPS)
                yn = yn_scr[rows, cols] * rs
                yl = yn * ng + nb
                s = _sigmoid(yl)
                dyl = wide[rows, LW + c0:LW + c0 + LANE] * (s * (1.0 + yl * (1.0 - s)))
                acc_g, acc_b = acc_g + _fold8(dyl * yn), acc_b + _fold8(dyl)
                dyn = dyl * ng
                st_b[rows, cols] = rs
                yn_scr[rows, cols] = yn
                dyn_scr[rows, cols] = dyn
                _store_hilo(hilo_a, rows, c0, dyn)
                _store_hilo(hilo_b, rows, c0, dyn * yn)
                vg_scr[rows, cols] = proj_ref[rows, pc:pc + LANE] * _sigmoid(proj_ref[rows, pc + CW:pc + CW + LANE])
            add_row(ga_ref, 8, c0, acc_g)
            add_row(ga_ref, 9, c0, acc_b)
        st_a[...] = _dot(hilo_a[...], mavg2)
        st_c[...] = _dot(hilo_b[...], mavg2)
        for c0 in range(0, CW, LANE):
            cols = slice(c0, c0 + LANE)
            acc_b = zero_l
            for r0 in range(0, t, rb):
                rows = pl.ds(r0, rb)
                dvc = st_b[rows, cols] * (dyn_scr[rows, cols] - st_a[rows, cols] - yn_scr[rows, cols] * st_c[rows, cols])
                dvc_ext[rows, cols] = dvc
                acc_b = acc_b + _fold8(dvc)
            add_row(ga_ref, 10, c0, acc_b)
        dvc_win = _shifted_rows(dvc_ext, rot_b, rb)
        for c0 in range(0, CW, LANE):
            cols, pc = slice(c0, c0 + LANE), 2 * LW + c0
            for k in range(CONV_K):
                acc_w = zero_l
                for r0 in range(0, t, rb):
                    acc_w = acc_w + _fold8(vg_scr[pl.ds(r0, rb), cols] * dvc_win(CONV_K - 1 - k, r0, cols))
                add_row(gw31_ref, k, c0, acc_w)
            for r0 in range(0, t, rb):
                rows = pl.ds(r0, rb)
                dvg = jnp.zeros((rb, LANE), F32)
                for k in range(CONV_K):
                    dvg = dvg + vb_ref[k:k + 1, cols] * dvc_win(CONV_K - 1 - k, r0, cols)
                vbr = proj_ref[rows, pc:pc + LANE]
                sgb = _sigmoid(proj_ref[rows, pc + CW:pc + CW + LANE])
                dproj_ref[rows, pc:pc + LANE] = (dvg * sgb).astype(MXU_DTYPE)
                dproj_ref[rows, pc + CW:pc + CW + LANE] = (dvg * vbr * sgb * (1.0 - sgb)).astype(MXU_DTYPE)
        dvc_ext[t:t + HALO31, :] = dvc_ext[0:HALO31, :]

        xa_ext[0:HALO, :] = phalo_ref[HALO31 - HALO:HALO31, 0:LW] * halo_keep
        xa_ext[HALO:HALO + t, :] = proj_ref[:, 0:LW]
        for k in range(LRU_K - 1):
            rot_c[k] = xa_ext[pl.ds(HALO - (LRU_K - 1) + k, t), :]
        hp_ext[0:HALO, :] = hhalo_ref[...] * halo_keep
        hp_ext[HALO:HALO + t, :] = h_ref[...]
        hprev_scr[...] = hp_ext[pl.ds(HALO - 1, t), :]

        def xa_tap(k, rows, r0, cols):
            return xa_ext[pl.ds(HALO + r0, rb), cols] if k == LRU_K - 1 else rot_c[k, rows, cols]

        for r0, c0 in blocks:
            rows, cols = pl.ds(r0, rb), slice(c0, c0 + LANE)
            xc = va_ref[4:5, cols]
            for k in range(LRU_K):
                xc = xc + va_ref[k:k + 1, cols] * xa_tap(k, rows, r0, cols)
            xc_scr[rows, cols] = xc
            xcb_scr[rows, cols] = xc.astype(MXU_DTYPE)
        gates[...] = _dot(xcb_scr[...], wri_ref[...])
        for r0, c0 in blocks:
            rows, cols = pl.ds(r0, rb), slice(c0, c0 + LANE)
            r = _sigmoid(gates[rows, cols] + va_ref[5:6, cols])
            ig = _sigmoid(gates[rows, LW + c0:LW + c0 + LANE] + va_ref[6:7, cols])
            la = (-LRU_C) * r * _softplus(-va_ref[7:8, cols])
            a = jnp.exp(la)
            gates[rows, cols] = r
            gates[rows, LW + c0:LW + c0 + LANE] = ig
            a_scr[rows, cols] = a
            mult_scr[rows, cols] = jnp.sqrt(-jnp.tanh(la) * (a * a + 1.0))
            ge, dge = _gelu_and_grad(proj_ref[rows, LW + c0:LW + c0 + LANE])
            dya = wide[rows, cols]
            dproj_ref[rows, LW + c0:LW + c0 + LANE] = (dya * h_ref[rows, cols] * dge).astype(MXU_DTYPE)
            g_scr[rows, cols] = dya * ge
        _scan_bwd_blocks(a_scr, g_scr, car, t)
        for c0 in range(0, LW, LANE):
            cols = slice(c0, c0 + LANE)
            sp = _softplus(-va_ref[7:8, cols])
            acc_l, acc_r, acc_i = zero_l, zero_l, zero_l
            for r0 in range(0, t, rb):
                rows = pl.ds(r0, rb)
                gv, xc, a, mult = g_scr[rows, cols], xc_scr[rows, cols], a_scr[rows, cols], mult_scr[rows, cols]
                r, ig = gates[rows, cols], gates[rows, LW + c0:LW + c0 + LANE]
                dla = gv * hprev_scr[rows, cols] * a - gv * (ig * xc) * (a * a / mult)
                acc_l = acc_l + _fold8(dla * ((-LRU_C) * r))
                dgr = dla * ((-LRU_C) * sp) * r * (1.0 - r)
                dgi = gv * (mult * xc) * ig * (1.0 - ig)
                acc_r, acc_i = acc_r + _fold8(dgr), acc_i + _fold8(dgi)
                dgates[rows, cols] = dgr.astype(MXU_DTYPE)
                dgates[rows, LW + c0:LW + c0 + LANE] = dgi.astype(MXU_DTYPE)
                dxc_ext[rows, cols] = gv * (mult * ig)
            add_row(ga_ref, 5, c0, acc_r)
            add_row(ga_ref, 6, c0, acc_i)
            ga_ref[7:8, cols] += _colsum(acc_l) * (-_sigmoid(-va_ref[7:8, cols]))
        dwr_ref[...] += _dot_tn(xcb_scr[...], dgates[:, 0:LW])
        dwi_ref[...] += _dot_tn(xcb_scr[...], dgates[:, LW:])
        st_a[...] = _dot_nt(dgates[...], wri_ref[...])
        for c0 in range(0, LW, LANE):
            cols = slice(c0, c0 + LANE)
            acc_b = zero_l
            for r0 in range(0, t, rb):
                rows = pl.ds(r0, rb)
                dxc = dxc_ext[rows, cols] + st_a[rows, cols]
                dxc_ext[rows, cols] = dxc
                acc_b = acc_b + _fold8(dxc)
            add_row(ga_ref, 4, c0, acc_b)
        for k in range(LRU_K - 1):
            rot_d[k] = dxc_ext[pl.ds(k + 1, t), :]
        for c0 in range(0, LW, LANE):
            cols = slice(c0, c0 + LANE)
            acc_w = [zero_l] * LRU_K
            for r0 in range(0, t, rb):
                rows = pl.ds(r0, rb)
                dxc = dxc_ext[rows, cols]
                dxa = va_ref[LRU_K - 1:LRU_K, cols] * dxc
                for k in range(LRU_K):
                    acc_w[k] = acc_w[k] + _fold8(dxc * xa_tap(k, rows, r0, cols))
                    if k < LRU_K - 1:
                        dxa = dxa + va_ref[k:k + 1, cols] * rot_d[LRU_K - 2 - k, rows, cols]
                dproj_ref[rows, cols] = dxa.astype(MXU_DTYPE)
            for k in range(LRU_K):
                add_row(ga_ref, k, c0, acc_w[k])
        dxc_ext[t:t + HALO, :] = dxc_ext[0:HALO, :]

        wide[...] = _dot(dproj_ref[...], win_v[...])

        def out_rows(r0, acc):
            rows = pl.ds(r0, 8)
            du1 = wide[rows, :]
            gx_ref[rows, :] = gx_ref[rows, :] + du1 * (1.0 + mod_ref[1:2, :])
            return acc[0] + du1 * x_ref[rows, :], acc[1] + du1

        acc = _row_blocks(t, 8, out_rows, (zero_d, zero_d))
        dmod_ref[1:2, :] += _colsum(acc[0])
        dmod_ref[0:1, :] += _colsum(acc[1])

        pl.when((b == bl - 1) & (i == nt - 1))(e_finish)

    rev = lambda w: pl.BlockSpec((None, t, w), lambda b, i: (b, nt - 1 - i, 0))

    def halo(rows_, w):
        return pl.BlockSpec((None, rows_, w), lambda b, i: (b, jnp.maximum((nt - 1 - i) * (t // rows_) - 1, 0), 0))

    return pl.pallas_call(
        body, name="mixer_bwd", grid=(bl, nt),
        out_shape=[jax.ShapeDtypeStruct((bl, seq, D), F32), jax.ShapeDtypeStruct((bl, seq, pw), MXU_DTYPE),
                   jax.ShapeDtypeStruct((bl, seq, D), MXU_DTYPE), jax.ShapeDtypeStruct((16, LW), F32),
                   jax.ShapeDtypeStruct((32, CW), F32), jax.ShapeDtypeStruct((LW, LW), F32),
                   jax.ShapeDtypeStruct((LW, LW), F32), jax.ShapeDtypeStruct((8, D), F32),
                   jax.ShapeDtypeStruct((bl, 8, D), F32)] + _exchange_out_shapes(ffn_wgrads),
        in_specs=[rev(D), rev(D), rev(pw), halo(HALO31, pw), rev(LW), halo(HALO, LW), rev(CW), rev(D),
                  pl.BlockSpec((None, 8, D), lambda b, i: (b, 0, 0)), _ANY, _ANY,
                  _full(wri.shape), _full(mavg2.shape), _full(va.shape), _full(vb.shape), _full(ln1.shape)]
        + [_ANY] * n_g,
        out_specs=[rev(D), rev(pw), rev(D), _full((16, LW)), _full((32, CW)), _full((LW, LW)), _full((LW, LW)),
                   _full((8, D)), pl.BlockSpec((None, 8, D), lambda b, i: (b, 0, 0))] + [_ANY] * n_g,
        scratch_shapes=[pltpu.VMEM(win_t.shape, MXU_DTYPE), pltpu.VMEM(wout.shape, MXU_DTYPE),
                        pltpu.VMEM((t + HALO, LW), F32), pltpu.VMEM((t, CW), F32),
                        pltpu.VMEM((t + HALO31, CW), F32), pltpu.VMEM((t + HALO, LW), F32),
                        pltpu.VMEM((7, t + HALO31 - 8, CW), F32),
                        pltpu.VMEM((8, LW), F32), pltpu.SemaphoreType.DMA((2,))] + half_f32
        + [pltpu.VMEM((t, D), F32), pltpu.VMEM((t, 2 * LW), F32), pltpu.VMEM((t, 2 * CW), MXU_DTYPE),
           pltpu.VMEM((t, 2 * CW), MXU_DTYPE), pltpu.VMEM((t, 2 * LW), MXU_DTYPE), pltpu.VMEM((t, LW), MXU_DTYPE),
           pltpu.VMEM((LRU_K - 1, t, LW), F32), pltpu.VMEM((LRU_K - 1, t, LW), F32), pltpu.VMEM((t + HALO, LW), F32)]
        + _comm_sems(n_g),
        compiler_params=_params(2),
    )(x, dx1, proj, proj, h, h, vc, mixed, mods, win_t, wout, wri, mavg2, va, vb, ln1, *ffn_wgrads)


def _weight_grad(a, bmat, tm, name, out_dtype=F32, exchange=(), exchange_half=None, col_phase=None, tok_tile=None):
    ntok, m = a.shape
    n = bmat.shape[1]
    tk = min(tok_tile or WG_TOK_TILE, ntok)
    nk = ntok // tk
    stride, col0 = (1, 0) if col_phase is None else (2, col_phase)
    m = m // stride
    ni = m // tm
    n_e = len(exchange)

    def body(a_ref, b_ref, *rest):
        e_in, o_ref, e_out = rest[:n_e], rest[n_e], rest[n_e + 1:2 * n_e + 1]
        acc_ref = rest[2 * n_e + 1]
        i, k = pl.program_id(0), pl.program_id(1)
        if n_e:
            e_start, e_finish = _exchange_steps(e_in, e_out, *rest[2 * n_e + 2:], half=exchange_half)
            pl.when((i == 0) & (k == 0))(e_start)

        @pl.when(k == 0)
        def _():
            acc_ref[...] = jnp.zeros_like(acc_ref)

        acc_ref[...] += _dot_tn(a_ref[...], b_ref[...])

        @pl.when(k == nk - 1)
        def _():
            o_ref[...] = acc_ref[...].astype(out_dtype)

        if n_e:
            pl.when((i == ni - 1) & (k == nk - 1))(e_finish)

    res = pl.pallas_call(
        body, name=name, grid=(ni, nk),
        out_shape=[jax.ShapeDtypeStruct((m, n), out_dtype)] + _exchange_out_shapes(exchange, exchange_half),
        in_specs=[pl.BlockSpec((tk, tm), lambda i, k: (k, i * stride + col0)), pl.BlockSpec((tk, n), lambda i, k: (k, 0))]
        + [_ANY] * n_e,
        out_specs=[pl.BlockSpec((tm, n), lambda i, k: (i, 0))] + [_ANY] * n_e,
        scratch_shapes=[pltpu.VMEM((tm, n), F32)] + (_comm_sems(n_e) if n_e else []),
        compiler_params=_params(2),
    )(a, bmat, *exchange)
    return res if n_e else res[0]


def _small_grad_layout(bl):
    r1 = 6 * bl
    return {"dmod": 0, "wide": r1, "ln": r1 + 32, "df3": r1 + 36, "g_wr": r1 + 48, "g_wi": r1 + 80, "rows": r1 + 112}


def _pack_small_grads(dmod_a, dmod_b, g_a, g_w31, dln1, dln2, df3, g_wr, g_wi, loss_part):
    bl = dmod_a.shape[0] // 8
    lay = _small_grad_layout(bl)
    ins = [dmod_a, dmod_b, g_a, g_w31, dln1, dln2, df3, g_wr, g_wi, loss_part]

    def body(ma_ref, mb_ref, ga_ref, gw_ref, l1_ref, l2_ref, f3_ref, wr_ref, wi_ref, ls_ref, o_ref):
        o_ref[...] = jnp.zeros_like(o_ref)
        for b in range(bl):
            o_ref[6 * b:6 * b + 3, :] = ma_ref[8 * b:8 * b + 3, :]
            o_ref[6 * b + 3:6 * b + 6, :] = mb_ref[8 * b:8 * b + 3, :]
        wide = lay["wide"]
        o_ref[wide:wide + 32, 0:CW] = gw_ref[...]
        o_ref[wide:wide + 16, CW:CW + LW] = ga_ref[...]
        o_ref[wide + 16:wide + 24, CW:CW + LANE] = ls_ref[...]
        o_ref[lay["ln"]:lay["ln"] + 2, :] = l1_ref[0:2, :]
        o_ref[lay["ln"] + 2:lay["ln"] + 4, :] = l2_ref[0:2, :]
        for j, c0 in enumerate(range(0, DFF, D)):
            w = min(D, DFF - c0)
            o_ref[lay["df3"] + 4 * j:lay["df3"] + 4 * j + 4, 0:w] = f3_ref[0:4, c0:c0 + w]
        o_ref[lay["g_wr"]:lay["g_wr"] + 32, :] = wr_ref[...]
        o_ref[lay["g_wi"]:lay["g_wi"] + 32, :] = wi_ref[...]

    return pl.pallas_call(
        body, name="pack_small_grads", out_shape=jax.ShapeDtypeStruct((lay["rows"], D), F32),
        in_specs=[_full(a.shape) for a in ins], out_specs=_full((lay["rows"], D)),
        compiler_params=pltpu.CompilerParams(vmem_limit_bytes=VMEM_LIMIT),
    )(*ins)


def _sum_small_grads(packed_all, bl):
    lay = _small_grad_layout(bl)
    shapes = {"lru_conv_w": (LRU_K, LW), "lru_conv_b": (1, LW), "lru_b_r": (1, LW), "lru_b_i": (1, LW),
              "lru_lambda": (1, LW), "conv_norm_g": (1, CW), "conv_norm_b": (1, CW), "conv_b": (1, CW),
              "conv_w": (CONV_K, CW), "ln1_g": (1, D), "ln1_b": (1, D), "ln2_g": (1, D), "ln2_b": (1, D),
              "ffn_conv_w": (FFN_K, DFF), "ffn_conv_b": (1, DFF), "lru_w_r": (32, D), "lru_w_i": (32, D),
              "loss": (8, LANE)}
    names = list(shapes)

    def body(p_ref, *outs):
        o = dict(zip(names, outs))

        def part(r0, nr, c0=0, nc=D):
            acc = p_ref[0, r0:r0 + nr, c0:c0 + nc]
            for j in range(1, NDEV):
                acc = acc + p_ref[j, r0:r0 + nr, c0:c0 + nc]
            return acc

        wide = lay["wide"]
        ga = part(wide, 16, CW, LW)
        o["lru_conv_w"][...] = ga[0:LRU_K]
        for row, k in ((4, "lru_conv_b"), (5, "lru_b_r"), (6, "lru_b_i"), (7, "lru_lambda"), (8, "conv_norm_g"),
                       (9, "conv_norm_b"), (10, "conv_b")):
            o[k][...] = ga[row:row + 1]
        o["conv_w"][...] = part(wide, 32, 0, CW)[0:CONV_K]
        o["loss"][...] = part(wide + 16, 8, CW, LANE)
        ln = part(lay["ln"], 4)
        o["ln1_g"][...], o["ln1_b"][...], o["ln2_g"][...], o["ln2_b"][...] = ln[0:1], ln[1:2], ln[2:3], ln[3:4]
        for j, c0 in enumerate(range(0, DFF, D)):
            w = min(D, DFF - c0)
            f3 = part(lay["df3"] + 4 * j, 4, 0, w)
            o["ffn_conv_w"][:, c0:c0 + w] = f3[0:FFN_K]
            o["ffn_conv_b"][:, c0:c0 + w] = f3[FFN_K:FFN_K + 1]
        o["lru_w_r"][...] = part(lay["g_wr"], 32)
        o["lru_w_i"][...] = part(lay["g_wi"], 32)

    res = pl.pallas_call(
        body, name="sum_small_grads", out_shape=[jax.ShapeDtypeStruct(shapes[k], F32) for k in names],
        in_specs=[_full(packed_all.shape)], out_specs=[_full(shapes[k]) for k in names],
        compiler_params=pltpu.CompilerParams(vmem_limit_bytes=VMEM_LIMIT),
    )(packed_all)
    return dict(zip(names, res))


def _adamw_update(w_ref, g_ref, m_ref, v_ref, d_ref, nm_ref, nv_ref):
    gv = g_ref[...]
    nm = ADAM_B1 * m_ref[...] + (1.0 - ADAM_B1) * gv
    nv = ADAM_B2 * v_ref[...] + (1.0 - ADAM_B2) * (gv * gv)
    m_hat = nm / (1.0 - ADAM_B1 ** ADAM_STEP)
    v_hat = nv / (1.0 - ADAM_B2 ** ADAM_STEP)
    d_ref[...] = -ADAM_LR * (m_hat / (jnp.sqrt(v_hat) + ADAM_EPS) + ADAM_WD * w_ref[...])
    nm_ref[...] = nm
    nv_ref[...] = nv


def _sum_adamw(slots, w, m, v, name):
    _, r, cdim = slots.shape
    tr = next((cand for cand in (64, 32) if r % cand == 0), r)

    def body(s_ref, w_ref, m_ref, v_ref, g_ref, d_ref, nm_ref, nv_ref):
        acc = s_ref[0].astype(F32)
        for j in range(1, NDEV):
            acc = acc + s_ref[j].astype(F32)
        g_ref[...] = acc
        _adamw_update(w_ref, g_ref, m_ref, v_ref, d_ref, nm_ref, nv_ref)

    spec = pl.BlockSpec((tr, cdim), lambda i: (i, 0))
    return pl.pallas_call(
        body, name=name, grid=(r // tr,), out_shape=[jax.ShapeDtypeStruct((r, cdim), F32)] * 4,
        in_specs=[pl.BlockSpec((NDEV, tr, cdim), lambda i: (0, i, 0))] + [spec] * 3, out_specs=[spec] * 4,
        compiler_params=_params(1),
    )(slots, w, m, v)


def _sum_adamw_transposed(slots_c0, slots_c1, w, m, v, name):
    _, r, cdim = slots_c0.shape
    tc = 256

    def body(s0_ref, s1_ref, w_ref, m_ref, v_ref, g_ref, d_ref, nm_ref, nv_ref):
        def total(s_ref):
            acc = s_ref[0].astype(F32)
            for j in range(1, NDEV):
                acc = acc + s_ref[j].astype(F32)
            g_ref[...] = acc.T

        c = lax.axis_index("c")
        pl.when(c == 0)(functools.partial(total, s0_ref))
        pl.when(c == 1)(functools.partial(total, s1_ref))
        _adamw_update(w_ref, g_ref, m_ref, v_ref, d_ref, nm_ref, nv_ref)

    slot_spec = pl.BlockSpec((NDEV, r, tc), lambda j: (0, 0, j))
    spec = pl.BlockSpec((tc, r), lambda j: (j, 0))
    return pl.pallas_call(
        body, name=name, grid=(cdim // tc,), out_shape=[jax.ShapeDtypeStruct((cdim, r), F32)] * 4,
        in_specs=[slot_spec, slot_spec] + [spec] * 3, out_specs=[spec] * 4,
        compiler_params=_params(1),
    )(slots_c0, slots_c1, w, m, v)


def _adamw_many(ws, gs, ms, vs, name):
    n = len(ws)

    def body(*refs):
        ins, outs = refs[:4 * n], refs[4 * n:]
        for k in range(n):
            _adamw_update(ins[k], ins[n + k], ins[2 * n + k], ins[3 * n + k], outs[k], outs[n + k], outs[2 * n + k])

    specs = [_full(w.shape) for w in ws]
    res = pl.pallas_call(
        body, name=name, out_shape=[jax.ShapeDtypeStruct(w.shape, F32) for w in ws] * 3,
        in_specs=specs * 4, out_specs=specs * 3,
        compiler_params=pltpu.CompilerParams(vmem_limit_bytes=VMEM_LIMIT),
    )(*ws, *gs, *ms, *vs)
    return res[:n], res[n:2 * n], res[2 * n:]


def _adamw(w, g, m, v, name):
    r, cdim = w.shape
    tr = next((cand for cand in (256, 128) if r % cand == 0), r)
    body = functools.partial(_adamw_update)
    spec = pl.BlockSpec((tr, cdim), lambda i: (i, 0))
    return pl.pallas_call(
        body, name=name, grid=(r // tr,), out_shape=[jax.ShapeDtypeStruct((r, cdim), F32)] * 3,
        in_specs=[spec] * 4, out_specs=[spec] * 3, compiler_params=_params(1),
    )(w, g, m, v)


def _pack(arrs, width=D, row_mult=8):
    parts = []
    for a in arrs:
        flat = a.reshape(-1)
        pad = (-flat.shape[0]) % width
        parts.append(jnp.pad(flat, (0, pad)))
    flat = jnp.concatenate(parts)
    pad = (-flat.shape[0]) % (width * row_mult)
    return jnp.pad(flat, (0, pad)).reshape(-1, width)


def _unpack_gathered(buf, shapes, width=D):
    out, row = [], 0
    for shp in shapes:
        size = math.prod(shp)
        nrow = -(-size // width)
        out.append(buf[:, row:row + nrow].reshape(NDEV, -1)[:, :size].reshape((NDEV,) + tuple(shp)))
        row += nrow
    return out


def _block_diag(w):
    hn, dh, _ = w.shape
    eye = jnp.eye(hn, dtype=w.dtype)
    return (w[:, :, None, :] * eye[:, None, :, None]).reshape(hn * dh, hn * dh)


def _diag_blocks(wfull, hn):
    dh = wfull.shape[0] // hn
    keep = np.eye(hn, dtype=bool)[:, None, :, None]
    return jnp.where(keep, wfull.reshape(hn, dh, hn, dh), 0.0).sum(axis=2)


def _pad_rows(a, rows):
    return jnp.pad(a, ((0, rows - a.shape[0]), (0, 0)))


def kernel(x, c, w_ada, b_ada, w_in, lru_conv_w, lru_conv_b, lru_w_r, lru_b_r, lru_w_i, lru_b_i, lru_lambda, conv_w, conv_b, conv_norm_g, conv_norm_b, w_out, ln1_g, ln1_b, ffn_w_up, ffn_conv_w, ffn_conv_b, ffn_w_down, ln2_g, ln2_b, loss_target, m_w_ada, m_b_ada, m_w_in, m_lru_conv_w, m_lru_conv_b, m_lru_w_r, m_lru_b_r, m_lru_w_i, m_lru_b_i, m_lru_lambda, m_conv_w, m_conv_b, m_conv_norm_g, m_conv_norm_b, m_w_out, m_ln1_g, m_ln1_b, m_ffn_w_up, m_ffn_conv_w, m_ffn_conv_b, m_ffn_w_down, m_ln2_g, m_ln2_b, v_w_ada, v_b_ada, v_w_in, v_lru_conv_w, v_lru_conv_b, v_lru_w_r, v_lru_b_r, v_lru_w_i, v_lru_b_i, v_lru_lambda, v_conv_w, v_conv_b, v_conv_norm_g, v_conv_norm_b, v_w_out, v_ln1_g, v_ln1_b, v_ffn_w_up, v_ffn_conv_w, v_ffn_conv_b, v_ffn_w_down, v_ln2_g, v_ln2_b):
    weights = dict(w_ada=w_ada, b_ada=b_ada, w_in=w_in, lru_conv_w=lru_conv_w, lru_conv_b=lru_conv_b, lru_w_r=lru_w_r,
                   lru_b_r=lru_b_r, lru_w_i=lru_w_i, lru_b_i=lru_b_i, lru_lambda=lru_lambda, conv_w=conv_w, conv_b=conv_b,
                   conv_norm_g=conv_norm_g, conv_norm_b=conv_norm_b, w_out=w_out, ln1_g=ln1_g, ln1_b=ln1_b,
                   ffn_w_up=ffn_w_up, ffn_conv_w=ffn_conv_w, ffn_conv_b=ffn_conv_b, ffn_w_down=ffn_w_down, ln2_g=ln2_g,
                   ln2_b=ln2_b)
    mom_m = dict(w_ada=m_w_ada, b_ada=m_b_ada, w_in=m_w_in, lru_conv_w=m_lru_conv_w, lru_conv_b=m_lru_conv_b,
                 lru_w_r=m_lru_w_r, lru_b_r=m_lru_b_r, lru_w_i=m_lru_w_i, lru_b_i=m_lru_b_i, lru_lambda=m_lru_lambda,
                 conv_w=m_conv_w, conv_b=m_conv_b, conv_norm_g=m_conv_norm_g, conv_norm_b=m_conv_norm_b, w_out=m_w_out,
                 ln1_g=m_ln1_g, ln1_b=m_ln1_b, ffn_w_up=m_ffn_w_up, ffn_conv_w=m_ffn_conv_w, ffn_conv_b=m_ffn_conv_b,
                 ffn_w_down=m_ffn_w_down, ln2_g=m_ln2_g, ln2_b=m_ln2_b)
    mom_v = dict(w_ada=v_w_ada, b_ada=v_b_ada, w_in=v_w_in, lru_conv_w=v_lru_conv_w, lru_conv_b=v_lru_conv_b,
                 lru_w_r=v_lru_w_r, lru_b_r=v_lru_b_r, lru_w_i=v_lru_w_i, lru_b_i=v_lru_b_i, lru_lambda=v_lru_lambda,
                 conv_w=v_conv_w, conv_b=v_conv_b, conv_norm_g=v_conv_norm_g, conv_norm_b=v_conv_norm_b, w_out=v_w_out,
                 ln1_g=v_ln1_g, ln1_b=v_ln1_b, ffn_w_up=v_ffn_w_up, ffn_conv_w=v_ffn_conv_w, ffn_conv_b=v_ffn_conv_b,
                 ffn_w_down=v_ffn_w_down, ln2_g=v_ln2_g, ln2_b=v_ln2_b)
    names = list(weights)
    bl, seq, _ = x.shape
    ntok = bl * seq
    me = 4 * lax.axis_index("x") + 2 * lax.axis_index("y") + lax.axis_index("c")

    small_shapes = [(bl, D), (LRU_K, LW // NDEV), (CONV_K, CW // NDEV), (FFN_K, DFF // NDEV)]
    small = _pack([c, lru_conv_w[0], conv_w[0], ffn_conv_w[0]], width=128)
    n_small = small.shape[0]
    small_all, win_t, wout_b = _all_gather(
        [small, w_in[0].T.astype(MXU_DTYPE), w_out[0].astype(MXU_DTYPE)], "gather_small_and_mixer_weights")
    small_all = small_all.reshape(NDEV, n_small, 128)
    c_parts, k4_parts, k31_parts, k3_parts = _unpack_gathered(small_all, small_shapes, width=128)
    c_all = c_parts.reshape(NDEV * bl, D)
    lru_conv_w_f = k4_parts.transpose(1, 0, 2).reshape(LRU_K, LW)
    conv_w_f = k31_parts.transpose(1, 0, 2).reshape(CONV_K, CW)
    ffn_conv_w_f = k3_parts.transpose(1, 0, 2).reshape(FFN_K, DFF)

    ncol = w_ada.shape[2]
    b_ada_loc = lax.dynamic_slice(b_ada, (0, me * ncol), (1, ncol))
    mod_cols = _ada_fwd(c_all, w_ada[0], b_ada_loc)
    (mod_all,) = _all_gather([mod_cols], "gather_mod")
    mod_all = mod_all.reshape(NDEV, NDEV * bl, ncol)
    mod_mine = lax.dynamic_slice(mod_all, (0, me * bl, 0), (NDEV, bl, ncol))
    mods = mod_mine.transpose(1, 0, 2).reshape(bl, 6, D)
    mods = jnp.pad(mods, ((0, 0), (0, 2), (0, 0)))

    ffn_shards = [ffn_w_up[0].T.astype(MXU_DTYPE), ffn_w_down[0].astype(MXU_DTYPE)]

    wr_bd = _block_diag(lru_w_r[0]).astype(MXU_DTYPE)
    wi_bd = _block_diag(lru_w_i[0]).astype(MXU_DTYPE)
    mavg_np = np.kron(np.eye(CW // HEAD, dtype=np.float32), np.full((HEAD, HEAD), 1.0 / HEAD, np.float32))
    mavg = jnp.asarray(mavg_np, MXU_DTYPE)
    mavg2 = jnp.asarray(np.concatenate([mavg_np, mavg_np], axis=0), MXU_DTYPE)
    wri = jnp.concatenate([wr_bd, wi_bd], axis=1)
    va = jnp.concatenate([lru_conv_w_f, lru_conv_b, lru_b_r, lru_b_i, lru_lambda], axis=0)
    vb = _pad_rows(jnp.concatenate([conv_w_f, conv_b, conv_norm_g, conv_norm_b], axis=0), 40)
    ln1 = _pad_rows(jnp.concatenate([ln1_g, ln1_b], axis=0), 8)
    ln2 = _pad_rows(jnp.concatenate([ln2_g, ln2_b], axis=0), 8)
    f3 = _pad_rows(jnp.concatenate([ffn_conv_w_f, ffn_conv_b], axis=0), 8)

    proj, h, vc, y_b, mixed, x1, u1_b, wup_t, wdown_b = _mixer_fwd(
        x, mods, win_t, wout_b, wri, mavg2, va, vb, ln1, ffn_shards)
    hh, z_b, u2_b, y2, loss_part = _ffn_fwd(x1, mods, loss_target, wup_t, wdown_b, f3, ln2)

    dx1, dy2_b, dhh_b, dln2, df3, dmod_b = _ffn_bwd(x1, y2, loss_target, hh, mods, wup_t, wdown_b, f3, ln2)
    g_down = _weight_grad(z_b.reshape(ntok, DFF), dy2_b.reshape(ntok, D), FF_CHUNK, "wgrad_down", out_dtype=WIRE_DTYPE)
    g_up_t = _weight_grad(dhh_b.reshape(ntok, 2 * DFF), u2_b.reshape(ntok, D), FF_CHUNK, "wgrad_up",
                          out_dtype=WIRE_DTYPE)
    grad_x, dproj_b, dm_b, g_a, g_w31, g_wr, g_wi, dln1, dmod_a, s_up, s_down = _mixer_bwd(
        x, dx1, proj, h, vc, mixed, mods, win_t, wout_b, wri, mavg2, va, vb, ln1, [g_up_t, g_down])
    dproj_2d, u1_2d = dproj_b.reshape(ntok, 2 * LW + 2 * CW), u1_b.reshape(ntok, D)
    blk_in = (2 * LW + 2 * CW) // NDEV
    g_out_b = _weight_grad(y_b.reshape(ntok, D), dm_b.reshape(ntok, D), D, "wgrad_out", out_dtype=WIRE_DTYPE)
    g_in_c0, s_out = _weight_grad(dproj_2d, u1_2d, blk_in, "wgrad_in_c0", out_dtype=WIRE_DTYPE, exchange=[g_out_b],
                                  col_phase=0, tok_tile=2 * WG_TOK_TILE)
    g_in_c1, s_in_c0 = _weight_grad(dproj_2d, u1_2d, blk_in, "wgrad_in_c1", out_dtype=WIRE_DTYPE, exchange=[g_in_c0],
                                    exchange_half=0, col_phase=1, tok_tile=2 * WG_TOK_TILE)

    heads = LW // HEAD
    packed = _pack_small_grads(
        dmod_a.reshape(bl * 8, D), dmod_b.reshape(bl * 8, D), g_a, g_w31, dln1, dln2, df3,
        _diag_blocks(g_wr, heads).reshape(32, D), _diag_blocks(g_wi, heads).reshape(32, D), loss_part)
    n_rows = packed.shape[0]
    (s_in_c1,), (packed_all,) = _exchange_and_gather(
        [g_in_c1], [packed], "exchange_w_in_grads_gather_small_grads", half=1)
    packed_all = packed_all.reshape(NDEV, n_rows, D)
    full = _sum_small_grads(packed_all, bl)
    loss = jnp.sum(full.pop("loss")) * (0.5 / D)
    dmod_all = packed_all[:, 0:6 * bl].reshape(NDEV * bl, 6 * D)

    grads, delta, new_m, new_v = {}, {}, {}, {}
    for k, slots in (("w_out", s_out), ("ffn_w_down", s_down)):
        g_, d_, m_, v_ = _sum_adamw(slots, weights[k][0], mom_m[k][0], mom_v[k][0], "sum_adamw_" + k)
        grads[k], delta[k], new_m[k], new_v[k] = g_[None], d_[None], m_[None], v_[None]
    g_w_ada, g_b_ada = _ada_bwd(c_all, dmod_all, lax.dynamic_slice(dmod_all, (0, me * ncol), (NDEV * bl, ncol)))
    grads["w_ada"], grads["b_ada"] = g_w_ada[None], g_b_ada
    k = "w_in"
    g_, d_, m_, v_ = _sum_adamw_transposed(s_in_c0, s_in_c1, weights[k][0], mom_m[k][0], mom_v[k][0], "sum_adamw_" + k)
    grads[k], delta[k], new_m[k], new_v[k] = g_[None], d_[None], m_[None], v_[None]
    k = "ffn_w_up"
    g_, d_, m_, v_ = _sum_adamw(s_up, weights[k][0].T, mom_m[k][0].T, mom_v[k][0].T, "sum_adamw_" + k)
    grads[k], delta[k], new_m[k], new_v[k] = g_.T[None], d_.T[None], m_.T[None], v_.T[None]
    for k, gk in full.items():
        if k in ("lru_conv_w", "conv_w", "ffn_conv_w"):
            nloc = gk.shape[1] // NDEV
            gk = lax.dynamic_slice(gk, (0, me * nloc), (gk.shape[0], nloc))
        grads[k] = gk.reshape(weights[k].shape)

    big = ("w_ada", "w_in", "w_out", "ffn_w_up", "ffn_w_down")
    for k in ("w_ada",):
        d_, m_, v_ = _adamw(weights[k][0], grads[k][0], mom_m[k][0], mom_v[k][0], "adamw_" + k)
        delta[k], new_m[k], new_v[k] = d_[None], m_[None], v_[None]
    small_names = [k for k in names if k not in big]
    d_, m_, v_ = _adamw_many([weights[k] for k in small_names], [grads[k] for k in small_names],
                             [mom_m[k] for k in small_names], [mom_v[k] for k in small_names], "adamw_small")
    for k, dk, mk, vk in zip(small_names, d_, m_, v_):
        delta[k], new_m[k], new_v[k] = dk, mk, vk

    return (loss, grad_x, *[grads[k] for k in names], *[delta[k] for k in names], *[new_m[k] for k in names],
            *[new_v[k] for k in names])
```

```python
import functools
import math

import jax
import jax.numpy as jnp
import numpy as np
from jax import lax
from jax.experimental import pallas as pl
from jax.experimental.pallas import tpu as pltpu

NDEV = 8
D = 1024
LW = 512
CW = 512
HEAD = 64
DFF = 2816
FF_CHUNK = 1408
LRU_K = 4
CONV_K = 31
FFN_K = 3
LRU_C = 8.0
ALPHA = (2 * 1) ** 0.25
LN_EPS = 1e-5
ADAM_LR = 0.001
ADAM_B1 = 0.9
ADAM_B2 = 0.999
ADAM_EPS = 1e-08
ADAM_WD = 0.01
ADAM_STEP = 10

MXU_DTYPE = jnp.bfloat16
WIRE_DTYPE = jnp.bfloat16
TOK_TILE = 256
WG_TOK_TILE = 2048
VMEM_LIMIT = 60 * 1024 * 1024
HALO31 = 32
HALO = 8

F32 = jnp.float32
MESH = pl.DeviceIdType.MESH


def _sigmoid(x):
    return 0.5 * jnp.tanh(0.5 * x) + 0.5


def _shifted_rows(ext_ref, rot_ref, n_rows):
    span = rot_ref.shape[1]
    for r in range(1, 8):
        rot_ref[r - 1] = ext_ref[pl.ds(r, span), :]

    def window(o, r0=0, cols=slice(None)):
        q, r = divmod(o, 8)
        if r == 0:
            return ext_ref[pl.ds(8 * q + r0, n_rows), cols]
        return rot_ref[r - 1, pl.ds(8 * q + r0, n_rows), cols]

    return window


def _store_hilo(hilo_ref, rows, c0, v):
    half = hilo_ref.shape[1] // 2
    hi = v.astype(hilo_ref.dtype)
    hilo_ref[rows, c0:c0 + v.shape[1]] = hi
    hilo_ref[rows, half + c0:half + c0 + v.shape[1]] = (v - hi.astype(F32)).astype(hilo_ref.dtype)


def _scan_bwd_blocks(a_ref, g_ref, car, n_rows):
    row = lax.broadcasted_iota(jnp.int32, (8, LANE), 0)
    for c0 in range(0, a_ref.shape[1], LANE):
        cols = slice(c0, c0 + LANE)
        a_next = jnp.broadcast_to(car[0:1, cols], (8, LANE))
        g_next = jnp.broadcast_to(car[1:2, cols], (8, LANE))
        for r0 in range(n_rows - 8, -1, -8):
            a_blk, g = a_ref[r0:r0 + 8, cols], g_ref[r0:r0 + 8, cols]
            bm = jnp.where(row < 7, pltpu.roll(a_blk, 7, 0), a_next)
            for d in (1, 2, 4):
                keep = row < 8 - d
                g = g + bm * jnp.where(keep, pltpu.roll(g, 8 - d, 0), 0.0)
                bm = bm * jnp.where(keep, pltpu.roll(bm, 8 - d, 0), 1.0)
            g = g + bm * g_next
            g_ref[r0:r0 + 8, cols] = g
            a_next = jnp.broadcast_to(a_blk[0:1, :], (8, LANE))
            g_next = jnp.broadcast_to(g[0:1, :], (8, LANE))
        car[0:1, cols] = a_next[0:1, :]
        car[1:2, cols] = g_next[0:1, :]


def _scan_fwd_blocks(a_ref, u_ref, hcar, n_rows):
    row = lax.broadcasted_iota(jnp.int32, (8, LANE), 0)
    for c0 in range(0, a_ref.shape[1], LANE):
        cols = slice(c0, c0 + LANE)
        h_prev = jnp.broadcast_to(hcar[0:1, cols], (8, LANE))
        for r0 in range(0, n_rows, 8):
            a, u = a_ref[r0:r0 + 8, cols], u_ref[r0:r0 + 8, cols]
            for d in (1, 2, 4):
                keep = row >= d
                u = a * jnp.where(keep, pltpu.roll(u, d, 0), 0.0) + u
                a = a * jnp.where(keep, pltpu.roll(a, d, 0), 1.0)
            u = u + a * h_prev
            u_ref[r0:r0 + 8, cols] = u
            h_prev = jnp.broadcast_to(u[7:8, :], (8, LANE))
        hcar[0:1, cols] = h_prev[0:1, :]


def _dot(a, b):
    return jnp.dot(a, b, preferred_element_type=F32)


def _dot_nt(a, b):
    return lax.dot_general(a, b, (((1,), (1,)), ((), ())), preferred_element_type=F32)


def _dot_tn(a, b):
    return lax.dot_general(a, b, (((0,), (0,)), ((), ())), preferred_element_type=F32)


def _colsum(v):
    return jnp.sum(v, axis=0, keepdims=True)


def _rowmean(v):
    return jnp.mean(v, axis=-1, keepdims=True)


_GELU_C0 = math.sqrt(2.0 / math.pi)
_GELU_C1 = 0.044715


def _gelu_and_grad(x):
    x2 = x * x
    th = jnp.tanh(_GELU_C0 * (x + _GELU_C1 * x * x2))
    ge = 0.5 * x * (1.0 + th)
    dge = 0.5 * (1.0 + th) + 0.5 * x * (1.0 - th * th) * (_GELU_C0 * (1.0 + 3.0 * _GELU_C1 * x2))
    return ge, dge


def _softplus(x):
    return jnp.maximum(x, 0.0) + jnp.log1p(jnp.exp(-jnp.abs(x)))


def _layer_norm_stats(r):
    mu = _rowmean(r)
    dl = r - mu
    var = _rowmean(dl * dl)
    rstd = lax.rsqrt(var + LN_EPS)
    return dl * rstd, rstd


ROW_BLK = 32
LANE = 128


def _row_blocks(n_rows, rb, body, init):
    carry = init
    for j in range(n_rows // rb):
        carry = body(j * rb, carry)
    return carry


def _fold8(v):
    r, c = v.shape
    return v if r == 8 else v.reshape(r // 8, 8, c).sum(axis=0)


def _cast_rows(src_ref, dst_ref, n_rows):
    rb = min(16, n_rows)

    def rows(r0, carry):
        dst_ref[pl.ds(r0, rb), :] = src_ref[pl.ds(r0, rb), :].astype(dst_ref.dtype)
        return carry

    _row_blocks(n_rows, rb, rows, ())


def _full(shape):
    nd = len(shape)
    return pl.BlockSpec(shape, lambda *_: (0,) * nd)


_ANY = pl.BlockSpec(memory_space=pl.ANY)


def _params(n_grid):
    return pltpu.CompilerParams(dimension_semantics=("arbitrary",) * n_grid, vmem_limit_bytes=VMEM_LIMIT)


def _my_place():
    return lax.axis_index("x"), lax.axis_index("y"), lax.axis_index("c")


def _all_gather(arrays, name):
    n_arr = len(arrays)

    def body(*refs):
        start, forward, finish = _gather_steps(refs[:n_arr], refs[n_arr:2 * n_arr], *refs[2 * n_arr:])
        start()
        forward()
        finish()

    return pl.pallas_call(
        body, name=name, out_shape=_gather_out_shapes(arrays),
        in_specs=[_ANY] * n_arr, out_specs=[_ANY] * n_arr, scratch_shapes=_comm_sems(n_arr),
    )(*arrays)


def _gather_out_shapes(arrays):
    return [jax.ShapeDtypeStruct((NDEV * a.shape[0], a.shape[1]), a.dtype) for a in arrays]


def _comm_sems(n_arr):
    return [pltpu.SemaphoreType.DMA((n_arr, 7)), pltpu.SemaphoreType.DMA((n_arr, 7)), pltpu.SemaphoreType.DMA((n_arr,))]


def _gather_steps(x_refs, out_refs, send_sems, recv_sems, local_sems):
    n_arr = len(x_refs)
    x, y, c = _my_place()
    me, sibling = (x, y, c), (x, y, 1 - c)
    chips = [(1 - x, y), (x, 1 - y), (1 - x, 1 - y)]

    def rows(k, px, py, pc):
        m = x_refs[k].shape[0]
        return out_refs[k].at[pl.ds((4 * px + 2 * py + pc) * m, m), :]

    def copy(k, s, block, to, src=None):
        return pltpu.make_async_remote_copy(
            src_ref=rows(k, *block) if src is None else src, dst_ref=rows(k, *block),
            send_sem=send_sems.at[k, s], recv_sem=recv_sems.at[k, s], device_id=to, device_id_type=MESH)

    def mine():
        return [pltpu.make_async_copy(x_refs[k], rows(k, *me), local_sems.at[k]) for k in range(n_arr)]

    def first():
        cps = []
        for k in range(n_arr):
            cps.append(copy(k, 0, me, sibling, src=x_refs[k]))
            cps += [copy(k, 1 + j, me, (*chip, c), src=x_refs[k]) for j, chip in enumerate(chips)]
        return cps

    def passed():
        return [copy(k, 4 + j, (*chip, c), sibling) for k in range(n_arr) for j, chip in enumerate(chips)]

    def start():
        for cp in mine() + first():
            cp.start()

    def forward():
        fwd = passed()
        for k in range(n_arr):
            for j, chip in enumerate(chips):
                copy(k, 1 + j, (*chip, c), me).wait_recv()
                fwd[3 * k + j].start()

    def finish():
        for k in range(n_arr):
            copy(k, 0, sibling, me).wait_recv()
            for j, chip in enumerate(chips):
                copy(k, 4 + j, (*chip, 1 - c), me).wait_recv()
        for cp in first() + passed():
            cp.wait_send()
        for cp in mine():
            cp.wait()

    return start, forward, finish


def _exchange_and_gather(ex_arrays, ga_arrays, name, half=None):
    n_e, n_g = len(ex_arrays), len(ga_arrays)

    def body(*refs):
        e_in, g_in, refs = refs[:n_e], refs[n_e:n_e + n_g], refs[n_e + n_g:]
        e_out, g_out, sems = refs[:n_e], refs[n_e:n_e + n_g], refs[n_e + n_g:]
        e_start, e_finish = _exchange_steps(e_in, e_out, *sems[:3], half=half)
        g_start, g_forward, g_finish = _gather_steps(g_in, g_out, *sems[3:])
        g_start()
        e_start()
        g_forward()
        g_finish()
        e_finish()

    res = pl.pallas_call(
        body, name=name, out_shape=_exchange_out_shapes(ex_arrays, half) + _gather_out_shapes(ga_arrays),
        in_specs=[_ANY] * (n_e + n_g), out_specs=[_ANY] * (n_e + n_g),
        scratch_shapes=_comm_sems(n_e) + _comm_sems(n_g),
    )(*ex_arrays, *ga_arrays)
    return res[:n_e], res[n_e:]


def _exchange_out_shapes(arrays, half=None):
    n_blocks = NDEV if half is None else NDEV // 2
    return [jax.ShapeDtypeStruct((NDEV, a.shape[0] // n_blocks, a.shape[1]), a.dtype) for a in arrays]


def _exchange_steps(g_refs, out_refs, send_sems, recv_sems, local_sems, half=None):
    n_arr = len(g_refs)
    x, y, c = _my_place()
    me = 4 * x + 2 * y + c
    receives = True if half is None else c == half

    def when(cond, fn):
        if cond is True:
            fn()
        else:
            pl.when(cond)(fn)

    def sends(rel):
        return True if half is None else (c ^ (rel & 1)) == half

    def copy(k, rel):
        px, py, pc = x ^ ((rel >> 2) & 1), y ^ ((rel >> 1) & 1), c ^ (rel & 1)
        m = out_refs[k].shape[1]
        blk = 4 * px + 2 * py + pc if half is None else 2 * px + py
        return pltpu.make_async_remote_copy(
            src_ref=g_refs[k].at[pl.ds(blk * m, m), :], dst_ref=out_refs[k].at[me],
            send_sem=send_sems.at[k, rel - 1], recv_sem=recv_sems.at[k, rel - 1],
            device_id=(px, py, pc), device_id_type=MESH)

    def local(k):
        m = out_refs[k].shape[1]
        blk = me if half is None else 2 * x + y
        return pltpu.make_async_copy(g_refs[k].at[pl.ds(blk * m, m), :], out_refs[k].at[me], local_sems.at[k])

    def start():
        for k in range(n_arr):
            when(receives, local(k).start)
            for rel in range(1, NDEV):
                when(sends(rel), copy(k, rel).start)

    def finish():
        for k in range(n_arr):
            for rel in range(1, NDEV):
                when(sends(rel), copy(k, rel).wait_send)
                when(receives, copy(k, rel).wait_recv)
            when(receives, local(k).wait)

    return start, finish


def _ada_fwd(c_all, w_ada_loc, b_ada_loc):
    def body(c_ref, w_ref, b_ref, o_ref):
        cv = c_ref[...]
        ca = (cv * _sigmoid(cv)).astype(MXU_DTYPE)
        o_ref[...] = _dot(ca, w_ref[...].astype(MXU_DTYPE)) + b_ref[...]

    return pl.pallas_call(
        body, name="ada_fwd", out_shape=jax.ShapeDtypeStruct((c_all.shape[0], w_ada_loc.shape[1]), F32),
        in_specs=[_full(c_all.shape), _full(w_ada_loc.shape), _full(b_ada_loc.shape)],
        out_specs=_full((c_all.shape[0], w_ada_loc.shape[1])),
        compiler_params=pltpu.CompilerParams(vmem_limit_bytes=VMEM_LIMIT),
    )(c_all, w_ada_loc, b_ada_loc)


def _ada_bwd(c_all, dmod_all, dmod_cols):
    def body(c_ref, da_ref, d_ref, o_ref, b_ref):
        cv = c_ref[...]
        ca = (cv * _sigmoid(cv)).astype(MXU_DTYPE)
        o_ref[...] = _dot_tn(ca, d_ref[...].astype(MXU_DTYPE))
        b_ref[...] = _colsum(da_ref[...])

    return pl.pallas_call(
        body, name="ada_bwd",
        out_shape=[jax.ShapeDtypeStruct((c_all.shape[1], dmod_cols.shape[1]), F32),
                   jax.ShapeDtypeStruct((1, dmod_all.shape[1]), F32)],
        in_specs=[_full(c_all.shape), _full(dmod_all.shape), _full(dmod_cols.shape)],
        out_specs=[_full((c_all.shape[1], dmod_cols.shape[1])), _full((1, dmod_all.shape[1]))],
        compiler_params=pltpu.CompilerParams(vmem_limit_bytes=VMEM_LIMIT),
    )(c_all, dmod_all, dmod_cols)


def _mixer_fwd(x, mods, win_t, wout, wri, mavg2, va, vb, ln1, ffn_shards):
    bl, seq, _ = x.shape
    t = min(TOK_TILE, seq)
    nt = seq // t
    n_g = len(ffn_shards)
    rb = min(ROW_BLK, t)

    def body(x_ref, mod_ref, win_hbm, wout_hbm, wri_ref, mavg_ref, va_ref, vb_ref, ln1_ref, *rest):
        shard_refs, rest = rest[:n_g], rest[n_g:]
        proj_ref, h_ref, vc_ref, y_ref, mixed_ref, x1_ref, u1_ref = rest[:7]
        gathered_refs, rest = rest[7:7 + n_g], rest[7 + n_g:]
        (win_v, wout_v, xa_ext, vg_ext, rot_a, hcar, sems, rot_c, xc_scr, a_scr, st_a, st_b, xcb_scr, gates, hilo,
         g_send, g_recv, g_local) = rest
        b, i = pl.program_id(0), pl.program_id(1)
        step = b * nt + i
        g_start, g_forward, g_finish = _gather_steps(shard_refs, gathered_refs, g_send, g_recv, g_local)

        @pl.when(step == 0)
        def _():
            g_start()
            cps = [pltpu.make_async_copy(win_hbm, win_v, sems.at[0]), pltpu.make_async_copy(wout_hbm, wout_v, sems.at[1])]
            for cp in cps:
                cp.start()
            for cp in cps:
                cp.wait()

        pl.when(step == (bl * nt) // 2)(g_forward)

        @pl.when(i == 0)
        def _():
            xa_ext[0:HALO, :] = jnp.zeros((HALO, LW), F32)
            vg_ext[0:HALO31, :] = jnp.zeros((HALO31, CW), F32)
            hcar[...] = jnp.zeros_like(hcar)

        blocks = [(r0, c0) for c0 in range(0, LW, LANE) for r0 in range(0, t, rb)]

        def in_rows(r0, acc):
            rows = pl.ds(r0, 16)
            u1_ref[rows, :] = (x_ref[rows, :] * (1.0 + mod_ref[1:2, :]) + mod_ref[0:1, :]).astype(MXU_DTYPE)
            return acc

        _row_blocks(t, min(16, t), in_rows, ())
        proj_ref[...] = _dot_nt(u1_ref[...], win_v[...])

        xa_ext[HALO:HALO + t, :] = proj_ref[:, 0:LW]
        for k in range(LRU_K - 1):
            rot_c[k] = xa_ext[pl.ds(HALO - (LRU_K - 1) + k, t), :]
        for r0, c0 in blocks:
            rows, cols = pl.ds(r0, rb), slice(c0, c0 + LANE)
            xc = va_ref[4:5, cols] + va_ref[LRU_K - 1:LRU_K, cols] * xa_ext[pl.ds(HALO + r0, rb), cols]
            for k in range(LRU_K - 1):
                xc = xc + va_ref[k:k + 1, cols] * rot_c[k, rows, cols]
            xc_scr[rows, cols] = xc
            xcb_scr[rows, cols] = xc.astype(MXU_DTYPE)
        xa_ext[0:HALO, :] = xa_ext[t:t + HALO, :]
        gates[...] = _dot(xcb_scr[...], wri_ref[...])
        for r0, c0 in blocks:
            rows, cols = pl.ds(r0, rb), slice(c0, c0 + LANE)
            r = _sigmoid(gates[rows, cols] + va_ref[5:6, cols])
            ig = _sigmoid(gates[rows, LW + c0:LW + c0 + LANE] + va_ref[6:7, cols])
            la = (-LRU_C) * r * _softplus(-va_ref[7:8, cols])
            a = jnp.exp(la)
            a_scr[rows, cols] = a
            h_ref[rows, cols] = jnp.sqrt(-jnp.tanh(la) * (a * a + 1.0)) * (ig * xc_scr[rows, cols])
        _scan_fwd_blocks(a_scr, h_ref, hcar, t)

        for r0, c0 in blocks:
            rows, cols, pc = pl.ds(r0, rb), slice(c0, c0 + LANE), 2 * LW + c0
            vg_ext[pl.ds(HALO31 + r0, rb), cols] = (
                proj_ref[rows, pc:pc + LANE] * _sigmoid(proj_ref[rows, pc + CW:pc + CW + LANE]))
        vg_win = _shifted_rows(vg_ext, rot_a, rb)
        for r0, c0 in blocks:
            rows, cols = pl.ds(r0, rb), slice(c0, c0 + LANE)
            vc = vb_ref[CONV_K:CONV_K + 1, cols]
            for k in range(CONV_K):
                vc = vc + vb_ref[k:k + 1, cols] * vg_win(HALO31 - (CONV_K - 1) + k, r0, cols)
            vc_ref[rows, cols] = vc
            _store_hilo(hilo, rows, c0, vc)
        vg_ext[0:HALO31, :] = vg_ext[t:t + HALO31, :]
        mavg2 = mavg_ref[...]
        st_a[...] = _dot(hilo[...], mavg2)
        for r0, c0 in blocks:
            rows, cols = pl.ds(r0, rb), slice(c0, c0 + LANE)
            dl = vc_ref[rows, cols] - st_a[rows, cols]
            st_a[rows, cols] = dl
            _store_hilo(hilo, rows, c0, dl * dl)
        st_b[...] = _dot(hilo[...], mavg2)
        for r0, c0 in blocks:
            rows, cols = pl.ds(r0, rb), slice(c0, c0 + LANE)
            ge, _ = _gelu_and_grad(proj_ref[rows, LW + c0:LW + c0 + LANE])
            y_ref[rows, cols] = (ge * h_ref[rows, cols]).astype(MXU_DTYPE)
            yl = (st_a[rows, cols] * lax.rsqrt(st_b[rows, cols] + LN_EPS) * vb_ref[CONV_K + 1:CONV_K + 2, cols]
                  + vb_ref[CONV_K + 2:CONV_K + 3, cols])
            y_ref[rows, LW + c0:LW + c0 + LANE] = (yl * _sigmoid(yl)).astype(MXU_DTYPE)
        mixed_ref[...] = _dot(y_ref[...], wout_v[...])

        def out_rows(r0, acc):
            rows = pl.ds(r0, 8)
            xh, _ = _layer_norm_stats(ALPHA * x_ref[rows, :] + (1.0 + mod_ref[2:3, :]) * mixed_ref[rows, :])
            x1_ref[rows, :] = xh * ln1_ref[0:1, :] + ln1_ref[1:2, :]
            return acc

        _row_blocks(t, 8, out_rows, ())

        pl.when(step == bl * nt - 1)(g_finish)

    tok = lambda w: pl.BlockSpec((None, t, w), lambda b, i: (b, i, 0))
    outs = [(2 * LW + 2 * CW, F32), (LW, F32), (CW, F32), (D, MXU_DTYPE), (D, F32), (D, F32), (D, MXU_DTYPE)]
    return pl.pallas_call(
        body, name="mixer_fwd", grid=(bl, nt),
        out_shape=[jax.ShapeDtypeStruct((bl, seq, w), dt) for w, dt in outs] + _gather_out_shapes(ffn_shards),
        in_specs=[tok(D), pl.BlockSpec((None, 8, D), lambda b, i: (b, 0, 0)), _ANY, _ANY,
                  _full(wri.shape), _full(mavg2.shape), _full(va.shape), _full(vb.shape), _full(ln1.shape)]
        + [_ANY] * n_g,
        out_specs=[tok(w) for w, _ in outs] + [_ANY] * n_g,
        scratch_shapes=[pltpu.VMEM(win_t.shape, MXU_DTYPE), pltpu.VMEM(wout.shape, MXU_DTYPE),
                        pltpu.VMEM((t + HALO, LW), F32), pltpu.VMEM((t + HALO31, CW), F32),
                        pltpu.VMEM((7, t + HALO31 - 8, CW), F32), pltpu.VMEM((8, LW), F32),
                        pltpu.SemaphoreType.DMA((2,)), pltpu.VMEM((LRU_K - 1, t, LW), F32)]
        + [pltpu.VMEM((t, LW), F32)] * 4
        + [pltpu.VMEM((t, LW), MXU_DTYPE), pltpu.VMEM((t, 2 * LW), F32), pltpu.VMEM((t, 2 * CW), MXU_DTYPE)]
        + _comm_sems(n_g),
        compiler_params=_params(2),
    )(x, mods, win_t, wout, wri, mavg2, va, vb, ln1, *ffn_shards)


def _ffn_fwd(x1, mods, tgt, wup_t, wdown, f3, ln2):
    bl, seq, _ = x1.shape
    t = min(TOK_TILE, seq)
    nt = seq // t
    n_chunk = DFF // FF_CHUNK

    rb = min(ROW_BLK, t)

    def body(x1_ref, mod_ref, tgt_ref, wup_hbm, wdown_hbm, f3_ref, ln2_ref,
             hh_ref, z_ref, u2_ref, y2_ref, loss_ref,
             wup_v, wdown_v, g_ext, rot, sems):
        b, i = pl.program_id(0), pl.program_id(1)

        @pl.when((b == 0) & (i == 0))
        def _():
            cps = [pltpu.make_async_copy(wup_hbm, wup_v, sems.at[0]), pltpu.make_async_copy(wdown_hbm, wdown_v, sems.at[1])]
            for cp in cps:
                cp.start()
            for cp in cps:
                cp.wait()
            loss_ref[...] = jnp.zeros_like(loss_ref)

        @pl.when(i == 0)
        def _():
            for ch in range(n_chunk):
                g_ext[ch, 0:HALO, :] = jnp.zeros((HALO, FF_CHUNK), F32)

        def in_rows(r0, acc):
            rows = pl.ds(r0, 16)
            u2_ref[rows, :] = (x1_ref[rows, :] * (1.0 + mod_ref[4:5, :]) + mod_ref[3:4, :]).astype(MXU_DTYPE)
            return acc

        _row_blocks(t, min(16, t), in_rows, ())
        for ch in range(n_chunk):
            lo = ch * FF_CHUNK
            hh_ref[:, lo:lo + FF_CHUNK] = _dot_nt(u2_ref[...], wup_v[lo:lo + FF_CHUNK, :])
            hh_ref[:, DFF + lo:DFF + lo + FF_CHUNK] = _dot_nt(u2_ref[...], wup_v[DFF + lo:DFF + lo + FF_CHUNK, :])
            g_ext[ch, HALO:HALO + t, :] = hh_ref[:, DFF + lo:DFF + lo + FF_CHUNK]
            for k in range(FFN_K - 1):
                rot[k] = g_ext[ch, pl.ds(HALO - (FFN_K - 1) + k, t), :]
            for cb in range(FF_CHUNK // LANE):
                cols = slice(cb * LANE, (cb + 1) * LANE)
                fcols = slice(lo + cb * LANE, lo + (cb + 1) * LANE)
                for r0 in range(0, t, rb):
                    rows = pl.ds(r0, rb)
                    gc = (f3_ref[FFN_K:FFN_K + 1, fcols] + f3_ref[0:1, fcols] * rot[0, rows, cols]
                          + f3_ref[1:2, fcols] * rot[1, rows, cols]
                          + f3_ref[2:3, fcols] * g_ext[ch, pl.ds(HALO + r0, rb), cols])
                    z_ref[rows, fcols] = (gc * _sigmoid(gc) * hh_ref[rows, fcols]).astype(MXU_DTYPE)
            g_ext[ch, 0:HALO, :] = g_ext[ch, t:t + HALO, :]
        y2_ref[...] = _dot(z_ref[...], wdown_v[...])

        def out_rows(r0, acc):
            rows = pl.ds(r0, 8)
            xh, _ = _layer_norm_stats(ALPHA * x1_ref[rows, :] + (1.0 + mod_ref[5:6, :]) * y2_ref[rows, :])
            err = xh * ln2_ref[0:1, :] + ln2_ref[1:2, :] - tgt_ref[rows, :]
            return acc + err * err

        e2 = _row_blocks(t, 8, out_rows, jnp.zeros((8, D), F32))
        part = e2[:, 0:128]
        for j in range(1, D // 128):
            part = part + e2[:, 128 * j:128 * (j + 1)]
        loss_ref[...] += part

    tok = lambda w: pl.BlockSpec((None, t, w), lambda b, i: (b, i, 0))
    outs = [(2 * DFF, F32), (DFF, MXU_DTYPE), (D, MXU_DTYPE), (D, F32)]
    return pl.pallas_call(
        body, name="ffn_fwd", grid=(bl, nt),
        out_shape=[jax.ShapeDtypeStruct((bl, seq, w), dt) for w, dt in outs] + [jax.ShapeDtypeStruct((8, 128), F32)],
        in_specs=[tok(D), pl.BlockSpec((None, 8, D), lambda b, i: (b, 0, 0)), tok(D), _ANY, _ANY,
                  _full(f3.shape), _full(ln2.shape)],
        out_specs=[tok(w) for w, _ in outs] + [_full((8, 128))],
        scratch_shapes=[pltpu.VMEM(wup_t.shape, MXU_DTYPE), pltpu.VMEM(wdown.shape, MXU_DTYPE),
                        pltpu.VMEM((n_chunk, t + HALO, FF_CHUNK), F32), pltpu.VMEM((FFN_K - 1, t, FF_CHUNK), F32),
                        pltpu.SemaphoreType.DMA((2,))],
        compiler_params=_params(2),
    )(x1, mods, tgt, wup_t, wdown, f3, ln2)


def _ffn_bwd(x1, y2, tgt, hh, mods, wup_t, wdown, f3, ln2):
    bl, seq, _ = x1.shape
    t = min(TOK_TILE, seq)
    nt = seq // t
    n_chunk = DFF // FF_CHUNK
    rb = min(ROW_BLK, t)

    def body(x1_ref, y2_ref, tgt_ref, hh_ref, halo_ref, mod_ref, wup_hbm, wdown_hbm, f3_ref, ln2_ref,
             dx1_ref, dy2_ref, dhh_ref, dln2_ref, df3_ref, dmod_ref,
             wup_v, wdown_v, g_ext, dgc_ext, rot, dz_scr, wide_scr, sems):
        b, i = pl.program_id(0), pl.program_id(1)
        tt = nt - 1 - i

        @pl.when((b == 0) & (i == 0))
        def _():
            cps = [pltpu.make_async_copy(wup_hbm, wup_v, sems.at[0]), pltpu.make_async_copy(wdown_hbm, wdown_v, sems.at[1])]
            for cp in cps:
                cp.start()
            for cp in cps:
                cp.wait()
            dln2_ref[...] = jnp.zeros_like(dln2_ref)
            df3_ref[...] = jnp.zeros_like(df3_ref)

        @pl.when(i == 0)
        def _():
            dmod_ref[...] = jnp.zeros_like(dmod_ref)
            for ch in range(n_chunk):
                dgc_ext[ch, t:t + HALO, :] = jnp.zeros((HALO, FF_CHUNK), F32)

        def ln_rows(r0, acc):
            rows = pl.ds(r0, 8)
            x1v, y2v = x1_ref[rows, :], y2_ref[rows, :]
            gt = mod_ref[5:6, :]
            xh, rstd = _layer_norm_stats(ALPHA * x1v + (1.0 + gt) * y2v)
            g2 = ln2_ref[0:1, :]
            dx2 = (xh * g2 + ln2_ref[1:2, :] - tgt_ref[rows, :]) * (1.0 / D)
            dxh = dx2 * g2
            dr2 = rstd * (dxh - _rowmean(dxh) - xh * _rowmean(dxh * xh))
            wide_scr[rows, :] = (1.0 + gt) * dr2
            dx1_ref[rows, :] = ALPHA * dr2
            return acc[0] + dx2 * xh, acc[1] + dx2, acc[2] + dr2 * y2v

        zero_d = jnp.zeros((8, D), F32)
        acc = _row_blocks(t, 8, ln_rows, (zero_d, zero_d, zero_d))
        dln2_ref[0:1, :] += _colsum(acc[0])
        dln2_ref[1:2, :] += _colsum(acc[1])
        dmod_ref[2:3, :] += _colsum(acc[2])
        _cast_rows(wide_scr, dy2_ref, t)

        halo_keep = (tt > 0).astype(F32)
        zero_l = jnp.zeros((8, LANE), F32)
        for ch in range(n_chunk):
            lo = ch * FF_CHUNK
            dz_scr[...] = _dot_nt(dy2_ref[...], wdown_v[lo:lo + FF_CHUNK, :])
            g_ext[0:HALO, :] = halo_ref[:, lo:lo + FF_CHUNK] * halo_keep
            g_ext[HALO:HALO + t, :] = hh_ref[:, DFF + lo:DFF + lo + FF_CHUNK]
            for k in range(FFN_K - 1):
                rot[k] = g_ext[pl.ds(HALO - (FFN_K - 1) + k, t), :]
            for cb in range(FF_CHUNK // LANE):
                cols = slice(cb * LANE, (cb + 1) * LANE)
                fcols = slice(lo + cb * LANE, lo + (cb + 1) * LANE)

                def gate_rows(r0, acc, cols=cols, fcols=fcols, ch=ch):
                    rows = pl.ds(r0, rb)
                    taps = [rot[0, rows, cols], rot[1, rows, cols], g_ext[pl.ds(r0 + HALO, rb), cols]]
                    gc = f3_ref[FFN_K:FFN_K + 1, fcols]
                    for k in range(FFN_K):
                        gc = gc + f3_ref[k:k + 1, fcols] * taps[k]
                    s = _sigmoid(gc)
                    dz = dz_scr[rows, cols]
                    dhh_ref[rows, fcols] = (dz * (gc * s)).astype(MXU_DTYPE)
                    dgc = dz * hh_ref[rows, fcols] * (s * (1.0 + gc * (1.0 - s)))
                    dgc_ext[ch, rows, cols] = dgc
                    return tuple(acc[k] + _fold8(dgc * taps[k]) for k in range(FFN_K)) + (acc[FFN_K] + _fold8(dgc),)

                acc = _row_blocks(t, rb, gate_rows, (zero_l,) * (FFN_K + 1))
                for k in range(FFN_K + 1):
                    df3_ref[k:k + 1, fcols] += _colsum(acc[k])
            for k in range(FFN_K - 1):
                rot[k] = dgc_ext[ch, pl.ds(k + 1, t), :]
            for cb in range(FF_CHUNK // LANE):
                cols = slice(cb * LANE, (cb + 1) * LANE)
                fcols = slice(lo + cb * LANE, lo + (cb + 1) * LANE)
                gcols = slice(DFF + lo + cb * LANE, DFF + lo + (cb + 1) * LANE)

                def dg_rows(r0, acc, cols=cols, fcols=fcols, gcols=gcols, ch=ch):
                    rows = pl.ds(r0, rb)
                    dg = (f3_ref[2:3, fcols] * dgc_ext[ch, rows, cols] + f3_ref[1:2, fcols] * rot[0, rows, cols]
                          + f3_ref[0:1, fcols] * rot[1, rows, cols])
                    dhh_ref[rows, gcols] = dg.astype(MXU_DTYPE)
                    return acc

                _row_blocks(t, rb, dg_rows, ())
            dgc_ext[ch, t:t + HALO, :] = dgc_ext[ch, 0:HALO, :]

        wide_scr[...] = _dot(dhh_ref[...], wup_v[...])

        def out_rows(r0, acc):
            rows = pl.ds(r0, 8)
            du2 = wide_scr[rows, :]
            dx1_ref[rows, :] = dx1_ref[rows, :] + du2 * (1.0 + mod_ref[4:5, :])
            return acc[0] + du2 * x1_ref[rows, :], acc[1] + du2

        acc = _row_blocks(t, 8, out_rows, (zero_d, zero_d))
        dmod_ref[1:2, :] += _colsum(acc[0])
        dmod_ref[0:1, :] += _colsum(acc[1])

    rev = lambda w: pl.BlockSpec((None, t, w), lambda b, i: (b, nt - 1 - i, 0))
    halo = pl.BlockSpec((None, HALO, DFF), lambda b, i: (b, jnp.maximum((nt - 1 - i) * (t // HALO) - 1, 0), 1))
    return pl.pallas_call(
        body, name="ffn_bwd", grid=(bl, nt),
        out_shape=[jax.ShapeDtypeStruct((bl, seq, D), F32), jax.ShapeDtypeStruct((bl, seq, D), MXU_DTYPE),
                   jax.ShapeDtypeStruct((bl, seq, 2 * DFF), MXU_DTYPE), jax.ShapeDtypeStruct((8, D), F32),
                   jax.ShapeDtypeStruct((8, DFF), F32), jax.ShapeDtypeStruct((bl, 8, D), F32)],
        in_specs=[rev(D), rev(D), rev(D), rev(2 * DFF), halo, pl.BlockSpec((None, 8, D), lambda b, i: (b, 0, 0)),
                  _ANY, _ANY, _full(f3.shape), _full(ln2.shape)],
        out_specs=[rev(D), rev(D), rev(2 * DFF), _full((8, D)), _full((8, DFF)),
                   pl.BlockSpec((None, 8, D), lambda b, i: (b, 0, 0))],
        scratch_shapes=[pltpu.VMEM(wup_t.shape, MXU_DTYPE), pltpu.VMEM(wdown.shape, MXU_DTYPE),
                        pltpu.VMEM((t + HALO, FF_CHUNK), F32), pltpu.VMEM((n_chunk, t + HALO, FF_CHUNK), F32),
                        pltpu.VMEM((FFN_K - 1, t, FF_CHUNK), F32), pltpu.VMEM((t, FF_CHUNK), F32),
                        pltpu.VMEM((t, D), F32), pltpu.SemaphoreType.DMA((2,))],
        compiler_params=_params(2),
    )(x1, y2, tgt, hh, hh, mods, wup_t, wdown, f3, ln2)


def _mixer_bwd(x, dx1, proj, h, vc, mixed, mods, win_t, wout, wri, mavg2, va, vb, ln1, ffn_wgrads):
    bl, seq, _ = x.shape
    t = min(TOK_TILE, seq)
    nt = seq // t
    pw = 2 * LW + 2 * CW
    n_g = len(ffn_wgrads)
    rb = min(ROW_BLK, t)
    half_f32 = [pltpu.VMEM((t, LW), F32)] * 10

    def body(x_ref, dx1_ref, proj_ref, phalo_ref, h_ref, hhalo_ref, vc_ref, mixed_ref, mod_ref,
             win_hbm, wout_hbm, wri_ref, mavg_ref, va_ref, vb_ref, ln1_ref, *rest):
        wgrad_refs, rest = rest[:n_g], rest[n_g:]
        gx_ref, dproj_ref, dm_ref, ga_ref, gw31_ref, dwr_ref, dwi_ref, dln1_ref, dmod_ref = rest[:9]
        slot_refs, rest = rest[9:9 + n_g], rest[9 + n_g:]
        (win_v, wout_v, xa_ext, vg_scr, dvc_ext, dxc_ext, rot_b, car, sems,
         st_a, st_b, st_c, yn_scr, dyn_scr, hprev_scr, xc_scr, a_scr, mult_scr, g_scr,
         wide, gates, hilo_a, hilo_b, dgates, xcb_scr, rot_c, rot_d, hp_ext, e_send, e_recv, e_local) = rest
        b, i = pl.program_id(0), pl.program_id(1)
        tt = nt - 1 - i
        e_start, e_finish = _exchange_steps(wgrad_refs, slot_refs, e_send, e_recv, e_local)

        @pl.when((b == 0) & (i == 0))
        def _():
            e_start()
            cps = [pltpu.make_async_copy(win_hbm, win_v, sems.at[0]), pltpu.make_async_copy(wout_hbm, wout_v, sems.at[1])]
            for cp in cps:
                cp.start()
            for cp in cps:
                cp.wait()
            for ref in (ga_ref, gw31_ref, dwr_ref, dwi_ref, dln1_ref):
                ref[...] = jnp.zeros_like(ref)

        @pl.when(i == 0)
        def _():
            dmod_ref[...] = jnp.zeros_like(dmod_ref)
            dvc_ext[t:t + HALO31, :] = jnp.zeros((HALO31, CW), F32)
            dxc_ext[t:t + HALO, :] = jnp.zeros((HALO, LW), F32)
            car[...] = jnp.zeros_like(car)

        halo_keep = (tt > 0).astype(F32)
        zero_d = jnp.zeros((8, D), F32)
        zero_l = jnp.zeros((8, LANE), F32)
        blocks = [(r0, c0) for c0 in range(0, LW, LANE) for r0 in range(0, t, rb)]

        def add_row(ref, row, c0, acc):
            ref[row:row + 1, c0:c0 + LANE] += _colsum(acc)

        def ln_rows(r0, acc):
            rows = pl.ds(r0, 8)
            xv, mixed = x_ref[rows, :], mixed_ref[rows, :]
            gt = mod_ref[2:3, :]
            xh, rstd = _layer_norm_stats(ALPHA * xv + (1.0 + gt) * mixed)
            dx1 = dx1_ref[rows, :]
            dxh = dx1 * ln1_ref[0:1, :]
            dr1 = rstd * (dxh - _rowmean(dxh) - xh * _rowmean(dxh * xh))
            wide[rows, :] = (1.0 + gt) * dr1
            gx_ref[rows, :] = ALPHA * dr1
            return acc[0] + dx1 * xh, acc[1] + dx1, acc[2] + dr1 * mixed

        acc = _row_blocks(t, 8, ln_rows, (zero_d, zero_d, zero_d))
        dln1_ref[0:1, :] += _colsum(acc[0])
        dln1_ref[1:2, :] += _colsum(acc[1])
        dmod_ref[2:3, :] += _colsum(acc[2])
        _cast_rows(wide, dm_ref, t)
        wide[...] = _dot_nt(dm_ref[...], wout_v[...])

        mavg2 = mavg_ref[...]
        for r0, c0 in blocks:
            rows, cols = pl.ds(r0, rb), slice(c0, c0 + LANE)
            _store_hilo(hilo_a, rows, c0, vc_ref[rows, cols])
        st_a[...] = _dot(hilo_a[...], mavg2)
        for r0, c0 in blocks:
            rows, cols = pl.ds(r0, rb), slice(c0, c0 + LANE)
            dl = vc_ref[rows, cols] - st_a[rows, cols]
            yn_scr[rows, cols] = dl
            _store_hilo(hilo_a, rows, c0, dl * dl)
        st_b[...] = _dot(hilo_a[...], mavg2)
        for c0 in range(0, CW, LANE):
            cols, pc = slice(c0, c0 + LANE), 2 * LW + c0
            ng, nb = vb_ref[CONV_K + 1:CONV_K + 2, cols], vb_ref[CONV_K + 2:CONV_K + 3, cols]
            acc_g, acc_b = zero_l, zero_l
            for r0 in range(0, t, rb):
                rows = pl.ds(r0, rb)
                rs = lax.rsqrt(st_b[rows, cols] + LN_EPS)
                yn = yn_scr[rows, cols] * rs
                yl = yn * ng + nb
                s = _sigmoid(yl)
                dyl = wide[rows, LW + c0:LW + c0 + LANE] * (s * (1.0 + yl * (1.0 - s)))
                acc_g, acc_b = acc_g + _fold8(dyl * yn), acc_b + _fold8(dyl)
                dyn = dyl * ng
                st_b[rows, cols] = rs
                yn_scr[rows, cols] = yn
                dyn_scr[rows, cols] = dyn
                _store_hilo(hilo_a, rows, c0, dyn)
                _store_hilo(hilo_b, rows, c0, dyn * yn)
                vg_scr[rows, cols] = proj_ref[rows, pc:pc + LANE] * _sigmoid(proj_ref[rows, pc + CW:pc + CW + LANE])
            add_row(ga_ref, 8, c0, acc_g)
            add_row(ga_ref, 9, c0, acc_b)
        st_a[...] = _dot(hilo_a[...], mavg2)
        st_c[...] = _dot(hilo_b[...], mavg2)
        for c0 in range(0, CW, LANE):
            cols = slice(c0, c0 + LANE)
            acc_b = zero_l
            for r0 in range(0, t, rb):
                rows = pl.ds(r0, rb)
                dvc = st_b[rows, cols] * (dyn_scr[rows, cols] - st_a[rows, cols] - yn_scr[rows, cols] * st_c[rows, cols])
                dvc_ext[rows, cols] = dvc
                acc_b = acc_b + _fold8(dvc)
            add_row(ga_ref, 10, c0, acc_b)
        dvc_win = _shifted_rows(dvc_ext, rot_b, rb)
        for c0 in range(0, CW, LANE):
            cols, pc = slice(c0, c0 + LANE), 2 * LW + c0
            for k in range(CONV_K):
                acc_w = zero_l
                for r0 in range(0, t, rb):
                    acc_w = acc_w + _fold8(vg_scr[pl.ds(r0, rb), cols] * dvc_win(CONV_K - 1 - k, r0, cols))
                add_row(gw31_ref, k, c0, acc_w)
            for r0 in range(0, t, rb):
                rows = pl.ds(r0, rb)
                dvg = jnp.zeros((rb, LANE), F32)
                for k in range(CONV_K):
                    dvg = dvg + vb_ref[k:k + 1, cols] * dvc_win(CONV_K - 1 - k, r0, cols)
                vbr = proj_ref[rows, pc:pc + LANE]
                sgb = _sigmoid(proj_ref[rows, pc + CW:pc + CW + LANE])
                dproj_ref[rows, pc:pc + LANE] = (dvg * sgb).astype(MXU_DTYPE)
                dproj_ref[rows, pc + CW:pc + CW + LANE] = (dvg * vbr * sgb * (1.0 - sgb)).astype(MXU_DTYPE)
        dvc_ext[t:t + HALO31, :] = dvc_ext[0:HALO31, :]

        xa_ext[0:HALO, :] = phalo_ref[HALO31 - HALO:HALO31, 0:LW] * halo_keep
        xa_ext[HALO:HALO + t, :] = proj_ref[:, 0:LW]
        for k in range(LRU_K - 1):
            rot_c[k] = xa_ext[pl.ds(HALO - (LRU_K - 1) + k, t), :]
        hp_ext[0:HALO, :] = hhalo_ref[...] * halo_keep
        hp_ext[HALO:HALO + t, :] = h_ref[...]
        hprev_scr[...] = hp_ext[pl.ds(HALO - 1, t), :]

        def xa_tap(k, rows, r0, cols):
            return xa_ext[pl.ds(HALO + r0, rb), cols] if k == LRU_K - 1 else rot_c[k, rows, cols]

        for r0, c0 in blocks:
            rows, cols = pl.ds(r0, rb), slice(c0, c0 + LANE)
            xc = va_ref[4:5, cols]
            for k in range(LRU_K):
                xc = xc + va_ref[k:k + 1, cols] * xa_tap(k, rows, r0, cols)
            xc_scr[rows, cols] = xc
            xcb_scr[rows, cols] = xc.astype(MXU_DTYPE)
        gates[...] = _dot(xcb_scr[...], wri_ref[...])
        for r0, c0 in blocks:
            rows, cols = pl.ds(r0, rb), slice(c0, c0 + LANE)
            r = _sigmoid(gates[rows, cols] + va_ref[5:6, cols])
            ig = _sigmoid(gates[rows, LW + c0:LW + c0 + LANE] + va_ref[6:7, cols])
            la = (-LRU_C) * r * _softplus(-va_ref[7:8, cols])
            a = jnp.exp(la)
            gates[rows, cols] = r
            gates[rows, LW + c0:LW + c0 + LANE] = ig
            a_scr[rows, cols] = a
            mult_scr[rows, cols] = jnp.sqrt(-jnp.tanh(la) * (a * a + 1.0))
            ge, dge = _gelu_and_grad(proj_ref[rows, LW + c0:LW + c0 + LANE])
            dya = wide[rows, cols]
            dproj_ref[rows, LW + c0:LW + c0 + LANE] = (dya * h_ref[rows, cols] * dge).astype(MXU_DTYPE)
            g_scr[rows, cols] = dya * ge
        _scan_bwd_blocks(a_scr, g_scr, car, t)
        for c0 in range(0, LW, LANE):
            cols = slice(c0, c0 + LANE)
            sp = _softplus(-va_ref[7:8, cols])
            acc_l, acc_r, acc_i = zero_l, zero_l, zero_l
            for r0 in range(0, t, rb):
                rows = pl.ds(r0, rb)
                gv, xc, a, mult = g_scr[rows, cols], xc_scr[rows, cols], a_scr[rows, cols], mult_scr[rows, cols]
                r, ig = gates[rows, cols], gates[rows, LW + c0:LW + c0 + LANE]
                dla = gv * hprev_scr[rows, cols] * a - gv * (ig * xc) * (a * a / mult)
                acc_l = acc_l + _fold8(dla * ((-LRU_C) * r))
                dgr = dla * ((-LRU_C) * sp) * r * (1.0 - r)
                dgi = gv * (mult * xc) * ig * (1.0 - ig)
                acc_r, acc_i = acc_r + _fold8(dgr), acc_i + _fold8(dgi)
                dgates[rows, cols] = dgr.astype(MXU_DTYPE)
                dgates[rows, LW + c0:LW + c0 + LANE] = dgi.astype(MXU_DTYPE)
                dxc_ext[rows, cols] = gv * (mult * ig)
            add_row(ga_ref, 5, c0, acc_r)
            add_row(ga_ref, 6, c0, acc_i)
            ga_ref[7:8, cols] += _colsum(acc_l) * (-_sigmoid(-va_ref[7:8, cols]))
        dwr_ref[...] += _dot_tn(xcb_scr[...], dgates[:, 0:LW])
        dwi_ref[...] += _dot_tn(xcb_scr[...], dgates[:, LW:])
        st_a[...] = _dot_nt(dgates[...], wri_ref[...])
        for c0 in range(0, LW, LANE):
            cols = slice(c0, c0 + LANE)
            acc_b = zero_l
            for r0 in range(0, t, rb):
                rows = pl.ds(r0, rb)
                dxc = dxc_ext[rows, cols] + st_a[rows, cols]
                dxc_ext[rows, cols] = dxc
                acc_b = acc_b + _fold8(dxc)
            add_row(ga_ref, 4, c0, acc_b)
        for k in range(LRU_K - 1):
            rot_d[k] = dxc_ext[pl.ds(k + 1, t), :]
        for c0 in range(0, LW, LANE):
            cols = slice(c0, c0 + LANE)
            acc_w = [zero_l] * LRU_K
            for r0 in range(0, t, rb):
                rows = pl.ds(r0, rb)
                dxc = dxc_ext[rows, cols]
                dxa = va_ref[LRU_K - 1:LRU_K, cols] * dxc
                for k in range(LRU_K):
                    acc_w[k] = acc_w[k] + _fold8(dxc * xa_tap(k, rows, r0, cols))
                    if k < LRU_K - 1:
                        dxa = dxa + va_ref[k:k + 1, cols] * rot_d[LRU_K - 2 - k, rows, cols]
                dproj_ref[rows, cols] = dxa.astype(MXU_DTYPE)
            for k in range(LRU_K):
                add_row(ga_ref, k, c0, acc_w[k])
        dxc_ext[t:t + HALO, :] = dxc_ext[0:HALO, :]

        wide[...] = _dot(dproj_ref[...], win_v[...])

        def out_rows(r0, acc):
            rows = pl.ds(r0, 8)
            du1 = wide[rows, :]
            gx_ref[rows, :] = gx_ref[rows, :] + du1 * (1.0 + mod_ref[1:2, :])
            return acc[0] + du1 * x_ref[rows, :], acc[1] + du1

        acc = _row_blocks(t, 8, out_rows, (zero_d, zero_d))
        dmod_ref[1:2, :] += _colsum(acc[0])
        dmod_ref[0:1, :] += _colsum(acc[1])

        pl.when((b == bl - 1) & (i == nt - 1))(e_finish)

    rev = lambda w: pl.BlockSpec((None, t, w), lambda b, i: (b, nt - 1 - i, 0))

    def halo(rows_, w):
        return pl.BlockSpec((None, rows_, w), lambda b, i: (b, jnp.maximum((nt - 1 - i) * (t // rows_) - 1, 0), 0))

    return pl.pallas_call(
        body, name="mixer_bwd", grid=(bl, nt),
        out_shape=[jax.ShapeDtypeStruct((bl, seq, D), F32), jax.ShapeDtypeStruct((bl, seq, pw), MXU_DTYPE),
                   jax.ShapeDtypeStruct((bl, seq, D), MXU_DTYPE), jax.ShapeDtypeStruct((16, LW), F32),
                   jax.ShapeDtypeStruct((32, CW), F32), jax.ShapeDtypeStruct((LW, LW), F32),
                   jax.ShapeDtypeStruct((LW, LW), F32), jax.ShapeDtypeStruct((8, D), F32),
                   jax.ShapeDtypeStruct((bl, 8, D), F32)] + _exchange_out_shapes(ffn_wgrads),
        in_specs=[rev(D), rev(D), rev(pw), halo(HALO31, pw), rev(LW), halo(HALO, LW), rev(CW), rev(D),
                  pl.BlockSpec((None, 8, D), lambda b, i: (b, 0, 0)), _ANY, _ANY,
                  _full(wri.shape), _full(mavg2.shape), _full(va.shape), _full(vb.shape), _full(ln1.shape)]
        + [_ANY] * n_g,
        out_specs=[rev(D), rev(pw), rev(D), _full((16, LW)), _full((32, CW)), _full((LW, LW)), _full((LW, LW)),
                   _full((8, D)), pl.BlockSpec((None, 8, D), lambda b, i: (b, 0, 0))] + [_ANY] * n_g,
        scratch_shapes=[pltpu.VMEM(win_t.shape, MXU_DTYPE), pltpu.VMEM(wout.shape, MXU_DTYPE),
                        pltpu.VMEM((t + HALO, LW), F32), pltpu.VMEM((t, CW), F32),
                        pltpu.VMEM((t + HALO31, CW), F32), pltpu.VMEM((t + HALO, LW), F32),
                        pltpu.VMEM((7, t + HALO31 - 8, CW), F32),
                        pltpu.VMEM((8, LW), F32), pltpu.SemaphoreType.DMA((2,))] + half_f32
        + [pltpu.VMEM((t, D), F32), pltpu.VMEM((t, 2 * LW), F32), pltpu.VMEM((t, 2 * CW), MXU_DTYPE),
           pltpu.VMEM((t, 2 * CW), MXU_DTYPE), pltpu.VMEM((t, 2 * LW), MXU_DTYPE), pltpu.VMEM((t, LW), MXU_DTYPE),
           pltpu.VMEM((LRU_K - 1, t, LW), F32), pltpu.VMEM((LRU_K - 1, t, LW), F32), pltpu.VMEM((t + HALO, LW), F32)]
        + _comm_sems(n_g),
        compiler_params=_params(2),
    )(x, dx1, proj, proj, h, h, vc, mixed, mods, win_t, wout, wri, mavg2, va, vb, ln1, *ffn_wgrads)


def _weight_grad(a, bmat, tm, name, out_dtype=F32, exchange=(), exchange_half=None, col_phase=None, tok_tile=None):
    ntok, m = a.shape
    n = bmat.shape[1]
    tk = min(tok_tile or WG_TOK_TILE, ntok)
    nk = ntok // tk
    stride, col0 = (1, 0) if col_phase is None else (2, col_phase)
    m = m // stride
    ni = m // tm
    n_e = len(exchange)

    def body(a_ref, b_ref, *rest):
        e_in, o_ref, e_out = rest[:n_e], rest[n_e], rest[n_e + 1:2 * n_e + 1]
        acc_ref = rest[2 * n_e + 1]
        i, k = pl.program_id(0), pl.program_id(1)
        if n_e:
            e_start, e_finish = _exchange_steps(e_in, e_out, *rest[2 * n_e + 2:], half=exchange_half)
            pl.when((i == 0) & (k == 0))(e_start)

        @pl.when(k == 0)
        def _():
            acc_ref[...] = jnp.zeros_like(acc_ref)

        acc_ref[...] += _dot_tn(a_ref[...], b_ref[...])

        @pl.when(k == nk - 1)
        def _():
            o_ref[...] = acc_ref[...].astype(out_dtype)

        if n_e:
            pl.when((i == ni - 1) & (k == nk - 1))(e_finish)

    res = pl.pallas_call(
        body, name=name, grid=(ni, nk),
        out_shape=[jax.ShapeDtypeStruct((m, n), out_dtype)] + _exchange_out_shapes(exchange, exchange_half),
        in_specs=[pl.BlockSpec((tk, tm), lambda i, k: (k, i * stride + col0)), pl.BlockSpec((tk, n), lambda i, k: (k, 0))]
        + [_ANY] * n_e,
        out_specs=[pl.BlockSpec((tm, n), lambda i, k: (i, 0))] + [_ANY] * n_e,
        scratch_shapes=[pltpu.VMEM((tm, n), F32)] + (_comm_sems(n_e) if n_e else []),
        compiler_params=_params(2),
    )(a, bmat, *exchange)
    return res if n_e else res[0]


def _small_grad_layout(bl):
    r1 = 6 * bl
    return {"dmod": 0, "wide": r1, "ln": r1 + 32, "df3": r1 + 36, "g_wr": r1 + 48, "g_wi": r1 + 80, "rows": r1 + 112}


def _pack_small_grads(dmod_a, dmod_b, g_a, g_w31, dln1, dln2, df3, g_wr, g_wi, loss_part):
    bl = dmod_a.shape[0] // 8
    lay = _small_grad_layout(bl)
    ins = [dmod_a, dmod_b, g_a, g_w31, dln1, dln2, df3, g_wr, g_wi, loss_part]

    def body(ma_ref, mb_ref, ga_ref, gw_ref, l1_ref, l2_ref, f3_ref, wr_ref, wi_ref, ls_ref, o_ref):
        o_ref[...] = jnp.zeros_like(o_ref)
        for b in range(bl):
            o_ref[6 * b:6 * b + 3, :] = ma_ref[8 * b:8 * b + 3, :]
            o_ref[6 * b + 3:6 * b + 6, :] = mb_ref[8 * b:8 * b + 3, :]
        wide = lay["wide"]
        o_ref[wide:wide + 32, 0:CW] = gw_ref[...]
        o_ref[wide:wide + 16, CW:CW + LW] = ga_ref[...]
        o_ref[wide + 16:wide + 24, CW:CW + LANE] = ls_ref[...]
        o_ref[lay["ln"]:lay["ln"] + 2, :] = l1_ref[0:2, :]
        o_ref[lay["ln"] + 2:lay["ln"] + 4, :] = l2_ref[0:2, :]
        for j, c0 in enumerate(range(0, DFF, D)):
            w = min(D, DFF - c0)
            o_ref[lay["df3"] + 4 * j:lay["df3"] + 4 * j + 4, 0:w] = f3_ref[0:4, c0:c0 + w]
        o_ref[lay["g_wr"]:lay["g_wr"] + 32, :] = wr_ref[...]
        o_ref[lay["g_wi"]:lay["g_wi"] + 32, :] = wi_ref[...]

    return pl.pallas_call(
        body, name="pack_small_grads", out_shape=jax.ShapeDtypeStruct((lay["rows"], D), F32),
        in_specs=[_full(a.shape) for a in ins], out_specs=_full((lay["rows"], D)),
        compiler_params=pltpu.CompilerParams(vmem_limit_bytes=VMEM_LIMIT),
    )(*ins)


def _sum_small_grads(packed_all, bl):
    lay = _small_grad_layout(bl)
    shapes = {"lru_conv_w": (LRU_K, LW), "lru_conv_b": (1, LW), "lru_b_r": (1, LW), "lru_b_i": (1, LW),
              "lru_lambda": (1, LW), "conv_norm_g": (1, CW), "conv_norm_b": (1, CW), "conv_b": (1, CW),
              "conv_w": (CONV_K, CW), "ln1_g": (1, D), "ln1_b": (1, D), "ln2_g": (1, D), "ln2_b": (1, D),
              "ffn_conv_w": (FFN_K, DFF), "ffn_conv_b": (1, DFF), "lru_w_r": (32, D), "lru_w_i": (32, D),
              "loss": (8, LANE)}
    names = list(shapes)

    def body(p_ref, *outs):
        o = dict(zip(names, outs))

        def part(r0, nr, c0=0, nc=D):
            acc = p_ref[0, r0:r0 + nr, c0:c0 + nc]
            for j in range(1, NDEV):
                acc = acc + p_ref[j, r0:r0 + nr, c0:c0 + nc]
            return acc

        wide = lay["wide"]
        ga = part(wide, 16, CW, LW)
        o["lru_conv_w"][...] = ga[0:LRU_K]
        for row, k in ((4, "lru_conv_b"), (5, "lru_b_r"), (6, "lru_b_i"), (7, "lru_lambda"), (8, "conv_norm_g"),
                       (9, "conv_norm_b"), (10, "conv_b")):
            o[k][...] = ga[row:row + 1]
        o["conv_w"][...] = part(wide, 32, 0, CW)[0:CONV_K]
        o["loss"][...] = part(wide + 16, 8, CW, LANE)
        ln = part(lay["ln"], 4)
        o["ln1_g"][...], o["ln1_b"][...], o["ln2_g"][...], o["ln2_b"][...] = ln[0:1], ln[1:2], ln[2:3], ln[3:4]
        for j, c0 in enumerate(range(0, DFF, D)):
            w = min(D, DFF - c0)
            f3 = part(lay["df3"] + 4 * j, 4, 0, w)
            o["ffn_conv_w"][:, c0:c0 + w] = f3[0:FFN_K]
            o["ffn_conv_b"][:, c0:c0 + w] = f3[FFN_K:FFN_K + 1]
        o["lru_w_r"][...] = part(lay["g_wr"], 32)
        o["lru_w_i"][...] = part(lay["g_wi"], 32)

    res = pl.pallas_call(
        body, name="sum_small_grads", out_shape=[jax.ShapeDtypeStruct(shapes[k], F32) for k in names],
        in_specs=[_full(packed_all.shape)], out_specs=[_full(shapes[k]) for k in names],
        compiler_params=pltpu.CompilerParams(vmem_limit_bytes=VMEM_LIMIT),
    )(packed_all)
    return dict(zip(names, res))


def _adamw_update(w_ref, g_ref, m_ref, v_ref, d_ref, nm_ref, nv_ref):
    gv = g_ref[...]
    nm = ADAM_B1 * m_ref[...] + (1.0 - ADAM_B1) * gv
    nv = ADAM_B2 * v_ref[...] + (1.0 - ADAM_B2) * (gv * gv)
    m_hat = nm / (1.0 - ADAM_B1 ** ADAM_STEP)
    v_hat = nv / (1.0 - ADAM_B2 ** ADAM_STEP)
    d_ref[...] = -ADAM_LR * (m_hat / (jnp.sqrt(v_hat) + ADAM_EPS) + ADAM_WD * w_ref[...])
    nm_ref[...] = nm
    nv_ref[...] = nv


def _sum_adamw(slots, w, m, v, name):
    _, r, cdim = slots.shape
    tr = next((cand for cand in (352, 176, 128, 64, 32) if r % cand == 0), r)

    def body(s_ref, w_ref, m_ref, v_ref, g_ref, d_ref, nm_ref, nv_ref):
        acc = s_ref[0].astype(F32)
        for j in range(1, NDEV):
            acc = acc + s_ref[j].astype(F32)
        g_ref[...] = acc
        _adamw_update(w_ref, g_ref, m_ref, v_ref, d_ref, nm_ref, nv_ref)

    spec = pl.BlockSpec((tr, cdim), lambda i: (i, 0))
    return pl.pallas_call(
        body, name=name, grid=(r // tr,), out_shape=[jax.ShapeDtypeStruct((r, cdim), F32)] * 4,
        in_specs=[pl.BlockSpec((NDEV, tr, cdim), lambda i: (0, i, 0))] + [spec] * 3, out_specs=[spec] * 4,
        compiler_params=_params(1),
    )(slots, w, m, v)


def _sum_adamw_transposed(slots_c0, slots_c1, w, m, v, name):
    _, r, cdim = slots_c0.shape
    tc = 256

    def body(s0_ref, s1_ref, w_ref, m_ref, v_ref, g_ref, d_ref, nm_ref, nv_ref):
        def total(s_ref):
            acc = s_ref[0].astype(F32)
            for j in range(1, NDEV):
                acc = acc + s_ref[j].astype(F32)
            g_ref[...] = acc.T

        c = lax.axis_index("c")
        pl.when(c == 0)(functools.partial(total, s0_ref))
        pl.when(c == 1)(functools.partial(total, s1_ref))
        _adamw_update(w_ref, g_ref, m_ref, v_ref, d_ref, nm_ref, nv_ref)

    slot_spec = pl.BlockSpec((NDEV, r, tc), lambda j: (0, 0, j))
    spec = pl.BlockSpec((tc, r), lambda j: (j, 0))
    return pl.pallas_call(
        body, name=name, grid=(cdim // tc,), out_shape=[jax.ShapeDtypeStruct((cdim, r), F32)] * 4,
        in_specs=[slot_spec, slot_spec] + [spec] * 3, out_specs=[spec] * 4,
        compiler_params=_params(1),
    )(slots_c0, slots_c1, w, m, v)


def _adamw_many(ws, gs, ms, vs, name):
    n = len(ws)

    def body(*refs):
        ins, outs = refs[:4 * n], refs[4 * n:]
        for k in range(n):
            _adamw_update(ins[k], ins[n + k], ins[2 * n + k], ins[3 * n + k], outs[k], outs[n + k], outs[2 * n + k])

    specs = [_full(w.shape) for w in ws]
    res = pl.pallas_call(
        body, name=name, out_shape=[jax.ShapeDtypeStruct(w.shape, F32) for w in ws] * 3,
        in_specs=specs * 4, out_specs=specs * 3,
        compiler_params=pltpu.CompilerParams(vmem_limit_bytes=VMEM_LIMIT),
    )(*ws, *gs, *ms, *vs)
    return res[:n], res[n:2 * n], res[2 * n:]


def _adamw(w, g, m, v, name):
    r, cdim = w.shape
    tr = next((cand for cand in (256, 128) if r % cand == 0), r)
    body = functools.partial(_adamw_update)
    spec = pl.BlockSpec((tr, cdim), lambda i: (i, 0))
    return pl.pallas_call(
        body, name=name, grid=(r // tr,), out_shape=[jax.ShapeDtypeStruct((r, cdim), F32)] * 3,
        in_specs=[spec] * 4, out_specs=[spec] * 3, compiler_params=_params(1),
    )(w, g, m, v)


def _pack(arrs, width=D, row_mult=8):
    parts = []
    for a in arrs:
        flat = a.reshape(-1)
        pad = (-flat.shape[0]) % width
        parts.append(jnp.pad(flat, (0, pad)))
    flat = jnp.concatenate(parts)
    pad = (-flat.shape[0]) % (width * row_mult)
    return jnp.pad(flat, (0, pad)).reshape(-1, width)


def _unpack_gathered(buf, shapes, width=D):
    out, row = [], 0
    for shp in shapes:
        size = math.prod(shp)
        nrow = -(-size // width)
        out.append(buf[:, row:row + nrow].reshape(NDEV, -1)[:, :size].reshape((NDEV,) + tuple(shp)))
        row += nrow
    return out


def _block_diag(w):
    hn, dh, _ = w.shape
    eye = jnp.eye(hn, dtype=w.dtype)
    return (w[:, :, None, :] * eye[:, None, :, None]).reshape(hn * dh, hn * dh)


def _diag_blocks(wfull, hn):
    dh = wfull.shape[0] // hn
    keep = np.eye(hn, dtype=bool)[:, None, :, None]
    return jnp.where(keep, wfull.reshape(hn, dh, hn, dh), 0.0).sum(axis=2)


def _pad_rows(a, rows):
    return jnp.pad(a, ((0, rows - a.shape[0]), (0, 0)))


def kernel(x, c, w_ada, b_ada, w_in, lru_conv_w, lru_conv_b, lru_w_r, lru_b_r, lru_w_i, lru_b_i, lru_lambda, conv_w, conv_b, conv_norm_g, conv_norm_b, w_out, ln1_g, ln1_b, ffn_w_up, ffn_conv_w, ffn_conv_b, ffn_w_down, ln2_g, ln2_b, loss_target, m_w_ada, m_b_ada, m_w_in, m_lru_conv_w, m_lru_conv_b, m_lru_w_r, m_lru_b_r, m_lru_w_i, m_lru_b_i, m_lru_lambda, m_conv_w, m_conv_b, m_conv_norm_g, m_conv_norm_b, m_w_out, m_ln1_g, m_ln1_b, m_ffn_w_up, m_ffn_conv_w, m_ffn_conv_b, m_ffn_w_down, m_ln2_g, m_ln2_b, v_w_ada, v_b_ada, v_w_in, v_lru_conv_w, v_lru_conv_b, v_lru_w_r, v_lru_b_r, v_lru_w_i, v_lru_b_i, v_lru_lambda, v_conv_w, v_conv_b, v_conv_norm_g, v_conv_norm_b, v_w_out, v_ln1_g, v_ln1_b, v_ffn_w_up, v_ffn_conv_w, v_ffn_conv_b, v_ffn_w_down, v_ln2_g, v_ln2_b):
    weights = dict(w_ada=w_ada, b_ada=b_ada, w_in=w_in, lru_conv_w=lru_conv_w, lru_conv_b=lru_conv_b, lru_w_r=lru_w_r,
                   lru_b_r=lru_b_r, lru_w_i=lru_w_i, lru_b_i=lru_b_i, lru_lambda=lru_lambda, conv_w=conv_w, conv_b=conv_b,
                   conv_norm_g=conv_norm_g, conv_norm_b=conv_norm_b, w_out=w_out, ln1_g=ln1_g, ln1_b=ln1_b,
                   ffn_w_up=ffn_w_up, ffn_conv_w=ffn_conv_w, ffn_conv_b=ffn_conv_b, ffn_w_down=ffn_w_down, ln2_g=ln2_g,
                   ln2_b=ln2_b)
    mom_m = dict(w_ada=m_w_ada, b_ada=m_b_ada, w_in=m_w_in, lru_conv_w=m_lru_conv_w, lru_conv_b=m_lru_conv_b,
                 lru_w_r=m_lru_w_r, lru_b_r=m_lru_b_r, lru_w_i=m_lru_w_i, lru_b_i=m_lru_b_i, lru_lambda=m_lru_lambda,
                 conv_w=m_conv_w, conv_b=m_conv_b, conv_norm_g=m_conv_norm_g, conv_norm_b=m_conv_norm_b, w_out=m_w_out,
                 ln1_g=m_ln1_g, ln1_b=m_ln1_b, ffn_w_up=m_ffn_w_up, ffn_conv_w=m_ffn_conv_w, ffn_conv_b=m_ffn_conv_b,
                 ffn_w_down=m_ffn_w_down, ln2_g=m_ln2_g, ln2_b=m_ln2_b)
    mom_v = dict(w_ada=v_w_ada, b_ada=v_b_ada, w_in=v_w_in, lru_conv_w=v_lru_conv_w, lru_conv_b=v_lru_conv_b,
                 lru_w_r=v_lru_w_r, lru_b_r=v_lru_b_r, lru_w_i=v_lru_w_i, lru_b_i=v_lru_b_i, lru_lambda=v_lru_lambda,
                 conv_w=v_conv_w, conv_b=v_conv_b, conv_norm_g=v_conv_norm_g, conv_norm_b=v_conv_norm_b, w_out=v_w_out,
                 ln1_g=v_ln1_g, ln1_b=v_ln1_b, ffn_w_up=v_ffn_w_up, ffn_conv_w=v_ffn_conv_w, ffn_conv_b=v_ffn_conv_b,
                 ffn_w_down=v_ffn_w_down, ln2_g=v_ln2_g, ln2_b=v_ln2_b)
    names = list(weights)
    bl, seq, _ = x.shape
    ntok = bl * seq
    me = 4 * lax.axis_index("x") + 2 * lax.axis_index("y") + lax.axis_index("c")

    small_shapes = [(bl, D), (LRU_K, LW // NDEV), (CONV_K, CW // NDEV), (FFN_K, DFF // NDEV)]
    small = _pack([c, lru_conv_w[0], conv_w[0], ffn_conv_w[0]], width=128)
    n_small = small.shape[0]
    small_all, win_t, wout_b = _all_gather(
        [small, w_in[0].T.astype(MXU_DTYPE), w_out[0].astype(MXU_DTYPE)], "gather_small_and_mixer_weights")
    small_all = small_all.reshape(NDEV, n_small, 128)
    c_parts, k4_parts, k31_parts, k3_parts = _unpack_gathered(small_all, small_shapes, width=128)
    c_all = c_parts.reshape(NDEV * bl, D)
    lru_conv_w_f = k4_parts.transpose(1, 0, 2).reshape(LRU_K, LW)
    conv_w_f = k31_parts.transpose(1, 0, 2).reshape(CONV_K, CW)
    ffn_conv_w_f = k3_parts.transpose(1, 0, 2).reshape(FFN_K, DFF)

    ncol = w_ada.shape[2]
    b_ada_loc = lax.dynamic_slice(b_ada, (0, me * ncol), (1, ncol))
    mod_cols = _ada_fwd(c_all, w_ada[0], b_ada_loc)
    (mod_all,) = _all_gather([mod_cols], "gather_mod")
    mod_all = mod_all.reshape(NDEV, NDEV * bl, ncol)
    mod_mine = lax.dynamic_slice(mod_all, (0, me * bl, 0), (NDEV, bl, ncol))
    mods = mod_mine.transpose(1, 0, 2).reshape(bl, 6, D)
    mods = jnp.pad(mods, ((0, 0), (0, 2), (0, 0)))

    ffn_shards = [ffn_w_up[0].T.astype(MXU_DTYPE), ffn_w_down[0].astype(MXU_DTYPE)]

    wr_bd = _block_diag(lru_w_r[0]).astype(MXU_DTYPE)
    wi_bd = _block_diag(lru_w_i[0]).astype(MXU_DTYPE)
    mavg_np = np.kron(np.eye(CW // HEAD, dtype=np.float32), np.full((HEAD, HEAD), 1.0 / HEAD, np.float32))
    mavg = jnp.asarray(mavg_np, MXU_DTYPE)
    mavg2 = jnp.asarray(np.concatenate([mavg_np, mavg_np], axis=0), MXU_DTYPE)
    wri = jnp.concatenate([wr_bd, wi_bd], axis=1)
    va = jnp.concatenate([lru_conv_w_f, lru_conv_b, lru_b_r, lru_b_i, lru_lambda], axis=0)
    vb = _pad_rows(jnp.concatenate([conv_w_f, conv_b, conv_norm_g, conv_norm_b], axis=0), 40)
    ln1 = _pad_rows(jnp.concatenate([ln1_g, ln1_b], axis=0), 8)
    ln2 = _pad_rows(jnp.concatenate([ln2_g, ln2_b], axis=0), 8)
    f3 = _pad_rows(jnp.concatenate([ffn_conv_w_f, ffn_conv_b], axis=0), 8)

    proj, h, vc, y_b, mixed, x1, u1_b, wup_t, wdown_b = _mixer_fwd(
        x, mods, win_t, wout_b, wri, mavg2, va, vb, ln1, ffn_shards)
    hh, z_b, u2_b, y2, loss_part = _ffn_fwd(x1, mods, loss_target, wup_t, wdown_b, f3, ln2)

    dx1, dy2_b, dhh_b, dln2, df3, dmod_b = _ffn_bwd(x1, y2, loss_target, hh, mods, wup_t, wdown_b, f3, ln2)
    g_down = _weight_grad(z_b.reshape(ntok, DFF), dy2_b.reshape(ntok, D), FF_CHUNK, "wgrad_down", out_dtype=WIRE_DTYPE)
    g_up_t = _weight_grad(dhh_b.reshape(ntok, 2 * DFF), u2_b.reshape(ntok, D), FF_CHUNK, "wgrad_up",
                          out_dtype=WIRE_DTYPE)
    grad_x, dproj_b, dm_b, g_a, g_w31, g_wr, g_wi, dln1, dmod_a, s_up, s_down = _mixer_bwd(
        x, dx1, proj, h, vc, mixed, mods, win_t, wout_b, wri, mavg2, va, vb, ln1, [g_up_t, g_down])
    dproj_2d, u1_2d = dproj_b.reshape(ntok, 2 * LW + 2 * CW), u1_b.reshape(ntok, D)
    blk_in = (2 * LW + 2 * CW) // NDEV
    g_out_b = _weight_grad(y_b.reshape(ntok, D), dm_b.reshape(ntok, D), D, "wgrad_out", out_dtype=WIRE_DTYPE)
    g_in_c0, s_out = _weight_grad(dproj_2d, u1_2d, blk_in, "wgrad_in_c0", out_dtype=WIRE_DTYPE, exchange=[g_out_b],
                                  col_phase=0, tok_tile=2 * WG_TOK_TILE)
    g_in_c1, s_in_c0 = _weight_grad(dproj_2d, u1_2d, blk_in, "wgrad_in_c1", out_dtype=WIRE_DTYPE, exchange=[g_in_c0],
                                    exchange_half=0, col_phase=1, tok_tile=2 * WG_TOK_TILE)

    heads = LW // HEAD
    packed = _pack_small_grads(
        dmod_a.reshape(bl * 8, D), dmod_b.reshape(bl * 8, D), g_a, g_w31, dln1, dln2, df3,
        _diag_blocks(g_wr, heads).reshape(32, D), _diag_blocks(g_wi, heads).reshape(32, D), loss_part)
    n_rows = packed.shape[0]
    (s_in_c1,), (packed_all,) = _exchange_and_gather(
        [g_in_c1], [packed], "exchange_w_in_grads_gather_small_grads", half=1)
    packed_all = packed_all.reshape(NDEV, n_rows, D)
    full = _sum_small_grads(packed_all, bl)
    loss = jnp.sum(full.pop("loss")) * (0.5 / D)
    dmod_all = packed_all[:, 0:6 * bl].reshape(NDEV * bl, 6 * D)

    grads, delta, new_m, new_v = {}, {}, {}, {}
    for k, slots in (("w_out", s_out), ("ffn_w_down", s_down)):
        g_, d_, m_, v_ = _sum_adamw(slots, weights[k][0], mom_m[k][0], mom_v[k][0], "sum_adamw_" + k)
        grads[k], delta[k], new_m[k], new_v[k] = g_[None], d_[None], m_[None], v_[None]
    g_w_ada, g_b_ada = _ada_bwd(c_all, dmod_all, lax.dynamic_slice(dmod_all, (0, me * ncol), (NDEV * bl, ncol)))
    grads["w_ada"], grads["b_ada"] = g_w_ada[None], g_b_ada
    k = "w_in"
    g_, d_, m_, v_ = _sum_adamw_transposed(s_in_c0, s_in_c1, weights[k][0], mom_m[k][0], mom_v[k][0], "sum_adamw_" + k)
    grads[k], delta[k], new_m[k], new_v[k] = g_[None], d_[None], m_[None], v_[None]
    k = "ffn_w_up"
    g_, d_, m_, v_ = _sum_adamw(s_up, weights[k][0].T, mom_m[k][0].T, mom_v[k][0].T, "sum_adamw_" + k)
    grads[k], delta[k], new_m[k], new_v[k] = g_.T[None], d_.T[None], m_.T[None], v_.T[None]
    for k, gk in full.items():
        if k in ("lru_conv_w", "conv_w", "ffn_conv_w"):
            nloc = gk.shape[1] // NDEV
            gk = lax.dynamic_slice(gk, (0, me * nloc), (gk.shape[0], nloc))
        grads[k] = gk.reshape(weights[k].shape)

    big = ("w_ada", "w_in", "w_out", "ffn_w_up", "ffn_w_down")
    for k in ("w_ada",):
        d_, m_, v_ = _adamw(weights[k][0], grads[k][0], mom_m[k][0], mom_v[k][0], "adamw_" + k)
        delta[k], new_m[k], new_v[k] = d_[None], m_[None], v_[None]
    small_names = [k for k in names if k not in big]
    d_, m_, v_ = _adamw_many([weights[k] for k in small_names], [grads[k] for k in small_names],
                             [mom_m[k] for k in small_names], [mom_v[k] for k in small_names], "adamw_small")
    for k, dk, mk, vk in zip(small_names, d_, m_, v_):
        delta[k], new_m[k], new_v[k] = dk, mk, vk

    return (loss, grad_x, *[grads[k] for k in names], *[delta[k] for k in names], *[new_m[k] for k in names],
            *[new_v[k] for k in names])
```
